```python
import math
import functools
import jax
import jax.numpy as jnp
from jax import lax
import numpy as np

D_MODEL = 4096
BATCH = 4
SEQ = 2048
DEPTH = 2
DEC_BATCH = 8
DEC_SEQ = 4
PAST_LEN = 16384
PAGE_SIZE = 128

D_MIX = D_MODEL
W_A = D_MIX // 4
A_GROUP = 128
A_HEADS = W_A // A_GROUP
A_CHUNK = 128
W_B = D_MIX // 2
DN_HEAD = 128
DN_HEADS = W_B // DN_HEAD
CONV_K = 4
DN_CONV_DIM = 3 * W_B
DN_CHUNK = 64
W_C = D_MIX - W_A - W_B
C_HEAD = 128
C_HEADS = W_C // C_HEAD
C_KV_HEADS = 2
C_GROUPS = C_HEADS // C_KV_HEADS
C_KV = C_KV_HEADS * C_HEAD
IDX_HEADS = 16
IDX_DIM = 64
TOPK_MAX = 256
Q_BLOCK = 128
EPS = 1e-6

IN_SPLITS = (W_A, W_A, W_A,
             DN_CONV_DIM, W_B, DN_HEADS, DN_HEADS,
             W_C, C_KV, C_KV, W_C,
             IDX_HEADS * IDX_DIM, IDX_DIM, IDX_HEADS)
D_IN = sum(IN_SPLITS)

kernel_name = 'hybrid_gmlp_deltanet_dsa_step'


def _rmsnorm(x, g):
    xf = x.astype(jnp.float32)
    y = xf * lax.rsqrt(jnp.mean(xf * xf, axis=-1, keepdims=True) + EPS)
    return (y * g.astype(jnp.float32)).astype(x.dtype)


def _layernorm(x, g):
    xf = x.astype(jnp.float32)
    mu = jnp.mean(xf, axis=-1, keepdims=True)
    d = xf - mu
    y = d * lax.rsqrt(jnp.mean(d * d, axis=-1, keepdims=True) + EPS)
    return (y * g.astype(jnp.float32)).astype(x.dtype)


def _l2norm(x):
    xf = x.astype(jnp.float32)
    return (xf * lax.rsqrt(jnp.sum(xf * xf, axis=-1, keepdims=True) + EPS)).astype(x.dtype)


def _split_in(p):
    points = np.cumsum(np.array(IN_SPLITS))[:-1].tolist()
    return jnp.split(p, points, axis=-1)


def _gather_rows(x, idx):
    return jax.vmap(lambda xb, ib: xb[ib])(x, idx)


def _causal_conv(x, prev, w):
    L = x.shape[1]
    xp = jnp.concatenate([prev, x], axis=1)
    y = xp[:, 0:L] * w[0]
    for j in range(1, CONV_K):
        y = y + xp[:, j:j + L] * w[j]
    return jax.nn.silu(y), xp[:, -(CONV_K - 1):]


def _chunk_mlp(u, v, w_s, b_s):
    B, L, _ = u.shape
    C = min(A_CHUNK, L)
    n = L // C
    vr = v.reshape(B, n, C, A_HEADS, A_GROUP)
    wm = jnp.where(jnp.tril(jnp.ones((C, C), bool)), w_s[:, :C, :C], 0.0)
    mixed = jnp.einsum('hts,bnshd->bnthd', wm, vr) + b_s[:, :C].T[None, None, :, :, None]
    return u * mixed.reshape(B, L, W_A)


def _gated_delta(q, k, v, g, beta, s0):
    out_dtype = v.dtype
    B, L, H, DK = q.shape
    DV = v.shape[-1]
    C = math.gcd(L, DN_CHUNK)
    n = L // C
    f = jnp.float32

    def chunks(t):
        t = t.astype(f).reshape((B, n, C, H) + t.shape[3:])
        return jnp.moveaxis(t, (1, 3), (0, 2))

    qc = chunks(q) * DK ** -0.5
    kc, vc, gc, bc = chunks(k), chunks(v), chunks(g), chunks(beta)
    G = jnp.cumsum(gc, axis=-1)
    incl = jnp.tril(jnp.ones((C, C), bool))
    strict = jnp.tril(jnp.ones((C, C), bool), -1)
    diff = G[..., :, None] - G[..., None, :]
    decay = jnp.where(incl, jnp.exp(jnp.where(incl, diff, 0.0)), 0.0)
    kb = kc * bc[..., None]
    a = jnp.where(strict, jnp.einsum('nbhcd,nbhsd->nbhcs', kb, kc) * decay, 0.0)
    eye = jnp.broadcast_to(jnp.eye(C, dtype=f), a.shape)
    rhs = jnp.concatenate([vc * bc[..., None], kb * jnp.exp(G)[..., None]], axis=-1)
    sol = lax.linalg.triangular_solve(eye + a, rhs, left_side=True, lower=True, unit_diagonal=True)
    uc, wc = sol[..., :DV], sol[..., DV:]
    attn = jnp.where(incl, jnp.einsum('nbhcd,nbhsd->nbhcs', qc, kc) * decay, 0.0)
    qg = qc * jnp.exp(G)[..., None]
    kg = kc * jnp.exp(G[..., -1:] - G)[..., None]
    gl = jnp.exp(G[..., -1])

    def step(s, xs):
        qg_i, kg_i, u_i, w_i, attn_i, gl_i = xs
        v_new = u_i - jnp.einsum('bhcd,bhde->bhce', w_i, s)
        o = jnp.einsum('bhcd,bhde->bhce', qg_i, s) + jnp.einsum('bhcs,bhse->bhce', attn_i, v_new)
        s = s * gl_i[..., None, None] + jnp.einsum('bhcd,bhce->bhde', kg_i, v_new)
        return s, o

    s_fin, o = lax.scan(step, s0.astype(f), (qg, kg, uc, wc, attn, gl))
    o = jnp.swapaxes(jnp.moveaxis(o, 0, 1), 2, 3).reshape(B, L, H, DV)
    return o.astype(out_dtype), s_fin.astype(s0.dtype)


def _indexer_topk(q_idx, w_idx, q_pos, k_idx, top_k):
    n_keys = k_idx.shape[1]
    logits = jnp.einsum('bthd,bld->bthl', q_idx, k_idx).astype(jnp.float32) * IDX_DIM ** -0.5
    w = w_idx.astype(jnp.float32) * IDX_HEADS ** -0.5
    score = jnp.einsum('bth,bthl->btl', w, jax.nn.relu(logits))
    causal = jnp.arange(n_keys, dtype=jnp.int32)[None, :] <= q_pos[:, None]
    score = jnp.where(causal[None], score, -jnp.inf)
    _, idx = lax.top_k(score, top_k)
    valid = idx <= q_pos[None, :, None]
    return idx, valid


def _sparse_attend(q, k_sel, v_sel, valid):
    s = jnp.einsum('bthgd,btkhd->bthgk', q, k_sel).astype(jnp.float32) * C_HEAD ** -0.5
    s = jnp.where(valid[:, :, None, None, :], s, -jnp.inf)
    p = jax.nn.softmax(s, axis=-1)
    return jnp.einsum('bthgk,btkhd->bthgd', p.astype(v_sel.dtype), v_sel)


def _dsa_prompt(q, k, v, q_idx, k_idx, w_idx):
    B, L = q.shape[:2]
    top_k = min(TOPK_MAX, L // 4)
    qb = min(Q_BLOCK, L)
    nb = L // qb

    def blocks(t):
        return jnp.moveaxis(t.reshape((B, nb, qb) + t.shape[2:]), 1, 0)

    pos = jnp.arange(L, dtype=jnp.int32).reshape(nb, qb)

    def one_block(xs):
        q_b, qi_b, wi_b, pos_b = xs
        idx, valid = _indexer_topk(qi_b, wi_b, pos_b, k_idx, top_k)
        return _sparse_attend(q_b, _gather_rows(k, idx), _gather_rows(v, idx), valid)

    o = lax.map(one_block, (blocks(q), blocks(q_idx), blocks(w_idx), pos))
    return jnp.moveaxis(o, 0, 1).reshape(B, L, W_C)


def _dsa_sample(q, k, v, q_idx, k_idx, w_idx, cache_k, cache_v, cache_kidx, page_table):
    B, T = q.shape[:2]
    past = page_table.shape[1] * PAGE_SIZE
    top_k = min(TOPK_MAX, (past + T) // 4)
    kidx_past = cache_kidx[page_table].reshape(B, past, IDX_DIM)
    kidx_all = jnp.concatenate([kidx_past.astype(k_idx.dtype), k_idx], axis=1)
    pos = past + jnp.arange(T, dtype=jnp.int32)
    idx, valid = _indexer_topk(q_idx, w_idx, pos, kidx_all, top_k)
    is_past = (idx < past)[..., None, None]
    pc = jnp.minimum(idx, past - 1)
    phys = jax.vmap(lambda pt, p: pt[p])(page_table, pc // PAGE_SIZE)
    off = pc % PAGE_SIZE
    nidx = jnp.clip(idx - past, 0, T - 1)
    k_sel = jnp.where(is_past, cache_k[phys, off].astype(k.dtype), _gather_rows(k, nidx))
    v_sel = jnp.where(is_past, cache_v[phys, off].astype(v.dtype), _gather_rows(v, nidx))
    return _sparse_attend(q, k_sel, v_sel, valid).reshape(B, T, W_C)


def _layer(x, c, w_ada, b_ada, g_norm, w_in, a_vnorm, a_ws, a_bs, dn_conv_w, dn_a_log, dn_dt_bias,
           dn_onorm, w_out, conv_prev, s0, attend):
    B, L, _ = x.shape
    m = jax.nn.silu(c) @ w_ada + b_ada
    shift, scale, gate = jnp.split(m[:, None, :], 3, axis=-1)
    h = _rmsnorm(x, g_norm) * (1 + scale) + shift
    (u_a, v_a, z_a, qkv_b, z_b, a_b, b_b, q_c, k_c, v_c, z_c, qi_c, ki_c, wi_c) = _split_in(h @ w_in)
    u_a = jax.nn.gelu(u_a)
    v_a = _layernorm(jax.nn.gelu(v_a), a_vnorm)
    y_a = _chunk_mlp(u_a, v_a, a_ws, a_bs) * jax.nn.silu(z_a)
    qkv_conv, conv_new = _causal_conv(qkv_b, conv_prev, dn_conv_w)
    q_b, k_b, v_b = jnp.split(qkv_conv, 3, axis=-1)
    hs = (B, L, DN_HEADS, DN_HEAD)
    g = -jnp.exp(dn_a_log.astype(jnp.float32)) * jax.nn.softplus(a_b.astype(jnp.float32) + dn_dt_bias.astype(jnp.float32))
    beta = jax.nn.sigmoid(b_b.astype(jnp.float32))
    o_b, s_new = _gated_delta(_l2norm(q_b.reshape(hs)), _l2norm(k_b.reshape(hs)), v_b.reshape(hs), g, beta, s0)
    y_b = (_rmsnorm(o_b, dn_onorm) * jax.nn.silu(z_b.reshape(hs))).reshape(B, L, W_B)
    k_rows = k_c.reshape(B, L, C_KV_HEADS, C_HEAD)
    v_rows = v_c.reshape(B, L, C_KV_HEADS, C_HEAD)
    o_c = attend(q_c.reshape(B, L, C_KV_HEADS, C_GROUPS, C_HEAD), k_rows, v_rows,
                 qi_c.reshape(B, L, IDX_HEADS, IDX_DIM), ki_c, wi_c)
    y_c = o_c * jax.nn.silu(z_c)
    mix = jnp.concatenate([y_a, y_b, y_c], axis=-1)
    x = x + gate * (mix @ w_out)
    return x, v_a, conv_new, s_new, k_rows, v_rows, ki_c


def setup_inputs(seed: int = 0) -> dict:
    key = jax.random.key(seed)
    ks = jax.random.split(key, 24)
    f = jnp.float32
    n_pages = PAST_LEN // PAGE_SIZE
    n_used = DEC_BATCH * n_pages
    n_pool = n_used + max(1, n_used // 4)
    page_table = jax.random.permutation(ks[0], n_pool)[:n_used].reshape(DEC_BATCH, n_pages).astype(jnp.int32)

    def nrm(k, shape, s=1.0):
        return s * jax.random.normal(k, shape, f)

    dt = jnp.exp(jax.random.uniform(ks[1], (DEPTH, DN_HEADS), f, math.log(1e-3), math.log(1e-1)))
    return {
        'x_prompt': nrm(ks[2], (BATCH, SEQ, D_MODEL)),
        'x_sample': nrm(ks[3], (DEC_BATCH, DEC_SEQ, D_MODEL)),
        'cache_k': nrm(ks[4], (DEPTH, n_pool, PAGE_SIZE, C_KV_HEADS, C_HEAD)),
        'cache_v': nrm(ks[5], (DEPTH, n_pool, PAGE_SIZE, C_KV_HEADS, C_HEAD)),
        'cache_kidx': nrm(ks[6], (DEPTH, n_pool, PAGE_SIZE, IDX_DIM)),
        'state_dn': nrm(ks[7], (DEPTH, DEC_BATCH, DN_HEADS, DN_HEAD, DN_HEAD), DN_HEAD ** -0.5),
        'state_conv': nrm(ks[8], (DEPTH, DEC_BATCH, CONV_K - 1, DN_CONV_DIM)),
        'page_table': page_table,
        'c_prompt': nrm(ks[9], (BATCH, D_MODEL)),
        'c_sample': nrm(ks[10], (DEC_BATCH, D_MODEL)),
        'w_ada': nrm(ks[11], (DEPTH, D_MODEL, 3 * D_MODEL), 0.5 * D_MODEL ** -0.5),
        'b_ada': nrm(ks[12], (DEPTH, 3 * D_MODEL), 0.02),
        'g_norm': 1.0 + nrm(ks[13], (DEPTH, D_MODEL), 0.02),
        'w_in': nrm(ks[14], (DEPTH, D_MODEL, D_IN), D_MODEL ** -0.5),
        'a_vnorm': 1.0 + nrm(ks[15], (DEPTH, W_A), 0.02),
        'a_ws': nrm(ks[16], (DEPTH, A_HEADS, A_CHUNK, A_CHUNK), 0.5 * A_CHUNK ** -0.5),
        'a_bs': 1.0 + nrm(ks[17], (DEPTH, A_HEADS, A_CHUNK), 0.02),
        'dn_conv_w': nrm(ks[18], (DEPTH, CONV_K, DN_CONV_DIM), CONV_K ** -0.5),
        'dn_a_log': jnp.log(jax.random.uniform(ks[19], (DEPTH, DN_HEADS), f, 1.0, 16.0)),
        'dn_dt_bias': dt + jnp.log(-jnp.expm1(-dt)),
        'dn_onorm': 1.0 + nrm(ks[20], (DEPTH, DN_HEAD), 0.02),
        'w_out': nrm(ks[21], (DEPTH, D_MIX, D_MODEL), D_MIX ** -0.5),
        'g_final': 1.0 + nrm(ks[22], (D_MODEL,), 0.02),
    }


def reference(x_prompt, x_sample, cache_k, cache_v, cache_kidx, state_dn, state_conv, page_table,
              c_prompt, c_sample, w_ada, b_ada, g_norm, w_in, a_vnorm, a_ws, a_bs, dn_conv_w,
              dn_a_log, dn_dt_bias, dn_onorm, w_out, g_final):
    b_p = x_prompt.shape[0]
    xp, xs = x_prompt, x_sample
    pk, pv, pki, pdn, pconv = [], [], [], [], []
    sk, sv, ski, sdn, sconv, samlp = [], [], [], [], [], []
    for l in range(DEPTH):
        lw = (w_ada[l], b_ada[l], g_norm[l], w_in[l], a_vnorm[l], a_ws[l], a_bs[l], dn_conv_w[l],
              dn_a_log[l], dn_dt_bias[l], dn_onorm[l], w_out[l])
        conv0 = jnp.zeros((b_p, CONV_K - 1, DN_CONV_DIM), x_prompt.dtype)
        s0 = jnp.zeros((b_p, DN_HEADS, DN_HEAD, DN_HEAD), x_prompt.dtype)
        xp, _, conv_p, s_p, k_p, v_p, ki_p = _layer(xp, c_prompt, *lw, conv0, s0, _dsa_prompt)
        attend_s = functools.partial(_dsa_sample, cache_k=cache_k[l], cache_v=cache_v[l],
                                     cache_kidx=cache_kidx[l], page_table=page_table)
        xs, va_s, conv_s, s_s, k_s, v_s, ki_s = _layer(xs, c_sample, *lw, state_conv[l], state_dn[l], attend_s)
        pk.append(k_p); pv.append(v_p); pki.append(ki_p); pdn.append(s_p); pconv.append(conv_p)
        sk.append(k_s); sv.append(v_s); ski.append(ki_s); sdn.append(s_s); sconv.append(conv_s); samlp.append(va_s)
    y_prompt = _rmsnorm(xp, g_final)
    y_sample = _rmsnorm(xs, g_final)
    return (y_prompt, y_sample,
            jnp.stack(pk), jnp.stack(pv), jnp.stack(pki), jnp.stack(pdn), jnp.stack(pconv),
            jnp.stack(sk), jnp.stack(sv), jnp.stack(ski), jnp.stack(sdn), jnp.stack(sconv), jnp.stack(samlp))
```

```python
import functools

import jax
import jax.numpy as jnp
from jax import lax
from jax.experimental import pallas as pl
from jax.experimental.pallas import tpu as pltpu

F32 = jnp.float32
BF16 = jnp.bfloat16
I32 = jnp.int32
EPS = 1e-6
INT_MIN = -(2 ** 31)
NEG_INF = float("-inf")

D_MODEL = 4096
W_A = D_MODEL // 4
A_GROUP = 128
A_HEADS = W_A // A_GROUP
A_CHUNK = 128
W_B = D_MODEL // 2
DN_HEAD = 128
DN_HEADS = W_B // DN_HEAD
CONV_K = 4
DN_CONV_DIM = 3 * W_B
DN_CHUNK = 64
W_C = D_MODEL - W_A - W_B
C_HEAD = 128
C_HEADS = W_C // C_HEAD
C_KV_HEADS = 2
C_GROUPS = C_HEADS // C_KV_HEADS
C_KV = C_KV_HEADS * C_HEAD
IDX_HEADS = 16
IDX_DIM = 64
TOPK_MAX = 256
Q_BLOCK = 128

_SRC_SPLITS = (W_A, W_A, W_A, DN_CONV_DIM, W_B, DN_HEADS, DN_HEADS,
               W_C, C_KV, C_KV, W_C, IDX_HEADS * IDX_DIM, IDX_DIM, IDX_HEADS)
_SRC_NAMES = ("u", "v", "za", "qkv", "zb", "a", "b", "qc", "kc", "vc", "zc", "qi", "ki", "wi")
_SRC_OFF = {}
_o = 0
for _n, _w in zip(_SRC_NAMES, _SRC_SPLITS):
    _SRC_OFF[_n] = (_o, _w)
    _o += _w
D_IN = _o

_DST_ORDER = ("u", "v", "za", "qkv", "zb", "qc", "zc", "qi", "kc", "vc", "ki", "a", "b", "wi")
OFF = {}
_o = 0
for _n in _DST_ORDER:
    OFF[_n] = _o
    _o += _SRC_OFF[_n][1]
NP = ((_o + 127) // 128) * 128
OFF_TAIL = OFF["ki"]
TAIL_A = OFF["a"] - OFF_TAIL
TAIL_B = OFF["b"] - OFF_TAIL
TAIL_WI = OFF["wi"] - OFF_TAIL

DELTA_HEADS_PER_STEP = 4
DSA_KEY_TILE = 512


def _cp(sem, vmem_mb=48):
    return pltpu.CompilerParams(dimension_semantics=sem, vmem_limit_bytes=vmem_mb * 1024 * 1024)


def _silu(x):
    return x * (1.0 / (1.0 + jnp.exp(-x)))


def _gelu(x):
    return 0.5 * x * (1.0 + jnp.tanh(0.7978845608028654 * (x + 0.044715 * (x * x * x))))


def _dot(a, b):
    return jnp.dot(a, b, preferred_element_type=F32)


def _dot_nt(a, b):
    return lax.dot_general(a, b, (((1,), (1,)), ((), ())), preferred_element_type=F32)


def _dot_tn(a, b):
    return lax.dot_general(a, b, (((0,), (0,)), ((), ())), preferred_element_type=F32)


def _split_bf16(a):
    hi = a.astype(BF16)
    lo = (a - hi.astype(F32)).astype(BF16)
    return hi, lo


def _dot3s(a, b):
    ah, al = a
    bh, bl = b
    return _dot(ah, bh) + (_dot(al, bh) + _dot(ah, bl))


def _ada_kernel(c_ref, w_ref, b_ref, o_ref):
    s = _silu(c_ref[...]).astype(BF16)
    o_ref[...] = _dot(s, w_ref[...].astype(BF16)) + b_ref[...]


def _ada(c_rows, w_ada, b_ada):
    depth, d, n = w_ada.shape
    r = c_rows.shape[0]
    tn = 512
    return pl.pallas_call(
        _ada_kernel,
        grid=(depth, n // tn),
        in_specs=[pl.BlockSpec((r, d), lambda l, j: (0, 0)),
                  pl.BlockSpec((None, d, tn), lambda l, j: (l, 0, j)),
                  pl.BlockSpec((None, 1, tn), lambda l, j: (l, 0, j))],
        out_specs=pl.BlockSpec((None, r, tn), lambda l, j: (l, 0, j)),
        out_shape=jax.ShapeDtypeStruct((depth, r, n), F32),
        compiler_params=_cp(("arbitrary", "arbitrary"), 40),
        name="ada",
    )(c_rows, w_ada, b_ada.reshape(depth, 1, n))


def _inproj_kernel(x_ref, g_ref, sc_ref, sh_ref, w_ref, o_ref, h_scr, *, rc):
    @pl.when(pl.program_id(1) == 0)
    def _():
        tm = x_ref.shape[0]
        per_row = sc_ref.shape[0] != 1

        def body(c, carry):
            r = pl.ds(pl.multiple_of(c * rc, rc), rc)
            x = x_ref[r, :]
            y = x * lax.rsqrt(jnp.mean(x * x, axis=-1, keepdims=True) + EPS)
            sc = sc_ref[r, :] if per_row else sc_ref[...]
            sh = sh_ref[r, :] if per_row else sh_ref[...]
            h_scr[r, :] = ((y * g_ref[...]) * (1.0 + sc) + sh).astype(BF16)
            return carry

        lax.fori_loop(0, tm // rc, body, 0)

    o_ref[...] = _dot(h_scr[...], w_ref[...])


def _inproj(x2, g, sc, sh, w_bf16, *, rows_per_batch, mod_row0):
    m, d = x2.shape
    n = w_bf16.shape[1]
    tn = 1152
    assert n % tn == 0
    if sc.ndim == 3:
        tm = 512
        assert rows_per_batch % tm == 0
        sc_spec = pl.BlockSpec((None, 1, d), lambda i, j: ((i * tm) // rows_per_batch + mod_row0, 0, 1))
        sh_spec = pl.BlockSpec((None, 1, d), lambda i, j: ((i * tm) // rows_per_batch + mod_row0, 0, 0))
        rc = 64
    else:
        tm = m
        sc_spec = pl.BlockSpec((tm, d), lambda i, j: (i, 0))
        sh_spec = pl.BlockSpec((tm, d), lambda i, j: (i, 0))
        rc = tm
    assert m % tm == 0
    return pl.pallas_call(
        functools.partial(_inproj_kernel, rc=rc),
        grid=(m // tm, n // tn),
        in_specs=[pl.BlockSpec((tm, d), lambda i, j: (i, 0)),
                  pl.BlockSpec((1, d), lambda i, j: (0, 0)),
                  sc_spec, sh_spec,
                  pl.BlockSpec((d, tn), lambda i, j: (0, j))],
        out_specs=pl.BlockSpec((tm, tn), lambda i, j: (i, j)),
        out_shape=jax.ShapeDtypeStruct((m, n), F32),
        scratch_shapes=[pltpu.VMEM((tm, d), BF16)],
        compiler_params=_cp(("arbitrary", "arbitrary"), 56),
        name="inproj",
    )(x2, g, sc, sh, w_bf16)


def _mixa_kernel(u_ref, v_ref, z_ref, vn_ref, ws_ref, bst_ref, y_ref, *rest, c, emit_va):
    u = _gelu(u_ref[...])
    v = _gelu(v_ref[...])
    mu = jnp.mean(v, axis=-1, keepdims=True)
    dv = v - mu
    va = dv * lax.rsqrt(jnp.mean(dv * dv, axis=-1, keepdims=True) + EPS) * vn_ref[...]
    if emit_va:
        rest[0][...] = va
    z = _silu(z_ref[...])
    row = lax.broadcasted_iota(I32, (c, c), 0)
    col = lax.broadcasted_iota(I32, (c, c), 1)
    tril = col <= row
    for h in range(A_HEADS):
        cols = slice(h * A_GROUP, (h + 1) * A_GROUP)
        wm = jnp.where(tril, ws_ref[h], 0.0)
        vh = va[:, cols]
        if c >= 128:
            mixed = _dot(wm.astype(BF16), vh.astype(BF16))
        else:
            mixed = wm[:, 0:1] * vh[0:1, :]
            for s in range(1, c):
                mixed = mixed + wm[:, s:s + 1] * vh[s:s + 1, :]
        mixed = mixed + bst_ref[:, h:h + 1]
        y_ref[:, cols] = (u[:, cols] * mixed * z[:, cols]).astype(y_ref.dtype)


def _mixa(p3, a_vnorm, a_ws, a_bs, *, emit_va):
    b, l, _ = p3.shape
    c = min(A_CHUNK, l)
    n = l // c
    ws = a_ws[:, :c, :c]
    bst = a_bs[:, :c].T
    wblk = W_A
    outs = [jax.ShapeDtypeStruct((b, l, W_A), BF16)]
    out_specs = [pl.BlockSpec((None, c, W_A), lambda i, j: (i, j, 0))]
    if emit_va:
        outs.append(jax.ShapeDtypeStruct((b, l, W_A), F32))
        out_specs.append(pl.BlockSpec((None, c, W_A), lambda i, j: (i, j, 0)))
    res = pl.pallas_call(
        functools.partial(_mixa_kernel, c=c, emit_va=emit_va),
        grid=(b, n),
        in_specs=[pl.BlockSpec((None, c, wblk), lambda i, j: (i, j, OFF["u"] // wblk)),
                  pl.BlockSpec((None, c, wblk), lambda i, j: (i, j, OFF["v"] // wblk)),
                  pl.BlockSpec((None, c, wblk), lambda i, j: (i, j, OFF["za"] // wblk)),
                  pl.BlockSpec((1, W_A), lambda i, j: (0, 0)),
                  pl.BlockSpec((A_HEADS, c, c), lambda i, j: (0, 0, 0)),
                  pl.BlockSpec((c, A_HEADS), lambda i, j: (0, 0))],
        out_specs=out_specs,
        out_shape=outs,
        compiler_params=_cp(("arbitrary", "arbitrary"), 32),
        name="mixa",
    )(p3, p3, p3, a_vnorm.reshape(1, W_A), ws, bst)
    return res


def _delta_kernel(alog_ref, dtb_ref,
                  q_ref, k_ref, v_ref, z_ref, ab_ref,
                  cwq_ref, cwk_ref, cwv_ref, cpq_ref, cpk_ref, cpv_ref,
                  s0_ref, on_ref,
                  y_ref, sout_ref,
                  xbuf, s_scr, *, t, c, hp, valid_len):
    hg = pl.program_id(1)
    n = pl.program_id(2)
    nlast = pl.num_programs(2) - 1
    dh = DN_HEAD

    @pl.when(n == 0)
    def _():
        s_scr[...] = s0_ref[...]
        for j in range(hp):
            lanes = slice(j * dh, (j + 1) * dh)
            xbuf[3 * j + 0, 5:8, :] = cpq_ref[:, lanes]
            xbuf[3 * j + 1, 5:8, :] = cpk_ref[:, lanes]
            xbuf[3 * j + 2, 5:8, :] = cpv_ref[:, lanes]

    row = lax.broadcasted_iota(I32, (t, t), 0)
    col = lax.broadcasted_iota(I32, (t, t), 1)
    shift = c.bit_length() - 1
    same = (row >> shift) == (col >> shift)
    eye = row == col
    incl = same & (col <= row)
    strict = same & (col < row)
    incl_t = same & (row <= col)
    blk8 = (row >> 3) == (col >> 3)
    off_masks = []
    bs = 8
    while bs < c:
        sh_b = bs.bit_length() - 1
        inner = (row >> sh_b) == (col >> sh_b)
        outer = (row >> (sh_b + 1)) == (col >> (sh_b + 1))
        off_masks.append(outer & jnp.logical_not(inner))
        bs *= 2
    eye_f = jnp.where(eye, 1.0, 0.0)
    if valid_len < t:
        lane_valid = lax.broadcasted_iota(I32, (1, t), 1) < valid_len
        sub_valid = lax.broadcasted_iota(I32, (t, 1), 0) < valid_len

    def to_col(r):
        return jnp.sum(jnp.where(eye, r, 0.0), axis=1, keepdims=True)

    def conv(idx, x, w_ref, lanes):
        xbuf[idx, 8:8 + t, :] = x
        y = xbuf[idx, pl.ds(5, t), :] * w_ref[0:1, lanes]
        for jj in range(1, CONV_K):
            y = y + xbuf[idx, pl.ds(5 + jj, t), :] * w_ref[jj:jj + 1, lanes]
        xbuf[idx, 5:8, :] = x[t - 3:t, :]
        return _silu(y)

    heads = range(hp)
    lanes_of = [slice(j * dh, (j + 1) * dh) for j in heads]

    def per_head(f, *lists):
        return [f(*vals) for vals in zip(*lists)]

    q = [conv(3 * j + 0, q_ref[:, lanes_of[j]], cwq_ref, lanes_of[j]) for j in heads]
    k = [conv(3 * j + 1, k_ref[:, lanes_of[j]], cwk_ref, lanes_of[j]) for j in heads]
    v = [conv(3 * j + 2, v_ref[:, lanes_of[j]], cwv_ref, lanes_of[j]) for j in heads]
    qc = per_head(lambda a: a * lax.rsqrt(jnp.sum(a * a, axis=-1, keepdims=True) + EPS) * (dh ** -0.5), q)
    kc = per_head(lambda a: a * lax.rsqrt(jnp.sum(a * a, axis=-1, keepdims=True) + EPS), k)

    def gates(j):
        h = hg * hp + j
        a_row = ab_ref[pl.ds(h, 1), :]
        b_row = ab_ref[pl.ds(DN_HEADS + h, 1), :]
        xa = a_row + dtb_ref[h]
        softplus = jnp.maximum(xa, 0.0) + jnp.log(1.0 + jnp.exp(-jnp.abs(xa)))
        a_coef = jnp.exp(jnp.zeros((1, 1), F32) + alog_ref[h])
        g_row = -a_coef * softplus
        beta_row = 1.0 / (1.0 + jnp.exp(-b_row))
        if valid_len < t:
            g_row = jnp.where(lane_valid, g_row, 0.0)
            beta_row = jnp.where(lane_valid, beta_row, 0.0)
        return g_row, beta_row

    g_row, beta_row = zip(*[gates(j) for j in heads])
    if valid_len < t:
        kc = per_head(lambda a: jnp.where(sub_valid, a, 0.0), kc)
        v = per_head(lambda a: jnp.where(sub_valid, a, 0.0), v)

    g_col = per_head(to_col, g_row)
    beta_col = per_head(to_col, beta_row)
    gc_col = per_head(lambda r: jnp.sum(jnp.where(incl, r, 0.0), axis=1, keepdims=True), g_row)
    glast_col = per_head(lambda r: jnp.sum(jnp.where(same, r, 0.0), axis=1, keepdims=True), g_row)
    gc_row = per_head(lambda cl: jnp.sum(jnp.where(incl_t, cl, 0.0), axis=0, keepdims=True), g_col)
    decay = per_head(lambda gc, gr: jnp.where(incl, jnp.exp(jnp.where(incl, gc - gr, 0.0)), 0.0), gc_col, gc_row)
    kb = per_head(lambda a, b: a * b, kc, beta_col)
    kc16 = per_head(lambda a: a.astype(BF16), kc)
    a_mat = per_head(lambda a, b, dc: jnp.where(strict, _dot_nt(a.astype(BF16), b) * dc, 0.0), kb, kc16, decay)
    attn = per_head(lambda a, b, dc: (_dot_nt(a.astype(BF16), b) * dc).astype(BF16), qc, kc16, decay)
    eg = per_head(jnp.exp, gc_col)
    rhs = per_head(lambda vv, bc, kk, e: _split_bf16(jnp.concatenate([vv * bc, kk * e], axis=1)),
                   v, beta_col, kb, eg)

    n0f = per_head(lambda a: jnp.where(blk8, -a, 0.0), a_mat)
    n0 = per_head(_split_bf16, n0f)
    n2 = per_head(lambda a: _split_bf16(_dot3s(a, a)), n0)
    n4 = per_head(lambda a: _split_bf16(_dot3s(a, a)), n2)
    x = per_head(lambda a: eye_f + a, n0f)
    x = per_head(lambda xx, nn: xx + _dot3s(_split_bf16(xx), nn), x, n2)
    x = per_head(lambda xx, nn: xx + _dot3s(_split_bf16(xx), nn), x, n4)
    for om in off_masks:
        xs = per_head(_split_bf16, x)
        xa_off = per_head(lambda s_, a: _split_bf16(_dot3s(s_, _split_bf16(jnp.where(om, a, 0.0)))), xs, a_mat)
        x = per_head(lambda xx, xo, s_: xx - _dot3s(xo, s_), x, xa_off, xs)
    sol16 = per_head(lambda xx, r: _dot3s(_split_bf16(xx), r).astype(BF16), x, rhs)
    auw = per_head(_dot, attn, sol16)
    qw = per_head(lambda a, e, m_: (a * e - m_[:, dh:]).astype(BF16), qc, eg, auw)
    kg16 = per_head(lambda a, gl_, gc: (a * jnp.exp(gl_ - gc)).astype(BF16), kc, glast_col, gc_col)

    s = [s_scr[j] for j in heads]
    outs = [[] for _ in heads]
    for i in range(t // c):
        rows = slice(i * c, (i + 1) * c)
        s16 = per_head(lambda a: a.astype(BF16), s)
        kuw = per_head(lambda a, b: _dot_tn(a[rows], b[rows]), kg16, sol16)
        for j in heads:
            outs[j].append(_dot(qw[j][rows], s16[j]) + auw[j][rows, :dh])
        gl = per_head(lambda a: jnp.exp(a[i * c:i * c + 1, :]), glast_col)
        s = per_head(lambda g_, s_, m_, b16: g_ * s_ + m_[:, :dh] - _dot(m_[:, dh:].astype(BF16), b16),
                     gl, s, kuw, s16)
    for j in heads:
        s_scr[j] = s[j]
        o = outs[j][0] if len(outs[j]) == 1 else jnp.concatenate(outs[j], axis=0)
        on = o * lax.rsqrt(jnp.mean(o * o, axis=-1, keepdims=True) + EPS) * on_ref[...]
        y_ref[:, lanes_of[j]] = (on * _silu(z_ref[:, lanes_of[j]])).astype(y_ref.dtype)

    @pl.when(n == nlast)
    def _():
        sout_ref[...] = s_scr[...]


def _delta(p3, abt, conv_w, conv_prev, s0, a_log, dt_bias, onorm, *, t, valid_len):
    b, l, _ = p3.shape
    assert l % t == 0
    c = min(DN_CHUNK, t)
    hp = DELTA_HEADS_PER_STEP
    nh = DN_HEADS
    ng = nh // hp
    w = 128 * hp
    cb_q = OFF["qkv"] // w
    cb_z = OFF["zb"] // w
    assert OFF["qkv"] % w == 0 and OFF["zb"] % w == 0 and W_B % w == 0

    def pspec(cb0):
        return pl.BlockSpec((None, t, w), lambda i, h, n, a, d: (i, n, cb0 + h))

    def cwspec(sidx):
        return pl.BlockSpec((CONV_K, w), lambda i, h, n, a, d: (0, sidx * ng + h))

    def cpspec(sidx):
        return pl.BlockSpec((None, CONV_K - 1, w), lambda i, h, n, a, d: (i, 0, sidx * ng + h))

    grid_spec = pltpu.PrefetchScalarGridSpec(
        num_scalar_prefetch=2,
        grid=(b, ng, l // t),
        in_specs=[pspec(cb_q), pspec(cb_q + ng), pspec(cb_q + 2 * ng), pspec(cb_z),
                  pl.BlockSpec((None, 2 * nh, t), lambda i, h, n, a, d: (i, 0, n)),
                  cwspec(0), cwspec(1), cwspec(2), cpspec(0), cpspec(1), cpspec(2),
                  pl.BlockSpec((None, hp, DN_HEAD, DN_HEAD), lambda i, h, n, a, d: (i, h, 0, 0)),
                  pl.BlockSpec((1, DN_HEAD), lambda i, h, n, a, d: (0, 0))],
        out_specs=[pl.BlockSpec((None, t, w), lambda i, h, n, a, d: (i, n, h)),
                   pl.BlockSpec((None, hp, DN_HEAD, DN_HEAD), lambda i, h, n, a, d: (i, h, 0, 0))],
        scratch_shapes=[pltpu.VMEM((3 * hp, t + 8, 128), F32), pltpu.VMEM((hp, DN_HEAD, DN_HEAD), F32)],
    )
    return pl.pallas_call(
        functools.partial(_delta_kernel, t=t, c=c, hp=hp, valid_len=valid_len),
        grid_spec=grid_spec,
        out_shape=[jax.ShapeDtypeStruct((b, l, W_B), BF16),
                   jax.ShapeDtypeStruct((b, nh, DN_HEAD, DN_HEAD), F32)],
        compiler_params=_cp(("arbitrary", "arbitrary", "arbitrary"), 40),
        name="delta",
    )(a_log, dt_bias, p3, p3, p3, p3, abt, conv_w, conv_w, conv_w,
      conv_prev, conv_prev, conv_prev, s0, onorm.reshape(1, DN_HEAD))


def _ordered_bits_to_float(u):
    key = u ^ jnp.int32(INT_MIN)
    bits = jnp.where(key < 0, key ^ jnp.int32(0x7FFFFFFF), key)
    return pltpu.bitcast(bits, F32)


def _topk_threshold(count_ge, rows, topk):
    def bit_body(i, u):
        bit = jnp.left_shift(jnp.int32(1), 31 - i)
        cand_u = u | bit
        cnt = count_ge(_ordered_bits_to_float(cand_u))
        return jnp.where(cnt >= float(topk), cand_u, u)

    u = lax.fori_loop(0, 32, bit_body, jnp.zeros((rows, 1), I32))
    return _ordered_bits_to_float(u)


def _softmax_update(m_ref, l_ref, acc_ref, idx, s, v_tiles):
    m_old = m_ref[idx]
    m_new = jnp.maximum(m_old, jnp.max(s, axis=1, keepdims=True))
    m_safe = jnp.where(m_new == NEG_INF, 0.0, m_new)
    alpha = jnp.exp(m_old - m_safe)
    p = jnp.exp(s - m_safe)
    l_ref[idx] = alpha * l_ref[idx] + jnp.sum(p, axis=1, keepdims=True)
    acc = alpha * acc_ref[idx]
    p16 = p.astype(BF16)
    for i, vt in enumerate(v_tiles):
        acc = acc + _dot(p16[:, i * 128:(i + 1) * 128], vt)
    acc_ref[idx] = acc
    m_ref[idx] = m_new


def _dsa_prompt_kernel(qi_ref, tq_ref, tall_ref, qc_ref, k_ref, v_ref, z_ref, y_ref,
                       sc_scr, m_scr, l_scr, acc_scr, *, l, topk, kt):
    qb = pl.program_id(1)
    nq = Q_BLOCK
    n_tiles = (qb * nq + nq + kt - 1) // kt
    w_eff = tq_ref[:, TAIL_WI:TAIL_WI + IDX_HEADS] * ((IDX_HEADS ** -0.5) * (IDX_DIM ** -0.5))
    qi16 = qi_ref[...].astype(BF16)
    qrow = qb * nq + lax.broadcasted_iota(I32, (nq, kt), 0)
    kcol = lax.broadcasted_iota(I32, (nq, kt), 1)
    lane = lax.broadcasted_iota(I32, (kt, 128), 1)

    def score_tile(ti, carry):
        k0 = pl.multiple_of(ti * kt, kt)
        ki_lo = jnp.where(lane < IDX_DIM, tall_ref[pl.ds(k0, kt), :], 0.0)
        ki_hi = pltpu.roll(ki_lo, IDX_DIM, 1).astype(BF16)
        ki_lo = ki_lo.astype(BF16)
        score = jnp.zeros((nq, kt), F32)
        for j in range(IDX_HEADS // 2):
            slab = qi16[:, j * 128:(j + 1) * 128]
            score = score + w_eff[:, 2 * j:2 * j + 1] * jnp.maximum(_dot_nt(slab, ki_lo), 0.0)
            score = score + w_eff[:, 2 * j + 1:2 * j + 2] * jnp.maximum(_dot_nt(slab, ki_hi), 0.0)
        sc_scr[:, pl.ds(k0, kt)] = jnp.where(kcol + k0 <= qrow, score, NEG_INF)
        return carry

    lax.fori_loop(0, n_tiles, score_tile, 0)

    def count_ge(cand):
        def tile_body(ti, acc):
            k0 = pl.multiple_of(ti * kt, kt)
            for i in range(kt // 128):
                lanes = pl.ds(pl.multiple_of(k0 + i * 128, 128), 128)
                acc = acc + jnp.where(sc_scr[:, lanes] >= cand, 1.0, 0.0)
            return acc
        acc = lax.fori_loop(0, n_tiles, tile_body, jnp.zeros((nq, 128), F32))
        return jnp.sum(acc, axis=1, keepdims=True)

    thr = _topk_threshold(count_ge, nq, topk)
    keep_all = (qb * nq + lax.broadcasted_iota(I32, (nq, 1), 0)) < topk

    m_scr[...] = jnp.full(m_scr.shape, NEG_INF, F32)
    l_scr[...] = jnp.zeros(l_scr.shape, F32)
    acc_scr[...] = jnp.zeros(acc_scr.shape, F32)
    q16 = (qc_ref[...] * (C_HEAD ** -0.5)).astype(BF16)

    def attend_tile(ti, carry):
        k0 = pl.multiple_of(ti * kt, kt)
        sc = sc_scr[:, pl.ds(k0, kt)]
        valid = (kcol + k0 <= qrow) & (keep_all | (sc >= thr))
        bias = jnp.where(valid, 0.0, NEG_INF)
        for hk in range(C_KV_HEADS):
            kcols = slice(hk * C_HEAD, (hk + 1) * C_HEAD)
            k16 = k_ref[pl.ds(k0, kt), kcols].astype(BF16)
            v16 = v_ref[pl.ds(k0, kt), kcols].astype(BF16)
            v_tiles = [v16[i * 128:(i + 1) * 128, :] for i in range(kt // 128)]
            for g in range(C_GROUPS):
                hh = hk * C_GROUPS + g
                s = _dot_nt(q16[:, hh * C_HEAD:(hh + 1) * C_HEAD], k16) + bias
                _softmax_update(m_scr, l_scr, acc_scr, hh, s, v_tiles)
        return carry

    lax.fori_loop(0, n_tiles, attend_tile, 0)
    for hh in range(C_HEADS):
        cols = slice(hh * C_HEAD, (hh + 1) * C_HEAD)
        y_ref[:, cols] = ((acc_scr[hh] / l_scr[hh]) * _silu(z_ref[:, cols])).astype(y_ref.dtype)


def _dsa_prompt(p3):
    b, l, _ = p3.shape
    assert l % Q_BLOCK == 0
    topk = min(TOPK_MAX, l // 4)
    nq = Q_BLOCK
    kt = DSA_KEY_TILE if l % DSA_KEY_TILE == 0 else 128
    return pl.pallas_call(
        functools.partial(_dsa_prompt_kernel, l=l, topk=topk, kt=kt),
        grid=(b, l // nq),
        in_specs=[pl.BlockSpec((None, nq, W_C), lambda i, j: (i, j, OFF["qi"] // W_C)),
                  pl.BlockSpec((None, nq, 128), lambda i, j: (i, j, OFF_TAIL // 128)),
                  pl.BlockSpec((None, l, 128), lambda i, j: (i, 0, OFF_TAIL // 128)),
                  pl.BlockSpec((None, nq, W_C), lambda i, j: (i, j, OFF["qc"] // W_C)),
                  pl.BlockSpec((None, l, C_KV), lambda i, j: (i, 0, OFF["kc"] // C_KV)),
                  pl.BlockSpec((None, l, C_KV), lambda i, j: (i, 0, OFF["vc"] // C_KV)),
                  pl.BlockSpec((None, nq, W_C), lambda i, j: (i, j, OFF["zc"] // W_C))],
        out_specs=pl.BlockSpec((None, nq, W_C), lambda i, j: (i, j, 0)),
        out_shape=jax.ShapeDtypeStruct((b, l, W_C), BF16),
        scratch_shapes=[pltpu.VMEM((nq, l), F32),
                        pltpu.VMEM((C_HEADS, nq, 1), F32), pltpu.VMEM((C_HEADS, nq, 1), F32),
                        pltpu.VMEM((C_HEADS, nq, C_HEAD), F32)],
        compiler_params=_cp(("arbitrary", "arbitrary"), 48),
        name="dsa_prompt",
    )(p3, p3, p3, p3, p3, p3, p3)


def _idx_scores(logits, wcol, t):
    r = jnp.maximum(logits, 0.0) * wcol
    return jnp.sum(r.reshape(t, IDX_HEADS, logits.shape[1]), axis=1)


def _dsa_s_score_kernel(pt_ref, q_ref, w_ref, *refs, pg, t):
    pages = refs[:pg]
    out_ref = refs[pg]
    q16 = q_ref[...].astype(BF16)
    wcol = w_ref[...] * ((IDX_HEADS ** -0.5) * (IDX_DIM ** -0.5))
    for i in range(pg):
        out_ref[:, i * 128:(i + 1) * 128] = _idx_scores(_dot(q16, pages[i][...].astype(BF16)), wcol, t)


def _dsa_s_select_kernel(sc_ref, q_ref, w_ref, kn_ref, bias_ref, *, t, nk, topk):
    q16 = q_ref[...].astype(BF16)
    wcol = w_ref[...] * ((IDX_HEADS ** -0.5) * (IDX_DIM ** -0.5))
    sc_new = _idx_scores(_dot_nt(q16, kn_ref[...].astype(BF16)), wcol, t)
    tq = lax.broadcasted_iota(I32, (t, 128), 0)
    jn = lax.broadcasted_iota(I32, (t, 128), 1)
    new_ok = jn <= tq
    sc_new = jnp.where(new_ok, sc_new, NEG_INF)
    sc_past = sc_ref[...]

    def count_ge(cand):
        c1 = jnp.sum(jnp.where(sc_past >= cand, 1.0, 0.0), axis=1, keepdims=True)
        c2 = jnp.sum(jnp.where(sc_new >= cand, 1.0, 0.0), axis=1, keepdims=True)
        return c1 + c2

    thr = _topk_threshold(count_ge, t, topk)
    keep_all = (nk + 1 + lax.broadcasted_iota(I32, (t, 1), 0)) <= topk
    bias_ref[:, :nk] = jnp.where(keep_all | (sc_past >= thr), 0.0, NEG_INF)
    bias_ref[:, nk:] = jnp.where(new_ok & (keep_all | (sc_new >= thr)), 0.0, NEG_INF)


def _dsa_s_attend_kernel(pt_ref, q_ref, bias_ref, biasn_ref, kn_ref, vn_ref, z_ref, *refs, pg):
    kpages = refs[:pg]
    vpages = refs[pg:2 * pg]
    o_ref = refs[2 * pg]
    m_scr, l_scr, acc_scr = refs[2 * pg + 1:]
    g = pl.program_id(1)
    scale = C_HEAD ** -0.5
    page = kpages[0].shape[0] // C_KV_HEADS

    @pl.when(g == 0)
    def _():
        m_scr[...] = jnp.full(m_scr.shape, NEG_INF, F32)
        l_scr[...] = jnp.zeros(l_scr.shape, F32)
        acc_scr[...] = jnp.zeros(acc_scr.shape, F32)
        for hk in range(C_KV_HEADS):
            cols = slice(hk * C_HEAD, (hk + 1) * C_HEAD)
            q16 = (q_ref[hk] * scale).astype(BF16)
            s = _dot_nt(q16, kn_ref[:, cols].astype(BF16)) + biasn_ref[...]
            _softmax_update(m_scr, l_scr, acc_scr, hk, s, [vn_ref[:, cols].astype(BF16)])

    for hk in range(C_KV_HEADS):
        q16 = (q_ref[hk] * scale).astype(BF16)
        rows = pl.ds(hk, page, stride=C_KV_HEADS)
        tiles = [_dot_nt(q16, kpages[i][rows, :].astype(BF16)) for i in range(pg)]
        s = jnp.concatenate(tiles, axis=1) + bias_ref[...]
        _softmax_update(m_scr, l_scr, acc_scr, hk, s, [vpages[i][rows, :].astype(BF16) for i in range(pg)])

    @pl.when(g == pl.num_programs(1) - 1)
    def _():
        for hk in range(C_KV_HEADS):
            o_ref[hk] = (acc_scr[hk] / l_scr[hk]) * _silu(z_ref[hk])


def _dsa_sample(p3s, cache_k, cache_v, cache_kidx, page_table, layer):
    b, t, _ = p3s.shape
    n_pages = page_table.shape[1]
    page = cache_k.shape[2]
    assert page == 128
    past = n_pages * page
    topk = min(TOPK_MAX, (past + t) // 4)
    pg = 16 if n_pages % 16 == 0 else n_pages
    ng = n_pages // pg
    rows = t * C_GROUPS

    qi = p3s[:, :, OFF["qi"]:OFF["qi"] + IDX_HEADS * IDX_DIM].reshape(b, t * IDX_HEADS, IDX_DIM)
    wi = p3s[:, :, OFF_TAIL + TAIL_WI:OFF_TAIL + TAIL_WI + IDX_HEADS].reshape(b, t * IDX_HEADS, 1)
    ki_new = jnp.pad(p3s[:, :, OFF_TAIL:OFF_TAIL + IDX_DIM], ((0, 0), (0, 128 - t), (0, 0)))
    k_new = jnp.pad(p3s[:, :, OFF["kc"]:OFF["kc"] + C_KV], ((0, 0), (0, 128 - t), (0, 0)))
    v_new = jnp.pad(p3s[:, :, OFF["vc"]:OFF["vc"] + C_KV], ((0, 0), (0, 128 - t), (0, 0)))

    def heads_major(a):
        a = a.reshape(b, t, C_KV_HEADS, C_GROUPS, C_HEAD)
        return jnp.transpose(a, (0, 2, 1, 3, 4)).reshape(b, C_KV_HEADS, rows, C_HEAD)

    qh = heads_major(p3s[:, :, OFF["qc"]:OFF["qc"] + W_C])
    zh = heads_major(p3s[:, :, OFF["zc"]:OFF["zc"] + W_C])

    kidx_t = jnp.swapaxes(cache_kidx, 2, 3)
    score_spec = pltpu.PrefetchScalarGridSpec(
        num_scalar_prefetch=1,
        grid=(b, ng),
        in_specs=[pl.BlockSpec((None, t * IDX_HEADS, IDX_DIM), lambda i, g, pt: (i, 0, 0)),
                  pl.BlockSpec((None, t * IDX_HEADS, 1), lambda i, g, pt: (i, 0, 0))]
                 + [pl.BlockSpec((None, None, IDX_DIM, page),
                                 lambda i, g, pt, j=j: (layer, pt[i, g * pg + j], 0, 0)) for j in range(pg)],
        out_specs=pl.BlockSpec((None, t, pg * page), lambda i, g, pt: (i, 0, g)),
    )
    scores = pl.pallas_call(
        functools.partial(_dsa_s_score_kernel, pg=pg, t=t),
        grid_spec=score_spec,
        out_shape=jax.ShapeDtypeStruct((b, t, past), F32),
        compiler_params=_cp(("arbitrary", "arbitrary"), 32),
        name="dsa_s_score",
    )(page_table, qi, wi, *([kidx_t] * pg))

    bias = pl.pallas_call(
        functools.partial(_dsa_s_select_kernel, t=t, nk=past, topk=topk),
        grid=(b,),
        in_specs=[pl.BlockSpec((None, t, past), lambda i: (i, 0, 0)),
                  pl.BlockSpec((None, t * IDX_HEADS, IDX_DIM), lambda i: (i, 0, 0)),
                  pl.BlockSpec((None, t * IDX_HEADS, 1), lambda i: (i, 0, 0)),
                  pl.BlockSpec((None, 128, IDX_DIM), lambda i: (i, 0, 0))],
        out_specs=pl.BlockSpec((None, t, past + 128), lambda i: (i, 0, 0)),
        out_shape=jax.ShapeDtypeStruct((b, t, past + 128), F32),
        compiler_params=_cp(("arbitrary",), 32),
        name="dsa_s_select",
    )(scores, qi, wi, ki_new)

    bias_rows = jnp.repeat(bias, C_GROUPS, axis=1)
    bias_past = bias_rows[:, :, :past]
    bias_new = bias_rows[:, :, past:]

    ck = cache_k.reshape(cache_k.shape[0], cache_k.shape[1], page * C_KV_HEADS, C_HEAD)
    cv = cache_v.reshape(cache_v.shape[0], cache_v.shape[1], page * C_KV_HEADS, C_HEAD)
    pspec = [pl.BlockSpec((None, None, page * C_KV_HEADS, C_HEAD),
                          lambda i, g, pt, j=j: (layer, pt[i, g * pg + j], 0, 0)) for j in range(pg)]
    attend_spec = pltpu.PrefetchScalarGridSpec(
        num_scalar_prefetch=1,
        grid=(b, ng),
        in_specs=[pl.BlockSpec((None, C_KV_HEADS, rows, C_HEAD), lambda i, g, pt: (i, 0, 0, 0)),
                  pl.BlockSpec((None, rows, pg * page), lambda i, g, pt: (i, 0, g)),
                  pl.BlockSpec((None, rows, 128), lambda i, g, pt: (i, 0, 0)),
                  pl.BlockSpec((None, 128, C_KV), lambda i, g, pt: (i, 0, 0)),
                  pl.BlockSpec((None, 128, C_KV), lambda i, g, pt: (i, 0, 0)),
                  pl.BlockSpec((None, C_KV_HEADS, rows, C_HEAD), lambda i, g, pt: (i, 0, 0, 0))]
                 + pspec + pspec,
        out_specs=pl.BlockSpec((None, C_KV_HEADS, rows, C_HEAD), lambda i, g, pt: (i, 0, 0, 0)),
        scratch_shapes=[pltpu.VMEM((C_KV_HEADS, rows, 1), F32), pltpu.VMEM((C_KV_HEADS, rows, 1), F32),
                        pltpu.VMEM((C_KV_HEADS, rows, C_HEAD), F32)],
    )
    oh = pl.pallas_call(
        functools.partial(_dsa_s_attend_kernel, pg=pg),
        grid_spec=attend_spec,
        out_shape=jax.ShapeDtypeStruct((b, C_KV_HEADS, rows, C_HEAD), F32),
        compiler_params=_cp(("arbitrary", "arbitrary"), 40),
        name="dsa_s_attend",
    )(page_table, qh, bias_past, bias_new, k_new, v_new, zh, *([ck] * pg), *([cv] * pg))
    y = jnp.transpose(oh.reshape(b, C_KV_HEADS, t, C_GROUPS, C_HEAD), (0, 2, 1, 3, 4)).reshape(b, t, W_C)
    return y.astype(BF16)


def _outproj_kernel(ya_ref, yb_ref, yc_ref, w_ref, x_ref, gate_ref, o_ref, mix_scr):
    @pl.when(pl.program_id(1) == 0)
    def _():
        mix_scr[:, 0:W_A] = ya_ref[...]
        mix_scr[:, W_A:W_A + W_B] = yb_ref[...]
        mix_scr[:, W_A + W_B:] = yc_ref[...]

    o_ref[...] = x_ref[...] + gate_ref[...] * _dot(mix_scr[...], w_ref[...])


def _outproj(ya, yb, yc, w_bf16, x2, gate, *, rows_per_batch, mod_row0):
    m, d = x2.shape
    tn = 1024
    if gate.ndim == 3:
        tm = 512
        gate_spec = pl.BlockSpec((None, 1, tn),
                                 lambda i, j: ((i * tm) // rows_per_batch + mod_row0, 0, 2 * (d // tn) + j))
    else:
        tm = m
        gate_spec = pl.BlockSpec((tm, tn), lambda i, j: (i, j))
    assert m % tm == 0
    return pl.pallas_call(
        _outproj_kernel,
        grid=(m // tm, d // tn),
        in_specs=[pl.BlockSpec((tm, W_A), lambda i, j: (i, 0)),
                  pl.BlockSpec((tm, W_B), lambda i, j: (i, 0)),
                  pl.BlockSpec((tm, W_C), lambda i, j: (i, 0)),
                  pl.BlockSpec((d, tn), lambda i, j: (0, j)),
                  pl.BlockSpec((tm, tn), lambda i, j: (i, j)),
                  gate_spec],
        out_specs=pl.BlockSpec((tm, tn), lambda i, j: (i, j)),
        out_shape=jax.ShapeDtypeStruct((m, d), F32),
        scratch_shapes=[pltpu.VMEM((tm, d), BF16)],
        compiler_params=_cp(("arbitrary", "arbitrary"), 48),
        name="outproj",
    )(ya, yb, yc, w_bf16, x2, gate)


def _final_norm_kernel(x_ref, g_ref, o_ref):
    x = x_ref[...]
    o_ref[...] = x * lax.rsqrt(jnp.mean(x * x, axis=-1, keepdims=True) + EPS) * g_ref[...]


def _final_norm(x2, g):
    m, d = x2.shape
    tm = 256 if m % 256 == 0 else m
    return pl.pallas_call(
        _final_norm_kernel,
        grid=(m // tm,),
        in_specs=[pl.BlockSpec((tm, d), lambda i: (i, 0)), pl.BlockSpec((1, d), lambda i: (0, 0))],
        out_specs=pl.BlockSpec((tm, d), lambda i: (i, 0)),
        out_shape=jax.ShapeDtypeStruct((m, d), F32),
        compiler_params=_cp(("arbitrary",), 32),
        name="final_norm",
    )(x2, g.reshape(1, d))


def _permute_w_in(w):
    parts = [w[:, _SRC_OFF[n][0]:_SRC_OFF[n][0] + _SRC_OFF[n][1]] for n in _DST_ORDER]
    pad = NP - D_IN
    if pad:
        parts.append(jnp.zeros((w.shape[0], pad), w.dtype))
    return jnp.concatenate(parts, axis=1).astype(BF16)


def _delta_t(l):
    for t in (128, 64):
        if l % t == 0:
            return t
    raise ValueError("sequence length must be a multiple of 64")


def kernel(x_prompt, x_sample, cache_k, cache_v, cache_kidx, state_dn, state_conv, page_table,
           c_prompt, c_sample, w_ada, b_ada, g_norm, w_in, a_vnorm, a_ws, a_bs, dn_conv_w,
           dn_a_log, dn_dt_bias, dn_onorm, w_out, g_final):
    bp, lp, d = x_prompt.shape
    bs, ls, _ = x_sample.shape
    depth = w_ada.shape[0]
    assert d == D_MODEL and w_in.shape[2] == D_IN
    assert CONV_K - 1 <= ls <= DN_CHUNK

    n_c = bp + bs
    c_rows = jnp.concatenate([c_prompt, c_sample], axis=0)
    r_pad = (-n_c) % 8
    if r_pad:
        c_rows = jnp.pad(c_rows, ((0, r_pad), (0, 0)))
    m_all = _ada(c_rows, w_ada, b_ada)

    xp = x_prompt.reshape(bp * lp, d)
    xs = x_sample.reshape(bs * ls, d)
    ls_pad = DN_CHUNK
    outs = {k: [] for k in ("pk", "pv", "pki", "pdn", "pconv", "sk", "sv", "ski", "sdn", "sconv", "samlp")}
    zeros_conv = jnp.zeros((bp, CONV_K - 1, DN_CONV_DIM), F32)
    zeros_state = jnp.zeros((bp, DN_HEADS, DN_HEAD, DN_HEAD), F32)

    for l in range(depth):
        w_in_l = _permute_w_in(w_in[l])
        w_out_l = w_out[l].astype(BF16)
        g_l = g_norm[l].reshape(1, d)
        m_l = m_all[l]
        m3 = m_l.reshape(m_l.shape[0], 1, 3 * d)
        ms = jnp.repeat(m_l[bp:bp + bs], ls, axis=0)

        pp = _inproj(xp, g_l, m3, m3, w_in_l, rows_per_batch=lp, mod_row0=0)
        p3 = pp.reshape(bp, lp, NP)
        (ya,) = _mixa(p3, a_vnorm[l], a_ws[l], a_bs[l], emit_va=False)
        abt = jnp.transpose(p3[:, :, OFF["a"]:OFF["a"] + 2 * DN_HEADS], (0, 2, 1))
        yb, s_p = _delta(p3, abt, dn_conv_w[l], zeros_conv, zeros_state, dn_a_log[l], dn_dt_bias[l],
                         dn_onorm[l], t=_delta_t(lp), valid_len=_delta_t(lp))
        yc = _dsa_prompt(p3)
        xp = _outproj(ya.reshape(bp * lp, W_A), yb.reshape(bp * lp, W_B), yc.reshape(bp * lp, W_C),
                      w_out_l, xp, m3, rows_per_batch=lp, mod_row0=0)
        outs["pk"].append(p3[:, :, OFF["kc"]:OFF["kc"] + C_KV].reshape(bp, lp, C_KV_HEADS, C_HEAD))
        outs["pv"].append(p3[:, :, OFF["vc"]:OFF["vc"] + C_KV].reshape(bp, lp, C_KV_HEADS, C_HEAD))
        outs["pki"].append(p3[:, :, OFF_TAIL:OFF_TAIL + IDX_DIM])
        outs["pdn"].append(s_p)
        outs["pconv"].append(p3[:, lp - (CONV_K - 1):, OFF["qkv"]:OFF["qkv"] + DN_CONV_DIM])

        ps = _inproj(xs, g_l, ms[:, d:2 * d], ms[:, 0:d], w_in_l, rows_per_batch=ls, mod_row0=bp)
        p3s = ps.reshape(bs, ls, NP)
        ya_s, va_s = _mixa(p3s, a_vnorm[l], a_ws[l], a_bs[l], emit_va=True)
        p3s_pad = jnp.pad(p3s, ((0, 0), (0, ls_pad - ls), (0, 0)))
        abt_s = jnp.transpose(p3s_pad[:, :, OFF["a"]:OFF["a"] + 2 * DN_HEADS], (0, 2, 1))
        yb_s, s_s = _delta(p3s_pad, abt_s, dn_conv_w[l], state_conv[l], state_dn[l], dn_a_log[l],
                           dn_dt_bias[l], dn_onorm[l], t=ls_pad, valid_len=ls)
        yc_s = _dsa_sample(p3s, cache_k, cache_v, cache_kidx, page_table, l)
        xs = _outproj(ya_s.reshape(bs * ls, W_A), yb_s[:, :ls].reshape(bs * ls, W_B),
                      yc_s.reshape(bs * ls, W_C), w_out_l, xs, ms[:, 2 * d:3 * d],
                      rows_per_batch=ls, mod_row0=bp)
        outs["sk"].append(p3s[:, :, OFF["kc"]:OFF["kc"] + C_KV].reshape(bs, ls, C_KV_HEADS, C_HEAD))
        outs["sv"].append(p3s[:, :, OFF["vc"]:OFF["vc"] + C_KV].reshape(bs, ls, C_KV_HEADS, C_HEAD))
        outs["ski"].append(p3s[:, :, OFF_TAIL:OFF_TAIL + IDX_DIM])
        outs["sdn"].append(s_s)
        outs["sconv"].append(p3s[:, ls - (CONV_K - 1):, OFF["qkv"]:OFF["qkv"] + DN_CONV_DIM])
        outs["samlp"].append(va_s)

    y_prompt = _final_norm(xp, g_final).reshape(bp, lp, d)
    y_sample = _final_norm(xs, g_final).reshape(bs, ls, d)
    st = jnp.stack
    return (y_prompt, y_sample, st(outs["pk"]), st(outs["pv"]), st(outs["pki"]), st(outs["pdn"]),
            st(outs["pconv"]), st(outs["sk"]), st(outs["sv"]), st(outs["ski"]), st(outs["sdn"]),
            st(outs["sconv"]), st(outs["samlp"]))
```

```python
import functools

import jax
import jax.numpy as jnp
from jax import lax
from jax.experimental import pallas as pl
from jax.experimental.pallas import tpu as pltpu

F32 = jnp.float32
BF16 = jnp.bfloat16
I32 = jnp.int32
EPS = 1e-6
INT_MIN = -(2 ** 31)
NEG_INF = float("-inf")

D_MODEL = 4096
W_A = D_MODEL // 4
A_GROUP = 128
A_HEADS = W_A // A_GROUP
A_CHUNK = 128
W_B = D_MODEL // 2
DN_HEAD = 128
DN_HEADS = W_B // DN_HEAD
CONV_K = 4
DN_CONV_DIM = 3 * W_B
DN_CHUNK = 64
W_C = D_MODEL - W_A - W_B
C_HEAD = 128
C_HEADS = W_C // C_HEAD
C_KV_HEADS = 2
C_GROUPS = C_HEADS // C_KV_HEADS
C_KV = C_KV_HEADS * C_HEAD
IDX_HEADS = 16
IDX_DIM = 64
TOPK_MAX = 256
Q_BLOCK = 128

_SRC_SPLITS = (W_A, W_A, W_A, DN_CONV_DIM, W_B, DN_HEADS, DN_HEADS,
               W_C, C_KV, C_KV, W_C, IDX_HEADS * IDX_DIM, IDX_DIM, IDX_HEADS)
_SRC_NAMES = ("u", "v", "za", "qkv", "zb", "a", "b", "qc", "kc", "vc", "zc", "qi", "ki", "wi")
_SRC_OFF = {}
_o = 0
for _n, _w in zip(_SRC_NAMES, _SRC_SPLITS):
    _SRC_OFF[_n] = (_o, _w)
    _o += _w
D_IN = _o

_DST_ORDER = ("u", "v", "za", "qkv", "zb", "qc", "zc", "qi", "kc", "vc", "ki", "a", "b", "wi")
OFF = {}
_o = 0
for _n in _DST_ORDER:
    OFF[_n] = _o
    _o += _SRC_OFF[_n][1]
NP = ((_o + 127) // 128) * 128
OFF_TAIL = OFF["ki"]
TAIL_A = OFF["a"] - OFF_TAIL
TAIL_B = OFF["b"] - OFF_TAIL
TAIL_WI = OFF["wi"] - OFF_TAIL

DELTA_HEADS_PER_STEP = 4
DSA_KEY_TILE = 512


def _cp(sem, vmem_mb=48):
    return pltpu.CompilerParams(dimension_semantics=sem, vmem_limit_bytes=vmem_mb * 1024 * 1024)


def _silu(x):
    return x * (1.0 / (1.0 + jnp.exp(-x)))


def _gelu(x):
    return 0.5 * x * (1.0 + jnp.tanh(0.7978845608028654 * (x + 0.044715 * (x * x * x))))


def _dot(a, b):
    return jnp.dot(a, b, preferred_element_type=F32)


def _dot_nt(a, b):
    return lax.dot_general(a, b, (((1,), (1,)), ((), ())), preferred_element_type=F32)


def _dot_tn(a, b):
    return lax.dot_general(a, b, (((0,), (0,)), ((), ())), preferred_element_type=F32)


def _split_bf16(a):
    hi = a.astype(BF16)
    lo = (a - hi.astype(F32)).astype(BF16)
    return hi, lo


def _dot3s(a, b):
    ah, al = a
    bh, bl = b
    return _dot(ah, bh) + (_dot(al, bh) + _dot(ah, bl))


def _ada_kernel(c_ref, w_ref, b_ref, o_ref):
    s = _silu(c_ref[...]).astype(BF16)
    o_ref[...] = _dot(s, w_ref[...].astype(BF16)) + b_ref[...]


def _ada(c_rows, w_ada, b_ada):
    depth, d, n = w_ada.shape
    r = c_rows.shape[0]
    tn = 512
    return pl.pallas_call(
        _ada_kernel,
        grid=(depth, n // tn),
        in_specs=[pl.BlockSpec((r, d), lambda l, j: (0, 0)),
                  pl.BlockSpec((None, d, tn), lambda l, j: (l, 0, j)),
                  pl.BlockSpec((None, 1, tn), lambda l, j: (l, 0, j))],
        out_specs=pl.BlockSpec((None, r, tn), lambda l, j: (l, 0, j)),
        out_shape=jax.ShapeDtypeStruct((depth, r, n), F32),
        compiler_params=_cp(("arbitrary", "arbitrary"), 40),
        name="ada",
    )(c_rows, w_ada, b_ada.reshape(depth, 1, n))


def _inproj_kernel(x_ref, g_ref, sc_ref, sh_ref, w_ref, o_ref, h_scr, *, rc):
    @pl.when(pl.program_id(1) == 0)
    def _():
        tm = x_ref.shape[0]
        per_row = sc_ref.shape[0] != 1

        def body(c, carry):
            r = pl.ds(pl.multiple_of(c * rc, rc), rc)
            x = x_ref[r, :]
            y = x * lax.rsqrt(jnp.mean(x * x, axis=-1, keepdims=True) + EPS)
            sc = sc_ref[r, :] if per_row else sc_ref[...]
            sh = sh_ref[r, :] if per_row else sh_ref[...]
            h_scr[r, :] = ((y * g_ref[...]) * (1.0 + sc) + sh).astype(BF16)
            return carry

        lax.fori_loop(0, tm // rc, body, 0)

    o_ref[...] = _dot(h_scr[...], w_ref[...])


def _inproj(x2, g, sc, sh, w_bf16, layer, *, rows_per_batch, mod_row0):
    m, d = x2.shape
    n = w_bf16.shape[2]
    tn = 1152
    assert n % tn == 0
    if sc.ndim == 3:
        tm = 512
        assert rows_per_batch % tm == 0
        sc_spec = pl.BlockSpec((None, 1, d), lambda i, j: ((i * tm) // rows_per_batch + mod_row0, 0, 1))
        sh_spec = pl.BlockSpec((None, 1, d), lambda i, j: ((i * tm) // rows_per_batch + mod_row0, 0, 0))
        rc = 64
    else:
        tm = m
        sc_spec = pl.BlockSpec((tm, d), lambda i, j: (i, 0))
        sh_spec = pl.BlockSpec((tm, d), lambda i, j: (i, 0))
        rc = tm
    assert m % tm == 0
    return pl.pallas_call(
        functools.partial(_inproj_kernel, rc=rc),
        grid=(m // tm, n // tn),
        in_specs=[pl.BlockSpec((tm, d), lambda i, j: (i, 0)),
                  pl.BlockSpec((1, d), lambda i, j: (0, 0)),
                  sc_spec, sh_spec,
                  pl.BlockSpec((None, d, tn), lambda i, j: (layer, 0, j))],
        out_specs=pl.BlockSpec((tm, tn), lambda i, j: (i, j)),
        out_shape=jax.ShapeDtypeStruct((m, n), F32),
        scratch_shapes=[pltpu.VMEM((tm, d), BF16)],
        compiler_params=_cp(("arbitrary", "arbitrary"), 56),
        name="inproj",
    )(x2, g, sc, sh, w_bf16)


def _mixa_kernel(u_ref, v_ref, z_ref, vn_ref, ws_ref, bst_ref, y_ref, *rest, c, emit_va):
    u = _gelu(u_ref[...])
    v = _gelu(v_ref[...])
    mu = jnp.mean(v, axis=-1, keepdims=True)
    dv = v - mu
    va = dv * lax.rsqrt(jnp.mean(dv * dv, axis=-1, keepdims=True) + EPS) * vn_ref[...]
    if emit_va:
        rest[0][...] = va
    z = _silu(z_ref[...])
    row = lax.broadcasted_iota(I32, (c, c), 0)
    col = lax.broadcasted_iota(I32, (c, c), 1)
    tril = col <= row
    for h in range(A_HEADS):
        cols = slice(h * A_GROUP, (h + 1) * A_GROUP)
        wm = jnp.where(tril, ws_ref[h], 0.0)
        vh = va[:, cols]
        if c >= 128:
            mixed = _dot(wm.astype(BF16), vh.astype(BF16))
        else:
            mixed = wm[:, 0:1] * vh[0:1, :]
            for s in range(1, c):
                mixed = mixed + wm[:, s:s + 1] * vh[s:s + 1, :]
        mixed = mixed + bst_ref[:, h:h + 1]
        y_ref[:, cols] = (u[:, cols] * mixed * z[:, cols]).astype(y_ref.dtype)


def _mixa(p3, a_vnorm, a_ws, a_bs, *, emit_va):
    b, l, _ = p3.shape
    c = min(A_CHUNK, l)
    n = l // c
    ws = a_ws[:, :c, :c]
    bst = a_bs[:, :c].T
    wblk = W_A
    outs = [jax.ShapeDtypeStruct((b, l, W_A), BF16)]
    out_specs = [pl.BlockSpec((None, c, W_A), lambda i, j: (i, j, 0))]
    if emit_va:
        outs.append(jax.ShapeDtypeStruct((b, l, W_A), F32))
        out_specs.append(pl.BlockSpec((None, c, W_A), lambda i, j: (i, j, 0)))
    res = pl.pallas_call(
        functools.partial(_mixa_kernel, c=c, emit_va=emit_va),
        grid=(b, n),
        in_specs=[pl.BlockSpec((None, c, wblk), lambda i, j: (i, j, OFF["u"] // wblk)),
                  pl.BlockSpec((None, c, wblk), lambda i, j: (i, j, OFF["v"] // wblk)),
                  pl.BlockSpec((None, c, wblk), lambda i, j: (i, j, OFF["za"] // wblk)),
                  pl.BlockSpec((1, W_A), lambda i, j: (0, 0)),
                  pl.BlockSpec((A_HEADS, c, c), lambda i, j: (0, 0, 0)),
                  pl.BlockSpec((c, A_HEADS), lambda i, j: (0, 0))],
        out_specs=out_specs,
        out_shape=outs,
        compiler_params=_cp(("arbitrary", "arbitrary"), 32),
        name="mixa",
    )(p3, p3, p3, a_vnorm.reshape(1, W_A), ws, bst)
    return res


def _delta_kernel(alog_ref, dtb_ref,
                  q_ref, k_ref, v_ref, z_ref, ab_ref,
                  cwq_ref, cwk_ref, cwv_ref, cpq_ref, cpk_ref, cpv_ref,
                  s0_ref, on_ref,
                  y_ref, sout_ref,
                  xbuf, s_scr, *, t, c, hp, valid_len):
    hg = pl.program_id(1)
    n = pl.program_id(2)
    nlast = pl.num_programs(2) - 1
    dh = DN_HEAD

    @pl.when(n == 0)
    def _():
        s_scr[...] = s0_ref[...]
        for j in range(hp):
            lanes = slice(j * dh, (j + 1) * dh)
            xbuf[3 * j + 0, 5:8, :] = cpq_ref[:, lanes]
            xbuf[3 * j + 1, 5:8, :] = cpk_ref[:, lanes]
            xbuf[3 * j + 2, 5:8, :] = cpv_ref[:, lanes]

    row = lax.broadcasted_iota(I32, (t, t), 0)
    col = lax.broadcasted_iota(I32, (t, t), 1)
    shift = c.bit_length() - 1
    same = (row >> shift) == (col >> shift)
    eye = row == col
    incl = same & (col <= row)
    strict = same & (col < row)
    incl_t = same & (row <= col)
    blk8 = (row >> 3) == (col >> 3)
    off_masks = []
    bs = 8
    while bs < c:
        sh_b = bs.bit_length() - 1
        inner = (row >> sh_b) == (col >> sh_b)
        outer = (row >> (sh_b + 1)) == (col >> (sh_b + 1))
        off_masks.append(outer & jnp.logical_not(inner))
        bs *= 2
    eye_f = jnp.where(eye, 1.0, 0.0)
    if valid_len < t:
        lane_valid = lax.broadcasted_iota(I32, (1, t), 1) < valid_len
        sub_valid = lax.broadcasted_iota(I32, (t, 1), 0) < valid_len

    def to_col(r):
        return jnp.sum(jnp.where(eye, r, 0.0), axis=1, keepdims=True)

    def conv(idx, x, w_ref, lanes):
        xbuf[idx, 8:8 + t, :] = x
        y = xbuf[idx, pl.ds(5, t), :] * w_ref[0:1, lanes]
        for jj in range(1, CONV_K):
            y = y + xbuf[idx, pl.ds(5 + jj, t), :] * w_ref[jj:jj + 1, lanes]
        xbuf[idx, 5:8, :] = x[t - 3:t, :]
        return _silu(y)

    heads = range(hp)
    lanes_of = [slice(j * dh, (j + 1) * dh) for j in heads]

    def per_head(f, *lists):
        return [f(*vals) for vals in zip(*lists)]

    q = [conv(3 * j + 0, q_ref[:, lanes_of[j]], cwq_ref, lanes_of[j]) for j in heads]
    k = [conv(3 * j + 1, k_ref[:, lanes_of[j]], cwk_ref, lanes_of[j]) for j in heads]
    v = [conv(3 * j + 2, v_ref[:, lanes_of[j]], cwv_ref, lanes_of[j]) for j in heads]
    qc = per_head(lambda a: a * lax.rsqrt(jnp.sum(a * a, axis=-1, keepdims=True) + EPS) * (dh ** -0.5), q)
    kc = per_head(lambda a: a * lax.rsqrt(jnp.sum(a * a, axis=-1, keepdims=True) + EPS), k)

    def gates(j):
        h = hg * hp + j
        a_row = ab_ref[pl.ds(h, 1), :]
        b_row = ab_ref[pl.ds(DN_HEADS + h, 1), :]
        xa = a_row + dtb_ref[h]
        softplus = jnp.maximum(xa, 0.0) + jnp.log(1.0 + jnp.exp(-jnp.abs(xa)))
        a_coef = jnp.exp(jnp.zeros((1, 1), F32) + alog_ref[h])
        g_row = -a_coef * softplus
        beta_row = 1.0 / (1.0 + jnp.exp(-b_row))
        if valid_len < t:
            g_row = jnp.where(lane_valid, g_row, 0.0)
            beta_row = jnp.where(lane_valid, beta_row, 0.0)
        return g_row, beta_row

    g_row, beta_row = zip(*[gates(j) for j in heads])
    if valid_len < t:
        kc = per_head(lambda a: jnp.where(sub_valid, a, 0.0), kc)
        v = per_head(lambda a: jnp.where(sub_valid, a, 0.0), v)

    g_col = per_head(to_col, g_row)
    beta_col = per_head(to_col, beta_row)
    gc_col = per_head(lambda r: jnp.sum(jnp.where(incl, r, 0.0), axis=1, keepdims=True), g_row)
    glast_col = per_head(lambda r: jnp.sum(jnp.where(same, r, 0.0), axis=1, keepdims=True), g_row)
    gc_row = per_head(lambda cl: jnp.sum(jnp.where(incl_t, cl, 0.0), axis=0, keepdims=True), g_col)
    decay = per_head(lambda gc, gr: jnp.where(incl, jnp.exp(jnp.where(incl, gc - gr, 0.0)), 0.0), gc_col, gc_row)
    kb = per_head(lambda a, b: a * b, kc, beta_col)
    kc16 = per_head(lambda a: a.astype(BF16), kc)
    a_mat = per_head(lambda a, b, dc: jnp.where(strict, _dot_nt(a.astype(BF16), b) * dc, 0.0), kb, kc16, decay)
    attn = per_head(lambda a, b, dc: (_dot_nt(a.astype(BF16), b) * dc).astype(BF16), qc, kc16, decay)
    eg = per_head(jnp.exp, gc_col)
    rhs = per_head(lambda vv, bc, kk, e: _split_bf16(jnp.concatenate([vv * bc, kk * e], axis=1)),
                   v, beta_col, kb, eg)

    n0f = per_head(lambda a: jnp.where(blk8, -a, 0.0), a_mat)
    n0 = per_head(_split_bf16, n0f)
    n2 = per_head(lambda a: _split_bf16(_dot3s(a, a)), n0)
    n4 = per_head(lambda a: _split_bf16(_dot3s(a, a)), n2)
    x = per_head(lambda a: eye_f + a, n0f)
    x = per_head(lambda xx, nn: xx + _dot3s(_split_bf16(xx), nn), x, n2)
    x = per_head(lambda xx, nn: xx + _dot3s(_split_bf16(xx), nn), x, n4)
    for om in off_masks:
        xs = per_head(_split_bf16, x)
        xa_off = per_head(lambda s_, a: _split_bf16(_dot3s(s_, _split_bf16(jnp.where(om, a, 0.0)))), xs, a_mat)
        x = per_head(lambda xx, xo, s_: xx - _dot3s(xo, s_), x, xa_off, xs)
    sol16 = per_head(lambda xx, r: _dot3s(_split_bf16(xx), r).astype(BF16), x, rhs)
    auw = per_head(_dot, attn, sol16)
    qw = per_head(lambda a, e, m_: (a * e - m_[:, dh:]).astype(BF16), qc, eg, auw)
    kg16 = per_head(lambda a, gl_, gc: (a * jnp.exp(gl_ - gc)).astype(BF16), kc, glast_col, gc_col)

    s = [s_scr[j] for j in heads]
    outs = [[] for _ in heads]
    for i in range(t // c):
        rows = slice(i * c, (i + 1) * c)
        s16 = per_head(lambda a: a.astype(BF16), s)
        kuw = per_head(lambda a, b: _dot_tn(a[rows], b[rows]), kg16, sol16)
        for j in heads:
            outs[j].append(_dot(qw[j][rows], s16[j]) + auw[j][rows, :dh])
        gl = per_head(lambda a: jnp.exp(a[i * c:i * c + 1, :]), glast_col)
        s = per_head(lambda g_, s_, m_, b16: g_ * s_ + m_[:, :dh] - _dot(m_[:, dh:].astype(BF16), b16),
                     gl, s, kuw, s16)
    for j in heads:
        s_scr[j] = s[j]
        o = outs[j][0] if len(outs[j]) == 1 else jnp.concatenate(outs[j], axis=0)
        on = o * lax.rsqrt(jnp.mean(o * o, axis=-1, keepdims=True) + EPS) * on_ref[...]
        y_ref[:, lanes_of[j]] = (on * _silu(z_ref[:, lanes_of[j]])).astype(y_ref.dtype)

    @pl.when(n == nlast)
    def _():
        sout_ref[...] = s_scr[...]


def _delta(p3, abt, conv_w, conv_prev, s0, a_log, dt_bias, onorm, *, t, valid_len):
    b, l, _ = p3.shape
    assert l % t == 0
    c = min(DN_CHUNK, t)
    hp = DELTA_HEADS_PER_STEP
    nh = DN_HEADS
    ng = nh // hp
    w = 128 * hp
    cb_q = OFF["qkv"] // w
    cb_z = OFF["zb"] // w
    assert OFF["qkv"] % w == 0 and OFF["zb"] % w == 0 and W_B % w == 0

    def pspec(cb0):
        return pl.BlockSpec((None, t, w), lambda i, h, n, a, d: (i, n, cb0 + h))

    def cwspec(sidx):
        return pl.BlockSpec((CONV_K, w), lambda i, h, n, a, d: (0, sidx * ng + h))

    def cpspec(sidx):
        return pl.BlockSpec((None, CONV_K - 1, w), lambda i, h, n, a, d: (i, 0, sidx * ng + h))

    grid_spec = pltpu.PrefetchScalarGridSpec(
        num_scalar_prefetch=2,
        grid=(b, ng, l // t),
        in_specs=[pspec(cb_q), pspec(cb_q + ng), pspec(cb_q + 2 * ng), pspec(cb_z),
                  pl.BlockSpec((None, 2 * nh, t), lambda i, h, n, a, d: (i, 0, n)),
                  cwspec(0), cwspec(1), cwspec(2), cpspec(0), cpspec(1), cpspec(2),
                  pl.BlockSpec((None, hp, DN_HEAD, DN_HEAD), lambda i, h, n, a, d: (i, h, 0, 0)),
                  pl.BlockSpec((1, DN_HEAD), lambda i, h, n, a, d: (0, 0))],
        out_specs=[pl.BlockSpec((None, t, w), lambda i, h, n, a, d: (i, n, h)),
                   pl.BlockSpec((None, hp, DN_HEAD, DN_HEAD), lambda i, h, n, a, d: (i, h, 0, 0))],
        scratch_shapes=[pltpu.VMEM((3 * hp, t + 8, 128), F32), pltpu.VMEM((hp, DN_HEAD, DN_HEAD), F32)],
    )
    return pl.pallas_call(
        functools.partial(_delta_kernel, t=t, c=c, hp=hp, valid_len=valid_len),
        grid_spec=grid_spec,
        out_shape=[jax.ShapeDtypeStruct((b, l, W_B), BF16),
                   jax.ShapeDtypeStruct((b, nh, DN_HEAD, DN_HEAD), F32)],
        compiler_params=_cp(("arbitrary", "arbitrary", "arbitrary"), 40),
        name="delta",
    )(a_log, dt_bias, p3, p3, p3, p3, abt, conv_w, conv_w, conv_w,
      conv_prev, conv_prev, conv_prev, s0, onorm.reshape(1, DN_HEAD))


def _ordered_bits_to_float(u):
    key = u ^ jnp.int32(INT_MIN)
    bits = jnp.where(key < 0, key ^ jnp.int32(0x7FFFFFFF), key)
    return pltpu.bitcast(bits, F32)


def _topk_threshold(count_ge, shape, topk):
    def bit_body(i, u):
        bit = jnp.left_shift(jnp.int32(1), 31 - i)
        cand_u = u | bit
        cnt = count_ge(_ordered_bits_to_float(cand_u))
        return jnp.where(cnt >= float(topk), cand_u, u)

    u = lax.fori_loop(0, 32, bit_body, jnp.zeros(shape, I32))
    return _ordered_bits_to_float(u)


def _softmax_update(m_ref, l_ref, acc_ref, idx, s, v_tiles):
    m_old = m_ref[idx]
    m_new = jnp.maximum(m_old, jnp.max(s, axis=1, keepdims=True))
    m_safe = jnp.where(m_new == NEG_INF, 0.0, m_new)
    alpha = jnp.exp(m_old - m_safe)
    p = jnp.exp(s - m_safe)
    l_ref[idx] = alpha * l_ref[idx] + jnp.sum(p, axis=1, keepdims=True)
    acc = alpha * acc_ref[idx]
    p16 = p.astype(BF16)
    for i, vt in enumerate(v_tiles):
        acc = acc + _dot(p16[:, i * 128:(i + 1) * 128], vt)
    acc_ref[idx] = acc
    m_ref[idx] = m_new


def _dsa_prompt_kernel(qi_ref, tq_ref, tall_ref, qc_ref, k_ref, v_ref, z_ref, y_ref,
                       sc_scr, m_scr, l_scr, acc_scr, *, l, topk, kt):
    qb = pl.program_id(1)
    nq = Q_BLOCK
    n_tiles = (qb * nq + nq + kt - 1) // kt
    n_pairs = C_HEADS // 2
    w_rows = tq_ref[...].T[TAIL_WI:TAIL_WI + IDX_HEADS, :] * ((IDX_HEADS ** -0.5) * (IDX_DIM ** -0.5))
    rq = []
    for j in range(IDX_HEADS // 2):
        a = qi_ref[:, j * 128:(j + 1) * 128]
        rq.append(jnp.concatenate([a, pltpu.roll(a, IDX_DIM, 1)], axis=0).astype(BF16))
    kpos = lax.broadcasted_iota(I32, (kt, nq), 0)
    qpos = qb * nq + lax.broadcasted_iota(I32, (kt, nq), 1)
    lane = lax.broadcasted_iota(I32, (kt, 128), 1)

    def score_tile(ti, carry):
        k0 = pl.multiple_of(ti * kt, kt)
        ki = jnp.where(lane < IDX_DIM, tall_ref[pl.ds(k0, kt), :], 0.0).astype(BF16)
        score = jnp.zeros((kt, nq), F32)
        for j in range(IDX_HEADS // 2):
            lg = _dot_nt(ki, rq[j])
            score = score + w_rows[2 * j:2 * j + 1, :] * jnp.maximum(lg[:, :nq], 0.0)
            score = score + w_rows[2 * j + 1:2 * j + 2, :] * jnp.maximum(lg[:, nq:], 0.0)
        sc_scr[pl.ds(k0, kt), :] = jnp.where(kpos + k0 <= qpos, score, NEG_INF)
        return carry

    lax.fori_loop(0, n_tiles, score_tile, 0)

    ct = 256
    n_ct = (qb * nq + nq + ct - 1) // ct
    n_acc = 4

    def count_ge(cand):
        cand8 = jnp.broadcast_to(cand, (8, nq))

        def tile_body(ti, accs):
            k0 = pl.multiple_of(ti * ct, ct)
            tile = sc_scr[pl.ds(k0, ct), :]
            accs = list(accs)
            for r in range(ct // 8):
                accs[r % n_acc] = accs[r % n_acc] + jnp.where(tile[8 * r:8 * r + 8, :] >= cand8, 1.0, 0.0)
            return tuple(accs)

        accs = lax.fori_loop(0, n_ct, tile_body, tuple(jnp.zeros((8, nq), F32) for _ in range(n_acc)))
        acc = (accs[0] + accs[1]) + (accs[2] + accs[3])
        return jnp.sum(acc, axis=0, keepdims=True)

    thr = _topk_threshold(count_ge, (1, nq), topk)
    keep_all = (qb * nq + lax.broadcasted_iota(I32, (1, nq), 1)) < topk

    m_scr[...] = jnp.full(m_scr.shape, NEG_INF, F32)
    l_scr[...] = jnp.zeros(l_scr.shape, F32)
    acc_scr[...] = jnp.zeros(acc_scr.shape, F32)
    scale = C_HEAD ** -0.5
    qp = []
    for pr in range(n_pairs):
        c0 = slice(2 * pr * C_HEAD, (2 * pr + 1) * C_HEAD)
        c1 = slice((2 * pr + 1) * C_HEAD, (2 * pr + 2) * C_HEAD)
        qp.append(jnp.concatenate([qc_ref[:, c0] * scale, qc_ref[:, c1] * scale], axis=0).astype(BF16))
    pairs_per_kv = n_pairs // C_KV_HEADS

    def attend_tile(ti, carry):
        k0 = pl.multiple_of(ti * kt, kt)
        sc = sc_scr[pl.ds(k0, kt), :]
        valid = (kpos + k0 <= qpos) & (keep_all | (sc >= thr))
        bias = jnp.where(valid, 0.0, NEG_INF)
        bias2 = jnp.concatenate([bias, bias], axis=1)
        k16 = [k_ref[pl.ds(k0, kt), hk * C_HEAD:(hk + 1) * C_HEAD].astype(BF16) for hk in range(C_KV_HEADS)]
        vt16 = [v_ref[pl.ds(k0, kt), hk * C_HEAD:(hk + 1) * C_HEAD].T.astype(BF16) for hk in range(C_KV_HEADS)]
        pr_all = range(n_pairs)
        s = [_dot_nt(k16[pr // pairs_per_kv], qp[pr]) + bias2 for pr in pr_all]
        m_old = [m_scr[pr] for pr in pr_all]
        m_new = [jnp.maximum(m_old[pr], jnp.max(s[pr], axis=0, keepdims=True)) for pr in pr_all]
        m_safe = [jnp.where(m_new[pr] == NEG_INF, 0.0, m_new[pr]) for pr in pr_all]
        alpha = [jnp.exp(m_old[pr] - m_safe[pr]) for pr in pr_all]
        p = [jnp.exp(s[pr] - m_safe[pr]) for pr in pr_all]
        for pr in pr_all:
            l_scr[pr] = alpha[pr] * l_scr[pr] + jnp.sum(p[pr], axis=0, keepdims=True)
            acc_scr[pr] = alpha[pr] * acc_scr[pr] + _dot(vt16[pr // pairs_per_kv], p[pr].astype(BF16))
            m_scr[pr] = m_new[pr]
        return carry

    lax.fori_loop(0, n_tiles, attend_tile, 0)
    for pr in range(n_pairs):
        o_t = acc_scr[pr] / l_scr[pr]
        for e in range(2):
            cols = slice((2 * pr + e) * C_HEAD, (2 * pr + e + 1) * C_HEAD)
            o = o_t[:, e * nq:(e + 1) * nq].T
            y_ref[:, cols] = (o * _silu(z_ref[:, cols])).astype(y_ref.dtype)


def _dsa_prompt(p3):
    b, l, _ = p3.shape
    assert l % Q_BLOCK == 0
    topk = min(TOPK_MAX, l // 4)
    nq = Q_BLOCK
    kt = DSA_KEY_TILE if l % DSA_KEY_TILE == 0 else 128
    return pl.pallas_call(
        functools.partial(_dsa_prompt_kernel, l=l, topk=topk, kt=kt),
        grid=(b, l // nq),
        in_specs=[pl.BlockSpec((None, nq, W_C), lambda i, j: (i, j, OFF["qi"] // W_C)),
                  pl.BlockSpec((None, nq, 128), lambda i, j: (i, j, OFF_TAIL // 128)),
                  pl.BlockSpec((None, l, 128), lambda i, j: (i, 0, OFF_TAIL // 128)),
                  pl.BlockSpec((None, nq, W_C), lambda i, j: (i, j, OFF["qc"] // W_C)),
                  pl.BlockSpec((None, l, C_KV), lambda i, j: (i, 0, OFF["kc"] // C_KV)),
                  pl.BlockSpec((None, l, C_KV), lambda i, j: (i, 0, OFF["vc"] // C_KV)),
                  pl.BlockSpec((None, nq, W_C), lambda i, j: (i, j, OFF["zc"] // W_C))],
        out_specs=pl.BlockSpec((None, nq, W_C), lambda i, j: (i, j, 0)),
        out_shape=jax.ShapeDtypeStruct((b, l, W_C), BF16),
        scratch_shapes=[pltpu.VMEM((l, nq), F32),
                        pltpu.VMEM((C_HEADS // 2, 1, 2 * nq), F32), pltpu.VMEM((C_HEADS // 2, 1, 2 * nq), F32),
                        pltpu.VMEM((C_HEADS // 2, C_HEAD, 2 * nq), F32)],
        compiler_params=_cp(("arbitrary", "arbitrary"), 48),
        name="dsa_prompt",
    )(p3, p3, p3, p3, p3, p3, p3)


def _idx_scores(logits, wcol, t):
    r = jnp.maximum(logits, 0.0) * wcol
    return jnp.sum(r.reshape(t, IDX_HEADS, logits.shape[1]), axis=1)


def _dsa_s_score_kernel(pt_ref, q_ref, w_ref, *refs, pg, t):
    pages = refs[:pg]
    out_ref = refs[pg]
    q16 = q_ref[...].astype(BF16)
    wcol = w_ref[...] * ((IDX_HEADS ** -0.5) * (IDX_DIM ** -0.5))
    for i in range(pg):
        out_ref[:, i * 128:(i + 1) * 128] = _idx_scores(_dot(q16, pages[i][...].astype(BF16)), wcol, t)


def _dsa_s_select_kernel(sc_ref, q_ref, w_ref, kn_ref, bias_ref, *, t, nk, topk):
    q16 = q_ref[...].astype(BF16)
    wcol = w_ref[...] * ((IDX_HEADS ** -0.5) * (IDX_DIM ** -0.5))
    sc_new = _idx_scores(_dot_nt(q16, kn_ref[...].astype(BF16)), wcol, t)
    tq = lax.broadcasted_iota(I32, (t, 128), 0)
    jn = lax.broadcasted_iota(I32, (t, 128), 1)
    new_ok = jn <= tq
    sc_new = jnp.where(new_ok, sc_new, NEG_INF)
    sc_past = sc_ref[...]

    def count_ge(cand):
        c1 = jnp.sum(jnp.where(sc_past >= cand, 1.0, 0.0), axis=1, keepdims=True)
        c2 = jnp.sum(jnp.where(sc_new >= cand, 1.0, 0.0), axis=1, keepdims=True)
        return c1 + c2

    thr = _topk_threshold(count_ge, (t, 1), topk)
    keep_all = (nk + 1 + lax.broadcasted_iota(I32, (t, 1), 0)) <= topk
    bias_ref[:, :nk] = jnp.where(keep_all | (sc_past >= thr), 0.0, NEG_INF)
    bias_ref[:, nk:] = jnp.where(new_ok & (keep_all | (sc_new >= thr)), 0.0, NEG_INF)


def _dsa_s_attend_kernel(pt_ref, q_ref, bias_ref, biasn_ref, kn_ref, vn_ref, z_ref, *refs, pg):
    kpages = refs[:pg]
    vpages = refs[pg:2 * pg]
    o_ref = refs[2 * pg]
    m_scr, l_scr, acc_scr = refs[2 * pg + 1:]
    g = pl.program_id(1)
    scale = C_HEAD ** -0.5
    page = kpages[0].shape[0] // C_KV_HEADS

    @pl.when(g == 0)
    def _():
        m_scr[...] = jnp.full(m_scr.shape, NEG_INF, F32)
        l_scr[...] = jnp.zeros(l_scr.shape, F32)
        acc_scr[...] = jnp.zeros(acc_scr.shape, F32)
        for hk in range(C_KV_HEADS):
            cols = slice(hk * C_HEAD, (hk + 1) * C_HEAD)
            q16 = (q_ref[hk] * scale).astype(BF16)
            s = _dot_nt(q16, kn_ref[:, cols].astype(BF16)) + biasn_ref[...]
            _softmax_update(m_scr, l_scr, acc_scr, hk, s, [vn_ref[:, cols].astype(BF16)])

    for hk in range(C_KV_HEADS):
        q16 = (q_ref[hk] * scale).astype(BF16)
        rows = pl.ds(hk, page, stride=C_KV_HEADS)
        tiles = [_dot_nt(q16, kpages[i][rows, :].astype(BF16)) for i in range(pg)]
        s = jnp.concatenate(tiles, axis=1) + bias_ref[...]
        _softmax_update(m_scr, l_scr, acc_scr, hk, s, [vpages[i][rows, :].astype(BF16) for i in range(pg)])

    @pl.when(g == pl.num_programs(1) - 1)
    def _():
        for hk in range(C_KV_HEADS):
            o_ref[hk] = (acc_scr[hk] / l_scr[hk]) * _silu(z_ref[hk])


def _dsa_sample(p3s, cache_k, cache_v, cache_kidx, page_table, layer):
    b, t, _ = p3s.shape
    n_pages = page_table.shape[1]
    page = cache_k.shape[2]
    assert page == 128
    past = n_pages * page
    topk = min(TOPK_MAX, (past + t) // 4)
    pg = 16 if n_pages % 16 == 0 else n_pages
    ng = n_pages // pg
    rows = t * C_GROUPS

    qi = p3s[:, :, OFF["qi"]:OFF["qi"] + IDX_HEADS * IDX_DIM].reshape(b, t * IDX_HEADS, IDX_DIM)
    wi = p3s[:, :, OFF_TAIL + TAIL_WI:OFF_TAIL + TAIL_WI + IDX_HEADS].reshape(b, t * IDX_HEADS, 1)
    ki_new = jnp.pad(p3s[:, :, OFF_TAIL:OFF_TAIL + IDX_DIM], ((0, 0), (0, 128 - t), (0, 0)))
    k_new = jnp.pad(p3s[:, :, OFF["kc"]:OFF["kc"] + C_KV], ((0, 0), (0, 128 - t), (0, 0)))
    v_new = jnp.pad(p3s[:, :, OFF["vc"]:OFF["vc"] + C_KV], ((0, 0), (0, 128 - t), (0, 0)))

    def heads_major(a):
        a = a.reshape(b, t, C_KV_HEADS, C_GROUPS, C_HEAD)
        return jnp.transpose(a, (0, 2, 1, 3, 4)).reshape(b, C_KV_HEADS, rows, C_HEAD)

    qh = heads_major(p3s[:, :, OFF["qc"]:OFF["qc"] + W_C])
    zh = heads_major(p3s[:, :, OFF["zc"]:OFF["zc"] + W_C])

    kidx_t = jnp.swapaxes(cache_kidx, 2, 3)
    score_spec = pltpu.PrefetchScalarGridSpec(
        num_scalar_prefetch=1,
        grid=(b, ng),
        in_specs=[pl.BlockSpec((None, t * IDX_HEADS, IDX_DIM), lambda i, g, pt: (i, 0, 0)),
                  pl.BlockSpec((None, t * IDX_HEADS, 1), lambda i, g, pt: (i, 0, 0))]
                 + [pl.BlockSpec((None, None, IDX_DIM, page),
                                 lambda i, g, pt, j=j: (layer, pt[i, g * pg + j], 0, 0)) for j in range(pg)],
        out_specs=pl.BlockSpec((None, t, pg * page), lambda i, g, pt: (i, 0, g)),
    )
    scores = pl.pallas_call(
        functools.partial(_dsa_s_score_kernel, pg=pg, t=t),
        grid_spec=score_spec,
        out_shape=jax.ShapeDtypeStruct((b, t, past), F32),
        compiler_params=_cp(("arbitrary", "arbitrary"), 32),
        name="dsa_s_score",
    )(page_table, qi, wi, *([kidx_t] * pg))

    bias = pl.pallas_call(
        functools.partial(_dsa_s_select_kernel, t=t, nk=past, topk=topk),
        grid=(b,),
        in_specs=[pl.BlockSpec((None, t, past), lambda i: (i, 0, 0)),
                  pl.BlockSpec((None, t * IDX_HEADS, IDX_DIM), lambda i: (i, 0, 0)),
                  pl.BlockSpec((None, t * IDX_HEADS, 1), lambda i: (i, 0, 0)),
                  pl.BlockSpec((None, 128, IDX_DIM), lambda i: (i, 0, 0))],
        out_specs=pl.BlockSpec((None, t, past + 128), lambda i: (i, 0, 0)),
        out_shape=jax.ShapeDtypeStruct((b, t, past + 128), F32),
        compiler_params=_cp(("arbitrary",), 32),
        name="dsa_s_select",
    )(scores, qi, wi, ki_new)

    bias_rows = jnp.repeat(bias, C_GROUPS, axis=1)
    bias_past = bias_rows[:, :, :past]
    bias_new = bias_rows[:, :, past:]

    ck = cache_k.reshape(cache_k.shape[0], cache_k.shape[1], page * C_KV_HEADS, C_HEAD)
    cv = cache_v.reshape(cache_v.shape[0], cache_v.shape[1], page * C_KV_HEADS, C_HEAD)
    pspec = [pl.BlockSpec((None, None, page * C_KV_HEADS, C_HEAD),
                          lambda i, g, pt, j=j: (layer, pt[i, g * pg + j], 0, 0)) for j in range(pg)]
    attend_spec = pltpu.PrefetchScalarGridSpec(
        num_scalar_prefetch=1,
        grid=(b, ng),
        in_specs=[pl.BlockSpec((None, C_KV_HEADS, rows, C_HEAD), lambda i, g, pt: (i, 0, 0, 0)),
                  pl.BlockSpec((None, rows, pg * page), lambda i, g, pt: (i, 0, g)),
                  pl.BlockSpec((None, rows, 128), lambda i, g, pt: (i, 0, 0)),
                  pl.BlockSpec((None, 128, C_KV), lambda i, g, pt: (i, 0, 0)),
                  pl.BlockSpec((None, 128, C_KV), lambda i, g, pt: (i, 0, 0)),
                  pl.BlockSpec((None, C_KV_HEADS, rows, C_HEAD), lambda i, g, pt: (i, 0, 0, 0))]
                 + pspec + pspec,
        out_specs=pl.BlockSpec((None, C_KV_HEADS, rows, C_HEAD), lambda i, g, pt: (i, 0, 0, 0)),
        scratch_shapes=[pltpu.VMEM((C_KV_HEADS, rows, 1), F32), pltpu.VMEM((C_KV_HEADS, rows, 1), F32),
                        pltpu.VMEM((C_KV_HEADS, rows, C_HEAD), F32)],
    )
    oh = pl.pallas_call(
        functools.partial(_dsa_s_attend_kernel, pg=pg),
        grid_spec=attend_spec,
        out_shape=jax.ShapeDtypeStruct((b, C_KV_HEADS, rows, C_HEAD), F32),
        compiler_params=_cp(("arbitrary", "arbitrary"), 40),
        name="dsa_s_attend",
    )(page_table, qh, bias_past, bias_new, k_new, v_new, zh, *([ck] * pg), *([cv] * pg))
    y = jnp.transpose(oh.reshape(b, C_KV_HEADS, t, C_GROUPS, C_HEAD), (0, 2, 1, 3, 4)).reshape(b, t, W_C)
    return y.astype(BF16)


def _outproj_kernel(ya_ref, yb_ref, yc_ref, w_ref, x_ref, gate_ref, o_ref, mix_scr):
    @pl.when(pl.program_id(1) == 0)
    def _():
        mix_scr[:, 0:W_A] = ya_ref[...]
        mix_scr[:, W_A:W_A + W_B] = yb_ref[...]
        mix_scr[:, W_A + W_B:] = yc_ref[...]

    o_ref[...] = x_ref[...] + gate_ref[...] * _dot(mix_scr[...], w_ref[...])


def _outproj(ya, yb, yc, w_bf16, layer, x2, gate, *, rows_per_batch, mod_row0):
    m, d = x2.shape
    tn = 1024
    if gate.ndim == 3:
        tm = 512
        gate_spec = pl.BlockSpec((None, 1, tn),
                                 lambda i, j: ((i * tm) // rows_per_batch + mod_row0, 0, 2 * (d // tn) + j))
    else:
        tm = m
        gate_spec = pl.BlockSpec((tm, tn), lambda i, j: (i, j))
    assert m % tm == 0
    return pl.pallas_call(
        _outproj_kernel,
        grid=(m // tm, d // tn),
        in_specs=[pl.BlockSpec((tm, W_A), lambda i, j: (i, 0)),
                  pl.BlockSpec((tm, W_B), lambda i, j: (i, 0)),
                  pl.BlockSpec((tm, W_C), lambda i, j: (i, 0)),
                  pl.BlockSpec((None, d, tn), lambda i, j: (layer, 0, j)),
                  pl.BlockSpec((tm, tn), lambda i, j: (i, j)),
                  gate_spec],
        out_specs=pl.BlockSpec((tm, tn), lambda i, j: (i, j)),
        out_shape=jax.ShapeDtypeStruct((m, d), F32),
        scratch_shapes=[pltpu.VMEM((tm, d), BF16)],
        compiler_params=_cp(("arbitrary", "arbitrary"), 48),
        name="outproj",
    )(ya, yb, yc, w_bf16, x2, gate)


def _final_norm_kernel(x_ref, g_ref, o_ref):
    x = x_ref[...]
    o_ref[...] = x * lax.rsqrt(jnp.mean(x * x, axis=-1, keepdims=True) + EPS) * g_ref[...]


def _final_norm(x2, g):
    m, d = x2.shape
    tm = 256 if m % 256 == 0 else m
    return pl.pallas_call(
        _final_norm_kernel,
        grid=(m // tm,),
        in_specs=[pl.BlockSpec((tm, d), lambda i: (i, 0)), pl.BlockSpec((1, d), lambda i: (0, 0))],
        out_specs=pl.BlockSpec((tm, d), lambda i: (i, 0)),
        out_shape=jax.ShapeDtypeStruct((m, d), F32),
        compiler_params=_cp(("arbitrary",), 32),
        name="final_norm",
    )(x2, g.reshape(1, d))


def _w_in_tile_table():
    n_t = NP // 128
    table, n_plain = [], None
    for dt in range(n_t - 1):
        c = dt * 128
        seg = [n for n in _DST_ORDER if OFF[n] <= c < OFF[n] + _SRC_OFF[n][1]][0]
        src = _SRC_OFF[seg][0] + (c - OFF[seg])
        if src % 128 == 0:
            assert n_plain is None
        else:
            assert src % 128 == 32
            if n_plain is None:
                n_plain = dt
        table.append(src // 128)
    t_kw, t_ab = _SRC_OFF["ki"][0] // 128, _SRC_OFF["a"][0] // 128
    assert _SRC_OFF["ki"][0] % 128 == 32 and _SRC_OFF["wi"][0] == t_kw * 128 + TAIL_WI
    assert _SRC_OFF["a"][0] % 128 == 0 and _SRC_OFF["b"][0] == _SRC_OFF["a"][0] + DN_HEADS
    assert TAIL_A == IDX_DIM and TAIL_B == TAIL_A + DN_HEADS and TAIL_WI == TAIL_B + DN_HEADS
    tile_a = table + [t_kw]
    tile_b = [t_ab] * n_plain + [s + 1 for s in table[n_plain:]] + [t_ab]
    return tile_a, tile_b, n_plain


def _wprep_kernel(ta_ref, tb_ref, a_ref, b_ref, o_ref, *, n_plain, n_t, rc):
    dt = pl.program_id(1)
    rows = a_ref.shape[0]
    lane = lax.broadcasted_iota(I32, (rc, 128), 1)

    def chunks(fn):
        def body(c, carry):
            r = pl.ds(pl.multiple_of(c * rc, rc), rc)
            o_ref[r, :] = fn(a_ref[r, :], b_ref[r, :]).astype(o_ref.dtype)
            return carry
        lax.fori_loop(0, rows // rc, body, 0)

    @pl.when(dt < n_plain)
    def _():
        chunks(lambda a, b: a)

    @pl.when((dt >= n_plain) & (dt < n_t - 1))
    def _():
        chunks(lambda a, b: jnp.where(lane < 96, pltpu.roll(a, 96, 1), pltpu.roll(b, 96, 1)))

    @pl.when(dt == n_t - 1)
    def _():
        chunks(lambda a, b: jnp.where(lane < TAIL_A, pltpu.roll(a, 96, 1),
                                      jnp.where(lane < TAIL_WI, pltpu.roll(b, TAIL_A, 1),
                                                jnp.where(lane < TAIL_WI + IDX_HEADS, a, 0.0))))


def _permute_w_in(w_in):
    depth, d, _ = w_in.shape
    tile_a, tile_b, n_plain = _w_in_tile_table()
    n_t = NP // 128
    grid_spec = pltpu.PrefetchScalarGridSpec(
        num_scalar_prefetch=2,
        grid=(depth, n_t),
        in_specs=[pl.BlockSpec((None, d, 128), lambda l, t, ta, tb: (l, 0, ta[t])),
                  pl.BlockSpec((None, d, 128), lambda l, t, ta, tb: (l, 0, tb[t]))],
        out_specs=pl.BlockSpec((None, d, 128), lambda l, t, ta, tb: (l, 0, t)),
    )
    return pl.pallas_call(
        functools.partial(_wprep_kernel, n_plain=n_plain, n_t=n_t, rc=512),
        grid_spec=grid_spec,
        out_shape=jax.ShapeDtypeStruct((depth, d, NP), BF16),
        compiler_params=_cp(("arbitrary", "arbitrary"), 32),
        name="w_in_layout",
    )(jnp.asarray(tile_a, I32), jnp.asarray(tile_b, I32), w_in, w_in)


def _delta_t(l):
    for t in (128, 64):
        if l % t == 0:
            return t
    raise ValueError("sequence length must be a multiple of 64")


def kernel(x_prompt, x_sample, cache_k, cache_v, cache_kidx, state_dn, state_conv, page_table,
           c_prompt, c_sample, w_ada, b_ada, g_norm, w_in, a_vnorm, a_ws, a_bs, dn_conv_w,
           dn_a_log, dn_dt_bias, dn_onorm, w_out, g_final):
    bp, lp, d = x_prompt.shape
    bs, ls, _ = x_sample.shape
    depth = w_ada.shape[0]
    assert d == D_MODEL and w_in.shape[2] == D_IN
    assert CONV_K - 1 <= ls <= DN_CHUNK

    n_c = bp + bs
    c_rows = jnp.concatenate([c_prompt, c_sample], axis=0)
    r_pad = (-n_c) % 8
    if r_pad:
        c_rows = jnp.pad(c_rows, ((0, r_pad), (0, 0)))
    m_all = _ada(c_rows, w_ada, b_ada)

    xp = x_prompt.reshape(bp * lp, d)
    xs = x_sample.reshape(bs * ls, d)
    ls_pad = DN_CHUNK
    outs = {k: [] for k in ("pk", "pv", "pki", "pdn", "pconv", "sk", "sv", "ski", "sdn", "sconv", "samlp")}
    zeros_conv = jnp.zeros((bp, CONV_K - 1, DN_CONV_DIM), F32)
    zeros_state = jnp.zeros((bp, DN_HEADS, DN_HEAD, DN_HEAD), F32)

    w_in_all = _permute_w_in(w_in)
    w_out_all = w_out.astype(BF16)
    for l in range(depth):
        g_l = g_norm[l].reshape(1, d)
        m_l = m_all[l]
        m3 = m_l.reshape(m_l.shape[0], 1, 3 * d)
        ms = jnp.repeat(m_l[bp:bp + bs], ls, axis=0)

        pp = _inproj(xp, g_l, m3, m3, w_in_all, l, rows_per_batch=lp, mod_row0=0)
        p3 = pp.reshape(bp, lp, NP)
        (ya,) = _mixa(p3, a_vnorm[l], a_ws[l], a_bs[l], emit_va=False)
        abt = jnp.transpose(p3[:, :, OFF["a"]:OFF["a"] + 2 * DN_HEADS], (0, 2, 1))
        yb, s_p = _delta(p3, abt, dn_conv_w[l], zeros_conv, zeros_state, dn_a_log[l], dn_dt_bias[l],
                         dn_onorm[l], t=_delta_t(lp), valid_len=_delta_t(lp))
        yc = _dsa_prompt(p3)
        xp = _outproj(ya.reshape(bp * lp, W_A), yb.reshape(bp * lp, W_B), yc.reshape(bp * lp, W_C),
                      w_out_all, l, xp, m3, rows_per_batch=lp, mod_row0=0)
        outs["pk"].append(p3[:, :, OFF["kc"]:OFF["kc"] + C_KV].reshape(bp, lp, C_KV_HEADS, C_HEAD))
        outs["pv"].append(p3[:, :, OFF["vc"]:OFF["vc"] + C_KV].reshape(bp, lp, C_KV_HEADS, C_HEAD))
        outs["pki"].append(p3[:, :, OFF_TAIL:OFF_TAIL + IDX_DIM])
        outs["pdn"].append(s_p)
        outs["pconv"].append(p3[:, lp - (CONV_K - 1):, OFF["qkv"]:OFF["qkv"] + DN_CONV_DIM])

        ps = _inproj(xs, g_l, ms[:, d:2 * d], ms[:, 0:d], w_in_all, l, rows_per_batch=ls, mod_row0=bp)
        p3s = ps.reshape(bs, ls, NP)
        ya_s, va_s = _mixa(p3s, a_vnorm[l], a_ws[l], a_bs[l], emit_va=True)
        p3s_pad = jnp.pad(p3s, ((0, 0), (0, ls_pad - ls), (0, 0)))
        abt_s = jnp.transpose(p3s_pad[:, :, OFF["a"]:OFF["a"] + 2 * DN_HEADS], (0, 2, 1))
        yb_s, s_s = _delta(p3s_pad, abt_s, dn_conv_w[l], state_conv[l], state_dn[l], dn_a_log[l],
                           dn_dt_bias[l], dn_onorm[l], t=ls_pad, valid_len=ls)
        yc_s = _dsa_sample(p3s, cache_k, cache_v, cache_kidx, page_table, l)
        xs = _outproj(ya_s.reshape(bs * ls, W_A), yb_s[:, :ls].reshape(bs * ls, W_B),
                      yc_s.reshape(bs * ls, W_C), w_out_all, l, xs, ms[:, 2 * d:3 * d],
                      rows_per_batch=ls, mod_row0=bp)
        outs["sk"].append(p3s[:, :, OFF["kc"]:OFF["kc"] + C_KV].reshape(bs, ls, C_KV_HEADS, C_HEAD))
        outs["sv"].append(p3s[:, :, OFF["vc"]:OFF["vc"] + C_KV].reshape(bs, ls, C_KV_HEADS, C_HEAD))
        outs["ski"].append(p3s[:, :, OFF_TAIL:OFF_TAIL + IDX_DIM])
        outs["sdn"].append(s_s)
        outs["sconv"].append(p3s[:, ls - (CONV_K - 1):, OFF["qkv"]:OFF["qkv"] + DN_CONV_DIM])
        outs["samlp"].append(va_s)

    y_prompt = _final_norm(xp, g_final).reshape(bp, lp, d)
    y_sample = _final_norm(xs, g_final).reshape(bs, ls, d)
    st = jnp.stack
    return (y_prompt, y_sample, st(outs["pk"]), st(outs["pv"]), st(outs["pki"]), st(outs["pdn"]),
            st(outs["pconv"]), st(outs["sk"]), st(outs["sv"]), st(outs["ski"]), st(outs["sdn"]),
            st(outs["sconv"]), st(outs["samlp"]))
```

```python
import functools

import jax
import jax.numpy as jnp
from jax import lax
from jax.experimental import pallas as pl
from jax.experimental.pallas import tpu as pltpu

F32 = jnp.float32
BF16 = jnp.bfloat16
I32 = jnp.int32
EPS = 1e-6
INT_MIN = -(2 ** 31)
NEG_INF = float("-inf")

D_MODEL = 4096
W_A = D_MODEL // 4
A_GROUP = 128
A_HEADS = W_A // A_GROUP
A_CHUNK = 128
W_B = D_MODEL // 2
DN_HEAD = 128
DN_HEADS = W_B // DN_HEAD
CONV_K = 4
DN_CONV_DIM = 3 * W_B
DN_CHUNK = 64
W_C = D_MODEL - W_A - W_B
C_HEAD = 128
C_HEADS = W_C // C_HEAD
C_KV_HEADS = 2
C_GROUPS = C_HEADS // C_KV_HEADS
C_KV = C_KV_HEADS * C_HEAD
IDX_HEADS = 16
IDX_DIM = 64
TOPK_MAX = 256
Q_BLOCK = 128

_SRC_SPLITS = (W_A, W_A, W_A, DN_CONV_DIM, W_B, DN_HEADS, DN_HEADS,
               W_C, C_KV, C_KV, W_C, IDX_HEADS * IDX_DIM, IDX_DIM, IDX_HEADS)
_SRC_NAMES = ("u", "v", "za", "qkv", "zb", "a", "b", "qc", "kc", "vc", "zc", "qi", "ki", "wi")
_SRC_OFF = {}
_o = 0
for _n, _w in zip(_SRC_NAMES, _SRC_SPLITS):
    _SRC_OFF[_n] = (_o, _w)
    _o += _w
D_IN = _o

_DST_ORDER = ("u", "v", "za", "qkv", "zb", "qc", "zc", "qi", "kc", "vc", "ki", "a", "b", "wi")
OFF = {}
_o = 0
for _n in _DST_ORDER:
    OFF[_n] = _o
    _o += _SRC_OFF[_n][1]
NP = ((_o + 127) // 128) * 128
OFF_TAIL = OFF["ki"]
TAIL_A = OFF["a"] - OFF_TAIL
TAIL_B = OFF["b"] - OFF_TAIL
TAIL_WI = OFF["wi"] - OFF_TAIL

INPROJ_TN = 1536
DELTA_HEADS_PER_STEP = 8
DSA_KEY_TILE = 512


def _cp(sem, vmem_mb=48):
    return pltpu.CompilerParams(dimension_semantics=sem, vmem_limit_bytes=vmem_mb * 1024 * 1024)


def _silu(x):
    return (0.5 * x) * (1.0 + jnp.tanh(0.5 * x))


def _gelu(x):
    return 0.5 * x * (1.0 + jnp.tanh(0.7978845608028654 * (x + 0.044715 * (x * x * x))))


def _dot(a, b):
    return jnp.dot(a, b, preferred_element_type=F32)


def _dot_nt(a, b):
    return lax.dot_general(a, b, (((1,), (1,)), ((), ())), preferred_element_type=F32)


def _dot_tn(a, b):
    return lax.dot_general(a, b, (((0,), (0,)), ((), ())), preferred_element_type=F32)


def _split_bf16(a):
    hi = a.astype(BF16)
    lo = (a - hi.astype(F32)).astype(BF16)
    return hi, lo


def _dot3s(a, b):
    ah, al = a
    bh, bl = b
    return _dot(ah, bh) + (_dot(al, bh) + _dot(ah, bl))


def _ada_kernel(c_ref, w_ref, b_ref, o_ref):
    s = _silu(c_ref[...]).astype(BF16)
    o_ref[...] = _dot(s, w_ref[...].astype(BF16)) + b_ref[...]


def _ada(c_rows, w_ada, b_ada):
    depth, d, n = w_ada.shape
    r = c_rows.shape[0]
    tn = 512
    return pl.pallas_call(
        _ada_kernel,
        grid=(depth, n // tn),
        in_specs=[pl.BlockSpec((r, d), lambda l, j: (0, 0)),
                  pl.BlockSpec((None, d, tn), lambda l, j: (l, 0, j)),
                  pl.BlockSpec((None, 1, tn), lambda l, j: (l, 0, j))],
        out_specs=pl.BlockSpec((None, r, tn), lambda l, j: (l, 0, j)),
        out_shape=jax.ShapeDtypeStruct((depth, r, n), F32),
        compiler_params=_cp(("arbitrary", "arbitrary"), 40),
        name="ada",
    )(c_rows, w_ada, b_ada.reshape(depth, 1, n))


def _inproj_kernel(x_ref, g_ref, sc_ref, sh_ref, w_ref, o_ref, h_scr, *, rc):
    @pl.when(pl.program_id(1) == 0)
    def _():
        tm = x_ref.shape[0]
        per_row = sc_ref.shape[0] != 1

        def body(c, carry):
            r = pl.ds(pl.multiple_of(c * rc, rc), rc)
            x = x_ref[r, :]
            y = x * lax.rsqrt(jnp.mean(x * x, axis=-1, keepdims=True) + EPS)
            sc = sc_ref[r, :] if per_row else sc_ref[...]
            sh = sh_ref[r, :] if per_row else sh_ref[...]
            h_scr[r, :] = ((y * g_ref[...]) * (1.0 + sc) + sh).astype(BF16)
            return carry

        lax.fori_loop(0, tm // rc, body, 0)

    o_ref[...] = _dot_nt(h_scr[...], w_ref[...])


def _inproj(x2, g, sc, sh, w_bf16, layer, *, rows_per_batch, mod_row0):
    m, d = x2.shape
    n = w_bf16.shape[1]
    tn = INPROJ_TN
    if sc.ndim == 3:
        tm = 512
        assert rows_per_batch % tm == 0
        sc_spec = pl.BlockSpec((None, 1, d), lambda i, j: ((i * tm) // rows_per_batch + mod_row0, 0, 1))
        sh_spec = pl.BlockSpec((None, 1, d), lambda i, j: ((i * tm) // rows_per_batch + mod_row0, 0, 0))
        rc = 64
    else:
        tm = m
        sc_spec = pl.BlockSpec((tm, d), lambda i, j: (i, 0))
        sh_spec = pl.BlockSpec((tm, d), lambda i, j: (i, 0))
        rc = tm
    assert m % tm == 0
    return pl.pallas_call(
        functools.partial(_inproj_kernel, rc=rc),
        grid=(m // tm, pl.cdiv(n, tn)),
        in_specs=[pl.BlockSpec((tm, d), lambda i, j: (i, 0)),
                  pl.BlockSpec((1, d), lambda i, j: (0, 0)),
                  sc_spec, sh_spec,
                  pl.BlockSpec((None, tn, d), lambda i, j: (layer, j, 0))],
        out_specs=pl.BlockSpec((tm, tn), lambda i, j: (i, j)),
        out_shape=jax.ShapeDtypeStruct((m, n), F32),
        scratch_shapes=[pltpu.VMEM((tm, d), BF16)],
        compiler_params=_cp(("arbitrary", "arbitrary"), 56),
        name="inproj",
    )(x2, g, sc, sh, w_bf16)


def _mixa_kernel(u_ref, v_ref, z_ref, vn_ref, ws_ref, bst_ref, y_ref, *rest, c, emit_va):
    u = _gelu(u_ref[...])
    v = _gelu(v_ref[...])
    mu = jnp.mean(v, axis=-1, keepdims=True)
    dv = v - mu
    va = dv * lax.rsqrt(jnp.mean(dv * dv, axis=-1, keepdims=True) + EPS) * vn_ref[...]
    if emit_va:
        rest[0][...] = va
    z = _silu(z_ref[...])
    row = lax.broadcasted_iota(I32, (c, c), 0)
    col = lax.broadcasted_iota(I32, (c, c), 1)
    tril = col <= row
    for h in range(A_HEADS):
        cols = slice(h * A_GROUP, (h + 1) * A_GROUP)
        wm = jnp.where(tril, ws_ref[h], 0.0)
        vh = va[:, cols]
        if c >= 128:
            mixed = _dot(wm.astype(BF16), vh.astype(BF16))
        else:
            mixed = wm[:, 0:1] * vh[0:1, :]
            for s in range(1, c):
                mixed = mixed + wm[:, s:s + 1] * vh[s:s + 1, :]
        mixed = mixed + bst_ref[:, h:h + 1]
        y_ref[:, cols] = (u[:, cols] * mixed * z[:, cols]).astype(y_ref.dtype)


def _mixa(p3, a_vnorm, a_ws, a_bs, *, emit_va):
    b, l, _ = p3.shape
    c = min(A_CHUNK, l)
    n = l // c
    ws = a_ws[:, :c, :c]
    bst = a_bs[:, :c].T
    wblk = W_A
    outs = [jax.ShapeDtypeStruct((b, l, W_A), BF16)]
    out_specs = [pl.BlockSpec((None, c, W_A), lambda i, j: (i, j, 0))]
    if emit_va:
        outs.append(jax.ShapeDtypeStruct((b, l, W_A), F32))
        out_specs.append(pl.BlockSpec((None, c, W_A), lambda i, j: (i, j, 0)))
    res = pl.pallas_call(
        functools.partial(_mixa_kernel, c=c, emit_va=emit_va),
        grid=(b, n),
        in_specs=[pl.BlockSpec((None, c, wblk), lambda i, j: (i, j, OFF["u"] // wblk)),
                  pl.BlockSpec((None, c, wblk), lambda i, j: (i, j, OFF["v"] // wblk)),
                  pl.BlockSpec((None, c, wblk), lambda i, j: (i, j, OFF["za"] // wblk)),
                  pl.BlockSpec((1, W_A), lambda i, j: (0, 0)),
                  pl.BlockSpec((A_HEADS, c, c), lambda i, j: (0, 0, 0)),
                  pl.BlockSpec((c, A_HEADS), lambda i, j: (0, 0))],
        out_specs=out_specs,
        out_shape=outs,
        compiler_params=_cp(("arbitrary", "arbitrary"), 32),
        name="mixa",
    )(p3, p3, p3, a_vnorm.reshape(1, W_A), ws, bst)
    return res


def _delta_kernel(alog_ref, dtb_ref,
                  q_ref, k_ref, v_ref, z_ref, ab_ref,
                  cwq_ref, cwk_ref, cwv_ref, cpq_ref, cpk_ref, cpv_ref,
                  s0_ref, on_ref,
                  y_ref, sout_ref,
                  xbuf, s_scr, *, t, c, hp, valid_len):
    hg = pl.program_id(1)
    n = pl.program_id(2)
    nlast = pl.num_programs(2) - 1
    dh = DN_HEAD

    @pl.when(n == 0)
    def _():
        s_scr[...] = s0_ref[...]
        for j in range(hp):
            lanes = slice(j * dh, (j + 1) * dh)
            xbuf[3 * j + 0, 5:8, :] = cpq_ref[:, lanes]
            xbuf[3 * j + 1, 5:8, :] = cpk_ref[:, lanes]
            xbuf[3 * j + 2, 5:8, :] = cpv_ref[:, lanes]

    row = lax.broadcasted_iota(I32, (t, t), 0)
    col = lax.broadcasted_iota(I32, (t, t), 1)
    shift = c.bit_length() - 1
    same = (row >> shift) == (col >> shift)
    eye = row == col
    incl = same & (col <= row)
    strict = same & (col < row)
    incl_t = same & (row <= col)
    blk8 = (row >> 3) == (col >> 3)
    off_masks = []
    bs = 8
    while bs < c:
        sh_b = bs.bit_length() - 1
        inner = (row >> sh_b) == (col >> sh_b)
        outer = (row >> (sh_b + 1)) == (col >> (sh_b + 1))
        off_masks.append(outer & jnp.logical_not(inner))
        bs *= 2
    eye_f = jnp.where(eye, 1.0, 0.0)
    if valid_len < t:
        lane_valid = lax.broadcasted_iota(I32, (1, t), 1) < valid_len
        sub_valid = lax.broadcasted_iota(I32, (t, 1), 0) < valid_len

    def to_col(r):
        return jnp.sum(jnp.where(eye, r, 0.0), axis=1, keepdims=True)

    def conv(idx, x, w_ref, lanes):
        xbuf[idx, 8:8 + t, :] = x
        y = xbuf[idx, pl.ds(5, t), :] * w_ref[0:1, lanes]
        for jj in range(1, CONV_K):
            y = y + xbuf[idx, pl.ds(5 + jj, t), :] * w_ref[jj:jj + 1, lanes]
        xbuf[idx, 5:8, :] = x[t - 3:t, :]
        return _silu(y)

    heads = range(hp)
    lanes_of = [slice(j * dh, (j + 1) * dh) for j in heads]

    def per_head(f, *lists):
        return [f(*vals) for vals in zip(*lists)]

    q = [conv(3 * j + 0, q_ref[:, lanes_of[j]], cwq_ref, lanes_of[j]) for j in heads]
    k = [conv(3 * j + 1, k_ref[:, lanes_of[j]], cwk_ref, lanes_of[j]) for j in heads]
    v = [conv(3 * j + 2, v_ref[:, lanes_of[j]], cwv_ref, lanes_of[j]) for j in heads]
    qc = per_head(lambda a: a * lax.rsqrt(jnp.sum(a * a, axis=-1, keepdims=True) + EPS) * (dh ** -0.5), q)
    kc = per_head(lambda a: a * lax.rsqrt(jnp.sum(a * a, axis=-1, keepdims=True) + EPS), k)

    def gates(j):
        h = hg * hp + j
        a_row = ab_ref[pl.ds(h, 1), :]
        b_row = ab_ref[pl.ds(DN_HEADS + h, 1), :]
        xa = a_row + dtb_ref[h]
        softplus = jnp.maximum(xa, 0.0) + jnp.log(1.0 + jnp.exp(-jnp.abs(xa)))
        a_coef = jnp.exp(jnp.zeros((1, 1), F32) + alog_ref[h])
        g_row = -a_coef * softplus
        beta_row = 1.0 / (1.0 + jnp.exp(-b_row))
        if valid_len < t:
            g_row = jnp.where(lane_valid, g_row, 0.0)
            beta_row = jnp.where(lane_valid, beta_row, 0.0)
        return g_row, beta_row

    g_row, beta_row = zip(*[gates(j) for j in heads])
    if valid_len < t:
        kc = per_head(lambda a: jnp.where(sub_valid, a, 0.0), kc)
        v = per_head(lambda a: jnp.where(sub_valid, a, 0.0), v)

    g_col = per_head(to_col, g_row)
    beta_col = per_head(to_col, beta_row)
    gc_col = per_head(lambda r: jnp.sum(jnp.where(incl, r, 0.0), axis=1, keepdims=True), g_row)
    glast_col = per_head(lambda r: jnp.sum(jnp.where(same, r, 0.0), axis=1, keepdims=True), g_row)
    gc_row = per_head(lambda cl: jnp.sum(jnp.where(incl_t, cl, 0.0), axis=0, keepdims=True), g_col)
    decay = per_head(lambda gc, gr: jnp.where(incl, jnp.exp(jnp.where(incl, gc - gr, 0.0)), 0.0), gc_col, gc_row)
    kb = per_head(lambda a, b: a * b, kc, beta_col)
    kc16 = per_head(lambda a: a.astype(BF16), kc)
    a_mat = per_head(lambda a, b, dc: jnp.where(strict, _dot_nt(a.astype(BF16), b) * dc, 0.0), kb, kc16, decay)
    attn = per_head(lambda a, b, dc: (_dot_nt(a.astype(BF16), b) * dc).astype(BF16), qc, kc16, decay)
    eg = per_head(jnp.exp, gc_col)
    rhs = per_head(lambda vv, bc, kk, e: _split_bf16(jnp.concatenate([vv * bc, kk * e], axis=1)),
                   v, beta_col, kb, eg)

    n0f = per_head(lambda a: jnp.where(blk8, -a, 0.0), a_mat)
    n0 = per_head(_split_bf16, n0f)
    n2 = per_head(lambda a: _split_bf16(_dot3s(a, a)), n0)
    n4 = per_head(lambda a: _split_bf16(_dot3s(a, a)), n2)
    x = per_head(lambda a: eye_f + a, n0f)
    x = per_head(lambda xx, nn: xx + _dot3s(_split_bf16(xx), nn), x, n2)
    x = per_head(lambda xx, nn: xx + _dot3s(_split_bf16(xx), nn), x, n4)
    for om in off_masks:
        xs = per_head(_split_bf16, x)
        xa_off = per_head(lambda s_, a: _split_bf16(_dot3s(s_, _split_bf16(jnp.where(om, a, 0.0)))), xs, a_mat)
        x = per_head(lambda xx, xo, s_: xx - _dot3s(xo, s_), x, xa_off, xs)
    sol16 = per_head(lambda xx, r: _dot3s(_split_bf16(xx), r).astype(BF16), x, rhs)
    auw = per_head(_dot, attn, sol16)
    qw = per_head(lambda a, e, m_: (a * e - m_[:, dh:]).astype(BF16), qc, eg, auw)
    kg16 = per_head(lambda a, gl_, gc: (a * jnp.exp(gl_ - gc)).astype(BF16), kc, glast_col, gc_col)

    s = [s_scr[j] for j in heads]
    outs = [[] for _ in heads]
    for i in range(t // c):
        rows = slice(i * c, (i + 1) * c)
        s16 = per_head(lambda a: a.astype(BF16), s)
        kuw = per_head(lambda a, b: _dot_tn(a[rows], b[rows]), kg16, sol16)
        for j in heads:
            outs[j].append(_dot(qw[j][rows], s16[j]) + auw[j][rows, :dh])
        gl = per_head(lambda a: jnp.exp(a[i * c:i * c + 1, :]), glast_col)
        s = per_head(lambda g_, s_, m_, b16: g_ * s_ + m_[:, :dh] - _dot(m_[:, dh:].astype(BF16), b16),
                     gl, s, kuw, s16)
    for j in heads:
        s_scr[j] = s[j]
        o = outs[j][0] if len(outs[j]) == 1 else jnp.concatenate(outs[j], axis=0)
        on = o * lax.rsqrt(jnp.mean(o * o, axis=-1, keepdims=True) + EPS) * on_ref[...]
        y_ref[:, lanes_of[j]] = (on * _silu(z_ref[:, lanes_of[j]])).astype(y_ref.dtype)

    @pl.when(n == nlast)
    def _():
        sout_ref[...] = s_scr[...]


def _delta(p3, abt, conv_w, conv_prev, s0, a_log, dt_bias, onorm, *, t, valid_len):
    b, l, _ = p3.shape
    assert l % t == 0
    c = min(DN_CHUNK, t)
    hp = DELTA_HEADS_PER_STEP
    nh = DN_HEADS
    ng = nh // hp
    w = 128 * hp
    cb_q = OFF["qkv"] // w
    cb_z = OFF["zb"] // w
    assert OFF["qkv"] % w == 0 and OFF["zb"] % w == 0 and W_B % w == 0

    def pspec(cb0):
        return pl.BlockSpec((None, t, w), lambda i, h, n, a, d: (i, n, cb0 + h))

    def cwspec(sidx):
        return pl.BlockSpec((CONV_K, w), lambda i, h, n, a, d: (0, sidx * ng + h))

    def cpspec(sidx):
        return pl.BlockSpec((None, CONV_K - 1, w), lambda i, h, n, a, d: (i, 0, sidx * ng + h))

    grid_spec = pltpu.PrefetchScalarGridSpec(
        num_scalar_prefetch=2,
        grid=(b, ng, l // t),
        in_specs=[pspec(cb_q), pspec(cb_q + ng), pspec(cb_q + 2 * ng), pspec(cb_z),
                  pl.BlockSpec((None, 2 * nh, t), lambda i, h, n, a, d: (i, 0, n)),
                  cwspec(0), cwspec(1), cwspec(2), cpspec(0), cpspec(1), cpspec(2),
                  pl.BlockSpec((None, hp, DN_HEAD, DN_HEAD), lambda i, h, n, a, d: (i, h, 0, 0)),
                  pl.BlockSpec((1, DN_HEAD), lambda i, h, n, a, d: (0, 0))],
        out_specs=[pl.BlockSpec((None, t, w), lambda i, h, n, a, d: (i, n, h)),
                   pl.BlockSpec((None, hp, DN_HEAD, DN_HEAD), lambda i, h, n, a, d: (i, h, 0, 0))],
        scratch_shapes=[pltpu.VMEM((3 * hp, t + 8, 128), F32), pltpu.VMEM((hp, DN_HEAD, DN_HEAD), F32)],
    )
    return pl.pallas_call(
        functools.partial(_delta_kernel, t=t, c=c, hp=hp, valid_len=valid_len),
        grid_spec=grid_spec,
        out_shape=[jax.ShapeDtypeStruct((b, l, W_B), BF16),
                   jax.ShapeDtypeStruct((b, nh, DN_HEAD, DN_HEAD), F32)],
        compiler_params=_cp(("arbitrary", "arbitrary", "arbitrary"), 40),
        name="delta",
    )(a_log, dt_bias, p3, p3, p3, p3, abt, conv_w, conv_w, conv_w,
      conv_prev, conv_prev, conv_prev, s0, onorm.reshape(1, DN_HEAD))


def _ordered_bits_to_float(u):
    key = u ^ jnp.int32(INT_MIN)
    bits = jnp.where(key < 0, key ^ jnp.int32(0x7FFFFFFF), key)
    return pltpu.bitcast(bits, F32)


def _topk_threshold(count_ge, shape, topk):
    def bit_body(i, u):
        bit = jnp.left_shift(jnp.int32(1), 31 - i)
        cand_u = u | bit
        cnt = count_ge(_ordered_bits_to_float(cand_u))
        return jnp.where(cnt >= float(topk), cand_u, u)

    u = lax.fori_loop(0, 32, bit_body, jnp.zeros(shape, I32))
    return _ordered_bits_to_float(u)


def _softmax_update(m_ref, l_ref, acc_ref, idx, s, v_tiles):
    m_old = m_ref[idx]
    m_new = jnp.maximum(m_old, jnp.max(s, axis=1, keepdims=True))
    m_safe = jnp.where(m_new == NEG_INF, 0.0, m_new)
    alpha = jnp.exp(m_old - m_safe)
    p = jnp.exp(s - m_safe)
    l_ref[idx] = alpha * l_ref[idx] + jnp.sum(p, axis=1, keepdims=True)
    acc = alpha * acc_ref[idx]
    p16 = p.astype(BF16)
    for i, vt in enumerate(v_tiles):
        acc = acc + _dot(p16[:, i * 128:(i + 1) * 128], vt)
    acc_ref[idx] = acc
    m_ref[idx] = m_new


def _dsa_prompt_kernel(qi_ref, tq_ref, tall_ref, qc_ref, k_ref, v_ref, z_ref, y_ref,
                       sc_scr, m_scr, l_scr, acc_scr, *, l, topk, kt):
    qb = pl.program_id(1)
    nq = Q_BLOCK
    n_tiles = (qb * nq + nq + kt - 1) // kt
    n_pairs = C_HEADS // 2
    w_rows = tq_ref[...].T[TAIL_WI:TAIL_WI + IDX_HEADS, :] * ((IDX_HEADS ** -0.5) * (IDX_DIM ** -0.5))
    rq = []
    for j in range(IDX_HEADS // 2):
        a = qi_ref[:, j * 128:(j + 1) * 128]
        rq.append(jnp.concatenate([a, pltpu.roll(a, IDX_DIM, 1)], axis=0).astype(BF16))
    kpos = lax.broadcasted_iota(I32, (kt, nq), 0)
    qpos = qb * nq + lax.broadcasted_iota(I32, (kt, nq), 1)
    lane = lax.broadcasted_iota(I32, (kt, 128), 1)

    def score_tile(ti, carry):
        k0 = pl.multiple_of(ti * kt, kt)
        ki = jnp.where(lane < IDX_DIM, tall_ref[pl.ds(k0, kt), :], 0.0).astype(BF16)
        score = jnp.zeros((kt, nq), F32)
        for j in range(IDX_HEADS // 2):
            lg = _dot_nt(ki, rq[j])
            score = score + w_rows[2 * j:2 * j + 1, :] * jnp.maximum(lg[:, :nq], 0.0)
            score = score + w_rows[2 * j + 1:2 * j + 2, :] * jnp.maximum(lg[:, nq:], 0.0)
        sc_scr[pl.ds(k0, kt), :] = jnp.where(kpos + k0 <= qpos, score, NEG_INF)
        return carry

    lax.fori_loop(0, n_tiles, score_tile, 0)

    ct = 256
    n_ct = (qb * nq + nq + ct - 1) // ct
    n_acc = 4

    def count_ge(cand):
        cand8 = jnp.broadcast_to(cand, (8, nq))

        def tile_body(ti, accs):
            k0 = pl.multiple_of(ti * ct, ct)
            tile = sc_scr[pl.ds(k0, ct), :]
            accs = list(accs)
            for r in range(ct // 8):
                accs[r % n_acc] = accs[r % n_acc] + jnp.where(tile[8 * r:8 * r + 8, :] >= cand8, 1.0, 0.0)
            return tuple(accs)

        accs = lax.fori_loop(0, n_ct, tile_body, tuple(jnp.zeros((8, nq), F32) for _ in range(n_acc)))
        acc = (accs[0] + accs[1]) + (accs[2] + accs[3])
        return jnp.sum(acc, axis=0, keepdims=True)

    thr = _topk_threshold(count_ge, (1, nq), topk)
    keep_all = (qb * nq + lax.broadcasted_iota(I32, (1, nq), 1)) < topk

    m_scr[...] = jnp.full(m_scr.shape, NEG_INF, F32)
    l_scr[...] = jnp.zeros(l_scr.shape, F32)
    acc_scr[...] = jnp.zeros(acc_scr.shape, F32)
    scale = C_HEAD ** -0.5
    qp = []
    for pr in range(n_pairs):
        c0 = slice(2 * pr * C_HEAD, (2 * pr + 1) * C_HEAD)
        c1 = slice((2 * pr + 1) * C_HEAD, (2 * pr + 2) * C_HEAD)
        qp.append(jnp.concatenate([qc_ref[:, c0] * scale, qc_ref[:, c1] * scale], axis=0).astype(BF16))
    pairs_per_kv = n_pairs // C_KV_HEADS

    def attend_tile(ti, carry):
        k0 = pl.multiple_of(ti * kt, kt)
        sc = sc_scr[pl.ds(k0, kt), :]
        valid = (kpos + k0 <= qpos) & (keep_all | (sc >= thr))
        bias = jnp.where(valid, 0.0, NEG_INF)
        bias2 = jnp.concatenate([bias, bias], axis=1)
        k16 = [k_ref[pl.ds(k0, kt), hk * C_HEAD:(hk + 1) * C_HEAD].astype(BF16) for hk in range(C_KV_HEADS)]
        vt16 = [v_ref[pl.ds(k0, kt), hk * C_HEAD:(hk + 1) * C_HEAD].T.astype(BF16) for hk in range(C_KV_HEADS)]
        pr_all = range(n_pairs)
        s = [_dot_nt(k16[pr // pairs_per_kv], qp[pr]) + bias2 for pr in pr_all]
        m_old = [m_scr[pr] for pr in pr_all]
        m_new = [jnp.maximum(m_old[pr], jnp.max(s[pr], axis=0, keepdims=True)) for pr in pr_all]
        m_safe = [jnp.where(m_new[pr] == NEG_INF, 0.0, m_new[pr]) for pr in pr_all]
        alpha = [jnp.exp(m_old[pr] - m_safe[pr]) for pr in pr_all]
        p = [jnp.exp(s[pr] - m_safe[pr]) for pr in pr_all]
        for pr in pr_all:
            l_scr[pr] = alpha[pr] * l_scr[pr] + jnp.sum(p[pr], axis=0, keepdims=True)
            acc_scr[pr] = alpha[pr] * acc_scr[pr] + _dot(vt16[pr // pairs_per_kv], p[pr].astype(BF16))
            m_scr[pr] = m_new[pr]
        return carry

    lax.fori_loop(0, n_tiles, attend_tile, 0)
    for pr in range(n_pairs):
        o_t = acc_scr[pr] / l_scr[pr]
        for e in range(2):
            cols = slice((2 * pr + e) * C_HEAD, (2 * pr + e + 1) * C_HEAD)
            o = o_t[:, e * nq:(e + 1) * nq].T
            y_ref[:, cols] = (o * _silu(z_ref[:, cols])).astype(y_ref.dtype)


def _dsa_prompt(p3):
    b, l, _ = p3.shape
    assert l % Q_BLOCK == 0
    topk = min(TOPK_MAX, l // 4)
    nq = Q_BLOCK
    kt = DSA_KEY_TILE if l % DSA_KEY_TILE == 0 else 128
    return pl.pallas_call(
        functools.partial(_dsa_prompt_kernel, l=l, topk=topk, kt=kt),
        grid=(b, l // nq),
        in_specs=[pl.BlockSpec((None, nq, W_C), lambda i, j: (i, j, OFF["qi"] // W_C)),
                  pl.BlockSpec((None, nq, 128), lambda i, j: (i, j, OFF_TAIL // 128)),
                  pl.BlockSpec((None, l, 128), lambda i, j: (i, 0, OFF_TAIL // 128)),
                  pl.BlockSpec((None, nq, W_C), lambda i, j: (i, j, OFF["qc"] // W_C)),
                  pl.BlockSpec((None, l, C_KV), lambda i, j: (i, 0, OFF["kc"] // C_KV)),
                  pl.BlockSpec((None, l, C_KV), lambda i, j: (i, 0, OFF["vc"] // C_KV)),
                  pl.BlockSpec((None, nq, W_C), lambda i, j: (i, j, OFF["zc"] // W_C))],
        out_specs=pl.BlockSpec((None, nq, W_C), lambda i, j: (i, j, 0)),
        out_shape=jax.ShapeDtypeStruct((b, l, W_C), BF16),
        scratch_shapes=[pltpu.VMEM((l, nq), F32),
                        pltpu.VMEM((C_HEADS // 2, 1, 2 * nq), F32), pltpu.VMEM((C_HEADS // 2, 1, 2 * nq), F32),
                        pltpu.VMEM((C_HEADS // 2, C_HEAD, 2 * nq), F32)],
        compiler_params=_cp(("arbitrary", "arbitrary"), 48),
        name="dsa_prompt",
    )(p3, p3, p3, p3, p3, p3, p3)


def _idx_scores(logits, wcol, t):
    r = jnp.maximum(logits, 0.0) * wcol
    return jnp.sum(r.reshape(t, IDX_HEADS, logits.shape[1]), axis=1)


def _dsa_s_score_kernel(pt_ref, q_ref, w_ref, *refs, pg, t):
    pages = refs[:pg]
    out_ref = refs[pg]
    q16 = q_ref[...].astype(BF16)
    wcol = w_ref[...] * ((IDX_HEADS ** -0.5) * (IDX_DIM ** -0.5))
    for i in range(pg):
        out_ref[:, i * 128:(i + 1) * 128] = _idx_scores(_dot(q16, pages[i][...].astype(BF16)), wcol, t)


def _dsa_s_select_kernel(sc_ref, q_ref, w_ref, kn_ref, bias_ref, *, t, nk, topk):
    q16 = q_ref[...].astype(BF16)
    wcol = w_ref[...] * ((IDX_HEADS ** -0.5) * (IDX_DIM ** -0.5))
    sc_new = _idx_scores(_dot_nt(q16, kn_ref[...].astype(BF16)), wcol, t)
    tq = lax.broadcasted_iota(I32, (t, 128), 0)
    jn = lax.broadcasted_iota(I32, (t, 128), 1)
    new_ok = jn <= tq
    sc_new = jnp.where(new_ok, sc_new, NEG_INF)
    sc_past = sc_ref[...]

    def count_ge(cand):
        c1 = jnp.sum(jnp.where(sc_past >= cand, 1.0, 0.0), axis=1, keepdims=True)
        c2 = jnp.sum(jnp.where(sc_new >= cand, 1.0, 0.0), axis=1, keepdims=True)
        return c1 + c2

    thr = _topk_threshold(count_ge, (t, 1), topk)
    keep_all = (nk + 1 + lax.broadcasted_iota(I32, (t, 1), 0)) <= topk
    bias_ref[:, :nk] = jnp.where(keep_all | (sc_past >= thr), 0.0, NEG_INF)
    bias_ref[:, nk:] = jnp.where(new_ok & (keep_all | (sc_new >= thr)), 0.0, NEG_INF)


def _dsa_s_attend_kernel(pt_ref, q_ref, bias_ref, biasn_ref, kn_ref, vn_ref, z_ref, *refs, pg):
    kpages = refs[:pg]
    vpages = refs[pg:2 * pg]
    o_ref = refs[2 * pg]
    m_scr, l_scr, acc_scr = refs[2 * pg + 1:]
    g = pl.program_id(1)
    scale = C_HEAD ** -0.5
    page = kpages[0].shape[0] // C_KV_HEADS

    @pl.when(g == 0)
    def _():
        m_scr[...] = jnp.full(m_scr.shape, NEG_INF, F32)
        l_scr[...] = jnp.zeros(l_scr.shape, F32)
        acc_scr[...] = jnp.zeros(acc_scr.shape, F32)
        for hk in range(C_KV_HEADS):
            cols = slice(hk * C_HEAD, (hk + 1) * C_HEAD)
            q16 = (q_ref[hk] * scale).astype(BF16)
            s = _dot_nt(q16, kn_ref[:, cols].astype(BF16)) + biasn_ref[...]
            _softmax_update(m_scr, l_scr, acc_scr, hk, s, [vn_ref[:, cols].astype(BF16)])

    for hk in range(C_KV_HEADS):
        q16 = (q_ref[hk] * scale).astype(BF16)
        rows = pl.ds(hk, page, stride=C_KV_HEADS)
        tiles = [_dot_nt(q16, kpages[i][rows, :].astype(BF16)) for i in range(pg)]
        s = jnp.concatenate(tiles, axis=1) + bias_ref[...]
        _softmax_update(m_scr, l_scr, acc_scr, hk, s, [vpages[i][rows, :].astype(BF16) for i in range(pg)])

    @pl.when(g == pl.num_programs(1) - 1)
    def _():
        for hk in range(C_KV_HEADS):
            o_ref[hk] = (acc_scr[hk] / l_scr[hk]) * _silu(z_ref[hk])


def _dsa_sample(p3s, cache_k, cache_v, cache_kidx, page_table, layer):
    b, t, _ = p3s.shape
    n_pages = page_table.shape[1]
    page = cache_k.shape[2]
    assert page == 128
    past = n_pages * page
    topk = min(TOPK_MAX, (past + t) // 4)
    pg = 16 if n_pages % 16 == 0 else n_pages
    ng = n_pages // pg
    rows = t * C_GROUPS

    qi = p3s[:, :, OFF["qi"]:OFF["qi"] + IDX_HEADS * IDX_DIM].reshape(b, t * IDX_HEADS, IDX_DIM)
    wi = p3s[:, :, OFF_TAIL + TAIL_WI:OFF_TAIL + TAIL_WI + IDX_HEADS].reshape(b, t * IDX_HEADS, 1)
    ki_new = jnp.pad(p3s[:, :, OFF_TAIL:OFF_TAIL + IDX_DIM], ((0, 0), (0, 128 - t), (0, 0)))
    k_new = jnp.pad(p3s[:, :, OFF["kc"]:OFF["kc"] + C_KV], ((0, 0), (0, 128 - t), (0, 0)))
    v_new = jnp.pad(p3s[:, :, OFF["vc"]:OFF["vc"] + C_KV], ((0, 0), (0, 128 - t), (0, 0)))

    def heads_major(a):
        a = a.reshape(b, t, C_KV_HEADS, C_GROUPS, C_HEAD)
        return jnp.transpose(a, (0, 2, 1, 3, 4)).reshape(b, C_KV_HEADS, rows, C_HEAD)

    qh = heads_major(p3s[:, :, OFF["qc"]:OFF["qc"] + W_C])
    zh = heads_major(p3s[:, :, OFF["zc"]:OFF["zc"] + W_C])

    kidx_t = jnp.swapaxes(cache_kidx, 2, 3)
    score_spec = pltpu.PrefetchScalarGridSpec(
        num_scalar_prefetch=1,
        grid=(b, ng),
        in_specs=[pl.BlockSpec((None, t * IDX_HEADS, IDX_DIM), lambda i, g, pt: (i, 0, 0)),
                  pl.BlockSpec((None, t * IDX_HEADS, 1), lambda i, g, pt: (i, 0, 0))]
                 + [pl.BlockSpec((None, None, IDX_DIM, page),
                                 lambda i, g, pt, j=j: (layer, pt[i, g * pg + j], 0, 0)) for j in range(pg)],
        out_specs=pl.BlockSpec((None, t, pg * page), lambda i, g, pt: (i, 0, g)),
    )
    scores = pl.pallas_call(
        functools.partial(_dsa_s_score_kernel, pg=pg, t=t),
        grid_spec=score_spec,
        out_shape=jax.ShapeDtypeStruct((b, t, past), F32),
        compiler_params=_cp(("arbitrary", "arbitrary"), 32),
        name="dsa_s_score",
    )(page_table, qi, wi, *([kidx_t] * pg))

    bias = pl.pallas_call(
        functools.partial(_dsa_s_select_kernel, t=t, nk=past, topk=topk),
        grid=(b,),
        in_specs=[pl.BlockSpec((None, t, past), lambda i: (i, 0, 0)),
                  pl.BlockSpec((None, t * IDX_HEADS, IDX_DIM), lambda i: (i, 0, 0)),
                  pl.BlockSpec((None, t * IDX_HEADS, 1), lambda i: (i, 0, 0)),
                  pl.BlockSpec((None, 128, IDX_DIM), lambda i: (i, 0, 0))],
        out_specs=pl.BlockSpec((None, t, past + 128), lambda i: (i, 0, 0)),
        out_shape=jax.ShapeDtypeStruct((b, t, past + 128), F32),
        compiler_params=_cp(("arbitrary",), 32),
        name="dsa_s_select",
    )(scores, qi, wi, ki_new)

    bias_rows = jnp.repeat(bias, C_GROUPS, axis=1)
    bias_past = bias_rows[:, :, :past]
    bias_new = bias_rows[:, :, past:]

    ck = cache_k.reshape(cache_k.shape[0], cache_k.shape[1], page * C_KV_HEADS, C_HEAD)
    cv = cache_v.reshape(cache_v.shape[0], cache_v.shape[1], page * C_KV_HEADS, C_HEAD)
    pspec = [pl.BlockSpec((None, None, page * C_KV_HEADS, C_HEAD),
                          lambda i, g, pt, j=j: (layer, pt[i, g * pg + j], 0, 0)) for j in range(pg)]
    attend_spec = pltpu.PrefetchScalarGridSpec(
        num_scalar_prefetch=1,
        grid=(b, ng),
        in_specs=[pl.BlockSpec((None, C_KV_HEADS, rows, C_HEAD), lambda i, g, pt: (i, 0, 0, 0)),
                  pl.BlockSpec((None, rows, pg * page), lambda i, g, pt: (i, 0, g)),
                  pl.BlockSpec((None, rows, 128), lambda i, g, pt: (i, 0, 0)),
                  pl.BlockSpec((None, 128, C_KV), lambda i, g, pt: (i, 0, 0)),
                  pl.BlockSpec((None, 128, C_KV), lambda i, g, pt: (i, 0, 0)),
                  pl.BlockSpec((None, C_KV_HEADS, rows, C_HEAD), lambda i, g, pt: (i, 0, 0, 0))]
                 + pspec + pspec,
        out_specs=pl.BlockSpec((None, C_KV_HEADS, rows, C_HEAD), lambda i, g, pt: (i, 0, 0, 0)),
        scratch_shapes=[pltpu.VMEM((C_KV_HEADS, rows, 1), F32), pltpu.VMEM((C_KV_HEADS, rows, 1), F32),
                        pltpu.VMEM((C_KV_HEADS, rows, C_HEAD), F32)],
    )
    oh = pl.pallas_call(
        functools.partial(_dsa_s_attend_kernel, pg=pg),
        grid_spec=attend_spec,
        out_shape=jax.ShapeDtypeStruct((b, C_KV_HEADS, rows, C_HEAD), F32),
        compiler_params=_cp(("arbitrary", "arbitrary"), 40),
        name="dsa_s_attend",
    )(page_table, qh, bias_past, bias_new, k_new, v_new, zh, *([ck] * pg), *([cv] * pg))
    y = jnp.transpose(oh.reshape(b, C_KV_HEADS, t, C_GROUPS, C_HEAD), (0, 2, 1, 3, 4)).reshape(b, t, W_C)
    return y.astype(BF16)


def _outproj_kernel(ya_ref, yb_ref, yc_ref, w_ref, x_ref, gate_ref, o_ref, mix_scr):
    @pl.when(pl.program_id(1) == 0)
    def _():
        mix_scr[:, 0:W_A] = ya_ref[...]
        mix_scr[:, W_A:W_A + W_B] = yb_ref[...]
        mix_scr[:, W_A + W_B:] = yc_ref[...]

    o_ref[...] = x_ref[...] + gate_ref[...] * _dot(mix_scr[...], w_ref[...])


def _outproj(ya, yb, yc, w_bf16, layer, x2, gate, *, rows_per_batch, mod_row0):
    m, d = x2.shape
    tn = 1024
    if gate.ndim == 3:
        tm = 512
        gate_spec = pl.BlockSpec((None, 1, tn),
                                 lambda i, j: ((i * tm) // rows_per_batch + mod_row0, 0, 2 * (d // tn) + j))
    else:
        tm = m
        gate_spec = pl.BlockSpec((tm, tn), lambda i, j: (i, j))
    assert m % tm == 0
    return pl.pallas_call(
        _outproj_kernel,
        grid=(m // tm, d // tn),
        in_specs=[pl.BlockSpec((tm, W_A), lambda i, j: (i, 0)),
                  pl.BlockSpec((tm, W_B), lambda i, j: (i, 0)),
                  pl.BlockSpec((tm, W_C), lambda i, j: (i, 0)),
                  pl.BlockSpec((None, d, tn), lambda i, j: (layer, 0, j)),
                  pl.BlockSpec((tm, tn), lambda i, j: (i, j)),
                  gate_spec],
        out_specs=pl.BlockSpec((tm, tn), lambda i, j: (i, j)),
        out_shape=jax.ShapeDtypeStruct((m, d), F32),
        scratch_shapes=[pltpu.VMEM((tm, d), BF16)],
        compiler_params=_cp(("arbitrary", "arbitrary"), 48),
        name="outproj",
    )(ya, yb, yc, w_bf16, x2, gate)


def _final_norm_kernel(x_ref, g_ref, o_ref):
    x = x_ref[...]
    o_ref[...] = x * lax.rsqrt(jnp.mean(x * x, axis=-1, keepdims=True) + EPS) * g_ref[...]


def _final_norm(x2, g):
    m, d = x2.shape
    tm = 256 if m % 256 == 0 else m
    return pl.pallas_call(
        _final_norm_kernel,
        grid=(m // tm,),
        in_specs=[pl.BlockSpec((tm, d), lambda i: (i, 0)), pl.BlockSpec((1, d), lambda i: (0, 0))],
        out_specs=pl.BlockSpec((tm, d), lambda i: (i, 0)),
        out_shape=jax.ShapeDtypeStruct((m, d), F32),
        compiler_params=_cp(("arbitrary",), 32),
        name="final_norm",
    )(x2, g.reshape(1, d))


def _w_in_tile_table():
    n_t = NP // 128
    table, n_plain = [], None
    for dt in range(n_t - 1):
        c = dt * 128
        seg = [n for n in _DST_ORDER if OFF[n] <= c < OFF[n] + _SRC_OFF[n][1]][0]
        src = _SRC_OFF[seg][0] + (c - OFF[seg])
        if src % 128 == 0:
            assert n_plain is None
        else:
            assert src % 128 == 32
            if n_plain is None:
                n_plain = dt
        table.append(src // 128)
    t_kw, t_ab = _SRC_OFF["ki"][0] // 128, _SRC_OFF["a"][0] // 128
    assert _SRC_OFF["ki"][0] % 128 == 32 and _SRC_OFF["wi"][0] == t_kw * 128 + TAIL_WI
    assert _SRC_OFF["a"][0] % 128 == 0 and _SRC_OFF["b"][0] == _SRC_OFF["a"][0] + DN_HEADS
    assert TAIL_A == IDX_DIM and TAIL_B == TAIL_A + DN_HEADS and TAIL_WI == TAIL_B + DN_HEADS
    tile_a = table + [t_kw]
    rows_b = [t_ab * 4] * n_plain + [(s + 1) * 4 for s in table[n_plain:]] + [t_ab * 4]
    return tile_a, rows_b, n_plain


def _wprep_kernel(ta_ref, tb_ref, a_ref, b_ref, o_ref, *, n_plain, n_t):
    dt = pl.program_id(1)
    dt_o = o_ref.dtype

    @pl.when(dt < n_plain)
    def _():
        o_ref[...] = a_ref[...].astype(dt_o)

    @pl.when((dt >= n_plain) & (dt < n_t - 1))
    def _():
        o_ref[0:96, :] = a_ref[32:128, :].astype(dt_o)
        o_ref[96:128, :] = b_ref[...].astype(dt_o)

    @pl.when(dt == n_t - 1)
    def _():
        o_ref[0:TAIL_A, :] = a_ref[32:32 + IDX_DIM, :].astype(dt_o)
        o_ref[TAIL_A:TAIL_WI, :] = b_ref[...].astype(dt_o)
        o_ref[TAIL_WI:TAIL_WI + IDX_HEADS, :] = a_ref[TAIL_WI:TAIL_WI + IDX_HEADS, :].astype(dt_o)
        o_ref[TAIL_WI + IDX_HEADS:, :] = jnp.zeros((128 - TAIL_WI - IDX_HEADS, o_ref.shape[1]), dt_o)


def _permute_w_in(w_in):
    depth, d, _ = w_in.shape
    w_t = jnp.swapaxes(w_in, 1, 2)
    tile_a, rows_b, n_plain = _w_in_tile_table()
    n_t = NP // 128
    grid_spec = pltpu.PrefetchScalarGridSpec(
        num_scalar_prefetch=2,
        grid=(depth, n_t),
        in_specs=[pl.BlockSpec((None, 128, d), lambda l, t, ta, tb: (l, ta[t], 0)),
                  pl.BlockSpec((None, 32, d), lambda l, t, ta, tb: (l, tb[t], 0))],
        out_specs=pl.BlockSpec((None, 128, d), lambda l, t, ta, tb: (l, t, 0)),
    )
    return pl.pallas_call(
        functools.partial(_wprep_kernel, n_plain=n_plain, n_t=n_t),
        grid_spec=grid_spec,
        out_shape=jax.ShapeDtypeStruct((depth, NP, d), BF16),
        compiler_params=_cp(("arbitrary", "arbitrary"), 32),
        name="w_in_layout",
    )(jnp.asarray(tile_a, I32), jnp.asarray(rows_b, I32), w_t, w_t)


def _delta_t(l):
    for t in (128, 64):
        if l % t == 0:
            return t
    raise ValueError("sequence length must be a multiple of 64")


def kernel(x_prompt, x_sample, cache_k, cache_v, cache_kidx, state_dn, state_conv, page_table,
           c_prompt, c_sample, w_ada, b_ada, g_norm, w_in, a_vnorm, a_ws, a_bs, dn_conv_w,
           dn_a_log, dn_dt_bias, dn_onorm, w_out, g_final):
    bp, lp, d = x_prompt.shape
    bs, ls, _ = x_sample.shape
    depth = w_ada.shape[0]
    assert d == D_MODEL and w_in.shape[2] == D_IN
    assert CONV_K - 1 <= ls <= DN_CHUNK

    n_c = bp + bs
    c_rows = jnp.concatenate([c_prompt, c_sample], axis=0)
    r_pad = (-n_c) % 8
    if r_pad:
        c_rows = jnp.pad(c_rows, ((0, r_pad), (0, 0)))
    m_all = _ada(c_rows, w_ada, b_ada)

    xp = x_prompt.reshape(bp * lp, d)
    xs = x_sample.reshape(bs * ls, d)
    ls_pad = DN_CHUNK
    outs = {k: [] for k in ("pk", "pv", "pki", "pdn", "pconv", "sk", "sv", "ski", "sdn", "sconv", "samlp")}
    zeros_conv = jnp.zeros((bp, CONV_K - 1, DN_CONV_DIM), F32)
    zeros_state = jnp.zeros((bp, DN_HEADS, DN_HEAD, DN_HEAD), F32)

    w_in_all = _permute_w_in(w_in)
    w_out_all = w_out.astype(BF16)
    for l in range(depth):
        g_l = g_norm[l].reshape(1, d)
        m_l = m_all[l]
        m3 = m_l.reshape(m_l.shape[0], 1, 3 * d)
        ms = jnp.repeat(m_l[bp:bp + bs], ls, axis=0)

        pp = _inproj(xp, g_l, m3, m3, w_in_all, l, rows_per_batch=lp, mod_row0=0)
        p3 = pp.reshape(bp, lp, NP)
        (ya,) = _mixa(p3, a_vnorm[l], a_ws[l], a_bs[l], emit_va=False)
        abt = jnp.transpose(p3[:, :, OFF["a"]:OFF["a"] + 2 * DN_HEADS], (0, 2, 1))
        yb, s_p = _delta(p3, abt, dn_conv_w[l], zeros_conv, zeros_state, dn_a_log[l], dn_dt_bias[l],
                         dn_onorm[l], t=_delta_t(lp), valid_len=_delta_t(lp))
        yc = _dsa_prompt(p3)
        xp = _outproj(ya.reshape(bp * lp, W_A), yb.reshape(bp * lp, W_B), yc.reshape(bp * lp, W_C),
                      w_out_all, l, xp, m3, rows_per_batch=lp, mod_row0=0)
        outs["pk"].append(p3[:, :, OFF["kc"]:OFF["kc"] + C_KV].reshape(bp, lp, C_KV_HEADS, C_HEAD))
        outs["pv"].append(p3[:, :, OFF["vc"]:OFF["vc"] + C_KV].reshape(bp, lp, C_KV_HEADS, C_HEAD))
        outs["pki"].append(p3[:, :, OFF_TAIL:OFF_TAIL + IDX_DIM])
        outs["pdn"].append(s_p)
        outs["pconv"].append(p3[:, lp - (CONV_K - 1):, OFF["qkv"]:OFF["qkv"] + DN_CONV_DIM])

        ps = _inproj(xs, g_l, ms[:, d:2 * d], ms[:, 0:d], w_in_all, l, rows_per_batch=ls, mod_row0=bp)
        p3s = ps.reshape(bs, ls, NP)
        ya_s, va_s = _mixa(p3s, a_vnorm[l], a_ws[l], a_bs[l], emit_va=True)
        p3s_pad = jnp.pad(p3s, ((0, 0), (0, ls_pad - ls), (0, 0)))
        abt_s = jnp.transpose(p3s_pad[:, :, OFF["a"]:OFF["a"] + 2 * DN_HEADS], (0, 2, 1))
        yb_s, s_s = _delta(p3s_pad, abt_s, dn_conv_w[l], state_conv[l], state_dn[l], dn_a_log[l],
                           dn_dt_bias[l], dn_onorm[l], t=ls_pad, valid_len=ls)
        yc_s = _dsa_sample(p3s, cache_k, cache_v, cache_kidx, page_table, l)
        xs = _outproj(ya_s.reshape(bs * ls, W_A), yb_s[:, :ls].reshape(bs * ls, W_B),
                      yc_s.reshape(bs * ls, W_C), w_out_all, l, xs, ms[:, 2 * d:3 * d],
                      rows_per_batch=ls, mod_row0=bp)
        outs["sk"].append(p3s[:, :, OFF["kc"]:OFF["kc"] + C_KV].reshape(bs, ls, C_KV_HEADS, C_HEAD))
        outs["sv"].append(p3s[:, :, OFF["vc"]:OFF["vc"] + C_KV].reshape(bs, ls, C_KV_HEADS, C_HEAD))
        outs["ski"].append(p3s[:, :, OFF_TAIL:OFF_TAIL + IDX_DIM])
        outs["sdn"].append(s_s)
        outs["sconv"].append(p3s[:, ls - (CONV_K - 1):, OFF["qkv"]:OFF["qkv"] + DN_CONV_DIM])
        outs["samlp"].append(va_s)

    y_prompt = _final_norm(xp, g_final).reshape(bp, lp, d)
    y_sample = _final_norm(xs, g_final).reshape(bs, ls, d)
    st = jnp.stack
    return (y_prompt, y_sample, st(outs["pk"]), st(outs["pv"]), st(outs["pki"]), st(outs["pdn"]),
            st(outs["pconv"]), st(outs["sk"]), st(outs["sv"]), st(outs["ski"]), st(outs["sdn"]),
            st(outs["sconv"]), st(outs["samlp"]))
```

```python
import functools

import jax
import jax.numpy as jnp
from jax import lax
from jax.experimental import pallas as pl
from jax.experimental.pallas import tpu as pltpu

F32 = jnp.float32
BF16 = jnp.bfloat16
I32 = jnp.int32
EPS = 1e-6
INT_MIN = -(2 ** 31)
NEG_INF = float("-inf")

D_MODEL = 4096
W_A = D_MODEL // 4
A_GROUP = 128
A_HEADS = W_A // A_GROUP
A_CHUNK = 128
W_B = D_MODEL // 2
DN_HEAD = 128
DN_HEADS = W_B // DN_HEAD
CONV_K = 4
DN_CONV_DIM = 3 * W_B
DN_CHUNK = 64
W_C = D_MODEL - W_A - W_B
C_HEAD = 128
C_HEADS = W_C // C_HEAD
C_KV_HEADS = 2
C_GROUPS = C_HEADS // C_KV_HEADS
C_KV = C_KV_HEADS * C_HEAD
IDX_HEADS = 16
IDX_DIM = 64
TOPK_MAX = 256
Q_BLOCK = 128

_SRC_SPLITS = (W_A, W_A, W_A, DN_CONV_DIM, W_B, DN_HEADS, DN_HEADS,
               W_C, C_KV, C_KV, W_C, IDX_HEADS * IDX_DIM, IDX_DIM, IDX_HEADS)
_SRC_NAMES = ("u", "v", "za", "qkv", "zb", "a", "b", "qc", "kc", "vc", "zc", "qi", "ki", "wi")
_SRC_OFF = {}
_o = 0
for _n, _w in zip(_SRC_NAMES, _SRC_SPLITS):
    _SRC_OFF[_n] = (_o, _w)
    _o += _w
D_IN = _o

_DST_ORDER = ("u", "v", "za", "qkv", "zb", "qc", "zc", "qi", "kc", "vc", "ki", "a", "b", "wi")
OFF = {}
_o = 0
for _n in _DST_ORDER:
    OFF[_n] = _o
    _o += _SRC_OFF[_n][1]
NP = ((_o + 127) // 128) * 128
OFF_TAIL = OFF["ki"]
TAIL_A = OFF["a"] - OFF_TAIL
TAIL_B = OFF["b"] - OFF_TAIL
TAIL_WI = OFF["wi"] - OFF_TAIL

INPROJ_TN = 1536
DELTA_HEADS_PER_STEP = 8
DSA_KEY_TILE = 512


def _cp(sem, vmem_mb=48):
    return pltpu.CompilerParams(dimension_semantics=sem, vmem_limit_bytes=vmem_mb * 1024 * 1024)


def _silu(x):
    return (0.5 * x) * (1.0 + jnp.tanh(0.5 * x))


def _gelu(x):
    return 0.5 * x * (1.0 + jnp.tanh(0.7978845608028654 * (x + 0.044715 * (x * x * x))))


def _dot(a, b):
    return jnp.dot(a, b, preferred_element_type=F32)


def _dot_nt(a, b):
    return lax.dot_general(a, b, (((1,), (1,)), ((), ())), preferred_element_type=F32)


def _dot_tn(a, b):
    return lax.dot_general(a, b, (((0,), (0,)), ((), ())), preferred_element_type=F32)


def _b16(a):
    return a.astype(BF16)


def _ada_kernel(c_ref, w_ref, b_ref, o_ref):
    s = _silu(c_ref[...]).astype(BF16)
    o_ref[...] = _dot(s, w_ref[...].astype(BF16)) + b_ref[...]


def _ada(c_rows, w_ada, b_ada):
    depth, d, n = w_ada.shape
    r = c_rows.shape[0]
    tn = 512
    return pl.pallas_call(
        _ada_kernel,
        grid=(depth, n // tn),
        in_specs=[pl.BlockSpec((r, d), lambda l, j: (0, 0)),
                  pl.BlockSpec((None, d, tn), lambda l, j: (l, 0, j)),
                  pl.BlockSpec((None, 1, tn), lambda l, j: (l, 0, j))],
        out_specs=pl.BlockSpec((None, r, tn), lambda l, j: (l, 0, j)),
        out_shape=jax.ShapeDtypeStruct((depth, r, n), F32),
        compiler_params=_cp(("arbitrary", "arbitrary"), 40),
        name="ada",
    )(c_rows, w_ada, b_ada.reshape(depth, 1, n))


def _inproj_kernel(x_ref, g_ref, sc_ref, sh_ref, w_ref, o_ref, h_scr, *, rc):
    @pl.when(pl.program_id(1) == 0)
    def _():
        tm = x_ref.shape[0]
        per_row = sc_ref.shape[0] != 1

        def body(c, carry):
            r = pl.ds(pl.multiple_of(c * rc, rc), rc)
            x = x_ref[r, :]
            y = x * lax.rsqrt(jnp.mean(x * x, axis=-1, keepdims=True) + EPS)
            sc = sc_ref[r, :] if per_row else sc_ref[...]
            sh = sh_ref[r, :] if per_row else sh_ref[...]
            h_scr[r, :] = ((y * g_ref[...]) * (1.0 + sc) + sh).astype(BF16)
            return carry

        lax.fori_loop(0, tm // rc, body, 0)

    o_ref[...] = _dot_nt(h_scr[...], w_ref[...])


def _inproj(x2, g, sc, sh, w_bf16, layer, *, rows_per_batch, mod_row0):
    m, d = x2.shape
    n = w_bf16.shape[1]
    tn = INPROJ_TN
    if sc.ndim == 3:
        tm = 512
        assert rows_per_batch % tm == 0
        sc_spec = pl.BlockSpec((None, 1, d), lambda i, j: ((i * tm) // rows_per_batch + mod_row0, 0, 1))
        sh_spec = pl.BlockSpec((None, 1, d), lambda i, j: ((i * tm) // rows_per_batch + mod_row0, 0, 0))
        rc = 64
    else:
        tm = m
        sc_spec = pl.BlockSpec((tm, d), lambda i, j: (i, 0))
        sh_spec = pl.BlockSpec((tm, d), lambda i, j: (i, 0))
        rc = tm
    assert m % tm == 0
    return pl.pallas_call(
        functools.partial(_inproj_kernel, rc=rc),
        grid=(m // tm, pl.cdiv(n, tn)),
        in_specs=[pl.BlockSpec((tm, d), lambda i, j: (i, 0)),
                  pl.BlockSpec((1, d), lambda i, j: (0, 0)),
                  sc_spec, sh_spec,
                  pl.BlockSpec((None, tn, d), lambda i, j: (layer, j, 0))],
        out_specs=pl.BlockSpec((tm, tn), lambda i, j: (i, j)),
        out_shape=jax.ShapeDtypeStruct((m, n), F32),
        scratch_shapes=[pltpu.VMEM((tm, d), BF16)],
        compiler_params=_cp(("arbitrary", "arbitrary"), 56),
        name="inproj",
    )(x2, g, sc, sh, w_bf16)


def _mixa_kernel(u_ref, v_ref, z_ref, vn_ref, ws_ref, bst_ref, y_ref, *rest, c, emit_va):
    u = _gelu(u_ref[...])
    v = _gelu(v_ref[...])
    mu = jnp.mean(v, axis=-1, keepdims=True)
    dv = v - mu
    va = dv * lax.rsqrt(jnp.mean(dv * dv, axis=-1, keepdims=True) + EPS) * vn_ref[...]
    if emit_va:
        rest[0][...] = va
    z = _silu(z_ref[...])
    row = lax.broadcasted_iota(I32, (c, c), 0)
    col = lax.broadcasted_iota(I32, (c, c), 1)
    tril = col <= row
    for h in range(A_HEADS):
        cols = slice(h * A_GROUP, (h + 1) * A_GROUP)
        wm = jnp.where(tril, ws_ref[h], 0.0)
        vh = va[:, cols]
        if c >= 128:
            mixed = _dot(wm.astype(BF16), vh.astype(BF16))
        else:
            mixed = wm[:, 0:1] * vh[0:1, :]
            for s in range(1, c):
                mixed = mixed + wm[:, s:s + 1] * vh[s:s + 1, :]
        mixed = mixed + bst_ref[:, h:h + 1]
        y_ref[:, cols] = (u[:, cols] * mixed * z[:, cols]).astype(y_ref.dtype)


def _mixa(p3, a_vnorm, a_ws, a_bs, *, emit_va):
    b, l, _ = p3.shape
    c = min(A_CHUNK, l)
    n = l // c
    ws = a_ws[:, :c, :c]
    bst = a_bs[:, :c].T
    wblk = W_A
    outs = [jax.ShapeDtypeStruct((b, l, W_A), BF16)]
    out_specs = [pl.BlockSpec((None, c, W_A), lambda i, j: (i, j, 0))]
    if emit_va:
        outs.append(jax.ShapeDtypeStruct((b, l, W_A), F32))
        out_specs.append(pl.BlockSpec((None, c, W_A), lambda i, j: (i, j, 0)))
    res = pl.pallas_call(
        functools.partial(_mixa_kernel, c=c, emit_va=emit_va),
        grid=(b, n),
        in_specs=[pl.BlockSpec((None, c, wblk), lambda i, j: (i, j, OFF["u"] // wblk)),
                  pl.BlockSpec((None, c, wblk), lambda i, j: (i, j, OFF["v"] // wblk)),
                  pl.BlockSpec((None, c, wblk), lambda i, j: (i, j, OFF["za"] // wblk)),
                  pl.BlockSpec((1, W_A), lambda i, j: (0, 0)),
                  pl.BlockSpec((A_HEADS, c, c), lambda i, j: (0, 0, 0)),
                  pl.BlockSpec((c, A_HEADS), lambda i, j: (0, 0))],
        out_specs=out_specs,
        out_shape=outs,
        compiler_params=_cp(("arbitrary", "arbitrary"), 32),
        name="mixa",
    )(p3, p3, p3, a_vnorm.reshape(1, W_A), ws, bst)
    return res


def _delta_kernel(alog_ref, dtb_ref,
                  q_ref, k_ref, v_ref, z_ref, ab_ref,
                  cwq_ref, cwk_ref, cwv_ref, cpq_ref, cpk_ref, cpv_ref,
                  s0_ref, on_ref,
                  y_ref, sout_ref,
                  xbuf, s_scr, *, t, c, hp, valid_len):
    hg = pl.program_id(1)
    n = pl.program_id(2)
    nlast = pl.num_programs(2) - 1
    dh = DN_HEAD

    @pl.when(n == 0)
    def _():
        s_scr[...] = s0_ref[...]
        for j in range(hp):
            lanes = slice(j * dh, (j + 1) * dh)
            xbuf[3 * j + 0, 5:8, :] = cpq_ref[:, lanes]
            xbuf[3 * j + 1, 5:8, :] = cpk_ref[:, lanes]
            xbuf[3 * j + 2, 5:8, :] = cpv_ref[:, lanes]

    row = lax.broadcasted_iota(I32, (t, t), 0)
    col = lax.broadcasted_iota(I32, (t, t), 1)
    shift = c.bit_length() - 1
    same = (row >> shift) == (col >> shift)
    eye = row == col
    incl = same & (col <= row)
    strict = same & (col < row)
    incl_t = same & (row <= col)
    blk8 = (row >> 3) == (col >> 3)
    off_masks = []
    bs = 8
    while bs < c:
        sh_b = bs.bit_length() - 1
        inner = (row >> sh_b) == (col >> sh_b)
        outer = (row >> (sh_b + 1)) == (col >> (sh_b + 1))
        off_masks.append(outer & jnp.logical_not(inner))
        bs *= 2
    eye_f = jnp.where(eye, 1.0, 0.0)
    if valid_len < t:
        lane_valid = lax.broadcasted_iota(I32, (1, t), 1) < valid_len
        sub_valid = lax.broadcasted_iota(I32, (t, 1), 0) < valid_len

    def to_col(r):
        return jnp.sum(jnp.where(eye, r, 0.0), axis=1, keepdims=True)

    def conv(idx, x, w_ref, lanes):
        xbuf[idx, 8:8 + t, :] = x
        y = xbuf[idx, pl.ds(5, t), :] * w_ref[0:1, lanes]
        for jj in range(1, CONV_K):
            y = y + xbuf[idx, pl.ds(5 + jj, t), :] * w_ref[jj:jj + 1, lanes]
        xbuf[idx, 5:8, :] = x[t - 3:t, :]
        return _silu(y)

    heads = range(hp)
    lanes_of = [slice(j * dh, (j + 1) * dh) for j in heads]

    def per_head(f, *lists):
        return [f(*vals) for vals in zip(*lists)]

    q = [conv(3 * j + 0, q_ref[:, lanes_of[j]], cwq_ref, lanes_of[j]) for j in heads]
    k = [conv(3 * j + 1, k_ref[:, lanes_of[j]], cwk_ref, lanes_of[j]) for j in heads]
    v = [conv(3 * j + 2, v_ref[:, lanes_of[j]], cwv_ref, lanes_of[j]) for j in heads]
    qc = per_head(lambda a: a * lax.rsqrt(jnp.sum(a * a, axis=-1, keepdims=True) + EPS) * (dh ** -0.5), q)
    kc = per_head(lambda a: a * lax.rsqrt(jnp.sum(a * a, axis=-1, keepdims=True) + EPS), k)

    def gates(j):
        h = hg * hp + j
        a_row = ab_ref[pl.ds(h, 1), :]
        b_row = ab_ref[pl.ds(DN_HEADS + h, 1), :]
        xa = a_row + dtb_ref[h]
        softplus = jnp.maximum(xa, 0.0) + jnp.log(1.0 + jnp.exp(-jnp.abs(xa)))
        a_coef = jnp.exp(jnp.zeros((1, 1), F32) + alog_ref[h])
        g_row = -a_coef * softplus
        beta_row = 1.0 / (1.0 + jnp.exp(-b_row))
        if valid_len < t:
            g_row = jnp.where(lane_valid, g_row, 0.0)
            beta_row = jnp.where(lane_valid, beta_row, 0.0)
        return g_row, beta_row

    g_row, beta_row = zip(*[gates(j) for j in heads])
    if valid_len < t:
        kc = per_head(lambda a: jnp.where(sub_valid, a, 0.0), kc)
        v = per_head(lambda a: jnp.where(sub_valid, a, 0.0), v)

    g_col = per_head(to_col, g_row)
    beta_col = per_head(to_col, beta_row)
    gc_col = per_head(lambda r: jnp.sum(jnp.where(incl, r, 0.0), axis=1, keepdims=True), g_row)
    glast_col = per_head(lambda r: jnp.sum(jnp.where(same, r, 0.0), axis=1, keepdims=True), g_row)
    gc_row = per_head(lambda cl: jnp.sum(jnp.where(incl_t, cl, 0.0), axis=0, keepdims=True), g_col)
    decay = per_head(lambda gc, gr: jnp.where(incl, jnp.exp(jnp.where(incl, gc - gr, 0.0)), 0.0), gc_col, gc_row)
    kb = per_head(lambda a, b: a * b, kc, beta_col)
    kc16 = per_head(lambda a: a.astype(BF16), kc)
    a_mat = per_head(lambda a, b, dc: jnp.where(strict, _dot_nt(a.astype(BF16), b) * dc, 0.0), kb, kc16, decay)
    attn = per_head(lambda a, b, dc: (_dot_nt(a.astype(BF16), b) * dc).astype(BF16), qc, kc16, decay)
    eg = per_head(jnp.exp, gc_col)
    rhs = per_head(lambda vv, bc, kk, e: _b16(jnp.concatenate([vv * bc, kk * e], axis=1)),
                   v, beta_col, kb, eg)

    n0f = per_head(lambda a: jnp.where(blk8, -a, 0.0), a_mat)
    n0 = per_head(_b16, n0f)
    n2 = per_head(lambda a: _b16(_dot(a, a)), n0)
    n4 = per_head(lambda a: _b16(_dot(a, a)), n2)
    x = per_head(lambda a: eye_f + a, n0f)
    x = per_head(lambda xx, nn: xx + _dot(_b16(xx), nn), x, n2)
    x = per_head(lambda xx, nn: xx + _dot(_b16(xx), nn), x, n4)
    for om in off_masks:
        xs = per_head(_b16, x)
        xa_off = per_head(lambda s_, a: _b16(_dot(s_, _b16(jnp.where(om, a, 0.0)))), xs, a_mat)
        x = per_head(lambda xx, xo, s_: xx - _dot(xo, s_), x, xa_off, xs)
    sol16 = per_head(lambda xx, r: _b16(_dot(_b16(xx), r)), x, rhs)
    auw = per_head(_dot, attn, sol16)
    qw = per_head(lambda a, e, m_: (a * e - m_[:, dh:]).astype(BF16), qc, eg, auw)
    kg16 = per_head(lambda a, gl_, gc: (a * jnp.exp(gl_ - gc)).astype(BF16), kc, glast_col, gc_col)

    s = [s_scr[j] for j in heads]
    outs = [[] for _ in heads]
    for i in range(t // c):
        rows = slice(i * c, (i + 1) * c)
        s16 = per_head(lambda a: a.astype(BF16), s)
        kuw = per_head(lambda a, b: _dot_tn(a[rows], b[rows]), kg16, sol16)
        for j in heads:
            outs[j].append(_dot(qw[j][rows], s16[j]) + auw[j][rows, :dh])
        gl = per_head(lambda a: jnp.exp(a[i * c:i * c + 1, :]), glast_col)
        s = per_head(lambda g_, s_, m_, b16: g_ * s_ + m_[:, :dh] - _dot(m_[:, dh:].astype(BF16), b16),
                     gl, s, kuw, s16)
    for j in heads:
        s_scr[j] = s[j]
        o = outs[j][0] if len(outs[j]) == 1 else jnp.concatenate(outs[j], axis=0)
        on = o * lax.rsqrt(jnp.mean(o * o, axis=-1, keepdims=True) + EPS) * on_ref[...]
        y_ref[:, lanes_of[j]] = (on * _silu(z_ref[:, lanes_of[j]])).astype(y_ref.dtype)

    @pl.when(n == nlast)
    def _():
        sout_ref[...] = s_scr[...]


def _delta(p3, abt, conv_w, conv_prev, s0, a_log, dt_bias, onorm, *, t, valid_len):
    b, l, _ = p3.shape
    assert l % t == 0
    c = min(DN_CHUNK, t)
    hp = DELTA_HEADS_PER_STEP
    nh = DN_HEADS
    ng = nh // hp
    w = 128 * hp
    cb_q = OFF["qkv"] // w
    cb_z = OFF["zb"] // w
    assert OFF["qkv"] % w == 0 and OFF["zb"] % w == 0 and W_B % w == 0

    def pspec(cb0):
        return pl.BlockSpec((None, t, w), lambda i, h, n, a, d: (i, n, cb0 + h))

    def cwspec(sidx):
        return pl.BlockSpec((CONV_K, w), lambda i, h, n, a, d: (0, sidx * ng + h))

    def cpspec(sidx):
        return pl.BlockSpec((None, CONV_K - 1, w), lambda i, h, n, a, d: (i, 0, sidx * ng + h))

    grid_spec = pltpu.PrefetchScalarGridSpec(
        num_scalar_prefetch=2,
        grid=(b, ng, l // t),
        in_specs=[pspec(cb_q), pspec(cb_q + ng), pspec(cb_q + 2 * ng), pspec(cb_z),
                  pl.BlockSpec((None, 2 * nh, t), lambda i, h, n, a, d: (i, 0, n)),
                  cwspec(0), cwspec(1), cwspec(2), cpspec(0), cpspec(1), cpspec(2),
                  pl.BlockSpec((None, hp, DN_HEAD, DN_HEAD), lambda i, h, n, a, d: (i, h, 0, 0)),
                  pl.BlockSpec((1, DN_HEAD), lambda i, h, n, a, d: (0, 0))],
        out_specs=[pl.BlockSpec((None, t, w), lambda i, h, n, a, d: (i, n, h)),
                   pl.BlockSpec((None, hp, DN_HEAD, DN_HEAD), lambda i, h, n, a, d: (i, h, 0, 0))],
        scratch_shapes=[pltpu.VMEM((3 * hp, t + 8, 128), F32), pltpu.VMEM((hp, DN_HEAD, DN_HEAD), F32)],
    )
    return pl.pallas_call(
        functools.partial(_delta_kernel, t=t, c=c, hp=hp, valid_len=valid_len),
        grid_spec=grid_spec,
        out_shape=[jax.ShapeDtypeStruct((b, l, W_B), BF16),
                   jax.ShapeDtypeStruct((b, nh, DN_HEAD, DN_HEAD), F32)],
        compiler_params=_cp(("arbitrary", "arbitrary", "arbitrary"), 40),
        name="delta",
    )(a_log, dt_bias, p3, p3, p3, p3, abt, conv_w, conv_w, conv_w,
      conv_prev, conv_prev, conv_prev, s0, onorm.reshape(1, DN_HEAD))


def _ordered_bits_to_float(u):
    key = u ^ jnp.int32(INT_MIN)
    bits = jnp.where(key < 0, key ^ jnp.int32(0x7FFFFFFF), key)
    return pltpu.bitcast(bits, F32)


def _topk_threshold(count_ge, shape, topk):
    def bit_body(i, u):
        bit = jnp.left_shift(jnp.int32(1), 31 - i)
        cand_u = u | bit
        cnt = count_ge(_ordered_bits_to_float(cand_u))
        return jnp.where(cnt >= float(topk), cand_u, u)

    u = lax.fori_loop(0, 32, bit_body, jnp.zeros(shape, I32))
    return _ordered_bits_to_float(u)


def _softmax_update(m_ref, l_ref, acc_ref, idx, s, v_tiles):
    m_old = m_ref[idx]
    m_new = jnp.maximum(m_old, jnp.max(s, axis=1, keepdims=True))
    m_safe = jnp.where(m_new == NEG_INF, 0.0, m_new)
    alpha = jnp.exp(m_old - m_safe)
    p = jnp.exp(s - m_safe)
    l_ref[idx] = alpha * l_ref[idx] + jnp.sum(p, axis=1, keepdims=True)
    acc = alpha * acc_ref[idx]
    p16 = p.astype(BF16)
    for i, vt in enumerate(v_tiles):
        acc = acc + _dot(p16[:, i * 128:(i + 1) * 128], vt)
    acc_ref[idx] = acc
    m_ref[idx] = m_new


def _dsa_prompt_kernel(qi_ref, tq_ref, tall_ref, qc_ref, k_ref, v_ref, z_ref, y_ref,
                       sc_scr, m_scr, l_scr, acc_scr, *, l, topk, kt):
    qb = pl.program_id(1)
    nq = Q_BLOCK
    n_tiles = (qb * nq + nq + kt - 1) // kt
    n_pairs = C_HEADS // 2
    w_rows = tq_ref[...].T[TAIL_WI:TAIL_WI + IDX_HEADS, :] * ((IDX_HEADS ** -0.5) * (IDX_DIM ** -0.5))
    rq = []
    for j in range(IDX_HEADS // 2):
        a = qi_ref[:, j * 128:(j + 1) * 128]
        rq.append(jnp.concatenate([a, pltpu.roll(a, IDX_DIM, 1)], axis=0).astype(BF16))
    kpos = lax.broadcasted_iota(I32, (kt, nq), 0)
    qpos = qb * nq + lax.broadcasted_iota(I32, (kt, nq), 1)
    lane = lax.broadcasted_iota(I32, (kt, 128), 1)

    def score_tile(ti, carry):
        k0 = pl.multiple_of(ti * kt, kt)
        ki = jnp.where(lane < IDX_DIM, tall_ref[pl.ds(k0, kt), :], 0.0).astype(BF16)
        score = jnp.zeros((kt, nq), F32)
        for j in range(IDX_HEADS // 2):
            lg = _dot_nt(ki, rq[j])
            score = score + w_rows[2 * j:2 * j + 1, :] * jnp.maximum(lg[:, :nq], 0.0)
            score = score + w_rows[2 * j + 1:2 * j + 2, :] * jnp.maximum(lg[:, nq:], 0.0)
        sc_scr[pl.ds(k0, kt), :] = jnp.where(kpos + k0 <= qpos, score, NEG_INF)
        return carry

    lax.fori_loop(0, n_tiles, score_tile, 0)

    ct = 256
    n_ct = (qb * nq + nq + ct - 1) // ct
    n_acc = 4

    def count_ge(cand):
        cand8 = jnp.broadcast_to(cand, (8, nq))

        def tile_body(ti, accs):
            k0 = pl.multiple_of(ti * ct, ct)
            tile = sc_scr[pl.ds(k0, ct), :]
            accs = list(accs)
            for r in range(ct // 8):
                accs[r % n_acc] = accs[r % n_acc] + jnp.where(tile[8 * r:8 * r + 8, :] >= cand8, 1.0, 0.0)
            return tuple(accs)

        accs = lax.fori_loop(0, n_ct, tile_body, tuple(jnp.zeros((8, nq), F32) for _ in range(n_acc)))
        acc = (accs[0] + accs[1]) + (accs[2] + accs[3])
        return jnp.sum(acc, axis=0, keepdims=True)

    thr = _topk_threshold(count_ge, (1, nq), topk)
    keep_all = (qb * nq + lax.broadcasted_iota(I32, (1, nq), 1)) < topk

    m_scr[...] = jnp.full(m_scr.shape, NEG_INF, F32)
    l_scr[...] = jnp.zeros(l_scr.shape, F32)
    acc_scr[...] = jnp.zeros(acc_scr.shape, F32)
    scale = (C_HEAD ** -0.5) * 1.4426950408889634
    qp = []
    for pr in range(n_pairs):
        c0 = slice(2 * pr * C_HEAD, (2 * pr + 1) * C_HEAD)
        c1 = slice((2 * pr + 1) * C_HEAD, (2 * pr + 2) * C_HEAD)
        qp.append(jnp.concatenate([qc_ref[:, c0] * scale, qc_ref[:, c1] * scale], axis=0).astype(BF16))
    pairs_per_kv = n_pairs // C_KV_HEADS

    def attend_tile(ti, carry):
        k0 = pl.multiple_of(ti * kt, kt)
        sc = sc_scr[pl.ds(k0, kt), :]
        valid = (kpos + k0 <= qpos) & (keep_all | (sc >= thr))
        bias = jnp.where(valid, 0.0, NEG_INF)
        k16 = [k_ref[pl.ds(k0, kt), hk * C_HEAD:(hk + 1) * C_HEAD].astype(BF16) for hk in range(C_KV_HEADS)]
        vt16 = [v_ref[pl.ds(k0, kt), hk * C_HEAD:(hk + 1) * C_HEAD].T.astype(BF16) for hk in range(C_KV_HEADS)]
        bias2 = jnp.concatenate([bias, bias], axis=1)
        pr_all = range(n_pairs)
        s = [_dot_nt(k16[pr // pairs_per_kv], qp[pr]) + bias2 for pr in pr_all]
        m_old = [m_scr[pr] for pr in pr_all]
        m_new = [jnp.maximum(m_old[pr], jnp.max(s[pr], axis=0, keepdims=True)) for pr in pr_all]
        m_safe = [jnp.where(m_new[pr] == NEG_INF, 0.0, m_new[pr]) for pr in pr_all]
        alpha = [jnp.exp2(m_old[pr] - m_safe[pr]) for pr in pr_all]
        p = [jnp.exp2(s[pr] - m_safe[pr]) for pr in pr_all]
        for pr in pr_all:
            l_scr[pr] = alpha[pr] * l_scr[pr] + jnp.sum(p[pr], axis=0, keepdims=True)
            acc_scr[pr] = alpha[pr] * acc_scr[pr] + _dot(vt16[pr // pairs_per_kv], p[pr].astype(BF16))
            m_scr[pr] = m_new[pr]
        return carry

    lax.fori_loop(0, n_tiles, attend_tile, 0)
    for pr in range(n_pairs):
        o_t = acc_scr[pr] / l_scr[pr]
        for e in range(2):
            cols = slice((2 * pr + e) * C_HEAD, (2 * pr + e + 1) * C_HEAD)
            o = o_t[:, e * nq:(e + 1) * nq].T
            y_ref[:, cols] = (o * _silu(z_ref[:, cols])).astype(y_ref.dtype)


def _dsa_prompt(p3):
    b, l, _ = p3.shape
    assert l % Q_BLOCK == 0
    topk = min(TOPK_MAX, l // 4)
    nq = Q_BLOCK
    kt = DSA_KEY_TILE if l % DSA_KEY_TILE == 0 else 128
    return pl.pallas_call(
        functools.partial(_dsa_prompt_kernel, l=l, topk=topk, kt=kt),
        grid=(b, l // nq),
        in_specs=[pl.BlockSpec((None, nq, W_C), lambda i, j: (i, j, OFF["qi"] // W_C)),
                  pl.BlockSpec((None, nq, 128), lambda i, j: (i, j, OFF_TAIL // 128)),
                  pl.BlockSpec((None, l, 128), lambda i, j: (i, 0, OFF_TAIL // 128)),
                  pl.BlockSpec((None, nq, W_C), lambda i, j: (i, j, OFF["qc"] // W_C)),
                  pl.BlockSpec((None, l, C_KV), lambda i, j: (i, 0, OFF["kc"] // C_KV)),
                  pl.BlockSpec((None, l, C_KV), lambda i, j: (i, 0, OFF["vc"] // C_KV)),
                  pl.BlockSpec((None, nq, W_C), lambda i, j: (i, j, OFF["zc"] // W_C))],
        out_specs=pl.BlockSpec((None, nq, W_C), lambda i, j: (i, j, 0)),
        out_shape=jax.ShapeDtypeStruct((b, l, W_C), BF16),
        scratch_shapes=[pltpu.VMEM((l, nq), F32),
                        pltpu.VMEM((C_HEADS // 2, 1, 2 * nq), F32), pltpu.VMEM((C_HEADS // 2, 1, 2 * nq), F32),
                        pltpu.VMEM((C_HEADS // 2, C_HEAD, 2 * nq), F32)],
        compiler_params=_cp(("arbitrary", "arbitrary"), 48),
        name="dsa_prompt",
    )(p3, p3, p3, p3, p3, p3, p3)


def _idx_scores(logits, wcol, t):
    r = jnp.maximum(logits, 0.0) * wcol
    return jnp.sum(r.reshape(t, IDX_HEADS, logits.shape[1]), axis=1)


def _dsa_s_score_kernel(pt_ref, q_ref, w_ref, *refs, pg, t):
    pages = refs[:pg]
    out_ref = refs[pg]
    q16 = q_ref[...].astype(BF16)
    wcol = w_ref[...] * ((IDX_HEADS ** -0.5) * (IDX_DIM ** -0.5))
    for i in range(pg):
        out_ref[:, i * 128:(i + 1) * 128] = _idx_scores(_dot(q16, pages[i][...].astype(BF16)), wcol, t)


def _dsa_s_select_kernel(sc_ref, q_ref, w_ref, kn_ref, bias_ref, *, t, nk, topk):
    q16 = q_ref[...].astype(BF16)
    wcol = w_ref[...] * ((IDX_HEADS ** -0.5) * (IDX_DIM ** -0.5))
    sc_new = _idx_scores(_dot_nt(q16, kn_ref[...].astype(BF16)), wcol, t)
    tq = lax.broadcasted_iota(I32, (t, 128), 0)
    jn = lax.broadcasted_iota(I32, (t, 128), 1)
    new_ok = jn <= tq
    sc_new = jnp.where(new_ok, sc_new, NEG_INF)
    sc_past = sc_ref[...]

    def count_ge(cand):
        c1 = jnp.sum(jnp.where(sc_past >= cand, 1.0, 0.0), axis=1, keepdims=True)
        c2 = jnp.sum(jnp.where(sc_new >= cand, 1.0, 0.0), axis=1, keepdims=True)
        return c1 + c2

    thr = _topk_threshold(count_ge, (t, 1), topk)
    keep_all = (nk + 1 + lax.broadcasted_iota(I32, (t, 1), 0)) <= topk
    bias_ref[:, :nk] = jnp.where(keep_all | (sc_past >= thr), 0.0, NEG_INF)
    bias_ref[:, nk:] = jnp.where(new_ok & (keep_all | (sc_new >= thr)), 0.0, NEG_INF)


def _dsa_s_attend_kernel(pt_ref, q_ref, bias_ref, biasn_ref, kn_ref, vn_ref, z_ref, *refs, pg):
    kpages = refs[:pg]
    vpages = refs[pg:2 * pg]
    o_ref = refs[2 * pg]
    m_scr, l_scr, acc_scr = refs[2 * pg + 1:]
    g = pl.program_id(1)
    scale = C_HEAD ** -0.5
    page = kpages[0].shape[0] // C_KV_HEADS

    @pl.when(g == 0)
    def _():
        m_scr[...] = jnp.full(m_scr.shape, NEG_INF, F32)
        l_scr[...] = jnp.zeros(l_scr.shape, F32)
        acc_scr[...] = jnp.zeros(acc_scr.shape, F32)
        for hk in range(C_KV_HEADS):
            cols = slice(hk * C_HEAD, (hk + 1) * C_HEAD)
            q16 = (q_ref[hk] * scale).astype(BF16)
            s = _dot_nt(q16, kn_ref[:, cols].astype(BF16)) + biasn_ref[...]
            _softmax_update(m_scr, l_scr, acc_scr, hk, s, [vn_ref[:, cols].astype(BF16)])

    for hk in range(C_KV_HEADS):
        q16 = (q_ref[hk] * scale).astype(BF16)
        rows = pl.ds(hk, page, stride=C_KV_HEADS)
        tiles = [_dot_nt(q16, kpages[i][rows, :].astype(BF16)) for i in range(pg)]
        s = jnp.concatenate(tiles, axis=1) + bias_ref[...]
        _softmax_update(m_scr, l_scr, acc_scr, hk, s, [vpages[i][rows, :].astype(BF16) for i in range(pg)])

    @pl.when(g == pl.num_programs(1) - 1)
    def _():
        for hk in range(C_KV_HEADS):
            o_ref[hk] = (acc_scr[hk] / l_scr[hk]) * _silu(z_ref[hk])


def _dsa_sample(p3s, cache_k, cache_v, cache_kidx, page_table, layer):
    b, t, _ = p3s.shape
    n_pages = page_table.shape[1]
    page = cache_k.shape[2]
    assert page == 128
    past = n_pages * page
    topk = min(TOPK_MAX, (past + t) // 4)
    pg = 16 if n_pages % 16 == 0 else n_pages
    ng = n_pages // pg
    rows = t * C_GROUPS

    qi = p3s[:, :, OFF["qi"]:OFF["qi"] + IDX_HEADS * IDX_DIM].reshape(b, t * IDX_HEADS, IDX_DIM)
    wi = p3s[:, :, OFF_TAIL + TAIL_WI:OFF_TAIL + TAIL_WI + IDX_HEADS].reshape(b, t * IDX_HEADS, 1)
    ki_new = jnp.pad(p3s[:, :, OFF_TAIL:OFF_TAIL + IDX_DIM], ((0, 0), (0, 128 - t), (0, 0)))
    k_new = jnp.pad(p3s[:, :, OFF["kc"]:OFF["kc"] + C_KV], ((0, 0), (0, 128 - t), (0, 0)))
    v_new = jnp.pad(p3s[:, :, OFF["vc"]:OFF["vc"] + C_KV], ((0, 0), (0, 128 - t), (0, 0)))

    def heads_major(a):
        a = a.reshape(b, t, C_KV_HEADS, C_GROUPS, C_HEAD)
        return jnp.transpose(a, (0, 2, 1, 3, 4)).reshape(b, C_KV_HEADS, rows, C_HEAD)

    qh = heads_major(p3s[:, :, OFF["qc"]:OFF["qc"] + W_C])
    zh = heads_major(p3s[:, :, OFF["zc"]:OFF["zc"] + W_C])

    kidx_t = jnp.swapaxes(cache_kidx, 2, 3)
    score_spec = pltpu.PrefetchScalarGridSpec(
        num_scalar_prefetch=1,
        grid=(b, ng),
        in_specs=[pl.BlockSpec((None, t * IDX_HEADS, IDX_DIM), lambda i, g, pt: (i, 0, 0)),
                  pl.BlockSpec((None, t * IDX_HEADS, 1), lambda i, g, pt: (i, 0, 0))]
                 + [pl.BlockSpec((None, None, IDX_DIM, page),
                                 lambda i, g, pt, j=j: (layer, pt[i, g * pg + j], 0, 0)) for j in range(pg)],
        out_specs=pl.BlockSpec((None, t, pg * page), lambda i, g, pt: (i, 0, g)),
    )
    scores = pl.pallas_call(
        functools.partial(_dsa_s_score_kernel, pg=pg, t=t),
        grid_spec=score_spec,
        out_shape=jax.ShapeDtypeStruct((b, t, past), F32),
        compiler_params=_cp(("arbitrary", "arbitrary"), 32),
        name="dsa_s_score",
    )(page_table, qi, wi, *([kidx_t] * pg))

    bias = pl.pallas_call(
        functools.partial(_dsa_s_select_kernel, t=t, nk=past, topk=topk),
        grid=(b,),
        in_specs=[pl.BlockSpec((None, t, past), lambda i: (i, 0, 0)),
                  pl.BlockSpec((None, t * IDX_HEADS, IDX_DIM), lambda i: (i, 0, 0)),
                  pl.BlockSpec((None, t * IDX_HEADS, 1), lambda i: (i, 0, 0)),
                  pl.BlockSpec((None, 128, IDX_DIM), lambda i: (i, 0, 0))],
        out_specs=pl.BlockSpec((None, t, past + 128), lambda i: (i, 0, 0)),
        out_shape=jax.ShapeDtypeStruct((b, t, past + 128), F32),
        compiler_params=_cp(("arbitrary",), 32),
        name="dsa_s_select",
    )(scores, qi, wi, ki_new)

    bias_rows = jnp.repeat(bias, C_GROUPS, axis=1)
    bias_past = bias_rows[:, :, :past]
    bias_new = bias_rows[:, :, past:]

    ck = cache_k.reshape(cache_k.shape[0], cache_k.shape[1], page * C_KV_HEADS, C_HEAD)
    cv = cache_v.reshape(cache_v.shape[0], cache_v.shape[1], page * C_KV_HEADS, C_HEAD)
    pspec = [pl.BlockSpec((None, None, page * C_KV_HEADS, C_HEAD),
                          lambda i, g, pt, j=j: (layer, pt[i, g * pg + j], 0, 0)) for j in range(pg)]
    attend_spec = pltpu.PrefetchScalarGridSpec(
        num_scalar_prefetch=1,
        grid=(b, ng),
        in_specs=[pl.BlockSpec((None, C_KV_HEADS, rows, C_HEAD), lambda i, g, pt: (i, 0, 0, 0)),
                  pl.BlockSpec((None, rows, pg * page), lambda i, g, pt: (i, 0, g)),
                  pl.BlockSpec((None, rows, 128), lambda i, g, pt: (i, 0, 0)),
                  pl.BlockSpec((None, 128, C_KV), lambda i, g, pt: (i, 0, 0)),
                  pl.BlockSpec((None, 128, C_KV), lambda i, g, pt: (i, 0, 0)),
                  pl.BlockSpec((None, C_KV_HEADS, rows, C_HEAD), lambda i, g, pt: (i, 0, 0, 0))]
                 + pspec + pspec,
        out_specs=pl.BlockSpec((None, C_KV_HEADS, rows, C_HEAD), lambda i, g, pt: (i, 0, 0, 0)),
        scratch_shapes=[pltpu.VMEM((C_KV_HEADS, rows, 1), F32), pltpu.VMEM((C_KV_HEADS, rows, 1), F32),
                        pltpu.VMEM((C_KV_HEADS, rows, C_HEAD), F32)],
    )
    oh = pl.pallas_call(
        functools.partial(_dsa_s_attend_kernel, pg=pg),
        grid_spec=attend_spec,
        out_shape=jax.ShapeDtypeStruct((b, C_KV_HEADS, rows, C_HEAD), F32),
        compiler_params=_cp(("arbitrary", "arbitrary"), 40),
        name="dsa_s_attend",
    )(page_table, qh, bias_past, bias_new, k_new, v_new, zh, *([ck] * pg), *([cv] * pg))
    y = jnp.transpose(oh.reshape(b, C_KV_HEADS, t, C_GROUPS, C_HEAD), (0, 2, 1, 3, 4)).reshape(b, t, W_C)
    return y.astype(BF16)


def _outproj_kernel(ya_ref, yb_ref, yc_ref, w_ref, x_ref, gate_ref, o_ref, mix_scr):
    @pl.when(pl.program_id(1) == 0)
    def _():
        mix_scr[:, 0:W_A] = ya_ref[...]
        mix_scr[:, W_A:W_A + W_B] = yb_ref[...]
        mix_scr[:, W_A + W_B:] = yc_ref[...]

    o_ref[...] = x_ref[...] + gate_ref[...] * _dot(mix_scr[...], w_ref[...])


def _outproj(ya, yb, yc, w_bf16, layer, x2, gate, *, rows_per_batch, mod_row0):
    m, d = x2.shape
    tn = 1024
    if gate.ndim == 3:
        tm = 512
        gate_spec = pl.BlockSpec((None, 1, tn),
                                 lambda i, j: ((i * tm) // rows_per_batch + mod_row0, 0, 2 * (d // tn) + j))
    else:
        tm = m
        gate_spec = pl.BlockSpec((tm, tn), lambda i, j: (i, j))
    assert m % tm == 0
    return pl.pallas_call(
        _outproj_kernel,
        grid=(m // tm, d // tn),
        in_specs=[pl.BlockSpec((tm, W_A), lambda i, j: (i, 0)),
                  pl.BlockSpec((tm, W_B), lambda i, j: (i, 0)),
                  pl.BlockSpec((tm, W_C), lambda i, j: (i, 0)),
                  pl.BlockSpec((None, d, tn), lambda i, j: (layer, 0, j)),
                  pl.BlockSpec((tm, tn), lambda i, j: (i, j)),
                  gate_spec],
        out_specs=pl.BlockSpec((tm, tn), lambda i, j: (i, j)),
        out_shape=jax.ShapeDtypeStruct((m, d), F32),
        scratch_shapes=[pltpu.VMEM((tm, d), BF16)],
        compiler_params=_cp(("arbitrary", "arbitrary"), 48),
        name="outproj",
    )(ya, yb, yc, w_bf16, x2, gate)


def _final_norm_kernel(x_ref, g_ref, o_ref):
    x = x_ref[...]
    o_ref[...] = x * lax.rsqrt(jnp.mean(x * x, axis=-1, keepdims=True) + EPS) * g_ref[...]


def _final_norm(x2, g):
    m, d = x2.shape
    tm = 256 if m % 256 == 0 else m
    return pl.pallas_call(
        _final_norm_kernel,
        grid=(m // tm,),
        in_specs=[pl.BlockSpec((tm, d), lambda i: (i, 0)), pl.BlockSpec((1, d), lambda i: (0, 0))],
        out_specs=pl.BlockSpec((tm, d), lambda i: (i, 0)),
        out_shape=jax.ShapeDtypeStruct((m, d), F32),
        compiler_params=_cp(("arbitrary",), 32),
        name="final_norm",
    )(x2, g.reshape(1, d))


def _w_in_tile_table():
    n_t = NP // 128
    table, n_plain = [], None
    for dt in range(n_t - 1):
        c = dt * 128
        seg = [n for n in _DST_ORDER if OFF[n] <= c < OFF[n] + _SRC_OFF[n][1]][0]
        src = _SRC_OFF[seg][0] + (c - OFF[seg])
        if src % 128 == 0:
            assert n_plain is None
        else:
            assert src % 128 == 32
            if n_plain is None:
                n_plain = dt
        table.append(src // 128)
    t_kw, t_ab = _SRC_OFF["ki"][0] // 128, _SRC_OFF["a"][0] // 128
    assert _SRC_OFF["ki"][0] % 128 == 32 and _SRC_OFF["wi"][0] == t_kw * 128 + TAIL_WI
    assert _SRC_OFF["a"][0] % 128 == 0 and _SRC_OFF["b"][0] == _SRC_OFF["a"][0] + DN_HEADS
    assert TAIL_A == IDX_DIM and TAIL_B == TAIL_A + DN_HEADS and TAIL_WI == TAIL_B + DN_HEADS
    tile_a = table + [t_kw]
    rows_b = [t_ab * 4] * n_plain + [(s + 1) * 4 for s in table[n_plain:]] + [t_ab * 4]
    return tile_a, rows_b, n_plain


def _wprep_kernel(ta_ref, tb_ref, a_ref, b_ref, o_ref, *, n_plain, n_t):
    dt = pl.program_id(1)
    dt_o = o_ref.dtype

    @pl.when(dt < n_plain)
    def _():
        o_ref[...] = a_ref[...].astype(dt_o)

    @pl.when((dt >= n_plain) & (dt < n_t - 1))
    def _():
        o_ref[0:96, :] = a_ref[32:128, :].astype(dt_o)
        o_ref[96:128, :] = b_ref[...].astype(dt_o)

    @pl.when(dt == n_t - 1)
    def _():
        o_ref[0:TAIL_A, :] = a_ref[32:32 + IDX_DIM, :].astype(dt_o)
        o_ref[TAIL_A:TAIL_WI, :] = b_ref[...].astype(dt_o)
        o_ref[TAIL_WI:TAIL_WI + IDX_HEADS, :] = a_ref[TAIL_WI:TAIL_WI + IDX_HEADS, :].astype(dt_o)
        o_ref[TAIL_WI + IDX_HEADS:, :] = jnp.zeros((128 - TAIL_WI - IDX_HEADS, o_ref.shape[1]), dt_o)


def _permute_w_in(w_in):
    depth, d, _ = w_in.shape
    w_t = jnp.swapaxes(w_in, 1, 2)
    tile_a, rows_b, n_plain = _w_in_tile_table()
    n_t = NP // 128
    grid_spec = pltpu.PrefetchScalarGridSpec(
        num_scalar_prefetch=2,
        grid=(depth, n_t),
        in_specs=[pl.BlockSpec((None, 128, d), lambda l, t, ta, tb: (l, ta[t], 0)),
                  pl.BlockSpec((None, 32, d), lambda l, t, ta, tb: (l, tb[t], 0))],
        out_specs=pl.BlockSpec((None, 128, d), lambda l, t, ta, tb: (l, t, 0)),
    )
    return pl.pallas_call(
        functools.partial(_wprep_kernel, n_plain=n_plain, n_t=n_t),
        grid_spec=grid_spec,
        out_shape=jax.ShapeDtypeStruct((depth, NP, d), BF16),
        compiler_params=_cp(("arbitrary", "arbitrary"), 32),
        name="w_in_layout",
    )(jnp.asarray(tile_a, I32), jnp.asarray(rows_b, I32), w_t, w_t)


def _delta_t(l):
    for t in (128, 64):
        if l % t == 0:
            return t
    raise ValueError("sequence length must be a multiple of 64")


def kernel(x_prompt, x_sample, cache_k, cache_v, cache_kidx, state_dn, state_conv, page_table,
           c_prompt, c_sample, w_ada, b_ada, g_norm, w_in, a_vnorm, a_ws, a_bs, dn_conv_w,
           dn_a_log, dn_dt_bias, dn_onorm, w_out, g_final):
    bp, lp, d = x_prompt.shape
    bs, ls, _ = x_sample.shape
    depth = w_ada.shape[0]
    assert d == D_MODEL and w_in.shape[2] == D_IN
    assert CONV_K - 1 <= ls <= DN_CHUNK

    n_c = bp + bs
    c_rows = jnp.concatenate([c_prompt, c_sample], axis=0)
    r_pad = (-n_c) % 8
    if r_pad:
        c_rows = jnp.pad(c_rows, ((0, r_pad), (0, 0)))
    m_all = _ada(c_rows, w_ada, b_ada)

    xp = x_prompt.reshape(bp * lp, d)
    xs = x_sample.reshape(bs * ls, d)
    ls_pad = DN_CHUNK
    outs = {k: [] for k in ("pk", "pv", "pki", "pdn", "pconv", "sk", "sv", "ski", "sdn", "sconv", "samlp")}
    zeros_conv = jnp.zeros((bp, CONV_K - 1, DN_CONV_DIM), F32)
    zeros_state = jnp.zeros((bp, DN_HEADS, DN_HEAD, DN_HEAD), F32)

    w_in_all = _permute_w_in(w_in)
    w_out_all = w_out.astype(BF16)
    for l in range(depth):
        g_l = g_norm[l].reshape(1, d)
        m_l = m_all[l]
        m3 = m_l.reshape(m_l.shape[0], 1, 3 * d)
        ms = jnp.repeat(m_l[bp:bp + bs], ls, axis=0)

        pp = _inproj(xp, g_l, m3, m3, w_in_all, l, rows_per_batch=lp, mod_row0=0)
        p3 = pp.reshape(bp, lp, NP)
        (ya,) = _mixa(p3, a_vnorm[l], a_ws[l], a_bs[l], emit_va=False)
        abt = jnp.transpose(p3[:, :, OFF["a"]:OFF["a"] + 2 * DN_HEADS], (0, 2, 1))
        yb, s_p = _delta(p3, abt, dn_conv_w[l], zeros_conv, zeros_state, dn_a_log[l], dn_dt_bias[l],
                         dn_onorm[l], t=_delta_t(lp), valid_len=_delta_t(lp))
        yc = _dsa_prompt(p3)
        xp = _outproj(ya.reshape(bp * lp, W_A), yb.reshape(bp * lp, W_B), yc.reshape(bp * lp, W_C),
                      w_out_all, l, xp, m3, rows_per_batch=lp, mod_row0=0)
        outs["pk"].append(p3[:, :, OFF["kc"]:OFF["kc"] + C_KV].reshape(bp, lp, C_KV_HEADS, C_HEAD))
        outs["pv"].append(p3[:, :, OFF["vc"]:OFF["vc"] + C_KV].reshape(bp, lp, C_KV_HEADS, C_HEAD))
        outs["pki"].append(p3[:, :, OFF_TAIL:OFF_TAIL + IDX_DIM])
        outs["pdn"].append(s_p)
        outs["pconv"].append(p3[:, lp - (CONV_K - 1):, OFF["qkv"]:OFF["qkv"] + DN_CONV_DIM])

        ps = _inproj(xs, g_l, ms[:, d:2 * d], ms[:, 0:d], w_in_all, l, rows_per_batch=ls, mod_row0=bp)
        p3s = ps.reshape(bs, ls, NP)
        ya_s, va_s = _mixa(p3s, a_vnorm[l], a_ws[l], a_bs[l], emit_va=True)
        p3s_pad = jnp.pad(p3s, ((0, 0), (0, ls_pad - ls), (0, 0)))
        abt_s = jnp.transpose(p3s_pad[:, :, OFF["a"]:OFF["a"] + 2 * DN_HEADS], (0, 2, 1))
        yb_s, s_s = _delta(p3s_pad, abt_s, dn_conv_w[l], state_conv[l], state_dn[l], dn_a_log[l],
                           dn_dt_bias[l], dn_onorm[l], t=ls_pad, valid_len=ls)
        yc_s = _dsa_sample(p3s, cache_k, cache_v, cache_kidx, page_table, l)
        xs = _outproj(ya_s.reshape(bs * ls, W_A), yb_s[:, :ls].reshape(bs * ls, W_B),
                      yc_s.reshape(bs * ls, W_C), w_out_all, l, xs, ms[:, 2 * d:3 * d],
                      rows_per_batch=ls, mod_row0=bp)
        outs["sk"].append(p3s[:, :, OFF["kc"]:OFF["kc"] + C_KV].reshape(bs, ls, C_KV_HEADS, C_HEAD))
        outs["sv"].append(p3s[:, :, OFF["vc"]:OFF["vc"] + C_KV].reshape(bs, ls, C_KV_HEADS, C_HEAD))
        outs["ski"].append(p3s[:, :, OFF_TAIL:OFF_TAIL + IDX_DIM])
        outs["sdn"].append(s_s)
        outs["sconv"].append(p3s[:, ls - (CONV_K - 1):, OFF["qkv"]:OFF["qkv"] + DN_CONV_DIM])
        outs["samlp"].append(va_s)

    y_prompt = _final_norm(xp, g_final).reshape(bp, lp, d)
    y_sample = _final_norm(xs, g_final).reshape(bs, ls, d)
    st = jnp.stack
    return (y_prompt, y_sample, st(outs["pk"]), st(outs["pv"]), st(outs["pki"]), st(outs["pdn"]),
            st(outs["pconv"]), st(outs["sk"]), st(outs["sv"]), st(outs["ski"]), st(outs["sdn"]),
            st(outs["sconv"]), st(outs["samlp"]))
```

```python
import functools

import jax
import jax.numpy as jnp
from jax import lax
from jax.experimental import pallas as pl
from jax.experimental.pallas import tpu as pltpu

F32 = jnp.float32
BF16 = jnp.bfloat16
I32 = jnp.int32
EPS = 1e-6
INT_MIN = -(2 ** 31)
NEG_INF = float("-inf")

D_MODEL = 4096
W_A = D_MODEL // 4
A_GROUP = 128
A_HEADS = W_A // A_GROUP
A_CHUNK = 128
W_B = D_MODEL // 2
DN_HEAD = 128
DN_HEADS = W_B // DN_HEAD
CONV_K = 4
DN_CONV_DIM = 3 * W_B
DN_CHUNK = 64
W_C = D_MODEL - W_A - W_B
C_HEAD = 128
C_HEADS = W_C // C_HEAD
C_KV_HEADS = 2
C_GROUPS = C_HEADS // C_KV_HEADS
C_KV = C_KV_HEADS * C_HEAD
IDX_HEADS = 16
IDX_DIM = 64
TOPK_MAX = 256
Q_BLOCK = 128

_SRC_SPLITS = (W_A, W_A, W_A, DN_CONV_DIM, W_B, DN_HEADS, DN_HEADS,
               W_C, C_KV, C_KV, W_C, IDX_HEADS * IDX_DIM, IDX_DIM, IDX_HEADS)
_SRC_NAMES = ("u", "v", "za", "qkv", "zb", "a", "b", "qc", "kc", "vc", "zc", "qi", "ki", "wi")
_SRC_OFF = {}
_o = 0
for _n, _w in zip(_SRC_NAMES, _SRC_SPLITS):
    _SRC_OFF[_n] = (_o, _w)
    _o += _w
D_IN = _o

_DST_ORDER = ("u", "v", "za", "qkv", "zb", "qc", "zc", "qi", "kc", "vc", "ki", "a", "b", "wi")
OFF = {}
_o = 0
for _n in _DST_ORDER:
    OFF[_n] = _o
    _o += _SRC_OFF[_n][1]
NP = ((_o + 127) // 128) * 128
OFF_TAIL = OFF["ki"]
TAIL_A = OFF["a"] - OFF_TAIL
TAIL_B = OFF["b"] - OFF_TAIL
TAIL_WI = OFF["wi"] - OFF_TAIL

INPROJ_TN = 1536
DELTA_HEADS_PER_STEP = 8
DSA_KEY_TILE = 512


def _cp(sem, vmem_mb=48):
    return pltpu.CompilerParams(dimension_semantics=sem, vmem_limit_bytes=vmem_mb * 1024 * 1024)


def _silu(x):
    return (0.5 * x) * (1.0 + jnp.tanh(0.5 * x))


def _gelu(x):
    return 0.5 * x * (1.0 + jnp.tanh(0.7978845608028654 * (x + 0.044715 * (x * x * x))))


def _dot(a, b):
    return jnp.dot(a, b, preferred_element_type=F32)


def _dot_nt(a, b):
    return lax.dot_general(a, b, (((1,), (1,)), ((), ())), preferred_element_type=F32)


def _dot_tn(a, b):
    return lax.dot_general(a, b, (((0,), (0,)), ((), ())), preferred_element_type=F32)


def _b16(a):
    return a.astype(BF16)


def _ada_kernel(c_ref, w_ref, b_ref, o_ref):
    s = _silu(c_ref[...]).astype(BF16)
    o_ref[...] = _dot(s, w_ref[...].astype(BF16)) + b_ref[...]


def _ada(c_rows, w_ada, b_ada):
    depth, d, n = w_ada.shape
    r = c_rows.shape[0]
    tn = 512
    return pl.pallas_call(
        _ada_kernel,
        grid=(depth, n // tn),
        in_specs=[pl.BlockSpec((r, d), lambda l, j: (0, 0)),
                  pl.BlockSpec((None, d, tn), lambda l, j: (l, 0, j)),
                  pl.BlockSpec((None, 1, tn), lambda l, j: (l, 0, j))],
        out_specs=pl.BlockSpec((None, r, tn), lambda l, j: (l, 0, j)),
        out_shape=jax.ShapeDtypeStruct((depth, r, n), F32),
        compiler_params=_cp(("arbitrary", "arbitrary"), 40),
        name="ada",
    )(c_rows, w_ada, b_ada.reshape(depth, 1, n))


def _inproj_kernel(x_ref, g_ref, sc_ref, sh_ref, w_ref, o_ref, h_scr, *, rc):
    @pl.when(pl.program_id(1) == 0)
    def _():
        tm = x_ref.shape[0]
        per_row = sc_ref.shape[0] != 1

        def body(c, carry):
            r = pl.ds(pl.multiple_of(c * rc, rc), rc)
            x = x_ref[r, :]
            y = x * lax.rsqrt(jnp.mean(x * x, axis=-1, keepdims=True) + EPS)
            sc = sc_ref[r, :] if per_row else sc_ref[...]
            sh = sh_ref[r, :] if per_row else sh_ref[...]
            h_scr[r, :] = ((y * g_ref[...]) * (1.0 + sc) + sh).astype(BF16)
            return carry

        lax.fori_loop(0, tm // rc, body, 0)

    o_ref[...] = _dot_nt(h_scr[...], w_ref[...])


def _inproj(x2, g, sc, sh, w_bf16, layer, *, rows_per_batch, mod_row0):
    m, d = x2.shape
    n = w_bf16.shape[1]
    tn = INPROJ_TN
    if sc.ndim == 3:
        tm = 512
        assert rows_per_batch % tm == 0
        sc_spec = pl.BlockSpec((None, 1, d), lambda i, j: ((i * tm) // rows_per_batch + mod_row0, 0, 1))
        sh_spec = pl.BlockSpec((None, 1, d), lambda i, j: ((i * tm) // rows_per_batch + mod_row0, 0, 0))
        rc = 64
    else:
        tm = m
        sc_spec = pl.BlockSpec((tm, d), lambda i, j: (i, 0))
        sh_spec = pl.BlockSpec((tm, d), lambda i, j: (i, 0))
        rc = tm
    assert m % tm == 0
    return pl.pallas_call(
        functools.partial(_inproj_kernel, rc=rc),
        grid=(m // tm, pl.cdiv(n, tn)),
        in_specs=[pl.BlockSpec((tm, d), lambda i, j: (i, 0)),
                  pl.BlockSpec((1, d), lambda i, j: (0, 0)),
                  sc_spec, sh_spec,
                  pl.BlockSpec((None, tn, d), lambda i, j: (layer, j, 0))],
        out_specs=pl.BlockSpec((tm, tn), lambda i, j: (i, j)),
        out_shape=jax.ShapeDtypeStruct((m, n), F32),
        scratch_shapes=[pltpu.VMEM((tm, d), BF16)],
        compiler_params=_cp(("arbitrary", "arbitrary"), 56),
        name="inproj",
    )(x2, g, sc, sh, w_bf16)


def _mixa_kernel(u_ref, v_ref, z_ref, vn_ref, ws_ref, bst_ref, y_ref, *rest, c, emit_va):
    u = _gelu(u_ref[...])
    v = _gelu(v_ref[...])
    mu = jnp.mean(v, axis=-1, keepdims=True)
    dv = v - mu
    va = dv * lax.rsqrt(jnp.mean(dv * dv, axis=-1, keepdims=True) + EPS) * vn_ref[...]
    if emit_va:
        rest[0][...] = va
    z = _silu(z_ref[...])
    row = lax.broadcasted_iota(I32, (c, c), 0)
    col = lax.broadcasted_iota(I32, (c, c), 1)
    tril = col <= row
    for h in range(A_HEADS):
        cols = slice(h * A_GROUP, (h + 1) * A_GROUP)
        wm = jnp.where(tril, ws_ref[h], 0.0)
        vh = va[:, cols]
        if c >= 128:
            mixed = _dot(wm.astype(BF16), vh.astype(BF16))
        else:
            mixed = wm[:, 0:1] * vh[0:1, :]
            for s in range(1, c):
                mixed = mixed + wm[:, s:s + 1] * vh[s:s + 1, :]
        mixed = mixed + bst_ref[:, h:h + 1]
        y_ref[:, cols] = (u[:, cols] * mixed * z[:, cols]).astype(y_ref.dtype)


def _mixa(p3, a_vnorm, a_ws, a_bs, *, emit_va):
    b, l, _ = p3.shape
    c = min(A_CHUNK, l)
    n = l // c
    ws = a_ws[:, :c, :c]
    bst = a_bs[:, :c].T
    wblk = W_A
    outs = [jax.ShapeDtypeStruct((b, l, W_A), BF16)]
    out_specs = [pl.BlockSpec((None, c, W_A), lambda i, j: (i, j, 0))]
    if emit_va:
        outs.append(jax.ShapeDtypeStruct((b, l, W_A), F32))
        out_specs.append(pl.BlockSpec((None, c, W_A), lambda i, j: (i, j, 0)))
    res = pl.pallas_call(
        functools.partial(_mixa_kernel, c=c, emit_va=emit_va),
        grid=(b, n),
        in_specs=[pl.BlockSpec((None, c, wblk), lambda i, j: (i, j, OFF["u"] // wblk)),
                  pl.BlockSpec((None, c, wblk), lambda i, j: (i, j, OFF["v"] // wblk)),
                  pl.BlockSpec((None, c, wblk), lambda i, j: (i, j, OFF["za"] // wblk)),
                  pl.BlockSpec((1, W_A), lambda i, j: (0, 0)),
                  pl.BlockSpec((A_HEADS, c, c), lambda i, j: (0, 0, 0)),
                  pl.BlockSpec((c, A_HEADS), lambda i, j: (0, 0))],
        out_specs=out_specs,
        out_shape=outs,
        compiler_params=_cp(("arbitrary", "arbitrary"), 32),
        name="mixa",
    )(p3, p3, p3, a_vnorm.reshape(1, W_A), ws, bst)
    return res


def _delta_kernel(alog_ref, dtb_ref,
                  q_ref, k_ref, v_ref, z_ref, ab_ref,
                  cwq_ref, cwk_ref, cwv_ref, cpq_ref, cpk_ref, cpv_ref,
                  s0_ref, on_ref,
                  y_ref, sout_ref,
                  xbuf, s_scr, *, t, c, hp, valid_len):
    hg = pl.program_id(1)
    n = pl.program_id(2)
    nlast = pl.num_programs(2) - 1
    dh = DN_HEAD

    @pl.when(n == 0)
    def _():
        s_scr[...] = s0_ref[...]
        for j in range(hp):
            lanes = slice(j * dh, (j + 1) * dh)
            xbuf[3 * j + 0, 5:8, :] = cpq_ref[:, lanes]
            xbuf[3 * j + 1, 5:8, :] = cpk_ref[:, lanes]
            xbuf[3 * j + 2, 5:8, :] = cpv_ref[:, lanes]

    row = lax.broadcasted_iota(I32, (t, t), 0)
    col = lax.broadcasted_iota(I32, (t, t), 1)
    shift = c.bit_length() - 1
    same = (row >> shift) == (col >> shift)
    eye = row == col
    incl = same & (col <= row)
    strict = same & (col < row)
    incl_t = same & (row <= col)
    blk8 = (row >> 3) == (col >> 3)
    off_masks = []
    bs = 8
    while bs < c:
        sh_b = bs.bit_length() - 1
        inner = (row >> sh_b) == (col >> sh_b)
        outer = (row >> (sh_b + 1)) == (col >> (sh_b + 1))
        off_masks.append(outer & jnp.logical_not(inner))
        bs *= 2
    eye_f = jnp.where(eye, 1.0, 0.0)
    if valid_len < t:
        lane_valid = lax.broadcasted_iota(I32, (1, t), 1) < valid_len
        sub_valid = lax.broadcasted_iota(I32, (t, 1), 0) < valid_len

    def to_col(r):
        return jnp.sum(jnp.where(eye, r, 0.0), axis=1, keepdims=True)

    def conv(idx, x, w_ref, lanes):
        xbuf[idx, 8:8 + t, :] = x
        y = xbuf[idx, pl.ds(5, t), :] * w_ref[0:1, lanes]
        for jj in range(1, CONV_K):
            y = y + xbuf[idx, pl.ds(5 + jj, t), :] * w_ref[jj:jj + 1, lanes]
        xbuf[idx, 5:8, :] = x[t - 3:t, :]
        return _silu(y)

    heads = range(hp)
    lanes_of = [slice(j * dh, (j + 1) * dh) for j in heads]

    def per_head(f, *lists):
        return [f(*vals) for vals in zip(*lists)]

    q = [conv(3 * j + 0, q_ref[:, lanes_of[j]], cwq_ref, lanes_of[j]) for j in heads]
    k = [conv(3 * j + 1, k_ref[:, lanes_of[j]], cwk_ref, lanes_of[j]) for j in heads]
    v = [conv(3 * j + 2, v_ref[:, lanes_of[j]], cwv_ref, lanes_of[j]) for j in heads]
    qc = per_head(lambda a: a * lax.rsqrt(jnp.sum(a * a, axis=-1, keepdims=True) + EPS) * (dh ** -0.5), q)
    kc = per_head(lambda a: a * lax.rsqrt(jnp.sum(a * a, axis=-1, keepdims=True) + EPS), k)

    def gates(j):
        h = hg * hp + j
        a_row = ab_ref[pl.ds(h, 1), :]
        b_row = ab_ref[pl.ds(DN_HEADS + h, 1), :]
        xa = a_row + dtb_ref[h]
        softplus = jnp.maximum(xa, 0.0) + jnp.log(1.0 + jnp.exp(-jnp.abs(xa)))
        a_coef = jnp.exp(jnp.zeros((1, 1), F32) + alog_ref[h])
        g_row = -a_coef * softplus
        beta_row = 1.0 / (1.0 + jnp.exp(-b_row))
        if valid_len < t:
            g_row = jnp.where(lane_valid, g_row, 0.0)
            beta_row = jnp.where(lane_valid, beta_row, 0.0)
        return g_row, beta_row

    g_row, beta_row = zip(*[gates(j) for j in heads])
    if valid_len < t:
        kc = per_head(lambda a: jnp.where(sub_valid, a, 0.0), kc)
        v = per_head(lambda a: jnp.where(sub_valid, a, 0.0), v)

    g_col = per_head(to_col, g_row)
    beta_col = per_head(to_col, beta_row)
    gc_col = per_head(lambda r: jnp.sum(jnp.where(incl, r, 0.0), axis=1, keepdims=True), g_row)
    glast_col = per_head(lambda r: jnp.sum(jnp.where(same, r, 0.0), axis=1, keepdims=True), g_row)
    gc_row = per_head(lambda cl: jnp.sum(jnp.where(incl_t, cl, 0.0), axis=0, keepdims=True), g_col)
    decay = per_head(lambda gc, gr: jnp.where(incl, jnp.exp(jnp.where(incl, gc - gr, 0.0)), 0.0), gc_col, gc_row)
    kb = per_head(lambda a, b: a * b, kc, beta_col)
    kc16 = per_head(lambda a: a.astype(BF16), kc)
    a_mat = per_head(lambda a, b, dc: jnp.where(strict, _dot_nt(a.astype(BF16), b) * dc, 0.0), kb, kc16, decay)
    attn = per_head(lambda a, b, dc: (_dot_nt(a.astype(BF16), b) * dc).astype(BF16), qc, kc16, decay)
    eg = per_head(jnp.exp, gc_col)
    rhs = per_head(lambda vv, bc, kk, e: _b16(jnp.concatenate([vv * bc, kk * e], axis=1)),
                   v, beta_col, kb, eg)

    n0f = per_head(lambda a: jnp.where(blk8, -a, 0.0), a_mat)
    n0 = per_head(_b16, n0f)
    n2 = per_head(lambda a: _b16(_dot(a, a)), n0)
    n4 = per_head(lambda a: _b16(_dot(a, a)), n2)
    x = per_head(lambda a: eye_f + a, n0f)
    x = per_head(lambda xx, nn: xx + _dot(_b16(xx), nn), x, n2)
    x = per_head(lambda xx, nn: xx + _dot(_b16(xx), nn), x, n4)
    for om in off_masks:
        xs = per_head(_b16, x)
        xa_off = per_head(lambda s_, a: _b16(_dot(s_, _b16(jnp.where(om, a, 0.0)))), xs, a_mat)
        x = per_head(lambda xx, xo, s_: xx - _dot(xo, s_), x, xa_off, xs)
    sol16 = per_head(lambda xx, r: _b16(_dot(_b16(xx), r)), x, rhs)
    auw = per_head(_dot, attn, sol16)
    qw = per_head(lambda a, e, m_: (a * e - m_[:, dh:]).astype(BF16), qc, eg, auw)
    kg16 = per_head(lambda a, gl_, gc: (a * jnp.exp(gl_ - gc)).astype(BF16), kc, glast_col, gc_col)

    s = [s_scr[j] for j in heads]
    outs = [[] for _ in heads]
    for i in range(t // c):
        rows = slice(i * c, (i + 1) * c)
        s16 = per_head(lambda a: a.astype(BF16), s)
        kuw = per_head(lambda a, b: _dot_tn(a[rows], b[rows]), kg16, sol16)
        for j in heads:
            outs[j].append(_dot(qw[j][rows], s16[j]) + auw[j][rows, :dh])
        gl = per_head(lambda a: jnp.exp(a[i * c:i * c + 1, :]), glast_col)
        s = per_head(lambda g_, s_, m_, b16: g_ * s_ + m_[:, :dh] - _dot(m_[:, dh:].astype(BF16), b16),
                     gl, s, kuw, s16)
    for j in heads:
        s_scr[j] = s[j]
        o = outs[j][0] if len(outs[j]) == 1 else jnp.concatenate(outs[j], axis=0)
        on = o * lax.rsqrt(jnp.mean(o * o, axis=-1, keepdims=True) + EPS) * on_ref[...]
        y_ref[:, lanes_of[j]] = (on * _silu(z_ref[:, lanes_of[j]])).astype(y_ref.dtype)

    @pl.when(n == nlast)
    def _():
        sout_ref[...] = s_scr[...]


def _delta(p3, abt, conv_w, conv_prev, s0, a_log, dt_bias, onorm, *, t, valid_len):
    b, l, _ = p3.shape
    assert l % t == 0
    c = min(DN_CHUNK, t)
    hp = DELTA_HEADS_PER_STEP
    nh = DN_HEADS
    ng = nh // hp
    w = 128 * hp
    cb_q = OFF["qkv"] // w
    cb_z = OFF["zb"] // w
    assert OFF["qkv"] % w == 0 and OFF["zb"] % w == 0 and W_B % w == 0

    def pspec(cb0):
        return pl.BlockSpec((None, t, w), lambda i, h, n, a, d: (i, n, cb0 + h))

    def cwspec(sidx):
        return pl.BlockSpec((CONV_K, w), lambda i, h, n, a, d: (0, sidx * ng + h))

    def cpspec(sidx):
        return pl.BlockSpec((None, CONV_K - 1, w), lambda i, h, n, a, d: (i, 0, sidx * ng + h))

    grid_spec = pltpu.PrefetchScalarGridSpec(
        num_scalar_prefetch=2,
        grid=(b, ng, l // t),
        in_specs=[pspec(cb_q), pspec(cb_q + ng), pspec(cb_q + 2 * ng), pspec(cb_z),
                  pl.BlockSpec((None, 2 * nh, t), lambda i, h, n, a, d: (i, 0, n)),
                  cwspec(0), cwspec(1), cwspec(2), cpspec(0), cpspec(1), cpspec(2),
                  pl.BlockSpec((None, hp, DN_HEAD, DN_HEAD), lambda i, h, n, a, d: (i, h, 0, 0)),
                  pl.BlockSpec((1, DN_HEAD), lambda i, h, n, a, d: (0, 0))],
        out_specs=[pl.BlockSpec((None, t, w), lambda i, h, n, a, d: (i, n, h)),
                   pl.BlockSpec((None, hp, DN_HEAD, DN_HEAD), lambda i, h, n, a, d: (i, h, 0, 0))],
        scratch_shapes=[pltpu.VMEM((3 * hp, t + 8, 128), F32), pltpu.VMEM((hp, DN_HEAD, DN_HEAD), F32)],
    )
    return pl.pallas_call(
        functools.partial(_delta_kernel, t=t, c=c, hp=hp, valid_len=valid_len),
        grid_spec=grid_spec,
        out_shape=[jax.ShapeDtypeStruct((b, l, W_B), BF16),
                   jax.ShapeDtypeStruct((b, nh, DN_HEAD, DN_HEAD), F32)],
        compiler_params=_cp(("arbitrary", "arbitrary", "arbitrary"), 40),
        name="delta",
    )(a_log, dt_bias, p3, p3, p3, p3, abt, conv_w, conv_w, conv_w,
      conv_prev, conv_prev, conv_prev, s0, onorm.reshape(1, DN_HEAD))


def _ordered_bits_to_float(u):
    key = u ^ jnp.int32(INT_MIN)
    bits = jnp.where(key < 0, key ^ jnp.int32(0x7FFFFFFF), key)
    return pltpu.bitcast(bits, F32)


def _topk_threshold(count_ge, shape, topk):
    def bit_body(i, u):
        bit = jnp.left_shift(jnp.int32(1), 31 - i)
        cand_u = u | bit
        cnt = count_ge(_ordered_bits_to_float(cand_u))
        return jnp.where(cnt >= float(topk), cand_u, u)

    u = lax.fori_loop(0, 32, bit_body, jnp.zeros(shape, I32))
    return _ordered_bits_to_float(u)


def _softmax_update(m_ref, l_ref, acc_ref, idx, s, v_tiles):
    m_old = m_ref[idx]
    m_new = jnp.maximum(m_old, jnp.max(s, axis=1, keepdims=True))
    m_safe = jnp.where(m_new == NEG_INF, 0.0, m_new)
    alpha = jnp.exp(m_old - m_safe)
    p = jnp.exp(s - m_safe)
    l_ref[idx] = alpha * l_ref[idx] + jnp.sum(p, axis=1, keepdims=True)
    acc = alpha * acc_ref[idx]
    p16 = p.astype(BF16)
    k0 = 0
    for vt in v_tiles:
        acc = acc + _dot(p16[:, k0:k0 + vt.shape[0]], vt)
        k0 += vt.shape[0]
    acc_ref[idx] = acc
    m_ref[idx] = m_new


def _dsa_prompt_kernel(qi_ref, tq_ref, tall_ref, qc_ref, k_ref, v_ref, z_ref, y_ref,
                       sc_scr, m_scr, l_scr, acc_scr, *, l, topk, kt):
    qb = pl.program_id(1)
    nq = Q_BLOCK
    n_tiles = (qb * nq + nq + kt - 1) // kt
    n_pairs = C_HEADS // 2
    w_rows = tq_ref[...].T[TAIL_WI:TAIL_WI + IDX_HEADS, :] * ((IDX_HEADS ** -0.5) * (IDX_DIM ** -0.5))
    rq = []
    for j in range(IDX_HEADS // 2):
        a = qi_ref[:, j * 128:(j + 1) * 128]
        rq.append(jnp.concatenate([a, pltpu.roll(a, IDX_DIM, 1)], axis=0).astype(BF16))
    kpos = lax.broadcasted_iota(I32, (kt, nq), 0)
    qpos = qb * nq + lax.broadcasted_iota(I32, (kt, nq), 1)
    lane = lax.broadcasted_iota(I32, (kt, 128), 1)

    def score_tile(ti, carry):
        k0 = pl.multiple_of(ti * kt, kt)
        ki = jnp.where(lane < IDX_DIM, tall_ref[pl.ds(k0, kt), :], 0.0).astype(BF16)
        score = jnp.zeros((kt, nq), F32)
        for j in range(IDX_HEADS // 2):
            lg = _dot_nt(ki, rq[j])
            score = score + w_rows[2 * j:2 * j + 1, :] * jnp.maximum(lg[:, :nq], 0.0)
            score = score + w_rows[2 * j + 1:2 * j + 2, :] * jnp.maximum(lg[:, nq:], 0.0)
        sc_scr[pl.ds(k0, kt), :] = jnp.where(kpos + k0 <= qpos, score, NEG_INF)
        return carry

    lax.fori_loop(0, n_tiles, score_tile, 0)

    ct = 256
    n_ct = (qb * nq + nq + ct - 1) // ct
    n_acc = 4

    def count_ge(cand):
        cand8 = jnp.broadcast_to(cand, (8, nq))

        def tile_body(ti, accs):
            k0 = pl.multiple_of(ti * ct, ct)
            tile = sc_scr[pl.ds(k0, ct), :]
            accs = list(accs)
            for r in range(ct // 8):
                accs[r % n_acc] = accs[r % n_acc] + jnp.where(tile[8 * r:8 * r + 8, :] >= cand8, 1.0, 0.0)
            return tuple(accs)

        accs = lax.fori_loop(0, n_ct, tile_body, tuple(jnp.zeros((8, nq), F32) for _ in range(n_acc)))
        acc = (accs[0] + accs[1]) + (accs[2] + accs[3])
        return jnp.sum(acc, axis=0, keepdims=True)

    thr = _topk_threshold(count_ge, (1, nq), topk)
    keep_all = (qb * nq + lax.broadcasted_iota(I32, (1, nq), 1)) < topk

    m_scr[...] = jnp.full(m_scr.shape, NEG_INF, F32)
    l_scr[...] = jnp.zeros(l_scr.shape, F32)
    acc_scr[...] = jnp.zeros(acc_scr.shape, F32)
    scale = (C_HEAD ** -0.5) * 1.4426950408889634
    qp = []
    for pr in range(n_pairs):
        c0 = slice(2 * pr * C_HEAD, (2 * pr + 1) * C_HEAD)
        c1 = slice((2 * pr + 1) * C_HEAD, (2 * pr + 2) * C_HEAD)
        qp.append(jnp.concatenate([qc_ref[:, c0] * scale, qc_ref[:, c1] * scale], axis=0).astype(BF16))
    pairs_per_kv = n_pairs // C_KV_HEADS

    def attend_tile(ti, carry):
        k0 = pl.multiple_of(ti * kt, kt)
        sc = sc_scr[pl.ds(k0, kt), :]
        valid = (kpos + k0 <= qpos) & (keep_all | (sc >= thr))
        bias = jnp.where(valid, 0.0, NEG_INF)
        k16 = [k_ref[pl.ds(k0, kt), hk * C_HEAD:(hk + 1) * C_HEAD].astype(BF16) for hk in range(C_KV_HEADS)]
        vt16 = [v_ref[pl.ds(k0, kt), hk * C_HEAD:(hk + 1) * C_HEAD].T.astype(BF16) for hk in range(C_KV_HEADS)]
        bias2 = jnp.concatenate([bias, bias], axis=1)
        pr_all = range(n_pairs)
        s = [_dot_nt(k16[pr // pairs_per_kv], qp[pr]) + bias2 for pr in pr_all]
        m_old = [m_scr[pr] for pr in pr_all]
        m_new = [jnp.maximum(m_old[pr], jnp.max(s[pr], axis=0, keepdims=True)) for pr in pr_all]
        m_safe = [jnp.where(m_new[pr] == NEG_INF, 0.0, m_new[pr]) for pr in pr_all]
        alpha = [jnp.exp2(m_old[pr] - m_safe[pr]) for pr in pr_all]
        p = [jnp.exp2(s[pr] - m_safe[pr]) for pr in pr_all]
        for pr in pr_all:
            l_scr[pr] = alpha[pr] * l_scr[pr] + jnp.sum(p[pr], axis=0, keepdims=True)
            acc_scr[pr] = alpha[pr] * acc_scr[pr] + _dot(vt16[pr // pairs_per_kv], p[pr].astype(BF16))
            m_scr[pr] = m_new[pr]
        return carry

    lax.fori_loop(0, n_tiles, attend_tile, 0)
    for pr in range(n_pairs):
        o_t = acc_scr[pr] / l_scr[pr]
        for e in range(2):
            cols = slice((2 * pr + e) * C_HEAD, (2 * pr + e + 1) * C_HEAD)
            o = o_t[:, e * nq:(e + 1) * nq].T
            y_ref[:, cols] = (o * _silu(z_ref[:, cols])).astype(y_ref.dtype)


def _dsa_prompt(p3):
    b, l, _ = p3.shape
    assert l % Q_BLOCK == 0
    topk = min(TOPK_MAX, l // 4)
    nq = Q_BLOCK
    kt = DSA_KEY_TILE if l % DSA_KEY_TILE == 0 else 128
    return pl.pallas_call(
        functools.partial(_dsa_prompt_kernel, l=l, topk=topk, kt=kt),
        grid=(b, l // nq),
        in_specs=[pl.BlockSpec((None, nq, W_C), lambda i, j: (i, j, OFF["qi"] // W_C)),
                  pl.BlockSpec((None, nq, 128), lambda i, j: (i, j, OFF_TAIL // 128)),
                  pl.BlockSpec((None, l, 128), lambda i, j: (i, 0, OFF_TAIL // 128)),
                  pl.BlockSpec((None, nq, W_C), lambda i, j: (i, j, OFF["qc"] // W_C)),
                  pl.BlockSpec((None, l, C_KV), lambda i, j: (i, 0, OFF["kc"] // C_KV)),
                  pl.BlockSpec((None, l, C_KV), lambda i, j: (i, 0, OFF["vc"] // C_KV)),
                  pl.BlockSpec((None, nq, W_C), lambda i, j: (i, j, OFF["zc"] // W_C))],
        out_specs=pl.BlockSpec((None, nq, W_C), lambda i, j: (i, j, 0)),
        out_shape=jax.ShapeDtypeStruct((b, l, W_C), BF16),
        scratch_shapes=[pltpu.VMEM((l, nq), F32),
                        pltpu.VMEM((C_HEADS // 2, 1, 2 * nq), F32), pltpu.VMEM((C_HEADS // 2, 1, 2 * nq), F32),
                        pltpu.VMEM((C_HEADS // 2, C_HEAD, 2 * nq), F32)],
        compiler_params=_cp(("arbitrary", "arbitrary"), 48),
        name="dsa_prompt",
    )(p3, p3, p3, p3, p3, p3, p3)


def _idx_scores(logits, wcol, t):
    r = jnp.maximum(logits, 0.0) * wcol
    return jnp.sum(r.reshape(t, IDX_HEADS, logits.shape[1]), axis=1)


def _dsa_s_score_kernel(pt_ref, q_ref, w_ref, kn_ref, *refs, pg, t):
    pages = refs[:pg]
    out_ref, new_ref = refs[pg:]
    q16 = q_ref[...].astype(BF16)
    wcol = w_ref[...] * ((IDX_HEADS ** -0.5) * (IDX_DIM ** -0.5))
    for i in range(pg):
        out_ref[:, i * 128:(i + 1) * 128] = _idx_scores(_dot(q16, pages[i][...].astype(BF16)), wcol, t)

    @pl.when(pl.program_id(1) == 0)
    def _():
        new_ref[...] = _idx_scores(_dot_nt(q16, kn_ref[...].astype(BF16)), wcol, t)


def _dsa_s_select_kernel(sc_ref, scn_ref, tq_ref, bias_ref, *, nk, topk):
    rows = sc_ref.shape[0]
    tq = tq_ref[...]
    new_ok = lax.broadcasted_iota(I32, (rows, 128), 1) <= tq
    sc_new = jnp.where(new_ok, scn_ref[...], NEG_INF)
    sc_past = sc_ref[...]

    def count_ge(cand):
        c1 = jnp.sum(jnp.where(sc_past >= cand, 1.0, 0.0), axis=1, keepdims=True)
        c2 = jnp.sum(jnp.where(sc_new >= cand, 1.0, 0.0), axis=1, keepdims=True)
        return c1 + c2

    thr = _topk_threshold(count_ge, (rows, 1), topk)
    keep_all = (nk + 1 + tq[:, 0:1]) <= topk
    bias_ref[:, :nk] = jnp.where(keep_all | (sc_past >= thr), 0.0, NEG_INF)
    bias_ref[:, nk:] = jnp.where(new_ok & (keep_all | (sc_new >= thr)), 0.0, NEG_INF)


def _dsa_s_attend_kernel(pt_ref, q_ref, bias_ref, biasn_ref, kn_ref, vn_ref, z_ref, *refs, pg):
    kpages = refs[:pg]
    vpages = refs[pg:2 * pg]
    o_ref = refs[2 * pg]
    m_scr, l_scr, acc_scr = refs[2 * pg + 1:]
    g = pl.program_id(1)
    scale = C_HEAD ** -0.5
    page = kpages[0].shape[0] // C_KV_HEADS

    @pl.when(g == 0)
    def _():
        m_scr[...] = jnp.full(m_scr.shape, NEG_INF, F32)
        l_scr[...] = jnp.zeros(l_scr.shape, F32)
        acc_scr[...] = jnp.zeros(acc_scr.shape, F32)
        for hk in range(C_KV_HEADS):
            cols = slice(hk * C_HEAD, (hk + 1) * C_HEAD)
            q16 = (q_ref[hk] * scale).astype(BF16)
            s = _dot_nt(q16, kn_ref[:, cols].astype(BF16)) + biasn_ref[...]
            _softmax_update(m_scr, l_scr, acc_scr, hk, s, [vn_ref[:, cols].astype(BF16)])

    def two_pages(refs_, i, rows):
        return jnp.concatenate([refs_[i][rows, :], refs_[i + 1][rows, :]], axis=0).astype(BF16)

    for hk in range(C_KV_HEADS):
        q16 = (q_ref[hk] * scale).astype(BF16)
        rows = pl.ds(hk, page, stride=C_KV_HEADS)
        tiles = [_dot_nt(q16, two_pages(kpages, i, rows)) for i in range(0, pg, 2)]
        s = jnp.concatenate(tiles, axis=1) + bias_ref[...]
        _softmax_update(m_scr, l_scr, acc_scr, hk, s, [two_pages(vpages, i, rows) for i in range(0, pg, 2)])

    @pl.when(g == pl.num_programs(1) - 1)
    def _():
        for hk in range(C_KV_HEADS):
            o_ref[hk] = (acc_scr[hk] / l_scr[hk]) * _silu(z_ref[hk])


def _dsa_sample(p3s, cache_k, cache_v, cache_kidx, page_table, layer):
    b, t, _ = p3s.shape
    n_pages = page_table.shape[1]
    page = cache_k.shape[2]
    assert page == 128
    past = n_pages * page
    topk = min(TOPK_MAX, (past + t) // 4)
    pg = 32 if n_pages % 32 == 0 else (16 if n_pages % 16 == 0 else n_pages)
    assert pg % 2 == 0
    ng = n_pages // pg
    rows = t * C_GROUPS

    qi = p3s[:, :, OFF["qi"]:OFF["qi"] + IDX_HEADS * IDX_DIM].reshape(b, t * IDX_HEADS, IDX_DIM)
    wi = p3s[:, :, OFF_TAIL + TAIL_WI:OFF_TAIL + TAIL_WI + IDX_HEADS].reshape(b, t * IDX_HEADS, 1)
    ki_new = jnp.pad(p3s[:, :, OFF_TAIL:OFF_TAIL + IDX_DIM], ((0, 0), (0, 128 - t), (0, 0)))
    k_new = jnp.pad(p3s[:, :, OFF["kc"]:OFF["kc"] + C_KV], ((0, 0), (0, 128 - t), (0, 0)))
    v_new = jnp.pad(p3s[:, :, OFF["vc"]:OFF["vc"] + C_KV], ((0, 0), (0, 128 - t), (0, 0)))

    def heads_major(a):
        a = a.reshape(b, t, C_KV_HEADS, C_GROUPS, C_HEAD)
        return jnp.transpose(a, (0, 2, 1, 3, 4)).reshape(b, C_KV_HEADS, rows, C_HEAD)

    qh = heads_major(p3s[:, :, OFF["qc"]:OFF["qc"] + W_C])
    zh = heads_major(p3s[:, :, OFF["zc"]:OFF["zc"] + W_C])

    kidx_t = jnp.swapaxes(cache_kidx, 2, 3)
    score_spec = pltpu.PrefetchScalarGridSpec(
        num_scalar_prefetch=1,
        grid=(b, ng),
        in_specs=[pl.BlockSpec((None, t * IDX_HEADS, IDX_DIM), lambda i, g, pt: (i, 0, 0)),
                  pl.BlockSpec((None, t * IDX_HEADS, 1), lambda i, g, pt: (i, 0, 0)),
                  pl.BlockSpec((None, 128, IDX_DIM), lambda i, g, pt: (i, 0, 0))]
                 + [pl.BlockSpec((None, None, IDX_DIM, page),
                                 lambda i, g, pt, j=j: (layer, pt[i, g * pg + j], 0, 0)) for j in range(pg)],
        out_specs=[pl.BlockSpec((None, t, pg * page), lambda i, g, pt: (i, 0, g)),
                   pl.BlockSpec((None, t, 128), lambda i, g, pt: (i, 0, 0))],
    )
    scores, scores_new = pl.pallas_call(
        functools.partial(_dsa_s_score_kernel, pg=pg, t=t),
        grid_spec=score_spec,
        out_shape=[jax.ShapeDtypeStruct((b, t, past), F32), jax.ShapeDtypeStruct((b, t, 128), F32)],
        compiler_params=_cp(("arbitrary", "arbitrary"), 32),
        name="dsa_s_score",
    )(page_table, qi, wi, ki_new, *([kidx_t] * pg))

    tq = jnp.broadcast_to(jnp.tile(jnp.arange(t, dtype=I32), b)[:, None], (b * t, 128))
    bias = pl.pallas_call(
        functools.partial(_dsa_s_select_kernel, nk=past, topk=topk),
        grid=(1,),
        in_specs=[pl.BlockSpec((b * t, past), lambda i: (0, 0)),
                  pl.BlockSpec((b * t, 128), lambda i: (0, 0)),
                  pl.BlockSpec((b * t, 128), lambda i: (0, 0))],
        out_specs=pl.BlockSpec((b * t, past + 128), lambda i: (0, 0)),
        out_shape=jax.ShapeDtypeStruct((b * t, past + 128), F32),
        compiler_params=_cp(("arbitrary",), 40),
        name="dsa_s_select",
    )(scores.reshape(b * t, past), scores_new.reshape(b * t, 128), tq).reshape(b, t, past + 128)

    bias_rows = jnp.repeat(bias, C_GROUPS, axis=1)
    bias_past = bias_rows[:, :, :past]
    bias_new = bias_rows[:, :, past:]

    ck = cache_k.reshape(cache_k.shape[0], cache_k.shape[1], page * C_KV_HEADS, C_HEAD)
    cv = cache_v.reshape(cache_v.shape[0], cache_v.shape[1], page * C_KV_HEADS, C_HEAD)
    pspec = [pl.BlockSpec((None, None, page * C_KV_HEADS, C_HEAD),
                          lambda i, g, pt, j=j: (layer, pt[i, g * pg + j], 0, 0)) for j in range(pg)]
    attend_spec = pltpu.PrefetchScalarGridSpec(
        num_scalar_prefetch=1,
        grid=(b, ng),
        in_specs=[pl.BlockSpec((None, C_KV_HEADS, rows, C_HEAD), lambda i, g, pt: (i, 0, 0, 0)),
                  pl.BlockSpec((None, rows, pg * page), lambda i, g, pt: (i, 0, g)),
                  pl.BlockSpec((None, rows, 128), lambda i, g, pt: (i, 0, 0)),
                  pl.BlockSpec((None, 128, C_KV), lambda i, g, pt: (i, 0, 0)),
                  pl.BlockSpec((None, 128, C_KV), lambda i, g, pt: (i, 0, 0)),
                  pl.BlockSpec((None, C_KV_HEADS, rows, C_HEAD), lambda i, g, pt: (i, 0, 0, 0))]
                 + pspec + pspec,
        out_specs=pl.BlockSpec((None, C_KV_HEADS, rows, C_HEAD), lambda i, g, pt: (i, 0, 0, 0)),
        scratch_shapes=[pltpu.VMEM((C_KV_HEADS, rows, 1), F32), pltpu.VMEM((C_KV_HEADS, rows, 1), F32),
                        pltpu.VMEM((C_KV_HEADS, rows, C_HEAD), F32)],
    )
    oh = pl.pallas_call(
        functools.partial(_dsa_s_attend_kernel, pg=pg),
        grid_spec=attend_spec,
        out_shape=jax.ShapeDtypeStruct((b, C_KV_HEADS, rows, C_HEAD), F32),
        compiler_params=_cp(("arbitrary", "arbitrary"), 40),
        name="dsa_s_attend",
    )(page_table, qh, bias_past, bias_new, k_new, v_new, zh, *([ck] * pg), *([cv] * pg))
    y = jnp.transpose(oh.reshape(b, C_KV_HEADS, t, C_GROUPS, C_HEAD), (0, 2, 1, 3, 4)).reshape(b, t, W_C)
    return y.astype(BF16)


def _outproj_kernel(ya_ref, yb_ref, yc_ref, w_ref, x_ref, gate_ref, o_ref, mix_scr):
    @pl.when(pl.program_id(1) == 0)
    def _():
        mix_scr[:, 0:W_A] = ya_ref[...]
        mix_scr[:, W_A:W_A + W_B] = yb_ref[...]
        mix_scr[:, W_A + W_B:] = yc_ref[...]

    o_ref[...] = x_ref[...] + gate_ref[...] * _dot(mix_scr[...], w_ref[...])


def _outproj(ya, yb, yc, w_bf16, layer, x2, gate, *, rows_per_batch, mod_row0):
    m, d = x2.shape
    tn = 1024
    if gate.ndim == 3:
        tm = 512
        gate_spec = pl.BlockSpec((None, 1, tn),
                                 lambda i, j: ((i * tm) // rows_per_batch + mod_row0, 0, 2 * (d // tn) + j))
    else:
        tm = m
        gate_spec = pl.BlockSpec((tm, tn), lambda i, j: (i, j))
    assert m % tm == 0
    return pl.pallas_call(
        _outproj_kernel,
        grid=(m // tm, d // tn),
        in_specs=[pl.BlockSpec((tm, W_A), lambda i, j: (i, 0)),
                  pl.BlockSpec((tm, W_B), lambda i, j: (i, 0)),
                  pl.BlockSpec((tm, W_C), lambda i, j: (i, 0)),
                  pl.BlockSpec((None, d, tn), lambda i, j: (layer, 0, j)),
                  pl.BlockSpec((tm, tn), lambda i, j: (i, j)),
                  gate_spec],
        out_specs=pl.BlockSpec((tm, tn), lambda i, j: (i, j)),
        out_shape=jax.ShapeDtypeStruct((m, d), F32),
        scratch_shapes=[pltpu.VMEM((tm, d), BF16)],
        compiler_params=_cp(("arbitrary", "arbitrary"), 48),
        name="outproj",
    )(ya, yb, yc, w_bf16, x2, gate)


def _final_norm_kernel(x_ref, g_ref, o_ref):
    x = x_ref[...]
    o_ref[...] = x * lax.rsqrt(jnp.mean(x * x, axis=-1, keepdims=True) + EPS) * g_ref[...]


def _final_norm(x2, g):
    m, d = x2.shape
    tm = 256 if m % 256 == 0 else m
    return pl.pallas_call(
        _final_norm_kernel,
        grid=(m // tm,),
        in_specs=[pl.BlockSpec((tm, d), lambda i: (i, 0)), pl.BlockSpec((1, d), lambda i: (0, 0))],
        out_specs=pl.BlockSpec((tm, d), lambda i: (i, 0)),
        out_shape=jax.ShapeDtypeStruct((m, d), F32),
        compiler_params=_cp(("arbitrary",), 32),
        name="final_norm",
    )(x2, g.reshape(1, d))


def _w_in_tile_table():
    n_t = NP // 128
    table, n_plain = [], None
    for dt in range(n_t - 1):
        c = dt * 128
        seg = [n for n in _DST_ORDER if OFF[n] <= c < OFF[n] + _SRC_OFF[n][1]][0]
        src = _SRC_OFF[seg][0] + (c - OFF[seg])
        if src % 128 == 0:
            assert n_plain is None
        else:
            assert src % 128 == 32
            if n_plain is None:
                n_plain = dt
        table.append(src // 128)
    t_kw, t_ab = _SRC_OFF["ki"][0] // 128, _SRC_OFF["a"][0] // 128
    assert _SRC_OFF["ki"][0] % 128 == 32 and _SRC_OFF["wi"][0] == t_kw * 128 + TAIL_WI
    assert _SRC_OFF["a"][0] % 128 == 0 and _SRC_OFF["b"][0] == _SRC_OFF["a"][0] + DN_HEADS
    assert TAIL_A == IDX_DIM and TAIL_B == TAIL_A + DN_HEADS and TAIL_WI == TAIL_B + DN_HEADS
    tile_a = table + [t_kw]
    rows_b = [t_ab * 4] * n_plain + [(s + 1) * 4 for s in table[n_plain:]] + [t_ab * 4]
    return tile_a, rows_b, n_plain


def _wprep_kernel(ta_ref, tb_ref, a_ref, b_ref, o_ref, *, n_plain, n_t):
    dt = pl.program_id(1)
    dt_o = o_ref.dtype

    @pl.when(dt < n_plain)
    def _():
        o_ref[...] = a_ref[...].astype(dt_o)

    @pl.when((dt >= n_plain) & (dt < n_t - 1))
    def _():
        o_ref[0:96, :] = a_ref[32:128, :].astype(dt_o)
        o_ref[96:128, :] = b_ref[...].astype(dt_o)

    @pl.when(dt == n_t - 1)
    def _():
        o_ref[0:TAIL_A, :] = a_ref[32:32 + IDX_DIM, :].astype(dt_o)
        o_ref[TAIL_A:TAIL_WI, :] = b_ref[...].astype(dt_o)
        o_ref[TAIL_WI:TAIL_WI + IDX_HEADS, :] = a_ref[TAIL_WI:TAIL_WI + IDX_HEADS, :].astype(dt_o)
        o_ref[TAIL_WI + IDX_HEADS:, :] = jnp.zeros((128 - TAIL_WI - IDX_HEADS, o_ref.shape[1]), dt_o)


def _permute_w_in(w_in):
    depth, d, _ = w_in.shape
    w_t = jnp.swapaxes(w_in, 1, 2)
    tile_a, rows_b, n_plain = _w_in_tile_table()
    n_t = NP // 128
    grid_spec = pltpu.PrefetchScalarGridSpec(
        num_scalar_prefetch=2,
        grid=(depth, n_t),
        in_specs=[pl.BlockSpec((None, 128, d), lambda l, t, ta, tb: (l, ta[t], 0)),
                  pl.BlockSpec((None, 32, d), lambda l, t, ta, tb: (l, tb[t], 0))],
        out_specs=pl.BlockSpec((None, 128, d), lambda l, t, ta, tb: (l, t, 0)),
    )
    return pl.pallas_call(
        functools.partial(_wprep_kernel, n_plain=n_plain, n_t=n_t),
        grid_spec=grid_spec,
        out_shape=jax.ShapeDtypeStruct((depth, NP, d), BF16),
        compiler_params=_cp(("arbitrary", "arbitrary"), 32),
        name="w_in_layout",
    )(jnp.asarray(tile_a, I32), jnp.asarray(rows_b, I32), w_t, w_t)


def _delta_t(l):
    for t in (128, 64):
        if l % t == 0:
            return t
    raise ValueError("sequence length must be a multiple of 64")


def kernel(x_prompt, x_sample, cache_k, cache_v, cache_kidx, state_dn, state_conv, page_table,
           c_prompt, c_sample, w_ada, b_ada, g_norm, w_in, a_vnorm, a_ws, a_bs, dn_conv_w,
           dn_a_log, dn_dt_bias, dn_onorm, w_out, g_final):
    bp, lp, d = x_prompt.shape
    bs, ls, _ = x_sample.shape
    depth = w_ada.shape[0]
    assert d == D_MODEL and w_in.shape[2] == D_IN
    assert CONV_K - 1 <= ls <= DN_CHUNK

    n_c = bp + bs
    c_rows = jnp.concatenate([c_prompt, c_sample], axis=0)
    r_pad = (-n_c) % 8
    if r_pad:
        c_rows = jnp.pad(c_rows, ((0, r_pad), (0, 0)))
    m_all = _ada(c_rows, w_ada, b_ada)

    xp = x_prompt.reshape(bp * lp, d)
    xs = x_sample.reshape(bs * ls, d)
    ls_pad = DN_CHUNK
    outs = {k: [] for k in ("pk", "pv", "pki", "pdn", "pconv", "sk", "sv", "ski", "sdn", "sconv", "samlp")}
    zeros_conv = jnp.zeros((bp, CONV_K - 1, DN_CONV_DIM), F32)
    zeros_state = jnp.zeros((bp, DN_HEADS, DN_HEAD, DN_HEAD), F32)

    w_in_all = _permute_w_in(w_in)
    w_out_all = w_out.astype(BF16)
    for l in range(depth):
        g_l = g_norm[l].reshape(1, d)
        m_l = m_all[l]
        m3 = m_l.reshape(m_l.shape[0], 1, 3 * d)
        ms = jnp.repeat(m_l[bp:bp + bs], ls, axis=0)

        pp = _inproj(xp, g_l, m3, m3, w_in_all, l, rows_per_batch=lp, mod_row0=0)
        p3 = pp.reshape(bp, lp, NP)
        (ya,) = _mixa(p3, a_vnorm[l], a_ws[l], a_bs[l], emit_va=False)
        abt = jnp.transpose(p3[:, :, OFF["a"]:OFF["a"] + 2 * DN_HEADS], (0, 2, 1))
        yb, s_p = _delta(p3, abt, dn_conv_w[l], zeros_conv, zeros_state, dn_a_log[l], dn_dt_bias[l],
                         dn_onorm[l], t=_delta_t(lp), valid_len=_delta_t(lp))
        yc = _dsa_prompt(p3)
        xp = _outproj(ya.reshape(bp * lp, W_A), yb.reshape(bp * lp, W_B), yc.reshape(bp * lp, W_C),
                      w_out_all, l, xp, m3, rows_per_batch=lp, mod_row0=0)
        outs["pk"].append(p3[:, :, OFF["kc"]:OFF["kc"] + C_KV].reshape(bp, lp, C_KV_HEADS, C_HEAD))
        outs["pv"].append(p3[:, :, OFF["vc"]:OFF["vc"] + C_KV].reshape(bp, lp, C_KV_HEADS, C_HEAD))
        outs["pki"].append(p3[:, :, OFF_TAIL:OFF_TAIL + IDX_DIM])
        outs["pdn"].append(s_p)
        outs["pconv"].append(p3[:, lp - (CONV_K - 1):, OFF["qkv"]:OFF["qkv"] + DN_CONV_DIM])

        ps = _inproj(xs, g_l, ms[:, d:2 * d], ms[:, 0:d], w_in_all, l, rows_per_batch=ls, mod_row0=bp)
        p3s = ps.reshape(bs, ls, NP)
        ya_s, va_s = _mixa(p3s, a_vnorm[l], a_ws[l], a_bs[l], emit_va=True)
        p3s_pad = jnp.pad(p3s, ((0, 0), (0, ls_pad - ls), (0, 0)))
        abt_s = jnp.transpose(p3s_pad[:, :, OFF["a"]:OFF["a"] + 2 * DN_HEADS], (0, 2, 1))
        yb_s, s_s = _delta(p3s_pad, abt_s, dn_conv_w[l], state_conv[l], state_dn[l], dn_a_log[l],
                           dn_dt_bias[l], dn_onorm[l], t=ls_pad, valid_len=ls)
        yc_s = _dsa_sample(p3s, cache_k, cache_v, cache_kidx, page_table, l)
        xs = _outproj(ya_s.reshape(bs * ls, W_A), yb_s[:, :ls].reshape(bs * ls, W_B),
                      yc_s.reshape(bs * ls, W_C), w_out_all, l, xs, ms[:, 2 * d:3 * d],
                      rows_per_batch=ls, mod_row0=bp)
        outs["sk"].append(p3s[:, :, OFF["kc"]:OFF["kc"] + C_KV].reshape(bs, ls, C_KV_HEADS, C_HEAD))
        outs["sv"].append(p3s[:, :, OFF["vc"]:OFF["vc"] + C_KV].reshape(bs, ls, C_KV_HEADS, C_HEAD))
        outs["ski"].append(p3s[:, :, OFF_TAIL:OFF_TAIL + IDX_DIM])
        outs["sdn"].append(s_s)
        outs["sconv"].append(p3s[:, ls - (CONV_K - 1):, OFF["qkv"]:OFF["qkv"] + DN_CONV_DIM])
        outs["samlp"].append(va_s)

    y_prompt = _final_norm(xp, g_final).reshape(bp, lp, d)
    y_sample = _final_norm(xs, g_final).reshape(bs, ls, d)
    st = jnp.stack
    return (y_prompt, y_sample, st(outs["pk"]), st(outs["pv"]), st(outs["pki"]), st(outs["pdn"]),
            st(outs["pconv"]), st(outs["sk"]), st(outs["sv"]), st(outs["ski"]), st(outs["sdn"]),
            st(outs["sconv"]), st(outs["samlp"]))
```

```python
import functools

import jax
import jax.numpy as jnp
from jax import lax
from jax.experimental import pallas as pl
from jax.experimental.pallas import tpu as pltpu

F32 = jnp.float32
BF16 = jnp.bfloat16
I32 = jnp.int32
EPS = 1e-6
INT_MIN = -(2 ** 31)
NEG_INF = float("-inf")

D_MODEL = 4096
W_A = D_MODEL // 4
A_GROUP = 128
A_HEADS = W_A // A_GROUP
A_CHUNK = 128
W_B = D_MODEL // 2
DN_HEAD = 128
DN_HEADS = W_B // DN_HEAD
CONV_K = 4
DN_CONV_DIM = 3 * W_B
DN_CHUNK = 64
W_C = D_MODEL - W_A - W_B
C_HEAD = 128
C_HEADS = W_C // C_HEAD
C_KV_HEADS = 2
C_GROUPS = C_HEADS // C_KV_HEADS
C_KV = C_KV_HEADS * C_HEAD
IDX_HEADS = 16
IDX_DIM = 64
TOPK_MAX = 256
Q_BLOCK = 128

_SRC_SPLITS = (W_A, W_A, W_A, DN_CONV_DIM, W_B, DN_HEADS, DN_HEADS,
               W_C, C_KV, C_KV, W_C, IDX_HEADS * IDX_DIM, IDX_DIM, IDX_HEADS)
_SRC_NAMES = ("u", "v", "za", "qkv", "zb", "a", "b", "qc", "kc", "vc", "zc", "qi", "ki", "wi")
_SRC_OFF = {}
_o = 0
for _n, _w in zip(_SRC_NAMES, _SRC_SPLITS):
    _SRC_OFF[_n] = (_o, _w)
    _o += _w
D_IN = _o

_DST_ORDER = ("u", "v", "za", "qkv", "zb", "qc", "zc", "qi", "kc", "vc", "ki", "a", "b", "wi")
OFF = {}
_o = 0
for _n in _DST_ORDER:
    OFF[_n] = _o
    _o += _SRC_OFF[_n][1]
NP = ((_o + 127) // 128) * 128
OFF_TAIL = OFF["ki"]
TAIL_A = OFF["a"] - OFF_TAIL
TAIL_B = OFF["b"] - OFF_TAIL
TAIL_WI = OFF["wi"] - OFF_TAIL

INPROJ_TN = 1536
DELTA_HEADS_PER_STEP = 8
DELTA_WAVE = 8
DSA_KEY_TILE = 512


def _cp(sem, vmem_mb=48):
    return pltpu.CompilerParams(dimension_semantics=sem, vmem_limit_bytes=vmem_mb * 1024 * 1024)


def _silu(x):
    return (0.5 * x) * (1.0 + jnp.tanh(0.5 * x))


def _gelu(x):
    return 0.5 * x * (1.0 + jnp.tanh(0.7978845608028654 * (x + 0.044715 * (x * x * x))))


def _dot(a, b):
    return jnp.dot(a, b, preferred_element_type=F32)


def _dot_nt(a, b):
    return lax.dot_general(a, b, (((1,), (1,)), ((), ())), preferred_element_type=F32)


def _dot_tn(a, b):
    return lax.dot_general(a, b, (((0,), (0,)), ((), ())), preferred_element_type=F32)


def _b16(a):
    return a.astype(BF16)


def _ada_kernel(c_ref, w_ref, b_ref, o_ref):
    s = _silu(c_ref[...]).astype(BF16)
    o_ref[...] = _dot(s, w_ref[...].astype(BF16)) + b_ref[...]


def _ada(c_rows, w_ada, b_ada):
    depth, d, n = w_ada.shape
    r = c_rows.shape[0]
    tn = 512
    return pl.pallas_call(
        _ada_kernel,
        grid=(depth, n // tn),
        in_specs=[pl.BlockSpec((r, d), lambda l, j: (0, 0)),
                  pl.BlockSpec((None, d, tn), lambda l, j: (l, 0, j)),
                  pl.BlockSpec((None, 1, tn), lambda l, j: (l, 0, j))],
        out_specs=pl.BlockSpec((None, r, tn), lambda l, j: (l, 0, j)),
        out_shape=jax.ShapeDtypeStruct((depth, r, n), F32),
        compiler_params=_cp(("arbitrary", "arbitrary"), 40),
        name="ada",
    )(c_rows, w_ada, b_ada.reshape(depth, 1, n))


def _inproj_kernel(x_ref, g_ref, sc_ref, sh_ref, w_ref, o_ref, h_scr, *, rc):
    @pl.when(pl.program_id(1) == 0)
    def _():
        tm = x_ref.shape[0]
        per_row = sc_ref.shape[0] != 1

        def body(c, carry):
            r = pl.ds(pl.multiple_of(c * rc, rc), rc)
            x = x_ref[r, :]
            y = x * lax.rsqrt(jnp.mean(x * x, axis=-1, keepdims=True) + EPS)
            sc = sc_ref[r, :] if per_row else sc_ref[...]
            sh = sh_ref[r, :] if per_row else sh_ref[...]
            h_scr[r, :] = ((y * g_ref[...]) * (1.0 + sc) + sh).astype(BF16)
            return carry

        lax.fori_loop(0, tm // rc, body, 0)

    o_ref[...] = _dot_nt(h_scr[...], w_ref[...])


def _inproj(x2, g, sc, sh, w_bf16, layer, *, rows_per_batch, mod_row0):
    m, d = x2.shape
    n = w_bf16.shape[1]
    tn = INPROJ_TN
    if sc.ndim == 3:
        tm = 512
        assert rows_per_batch % tm == 0
        sc_spec = pl.BlockSpec((None, 1, d), lambda i, j: ((i * tm) // rows_per_batch + mod_row0, 0, 1))
        sh_spec = pl.BlockSpec((None, 1, d), lambda i, j: ((i * tm) // rows_per_batch + mod_row0, 0, 0))
        rc = 64
    else:
        tm = m
        sc_spec = pl.BlockSpec((tm, d), lambda i, j: (i, 0))
        sh_spec = pl.BlockSpec((tm, d), lambda i, j: (i, 0))
        rc = tm
    assert m % tm == 0
    return pl.pallas_call(
        functools.partial(_inproj_kernel, rc=rc),
        grid=(m // tm, pl.cdiv(n, tn)),
        in_specs=[pl.BlockSpec((tm, d), lambda i, j: (i, 0)),
                  pl.BlockSpec((1, d), lambda i, j: (0, 0)),
                  sc_spec, sh_spec,
                  pl.BlockSpec((None, tn, d), lambda i, j: (layer, j, 0))],
        out_specs=pl.BlockSpec((tm, tn), lambda i, j: (i, j)),
        out_shape=jax.ShapeDtypeStruct((m, n), F32),
        scratch_shapes=[pltpu.VMEM((tm, d), BF16)],
        compiler_params=_cp(("arbitrary", "arbitrary"), 56),
        name="inproj",
    )(x2, g, sc, sh, w_bf16)


def _mixa_kernel(u_ref, v_ref, z_ref, vn_ref, ws_ref, bst_ref, y_ref, *rest, c, emit_va):
    u = _gelu(u_ref[...])
    v = _gelu(v_ref[...])
    mu = jnp.mean(v, axis=-1, keepdims=True)
    dv = v - mu
    va = dv * lax.rsqrt(jnp.mean(dv * dv, axis=-1, keepdims=True) + EPS) * vn_ref[...]
    if emit_va:
        rest[0][...] = va
    z = _silu(z_ref[...])
    row = lax.broadcasted_iota(I32, (c, c), 0)
    col = lax.broadcasted_iota(I32, (c, c), 1)
    tril = col <= row
    for h in range(A_HEADS):
        cols = slice(h * A_GROUP, (h + 1) * A_GROUP)
        wm = jnp.where(tril, ws_ref[h], 0.0)
        vh = va[:, cols]
        if c >= 128:
            mixed = _dot(wm.astype(BF16), vh.astype(BF16))
        else:
            mixed = wm[:, 0:1] * vh[0:1, :]
            for s in range(1, c):
                mixed = mixed + wm[:, s:s + 1] * vh[s:s + 1, :]
        mixed = mixed + bst_ref[:, h:h + 1]
        y_ref[:, cols] = (u[:, cols] * mixed * z[:, cols]).astype(y_ref.dtype)


def _mixa(p3, a_vnorm, a_ws, a_bs, *, emit_va):
    b, l, _ = p3.shape
    c = min(A_CHUNK, l)
    n = l // c
    ws = a_ws[:, :c, :c]
    bst = a_bs[:, :c].T
    wblk = W_A
    outs = [jax.ShapeDtypeStruct((b, l, W_A), BF16)]
    out_specs = [pl.BlockSpec((None, c, W_A), lambda i, j: (i, j, 0))]
    if emit_va:
        outs.append(jax.ShapeDtypeStruct((b, l, W_A), F32))
        out_specs.append(pl.BlockSpec((None, c, W_A), lambda i, j: (i, j, 0)))
    res = pl.pallas_call(
        functools.partial(_mixa_kernel, c=c, emit_va=emit_va),
        grid=(b, n),
        in_specs=[pl.BlockSpec((None, c, wblk), lambda i, j: (i, j, OFF["u"] // wblk)),
                  pl.BlockSpec((None, c, wblk), lambda i, j: (i, j, OFF["v"] // wblk)),
                  pl.BlockSpec((None, c, wblk), lambda i, j: (i, j, OFF["za"] // wblk)),
                  pl.BlockSpec((1, W_A), lambda i, j: (0, 0)),
                  pl.BlockSpec((A_HEADS, c, c), lambda i, j: (0, 0, 0)),
                  pl.BlockSpec((c, A_HEADS), lambda i, j: (0, 0))],
        out_specs=out_specs,
        out_shape=outs,
        compiler_params=_cp(("arbitrary", "arbitrary"), 32),
        name="mixa",
    )(p3, p3, p3, a_vnorm.reshape(1, W_A), ws, bst)
    return res


def _delta_kernel(alog_ref, dtb_ref,
                  q_ref, k_ref, v_ref, z_ref, ab_ref,
                  cwq_ref, cwk_ref, cwv_ref, cpq_ref, cpk_ref, cpv_ref,
                  s0_ref, on_ref,
                  y_ref, sout_ref,
                  xbuf, s_scr, *, t, c, hp, wave, valid_len):
    n = pl.program_id(2)
    dh = DN_HEAD

    @pl.when(n == 0)
    def _():
        s_scr[...] = s0_ref[...]
        for j in range(hp):
            lanes = slice(j * dh, (j + 1) * dh)
            xbuf[3 * j + 0, 5:8, :] = cpq_ref[:, lanes]
            xbuf[3 * j + 1, 5:8, :] = cpk_ref[:, lanes]
            xbuf[3 * j + 2, 5:8, :] = cpv_ref[:, lanes]

    for w0 in range(0, hp, wave):
        _delta_wave(alog_ref, dtb_ref, q_ref, k_ref, v_ref, z_ref, ab_ref, cwq_ref, cwk_ref, cwv_ref,
                    on_ref, y_ref, xbuf, s_scr, heads=list(range(w0, w0 + wave)),
                    t=t, c=c, hp=hp, valid_len=valid_len)

    @pl.when(n == pl.num_programs(2) - 1)
    def _():
        sout_ref[...] = s_scr[...]


def _delta_wave(alog_ref, dtb_ref, q_ref, k_ref, v_ref, z_ref, ab_ref, cwq_ref, cwk_ref, cwv_ref,
                on_ref, y_ref, xbuf, s_scr, *, heads, t, c, hp, valid_len):
    hg = pl.program_id(1)
    dh = DN_HEAD

    row = lax.broadcasted_iota(I32, (t, t), 0)
    col = lax.broadcasted_iota(I32, (t, t), 1)
    shift = c.bit_length() - 1
    same = (row >> shift) == (col >> shift)
    eye = row == col
    incl = same & (col <= row)
    strict = same & (col < row)
    incl_t = same & (row <= col)
    blk8 = (row >> 3) == (col >> 3)
    off_masks = []
    bs = 8
    while bs < c:
        sh_b = bs.bit_length() - 1
        inner = (row >> sh_b) == (col >> sh_b)
        outer = (row >> (sh_b + 1)) == (col >> (sh_b + 1))
        off_masks.append(outer & jnp.logical_not(inner))
        bs *= 2
    eye_f = jnp.where(eye, 1.0, 0.0)
    if valid_len < t:
        lane_valid = lax.broadcasted_iota(I32, (1, t), 1) < valid_len
        sub_valid = lax.broadcasted_iota(I32, (t, 1), 0) < valid_len

    def to_col(r):
        return jnp.sum(jnp.where(eye, r, 0.0), axis=1, keepdims=True)

    def conv(idx, x, w_ref, lanes):
        xbuf[idx, 8:8 + t, :] = x
        y = xbuf[idx, pl.ds(5, t), :] * w_ref[0:1, lanes]
        for jj in range(1, CONV_K):
            y = y + xbuf[idx, pl.ds(5 + jj, t), :] * w_ref[jj:jj + 1, lanes]
        xbuf[idx, 5:8, :] = x[t - 3:t, :]
        return _silu(y)

    lanes_of = [slice(j * dh, (j + 1) * dh) for j in range(hp)]

    def per_head(f, *lists):
        return [f(*vals) for vals in zip(*lists)]

    q = [conv(3 * j + 0, q_ref[:, lanes_of[j]], cwq_ref, lanes_of[j]) for j in heads]
    k = [conv(3 * j + 1, k_ref[:, lanes_of[j]], cwk_ref, lanes_of[j]) for j in heads]
    v = [conv(3 * j + 2, v_ref[:, lanes_of[j]], cwv_ref, lanes_of[j]) for j in heads]
    qc = per_head(lambda a: a * lax.rsqrt(jnp.sum(a * a, axis=-1, keepdims=True) + EPS) * (dh ** -0.5), q)
    kc = per_head(lambda a: a * lax.rsqrt(jnp.sum(a * a, axis=-1, keepdims=True) + EPS), k)

    def gates(j):
        h = hg * hp + j
        a_row = ab_ref[pl.ds(h, 1), :]
        b_row = ab_ref[pl.ds(DN_HEADS + h, 1), :]
        xa = a_row + dtb_ref[h]
        softplus = jnp.maximum(xa, 0.0) + jnp.log(1.0 + jnp.exp(-jnp.abs(xa)))
        a_coef = jnp.exp(jnp.zeros((1, 1), F32) + alog_ref[h])
        g_row = -a_coef * softplus
        beta_row = 1.0 / (1.0 + jnp.exp(-b_row))
        if valid_len < t:
            g_row = jnp.where(lane_valid, g_row, 0.0)
            beta_row = jnp.where(lane_valid, beta_row, 0.0)
        return g_row, beta_row

    g_row, beta_row = zip(*[gates(j) for j in heads])
    if valid_len < t:
        kc = per_head(lambda a: jnp.where(sub_valid, a, 0.0), kc)
        v = per_head(lambda a: jnp.where(sub_valid, a, 0.0), v)

    g_col = per_head(to_col, g_row)
    beta_col = per_head(to_col, beta_row)
    gc_col = per_head(lambda r: jnp.sum(jnp.where(incl, r, 0.0), axis=1, keepdims=True), g_row)
    glast_col = per_head(lambda r: jnp.sum(jnp.where(same, r, 0.0), axis=1, keepdims=True), g_row)
    gc_row = per_head(lambda cl: jnp.sum(jnp.where(incl_t, cl, 0.0), axis=0, keepdims=True), g_col)
    decay = per_head(lambda gc, gr: jnp.where(incl, jnp.exp(jnp.where(incl, gc - gr, 0.0)), 0.0), gc_col, gc_row)
    kb = per_head(lambda a, b: a * b, kc, beta_col)
    kc16 = per_head(lambda a: a.astype(BF16), kc)
    a_mat = per_head(lambda a, b, dc: jnp.where(strict, _dot_nt(a.astype(BF16), b) * dc, 0.0), kb, kc16, decay)
    attn = per_head(lambda a, b, dc: (_dot_nt(a.astype(BF16), b) * dc).astype(BF16), qc, kc16, decay)
    eg = per_head(jnp.exp, gc_col)
    rhs = per_head(lambda vv, bc, kk, e: _b16(jnp.concatenate([vv * bc, kk * e], axis=1)),
                   v, beta_col, kb, eg)

    n0f = per_head(lambda a: jnp.where(blk8, -a, 0.0), a_mat)
    n0 = per_head(_b16, n0f)
    n2 = per_head(lambda a: _b16(_dot(a, a)), n0)
    n4 = per_head(lambda a: _b16(_dot(a, a)), n2)
    x = per_head(lambda a: eye_f + a, n0f)
    x = per_head(lambda xx, nn: xx + _dot(_b16(xx), nn), x, n2)
    x = per_head(lambda xx, nn: xx + _dot(_b16(xx), nn), x, n4)
    for om in off_masks:
        xs = per_head(_b16, x)
        xa_off = per_head(lambda s_, a: _b16(_dot(s_, _b16(jnp.where(om, a, 0.0)))), xs, a_mat)
        x = per_head(lambda xx, xo, s_: xx - _dot(xo, s_), x, xa_off, xs)
    sol16 = per_head(lambda xx, r: _b16(_dot(_b16(xx), r)), x, rhs)
    auw = per_head(_dot, attn, sol16)
    qw = per_head(lambda a, e, m_: (a * e - m_[:, dh:]).astype(BF16), qc, eg, auw)
    kg16 = per_head(lambda a, gl_, gc: (a * jnp.exp(gl_ - gc)).astype(BF16), kc, glast_col, gc_col)

    s = [s_scr[j] for j in heads]
    outs = [[] for _ in heads]
    for i in range(t // c):
        rows = slice(i * c, (i + 1) * c)
        s16 = per_head(lambda a: a.astype(BF16), s)
        kuw = per_head(lambda a, b: _dot_tn(a[rows], b[rows]), kg16, sol16)
        o_i = per_head(lambda a, b16, m_: _dot(a[rows], b16) + m_[rows, :dh], qw, s16, auw)
        for pos in range(len(heads)):
            outs[pos].append(o_i[pos])
        gl = per_head(lambda a: jnp.exp(a[i * c:i * c + 1, :]), glast_col)
        s = per_head(lambda g_, s_, m_, b16: g_ * s_ + m_[:, :dh] - _dot(m_[:, dh:].astype(BF16), b16),
                     gl, s, kuw, s16)
    for pos, j in enumerate(heads):
        s_scr[j] = s[pos]
        o = outs[pos][0] if len(outs[pos]) == 1 else jnp.concatenate(outs[pos], axis=0)
        on = o * lax.rsqrt(jnp.mean(o * o, axis=-1, keepdims=True) + EPS) * on_ref[...]
        y_ref[:, lanes_of[j]] = (on * _silu(z_ref[:, lanes_of[j]])).astype(y_ref.dtype)


def _delta(p3, abt, conv_w, conv_prev, s0, a_log, dt_bias, onorm, *, t, valid_len):
    b, l, _ = p3.shape
    assert l % t == 0
    c = min(DN_CHUNK, t)
    hp = DELTA_HEADS_PER_STEP
    nh = DN_HEADS
    ng = nh // hp
    w = 128 * hp
    cb_q = OFF["qkv"] // w
    cb_z = OFF["zb"] // w
    assert OFF["qkv"] % w == 0 and OFF["zb"] % w == 0 and W_B % w == 0

    def pspec(cb0):
        return pl.BlockSpec((None, t, w), lambda i, h, n, a, d: (i, n, cb0 + h))

    def cwspec(sidx):
        return pl.BlockSpec((CONV_K, w), lambda i, h, n, a, d: (0, sidx * ng + h))

    def cpspec(sidx):
        return pl.BlockSpec((None, CONV_K - 1, w), lambda i, h, n, a, d: (i, 0, sidx * ng + h))

    grid_spec = pltpu.PrefetchScalarGridSpec(
        num_scalar_prefetch=2,
        grid=(b, ng, l // t),
        in_specs=[pspec(cb_q), pspec(cb_q + ng), pspec(cb_q + 2 * ng), pspec(cb_z),
                  pl.BlockSpec((None, 2 * nh, t), lambda i, h, n, a, d: (i, 0, n)),
                  cwspec(0), cwspec(1), cwspec(2), cpspec(0), cpspec(1), cpspec(2),
                  pl.BlockSpec((None, hp, DN_HEAD, DN_HEAD), lambda i, h, n, a, d: (i, h, 0, 0)),
                  pl.BlockSpec((1, DN_HEAD), lambda i, h, n, a, d: (0, 0))],
        out_specs=[pl.BlockSpec((None, t, w), lambda i, h, n, a, d: (i, n, h)),
                   pl.BlockSpec((None, hp, DN_HEAD, DN_HEAD), lambda i, h, n, a, d: (i, h, 0, 0))],
        scratch_shapes=[pltpu.VMEM((3 * hp, t + 8, 128), F32), pltpu.VMEM((hp, DN_HEAD, DN_HEAD), F32)],
    )
    return pl.pallas_call(
        functools.partial(_delta_kernel, t=t, c=c, hp=hp, wave=DELTA_WAVE, valid_len=valid_len),
        grid_spec=grid_spec,
        out_shape=[jax.ShapeDtypeStruct((b, l, W_B), BF16),
                   jax.ShapeDtypeStruct((b, nh, DN_HEAD, DN_HEAD), F32)],
        compiler_params=_cp(("arbitrary", "arbitrary", "arbitrary"), 40),
        name="delta",
    )(a_log, dt_bias, p3, p3, p3, p3, abt, conv_w, conv_w, conv_w,
      conv_prev, conv_prev, conv_prev, s0, onorm.reshape(1, DN_HEAD))


def _ordered_bits_to_float(u):
    key = u ^ jnp.int32(INT_MIN)
    bits = jnp.where(key < 0, key ^ jnp.int32(0x7FFFFFFF), key)
    return pltpu.bitcast(bits, F32)


def _topk_threshold(count_ge, shape, topk):
    def bit_body(i, u):
        bit = jnp.left_shift(jnp.int32(1), 31 - i)
        cand_u = u | bit
        cnt = count_ge(_ordered_bits_to_float(cand_u))
        return jnp.where(cnt >= float(topk), cand_u, u)

    u = lax.fori_loop(0, 32, bit_body, jnp.zeros(shape, I32))
    return _ordered_bits_to_float(u)


def _softmax_update(m_ref, l_ref, acc_ref, idx, s, v_tiles):
    m_old = m_ref[idx]
    m_new = jnp.maximum(m_old, jnp.max(s, axis=1, keepdims=True))
    m_safe = jnp.where(m_new == NEG_INF, 0.0, m_new)
    alpha = jnp.exp(m_old - m_safe)
    p = jnp.exp(s - m_safe)
    l_ref[idx] = alpha * l_ref[idx] + jnp.sum(p, axis=1, keepdims=True)
    acc = alpha * acc_ref[idx]
    p16 = p.astype(BF16)
    k0 = 0
    for vt in v_tiles:
        acc = acc + _dot(p16[:, k0:k0 + vt.shape[0]], vt)
        k0 += vt.shape[0]
    acc_ref[idx] = acc
    m_ref[idx] = m_new


def _dsa_prompt_kernel(qi_ref, tq_ref, tall_ref, qc_ref, k_ref, v_ref, z_ref, y_ref,
                       sc_scr, m_scr, l_scr, acc_scr, *, l, topk, kt):
    qb = pl.program_id(1)
    nq = Q_BLOCK
    n_tiles = (qb * nq + nq + kt - 1) // kt
    n_pairs = C_HEADS // 2
    w_rows = tq_ref[...].T[TAIL_WI:TAIL_WI + IDX_HEADS, :] * ((IDX_HEADS ** -0.5) * (IDX_DIM ** -0.5))
    rq = []
    for j in range(IDX_HEADS // 2):
        a = qi_ref[:, j * 128:(j + 1) * 128]
        rq.append(jnp.concatenate([a, pltpu.roll(a, IDX_DIM, 1)], axis=0).astype(BF16))
    kpos = lax.broadcasted_iota(I32, (kt, nq), 0)
    qpos = qb * nq + lax.broadcasted_iota(I32, (kt, nq), 1)
    lane = lax.broadcasted_iota(I32, (kt, 128), 1)

    def score_tile(ti, carry):
        k0 = pl.multiple_of(ti * kt, kt)
        ki = jnp.where(lane < IDX_DIM, tall_ref[pl.ds(k0, kt), :], 0.0).astype(BF16)
        score = jnp.zeros((kt, nq), F32)
        for j in range(IDX_HEADS // 2):
            lg = _dot_nt(ki, rq[j])
            score = score + w_rows[2 * j:2 * j + 1, :] * jnp.maximum(lg[:, :nq], 0.0)
            score = score + w_rows[2 * j + 1:2 * j + 2, :] * jnp.maximum(lg[:, nq:], 0.0)
        sc_scr[pl.ds(k0, kt), :] = jnp.where(kpos + k0 <= qpos, score, NEG_INF)
        return carry

    lax.fori_loop(0, n_tiles, score_tile, 0)

    ct = 256
    n_ct = (qb * nq + nq + ct - 1) // ct
    n_acc = 4

    def count(cand, strict):
        cand8 = jnp.broadcast_to(cand, (8, nq))

        def tile_body(ti, accs):
            k0 = pl.multiple_of(ti * ct, ct)
            tile = sc_scr[pl.ds(k0, ct), :]
            accs = list(accs)
            for r in range(ct // 8):
                blk = tile[8 * r:8 * r + 8, :]
                hit = (blk > cand8) if strict else (blk >= cand8)
                accs[r % n_acc] = accs[r % n_acc] + jnp.where(hit, 1.0, 0.0)
            return tuple(accs)

        accs = lax.fori_loop(0, n_ct, tile_body, tuple(jnp.zeros((8, nq), F32) for _ in range(n_acc)))
        acc = (accs[0] + accs[1]) + (accs[2] + accs[3])
        return jnp.sum(acc, axis=0, keepdims=True)

    thr = _topk_threshold(lambda cand: count(cand, False), (1, nq), topk)
    keep_all = (qb * nq + lax.broadcasted_iota(I32, (1, nq), 1)) < topk
    need = float(topk) - count(thr, True)
    tri16 = jnp.where(lax.broadcasted_iota(I32, (128, 128), 1) <= lax.broadcasted_iota(I32, (128, 128), 0),
                      1.0, 0.0).astype(BF16)

    def select_tile(ti, seen):
        k0 = pl.multiple_of(ti * kt, kt)
        sc = sc_scr[pl.ds(k0, kt), :]
        eq = sc == thr
        eq_f = jnp.where(eq, 1.0, 0.0)
        eq16 = eq_f.astype(BF16)
        prefs = [_dot(tri16, eq16[r:r + 128]) for r in range(0, kt, 128)]
        ranks = []
        for pref in prefs:
            ranks.append(seen + pref)
            seen = seen + pref[127:128, :]
        tie_ok = (jnp.concatenate(ranks, axis=0) - eq_f) < need
        chosen = jnp.where(sc > thr, 0.0, jnp.where(eq, jnp.where(tie_ok, 0.0, NEG_INF), NEG_INF))
        sc_scr[pl.ds(k0, kt), :] = jnp.where(keep_all, jnp.where(sc > NEG_INF, 0.0, NEG_INF), chosen)
        return seen

    lax.fori_loop(0, n_tiles, select_tile, jnp.zeros((1, nq), F32))

    m_scr[...] = jnp.full(m_scr.shape, NEG_INF, F32)
    l_scr[...] = jnp.zeros(l_scr.shape, F32)
    acc_scr[...] = jnp.zeros(acc_scr.shape, F32)
    scale = (C_HEAD ** -0.5) * 1.4426950408889634
    qp = []
    for pr in range(n_pairs):
        c0 = slice(2 * pr * C_HEAD, (2 * pr + 1) * C_HEAD)
        c1 = slice((2 * pr + 1) * C_HEAD, (2 * pr + 2) * C_HEAD)
        qp.append(jnp.concatenate([qc_ref[:, c0] * scale, qc_ref[:, c1] * scale], axis=0).astype(BF16))
    pairs_per_kv = n_pairs // C_KV_HEADS

    def attend_tile(ti, carry):
        k0 = pl.multiple_of(ti * kt, kt)
        bias = sc_scr[pl.ds(k0, kt), :]
        k16 = [k_ref[pl.ds(k0, kt), hk * C_HEAD:(hk + 1) * C_HEAD].astype(BF16) for hk in range(C_KV_HEADS)]
        vt16 = [v_ref[pl.ds(k0, kt), hk * C_HEAD:(hk + 1) * C_HEAD].T.astype(BF16) for hk in range(C_KV_HEADS)]
        bias2 = jnp.concatenate([bias, bias], axis=1)
        pr_all = range(n_pairs)
        s = [_dot_nt(k16[pr // pairs_per_kv], qp[pr]) + bias2 for pr in pr_all]
        m_old = [m_scr[pr] for pr in pr_all]
        m_new = [jnp.maximum(m_old[pr], jnp.max(s[pr], axis=0, keepdims=True)) for pr in pr_all]
        m_safe = [jnp.where(m_new[pr] == NEG_INF, 0.0, m_new[pr]) for pr in pr_all]
        alpha = [jnp.exp2(m_old[pr] - m_safe[pr]) for pr in pr_all]
        p = [jnp.exp2(s[pr] - m_safe[pr]) for pr in pr_all]
        for pr in pr_all:
            l_scr[pr] = alpha[pr] * l_scr[pr] + jnp.sum(p[pr], axis=0, keepdims=True)
            acc_scr[pr] = alpha[pr] * acc_scr[pr] + _dot(vt16[pr // pairs_per_kv], p[pr].astype(BF16))
            m_scr[pr] = m_new[pr]
        return carry

    lax.fori_loop(0, n_tiles, attend_tile, 0)
    for pr in range(n_pairs):
        o_t = acc_scr[pr] / l_scr[pr]
        for e in range(2):
            cols = slice((2 * pr + e) * C_HEAD, (2 * pr + e + 1) * C_HEAD)
            o = o_t[:, e * nq:(e + 1) * nq].T
            y_ref[:, cols] = (o * _silu(z_ref[:, cols])).astype(y_ref.dtype)


def _dsa_prompt(p3):
    b, l, _ = p3.shape
    assert l % Q_BLOCK == 0
    topk = min(TOPK_MAX, l // 4)
    nq = Q_BLOCK
    kt = DSA_KEY_TILE if l % DSA_KEY_TILE == 0 else 128
    return pl.pallas_call(
        functools.partial(_dsa_prompt_kernel, l=l, topk=topk, kt=kt),
        grid=(b, l // nq),
        in_specs=[pl.BlockSpec((None, nq, W_C), lambda i, j: (i, j, OFF["qi"] // W_C)),
                  pl.BlockSpec((None, nq, 128), lambda i, j: (i, j, OFF_TAIL // 128)),
                  pl.BlockSpec((None, l, 128), lambda i, j: (i, 0, OFF_TAIL // 128)),
                  pl.BlockSpec((None, nq, W_C), lambda i, j: (i, j, OFF["qc"] // W_C)),
                  pl.BlockSpec((None, l, C_KV), lambda i, j: (i, 0, OFF["kc"] // C_KV)),
                  pl.BlockSpec((None, l, C_KV), lambda i, j: (i, 0, OFF["vc"] // C_KV)),
                  pl.BlockSpec((None, nq, W_C), lambda i, j: (i, j, OFF["zc"] // W_C))],
        out_specs=pl.BlockSpec((None, nq, W_C), lambda i, j: (i, j, 0)),
        out_shape=jax.ShapeDtypeStruct((b, l, W_C), BF16),
        scratch_shapes=[pltpu.VMEM((l, nq), F32),
                        pltpu.VMEM((C_HEADS // 2, 1, 2 * nq), F32), pltpu.VMEM((C_HEADS // 2, 1, 2 * nq), F32),
                        pltpu.VMEM((C_HEADS // 2, C_HEAD, 2 * nq), F32)],
        compiler_params=_cp(("arbitrary", "arbitrary"), 48),
        name="dsa_prompt",
    )(p3, p3, p3, p3, p3, p3, p3)


def _idx_scores(logits, wcol, t):
    r = jnp.maximum(logits, 0.0) * wcol
    return jnp.sum(r.reshape(t, IDX_HEADS, logits.shape[1]), axis=1)


def _dsa_s_score_kernel(pt_ref, q_ref, w_ref, kn_ref, *refs, pg, t):
    pages = refs[:pg]
    out_ref, new_ref = refs[pg:]
    q16 = q_ref[...].astype(BF16)
    wcol = w_ref[...] * ((IDX_HEADS ** -0.5) * (IDX_DIM ** -0.5))
    for i in range(pg):
        out_ref[:, i * 128:(i + 1) * 128] = _idx_scores(_dot(q16, pages[i][...].astype(BF16)), wcol, t)

    @pl.when(pl.program_id(1) == 0)
    def _():
        new_ref[...] = _idx_scores(_dot_nt(q16, kn_ref[...].astype(BF16)), wcol, t)


def _dsa_s_select_kernel(sc_ref, scn_ref, tq_ref, bias_ref, *, nk, topk):
    rows = sc_ref.shape[0]
    tq = tq_ref[...]
    new_ok = lax.broadcasted_iota(I32, (rows, 128), 1) <= tq
    sc_new = jnp.where(new_ok, scn_ref[...], NEG_INF)
    sc_past = sc_ref[...]

    def count_ge(cand):
        c1 = jnp.sum(jnp.where(sc_past >= cand, 1.0, 0.0), axis=1, keepdims=True)
        c2 = jnp.sum(jnp.where(sc_new >= cand, 1.0, 0.0), axis=1, keepdims=True)
        return c1 + c2

    thr = _topk_threshold(count_ge, (rows, 1), topk)
    keep_all = (nk + 1 + tq[:, 0:1]) <= topk
    n_gt = (jnp.sum(jnp.where(sc_past > thr, 1.0, 0.0), axis=1, keepdims=True)
            + jnp.sum(jnp.where(sc_new > thr, 1.0, 0.0), axis=1, keepdims=True))
    need = float(topk) - n_gt
    tri16 = jnp.where(lax.broadcasted_iota(I32, (128, 128), 0) <= lax.broadcasted_iota(I32, (128, 128), 1),
                      1.0, 0.0).astype(BF16)

    def bias_blocks(scs, seen):
        eqs = [sc == thr for sc in scs]
        eq_fs = [jnp.where(eq, 1.0, 0.0) for eq in eqs]
        prefs = [_dot(eq_f.astype(BF16), tri16) for eq_f in eq_fs]
        out = []
        for sc, eq, eq_f, pref in zip(scs, eqs, eq_fs, prefs):
            tie_ok = (seen + pref - eq_f) < need
            chosen = jnp.where(sc > thr, 0.0, jnp.where(eq, jnp.where(tie_ok, 0.0, NEG_INF), NEG_INF))
            out.append(jnp.where(keep_all, jnp.where(sc > NEG_INF, 0.0, NEG_INF), chosen))
            seen = seen + pref[:, 127:128]
        return out, seen

    n_blk = nk // 128
    grp = 8 if n_blk % 8 == 0 else 1

    def past_group(gi, seen):
        lanes = [pl.ds(pl.multiple_of((gi * grp + i) * 128, 128), 128) for i in range(grp)]
        out, seen = bias_blocks([sc_ref[:, ln] for ln in lanes], seen)
        for ln, bias in zip(lanes, out):
            bias_ref[:, ln] = bias
        return seen

    seen = lax.fori_loop(0, n_blk // grp, past_group, jnp.zeros((rows, 1), F32))
    (bias_new,), _ = bias_blocks([sc_new], seen)
    bias_ref[:, nk:] = bias_new


def _dsa_s_attend_kernel(pt_ref, q_ref, bias_ref, biasn_ref, kn_ref, vn_ref, z_ref, *refs, pg):
    kpages = refs[:pg]
    vpages = refs[pg:2 * pg]
    o_ref = refs[2 * pg]
    m_scr, l_scr, acc_scr = refs[2 * pg + 1:]
    g = pl.program_id(1)
    scale = C_HEAD ** -0.5
    page = kpages[0].shape[0] // C_KV_HEADS

    @pl.when(g == 0)
    def _():
        m_scr[...] = jnp.full(m_scr.shape, NEG_INF, F32)
        l_scr[...] = jnp.zeros(l_scr.shape, F32)
        acc_scr[...] = jnp.zeros(acc_scr.shape, F32)
        for hk in range(C_KV_HEADS):
            cols = slice(hk * C_HEAD, (hk + 1) * C_HEAD)
            q16 = (q_ref[hk] * scale).astype(BF16)
            s = _dot_nt(q16, kn_ref[:, cols].astype(BF16)) + biasn_ref[...]
            _softmax_update(m_scr, l_scr, acc_scr, hk, s, [vn_ref[:, cols].astype(BF16)])

    def two_pages(refs_, i, rows):
        return jnp.concatenate([refs_[i][rows, :], refs_[i + 1][rows, :]], axis=0).astype(BF16)

    for hk in range(C_KV_HEADS):
        q16 = (q_ref[hk] * scale).astype(BF16)
        rows = pl.ds(hk, page, stride=C_KV_HEADS)
        tiles = [_dot_nt(q16, two_pages(kpages, i, rows)) for i in range(0, pg, 2)]
        s = jnp.concatenate(tiles, axis=1) + bias_ref[...]
        _softmax_update(m_scr, l_scr, acc_scr, hk, s, [two_pages(vpages, i, rows) for i in range(0, pg, 2)])

    @pl.when(g == pl.num_programs(1) - 1)
    def _():
        for hk in range(C_KV_HEADS):
            o_ref[hk] = (acc_scr[hk] / l_scr[hk]) * _silu(z_ref[hk])


def _dsa_sample(p3s, cache_k, cache_v, cache_kidx, page_table, layer):
    b, t, _ = p3s.shape
    n_pages = page_table.shape[1]
    page = cache_k.shape[2]
    assert page == 128
    past = n_pages * page
    topk = min(TOPK_MAX, (past + t) // 4)
    pg = 32 if n_pages % 32 == 0 else (16 if n_pages % 16 == 0 else n_pages)
    assert pg % 2 == 0
    ng = n_pages // pg
    rows = t * C_GROUPS

    qi = p3s[:, :, OFF["qi"]:OFF["qi"] + IDX_HEADS * IDX_DIM].reshape(b, t * IDX_HEADS, IDX_DIM)
    wi = p3s[:, :, OFF_TAIL + TAIL_WI:OFF_TAIL + TAIL_WI + IDX_HEADS].reshape(b, t * IDX_HEADS, 1)
    ki_new = jnp.pad(p3s[:, :, OFF_TAIL:OFF_TAIL + IDX_DIM], ((0, 0), (0, 128 - t), (0, 0)))
    k_new = jnp.pad(p3s[:, :, OFF["kc"]:OFF["kc"] + C_KV], ((0, 0), (0, 128 - t), (0, 0)))
    v_new = jnp.pad(p3s[:, :, OFF["vc"]:OFF["vc"] + C_KV], ((0, 0), (0, 128 - t), (0, 0)))

    def heads_major(a):
        a = a.reshape(b, t, C_KV_HEADS, C_GROUPS, C_HEAD)
        return jnp.transpose(a, (0, 2, 1, 3, 4)).reshape(b, C_KV_HEADS, rows, C_HEAD)

    qh = heads_major(p3s[:, :, OFF["qc"]:OFF["qc"] + W_C])
    zh = heads_major(p3s[:, :, OFF["zc"]:OFF["zc"] + W_C])

    kidx_t = jnp.swapaxes(cache_kidx, 2, 3)
    score_spec = pltpu.PrefetchScalarGridSpec(
        num_scalar_prefetch=1,
        grid=(b, ng),
        in_specs=[pl.BlockSpec((None, t * IDX_HEADS, IDX_DIM), lambda i, g, pt: (i, 0, 0)),
                  pl.BlockSpec((None, t * IDX_HEADS, 1), lambda i, g, pt: (i, 0, 0)),
                  pl.BlockSpec((None, 128, IDX_DIM), lambda i, g, pt: (i, 0, 0))]
                 + [pl.BlockSpec((None, None, IDX_DIM, page),
                                 lambda i, g, pt, j=j: (layer, pt[i, g * pg + j], 0, 0)) for j in range(pg)],
        out_specs=[pl.BlockSpec((None, t, pg * page), lambda i, g, pt: (i, 0, g)),
                   pl.BlockSpec((None, t, 128), lambda i, g, pt: (i, 0, 0))],
    )
    scores, scores_new = pl.pallas_call(
        functools.partial(_dsa_s_score_kernel, pg=pg, t=t),
        grid_spec=score_spec,
        out_shape=[jax.ShapeDtypeStruct((b, t, past), F32), jax.ShapeDtypeStruct((b, t, 128), F32)],
        compiler_params=_cp(("arbitrary", "arbitrary"), 32),
        name="dsa_s_score",
    )(page_table, qi, wi, ki_new, *([kidx_t] * pg))

    tq = jnp.broadcast_to(jnp.tile(jnp.arange(t, dtype=I32), b)[:, None], (b * t, 128))
    bias = pl.pallas_call(
        functools.partial(_dsa_s_select_kernel, nk=past, topk=topk),
        grid=(1,),
        in_specs=[pl.BlockSpec((b * t, past), lambda i: (0, 0)),
                  pl.BlockSpec((b * t, 128), lambda i: (0, 0)),
                  pl.BlockSpec((b * t, 128), lambda i: (0, 0))],
        out_specs=pl.BlockSpec((b * t, past + 128), lambda i: (0, 0)),
        out_shape=jax.ShapeDtypeStruct((b * t, past + 128), F32),
        compiler_params=_cp(("arbitrary",), 40),
        name="dsa_s_select",
    )(scores.reshape(b * t, past), scores_new.reshape(b * t, 128), tq).reshape(b, t, past + 128)

    bias_rows = jnp.repeat(bias, C_GROUPS, axis=1)
    bias_past = bias_rows[:, :, :past]
    bias_new = bias_rows[:, :, past:]

    ck = cache_k.reshape(cache_k.shape[0], cache_k.shape[1], page * C_KV_HEADS, C_HEAD)
    cv = cache_v.reshape(cache_v.shape[0], cache_v.shape[1], page * C_KV_HEADS, C_HEAD)
    pspec = [pl.BlockSpec((None, None, page * C_KV_HEADS, C_HEAD),
                          lambda i, g, pt, j=j: (layer, pt[i, g * pg + j], 0, 0)) for j in range(pg)]
    attend_spec = pltpu.PrefetchScalarGridSpec(
        num_scalar_prefetch=1,
        grid=(b, ng),
        in_specs=[pl.BlockSpec((None, C_KV_HEADS, rows, C_HEAD), lambda i, g, pt: (i, 0, 0, 0)),
                  pl.BlockSpec((None, rows, pg * page), lambda i, g, pt: (i, 0, g)),
                  pl.BlockSpec((None, rows, 128), lambda i, g, pt: (i, 0, 0)),
                  pl.BlockSpec((None, 128, C_KV), lambda i, g, pt: (i, 0, 0)),
                  pl.BlockSpec((None, 128, C_KV), lambda i, g, pt: (i, 0, 0)),
                  pl.BlockSpec((None, C_KV_HEADS, rows, C_HEAD), lambda i, g, pt: (i, 0, 0, 0))]
                 + pspec + pspec,
        out_specs=pl.BlockSpec((None, C_KV_HEADS, rows, C_HEAD), lambda i, g, pt: (i, 0, 0, 0)),
        scratch_shapes=[pltpu.VMEM((C_KV_HEADS, rows, 1), F32), pltpu.VMEM((C_KV_HEADS, rows, 1), F32),
                        pltpu.VMEM((C_KV_HEADS, rows, C_HEAD), F32)],
    )
    oh = pl.pallas_call(
        functools.partial(_dsa_s_attend_kernel, pg=pg),
        grid_spec=attend_spec,
        out_shape=jax.ShapeDtypeStruct((b, C_KV_HEADS, rows, C_HEAD), F32),
        compiler_params=_cp(("arbitrary", "arbitrary"), 40),
        name="dsa_s_attend",
    )(page_table, qh, bias_past, bias_new, k_new, v_new, zh, *([ck] * pg), *([cv] * pg))
    y = jnp.transpose(oh.reshape(b, C_KV_HEADS, t, C_GROUPS, C_HEAD), (0, 2, 1, 3, 4)).reshape(b, t, W_C)
    return y.astype(BF16)


def _outproj_kernel(ya_ref, yb_ref, yc_ref, w_ref, x_ref, gate_ref, o_ref, mix_scr):
    @pl.when(pl.program_id(1) == 0)
    def _():
        mix_scr[:, 0:W_A] = ya_ref[...]
        mix_scr[:, W_A:W_A + W_B] = yb_ref[...]
        mix_scr[:, W_A + W_B:] = yc_ref[...]

    o_ref[...] = x_ref[...] + gate_ref[...] * _dot(mix_scr[...], w_ref[...])


def _outproj(ya, yb, yc, w_bf16, layer, x2, gate, *, rows_per_batch, mod_row0):
    m, d = x2.shape
    tn = 1024
    if gate.ndim == 3:
        tm = 512
        gate_spec = pl.BlockSpec((None, 1, tn),
                                 lambda i, j: ((i * tm) // rows_per_batch + mod_row0, 0, 2 * (d // tn) + j))
    else:
        tm = m
        gate_spec = pl.BlockSpec((tm, tn), lambda i, j: (i, j))
    assert m % tm == 0
    return pl.pallas_call(
        _outproj_kernel,
        grid=(m // tm, d // tn),
        in_specs=[pl.BlockSpec((tm, W_A), lambda i, j: (i, 0)),
                  pl.BlockSpec((tm, W_B), lambda i, j: (i, 0)),
                  pl.BlockSpec((tm, W_C), lambda i, j: (i, 0)),
                  pl.BlockSpec((None, d, tn), lambda i, j: (layer, 0, j)),
                  pl.BlockSpec((tm, tn), lambda i, j: (i, j)),
                  gate_spec],
        out_specs=pl.BlockSpec((tm, tn), lambda i, j: (i, j)),
        out_shape=jax.ShapeDtypeStruct((m, d), F32),
        scratch_shapes=[pltpu.VMEM((tm, d), BF16)],
        compiler_params=_cp(("arbitrary", "arbitrary"), 48),
        name="outproj",
    )(ya, yb, yc, w_bf16, x2, gate)


def _final_norm_kernel(x_ref, g_ref, o_ref):
    x = x_ref[...]
    o_ref[...] = x * lax.rsqrt(jnp.mean(x * x, axis=-1, keepdims=True) + EPS) * g_ref[...]


def _final_norm(x2, g):
    m, d = x2.shape
    tm = 256 if m % 256 == 0 else m
    return pl.pallas_call(
        _final_norm_kernel,
        grid=(m // tm,),
        in_specs=[pl.BlockSpec((tm, d), lambda i: (i, 0)), pl.BlockSpec((1, d), lambda i: (0, 0))],
        out_specs=pl.BlockSpec((tm, d), lambda i: (i, 0)),
        out_shape=jax.ShapeDtypeStruct((m, d), F32),
        compiler_params=_cp(("arbitrary",), 32),
        name="final_norm",
    )(x2, g.reshape(1, d))


def _w_in_tile_table():
    n_t = NP // 128
    table, n_plain = [], None
    for dt in range(n_t - 1):
        c = dt * 128
        seg = [n for n in _DST_ORDER if OFF[n] <= c < OFF[n] + _SRC_OFF[n][1]][0]
        src = _SRC_OFF[seg][0] + (c - OFF[seg])
        if src % 128 == 0:
            assert n_plain is None
        else:
            assert src % 128 == 32
            if n_plain is None:
                n_plain = dt
        table.append(src // 128)
    t_kw, t_ab = _SRC_OFF["ki"][0] // 128, _SRC_OFF["a"][0] // 128
    assert _SRC_OFF["ki"][0] % 128 == 32 and _SRC_OFF["wi"][0] == t_kw * 128 + TAIL_WI
    assert _SRC_OFF["a"][0] % 128 == 0 and _SRC_OFF["b"][0] == _SRC_OFF["a"][0] + DN_HEADS
    assert TAIL_A == IDX_DIM and TAIL_B == TAIL_A + DN_HEADS and TAIL_WI == TAIL_B + DN_HEADS
    tile_a = table + [t_kw]
    rows_b = [t_ab * 4] * n_plain + [(s + 1) * 4 for s in table[n_plain:]] + [t_ab * 4]
    return tile_a, rows_b, n_plain


def _wprep_kernel(ta_ref, tb_ref, a_ref, b_ref, o_ref, *, n_plain, n_t):
    dt = pl.program_id(1)
    dt_o = o_ref.dtype

    @pl.when(dt < n_plain)
    def _():
        o_ref[...] = a_ref[...].astype(dt_o)

    @pl.when((dt >= n_plain) & (dt < n_t - 1))
    def _():
        o_ref[0:96, :] = a_ref[32:128, :].astype(dt_o)
        o_ref[96:128, :] = b_ref[...].astype(dt_o)

    @pl.when(dt == n_t - 1)
    def _():
        o_ref[0:TAIL_A, :] = a_ref[32:32 + IDX_DIM, :].astype(dt_o)
        o_ref[TAIL_A:TAIL_WI, :] = b_ref[...].astype(dt_o)
        o_ref[TAIL_WI:TAIL_WI + IDX_HEADS, :] = a_ref[TAIL_WI:TAIL_WI + IDX_HEADS, :].astype(dt_o)
        o_ref[TAIL_WI + IDX_HEADS:, :] = jnp.zeros((128 - TAIL_WI - IDX_HEADS, o_ref.shape[1]), dt_o)


def _permute_w_in(w_in):
    depth, d, _ = w_in.shape
    w_t = jnp.swapaxes(w_in, 1, 2)
    tile_a, rows_b, n_plain = _w_in_tile_table()
    n_t = NP // 128
    grid_spec = pltpu.PrefetchScalarGridSpec(
        num_scalar_prefetch=2,
        grid=(depth, n_t),
        in_specs=[pl.BlockSpec((None, 128, d), lambda l, t, ta, tb: (l, ta[t], 0)),
                  pl.BlockSpec((None, 32, d), lambda l, t, ta, tb: (l, tb[t], 0))],
        out_specs=pl.BlockSpec((None, 128, d), lambda l, t, ta, tb: (l, t, 0)),
    )
    return pl.pallas_call(
        functools.partial(_wprep_kernel, n_plain=n_plain, n_t=n_t),
        grid_spec=grid_spec,
        out_shape=jax.ShapeDtypeStruct((depth, NP, d), BF16),
        compiler_params=_cp(("arbitrary", "arbitrary"), 32),
        name="w_in_layout",
    )(jnp.asarray(tile_a, I32), jnp.asarray(rows_b, I32), w_t, w_t)


def _delta_t(l):
    for t in (128, 64):
        if l % t == 0:
            return t
    raise ValueError("sequence length must be a multiple of 64")


def kernel(x_prompt, x_sample, cache_k, cache_v, cache_kidx, state_dn, state_conv, page_table,
           c_prompt, c_sample, w_ada, b_ada, g_norm, w_in, a_vnorm, a_ws, a_bs, dn_conv_w,
           dn_a_log, dn_dt_bias, dn_onorm, w_out, g_final):
    bp, lp, d = x_prompt.shape
    bs, ls, _ = x_sample.shape
    depth = w_ada.shape[0]
    assert d == D_MODEL and w_in.shape[2] == D_IN
    assert CONV_K - 1 <= ls <= DN_CHUNK

    n_c = bp + bs
    c_rows = jnp.concatenate([c_prompt, c_sample], axis=0)
    r_pad = (-n_c) % 8
    if r_pad:
        c_rows = jnp.pad(c_rows, ((0, r_pad), (0, 0)))
    m_all = _ada(c_rows, w_ada, b_ada)

    xp = x_prompt.reshape(bp * lp, d)
    xs = x_sample.reshape(bs * ls, d)
    ls_pad = DN_CHUNK
    outs = {k: [] for k in ("pk", "pv", "pki", "pdn", "pconv", "sk", "sv", "ski", "sdn", "sconv", "samlp")}
    zeros_conv = jnp.zeros((bp, CONV_K - 1, DN_CONV_DIM), F32)
    zeros_state = jnp.zeros((bp, DN_HEADS, DN_HEAD, DN_HEAD), F32)

    w_in_all = _permute_w_in(w_in)
    w_out_all = w_out.astype(BF16)
    for l in range(depth):
        g_l = g_norm[l].reshape(1, d)
        m_l = m_all[l]
        m3 = m_l.reshape(m_l.shape[0], 1, 3 * d)
        ms = jnp.repeat(m_l[bp:bp + bs], ls, axis=0)

        pp = _inproj(xp, g_l, m3, m3, w_in_all, l, rows_per_batch=lp, mod_row0=0)
        p3 = pp.reshape(bp, lp, NP)
        (ya,) = _mixa(p3, a_vnorm[l], a_ws[l], a_bs[l], emit_va=False)
        abt = jnp.transpose(p3[:, :, OFF["a"]:OFF["a"] + 2 * DN_HEADS], (0, 2, 1))
        yb, s_p = _delta(p3, abt, dn_conv_w[l], zeros_conv, zeros_state, dn_a_log[l], dn_dt_bias[l],
                         dn_onorm[l], t=_delta_t(lp), valid_len=_delta_t(lp))
        yc = _dsa_prompt(p3)
        xp = _outproj(ya.reshape(bp * lp, W_A), yb.reshape(bp * lp, W_B), yc.reshape(bp * lp, W_C),
                      w_out_all, l, xp, m3, rows_per_batch=lp, mod_row0=0)
        outs["pk"].append(p3[:, :, OFF["kc"]:OFF["kc"] + C_KV].reshape(bp, lp, C_KV_HEADS, C_HEAD))
        outs["pv"].append(p3[:, :, OFF["vc"]:OFF["vc"] + C_KV].reshape(bp, lp, C_KV_HEADS, C_HEAD))
        outs["pki"].append(p3[:, :, OFF_TAIL:OFF_TAIL + IDX_DIM])
        outs["pdn"].append(s_p)
        outs["pconv"].append(p3[:, lp - (CONV_K - 1):, OFF["qkv"]:OFF["qkv"] + DN_CONV_DIM])

        ps = _inproj(xs, g_l, ms[:, d:2 * d], ms[:, 0:d], w_in_all, l, rows_per_batch=ls, mod_row0=bp)
        p3s = ps.reshape(bs, ls, NP)
        ya_s, va_s = _mixa(p3s, a_vnorm[l], a_ws[l], a_bs[l], emit_va=True)
        p3s_pad = jnp.pad(p3s, ((0, 0), (0, ls_pad - ls), (0, 0)))
        abt_s = jnp.transpose(p3s_pad[:, :, OFF["a"]:OFF["a"] + 2 * DN_HEADS], (0, 2, 1))
        yb_s, s_s = _delta(p3s_pad, abt_s, dn_conv_w[l], state_conv[l], state_dn[l], dn_a_log[l],
                           dn_dt_bias[l], dn_onorm[l], t=ls_pad, valid_len=ls)
        yc_s = _dsa_sample(p3s, cache_k, cache_v, cache_kidx, page_table, l)
        xs = _outproj(ya_s.reshape(bs * ls, W_A), yb_s[:, :ls].reshape(bs * ls, W_B),
                      yc_s.reshape(bs * ls, W_C), w_out_all, l, xs, ms[:, 2 * d:3 * d],
                      rows_per_batch=ls, mod_row0=bp)
        outs["sk"].append(p3s[:, :, OFF["kc"]:OFF["kc"] + C_KV].reshape(bs, ls, C_KV_HEADS, C_HEAD))
        outs["sv"].append(p3s[:, :, OFF["vc"]:OFF["vc"] + C_KV].reshape(bs, ls, C_KV_HEADS, C_HEAD))
        outs["ski"].append(p3s[:, :, OFF_TAIL:OFF_TAIL + IDX_DIM])
        outs["sdn"].append(s_s)
        outs["sconv"].append(p3s[:, ls - (CONV_K - 1):, OFF["qkv"]:OFF["qkv"] + DN_CONV_DIM])
        outs["samlp"].append(va_s)

    y_prompt = _final_norm(xp, g_final).reshape(bp, lp, d)
    y_sample = _final_norm(xs, g_final).reshape(bs, ls, d)
    st = jnp.stack
    return (y_prompt, y_sample, st(outs["pk"]), st(outs["pv"]), st(outs["pki"]), st(outs["pdn"]),
            st(outs["pconv"]), st(outs["sk"]), st(outs["sv"]), st(outs["ski"]), st(outs["sdn"]),
            st(outs["sconv"]), st(outs["samlp"]))
```

```python
import functools

import jax
import jax.numpy as jnp
from jax import lax
from jax.experimental import pallas as pl
from jax.experimental.pallas import tpu as pltpu

F32 = jnp.float32
BF16 = jnp.bfloat16
I32 = jnp.int32
EPS = 1e-6
INT_MIN = -(2 ** 31)
NEG_INF = float("-inf")

D_MODEL = 4096
W_A = D_MODEL // 4
A_GROUP = 128
A_HEADS = W_A // A_GROUP
A_CHUNK = 128
W_B = D_MODEL // 2
DN_HEAD = 128
DN_HEADS = W_B // DN_HEAD
CONV_K = 4
DN_CONV_DIM = 3 * W_B
DN_CHUNK = 64
W_C = D_MODEL - W_A - W_B
C_HEAD = 128
C_HEADS = W_C // C_HEAD
C_KV_HEADS = 2
C_GROUPS = C_HEADS // C_KV_HEADS
C_KV = C_KV_HEADS * C_HEAD
IDX_HEADS = 16
IDX_DIM = 64
TOPK_MAX = 256
Q_BLOCK = 128

_SRC_SPLITS = (W_A, W_A, W_A, DN_CONV_DIM, W_B, DN_HEADS, DN_HEADS,
               W_C, C_KV, C_KV, W_C, IDX_HEADS * IDX_DIM, IDX_DIM, IDX_HEADS)
_SRC_NAMES = ("u", "v", "za", "qkv", "zb", "a", "b", "qc", "kc", "vc", "zc", "qi", "ki", "wi")
_SRC_OFF = {}
_o = 0
for _n, _w in zip(_SRC_NAMES, _SRC_SPLITS):
    _SRC_OFF[_n] = (_o, _w)
    _o += _w
D_IN = _o

_DST_ORDER = ("u", "v", "za", "qkv", "zb", "qc", "zc", "qi", "kc", "vc", "ki", "a", "b", "wi")
OFF = {}
_o = 0
for _n in _DST_ORDER:
    OFF[_n] = _o
    _o += _SRC_OFF[_n][1]
NP = ((_o + 127) // 128) * 128
OFF_TAIL = OFF["ki"]
TAIL_A = OFF["a"] - OFF_TAIL
TAIL_B = OFF["b"] - OFF_TAIL
TAIL_WI = OFF["wi"] - OFF_TAIL

MXU_COLUMNS = 256
INPROJ_TN = 6 * MXU_COLUMNS
DELTA_HEADS_PER_STEP = 8
DSA_KEY_TILE = 512

VMEM_CAPACITY_MIB = 64
VMEM_LIMIT_MIB = {
    "ada": 40, "w_in_layout": 32, "inproj": 56, "mixa": 32, "delta": 40, "dsa_prompt": 48,
    "dsa_s_score": 32, "dsa_s_select": 40, "dsa_s_attend": 40, "outproj": 48, "final_norm": 32,
}
assert max(VMEM_LIMIT_MIB.values()) < VMEM_CAPACITY_MIB


def _cp(name, n_axes):
    return pltpu.CompilerParams(dimension_semantics=("arbitrary",) * n_axes,
                                vmem_limit_bytes=VMEM_LIMIT_MIB[name] * 1024 * 1024)


def _silu(x):
    return (0.5 * x) * (1.0 + jnp.tanh(0.5 * x))


def _gelu(x):
    return 0.5 * x * (1.0 + jnp.tanh(0.7978845608028654 * (x + 0.044715 * (x * x * x))))


def _dot(a, b):
    return jnp.dot(a, b, preferred_element_type=F32)


def _dot_nt(a, b):
    return lax.dot_general(a, b, (((1,), (1,)), ((), ())), preferred_element_type=F32)


def _dot_tn(a, b):
    return lax.dot_general(a, b, (((0,), (0,)), ((), ())), preferred_element_type=F32)


def _b16(a):
    return a.astype(BF16)


def _ada_kernel(c_ref, w_ref, b_ref, o_ref):
    s = _silu(c_ref[...]).astype(BF16)
    o_ref[...] = _dot(s, w_ref[...].astype(BF16)) + b_ref[...]


def _ada(c_rows, w_ada, b_ada):
    depth, d, n = w_ada.shape
    r = c_rows.shape[0]
    tn = 512
    return pl.pallas_call(
        _ada_kernel,
        grid=(depth, n // tn),
        in_specs=[pl.BlockSpec((r, d), lambda l, j: (0, 0)),
                  pl.BlockSpec((None, d, tn), lambda l, j: (l, 0, j)),
                  pl.BlockSpec((None, 1, tn), lambda l, j: (l, 0, j))],
        out_specs=pl.BlockSpec((None, r, tn), lambda l, j: (l, 0, j)),
        out_shape=jax.ShapeDtypeStruct((depth, r, n), F32),
        compiler_params=_cp("ada", 2),
        name="ada",
    )(c_rows, w_ada, b_ada.reshape(depth, 1, n))


def _inproj_kernel(x_ref, g_ref, sc_ref, sh_ref, w_ref, o_ref, h_scr, *, rc):
    @pl.when(pl.program_id(1) == 0)
    def _():
        tm = x_ref.shape[0]
        per_row = sc_ref.shape[0] != 1

        def body(c, carry):
            r = pl.ds(pl.multiple_of(c * rc, rc), rc)
            x = x_ref[r, :]
            y = x * lax.rsqrt(jnp.mean(x * x, axis=-1, keepdims=True) + EPS)
            sc = sc_ref[r, :] if per_row else sc_ref[...]
            sh = sh_ref[r, :] if per_row else sh_ref[...]
            h_scr[r, :] = ((y * g_ref[...]) * (1.0 + sc) + sh).astype(BF16)
            return carry

        lax.fori_loop(0, tm // rc, body, 0)

    o_ref[...] = _dot_nt(h_scr[...], w_ref[...])


def _inproj(x2, g, sc, sh, w_bf16, layer, *, rows_per_batch, mod_row0):
    m, d = x2.shape
    n = w_bf16.shape[1]
    tn = INPROJ_TN
    if sc.ndim == 3:
        tm = 512
        assert rows_per_batch % tm == 0
        sc_spec = pl.BlockSpec((None, 1, d), lambda i, j: ((i * tm) // rows_per_batch + mod_row0, 0, 1))
        sh_spec = pl.BlockSpec((None, 1, d), lambda i, j: ((i * tm) // rows_per_batch + mod_row0, 0, 0))
        rc = 64
    else:
        tm = m
        sc_spec = pl.BlockSpec((tm, d), lambda i, j: (i, 0))
        sh_spec = pl.BlockSpec((tm, d), lambda i, j: (i, 0))
        rc = tm
    assert m % tm == 0
    return pl.pallas_call(
        functools.partial(_inproj_kernel, rc=rc),
        grid=(m // tm, pl.cdiv(n, tn)),
        in_specs=[pl.BlockSpec((tm, d), lambda i, j: (i, 0)),
                  pl.BlockSpec((1, d), lambda i, j: (0, 0)),
                  sc_spec, sh_spec,
                  pl.BlockSpec((None, tn, d), lambda i, j: (layer, j, 0))],
        out_specs=pl.BlockSpec((tm, tn), lambda i, j: (i, j)),
        out_shape=jax.ShapeDtypeStruct((m, n), F32),
        scratch_shapes=[pltpu.VMEM((tm, d), BF16)],
        compiler_params=_cp("inproj", 2),
        name="inproj",
    )(x2, g, sc, sh, w_bf16)


def _mixa_kernel(u_ref, v_ref, z_ref, vn_ref, ws_ref, bst_ref, y_ref, *rest, c, emit_va):
    u = _gelu(u_ref[...])
    v = _gelu(v_ref[...])
    mu = jnp.mean(v, axis=-1, keepdims=True)
    dv = v - mu
    va = dv * lax.rsqrt(jnp.mean(dv * dv, axis=-1, keepdims=True) + EPS) * vn_ref[...]
    if emit_va:
        rest[0][...] = va
    z = _silu(z_ref[...])
    row = lax.broadcasted_iota(I32, (c, c), 0)
    col = lax.broadcasted_iota(I32, (c, c), 1)
    tril = col <= row
    for h in range(A_HEADS):
        cols = slice(h * A_GROUP, (h + 1) * A_GROUP)
        wm = jnp.where(tril, ws_ref[h], 0.0)
        wm16 = wm.astype(BF16)
        for ci in range(u.shape[0] // c):
            rows = slice(ci * c, (ci + 1) * c)
            vh = va[rows, cols]
            if c >= 128:
                mixed = _dot(wm16, vh.astype(BF16))
            else:
                mixed = wm[:, 0:1] * vh[0:1, :]
                for s in range(1, c):
                    mixed = mixed + wm[:, s:s + 1] * vh[s:s + 1, :]
            mixed = mixed + bst_ref[:, h:h + 1]
            y_ref[rows, cols] = (u[rows, cols] * mixed * z[rows, cols]).astype(y_ref.dtype)


def _mixa(p3, a_vnorm, a_ws, a_bs, *, emit_va):
    b, l, _ = p3.shape
    c = min(A_CHUNK, l)
    r = 2 * c if l % (2 * c) == 0 else c
    n = l // r
    ws = a_ws[:, :c, :c]
    bst = a_bs[:, :c].T
    wblk = W_A
    outs = [jax.ShapeDtypeStruct((b, l, W_A), BF16)]
    out_specs = [pl.BlockSpec((None, r, W_A), lambda i, j: (i, j, 0))]
    if emit_va:
        outs.append(jax.ShapeDtypeStruct((b, l, W_A), F32))
        out_specs.append(pl.BlockSpec((None, r, W_A), lambda i, j: (i, j, 0)))
    res = pl.pallas_call(
        functools.partial(_mixa_kernel, c=c, emit_va=emit_va),
        grid=(b, n),
        in_specs=[pl.BlockSpec((None, r, wblk), lambda i, j: (i, j, OFF["u"] // wblk)),
                  pl.BlockSpec((None, r, wblk), lambda i, j: (i, j, OFF["v"] // wblk)),
                  pl.BlockSpec((None, r, wblk), lambda i, j: (i, j, OFF["za"] // wblk)),
                  pl.BlockSpec((1, W_A), lambda i, j: (0, 0)),
                  pl.BlockSpec((A_HEADS, c, c), lambda i, j: (0, 0, 0)),
                  pl.BlockSpec((c, A_HEADS), lambda i, j: (0, 0))],
        out_specs=out_specs,
        out_shape=outs,
        compiler_params=_cp("mixa", 2),
        name="mixa",
    )(p3, p3, p3, a_vnorm.reshape(1, W_A), ws, bst)
    return res


def _delta_kernel(alog_ref, dtb_ref,
                  q_ref, k_ref, v_ref, z_ref, ab_ref,
                  cwq_ref, cwk_ref, cwv_ref, cpq_ref, cpk_ref, cpv_ref,
                  s0_ref, on_ref,
                  y_ref, sout_ref,
                  xbuf, s_scr, *, t, c, hp, valid_len):
    n = pl.program_id(2)
    dh = DN_HEAD

    @pl.when(n == 0)
    def _():
        s_scr[...] = s0_ref[...]
        for j in range(hp):
            lanes = slice(j * dh, (j + 1) * dh)
            xbuf[3 * j + 0, 5:8, :] = cpq_ref[:, lanes]
            xbuf[3 * j + 1, 5:8, :] = cpk_ref[:, lanes]
            xbuf[3 * j + 2, 5:8, :] = cpv_ref[:, lanes]

    _delta_heads(alog_ref, dtb_ref, q_ref, k_ref, v_ref, z_ref, ab_ref, cwq_ref, cwk_ref, cwv_ref,
                 on_ref, y_ref, xbuf, s_scr, t=t, c=c, hp=hp, valid_len=valid_len)

    @pl.when(n == pl.num_programs(2) - 1)
    def _():
        sout_ref[...] = s_scr[...]


def _delta_heads(alog_ref, dtb_ref, q_ref, k_ref, v_ref, z_ref, ab_ref, cwq_ref, cwk_ref, cwv_ref,
                 on_ref, y_ref, xbuf, s_scr, *, t, c, hp, valid_len):
    hg = pl.program_id(1)
    dh = DN_HEAD
    heads = list(range(hp))

    row = lax.broadcasted_iota(I32, (t, t), 0)
    col = lax.broadcasted_iota(I32, (t, t), 1)
    shift = c.bit_length() - 1
    same = (row >> shift) == (col >> shift)
    eye = row == col
    incl = same & (col <= row)
    strict = same & (col < row)
    incl_t = same & (row <= col)
    blk8 = (row >> 3) == (col >> 3)
    off_masks = []
    bs = 8
    while bs < c:
        sh_b = bs.bit_length() - 1
        inner = (row >> sh_b) == (col >> sh_b)
        outer = (row >> (sh_b + 1)) == (col >> (sh_b + 1))
        off_masks.append(outer & jnp.logical_not(inner))
        bs *= 2
    eye_f = jnp.where(eye, 1.0, 0.0)
    if valid_len < t:
        lane_valid = lax.broadcasted_iota(I32, (1, t), 1) < valid_len
        sub_valid = lax.broadcasted_iota(I32, (t, 1), 0) < valid_len

    def to_col(r):
        return jnp.sum(jnp.where(eye, r, 0.0), axis=1, keepdims=True)

    def conv(idx, x, w_ref, lanes):
        xbuf[idx, 8:8 + t, :] = x
        y = xbuf[idx, pl.ds(5, t), :] * w_ref[0:1, lanes]
        for jj in range(1, CONV_K):
            y = y + xbuf[idx, pl.ds(5 + jj, t), :] * w_ref[jj:jj + 1, lanes]
        xbuf[idx, 5:8, :] = x[t - 3:t, :]
        return _silu(y)

    lanes_of = [slice(j * dh, (j + 1) * dh) for j in range(hp)]

    def per_head(f, *lists):
        return [f(*vals) for vals in zip(*lists)]

    def l2n(a):
        return a * lax.rsqrt(jnp.sum(a * a, axis=-1, keepdims=True) + EPS)

    qc = [l2n(conv(3 * j + 0, q_ref[:, lanes_of[j]], cwq_ref, lanes_of[j])) * (dh ** -0.5) for j in heads]
    kc = [l2n(conv(3 * j + 1, k_ref[:, lanes_of[j]], cwk_ref, lanes_of[j])) for j in heads]
    v = [conv(3 * j + 2, v_ref[:, lanes_of[j]], cwv_ref, lanes_of[j]) for j in heads]

    def gates(j):
        h = hg * hp + j
        a_row = ab_ref[pl.ds(h, 1), :]
        b_row = ab_ref[pl.ds(DN_HEADS + h, 1), :]
        xa = a_row + dtb_ref[h]
        softplus = jnp.maximum(xa, 0.0) + jnp.log(1.0 + jnp.exp(-jnp.abs(xa)))
        a_coef = jnp.exp(jnp.zeros((1, 1), F32) + alog_ref[h])
        g_row = -a_coef * softplus
        beta_row = 1.0 / (1.0 + jnp.exp(-b_row))
        if valid_len < t:
            g_row = jnp.where(lane_valid, g_row, 0.0)
            beta_row = jnp.where(lane_valid, beta_row, 0.0)
        return g_row, beta_row

    g_row, beta_row = zip(*[gates(j) for j in heads])
    if valid_len < t:
        kc = per_head(lambda a: jnp.where(sub_valid, a, 0.0), kc)
        v = per_head(lambda a: jnp.where(sub_valid, a, 0.0), v)

    g_col = per_head(to_col, g_row)
    beta_col = per_head(to_col, beta_row)
    gc_col = per_head(lambda r: jnp.sum(jnp.where(incl, r, 0.0), axis=1, keepdims=True), g_row)
    glast_col = per_head(lambda r: jnp.sum(jnp.where(same, r, 0.0), axis=1, keepdims=True), g_row)
    gc_row = per_head(lambda cl: jnp.sum(jnp.where(incl_t, cl, 0.0), axis=0, keepdims=True), g_col)
    kb = per_head(lambda a, b: a * b, kc, beta_col)
    kc16 = per_head(lambda a: a.astype(BF16), kc)

    def key_products(kb_, kc16_, qc_, gc, gr):
        decay = jnp.where(incl, jnp.exp(jnp.where(incl, gc - gr, 0.0)), 0.0)
        a_mat = jnp.where(strict, _dot_nt(kb_.astype(BF16), kc16_) * decay, 0.0)
        attn16 = (_dot_nt(qc_.astype(BF16), kc16_) * decay).astype(BF16)
        n0f = jnp.where(blk8, -a_mat, 0.0)
        return attn16, _b16(n0f), eye_f + n0f, [_b16(jnp.where(om, a_mat, 0.0)) for om in off_masks]

    attn, n0, x, a_offs = zip(*per_head(key_products, kb, kc16, qc, gc_col, gc_row))
    eg = per_head(jnp.exp, gc_col)
    rhs = per_head(lambda vv, bc, kk, e: _b16(jnp.concatenate([vv * bc, kk * e], axis=1)),
                   v, beta_col, kb, eg)
    n2 = per_head(lambda a: _b16(_dot(a, a)), n0)
    n4 = per_head(lambda a: _b16(_dot(a, a)), n2)
    x = per_head(lambda xx, nn: xx + _dot(_b16(xx), nn), x, n2)
    x = per_head(lambda xx, nn: xx + _dot(_b16(xx), nn), x, n4)
    for lvl in range(len(off_masks)):
        xs = per_head(_b16, x)
        xa_off = per_head(lambda s_, offs: _b16(_dot(s_, offs[lvl])), xs, a_offs)
        x = per_head(lambda xx, xo, s_: xx - _dot(xo, s_), x, xa_off, xs)
    sol16 = per_head(lambda xx, r: _b16(_dot(_b16(xx), r)), x, rhs)
    auw = per_head(_dot, attn, sol16)
    qw = per_head(lambda a, e, m_: (a * e - m_[:, dh:]).astype(BF16), qc, eg, auw)
    kg16 = per_head(lambda a, gl_, gc: (a * jnp.exp(gl_ - gc)).astype(BF16), kc, glast_col, gc_col)

    s = [s_scr[j] for j in heads]
    outs = [[] for _ in heads]
    for i in range(t // c):
        rows = slice(i * c, (i + 1) * c)
        s16 = per_head(lambda a: a.astype(BF16), s)
        kuw = per_head(lambda a, b: _dot_tn(a[rows], b[rows]), kg16, sol16)
        o_i = per_head(lambda a, b16, m_: _dot(a[rows], b16) + m_[rows, :dh], qw, s16, auw)
        for pos in range(len(heads)):
            outs[pos].append(o_i[pos])
        gl = per_head(lambda a: jnp.exp(a[i * c:i * c + 1, :]), glast_col)
        s = per_head(lambda g_, s_, m_, b16: g_ * s_ + m_[:, :dh] - _dot(m_[:, dh:].astype(BF16), b16),
                     gl, s, kuw, s16)
    for pos, j in enumerate(heads):
        s_scr[j] = s[pos]
        o = outs[pos][0] if len(outs[pos]) == 1 else jnp.concatenate(outs[pos], axis=0)
        on = o * lax.rsqrt(jnp.mean(o * o, axis=-1, keepdims=True) + EPS) * on_ref[...]
        y_ref[:, lanes_of[j]] = (on * _silu(z_ref[:, lanes_of[j]])).astype(y_ref.dtype)


def _delta(p3, abt, conv_w, conv_prev, s0, a_log, dt_bias, onorm, *, t, valid_len):
    b, l, _ = p3.shape
    assert l % t == 0
    c = min(DN_CHUNK, t)
    hp = DELTA_HEADS_PER_STEP
    nh = DN_HEADS
    ng = nh // hp
    w = 128 * hp
    cb_q = OFF["qkv"] // w
    cb_z = OFF["zb"] // w
    assert OFF["qkv"] % w == 0 and OFF["zb"] % w == 0 and W_B % w == 0

    def pspec(cb0):
        return pl.BlockSpec((None, t, w), lambda i, h, n, a, d: (i, n, cb0 + h))

    def cwspec(sidx):
        return pl.BlockSpec((CONV_K, w), lambda i, h, n, a, d: (0, sidx * ng + h))

    def cpspec(sidx):
        return pl.BlockSpec((None, CONV_K - 1, w), lambda i, h, n, a, d: (i, 0, sidx * ng + h))

    grid_spec = pltpu.PrefetchScalarGridSpec(
        num_scalar_prefetch=2,
        grid=(b, ng, l // t),
        in_specs=[pspec(cb_q), pspec(cb_q + ng), pspec(cb_q + 2 * ng), pspec(cb_z),
                  pl.BlockSpec((None, 2 * nh, t), lambda i, h, n, a, d: (i, 0, n)),
                  cwspec(0), cwspec(1), cwspec(2), cpspec(0), cpspec(1), cpspec(2),
                  pl.BlockSpec((None, hp, DN_HEAD, DN_HEAD), lambda i, h, n, a, d: (i, h, 0, 0)),
                  pl.BlockSpec((1, DN_HEAD), lambda i, h, n, a, d: (0, 0))],
        out_specs=[pl.BlockSpec((None, t, w), lambda i, h, n, a, d: (i, n, h)),
                   pl.BlockSpec((None, hp, DN_HEAD, DN_HEAD), lambda i, h, n, a, d: (i, h, 0, 0))],
        scratch_shapes=[pltpu.VMEM((3 * hp, t + 8, 128), F32), pltpu.VMEM((hp, DN_HEAD, DN_HEAD), F32)],
    )
    return pl.pallas_call(
        functools.partial(_delta_kernel, t=t, c=c, hp=hp, valid_len=valid_len),
        grid_spec=grid_spec,
        out_shape=[jax.ShapeDtypeStruct((b, l, W_B), BF16),
                   jax.ShapeDtypeStruct((b, nh, DN_HEAD, DN_HEAD), F32)],
        compiler_params=_cp("delta", 3),
        name="delta",
    )(a_log, dt_bias, p3, p3, p3, p3, abt, conv_w, conv_w, conv_w,
      conv_prev, conv_prev, conv_prev, s0, onorm.reshape(1, DN_HEAD))


def _ordered_bits_to_float(u):
    key = u ^ jnp.int32(INT_MIN)
    bits = jnp.where(key < 0, key ^ jnp.int32(0x7FFFFFFF), key)
    return pltpu.bitcast(bits, F32)


def _topk_threshold(count_ge, shape, topk):
    def bit_body(i, u):
        bit = jnp.left_shift(jnp.int32(1), 31 - i)
        cand_u = u | bit
        cnt = count_ge(_ordered_bits_to_float(cand_u))
        return jnp.where(cnt >= float(topk), cand_u, u)

    u = lax.fori_loop(0, 32, bit_body, jnp.zeros(shape, I32))
    return _ordered_bits_to_float(u)


def _softmax_update(m_ref, l_ref, acc_ref, idx, s, v_tiles):
    m_old = m_ref[idx]
    m_new = jnp.maximum(m_old, jnp.max(s, axis=1, keepdims=True))
    m_safe = jnp.where(m_new == NEG_INF, 0.0, m_new)
    alpha = jnp.exp(m_old - m_safe)
    p = jnp.exp(s - m_safe)
    l_ref[idx] = alpha * l_ref[idx] + jnp.sum(p, axis=1, keepdims=True)
    acc = alpha * acc_ref[idx]
    p16 = p.astype(BF16)
    k0 = 0
    for vt in v_tiles:
        acc = acc + _dot(p16[:, k0:k0 + vt.shape[0]], vt)
        k0 += vt.shape[0]
    acc_ref[idx] = acc
    m_ref[idx] = m_new


def _dsa_prompt_kernel(qi_ref, tq_ref, tall_ref, qc_ref, k_ref, v_ref, z_ref, y_ref,
                       sc_scr, m_scr, l_scr, acc_scr, *, l, topk, kt):
    qb = pl.program_id(1)
    nq = Q_BLOCK
    n_tiles = (qb * nq + nq + kt - 1) // kt
    n_pairs = C_HEADS // 2
    w_rows = tq_ref[...].T[TAIL_WI:TAIL_WI + IDX_HEADS, :] * ((IDX_HEADS ** -0.5) * (IDX_DIM ** -0.5))
    rq = []
    for j in range(IDX_HEADS // 2):
        a = qi_ref[:, j * 128:(j + 1) * 128]
        rq.append(jnp.concatenate([a, pltpu.roll(a, IDX_DIM, 1)], axis=0).astype(BF16))
    kpos = lax.broadcasted_iota(I32, (kt, nq), 0)
    qpos = qb * nq + lax.broadcasted_iota(I32, (kt, nq), 1)
    lane = lax.broadcasted_iota(I32, (kt, 128), 1)

    def score_tile(ti, carry):
        k0 = pl.multiple_of(ti * kt, kt)
        ki = jnp.where(lane < IDX_DIM, tall_ref[pl.ds(k0, kt), :], 0.0).astype(BF16)
        score = jnp.zeros((kt, nq), F32)
        for j in range(IDX_HEADS // 2):
            lg = _dot_nt(ki, rq[j])
            score = score + w_rows[2 * j:2 * j + 1, :] * jnp.maximum(lg[:, :nq], 0.0)
            score = score + w_rows[2 * j + 1:2 * j + 2, :] * jnp.maximum(lg[:, nq:], 0.0)
        sc_scr[pl.ds(k0, kt), :] = jnp.where(kpos + k0 <= qpos, score, NEG_INF)
        return carry

    lax.fori_loop(0, n_tiles, score_tile, 0)

    ct = 256
    n_ct = (qb * nq + nq + ct - 1) // ct
    n_acc = 4

    def count(cand, strict):
        cand8 = jnp.broadcast_to(cand, (8, nq))

        def tile_body(ti, accs):
            k0 = pl.multiple_of(ti * ct, ct)
            tile = sc_scr[pl.ds(k0, ct), :]
            accs = list(accs)
            for r in range(ct // 8):
                blk = tile[8 * r:8 * r + 8, :]
                hit = (blk > cand8) if strict else (blk >= cand8)
                accs[r % n_acc] = accs[r % n_acc] + jnp.where(hit, 1.0, 0.0)
            return tuple(accs)

        accs = lax.fori_loop(0, n_ct, tile_body, tuple(jnp.zeros((8, nq), F32) for _ in range(n_acc)))
        acc = (accs[0] + accs[1]) + (accs[2] + accs[3])
        return jnp.sum(acc, axis=0, keepdims=True)

    thr = _topk_threshold(lambda cand: count(cand, False), (1, nq), topk)
    keep_all = (qb * nq + lax.broadcasted_iota(I32, (1, nq), 1)) < topk
    need = float(topk) - count(thr, True)
    tri16 = jnp.where(lax.broadcasted_iota(I32, (128, 128), 1) <= lax.broadcasted_iota(I32, (128, 128), 0),
                      1.0, 0.0).astype(BF16)

    def select_tile(ti, seen):
        k0 = pl.multiple_of(ti * kt, kt)
        sc = sc_scr[pl.ds(k0, kt), :]
        eq = sc == thr
        eq_f = jnp.where(eq, 1.0, 0.0)
        eq16 = eq_f.astype(BF16)
        prefs = [_dot(tri16, eq16[r:r + 128]) for r in range(0, kt, 128)]
        ranks = []
        for pref in prefs:
            ranks.append(seen + pref)
            seen = seen + pref[127:128, :]
        tie_ok = (jnp.concatenate(ranks, axis=0) - eq_f) < need
        chosen = jnp.where(sc > thr, 0.0, jnp.where(eq, jnp.where(tie_ok, 0.0, NEG_INF), NEG_INF))
        sc_scr[pl.ds(k0, kt), :] = jnp.where(keep_all, jnp.where(sc > NEG_INF, 0.0, NEG_INF), chosen)
        return seen

    lax.fori_loop(0, n_tiles, select_tile, jnp.zeros((1, nq), F32))

    m_scr[...] = jnp.full(m_scr.shape, NEG_INF, F32)
    l_scr[...] = jnp.zeros(l_scr.shape, F32)
    acc_scr[...] = jnp.zeros(acc_scr.shape, F32)
    scale = (C_HEAD ** -0.5) * 1.4426950408889634
    qp = []
    for pr in range(n_pairs):
        c0 = slice(2 * pr * C_HEAD, (2 * pr + 1) * C_HEAD)
        c1 = slice((2 * pr + 1) * C_HEAD, (2 * pr + 2) * C_HEAD)
        qp.append(jnp.concatenate([qc_ref[:, c0] * scale, qc_ref[:, c1] * scale], axis=0).astype(BF16))
    pairs_per_kv = n_pairs // C_KV_HEADS

    def attend_tile(ti, carry):
        k0 = pl.multiple_of(ti * kt, kt)
        bias = sc_scr[pl.ds(k0, kt), :]
        k16 = [k_ref[pl.ds(k0, kt), hk * C_HEAD:(hk + 1) * C_HEAD].astype(BF16) for hk in range(C_KV_HEADS)]
        vt16 = [v_ref[pl.ds(k0, kt), hk * C_HEAD:(hk + 1) * C_HEAD].T.astype(BF16) for hk in range(C_KV_HEADS)]
        bias2 = jnp.concatenate([bias, bias], axis=1)
        pr_all = range(n_pairs)
        s = [_dot_nt(k16[pr // pairs_per_kv], qp[pr]) + bias2 for pr in pr_all]
        m_old = [m_scr[pr] for pr in pr_all]
        m_new = [jnp.maximum(m_old[pr], jnp.max(s[pr], axis=0, keepdims=True)) for pr in pr_all]
        m_safe = [jnp.where(m_new[pr] == NEG_INF, 0.0, m_new[pr]) for pr in pr_all]
        alpha = [jnp.exp2(m_old[pr] - m_safe[pr]) for pr in pr_all]
        p = [jnp.exp2(s[pr] - m_safe[pr]) for pr in pr_all]
        for pr in pr_all:
            l_scr[pr] = alpha[pr] * l_scr[pr] + jnp.sum(p[pr], axis=0, keepdims=True)
            acc_scr[pr] = alpha[pr] * acc_scr[pr] + _dot(vt16[pr // pairs_per_kv], p[pr].astype(BF16))
            m_scr[pr] = m_new[pr]
        return carry

    lax.fori_loop(0, n_tiles, attend_tile, 0)
    for pr in range(n_pairs):
        o_t = acc_scr[pr] / l_scr[pr]
        for e in range(2):
            cols = slice((2 * pr + e) * C_HEAD, (2 * pr + e + 1) * C_HEAD)
            o = o_t[:, e * nq:(e + 1) * nq].T
            y_ref[:, cols] = (o * _silu(z_ref[:, cols])).astype(y_ref.dtype)


def _dsa_prompt(p3):
    b, l, _ = p3.shape
    assert l % Q_BLOCK == 0
    topk = min(TOPK_MAX, l // 4)
    nq = Q_BLOCK
    kt = DSA_KEY_TILE if l % DSA_KEY_TILE == 0 else 256
    assert l % kt == 0
    n_qb = l // nq
    return pl.pallas_call(
        functools.partial(_dsa_prompt_kernel, l=l, topk=topk, kt=kt),
        grid=(b, n_qb),
        in_specs=[pl.BlockSpec((None, nq, W_C), lambda i, j: (i, j, OFF["qi"] // W_C)),
                  pl.BlockSpec((None, nq, 128), lambda i, j: (i, j, OFF_TAIL // 128)),
                  pl.BlockSpec((None, l, 128), lambda i, j: (i, 0, OFF_TAIL // 128)),
                  pl.BlockSpec((None, nq, W_C), lambda i, j: (i, j, OFF["qc"] // W_C)),
                  pl.BlockSpec((None, l, C_KV), lambda i, j: (i, 0, OFF["kc"] // C_KV)),
                  pl.BlockSpec((None, l, C_KV), lambda i, j: (i, 0, OFF["vc"] // C_KV)),
                  pl.BlockSpec((None, nq, W_C), lambda i, j: (i, j, OFF["zc"] // W_C))],
        out_specs=pl.BlockSpec((None, nq, W_C), lambda i, j: (i, j, 0)),
        out_shape=jax.ShapeDtypeStruct((b, l, W_C), BF16),
        scratch_shapes=[pltpu.VMEM((l, nq), F32),
                        pltpu.VMEM((C_HEADS // 2, 1, 2 * nq), F32), pltpu.VMEM((C_HEADS // 2, 1, 2 * nq), F32),
                        pltpu.VMEM((C_HEADS // 2, C_HEAD, 2 * nq), F32)],
        compiler_params=_cp("dsa_prompt", 2),
        name="dsa_prompt",
    )(p3, p3, p3, p3, p3, p3, p3)


def _idx_scores(logits, wcol, t):
    r = jnp.maximum(logits, 0.0) * wcol
    return jnp.sum(r.reshape(t, IDX_HEADS, logits.shape[1]), axis=1)


def _dsa_s_score_kernel(pt_ref, q_ref, w_ref, kn_ref, *refs, pg, t):
    pages = refs[:pg]
    out_ref, new_ref = refs[pg:]
    q16 = q_ref[...].astype(BF16)
    wcol = w_ref[...] * ((IDX_HEADS ** -0.5) * (IDX_DIM ** -0.5))
    for i in range(pg):
        out_ref[:, i * 128:(i + 1) * 128] = _idx_scores(_dot(q16, pages[i][...].astype(BF16)), wcol, t)

    @pl.when(pl.program_id(1) == 0)
    def _():
        new_ref[...] = _idx_scores(_dot_nt(q16, kn_ref[...].astype(BF16)), wcol, t)


def _dsa_s_select_kernel(sc_ref, scn_ref, tq_ref, bias_ref, *, nk, topk):
    rows = sc_ref.shape[0]
    tq = tq_ref[...]
    new_ok = lax.broadcasted_iota(I32, (rows, 128), 1) <= tq
    sc_new = jnp.where(new_ok, scn_ref[...], NEG_INF)
    sc_past = sc_ref[...]

    def count_ge(cand):
        c1 = jnp.sum(jnp.where(sc_past >= cand, 1.0, 0.0), axis=1, keepdims=True)
        c2 = jnp.sum(jnp.where(sc_new >= cand, 1.0, 0.0), axis=1, keepdims=True)
        return c1 + c2

    thr = _topk_threshold(count_ge, (rows, 1), topk)
    keep_all = (nk + 1 + tq[:, 0:1]) <= topk
    n_gt = (jnp.sum(jnp.where(sc_past > thr, 1.0, 0.0), axis=1, keepdims=True)
            + jnp.sum(jnp.where(sc_new > thr, 1.0, 0.0), axis=1, keepdims=True))
    need = float(topk) - n_gt
    tri16 = jnp.where(lax.broadcasted_iota(I32, (128, 128), 0) <= lax.broadcasted_iota(I32, (128, 128), 1),
                      1.0, 0.0).astype(BF16)

    def bias_blocks(scs, seen):
        eqs = [sc == thr for sc in scs]
        eq_fs = [jnp.where(eq, 1.0, 0.0) for eq in eqs]
        prefs = [_dot(eq_f.astype(BF16), tri16) for eq_f in eq_fs]
        out = []
        for sc, eq, eq_f, pref in zip(scs, eqs, eq_fs, prefs):
            tie_ok = (seen + pref - eq_f) < need
            chosen = jnp.where(sc > thr, 0.0, jnp.where(eq, jnp.where(tie_ok, 0.0, NEG_INF), NEG_INF))
            out.append(jnp.where(keep_all, jnp.where(sc > NEG_INF, 0.0, NEG_INF), chosen))
            seen = seen + pref[:, 127:128]
        return out, seen

    n_blk = nk // 128
    grp = 8 if n_blk % 8 == 0 else 1

    def past_group(gi, seen):
        lanes = [pl.ds(pl.multiple_of((gi * grp + i) * 128, 128), 128) for i in range(grp)]
        out, seen = bias_blocks([sc_ref[:, ln] for ln in lanes], seen)
        for ln, bias in zip(lanes, out):
            bias_ref[:, ln] = bias
        return seen

    seen = lax.fori_loop(0, n_blk // grp, past_group, jnp.zeros((rows, 1), F32))
    (bias_new,), _ = bias_blocks([sc_new], seen)
    bias_ref[:, nk:] = bias_new


def _dsa_s_attend_kernel(pt_ref, q_ref, bias_ref, biasn_ref, kn_ref, vn_ref, z_ref, *refs, pg):
    kpages = refs[:pg]
    vpages = refs[pg:2 * pg]
    o_ref = refs[2 * pg]
    m_scr, l_scr, acc_scr = refs[2 * pg + 1:]
    g = pl.program_id(1)
    scale = C_HEAD ** -0.5
    page = kpages[0].shape[0] // C_KV_HEADS

    @pl.when(g == 0)
    def _():
        m_scr[...] = jnp.full(m_scr.shape, NEG_INF, F32)
        l_scr[...] = jnp.zeros(l_scr.shape, F32)
        acc_scr[...] = jnp.zeros(acc_scr.shape, F32)
        for hk in range(C_KV_HEADS):
            cols = slice(hk * C_HEAD, (hk + 1) * C_HEAD)
            q16 = (q_ref[hk] * scale).astype(BF16)
            s = _dot_nt(q16, kn_ref[:, cols].astype(BF16)) + biasn_ref[...]
            _softmax_update(m_scr, l_scr, acc_scr, hk, s, [vn_ref[:, cols].astype(BF16)])

    def two_pages(refs_, i, rows):
        return jnp.concatenate([refs_[i][rows, :], refs_[i + 1][rows, :]], axis=0).astype(BF16)

    for hk in range(C_KV_HEADS):
        q16 = (q_ref[hk] * scale).astype(BF16)
        rows = pl.ds(hk, page, stride=C_KV_HEADS)
        tiles = [_dot_nt(q16, two_pages(kpages, i, rows)) for i in range(0, pg, 2)]
        s = jnp.concatenate(tiles, axis=1) + bias_ref[...]
        _softmax_update(m_scr, l_scr, acc_scr, hk, s, [two_pages(vpages, i, rows) for i in range(0, pg, 2)])

    @pl.when(g == pl.num_programs(1) - 1)
    def _():
        for hk in range(C_KV_HEADS):
            o_ref[hk] = (acc_scr[hk] / l_scr[hk]) * _silu(z_ref[hk])


def _dsa_sample(p3s, cache_k, cache_v, cache_kidx, page_table, layer):
    b, t, _ = p3s.shape
    n_pages = page_table.shape[1]
    page = cache_k.shape[2]
    assert page == 128
    past = n_pages * page
    topk = min(TOPK_MAX, (past + t) // 4)
    pg = 32 if n_pages % 32 == 0 else (16 if n_pages % 16 == 0 else n_pages)
    assert pg % 2 == 0
    ng = n_pages // pg
    rows = t * C_GROUPS

    qi = p3s[:, :, OFF["qi"]:OFF["qi"] + IDX_HEADS * IDX_DIM].reshape(b, t * IDX_HEADS, IDX_DIM)
    wi = p3s[:, :, OFF_TAIL + TAIL_WI:OFF_TAIL + TAIL_WI + IDX_HEADS].reshape(b, t * IDX_HEADS, 1)
    ki_new = jnp.pad(p3s[:, :, OFF_TAIL:OFF_TAIL + IDX_DIM], ((0, 0), (0, 128 - t), (0, 0)))
    k_new = jnp.pad(p3s[:, :, OFF["kc"]:OFF["kc"] + C_KV], ((0, 0), (0, 128 - t), (0, 0)))
    v_new = jnp.pad(p3s[:, :, OFF["vc"]:OFF["vc"] + C_KV], ((0, 0), (0, 128 - t), (0, 0)))

    def heads_major(a):
        a = a.reshape(b, t, C_KV_HEADS, C_GROUPS, C_HEAD)
        return jnp.transpose(a, (0, 2, 1, 3, 4)).reshape(b, C_KV_HEADS, rows, C_HEAD)

    qh = heads_major(p3s[:, :, OFF["qc"]:OFF["qc"] + W_C])
    zh = heads_major(p3s[:, :, OFF["zc"]:OFF["zc"] + W_C])

    kidx_t = jnp.swapaxes(cache_kidx, 2, 3)
    score_spec = pltpu.PrefetchScalarGridSpec(
        num_scalar_prefetch=1,
        grid=(b, ng),
        in_specs=[pl.BlockSpec((None, t * IDX_HEADS, IDX_DIM), lambda i, g, pt: (i, 0, 0)),
                  pl.BlockSpec((None, t * IDX_HEADS, 1), lambda i, g, pt: (i, 0, 0)),
                  pl.BlockSpec((None, 128, IDX_DIM), lambda i, g, pt: (i, 0, 0))]
                 + [pl.BlockSpec((None, None, IDX_DIM, page),
                                 lambda i, g, pt, j=j: (layer, pt[i, g * pg + j], 0, 0)) for j in range(pg)],
        out_specs=[pl.BlockSpec((None, t, pg * page), lambda i, g, pt: (i, 0, g)),
                   pl.BlockSpec((None, t, 128), lambda i, g, pt: (i, 0, 0))],
    )
    scores, scores_new = pl.pallas_call(
        functools.partial(_dsa_s_score_kernel, pg=pg, t=t),
        grid_spec=score_spec,
        out_shape=[jax.ShapeDtypeStruct((b, t, past), F32), jax.ShapeDtypeStruct((b, t, 128), F32)],
        compiler_params=_cp("dsa_s_score", 2),
        name="dsa_s_score",
    )(page_table, qi, wi, ki_new, *([kidx_t] * pg))

    tq = jnp.broadcast_to(jnp.tile(jnp.arange(t, dtype=I32), b)[:, None], (b * t, 128))
    bias = pl.pallas_call(
        functools.partial(_dsa_s_select_kernel, nk=past, topk=topk),
        grid=(1,),
        in_specs=[pl.BlockSpec((b * t, past), lambda i: (0, 0)),
                  pl.BlockSpec((b * t, 128), lambda i: (0, 0)),
                  pl.BlockSpec((b * t, 128), lambda i: (0, 0))],
        out_specs=pl.BlockSpec((b * t, past + 128), lambda i: (0, 0)),
        out_shape=jax.ShapeDtypeStruct((b * t, past + 128), F32),
        compiler_params=_cp("dsa_s_select", 1),
        name="dsa_s_select",
    )(scores.reshape(b * t, past), scores_new.reshape(b * t, 128), tq).reshape(b, t, past + 128)

    bias_rows = jnp.repeat(bias, C_GROUPS, axis=1)
    bias_past = bias_rows[:, :, :past]
    bias_new = bias_rows[:, :, past:]

    ck = cache_k.reshape(cache_k.shape[0], cache_k.shape[1], page * C_KV_HEADS, C_HEAD)
    cv = cache_v.reshape(cache_v.shape[0], cache_v.shape[1], page * C_KV_HEADS, C_HEAD)
    pspec = [pl.BlockSpec((None, None, page * C_KV_HEADS, C_HEAD),
                          lambda i, g, pt, j=j: (layer, pt[i, g * pg + j], 0, 0)) for j in range(pg)]
    attend_spec = pltpu.PrefetchScalarGridSpec(
        num_scalar_prefetch=1,
        grid=(b, ng),
        in_specs=[pl.BlockSpec((None, C_KV_HEADS, rows, C_HEAD), lambda i, g, pt: (i, 0, 0, 0)),
                  pl.BlockSpec((None, rows, pg * page), lambda i, g, pt: (i, 0, g)),
                  pl.BlockSpec((None, rows, 128), lambda i, g, pt: (i, 0, 0)),
                  pl.BlockSpec((None, 128, C_KV), lambda i, g, pt: (i, 0, 0)),
                  pl.BlockSpec((None, 128, C_KV), lambda i, g, pt: (i, 0, 0)),
                  pl.BlockSpec((None, C_KV_HEADS, rows, C_HEAD), lambda i, g, pt: (i, 0, 0, 0))]
                 + pspec + pspec,
        out_specs=pl.BlockSpec((None, C_KV_HEADS, rows, C_HEAD), lambda i, g, pt: (i, 0, 0, 0)),
        scratch_shapes=[pltpu.VMEM((C_KV_HEADS, rows, 1), F32), pltpu.VMEM((C_KV_HEADS, rows, 1), F32),
                        pltpu.VMEM((C_KV_HEADS, rows, C_HEAD), F32)],
    )
    oh = pl.pallas_call(
        functools.partial(_dsa_s_attend_kernel, pg=pg),
        grid_spec=attend_spec,
        out_shape=jax.ShapeDtypeStruct((b, C_KV_HEADS, rows, C_HEAD), F32),
        compiler_params=_cp("dsa_s_attend", 2),
        name="dsa_s_attend",
    )(page_table, qh, bias_past, bias_new, k_new, v_new, zh, *([ck] * pg), *([cv] * pg))
    y = jnp.transpose(oh.reshape(b, C_KV_HEADS, t, C_GROUPS, C_HEAD), (0, 2, 1, 3, 4)).reshape(b, t, W_C)
    return y.astype(BF16)


def _outproj_kernel(ya_ref, yb_ref, yc_ref, w_ref, x_ref, gate_ref, o_ref, mix_scr):
    @pl.when(pl.program_id(1) == 0)
    def _():
        mix_scr[:, 0:W_A] = ya_ref[...]
        mix_scr[:, W_A:W_A + W_B] = yb_ref[...]
        mix_scr[:, W_A + W_B:] = yc_ref[...]

    o_ref[...] = x_ref[...] + gate_ref[...] * _dot(mix_scr[...], w_ref[...])


def _outproj(ya, yb, yc, w_bf16, layer, x2, gate, *, rows_per_batch, mod_row0):
    m, d = x2.shape
    tn = 1024
    if gate.ndim == 3:
        tm = 512
        gate_spec = pl.BlockSpec((None, 1, tn),
                                 lambda i, j: ((i * tm) // rows_per_batch + mod_row0, 0, 2 * (d // tn) + j))
    else:
        tm = m
        gate_spec = pl.BlockSpec((tm, tn), lambda i, j: (i, j))
    assert m % tm == 0
    return pl.pallas_call(
        _outproj_kernel,
        grid=(m // tm, d // tn),
        in_specs=[pl.BlockSpec((tm, W_A), lambda i, j: (i, 0)),
                  pl.BlockSpec((tm, W_B), lambda i, j: (i, 0)),
                  pl.BlockSpec((tm, W_C), lambda i, j: (i, 0)),
                  pl.BlockSpec((None, d, tn), lambda i, j: (layer, 0, j)),
                  pl.BlockSpec((tm, tn), lambda i, j: (i, j)),
                  gate_spec],
        out_specs=pl.BlockSpec((tm, tn), lambda i, j: (i, j)),
        out_shape=jax.ShapeDtypeStruct((m, d), F32),
        scratch_shapes=[pltpu.VMEM((tm, d), BF16)],
        compiler_params=_cp("outproj", 2),
        name="outproj",
    )(ya, yb, yc, w_bf16, x2, gate)


def _final_norm_kernel(x_ref, g_ref, o_ref):
    x = x_ref[...]
    o_ref[...] = x * lax.rsqrt(jnp.mean(x * x, axis=-1, keepdims=True) + EPS) * g_ref[...]


def _final_norm(x2, g):
    m, d = x2.shape
    tm = 256 if m % 256 == 0 else m
    return pl.pallas_call(
        _final_norm_kernel,
        grid=(m // tm,),
        in_specs=[pl.BlockSpec((tm, d), lambda i: (i, 0)), pl.BlockSpec((1, d), lambda i: (0, 0))],
        out_specs=pl.BlockSpec((tm, d), lambda i: (i, 0)),
        out_shape=jax.ShapeDtypeStruct((m, d), F32),
        compiler_params=_cp("final_norm", 1),
        name="final_norm",
    )(x2, g.reshape(1, d))


def _w_in_tile_table():
    n_t = NP // 128
    table, n_plain = [], None
    for dt in range(n_t - 1):
        c = dt * 128
        seg = [n for n in _DST_ORDER if OFF[n] <= c < OFF[n] + _SRC_OFF[n][1]][0]
        src = _SRC_OFF[seg][0] + (c - OFF[seg])
        if src % 128 == 0:
            assert n_plain is None
        else:
            assert src % 128 == 32
            if n_plain is None:
                n_plain = dt
        table.append(src // 128)
    t_kw, t_ab = _SRC_OFF["ki"][0] // 128, _SRC_OFF["a"][0] // 128
    assert _SRC_OFF["ki"][0] % 128 == 32 and _SRC_OFF["wi"][0] == t_kw * 128 + TAIL_WI
    assert _SRC_OFF["a"][0] % 128 == 0 and _SRC_OFF["b"][0] == _SRC_OFF["a"][0] + DN_HEADS
    assert TAIL_A == IDX_DIM and TAIL_B == TAIL_A + DN_HEADS and TAIL_WI == TAIL_B + DN_HEADS
    tile_a = table + [t_kw]
    rows_b = [t_ab * 4] * n_plain + [(s + 1) * 4 for s in table[n_plain:]] + [t_ab * 4]
    return tile_a, rows_b, n_plain


def _wprep_kernel(ta_ref, tb_ref, a_ref, b_ref, o_ref, *, n_plain, n_t):
    dt = pl.program_id(1)
    dt_o = o_ref.dtype

    @pl.when(dt < n_plain)
    def _():
        o_ref[...] = a_ref[...].astype(dt_o)

    @pl.when((dt >= n_plain) & (dt < n_t - 1))
    def _():
        o_ref[0:96, :] = a_ref[32:128, :].astype(dt_o)
        o_ref[96:128, :] = b_ref[...].astype(dt_o)

    @pl.when(dt == n_t - 1)
    def _():
        o_ref[0:TAIL_A, :] = a_ref[32:32 + IDX_DIM, :].astype(dt_o)
        o_ref[TAIL_A:TAIL_WI, :] = b_ref[...].astype(dt_o)
        o_ref[TAIL_WI:TAIL_WI + IDX_HEADS, :] = a_ref[TAIL_WI:TAIL_WI + IDX_HEADS, :].astype(dt_o)
        o_ref[TAIL_WI + IDX_HEADS:, :] = jnp.zeros((128 - TAIL_WI - IDX_HEADS, o_ref.shape[1]), dt_o)


def _permute_w_in(w_in):
    depth, d, _ = w_in.shape
    w_t = jnp.swapaxes(w_in, 1, 2)
    tile_a, rows_b, n_plain = _w_in_tile_table()
    n_t = NP // 128
    grid_spec = pltpu.PrefetchScalarGridSpec(
        num_scalar_prefetch=2,
        grid=(depth, n_t),
        in_specs=[pl.BlockSpec((None, 128, d), lambda l, t, ta, tb: (l, ta[t], 0)),
                  pl.BlockSpec((None, 32, d), lambda l, t, ta, tb: (l, tb[t], 0))],
        out_specs=pl.BlockSpec((None, 128, d), lambda l, t, ta, tb: (l, t, 0)),
    )
    return pl.pallas_call(
        functools.partial(_wprep_kernel, n_plain=n_plain, n_t=n_t),
        grid_spec=grid_spec,
        out_shape=jax.ShapeDtypeStruct((depth, NP, d), BF16),
        compiler_params=_cp("w_in_layout", 2),
        name="w_in_layout",
    )(jnp.asarray(tile_a, I32), jnp.asarray(rows_b, I32), w_t, w_t)


def _delta_t(l):
    for t in (128, 64):
        if l % t == 0:
            return t
    raise ValueError("sequence length must be a multiple of 64")


def kernel(x_prompt, x_sample, cache_k, cache_v, cache_kidx, state_dn, state_conv, page_table,
           c_prompt, c_sample, w_ada, b_ada, g_norm, w_in, a_vnorm, a_ws, a_bs, dn_conv_w,
           dn_a_log, dn_dt_bias, dn_onorm, w_out, g_final):
    bp, lp, d = x_prompt.shape
    bs, ls, _ = x_sample.shape
    depth = w_ada.shape[0]
    assert d == D_MODEL and w_in.shape[2] == D_IN
    assert CONV_K - 1 <= ls <= DN_CHUNK

    n_c = bp + bs
    c_rows = jnp.concatenate([c_prompt, c_sample], axis=0)
    r_pad = (-n_c) % 8
    if r_pad:
        c_rows = jnp.pad(c_rows, ((0, r_pad), (0, 0)))
    m_all = _ada(c_rows, w_ada, b_ada)

    xp = x_prompt.reshape(bp * lp, d)
    xs = x_sample.reshape(bs * ls, d)
    ls_pad = DN_CHUNK
    outs = {k: [] for k in ("pk", "pv", "pki", "pdn", "pconv", "sk", "sv", "ski", "sdn", "sconv", "samlp")}
    zeros_conv = jnp.zeros((bp, CONV_K - 1, DN_CONV_DIM), F32)
    zeros_state = jnp.zeros((bp, DN_HEADS, DN_HEAD, DN_HEAD), F32)

    w_in_all = _permute_w_in(w_in)
    w_out_all = w_out.astype(BF16)
    for l in range(depth):
        g_l = g_norm[l].reshape(1, d)
        m_l = m_all[l]
        m3 = m_l.reshape(m_l.shape[0], 1, 3 * d)
        ms = jnp.repeat(m_l[bp:bp + bs], ls, axis=0)

        pp = _inproj(xp, g_l, m3, m3, w_in_all, l, rows_per_batch=lp, mod_row0=0)
        p3 = pp.reshape(bp, lp, NP)
        (ya,) = _mixa(p3, a_vnorm[l], a_ws[l], a_bs[l], emit_va=False)
        abt = jnp.transpose(p3[:, :, OFF["a"]:OFF["a"] + 2 * DN_HEADS], (0, 2, 1))
        yb, s_p = _delta(p3, abt, dn_conv_w[l], zeros_conv, zeros_state, dn_a_log[l], dn_dt_bias[l],
                         dn_onorm[l], t=_delta_t(lp), valid_len=_delta_t(lp))
        yc = _dsa_prompt(p3)
        xp = _outproj(ya.reshape(bp * lp, W_A), yb.reshape(bp * lp, W_B), yc.reshape(bp * lp, W_C),
                      w_out_all, l, xp, m3, rows_per_batch=lp, mod_row0=0)
        outs["pk"].append(p3[:, :, OFF["kc"]:OFF["kc"] + C_KV].reshape(bp, lp, C_KV_HEADS, C_HEAD))
        outs["pv"].append(p3[:, :, OFF["vc"]:OFF["vc"] + C_KV].reshape(bp, lp, C_KV_HEADS, C_HEAD))
        outs["pki"].append(p3[:, :, OFF_TAIL:OFF_TAIL + IDX_DIM])
        outs["pdn"].append(s_p)
        outs["pconv"].append(p3[:, lp - (CONV_K - 1):, OFF["qkv"]:OFF["qkv"] + DN_CONV_DIM])

        ps = _inproj(xs, g_l, ms[:, d:2 * d], ms[:, 0:d], w_in_all, l, rows_per_batch=ls, mod_row0=bp)
        p3s = ps.reshape(bs, ls, NP)
        ya_s, va_s = _mixa(p3s, a_vnorm[l], a_ws[l], a_bs[l], emit_va=True)
        p3s_pad = jnp.pad(p3s, ((0, 0), (0, ls_pad - ls), (0, 0)))
        abt_s = jnp.transpose(p3s_pad[:, :, OFF["a"]:OFF["a"] + 2 * DN_HEADS], (0, 2, 1))
        yb_s, s_s = _delta(p3s_pad, abt_s, dn_conv_w[l], state_conv[l], state_dn[l], dn_a_log[l],
                           dn_dt_bias[l], dn_onorm[l], t=ls_pad, valid_len=ls)
        yc_s = _dsa_sample(p3s, cache_k, cache_v, cache_kidx, page_table, l)
        xs = _outproj(ya_s.reshape(bs * ls, W_A), yb_s[:, :ls].reshape(bs * ls, W_B),
                      yc_s.reshape(bs * ls, W_C), w_out_all, l, xs, ms[:, 2 * d:3 * d],
                      rows_per_batch=ls, mod_row0=bp)
        outs["sk"].append(p3s[:, :, OFF["kc"]:OFF["kc"] + C_KV].reshape(bs, ls, C_KV_HEADS, C_HEAD))
        outs["sv"].append(p3s[:, :, OFF["vc"]:OFF["vc"] + C_KV].reshape(bs, ls, C_KV_HEADS, C_HEAD))
        outs["ski"].append(p3s[:, :, OFF_TAIL:OFF_TAIL + IDX_DIM])
        outs["sdn"].append(s_s)
        outs["sconv"].append(p3s[:, ls - (CONV_K - 1):, OFF["qkv"]:OFF["qkv"] + DN_CONV_DIM])
        outs["samlp"].append(va_s)

    y_prompt = _final_norm(xp, g_final).reshape(bp, lp, d)
    y_sample = _final_norm(xs, g_final).reshape(bs, ls, d)
    st = jnp.stack
    return (y_prompt, y_sample, st(outs["pk"]), st(outs["pv"]), st(outs["pki"]), st(outs["pdn"]),
            st(outs["pconv"]), st(outs["sk"]), st(outs["sv"]), st(outs["ski"]), st(outs["sdn"]),
            st(outs["sconv"]), st(outs["samlp"]))
```

```python
import functools

import jax
import jax.numpy as jnp
from jax import lax
from jax.experimental import pallas as pl
from jax.experimental.pallas import tpu as pltpu

F32 = jnp.float32
BF16 = jnp.bfloat16
I32 = jnp.int32
EPS = 1e-6
INT_MIN = -(2 ** 31)
NEG_INF = float("-inf")

D_MODEL = 4096
W_A = D_MODEL // 4
A_GROUP = 128
A_HEADS = W_A // A_GROUP
A_CHUNK = 128
W_B = D_MODEL // 2
DN_HEAD = 128
DN_HEADS = W_B // DN_HEAD
CONV_K = 4
DN_CONV_DIM = 3 * W_B
DN_CHUNK = 64
W_C = D_MODEL - W_A - W_B
C_HEAD = 128
C_HEADS = W_C // C_HEAD
C_KV_HEADS = 2
C_GROUPS = C_HEADS // C_KV_HEADS
C_KV = C_KV_HEADS * C_HEAD
IDX_HEADS = 16
IDX_DIM = 64
TOPK_MAX = 256
Q_BLOCK = 256

_SRC_SPLITS = (W_A, W_A, W_A, DN_CONV_DIM, W_B, DN_HEADS, DN_HEADS,
               W_C, C_KV, C_KV, W_C, IDX_HEADS * IDX_DIM, IDX_DIM, IDX_HEADS)
_SRC_NAMES = ("u", "v", "za", "qkv", "zb", "a", "b", "qc", "kc", "vc", "zc", "qi", "ki", "wi")
_SRC_OFF = {}
_o = 0
for _n, _w in zip(_SRC_NAMES, _SRC_SPLITS):
    _SRC_OFF[_n] = (_o, _w)
    _o += _w
D_IN = _o

_DST_ORDER = ("u", "v", "za", "qkv", "zb", "qc", "zc", "qi", "kc", "vc", "ki", "a", "b", "wi")
OFF = {}
_o = 0
for _n in _DST_ORDER:
    OFF[_n] = _o
    _o += _SRC_OFF[_n][1]
NP = ((_o + 127) // 128) * 128
OFF_TAIL = OFF["ki"]
TAIL_A = OFF["a"] - OFF_TAIL
TAIL_B = OFF["b"] - OFF_TAIL
TAIL_WI = OFF["wi"] - OFF_TAIL

MXU_COLUMNS = 256
INPROJ_TN = 6 * MXU_COLUMNS
DELTA_HEADS_PER_STEP = 8
DSA_KEY_TILE = 512

VMEM_CAPACITY_MIB = 64
VMEM_LIMIT_MIB = {
    "ada": 40, "w_in_layout": 32, "inproj": 56, "mixa": 32, "delta": 40, "dsa_prompt": 48,
    "dsa_s_score": 32, "dsa_s_select": 40, "dsa_s_attend": 40, "outproj": 48, "final_norm": 32,
}
assert max(VMEM_LIMIT_MIB.values()) < VMEM_CAPACITY_MIB


def _cp(name, n_axes):
    return pltpu.CompilerParams(dimension_semantics=("arbitrary",) * n_axes,
                                vmem_limit_bytes=VMEM_LIMIT_MIB[name] * 1024 * 1024)


def _silu(x):
    return (0.5 * x) * (1.0 + jnp.tanh(0.5 * x))


def _gelu(x):
    return 0.5 * x * (1.0 + jnp.tanh(0.7978845608028654 * (x + 0.044715 * (x * x * x))))


def _dot(a, b):
    return jnp.dot(a, b, preferred_element_type=F32)


def _dot_nt(a, b):
    return lax.dot_general(a, b, (((1,), (1,)), ((), ())), preferred_element_type=F32)


def _dot_tn(a, b):
    return lax.dot_general(a, b, (((0,), (0,)), ((), ())), preferred_element_type=F32)


def _b16(a):
    return a.astype(BF16)


def _ada_kernel(c_ref, w_ref, b_ref, o_ref):
    s = _silu(c_ref[...]).astype(BF16)
    o_ref[...] = _dot(s, w_ref[...].astype(BF16)) + b_ref[...]


def _ada(c_rows, w_ada, b_ada):
    depth, d, n = w_ada.shape
    r = c_rows.shape[0]
    tn = 512
    return pl.pallas_call(
        _ada_kernel,
        grid=(depth, n // tn),
        in_specs=[pl.BlockSpec((r, d), lambda l, j: (0, 0)),
                  pl.BlockSpec((None, d, tn), lambda l, j: (l, 0, j)),
                  pl.BlockSpec((None, 1, tn), lambda l, j: (l, 0, j))],
        out_specs=pl.BlockSpec((None, r, tn), lambda l, j: (l, 0, j)),
        out_shape=jax.ShapeDtypeStruct((depth, r, n), F32),
        compiler_params=_cp("ada", 2),
        name="ada",
    )(c_rows, w_ada, b_ada.reshape(depth, 1, n))


def _inproj_kernel(x_ref, g_ref, sc_ref, sh_ref, w_ref, o_ref, h_scr, *, rc):
    @pl.when(pl.program_id(1) == 0)
    def _():
        tm = x_ref.shape[0]
        per_row = sc_ref.shape[0] != 1

        def body(c, carry):
            r = pl.ds(pl.multiple_of(c * rc, rc), rc)
            x = x_ref[r, :]
            y = x * lax.rsqrt(jnp.mean(x * x, axis=-1, keepdims=True) + EPS)
            sc = sc_ref[r, :] if per_row else sc_ref[...]
            sh = sh_ref[r, :] if per_row else sh_ref[...]
            h_scr[r, :] = ((y * g_ref[...]) * (1.0 + sc) + sh).astype(BF16)
            return carry

        lax.fori_loop(0, tm // rc, body, 0)

    o_ref[...] = _dot_nt(h_scr[...], w_ref[...])


def _inproj(x2, g, sc, sh, w_bf16, layer, *, rows_per_batch, mod_row0):
    m, d = x2.shape
    n = w_bf16.shape[1]
    tn = INPROJ_TN
    if sc.ndim == 3:
        tm = 512
        assert rows_per_batch % tm == 0
        sc_spec = pl.BlockSpec((None, 1, d), lambda i, j: ((i * tm) // rows_per_batch + mod_row0, 0, 1))
        sh_spec = pl.BlockSpec((None, 1, d), lambda i, j: ((i * tm) // rows_per_batch + mod_row0, 0, 0))
        rc = 64
    else:
        tm = m
        sc_spec = pl.BlockSpec((tm, d), lambda i, j: (i, 0))
        sh_spec = pl.BlockSpec((tm, d), lambda i, j: (i, 0))
        rc = tm
    assert m % tm == 0
    return pl.pallas_call(
        functools.partial(_inproj_kernel, rc=rc),
        grid=(m // tm, pl.cdiv(n, tn)),
        in_specs=[pl.BlockSpec((tm, d), lambda i, j: (i, 0)),
                  pl.BlockSpec((1, d), lambda i, j: (0, 0)),
                  sc_spec, sh_spec,
                  pl.BlockSpec((None, tn, d), lambda i, j: (layer, j, 0))],
        out_specs=pl.BlockSpec((tm, tn), lambda i, j: (i, j)),
        out_shape=jax.ShapeDtypeStruct((m, n), F32),
        scratch_shapes=[pltpu.VMEM((tm, d), BF16)],
        compiler_params=_cp("inproj", 2),
        name="inproj",
    )(x2, g, sc, sh, w_bf16)


def _mixa_kernel(u_ref, v_ref, z_ref, vn_ref, ws_ref, bst_ref, y_ref, *rest, c, emit_va):
    u = _gelu(u_ref[...])
    v = _gelu(v_ref[...])
    mu = jnp.mean(v, axis=-1, keepdims=True)
    dv = v - mu
    va = dv * lax.rsqrt(jnp.mean(dv * dv, axis=-1, keepdims=True) + EPS) * vn_ref[...]
    if emit_va:
        rest[0][...] = va
    z = _silu(z_ref[...])
    row = lax.broadcasted_iota(I32, (c, c), 0)
    col = lax.broadcasted_iota(I32, (c, c), 1)
    tril = col <= row
    for h in range(A_HEADS):
        cols = slice(h * A_GROUP, (h + 1) * A_GROUP)
        wm = jnp.where(tril, ws_ref[h], 0.0)
        wm16 = wm.astype(BF16)
        for ci in range(u.shape[0] // c):
            rows = slice(ci * c, (ci + 1) * c)
            vh = va[rows, cols]
            if c >= 128:
                mixed = _dot(wm16, vh.astype(BF16))
            else:
                mixed = wm[:, 0:1] * vh[0:1, :]
                for s in range(1, c):
                    mixed = mixed + wm[:, s:s + 1] * vh[s:s + 1, :]
            mixed = mixed + bst_ref[:, h:h + 1]
            y_ref[rows, cols] = (u[rows, cols] * mixed * z[rows, cols]).astype(y_ref.dtype)


def _mixa(p3, a_vnorm, a_ws, a_bs, *, emit_va):
    b, l, _ = p3.shape
    c = min(A_CHUNK, l)
    r = 2 * c if l % (2 * c) == 0 else c
    n = l // r
    ws = a_ws[:, :c, :c]
    bst = a_bs[:, :c].T
    wblk = W_A
    outs = [jax.ShapeDtypeStruct((b, l, W_A), BF16)]
    out_specs = [pl.BlockSpec((None, r, W_A), lambda i, j: (i, j, 0))]
    if emit_va:
        outs.append(jax.ShapeDtypeStruct((b, l, W_A), F32))
        out_specs.append(pl.BlockSpec((None, r, W_A), lambda i, j: (i, j, 0)))
    res = pl.pallas_call(
        functools.partial(_mixa_kernel, c=c, emit_va=emit_va),
        grid=(b, n),
        in_specs=[pl.BlockSpec((None, r, wblk), lambda i, j: (i, j, OFF["u"] // wblk)),
                  pl.BlockSpec((None, r, wblk), lambda i, j: (i, j, OFF["v"] // wblk)),
                  pl.BlockSpec((None, r, wblk), lambda i, j: (i, j, OFF["za"] // wblk)),
                  pl.BlockSpec((1, W_A), lambda i, j: (0, 0)),
                  pl.BlockSpec((A_HEADS, c, c), lambda i, j: (0, 0, 0)),
                  pl.BlockSpec((c, A_HEADS), lambda i, j: (0, 0))],
        out_specs=out_specs,
        out_shape=outs,
        compiler_params=_cp("mixa", 2),
        name="mixa",
    )(p3, p3, p3, a_vnorm.reshape(1, W_A), ws, bst)
    return res


def _delta_kernel(alog_ref, dtb_ref,
                  q_ref, k_ref, v_ref, z_ref, ab_ref,
                  cwq_ref, cwk_ref, cwv_ref, cpq_ref, cpk_ref, cpv_ref,
                  s0_ref, on_ref,
                  y_ref, sout_ref,
                  xbuf, s_scr, *, t, c, hp, valid_len):
    n = pl.program_id(2)
    dh = DN_HEAD

    @pl.when(n == 0)
    def _():
        s_scr[...] = s0_ref[...]
        for j in range(hp):
            lanes = slice(j * dh, (j + 1) * dh)
            xbuf[3 * j + 0, 5:8, :] = cpq_ref[:, lanes]
            xbuf[3 * j + 1, 5:8, :] = cpk_ref[:, lanes]
            xbuf[3 * j + 2, 5:8, :] = cpv_ref[:, lanes]

    _delta_heads(alog_ref, dtb_ref, q_ref, k_ref, v_ref, z_ref, ab_ref, cwq_ref, cwk_ref, cwv_ref,
                 on_ref, y_ref, xbuf, s_scr, t=t, c=c, hp=hp, valid_len=valid_len)

    @pl.when(n == pl.num_programs(2) - 1)
    def _():
        sout_ref[...] = s_scr[...]


def _delta_heads(alog_ref, dtb_ref, q_ref, k_ref, v_ref, z_ref, ab_ref, cwq_ref, cwk_ref, cwv_ref,
                 on_ref, y_ref, xbuf, s_scr, *, t, c, hp, valid_len):
    hg = pl.program_id(1)
    dh = DN_HEAD
    heads = list(range(hp))

    row = lax.broadcasted_iota(I32, (t, t), 0)
    col = lax.broadcasted_iota(I32, (t, t), 1)
    shift = c.bit_length() - 1
    same = (row >> shift) == (col >> shift)
    eye = row == col
    incl = same & (col <= row)
    strict = same & (col < row)
    incl_t = same & (row <= col)
    blk8 = (row >> 3) == (col >> 3)
    off_masks = []
    bs = 8
    while bs < c:
        sh_b = bs.bit_length() - 1
        inner = (row >> sh_b) == (col >> sh_b)
        outer = (row >> (sh_b + 1)) == (col >> (sh_b + 1))
        off_masks.append(outer & jnp.logical_not(inner))
        bs *= 2
    eye_f = jnp.where(eye, 1.0, 0.0)
    if valid_len < t:
        lane_valid = lax.broadcasted_iota(I32, (1, t), 1) < valid_len
        sub_valid = lax.broadcasted_iota(I32, (t, 1), 0) < valid_len

    def to_col(r):
        return jnp.sum(jnp.where(eye, r, 0.0), axis=1, keepdims=True)

    def conv(idx, x, w_ref, lanes):
        xbuf[idx, 8:8 + t, :] = x
        y = xbuf[idx, pl.ds(5, t), :] * w_ref[0:1, lanes]
        for jj in range(1, CONV_K):
            y = y + xbuf[idx, pl.ds(5 + jj, t), :] * w_ref[jj:jj + 1, lanes]
        xbuf[idx, 5:8, :] = x[t - 3:t, :]
        return _silu(y)

    lanes_of = [slice(j * dh, (j + 1) * dh) for j in range(hp)]

    def per_head(f, *lists):
        return [f(*vals) for vals in zip(*lists)]

    def l2n(a):
        return a * lax.rsqrt(jnp.sum(a * a, axis=-1, keepdims=True) + EPS)

    qc = [l2n(conv(3 * j + 0, q_ref[:, lanes_of[j]], cwq_ref, lanes_of[j])) * (dh ** -0.5) for j in heads]
    kc = [l2n(conv(3 * j + 1, k_ref[:, lanes_of[j]], cwk_ref, lanes_of[j])) for j in heads]
    v = [conv(3 * j + 2, v_ref[:, lanes_of[j]], cwv_ref, lanes_of[j]) for j in heads]

    def gates(j):
        h = hg * hp + j
        a_row = ab_ref[pl.ds(h, 1), :]
        b_row = ab_ref[pl.ds(DN_HEADS + h, 1), :]
        xa = a_row + dtb_ref[h]
        softplus = jnp.maximum(xa, 0.0) + jnp.log(1.0 + jnp.exp(-jnp.abs(xa)))
        a_coef = jnp.exp(jnp.zeros((1, 1), F32) + alog_ref[h])
        g_row = -a_coef * softplus
        beta_row = 1.0 / (1.0 + jnp.exp(-b_row))
        if valid_len < t:
            g_row = jnp.where(lane_valid, g_row, 0.0)
            beta_row = jnp.where(lane_valid, beta_row, 0.0)
        return g_row, beta_row

    g_row, beta_row = zip(*[gates(j) for j in heads])
    if valid_len < t:
        kc = per_head(lambda a: jnp.where(sub_valid, a, 0.0), kc)
        v = per_head(lambda a: jnp.where(sub_valid, a, 0.0), v)

    g_col = per_head(to_col, g_row)
    beta_col = per_head(to_col, beta_row)
    gc_col = per_head(lambda r: jnp.sum(jnp.where(incl, r, 0.0), axis=1, keepdims=True), g_row)
    glast_col = per_head(lambda r: jnp.sum(jnp.where(same, r, 0.0), axis=1, keepdims=True), g_row)
    gc_row = per_head(lambda cl: jnp.sum(jnp.where(incl_t, cl, 0.0), axis=0, keepdims=True), g_col)
    kb = per_head(lambda a, b: a * b, kc, beta_col)
    kc16 = per_head(lambda a: a.astype(BF16), kc)

    def key_products(kb_, kc16_, qc_, gc, gr):
        decay = jnp.where(incl, jnp.exp(jnp.where(incl, gc - gr, 0.0)), 0.0)
        a_mat = jnp.where(strict, _dot_nt(kb_.astype(BF16), kc16_) * decay, 0.0)
        attn16 = (_dot_nt(qc_.astype(BF16), kc16_) * decay).astype(BF16)
        n0f = jnp.where(blk8, -a_mat, 0.0)
        return attn16, _b16(n0f), eye_f + n0f, [_b16(jnp.where(om, a_mat, 0.0)) for om in off_masks]

    attn, n0, x, a_offs = zip(*per_head(key_products, kb, kc16, qc, gc_col, gc_row))
    eg = per_head(jnp.exp, gc_col)
    rhs = per_head(lambda vv, bc, kk, e: _b16(jnp.concatenate([vv * bc, kk * e], axis=1)),
                   v, beta_col, kb, eg)
    n2 = per_head(lambda a: _b16(_dot(a, a)), n0)
    n4 = per_head(lambda a: _b16(_dot(a, a)), n2)
    x = per_head(lambda xx, nn: xx + _dot(_b16(xx), nn), x, n2)
    x = per_head(lambda xx, nn: xx + _dot(_b16(xx), nn), x, n4)
    for lvl in range(len(off_masks)):
        xs = per_head(_b16, x)
        xa_off = per_head(lambda s_, offs: _b16(_dot(s_, offs[lvl])), xs, a_offs)
        x = per_head(lambda xx, xo, s_: xx - _dot(xo, s_), x, xa_off, xs)
    sol16 = per_head(lambda xx, r: _b16(_dot(_b16(xx), r)), x, rhs)
    auw = per_head(_dot, attn, sol16)
    qw = per_head(lambda a, e, m_: (a * e - m_[:, dh:]).astype(BF16), qc, eg, auw)
    kg16 = per_head(lambda a, gl_, gc: (a * jnp.exp(gl_ - gc)).astype(BF16), kc, glast_col, gc_col)

    s = [s_scr[j] for j in heads]
    outs = [[] for _ in heads]
    for i in range(t // c):
        rows = slice(i * c, (i + 1) * c)
        s16 = per_head(lambda a: a.astype(BF16), s)
        kuw = per_head(lambda a, b: _dot_tn(a[rows], b[rows]), kg16, sol16)
        o_i = per_head(lambda a, b16, m_: _dot(a[rows], b16) + m_[rows, :dh], qw, s16, auw)
        for pos in range(len(heads)):
            outs[pos].append(o_i[pos])
        gl = per_head(lambda a: jnp.exp(a[i * c:i * c + 1, :]), glast_col)
        s = per_head(lambda g_, s_, m_, b16: g_ * s_ + m_[:, :dh] - _dot(m_[:, dh:].astype(BF16), b16),
                     gl, s, kuw, s16)
    for pos, j in enumerate(heads):
        s_scr[j] = s[pos]
        o = outs[pos][0] if len(outs[pos]) == 1 else jnp.concatenate(outs[pos], axis=0)
        on = o * lax.rsqrt(jnp.mean(o * o, axis=-1, keepdims=True) + EPS) * on_ref[...]
        y_ref[:, lanes_of[j]] = (on * _silu(z_ref[:, lanes_of[j]])).astype(y_ref.dtype)


def _delta(p3, abt, conv_w, conv_prev, s0, a_log, dt_bias, onorm, *, t, valid_len):
    b, l, _ = p3.shape
    assert l % t == 0
    c = min(DN_CHUNK, t)
    hp = DELTA_HEADS_PER_STEP
    nh = DN_HEADS
    ng = nh // hp
    w = 128 * hp
    cb_q = OFF["qkv"] // w
    cb_z = OFF["zb"] // w
    assert OFF["qkv"] % w == 0 and OFF["zb"] % w == 0 and W_B % w == 0

    def pspec(cb0):
        return pl.BlockSpec((None, t, w), lambda i, h, n, a, d: (i, n, cb0 + h))

    def cwspec(sidx):
        return pl.BlockSpec((CONV_K, w), lambda i, h, n, a, d: (0, sidx * ng + h))

    def cpspec(sidx):
        return pl.BlockSpec((None, CONV_K - 1, w), lambda i, h, n, a, d: (i, 0, sidx * ng + h))

    grid_spec = pltpu.PrefetchScalarGridSpec(
        num_scalar_prefetch=2,
        grid=(b, ng, l // t),
        in_specs=[pspec(cb_q), pspec(cb_q + ng), pspec(cb_q + 2 * ng), pspec(cb_z),
                  pl.BlockSpec((None, 2 * nh, t), lambda i, h, n, a, d: (i, 0, n)),
                  cwspec(0), cwspec(1), cwspec(2), cpspec(0), cpspec(1), cpspec(2),
                  pl.BlockSpec((None, hp, DN_HEAD, DN_HEAD), lambda i, h, n, a, d: (i, h, 0, 0)),
                  pl.BlockSpec((1, DN_HEAD), lambda i, h, n, a, d: (0, 0))],
        out_specs=[pl.BlockSpec((None, t, w), lambda i, h, n, a, d: (i, n, h)),
                   pl.BlockSpec((None, hp, DN_HEAD, DN_HEAD), lambda i, h, n, a, d: (i, h, 0, 0))],
        scratch_shapes=[pltpu.VMEM((3 * hp, t + 8, 128), F32), pltpu.VMEM((hp, DN_HEAD, DN_HEAD), F32)],
    )
    return pl.pallas_call(
        functools.partial(_delta_kernel, t=t, c=c, hp=hp, valid_len=valid_len),
        grid_spec=grid_spec,
        out_shape=[jax.ShapeDtypeStruct((b, l, W_B), BF16),
                   jax.ShapeDtypeStruct((b, nh, DN_HEAD, DN_HEAD), F32)],
        compiler_params=_cp("delta", 3),
        name="delta",
    )(a_log, dt_bias, p3, p3, p3, p3, abt, conv_w, conv_w, conv_w,
      conv_prev, conv_prev, conv_prev, s0, onorm.reshape(1, DN_HEAD))


def _ordered_bits_to_float(u):
    key = u ^ jnp.int32(INT_MIN)
    bits = jnp.where(key < 0, key ^ jnp.int32(0x7FFFFFFF), key)
    return pltpu.bitcast(bits, F32)


def _topk_threshold(count_ge, shape, topk):
    def bit_body(i, u):
        bit = jnp.left_shift(jnp.int32(1), 31 - i)
        cand_u = u | bit
        cnt = count_ge(_ordered_bits_to_float(cand_u))
        return jnp.where(cnt >= float(topk), cand_u, u)

    u = lax.fori_loop(0, 32, bit_body, jnp.zeros(shape, I32))
    return _ordered_bits_to_float(u)


def _softmax_update(m_ref, l_ref, acc_ref, idx, s, v_tiles):
    m_old = m_ref[idx]
    m_new = jnp.maximum(m_old, jnp.max(s, axis=1, keepdims=True))
    m_safe = jnp.where(m_new == NEG_INF, 0.0, m_new)
    alpha = jnp.exp(m_old - m_safe)
    p = jnp.exp(s - m_safe)
    l_ref[idx] = alpha * l_ref[idx] + jnp.sum(p, axis=1, keepdims=True)
    acc = alpha * acc_ref[idx]
    p16 = p.astype(BF16)
    k0 = 0
    for vt in v_tiles:
        acc = acc + _dot(p16[:, k0:k0 + vt.shape[0]], vt)
        k0 += vt.shape[0]
    acc_ref[idx] = acc
    m_ref[idx] = m_new


def _dsa_prompt_kernel(qi_ref, tq_ref, tall_ref, qc_ref, k_ref, v_ref, z_ref, y_ref,
                       sc_scr, m_scr, l_scr, acc_scr, *, l, topk, kt):
    qb = pl.program_id(1)
    nq = Q_BLOCK
    n_tiles = (qb * nq + nq + kt - 1) // kt
    n_pairs = C_HEADS // 2
    w_rows = tq_ref[...].T[TAIL_WI:TAIL_WI + IDX_HEADS, :] * ((IDX_HEADS ** -0.5) * (IDX_DIM ** -0.5))
    rq = []
    for j in range(IDX_HEADS // 2):
        a = qi_ref[:, j * 128:(j + 1) * 128]
        rq.append(jnp.concatenate([a, pltpu.roll(a, IDX_DIM, 1)], axis=0).astype(BF16))
    kpos = lax.broadcasted_iota(I32, (kt, nq), 0)
    qpos = qb * nq + lax.broadcasted_iota(I32, (kt, nq), 1)
    lane = lax.broadcasted_iota(I32, (kt, 128), 1)

    def score_tile(ti, carry):
        k0 = pl.multiple_of(ti * kt, kt)
        ki = jnp.where(lane < IDX_DIM, tall_ref[pl.ds(k0, kt), :], 0.0).astype(BF16)
        score = jnp.zeros((kt, nq), F32)
        for j in range(IDX_HEADS // 2):
            lg = _dot_nt(ki, rq[j])
            score = score + w_rows[2 * j:2 * j + 1, :] * jnp.maximum(lg[:, :nq], 0.0)
            score = score + w_rows[2 * j + 1:2 * j + 2, :] * jnp.maximum(lg[:, nq:], 0.0)
        sc_scr[pl.ds(k0, kt), :] = jnp.where(kpos + k0 <= qpos, score, NEG_INF)
        return carry

    lax.fori_loop(0, n_tiles, score_tile, 0)

    ct = 256
    n_ct = (qb * nq + nq + ct - 1) // ct
    n_acc = 4

    def count(cand, strict):
        cand8 = jnp.broadcast_to(cand, (8, nq))

        def tile_body(ti, accs):
            k0 = pl.multiple_of(ti * ct, ct)
            tile = sc_scr[pl.ds(k0, ct), :]
            accs = list(accs)
            for r in range(ct // 8):
                blk = tile[8 * r:8 * r + 8, :]
                hit = (blk > cand8) if strict else (blk >= cand8)
                accs[r % n_acc] = accs[r % n_acc] + jnp.where(hit, 1.0, 0.0)
            return tuple(accs)

        accs = lax.fori_loop(0, n_ct, tile_body, tuple(jnp.zeros((8, nq), F32) for _ in range(n_acc)))
        acc = (accs[0] + accs[1]) + (accs[2] + accs[3])
        return jnp.sum(acc, axis=0, keepdims=True)

    thr = _topk_threshold(lambda cand: count(cand, False), (1, nq), topk)
    keep_all = (qb * nq + lax.broadcasted_iota(I32, (1, nq), 1)) < topk
    need = float(topk) - count(thr, True)
    tri16 = jnp.where(lax.broadcasted_iota(I32, (128, 128), 1) <= lax.broadcasted_iota(I32, (128, 128), 0),
                      1.0, 0.0).astype(BF16)

    def select_tile(ti, seen):
        k0 = pl.multiple_of(ti * kt, kt)
        sc = sc_scr[pl.ds(k0, kt), :]
        eq = sc == thr
        eq_f = jnp.where(eq, 1.0, 0.0)
        eq16 = eq_f.astype(BF16)
        prefs = [_dot(tri16, eq16[r:r + 128]) for r in range(0, kt, 128)]
        ranks = []
        for pref in prefs:
            ranks.append(seen + pref)
            seen = seen + pref[127:128, :]
        tie_ok = (jnp.concatenate(ranks, axis=0) - eq_f) < need
        chosen = jnp.where(sc > thr, 0.0, jnp.where(eq, jnp.where(tie_ok, 0.0, NEG_INF), NEG_INF))
        sc_scr[pl.ds(k0, kt), :] = jnp.where(keep_all, jnp.where(sc > NEG_INF, 0.0, NEG_INF), chosen)
        return seen

    lax.fori_loop(0, n_tiles, select_tile, jnp.zeros((1, nq), F32))

    m_scr[...] = jnp.full(m_scr.shape, NEG_INF, F32)
    l_scr[...] = jnp.zeros(l_scr.shape, F32)
    acc_scr[...] = jnp.zeros(acc_scr.shape, F32)
    scale = (C_HEAD ** -0.5) * 1.4426950408889634
    qp = []
    for pr in range(n_pairs):
        c0 = slice(2 * pr * C_HEAD, (2 * pr + 1) * C_HEAD)
        c1 = slice((2 * pr + 1) * C_HEAD, (2 * pr + 2) * C_HEAD)
        qp.append(jnp.concatenate([qc_ref[:, c0] * scale, qc_ref[:, c1] * scale], axis=0).astype(BF16))
    pairs_per_kv = n_pairs // C_KV_HEADS

    def attend_tile(ti, carry):
        k0 = pl.multiple_of(ti * kt, kt)
        bias = sc_scr[pl.ds(k0, kt), :]
        k16 = [k_ref[pl.ds(k0, kt), hk * C_HEAD:(hk + 1) * C_HEAD].astype(BF16) for hk in range(C_KV_HEADS)]
        vt16 = [v_ref[pl.ds(k0, kt), hk * C_HEAD:(hk + 1) * C_HEAD].T.astype(BF16) for hk in range(C_KV_HEADS)]
        bias2 = jnp.concatenate([bias, bias], axis=1)
        pr_all = range(n_pairs)
        s = [_dot_nt(k16[pr // pairs_per_kv], qp[pr]) + bias2 for pr in pr_all]
        m_old = [m_scr[pr] for pr in pr_all]
        m_new = [jnp.maximum(m_old[pr], jnp.max(s[pr], axis=0, keepdims=True)) for pr in pr_all]
        m_safe = [jnp.where(m_new[pr] == NEG_INF, 0.0, m_new[pr]) for pr in pr_all]
        alpha = [jnp.exp2(m_old[pr] - m_safe[pr]) for pr in pr_all]
        p = [jnp.exp2(s[pr] - m_safe[pr]) for pr in pr_all]
        for pr in pr_all:
            l_scr[pr] = alpha[pr] * l_scr[pr] + jnp.sum(p[pr], axis=0, keepdims=True)
            acc_scr[pr] = alpha[pr] * acc_scr[pr] + _dot(vt16[pr // pairs_per_kv], p[pr].astype(BF16))
            m_scr[pr] = m_new[pr]
        return carry

    lax.fori_loop(0, n_tiles, attend_tile, 0)
    for pr in range(n_pairs):
        o_t = acc_scr[pr] / l_scr[pr]
        for e in range(2):
            cols = slice((2 * pr + e) * C_HEAD, (2 * pr + e + 1) * C_HEAD)
            o = o_t[:, e * nq:(e + 1) * nq].T
            y_ref[:, cols] = (o * _silu(z_ref[:, cols])).astype(y_ref.dtype)


def _dsa_prompt(p3):
    b, l, _ = p3.shape
    assert l % Q_BLOCK == 0
    topk = min(TOPK_MAX, l // 4)
    nq = Q_BLOCK
    kt = DSA_KEY_TILE if l % DSA_KEY_TILE == 0 else 256
    assert l % kt == 0
    n_qb = l // nq
    return pl.pallas_call(
        functools.partial(_dsa_prompt_kernel, l=l, topk=topk, kt=kt),
        grid=(b, n_qb),
        in_specs=[pl.BlockSpec((None, nq, W_C), lambda i, j: (i, j, OFF["qi"] // W_C)),
                  pl.BlockSpec((None, nq, 128), lambda i, j: (i, j, OFF_TAIL // 128)),
                  pl.BlockSpec((None, l, 128), lambda i, j: (i, 0, OFF_TAIL // 128)),
                  pl.BlockSpec((None, nq, W_C), lambda i, j: (i, j, OFF["qc"] // W_C)),
                  pl.BlockSpec((None, l, C_KV), lambda i, j: (i, 0, OFF["kc"] // C_KV)),
                  pl.BlockSpec((None, l, C_KV), lambda i, j: (i, 0, OFF["vc"] // C_KV)),
                  pl.BlockSpec((None, nq, W_C), lambda i, j: (i, j, OFF["zc"] // W_C))],
        out_specs=pl.BlockSpec((None, nq, W_C), lambda i, j: (i, j, 0)),
        out_shape=jax.ShapeDtypeStruct((b, l, W_C), BF16),
        scratch_shapes=[pltpu.VMEM((l, nq), F32),
                        pltpu.VMEM((C_HEADS // 2, 1, 2 * nq), F32), pltpu.VMEM((C_HEADS // 2, 1, 2 * nq), F32),
                        pltpu.VMEM((C_HEADS // 2, C_HEAD, 2 * nq), F32)],
        compiler_params=_cp("dsa_prompt", 2),
        name="dsa_prompt",
    )(p3, p3, p3, p3, p3, p3, p3)


def _idx_scores(logits, wcol, t):
    r = jnp.maximum(logits, 0.0) * wcol
    return jnp.sum(r.reshape(t, IDX_HEADS, logits.shape[1]), axis=1)


def _dsa_s_score_kernel(pt_ref, q_ref, w_ref, kn_ref, *refs, pg, t):
    pages = refs[:pg]
    out_ref, new_ref = refs[pg:]
    q16 = q_ref[...].astype(BF16)
    wcol = w_ref[...] * ((IDX_HEADS ** -0.5) * (IDX_DIM ** -0.5))
    for i in range(pg):
        out_ref[:, i * 128:(i + 1) * 128] = _idx_scores(_dot(q16, pages[i][...].astype(BF16)), wcol, t)

    @pl.when(pl.program_id(1) == 0)
    def _():
        new_ref[...] = _idx_scores(_dot_nt(q16, kn_ref[...].astype(BF16)), wcol, t)


def _dsa_s_select_kernel(sc_ref, scn_ref, tq_ref, bias_ref, *, nk, topk):
    rows = sc_ref.shape[0]
    tq = tq_ref[...]
    new_ok = lax.broadcasted_iota(I32, (rows, 128), 1) <= tq
    sc_new = jnp.where(new_ok, scn_ref[...], NEG_INF)
    sc_past = sc_ref[...]

    def count_ge(cand):
        c1 = jnp.sum(jnp.where(sc_past >= cand, 1.0, 0.0), axis=1, keepdims=True)
        c2 = jnp.sum(jnp.where(sc_new >= cand, 1.0, 0.0), axis=1, keepdims=True)
        return c1 + c2

    thr = _topk_threshold(count_ge, (rows, 1), topk)
    keep_all = (nk + 1 + tq[:, 0:1]) <= topk
    n_gt = (jnp.sum(jnp.where(sc_past > thr, 1.0, 0.0), axis=1, keepdims=True)
            + jnp.sum(jnp.where(sc_new > thr, 1.0, 0.0), axis=1, keepdims=True))
    need = float(topk) - n_gt
    tri16 = jnp.where(lax.broadcasted_iota(I32, (128, 128), 0) <= lax.broadcasted_iota(I32, (128, 128), 1),
                      1.0, 0.0).astype(BF16)

    def bias_blocks(scs, seen):
        eqs = [sc == thr for sc in scs]
        eq_fs = [jnp.where(eq, 1.0, 0.0) for eq in eqs]
        prefs = [_dot(eq_f.astype(BF16), tri16) for eq_f in eq_fs]
        out = []
        for sc, eq, eq_f, pref in zip(scs, eqs, eq_fs, prefs):
            tie_ok = (seen + pref - eq_f) < need
            chosen = jnp.where(sc > thr, 0.0, jnp.where(eq, jnp.where(tie_ok, 0.0, NEG_INF), NEG_INF))
            out.append(jnp.where(keep_all, jnp.where(sc > NEG_INF, 0.0, NEG_INF), chosen))
            seen = seen + pref[:, 127:128]
        return out, seen

    n_blk = nk // 128
    grp = 8 if n_blk % 8 == 0 else 1

    def past_group(gi, seen):
        lanes = [pl.ds(pl.multiple_of((gi * grp + i) * 128, 128), 128) for i in range(grp)]
        out, seen = bias_blocks([sc_ref[:, ln] for ln in lanes], seen)
        for ln, bias in zip(lanes, out):
            bias_ref[:, ln] = bias
        return seen

    seen = lax.fori_loop(0, n_blk // grp, past_group, jnp.zeros((rows, 1), F32))
    (bias_new,), _ = bias_blocks([sc_new], seen)
    bias_ref[:, nk:] = bias_new


def _dsa_s_attend_kernel(pt_ref, q_ref, bias_ref, biasn_ref, kn_ref, vn_ref, z_ref, *refs, pg):
    kpages = refs[:pg]
    vpages = refs[pg:2 * pg]
    o_ref = refs[2 * pg]
    m_scr, l_scr, acc_scr = refs[2 * pg + 1:]
    g = pl.program_id(1)
    scale = C_HEAD ** -0.5
    page = kpages[0].shape[0] // C_KV_HEADS

    @pl.when(g == 0)
    def _():
        m_scr[...] = jnp.full(m_scr.shape, NEG_INF, F32)
        l_scr[...] = jnp.zeros(l_scr.shape, F32)
        acc_scr[...] = jnp.zeros(acc_scr.shape, F32)
        for hk in range(C_KV_HEADS):
            cols = slice(hk * C_HEAD, (hk + 1) * C_HEAD)
            q16 = (q_ref[hk] * scale).astype(BF16)
            s = _dot_nt(q16, kn_ref[:, cols].astype(BF16)) + biasn_ref[...]
            _softmax_update(m_scr, l_scr, acc_scr, hk, s, [vn_ref[:, cols].astype(BF16)])

    def two_pages(refs_, i, rows):
        return jnp.concatenate([refs_[i][rows, :], refs_[i + 1][rows, :]], axis=0).astype(BF16)

    for hk in range(C_KV_HEADS):
        q16 = (q_ref[hk] * scale).astype(BF16)
        rows = pl.ds(hk, page, stride=C_KV_HEADS)
        tiles = [_dot_nt(q16, two_pages(kpages, i, rows)) for i in range(0, pg, 2)]
        s = jnp.concatenate(tiles, axis=1) + bias_ref[...]
        _softmax_update(m_scr, l_scr, acc_scr, hk, s, [two_pages(vpages, i, rows) for i in range(0, pg, 2)])

    @pl.when(g == pl.num_programs(1) - 1)
    def _():
        for hk in range(C_KV_HEADS):
            o_ref[hk] = (acc_scr[hk] / l_scr[hk]) * _silu(z_ref[hk])


def _dsa_sample(p3s, cache_k, cache_v, cache_kidx, page_table, layer):
    b, t, _ = p3s.shape
    n_pages = page_table.shape[1]
    page = cache_k.shape[2]
    assert page == 128
    past = n_pages * page
    topk = min(TOPK_MAX, (past + t) // 4)
    pg = 32 if n_pages % 32 == 0 else (16 if n_pages % 16 == 0 else n_pages)
    assert pg % 2 == 0
    ng = n_pages // pg
    rows = t * C_GROUPS

    qi = p3s[:, :, OFF["qi"]:OFF["qi"] + IDX_HEADS * IDX_DIM].reshape(b, t * IDX_HEADS, IDX_DIM)
    wi = p3s[:, :, OFF_TAIL + TAIL_WI:OFF_TAIL + TAIL_WI + IDX_HEADS].reshape(b, t * IDX_HEADS, 1)
    ki_new = jnp.pad(p3s[:, :, OFF_TAIL:OFF_TAIL + IDX_DIM], ((0, 0), (0, 128 - t), (0, 0)))
    k_new = jnp.pad(p3s[:, :, OFF["kc"]:OFF["kc"] + C_KV], ((0, 0), (0, 128 - t), (0, 0)))
    v_new = jnp.pad(p3s[:, :, OFF["vc"]:OFF["vc"] + C_KV], ((0, 0), (0, 128 - t), (0, 0)))

    def heads_major(a):
        a = a.reshape(b, t, C_KV_HEADS, C_GROUPS, C_HEAD)
        return jnp.transpose(a, (0, 2, 1, 3, 4)).reshape(b, C_KV_HEADS, rows, C_HEAD)

    qh = heads_major(p3s[:, :, OFF["qc"]:OFF["qc"] + W_C])
    zh = heads_major(p3s[:, :, OFF["zc"]:OFF["zc"] + W_C])

    kidx_t = jnp.swapaxes(cache_kidx, 2, 3)
    score_spec = pltpu.PrefetchScalarGridSpec(
        num_scalar_prefetch=1,
        grid=(b, ng),
        in_specs=[pl.BlockSpec((None, t * IDX_HEADS, IDX_DIM), lambda i, g, pt: (i, 0, 0)),
                  pl.BlockSpec((None, t * IDX_HEADS, 1), lambda i, g, pt: (i, 0, 0)),
                  pl.BlockSpec((None, 128, IDX_DIM), lambda i, g, pt: (i, 0, 0))]
                 + [pl.BlockSpec((None, None, IDX_DIM, page),
                                 lambda i, g, pt, j=j: (layer, pt[i, g * pg + j], 0, 0)) for j in range(pg)],
        out_specs=[pl.BlockSpec((None, t, pg * page), lambda i, g, pt: (i, 0, g)),
                   pl.BlockSpec((None, t, 128), lambda i, g, pt: (i, 0, 0))],
    )
    scores, scores_new = pl.pallas_call(
        functools.partial(_dsa_s_score_kernel, pg=pg, t=t),
        grid_spec=score_spec,
        out_shape=[jax.ShapeDtypeStruct((b, t, past), F32), jax.ShapeDtypeStruct((b, t, 128), F32)],
        compiler_params=_cp("dsa_s_score", 2),
        name="dsa_s_score",
    )(page_table, qi, wi, ki_new, *([kidx_t] * pg))

    tq = jnp.broadcast_to(jnp.tile(jnp.arange(t, dtype=I32), b)[:, None], (b * t, 128))
    bias = pl.pallas_call(
        functools.partial(_dsa_s_select_kernel, nk=past, topk=topk),
        grid=(1,),
        in_specs=[pl.BlockSpec((b * t, past), lambda i: (0, 0)),
                  pl.BlockSpec((b * t, 128), lambda i: (0, 0)),
                  pl.BlockSpec((b * t, 128), lambda i: (0, 0))],
        out_specs=pl.BlockSpec((b * t, past + 128), lambda i: (0, 0)),
        out_shape=jax.ShapeDtypeStruct((b * t, past + 128), F32),
        compiler_params=_cp("dsa_s_select", 1),
        name="dsa_s_select",
    )(scores.reshape(b * t, past), scores_new.reshape(b * t, 128), tq).reshape(b, t, past + 128)

    bias_rows = jnp.repeat(bias, C_GROUPS, axis=1)
    bias_past = bias_rows[:, :, :past]
    bias_new = bias_rows[:, :, past:]

    ck = cache_k.reshape(cache_k.shape[0], cache_k.shape[1], page * C_KV_HEADS, C_HEAD)
    cv = cache_v.reshape(cache_v.shape[0], cache_v.shape[1], page * C_KV_HEADS, C_HEAD)
    pspec = [pl.BlockSpec((None, None, page * C_KV_HEADS, C_HEAD),
                          lambda i, g, pt, j=j: (layer, pt[i, g * pg + j], 0, 0)) for j in range(pg)]
    attend_spec = pltpu.PrefetchScalarGridSpec(
        num_scalar_prefetch=1,
        grid=(b, ng),
        in_specs=[pl.BlockSpec((None, C_KV_HEADS, rows, C_HEAD), lambda i, g, pt: (i, 0, 0, 0)),
                  pl.BlockSpec((None, rows, pg * page), lambda i, g, pt: (i, 0, g)),
                  pl.BlockSpec((None, rows, 128), lambda i, g, pt: (i, 0, 0)),
                  pl.BlockSpec((None, 128, C_KV), lambda i, g, pt: (i, 0, 0)),
                  pl.BlockSpec((None, 128, C_KV), lambda i, g, pt: (i, 0, 0)),
                  pl.BlockSpec((None, C_KV_HEADS, rows, C_HEAD), lambda i, g, pt: (i, 0, 0, 0))]
                 + pspec + pspec,
        out_specs=pl.BlockSpec((None, C_KV_HEADS, rows, C_HEAD), lambda i, g, pt: (i, 0, 0, 0)),
        scratch_shapes=[pltpu.VMEM((C_KV_HEADS, rows, 1), F32), pltpu.VMEM((C_KV_HEADS, rows, 1), F32),
                        pltpu.VMEM((C_KV_HEADS, rows, C_HEAD), F32)],
    )
    oh = pl.pallas_call(
        functools.partial(_dsa_s_attend_kernel, pg=pg),
        grid_spec=attend_spec,
        out_shape=jax.ShapeDtypeStruct((b, C_KV_HEADS, rows, C_HEAD), F32),
        compiler_params=_cp("dsa_s_attend", 2),
        name="dsa_s_attend",
    )(page_table, qh, bias_past, bias_new, k_new, v_new, zh, *([ck] * pg), *([cv] * pg))
    y = jnp.transpose(oh.reshape(b, C_KV_HEADS, t, C_GROUPS, C_HEAD), (0, 2, 1, 3, 4)).reshape(b, t, W_C)
    return y.astype(BF16)


def _outproj_kernel(ya_ref, yb_ref, yc_ref, w_ref, x_ref, gate_ref, o_ref, mix_scr):
    @pl.when(pl.program_id(1) == 0)
    def _():
        mix_scr[:, 0:W_A] = ya_ref[...]
        mix_scr[:, W_A:W_A + W_B] = yb_ref[...]
        mix_scr[:, W_A + W_B:] = yc_ref[...]

    o_ref[...] = x_ref[...] + gate_ref[...] * _dot(mix_scr[...], w_ref[...])


def _outproj(ya, yb, yc, w_bf16, layer, x2, gate, *, rows_per_batch, mod_row0):
    m, d = x2.shape
    tn = 1024
    if gate.ndim == 3:
        tm = 512
        gate_spec = pl.BlockSpec((None, 1, tn),
                                 lambda i, j: ((i * tm) // rows_per_batch + mod_row0, 0, 2 * (d // tn) + j))
    else:
        tm = m
        gate_spec = pl.BlockSpec((tm, tn), lambda i, j: (i, j))
    assert m % tm == 0
    return pl.pallas_call(
        _outproj_kernel,
        grid=(m // tm, d // tn),
        in_specs=[pl.BlockSpec((tm, W_A), lambda i, j: (i, 0)),
                  pl.BlockSpec((tm, W_B), lambda i, j: (i, 0)),
                  pl.BlockSpec((tm, W_C), lambda i, j: (i, 0)),
                  pl.BlockSpec((None, d, tn), lambda i, j: (layer, 0, j)),
                  pl.BlockSpec((tm, tn), lambda i, j: (i, j)),
                  gate_spec],
        out_specs=pl.BlockSpec((tm, tn), lambda i, j: (i, j)),
        out_shape=jax.ShapeDtypeStruct((m, d), F32),
        scratch_shapes=[pltpu.VMEM((tm, d), BF16)],
        compiler_params=_cp("outproj", 2),
        name="outproj",
    )(ya, yb, yc, w_bf16, x2, gate)


def _final_norm_kernel(x_ref, g_ref, o_ref):
    x = x_ref[...]
    o_ref[...] = x * lax.rsqrt(jnp.mean(x * x, axis=-1, keepdims=True) + EPS) * g_ref[...]


def _final_norm(x2, g):
    m, d = x2.shape
    tm = 256 if m % 256 == 0 else m
    return pl.pallas_call(
        _final_norm_kernel,
        grid=(m // tm,),
        in_specs=[pl.BlockSpec((tm, d), lambda i: (i, 0)), pl.BlockSpec((1, d), lambda i: (0, 0))],
        out_specs=pl.BlockSpec((tm, d), lambda i: (i, 0)),
        out_shape=jax.ShapeDtypeStruct((m, d), F32),
        compiler_params=_cp("final_norm", 1),
        name="final_norm",
    )(x2, g.reshape(1, d))


def _w_in_tile_table():
    n_t = NP // 128
    table, n_plain = [], None
    for dt in range(n_t - 1):
        c = dt * 128
        seg = [n for n in _DST_ORDER if OFF[n] <= c < OFF[n] + _SRC_OFF[n][1]][0]
        src = _SRC_OFF[seg][0] + (c - OFF[seg])
        if src % 128 == 0:
            assert n_plain is None
        else:
            assert src % 128 == 32
            if n_plain is None:
                n_plain = dt
        table.append(src // 128)
    t_kw, t_ab = _SRC_OFF["ki"][0] // 128, _SRC_OFF["a"][0] // 128
    assert _SRC_OFF["ki"][0] % 128 == 32 and _SRC_OFF["wi"][0] == t_kw * 128 + TAIL_WI
    assert _SRC_OFF["a"][0] % 128 == 0 and _SRC_OFF["b"][0] == _SRC_OFF["a"][0] + DN_HEADS
    assert TAIL_A == IDX_DIM and TAIL_B == TAIL_A + DN_HEADS and TAIL_WI == TAIL_B + DN_HEADS
    tile_a = table + [t_kw]
    rows_b = [t_ab * 4] * n_plain + [(s + 1) * 4 for s in table[n_plain:]] + [t_ab * 4]
    return tile_a, rows_b, n_plain


def _wprep_kernel(ta_ref, tb_ref, a_ref, b_ref, o_ref, *, n_plain, n_t):
    dt = pl.program_id(1)
    dt_o = o_ref.dtype

    @pl.when(dt < n_plain)
    def _():
        o_ref[...] = a_ref[...].astype(dt_o)

    @pl.when((dt >= n_plain) & (dt < n_t - 1))
    def _():
        o_ref[0:96, :] = a_ref[32:128, :].astype(dt_o)
        o_ref[96:128, :] = b_ref[...].astype(dt_o)

    @pl.when(dt == n_t - 1)
    def _():
        o_ref[0:TAIL_A, :] = a_ref[32:32 + IDX_DIM, :].astype(dt_o)
        o_ref[TAIL_A:TAIL_WI, :] = b_ref[...].astype(dt_o)
        o_ref[TAIL_WI:TAIL_WI + IDX_HEADS, :] = a_ref[TAIL_WI:TAIL_WI + IDX_HEADS, :].astype(dt_o)
        o_ref[TAIL_WI + IDX_HEADS:, :] = jnp.zeros((128 - TAIL_WI - IDX_HEADS, o_ref.shape[1]), dt_o)


def _permute_w_in(w_in):
    depth, d, _ = w_in.shape
    w_t = jnp.swapaxes(w_in, 1, 2)
    tile_a, rows_b, n_plain = _w_in_tile_table()
    n_t = NP // 128
    grid_spec = pltpu.PrefetchScalarGridSpec(
        num_scalar_prefetch=2,
        grid=(depth, n_t),
        in_specs=[pl.BlockSpec((None, 128, d), lambda l, t, ta, tb: (l, ta[t], 0)),
                  pl.BlockSpec((None, 32, d), lambda l, t, ta, tb: (l, tb[t], 0))],
        out_specs=pl.BlockSpec((None, 128, d), lambda l, t, ta, tb: (l, t, 0)),
    )
    return pl.pallas_call(
        functools.partial(_wprep_kernel, n_plain=n_plain, n_t=n_t),
        grid_spec=grid_spec,
        out_shape=jax.ShapeDtypeStruct((depth, NP, d), BF16),
        compiler_params=_cp("w_in_layout", 2),
        name="w_in_layout",
    )(jnp.asarray(tile_a, I32), jnp.asarray(rows_b, I32), w_t, w_t)


def _delta_t(l):
    for t in (128, 64):
        if l % t == 0:
            return t
    raise ValueError("sequence length must be a multiple of 64")


def kernel(x_prompt, x_sample, cache_k, cache_v, cache_kidx, state_dn, state_conv, page_table,
           c_prompt, c_sample, w_ada, b_ada, g_norm, w_in, a_vnorm, a_ws, a_bs, dn_conv_w,
           dn_a_log, dn_dt_bias, dn_onorm, w_out, g_final):
    bp, lp, d = x_prompt.shape
    bs, ls, _ = x_sample.shape
    depth = w_ada.shape[0]
    assert d == D_MODEL and w_in.shape[2] == D_IN
    assert CONV_K - 1 <= ls <= DN_CHUNK

    n_c = bp + bs
    c_rows = jnp.concatenate([c_prompt, c_sample], axis=0)
    r_pad = (-n_c) % 8
    if r_pad:
        c_rows = jnp.pad(c_rows, ((0, r_pad), (0, 0)))
    m_all = _ada(c_rows, w_ada, b_ada)

    xp = x_prompt.reshape(bp * lp, d)
    xs = x_sample.reshape(bs * ls, d)
    ls_pad = DN_CHUNK
    outs = {k: [] for k in ("pk", "pv", "pki", "pdn", "pconv", "sk", "sv", "ski", "sdn", "sconv", "samlp")}
    zeros_conv = jnp.zeros((bp, CONV_K - 1, DN_CONV_DIM), F32)
    zeros_state = jnp.zeros((bp, DN_HEADS, DN_HEAD, DN_HEAD), F32)

    w_in_all = _permute_w_in(w_in)
    w_out_all = w_out.astype(BF16)
    for l in range(depth):
        g_l = g_norm[l].reshape(1, d)
        m_l = m_all[l]
        m3 = m_l.reshape(m_l.shape[0], 1, 3 * d)
        ms = jnp.repeat(m_l[bp:bp + bs], ls, axis=0)

        pp = _inproj(xp, g_l, m3, m3, w_in_all, l, rows_per_batch=lp, mod_row0=0)
        p3 = pp.reshape(bp, lp, NP)
        (ya,) = _mixa(p3, a_vnorm[l], a_ws[l], a_bs[l], emit_va=False)
        abt = jnp.transpose(p3[:, :, OFF["a"]:OFF["a"] + 2 * DN_HEADS], (0, 2, 1))
        yb, s_p = _delta(p3, abt, dn_conv_w[l], zeros_conv, zeros_state, dn_a_log[l], dn_dt_bias[l],
                         dn_onorm[l], t=_delta_t(lp), valid_len=_delta_t(lp))
        yc = _dsa_prompt(p3)
        xp = _outproj(ya.reshape(bp * lp, W_A), yb.reshape(bp * lp, W_B), yc.reshape(bp * lp, W_C),
                      w_out_all, l, xp, m3, rows_per_batch=lp, mod_row0=0)
        outs["pk"].append(p3[:, :, OFF["kc"]:OFF["kc"] + C_KV].reshape(bp, lp, C_KV_HEADS, C_HEAD))
        outs["pv"].append(p3[:, :, OFF["vc"]:OFF["vc"] + C_KV].reshape(bp, lp, C_KV_HEADS, C_HEAD))
        outs["pki"].append(p3[:, :, OFF_TAIL:OFF_TAIL + IDX_DIM])
        outs["pdn"].append(s_p)
        outs["pconv"].append(p3[:, lp - (CONV_K - 1):, OFF["qkv"]:OFF["qkv"] + DN_CONV_DIM])

        ps = _inproj(xs, g_l, ms[:, d:2 * d], ms[:, 0:d], w_in_all, l, rows_per_batch=ls, mod_row0=bp)
        p3s = ps.reshape(bs, ls, NP)
        ya_s, va_s = _mixa(p3s, a_vnorm[l], a_ws[l], a_bs[l], emit_va=True)
        p3s_pad = jnp.pad(p3s, ((0, 0), (0, ls_pad - ls), (0, 0)))
        abt_s = jnp.transpose(p3s_pad[:, :, OFF["a"]:OFF["a"] + 2 * DN_HEADS], (0, 2, 1))
        yb_s, s_s = _delta(p3s_pad, abt_s, dn_conv_w[l], state_conv[l], state_dn[l], dn_a_log[l],
                           dn_dt_bias[l], dn_onorm[l], t=ls_pad, valid_len=ls)
        yc_s = _dsa_sample(p3s, cache_k, cache_v, cache_kidx, page_table, l)
        xs = _outproj(ya_s.reshape(bs * ls, W_A), yb_s[:, :ls].reshape(bs * ls, W_B),
                      yc_s.reshape(bs * ls, W_C), w_out_all, l, xs, ms[:, 2 * d:3 * d],
                      rows_per_batch=ls, mod_row0=bp)
        outs["sk"].append(p3s[:, :, OFF["kc"]:OFF["kc"] + C_KV].reshape(bs, ls, C_KV_HEADS, C_HEAD))
        outs["sv"].append(p3s[:, :, OFF["vc"]:OFF["vc"] + C_KV].reshape(bs, ls, C_KV_HEADS, C_HEAD))
        outs["ski"].append(p3s[:, :, OFF_TAIL:OFF_TAIL + IDX_DIM])
        outs["sdn"].append(s_s)
        outs["sconv"].append(p3s[:, ls - (CONV_K - 1):, OFF["qkv"]:OFF["qkv"] + DN_CONV_DIM])
        outs["samlp"].append(va_s)

    y_prompt = _final_norm(xp, g_final).reshape(bp, lp, d)
    y_sample = _final_norm(xs, g_final).reshape(bs, ls, d)
    st = jnp.stack
    return (y_prompt, y_sample, st(outs["pk"]), st(outs["pv"]), st(outs["pki"]), st(outs["pdn"]),
            st(outs["pconv"]), st(outs["sk"]), st(outs["sv"]), st(outs["ski"]), st(outs["sdn"]),
            st(outs["sconv"]), st(outs["samlp"]))
```

```python
import functools

import jax
import jax.numpy as jnp
from jax import lax
from jax.experimental import pallas as pl
from jax.experimental.pallas import tpu as pltpu

F32 = jnp.float32
BF16 = jnp.bfloat16
I32 = jnp.int32
EPS = 1e-6
INT_MIN = -(2 ** 31)
NEG_INF = float("-inf")

D_MODEL = 4096
W_A = D_MODEL // 4
A_GROUP = 128
A_HEADS = W_A // A_GROUP
A_CHUNK = 128
W_B = D_MODEL // 2
DN_HEAD = 128
DN_HEADS = W_B // DN_HEAD
CONV_K = 4
DN_CONV_DIM = 3 * W_B
DN_CHUNK = 64
W_C = D_MODEL - W_A - W_B
C_HEAD = 128
C_HEADS = W_C // C_HEAD
C_KV_HEADS = 2
C_GROUPS = C_HEADS // C_KV_HEADS
C_KV = C_KV_HEADS * C_HEAD
IDX_HEADS = 16
IDX_DIM = 64
TOPK_MAX = 256
Q_BLOCK = 256

_SRC_SPLITS = (W_A, W_A, W_A, DN_CONV_DIM, W_B, DN_HEADS, DN_HEADS,
               W_C, C_KV, C_KV, W_C, IDX_HEADS * IDX_DIM, IDX_DIM, IDX_HEADS)
_SRC_NAMES = ("u", "v", "za", "qkv", "zb", "a", "b", "qc", "kc", "vc", "zc", "qi", "ki", "wi")
_SRC_OFF = {}
_o = 0
for _n, _w in zip(_SRC_NAMES, _SRC_SPLITS):
    _SRC_OFF[_n] = (_o, _w)
    _o += _w
D_IN = _o

_DST_ORDER = ("u", "v", "za", "qkv", "zb", "qc", "zc", "qi", "kc", "vc", "ki", "a", "b", "wi")
OFF = {}
_o = 0
for _n in _DST_ORDER:
    OFF[_n] = _o
    _o += _SRC_OFF[_n][1]
NP = ((_o + 127) // 128) * 128
OFF_TAIL = OFF["ki"]
TAIL_A = OFF["a"] - OFF_TAIL
TAIL_B = OFF["b"] - OFF_TAIL
TAIL_WI = OFF["wi"] - OFF_TAIL

MXU_COLUMNS = 256
INPROJ_TN = 6 * MXU_COLUMNS
DELTA_HEADS_PER_STEP = 8
DSA_KEY_TILE = 512

VMEM_CAPACITY_MIB = 64
VMEM_LIMIT_MIB = {
    "ada": 40, "w_in_layout": 32, "inproj": 56, "mixa": 32, "delta": 40, "dsa_prompt": 48,
    "dsa_s_score": 32, "dsa_s_select": 40, "dsa_s_attend": 40, "outproj": 48, "final_norm": 32,
}
assert max(VMEM_LIMIT_MIB.values()) < VMEM_CAPACITY_MIB


def _cp(name, n_axes):
    return pltpu.CompilerParams(dimension_semantics=("arbitrary",) * n_axes,
                                vmem_limit_bytes=VMEM_LIMIT_MIB[name] * 1024 * 1024)


def _silu(x):
    return (0.5 * x) * (1.0 + jnp.tanh(0.5 * x))


def _gelu(x):
    return 0.5 * x * (1.0 + jnp.tanh(0.7978845608028654 * (x + 0.044715 * (x * x * x))))


def _dot(a, b):
    return jnp.dot(a, b, preferred_element_type=F32)


def _dot_nt(a, b):
    return lax.dot_general(a, b, (((1,), (1,)), ((), ())), preferred_element_type=F32)


def _dot_tn(a, b):
    return lax.dot_general(a, b, (((0,), (0,)), ((), ())), preferred_element_type=F32)


def _b16(a):
    return a.astype(BF16)


def _ada_kernel(c_ref, w_ref, b_ref, o_ref):
    s = _silu(c_ref[...]).astype(BF16)
    o_ref[...] = _dot(s, w_ref[...].astype(BF16)) + b_ref[...]


def _ada(c_rows, w_ada, b_ada):
    depth, d, n = w_ada.shape
    r = c_rows.shape[0]
    tn = 512
    return pl.pallas_call(
        _ada_kernel,
        grid=(depth, n // tn),
        in_specs=[pl.BlockSpec((r, d), lambda l, j: (0, 0)),
                  pl.BlockSpec((None, d, tn), lambda l, j: (l, 0, j)),
                  pl.BlockSpec((None, 1, tn), lambda l, j: (l, 0, j))],
        out_specs=pl.BlockSpec((None, r, tn), lambda l, j: (l, 0, j)),
        out_shape=jax.ShapeDtypeStruct((depth, r, n), F32),
        compiler_params=_cp("ada", 2),
        name="ada",
    )(c_rows, w_ada, b_ada.reshape(depth, 1, n))


def _inproj_kernel(x_ref, g_ref, sc_ref, sh_ref, w_ref, o_ref, h_scr, *, rc):
    @pl.when(pl.program_id(1) == 0)
    def _():
        tm = x_ref.shape[0]
        per_row = sc_ref.shape[0] != 1

        def body(c, carry):
            r = pl.ds(pl.multiple_of(c * rc, rc), rc)
            x = x_ref[r, :]
            y = x * lax.rsqrt(jnp.mean(x * x, axis=-1, keepdims=True) + EPS)
            sc = sc_ref[r, :] if per_row else sc_ref[...]
            sh = sh_ref[r, :] if per_row else sh_ref[...]
            h_scr[r, :] = ((y * g_ref[...]) * (1.0 + sc) + sh).astype(BF16)
            return carry

        lax.fori_loop(0, tm // rc, body, 0)

    o_ref[...] = _dot_nt(h_scr[...], w_ref[...])


def _inproj(x2, g, sc, sh, w_bf16, layer, *, rows_per_batch, mod_row0):
    m, d = x2.shape
    n = w_bf16.shape[1]
    tn = INPROJ_TN
    if sc.ndim == 3:
        tm = 512
        assert rows_per_batch % tm == 0
        sc_spec = pl.BlockSpec((None, 1, d), lambda i, j: ((i * tm) // rows_per_batch + mod_row0, 0, 1))
        sh_spec = pl.BlockSpec((None, 1, d), lambda i, j: ((i * tm) // rows_per_batch + mod_row0, 0, 0))
        rc = 64
    else:
        tm = m
        sc_spec = pl.BlockSpec((tm, d), lambda i, j: (i, 0))
        sh_spec = pl.BlockSpec((tm, d), lambda i, j: (i, 0))
        rc = tm
    assert m % tm == 0
    return pl.pallas_call(
        functools.partial(_inproj_kernel, rc=rc),
        grid=(m // tm, pl.cdiv(n, tn)),
        in_specs=[pl.BlockSpec((tm, d), lambda i, j: (i, 0)),
                  pl.BlockSpec((1, d), lambda i, j: (0, 0)),
                  sc_spec, sh_spec,
                  pl.BlockSpec((None, tn, d), lambda i, j: (layer, j, 0))],
        out_specs=pl.BlockSpec((tm, tn), lambda i, j: (i, j)),
        out_shape=jax.ShapeDtypeStruct((m, n), F32),
        scratch_shapes=[pltpu.VMEM((tm, d), BF16)],
        compiler_params=_cp("inproj", 2),
        name="inproj",
    )(x2, g, sc, sh, w_bf16)


def _mixa_kernel(u_ref, v_ref, z_ref, vn_ref, ws_ref, bst_ref, y_ref, *rest, c, emit_va):
    u = _gelu(u_ref[...])
    v = _gelu(v_ref[...])
    mu = jnp.mean(v, axis=-1, keepdims=True)
    dv = v - mu
    va = dv * lax.rsqrt(jnp.mean(dv * dv, axis=-1, keepdims=True) + EPS) * vn_ref[...]
    if emit_va:
        rest[0][...] = va
    z = _silu(z_ref[...])
    row = lax.broadcasted_iota(I32, (c, c), 0)
    col = lax.broadcasted_iota(I32, (c, c), 1)
    tril = col <= row
    for h in range(A_HEADS):
        cols = slice(h * A_GROUP, (h + 1) * A_GROUP)
        wm = jnp.where(tril, ws_ref[h], 0.0)
        wm16 = wm.astype(BF16)
        for ci in range(u.shape[0] // c):
            rows = slice(ci * c, (ci + 1) * c)
            vh = va[rows, cols]
            if c >= 128:
                mixed = _dot(wm16, vh.astype(BF16))
            else:
                mixed = wm[:, 0:1] * vh[0:1, :]
                for s in range(1, c):
                    mixed = mixed + wm[:, s:s + 1] * vh[s:s + 1, :]
            mixed = mixed + bst_ref[:, h:h + 1]
            y_ref[rows, cols] = (u[rows, cols] * mixed * z[rows, cols]).astype(y_ref.dtype)


def _mixa(p3, a_vnorm, a_ws, a_bs, *, emit_va):
    b, l, _ = p3.shape
    c = min(A_CHUNK, l)
    r = 2 * c if l % (2 * c) == 0 else c
    n = l // r
    ws = a_ws[:, :c, :c]
    bst = a_bs[:, :c].T
    wblk = W_A
    outs = [jax.ShapeDtypeStruct((b, l, W_A), BF16)]
    out_specs = [pl.BlockSpec((None, r, W_A), lambda i, j: (i, j, 0))]
    if emit_va:
        outs.append(jax.ShapeDtypeStruct((b, l, W_A), F32))
        out_specs.append(pl.BlockSpec((None, r, W_A), lambda i, j: (i, j, 0)))
    res = pl.pallas_call(
        functools.partial(_mixa_kernel, c=c, emit_va=emit_va),
        grid=(b, n),
        in_specs=[pl.BlockSpec((None, r, wblk), lambda i, j: (i, j, OFF["u"] // wblk)),
                  pl.BlockSpec((None, r, wblk), lambda i, j: (i, j, OFF["v"] // wblk)),
                  pl.BlockSpec((None, r, wblk), lambda i, j: (i, j, OFF["za"] // wblk)),
                  pl.BlockSpec((1, W_A), lambda i, j: (0, 0)),
                  pl.BlockSpec((A_HEADS, c, c), lambda i, j: (0, 0, 0)),
                  pl.BlockSpec((c, A_HEADS), lambda i, j: (0, 0))],
        out_specs=out_specs,
        out_shape=outs,
        compiler_params=_cp("mixa", 2),
        name="mixa",
    )(p3, p3, p3, a_vnorm.reshape(1, W_A), ws, bst)
    return res


def _delta_kernel(alog_ref, dtb_ref,
                  q_ref, k_ref, v_ref, z_ref, ab_ref,
                  cwq_ref, cwk_ref, cwv_ref, cpq_ref, cpk_ref, cpv_ref,
                  s0_ref, on_ref,
                  y_ref, sout_ref,
                  xbuf, s_scr, *, t, c, hp, valid_len):
    n = pl.program_id(2)
    dh = DN_HEAD

    @pl.when(n == 0)
    def _():
        s_scr[...] = s0_ref[...]
        for j in range(hp):
            lanes = slice(j * dh, (j + 1) * dh)
            xbuf[3 * j + 0, 5:8, :] = cpq_ref[:, lanes]
            xbuf[3 * j + 1, 5:8, :] = cpk_ref[:, lanes]
            xbuf[3 * j + 2, 5:8, :] = cpv_ref[:, lanes]

    _delta_heads(alog_ref, dtb_ref, q_ref, k_ref, v_ref, z_ref, ab_ref, cwq_ref, cwk_ref, cwv_ref,
                 on_ref, y_ref, xbuf, s_scr, t=t, c=c, hp=hp, valid_len=valid_len)

    @pl.when(n == pl.num_programs(2) - 1)
    def _():
        sout_ref[...] = s_scr[...]


def _delta_heads(alog_ref, dtb_ref, q_ref, k_ref, v_ref, z_ref, ab_ref, cwq_ref, cwk_ref, cwv_ref,
                 on_ref, y_ref, xbuf, s_scr, *, t, c, hp, valid_len):
    hg = pl.program_id(1)
    dh = DN_HEAD
    heads = list(range(hp))

    row = lax.broadcasted_iota(I32, (t, t), 0)
    col = lax.broadcasted_iota(I32, (t, t), 1)
    shift = c.bit_length() - 1
    same = (row >> shift) == (col >> shift)
    eye = row == col
    incl = same & (col <= row)
    strict = same & (col < row)
    incl_t = same & (row <= col)
    blk8 = (row >> 3) == (col >> 3)
    off_masks = []
    bs = 8
    while bs < c:
        sh_b = bs.bit_length() - 1
        inner = (row >> sh_b) == (col >> sh_b)
        outer = (row >> (sh_b + 1)) == (col >> (sh_b + 1))
        off_masks.append(outer & jnp.logical_not(inner))
        bs *= 2
    eye_f = jnp.where(eye, 1.0, 0.0)
    if valid_len < t:
        lane_valid = lax.broadcasted_iota(I32, (1, t), 1) < valid_len
        sub_valid = lax.broadcasted_iota(I32, (t, 1), 0) < valid_len

    def to_col(r):
        return jnp.sum(jnp.where(eye, r, 0.0), axis=1, keepdims=True)

    def conv(idx, x, w_ref, lanes):
        xbuf[idx, 8:8 + t, :] = x
        y = xbuf[idx, pl.ds(5, t), :] * w_ref[0:1, lanes]
        for jj in range(1, CONV_K):
            y = y + xbuf[idx, pl.ds(5 + jj, t), :] * w_ref[jj:jj + 1, lanes]
        xbuf[idx, 5:8, :] = x[t - 3:t, :]
        return _silu(y)

    lanes_of = [slice(j * dh, (j + 1) * dh) for j in range(hp)]

    def per_head(f, *lists):
        return [f(*vals) for vals in zip(*lists)]

    def l2n(a):
        return a * lax.rsqrt(jnp.sum(a * a, axis=-1, keepdims=True) + EPS)

    qc = [l2n(conv(3 * j + 0, q_ref[:, lanes_of[j]], cwq_ref, lanes_of[j])) * (dh ** -0.5) for j in heads]
    kc = [l2n(conv(3 * j + 1, k_ref[:, lanes_of[j]], cwk_ref, lanes_of[j])) for j in heads]
    v = [conv(3 * j + 2, v_ref[:, lanes_of[j]], cwv_ref, lanes_of[j]) for j in heads]

    def gates(j):
        h = hg * hp + j
        a_row = ab_ref[pl.ds(h, 1), :]
        b_row = ab_ref[pl.ds(DN_HEADS + h, 1), :]
        xa = a_row + dtb_ref[h]
        softplus = jnp.maximum(xa, 0.0) + jnp.log(1.0 + jnp.exp(-jnp.abs(xa)))
        a_coef = jnp.exp(jnp.zeros((1, 1), F32) + alog_ref[h])
        g_row = -a_coef * softplus
        beta_row = 1.0 / (1.0 + jnp.exp(-b_row))
        if valid_len < t:
            g_row = jnp.where(lane_valid, g_row, 0.0)
            beta_row = jnp.where(lane_valid, beta_row, 0.0)
        return g_row, beta_row

    g_row, beta_row = zip(*[gates(j) for j in heads])
    if valid_len < t:
        kc = per_head(lambda a: jnp.where(sub_valid, a, 0.0), kc)
        v = per_head(lambda a: jnp.where(sub_valid, a, 0.0), v)

    g_col = per_head(to_col, g_row)
    beta_col = per_head(to_col, beta_row)
    gc_col = per_head(lambda r: jnp.sum(jnp.where(incl, r, 0.0), axis=1, keepdims=True), g_row)
    glast_col = per_head(lambda r: jnp.sum(jnp.where(same, r, 0.0), axis=1, keepdims=True), g_row)
    gc_row = per_head(lambda cl: jnp.sum(jnp.where(incl_t, cl, 0.0), axis=0, keepdims=True), g_col)
    kb = per_head(lambda a, b: a * b, kc, beta_col)
    kc16 = per_head(lambda a: a.astype(BF16), kc)

    def key_products(kb_, kc16_, qc_, gc, gr):
        decay = jnp.where(incl, jnp.exp(jnp.where(incl, gc - gr, 0.0)), 0.0)
        a_mat = jnp.where(strict, _dot_nt(kb_.astype(BF16), kc16_) * decay, 0.0)
        attn16 = (_dot_nt(qc_.astype(BF16), kc16_) * decay).astype(BF16)
        n0f = jnp.where(blk8, -a_mat, 0.0)
        return attn16, _b16(n0f), eye_f + n0f, [_b16(jnp.where(om, a_mat, 0.0)) for om in off_masks]

    attn, n0, x, a_offs = zip(*per_head(key_products, kb, kc16, qc, gc_col, gc_row))
    eg = per_head(jnp.exp, gc_col)
    rhs = per_head(lambda vv, bc, kk, e: _b16(jnp.concatenate([vv * bc, kk * e], axis=1)),
                   v, beta_col, kb, eg)
    n2 = per_head(lambda a: _b16(_dot(a, a)), n0)
    n4 = per_head(lambda a: _b16(_dot(a, a)), n2)
    x = per_head(lambda xx, nn: xx + _dot(_b16(xx), nn), x, n2)
    x = per_head(lambda xx, nn: xx + _dot(_b16(xx), nn), x, n4)
    for lvl in range(len(off_masks)):
        xs = per_head(_b16, x)
        xa_off = per_head(lambda s_, offs: _b16(_dot(s_, offs[lvl])), xs, a_offs)
        x = per_head(lambda xx, xo, s_: xx - _dot(xo, s_), x, xa_off, xs)
    sol16 = per_head(lambda xx, r: _b16(_dot(_b16(xx), r)), x, rhs)
    auw = per_head(_dot, attn, sol16)
    qw = per_head(lambda a, e, m_: (a * e - m_[:, dh:]).astype(BF16), qc, eg, auw)
    kg16 = per_head(lambda a, gl_, gc: (a * jnp.exp(gl_ - gc)).astype(BF16), kc, glast_col, gc_col)

    s = [s_scr[j] for j in heads]
    outs = [[] for _ in heads]
    for i in range(t // c):
        rows = slice(i * c, (i + 1) * c)
        s16 = per_head(lambda a: a.astype(BF16), s)
        kuw = per_head(lambda a, b: _dot_tn(a[rows], b[rows]), kg16, sol16)
        o_i = per_head(lambda a, b16, m_: _dot(a[rows], b16) + m_[rows, :dh], qw, s16, auw)
        for pos in range(len(heads)):
            outs[pos].append(o_i[pos])
        gl = per_head(lambda a: jnp.exp(a[i * c:i * c + 1, :]), glast_col)
        s = per_head(lambda g_, s_, m_, b16: g_ * s_ + m_[:, :dh] - _dot(m_[:, dh:].astype(BF16), b16),
                     gl, s, kuw, s16)
    for pos, j in enumerate(heads):
        s_scr[j] = s[pos]
        o = outs[pos][0] if len(outs[pos]) == 1 else jnp.concatenate(outs[pos], axis=0)
        on = o * lax.rsqrt(jnp.mean(o * o, axis=-1, keepdims=True) + EPS) * on_ref[...]
        y_ref[:, lanes_of[j]] = (on * _silu(z_ref[:, lanes_of[j]])).astype(y_ref.dtype)


def _delta(p3, abt, conv_w, conv_prev, s0, a_log, dt_bias, onorm, *, t, valid_len):
    b, l, _ = p3.shape
    assert l % t == 0
    c = min(DN_CHUNK, t)
    hp = DELTA_HEADS_PER_STEP
    nh = DN_HEADS
    ng = nh // hp
    w = 128 * hp
    cb_q = OFF["qkv"] // w
    cb_z = OFF["zb"] // w
    assert OFF["qkv"] % w == 0 and OFF["zb"] % w == 0 and W_B % w == 0

    def pspec(cb0):
        return pl.BlockSpec((None, t, w), lambda i, h, n, a, d: (i, n, cb0 + h))

    def cwspec(sidx):
        return pl.BlockSpec((CONV_K, w), lambda i, h, n, a, d: (0, sidx * ng + h))

    def cpspec(sidx):
        return pl.BlockSpec((None, CONV_K - 1, w), lambda i, h, n, a, d: (i, 0, sidx * ng + h))

    grid_spec = pltpu.PrefetchScalarGridSpec(
        num_scalar_prefetch=2,
        grid=(b, ng, l // t),
        in_specs=[pspec(cb_q), pspec(cb_q + ng), pspec(cb_q + 2 * ng), pspec(cb_z),
                  pl.BlockSpec((None, 2 * nh, t), lambda i, h, n, a, d: (i, 0, n)),
                  cwspec(0), cwspec(1), cwspec(2), cpspec(0), cpspec(1), cpspec(2),
                  pl.BlockSpec((None, hp, DN_HEAD, DN_HEAD), lambda i, h, n, a, d: (i, h, 0, 0)),
                  pl.BlockSpec((1, DN_HEAD), lambda i, h, n, a, d: (0, 0))],
        out_specs=[pl.BlockSpec((None, t, w), lambda i, h, n, a, d: (i, n, h)),
                   pl.BlockSpec((None, hp, DN_HEAD, DN_HEAD), lambda i, h, n, a, d: (i, h, 0, 0))],
        scratch_shapes=[pltpu.VMEM((3 * hp, t + 8, 128), F32), pltpu.VMEM((hp, DN_HEAD, DN_HEAD), F32)],
    )
    return pl.pallas_call(
        functools.partial(_delta_kernel, t=t, c=c, hp=hp, valid_len=valid_len),
        grid_spec=grid_spec,
        out_shape=[jax.ShapeDtypeStruct((b, l, W_B), BF16),
                   jax.ShapeDtypeStruct((b, nh, DN_HEAD, DN_HEAD), F32)],
        compiler_params=_cp("delta", 3),
        name="delta",
    )(a_log, dt_bias, p3, p3, p3, p3, abt, conv_w, conv_w, conv_w,
      conv_prev, conv_prev, conv_prev, s0, onorm.reshape(1, DN_HEAD))


def _ordered_bits_to_float(u):
    key = u ^ jnp.int32(INT_MIN)
    bits = jnp.where(key < 0, key ^ jnp.int32(0x7FFFFFFF), key)
    return pltpu.bitcast(bits, F32)


def _topk_threshold(count_ge, shape, topk):
    def bit_body(i, u):
        bit = jnp.left_shift(jnp.int32(1), 31 - i)
        cand_u = u | bit
        cnt = count_ge(_ordered_bits_to_float(cand_u))
        return jnp.where(cnt >= float(topk), cand_u, u)

    u = lax.fori_loop(0, 32, bit_body, jnp.zeros(shape, I32))
    return _ordered_bits_to_float(u)


def _softmax_update(m_ref, l_ref, acc_ref, idx, s, v_tiles):
    m_old = m_ref[idx]
    m_new = jnp.maximum(m_old, jnp.max(s, axis=1, keepdims=True))
    m_safe = jnp.where(m_new == NEG_INF, 0.0, m_new)
    alpha = jnp.exp(m_old - m_safe)
    p = jnp.exp(s - m_safe)
    l_ref[idx] = alpha * l_ref[idx] + jnp.sum(p, axis=1, keepdims=True)
    acc = alpha * acc_ref[idx]
    p16 = p.astype(BF16)
    k0 = 0
    for vt in v_tiles:
        acc = acc + _dot(p16[:, k0:k0 + vt.shape[0]], vt)
        k0 += vt.shape[0]
    acc_ref[idx] = acc
    m_ref[idx] = m_new


def _dsa_prompt_kernel(qi_ref, tq_ref, tall_ref, qc_ref, k_ref, v_ref, z_ref, y_ref,
                       sc_scr, m_scr, l_scr, acc_scr, *, l, topk, kt):
    qb = pl.program_id(1)
    nq = Q_BLOCK
    n_tiles = (qb * nq + nq + kt - 1) // kt
    n_pairs = C_HEADS // 2
    w_rows = tq_ref[...].T[TAIL_WI:TAIL_WI + IDX_HEADS, :] * ((IDX_HEADS ** -0.5) * (IDX_DIM ** -0.5))
    rq = []
    for j in range(IDX_HEADS // 2):
        a = qi_ref[:, j * 128:(j + 1) * 128]
        rq.append(jnp.concatenate([a, pltpu.roll(a, IDX_DIM, 1)], axis=0).astype(BF16))
    kpos = lax.broadcasted_iota(I32, (kt, nq), 0)
    qpos = qb * nq + lax.broadcasted_iota(I32, (kt, nq), 1)
    lane = lax.broadcasted_iota(I32, (kt, 128), 1)

    def score_tile(ti, carry):
        k0 = pl.multiple_of(ti * kt, kt)
        ki = jnp.where(lane < IDX_DIM, tall_ref[pl.ds(k0, kt), :], 0.0).astype(BF16)
        score = jnp.zeros((kt, nq), F32)
        for j in range(IDX_HEADS // 2):
            lg = _dot_nt(ki, rq[j])
            score = score + w_rows[2 * j:2 * j + 1, :] * jnp.maximum(lg[:, :nq], 0.0)
            score = score + w_rows[2 * j + 1:2 * j + 2, :] * jnp.maximum(lg[:, nq:], 0.0)
        sc_scr[pl.ds(k0, kt), :] = jnp.where(kpos + k0 <= qpos, score, NEG_INF)
        return carry

    lax.fori_loop(0, n_tiles, score_tile, 0)

    ct = 256
    n_ct = (qb * nq + nq + ct - 1) // ct
    n_acc = 4

    def count(cand, strict):
        cand8 = jnp.broadcast_to(cand, (8, nq))

        def tile_body(ti, accs):
            k0 = pl.multiple_of(ti * ct, ct)
            tile = sc_scr[pl.ds(k0, ct), :]
            accs = list(accs)
            for r in range(ct // 8):
                blk = tile[8 * r:8 * r + 8, :]
                hit = (blk > cand8) if strict else (blk >= cand8)
                accs[r % n_acc] = accs[r % n_acc] + jnp.where(hit, 1.0, 0.0)
            return tuple(accs)

        accs = lax.fori_loop(0, n_ct, tile_body, tuple(jnp.zeros((8, nq), F32) for _ in range(n_acc)))
        acc = (accs[0] + accs[1]) + (accs[2] + accs[3])
        return jnp.sum(acc, axis=0, keepdims=True)

    thr = _topk_threshold(lambda cand: count(cand, False), (1, nq), topk)
    keep_all = (qb * nq + lax.broadcasted_iota(I32, (1, nq), 1)) < topk
    need = float(topk) - count(thr, True)
    tri16 = jnp.where(lax.broadcasted_iota(I32, (128, 128), 1) <= lax.broadcasted_iota(I32, (128, 128), 0),
                      1.0, 0.0).astype(BF16)

    def select_tile(ti, seen):
        k0 = pl.multiple_of(ti * kt, kt)
        sc = sc_scr[pl.ds(k0, kt), :]
        eq = sc == thr
        eq_f = jnp.where(eq, 1.0, 0.0)
        eq16 = eq_f.astype(BF16)
        prefs = [_dot(tri16, eq16[r:r + 128]) for r in range(0, kt, 128)]
        ranks = []
        for pref in prefs:
            ranks.append(seen + pref)
            seen = seen + pref[127:128, :]
        tie_ok = (jnp.concatenate(ranks, axis=0) - eq_f) < need
        chosen = jnp.where(sc > thr, 0.0, jnp.where(eq, jnp.where(tie_ok, 0.0, NEG_INF), NEG_INF))
        sc_scr[pl.ds(k0, kt), :] = jnp.where(keep_all, jnp.where(sc > NEG_INF, 0.0, NEG_INF), chosen)
        return seen

    lax.fori_loop(0, n_tiles, select_tile, jnp.zeros((1, nq), F32))

    m_scr[...] = jnp.full(m_scr.shape, NEG_INF, F32)
    l_scr[...] = jnp.zeros(l_scr.shape, F32)
    acc_scr[...] = jnp.zeros(acc_scr.shape, F32)
    scale = (C_HEAD ** -0.5) * 1.4426950408889634
    qp = []
    for pr in range(n_pairs):
        c0 = slice(2 * pr * C_HEAD, (2 * pr + 1) * C_HEAD)
        c1 = slice((2 * pr + 1) * C_HEAD, (2 * pr + 2) * C_HEAD)
        qp.append(jnp.concatenate([qc_ref[:, c0] * scale, qc_ref[:, c1] * scale], axis=0).astype(BF16))
    pairs_per_kv = n_pairs // C_KV_HEADS

    def attend_tile(ti, carry):
        k0 = pl.multiple_of(ti * kt, kt)
        bias = sc_scr[pl.ds(k0, kt), :]
        k16 = [k_ref[pl.ds(k0, kt), hk * C_HEAD:(hk + 1) * C_HEAD].astype(BF16) for hk in range(C_KV_HEADS)]
        vt16 = [v_ref[pl.ds(k0, kt), hk * C_HEAD:(hk + 1) * C_HEAD].T.astype(BF16) for hk in range(C_KV_HEADS)]
        bias2 = jnp.concatenate([bias, bias], axis=1)
        pr_all = range(n_pairs)
        s = [_dot_nt(k16[pr // pairs_per_kv], qp[pr]) + bias2 for pr in pr_all]
        m_old = [m_scr[pr] for pr in pr_all]
        m_new = [jnp.maximum(m_old[pr], jnp.max(s[pr], axis=0, keepdims=True)) for pr in pr_all]
        m_safe = [jnp.where(m_new[pr] == NEG_INF, 0.0, m_new[pr]) for pr in pr_all]
        alpha = [jnp.exp2(m_old[pr] - m_safe[pr]) for pr in pr_all]
        p = [jnp.exp2(s[pr] - m_safe[pr]) for pr in pr_all]
        for pr in pr_all:
            l_scr[pr] = alpha[pr] * l_scr[pr] + jnp.sum(p[pr], axis=0, keepdims=True)
            acc_scr[pr] = alpha[pr] * acc_scr[pr] + _dot(vt16[pr // pairs_per_kv], p[pr].astype(BF16))
            m_scr[pr] = m_new[pr]
        return carry

    lax.fori_loop(0, n_tiles, attend_tile, 0)
    for pr in range(n_pairs):
        o_t = acc_scr[pr] / l_scr[pr]
        for e in range(2):
            cols = slice((2 * pr + e) * C_HEAD, (2 * pr + e + 1) * C_HEAD)
            o = o_t[:, e * nq:(e + 1) * nq].T
            y_ref[:, cols] = (o * _silu(z_ref[:, cols])).astype(y_ref.dtype)


def _dsa_prompt(p3):
    b, l, _ = p3.shape
    assert l % Q_BLOCK == 0
    topk = min(TOPK_MAX, l // 4)
    nq = Q_BLOCK
    kt = DSA_KEY_TILE if l % DSA_KEY_TILE == 0 else 256
    assert l % kt == 0
    n_qb = l // nq
    return pl.pallas_call(
        functools.partial(_dsa_prompt_kernel, l=l, topk=topk, kt=kt),
        grid=(b, n_qb),
        in_specs=[pl.BlockSpec((None, nq, W_C), lambda i, j: (i, j, OFF["qi"] // W_C)),
                  pl.BlockSpec((None, nq, 128), lambda i, j: (i, j, OFF_TAIL // 128)),
                  pl.BlockSpec((None, l, 128), lambda i, j: (i, 0, OFF_TAIL // 128)),
                  pl.BlockSpec((None, nq, W_C), lambda i, j: (i, j, OFF["qc"] // W_C)),
                  pl.BlockSpec((None, l, C_KV), lambda i, j: (i, 0, OFF["kc"] // C_KV)),
                  pl.BlockSpec((None, l, C_KV), lambda i, j: (i, 0, OFF["vc"] // C_KV)),
                  pl.BlockSpec((None, nq, W_C), lambda i, j: (i, j, OFF["zc"] // W_C))],
        out_specs=pl.BlockSpec((None, nq, W_C), lambda i, j: (i, j, 0)),
        out_shape=jax.ShapeDtypeStruct((b, l, W_C), BF16),
        scratch_shapes=[pltpu.VMEM((l, nq), F32),
                        pltpu.VMEM((C_HEADS // 2, 1, 2 * nq), F32), pltpu.VMEM((C_HEADS // 2, 1, 2 * nq), F32),
                        pltpu.VMEM((C_HEADS // 2, C_HEAD, 2 * nq), F32)],
        compiler_params=_cp("dsa_prompt", 2),
        name="dsa_prompt",
    )(p3, p3, p3, p3, p3, p3, p3)


def _idx_scores(logits, wcol, t):
    r = jnp.maximum(logits, 0.0) * wcol
    return jnp.sum(r.reshape(t, IDX_HEADS, logits.shape[1]), axis=1)


def _dsa_s_score_kernel(pt_ref, q_ref, w_ref, kn_ref, *refs, pg, t):
    pages = refs[:pg]
    out_ref, new_ref = refs[pg:]
    q16 = q_ref[...].astype(BF16)
    wcol = w_ref[...] * ((IDX_HEADS ** -0.5) * (IDX_DIM ** -0.5))
    keys16 = jnp.concatenate([pages[i][...] for i in range(pg)], axis=1).astype(BF16)
    out_ref[...] = _idx_scores(_dot(q16, keys16), wcol, t)

    @pl.when(pl.program_id(1) == 0)
    def _():
        new_ref[...] = _idx_scores(_dot_nt(q16, kn_ref[...].astype(BF16)), wcol, t)


def _dsa_s_select_kernel(sc_ref, scn_ref, tq_ref, bias_ref, *, nk, topk):
    rows = sc_ref.shape[0]
    tq = tq_ref[...]
    new_ok = lax.broadcasted_iota(I32, (rows, 128), 1) <= tq
    sc_new = jnp.where(new_ok, scn_ref[...], NEG_INF)
    sc_past = sc_ref[...]

    def count_ge(cand):
        c1 = jnp.sum(jnp.where(sc_past >= cand, 1.0, 0.0), axis=1, keepdims=True)
        c2 = jnp.sum(jnp.where(sc_new >= cand, 1.0, 0.0), axis=1, keepdims=True)
        return c1 + c2

    thr = _topk_threshold(count_ge, (rows, 1), topk)
    keep_all = (nk + 1 + tq[:, 0:1]) <= topk
    n_gt = (jnp.sum(jnp.where(sc_past > thr, 1.0, 0.0), axis=1, keepdims=True)
            + jnp.sum(jnp.where(sc_new > thr, 1.0, 0.0), axis=1, keepdims=True))
    need = float(topk) - n_gt
    tri16 = jnp.where(lax.broadcasted_iota(I32, (128, 128), 0) <= lax.broadcasted_iota(I32, (128, 128), 1),
                      1.0, 0.0).astype(BF16)

    def bias_blocks(scs, seen):
        eqs = [sc == thr for sc in scs]
        eq_fs = [jnp.where(eq, 1.0, 0.0) for eq in eqs]
        prefs = [_dot(eq_f.astype(BF16), tri16) for eq_f in eq_fs]
        out = []
        for sc, eq, eq_f, pref in zip(scs, eqs, eq_fs, prefs):
            tie_ok = (seen + pref - eq_f) < need
            chosen = jnp.where(sc > thr, 0.0, jnp.where(eq, jnp.where(tie_ok, 0.0, NEG_INF), NEG_INF))
            out.append(jnp.where(keep_all, jnp.where(sc > NEG_INF, 0.0, NEG_INF), chosen))
            seen = seen + pref[:, 127:128]
        return out, seen

    n_blk = nk // 128
    grp = 8 if n_blk % 8 == 0 else 1

    def past_group(gi, seen):
        lanes = [pl.ds(pl.multiple_of((gi * grp + i) * 128, 128), 128) for i in range(grp)]
        out, seen = bias_blocks([sc_ref[:, ln] for ln in lanes], seen)
        for ln, bias in zip(lanes, out):
            bias_ref[:, ln] = bias
        return seen

    seen = lax.fori_loop(0, n_blk // grp, past_group, jnp.zeros((rows, 1), F32))
    (bias_new,), _ = bias_blocks([sc_new], seen)
    bias_ref[:, nk:] = bias_new


def _dsa_s_attend_kernel(pt_ref, q_ref, bias_ref, biasn_ref, kn_ref, vn_ref, z_ref, *refs, pg):
    kpages = refs[:pg]
    vpages = refs[pg:2 * pg]
    o_ref = refs[2 * pg]
    m_scr, l_scr, acc_scr = refs[2 * pg + 1:]
    g = pl.program_id(1)
    scale = C_HEAD ** -0.5
    page = kpages[0].shape[0] // C_KV_HEADS

    @pl.when(g == 0)
    def _():
        m_scr[...] = jnp.full(m_scr.shape, NEG_INF, F32)
        l_scr[...] = jnp.zeros(l_scr.shape, F32)
        acc_scr[...] = jnp.zeros(acc_scr.shape, F32)
        for hk in range(C_KV_HEADS):
            cols = slice(hk * C_HEAD, (hk + 1) * C_HEAD)
            q16 = (q_ref[hk] * scale).astype(BF16)
            s = _dot_nt(q16, kn_ref[:, cols].astype(BF16)) + biasn_ref[...]
            _softmax_update(m_scr, l_scr, acc_scr, hk, s, [vn_ref[:, cols].astype(BF16)])

    def two_pages(refs_, i, rows):
        return jnp.concatenate([refs_[i][rows, :], refs_[i + 1][rows, :]], axis=0).astype(BF16)

    for hk in range(C_KV_HEADS):
        q16 = (q_ref[hk] * scale).astype(BF16)
        rows = pl.ds(hk, page, stride=C_KV_HEADS)
        tiles = [_dot_nt(q16, two_pages(kpages, i, rows)) for i in range(0, pg, 2)]
        s = jnp.concatenate(tiles, axis=1) + bias_ref[...]
        _softmax_update(m_scr, l_scr, acc_scr, hk, s, [two_pages(vpages, i, rows) for i in range(0, pg, 2)])

    @pl.when(g == pl.num_programs(1) - 1)
    def _():
        for hk in range(C_KV_HEADS):
            o_ref[hk] = (acc_scr[hk] / l_scr[hk]) * _silu(z_ref[hk])


def _dsa_sample(p3s, cache_k, cache_v, cache_kidx, page_table, layer):
    b, t, _ = p3s.shape
    n_pages = page_table.shape[1]
    page = cache_k.shape[2]
    assert page == 128
    past = n_pages * page
    topk = min(TOPK_MAX, (past + t) // 4)
    pg = 32 if n_pages % 32 == 0 else (16 if n_pages % 16 == 0 else n_pages)
    assert pg % 2 == 0
    ng = n_pages // pg
    rows = t * C_GROUPS

    qi = p3s[:, :, OFF["qi"]:OFF["qi"] + IDX_HEADS * IDX_DIM].reshape(b, t * IDX_HEADS, IDX_DIM)
    wi = p3s[:, :, OFF_TAIL + TAIL_WI:OFF_TAIL + TAIL_WI + IDX_HEADS].reshape(b, t * IDX_HEADS, 1)
    ki_new = jnp.pad(p3s[:, :, OFF_TAIL:OFF_TAIL + IDX_DIM], ((0, 0), (0, 128 - t), (0, 0)))
    k_new = jnp.pad(p3s[:, :, OFF["kc"]:OFF["kc"] + C_KV], ((0, 0), (0, 128 - t), (0, 0)))
    v_new = jnp.pad(p3s[:, :, OFF["vc"]:OFF["vc"] + C_KV], ((0, 0), (0, 128 - t), (0, 0)))

    def heads_major(a):
        a = a.reshape(b, t, C_KV_HEADS, C_GROUPS, C_HEAD)
        return jnp.transpose(a, (0, 2, 1, 3, 4)).reshape(b, C_KV_HEADS, rows, C_HEAD)

    qh = heads_major(p3s[:, :, OFF["qc"]:OFF["qc"] + W_C])
    zh = heads_major(p3s[:, :, OFF["zc"]:OFF["zc"] + W_C])

    kidx_t = jnp.swapaxes(cache_kidx, 2, 3)
    score_spec = pltpu.PrefetchScalarGridSpec(
        num_scalar_prefetch=1,
        grid=(b, ng),
        in_specs=[pl.BlockSpec((None, t * IDX_HEADS, IDX_DIM), lambda i, g, pt: (i, 0, 0)),
                  pl.BlockSpec((None, t * IDX_HEADS, 1), lambda i, g, pt: (i, 0, 0)),
                  pl.BlockSpec((None, 128, IDX_DIM), lambda i, g, pt: (i, 0, 0))]
                 + [pl.BlockSpec((None, None, IDX_DIM, page),
                                 lambda i, g, pt, j=j: (layer, pt[i, g * pg + j], 0, 0)) for j in range(pg)],
        out_specs=[pl.BlockSpec((None, t, pg * page), lambda i, g, pt: (i, 0, g)),
                   pl.BlockSpec((None, t, 128), lambda i, g, pt: (i, 0, 0))],
    )
    scores, scores_new = pl.pallas_call(
        functools.partial(_dsa_s_score_kernel, pg=pg, t=t),
        grid_spec=score_spec,
        out_shape=[jax.ShapeDtypeStruct((b, t, past), F32), jax.ShapeDtypeStruct((b, t, 128), F32)],
        compiler_params=_cp("dsa_s_score", 2),
        name="dsa_s_score",
    )(page_table, qi, wi, ki_new, *([kidx_t] * pg))

    tq = jnp.broadcast_to(jnp.tile(jnp.arange(t, dtype=I32), b)[:, None], (b * t, 128))
    bias = pl.pallas_call(
        functools.partial(_dsa_s_select_kernel, nk=past, topk=topk),
        grid=(1,),
        in_specs=[pl.BlockSpec((b * t, past), lambda i: (0, 0)),
                  pl.BlockSpec((b * t, 128), lambda i: (0, 0)),
                  pl.BlockSpec((b * t, 128), lambda i: (0, 0))],
        out_specs=pl.BlockSpec((b * t, past + 128), lambda i: (0, 0)),
        out_shape=jax.ShapeDtypeStruct((b * t, past + 128), F32),
        compiler_params=_cp("dsa_s_select", 1),
        name="dsa_s_select",
    )(scores.reshape(b * t, past), scores_new.reshape(b * t, 128), tq).reshape(b, t, past + 128)

    bias_rows = jnp.repeat(bias, C_GROUPS, axis=1)
    bias_past = bias_rows[:, :, :past]
    bias_new = bias_rows[:, :, past:]

    ck = cache_k.reshape(cache_k.shape[0], cache_k.shape[1], page * C_KV_HEADS, C_HEAD)
    cv = cache_v.reshape(cache_v.shape[0], cache_v.shape[1], page * C_KV_HEADS, C_HEAD)
    pspec = [pl.BlockSpec((None, None, page * C_KV_HEADS, C_HEAD),
                          lambda i, g, pt, j=j: (layer, pt[i, g * pg + j], 0, 0)) for j in range(pg)]
    attend_spec = pltpu.PrefetchScalarGridSpec(
        num_scalar_prefetch=1,
        grid=(b, ng),
        in_specs=[pl.BlockSpec((None, C_KV_HEADS, rows, C_HEAD), lambda i, g, pt: (i, 0, 0, 0)),
                  pl.BlockSpec((None, rows, pg * page), lambda i, g, pt: (i, 0, g)),
                  pl.BlockSpec((None, rows, 128), lambda i, g, pt: (i, 0, 0)),
                  pl.BlockSpec((None, 128, C_KV), lambda i, g, pt: (i, 0, 0)),
                  pl.BlockSpec((None, 128, C_KV), lambda i, g, pt: (i, 0, 0)),
                  pl.BlockSpec((None, C_KV_HEADS, rows, C_HEAD), lambda i, g, pt: (i, 0, 0, 0))]
                 + pspec + pspec,
        out_specs=pl.BlockSpec((None, C_KV_HEADS, rows, C_HEAD), lambda i, g, pt: (i, 0, 0, 0)),
        scratch_shapes=[pltpu.VMEM((C_KV_HEADS, rows, 1), F32), pltpu.VMEM((C_KV_HEADS, rows, 1), F32),
                        pltpu.VMEM((C_KV_HEADS, rows, C_HEAD), F32)],
    )
    oh = pl.pallas_call(
        functools.partial(_dsa_s_attend_kernel, pg=pg),
        grid_spec=attend_spec,
        out_shape=jax.ShapeDtypeStruct((b, C_KV_HEADS, rows, C_HEAD), F32),
        compiler_params=_cp("dsa_s_attend", 2),
        name="dsa_s_attend",
    )(page_table, qh, bias_past, bias_new, k_new, v_new, zh, *([ck] * pg), *([cv] * pg))
    y = jnp.transpose(oh.reshape(b, C_KV_HEADS, t, C_GROUPS, C_HEAD), (0, 2, 1, 3, 4)).reshape(b, t, W_C)
    return y.astype(BF16)


def _outproj_kernel(ya_ref, yb_ref, yc_ref, w_ref, x_ref, gate_ref, o_ref, mix_scr):
    @pl.when(pl.program_id(1) == 0)
    def _():
        mix_scr[:, 0:W_A] = ya_ref[...]
        mix_scr[:, W_A:W_A + W_B] = yb_ref[...]
        mix_scr[:, W_A + W_B:] = yc_ref[...]

    o_ref[...] = x_ref[...] + gate_ref[...] * _dot(mix_scr[...], w_ref[...])


def _outproj(ya, yb, yc, w_bf16, layer, x2, gate, *, rows_per_batch, mod_row0):
    m, d = x2.shape
    tn = 1024
    if gate.ndim == 3:
        tm = 512
        gate_spec = pl.BlockSpec((None, 1, tn),
                                 lambda i, j: ((i * tm) // rows_per_batch + mod_row0, 0, 2 * (d // tn) + j))
    else:
        tm = m
        gate_spec = pl.BlockSpec((tm, tn), lambda i, j: (i, j))
    assert m % tm == 0
    return pl.pallas_call(
        _outproj_kernel,
        grid=(m // tm, d // tn),
        in_specs=[pl.BlockSpec((tm, W_A), lambda i, j: (i, 0)),
                  pl.BlockSpec((tm, W_B), lambda i, j: (i, 0)),
                  pl.BlockSpec((tm, W_C), lambda i, j: (i, 0)),
                  pl.BlockSpec((None, d, tn), lambda i, j: (layer, 0, j)),
                  pl.BlockSpec((tm, tn), lambda i, j: (i, j)),
                  gate_spec],
        out_specs=pl.BlockSpec((tm, tn), lambda i, j: (i, j)),
        out_shape=jax.ShapeDtypeStruct((m, d), F32),
        scratch_shapes=[pltpu.VMEM((tm, d), BF16)],
        compiler_params=_cp("outproj", 2),
        name="outproj",
    )(ya, yb, yc, w_bf16, x2, gate)


def _final_norm_kernel(x_ref, g_ref, o_ref):
    x = x_ref[...]
    o_ref[...] = x * lax.rsqrt(jnp.mean(x * x, axis=-1, keepdims=True) + EPS) * g_ref[...]


def _final_norm(x2, g):
    m, d = x2.shape
    tm = 256 if m % 256 == 0 else m
    return pl.pallas_call(
        _final_norm_kernel,
        grid=(m // tm,),
        in_specs=[pl.BlockSpec((tm, d), lambda i: (i, 0)), pl.BlockSpec((1, d), lambda i: (0, 0))],
        out_specs=pl.BlockSpec((tm, d), lambda i: (i, 0)),
        out_shape=jax.ShapeDtypeStruct((m, d), F32),
        compiler_params=_cp("final_norm", 1),
        name="final_norm",
    )(x2, g.reshape(1, d))


def _w_in_tile_table():
    n_t = NP // 128
    table, n_plain = [], None
    for dt in range(n_t - 1):
        c = dt * 128
        seg = [n for n in _DST_ORDER if OFF[n] <= c < OFF[n] + _SRC_OFF[n][1]][0]
        src = _SRC_OFF[seg][0] + (c - OFF[seg])
        if src % 128 == 0:
            assert n_plain is None
        else:
            assert src % 128 == 32
            if n_plain is None:
                n_plain = dt
        table.append(src // 128)
    t_kw, t_ab = _SRC_OFF["ki"][0] // 128, _SRC_OFF["a"][0] // 128
    assert _SRC_OFF["ki"][0] % 128 == 32 and _SRC_OFF["wi"][0] == t_kw * 128 + TAIL_WI
    assert _SRC_OFF["a"][0] % 128 == 0 and _SRC_OFF["b"][0] == _SRC_OFF["a"][0] + DN_HEADS
    assert TAIL_A == IDX_DIM and TAIL_B == TAIL_A + DN_HEADS and TAIL_WI == TAIL_B + DN_HEADS
    tile_a = table + [t_kw]
    rows_b = [t_ab * 4] * n_plain + [(s + 1) * 4 for s in table[n_plain:]] + [t_ab * 4]
    return tile_a, rows_b, n_plain


def _wprep_kernel(ta_ref, tb_ref, a_ref, b_ref, o_ref, *, n_plain, n_t):
    dt = pl.program_id(1)
    dt_o = o_ref.dtype

    @pl.when(dt < n_plain)
    def _():
        o_ref[...] = a_ref[...].astype(dt_o)

    @pl.when((dt >= n_plain) & (dt < n_t - 1))
    def _():
        o_ref[0:96, :] = a_ref[32:128, :].astype(dt_o)
        o_ref[96:128, :] = b_ref[...].astype(dt_o)

    @pl.when(dt == n_t - 1)
    def _():
        o_ref[0:TAIL_A, :] = a_ref[32:32 + IDX_DIM, :].astype(dt_o)
        o_ref[TAIL_A:TAIL_WI, :] = b_ref[...].astype(dt_o)
        o_ref[TAIL_WI:TAIL_WI + IDX_HEADS, :] = a_ref[TAIL_WI:TAIL_WI + IDX_HEADS, :].astype(dt_o)
        o_ref[TAIL_WI + IDX_HEADS:, :] = jnp.zeros((128 - TAIL_WI - IDX_HEADS, o_ref.shape[1]), dt_o)


def _permute_w_in(w_in):
    depth, d, _ = w_in.shape
    w_t = jnp.swapaxes(w_in, 1, 2)
    tile_a, rows_b, n_plain = _w_in_tile_table()
    n_t = NP // 128
    grid_spec = pltpu.PrefetchScalarGridSpec(
        num_scalar_prefetch=2,
        grid=(depth, n_t),
        in_specs=[pl.BlockSpec((None, 128, d), lambda l, t, ta, tb: (l, ta[t], 0)),
                  pl.BlockSpec((None, 32, d), lambda l, t, ta, tb: (l, tb[t], 0))],
        out_specs=pl.BlockSpec((None, 128, d), lambda l, t, ta, tb: (l, t, 0)),
    )
    return pl.pallas_call(
        functools.partial(_wprep_kernel, n_plain=n_plain, n_t=n_t),
        grid_spec=grid_spec,
        out_shape=jax.ShapeDtypeStruct((depth, NP, d), BF16),
        compiler_params=_cp("w_in_layout", 2),
        name="w_in_layout",
    )(jnp.asarray(tile_a, I32), jnp.asarray(rows_b, I32), w_t, w_t)


def _delta_t(l):
    for t in (128, 64):
        if l % t == 0:
            return t
    raise ValueError("sequence length must be a multiple of 64")


def kernel(x_prompt, x_sample, cache_k, cache_v, cache_kidx, state_dn, state_conv, page_table,
           c_prompt, c_sample, w_ada, b_ada, g_norm, w_in, a_vnorm, a_ws, a_bs, dn_conv_w,
           dn_a_log, dn_dt_bias, dn_onorm, w_out, g_final):
    bp, lp, d = x_prompt.shape
    bs, ls, _ = x_sample.shape
    depth = w_ada.shape[0]
    assert d == D_MODEL and w_in.shape[2] == D_IN
    assert CONV_K - 1 <= ls <= DN_CHUNK

    n_c = bp + bs
    c_rows = jnp.concatenate([c_prompt, c_sample], axis=0)
    r_pad = (-n_c) % 8
    if r_pad:
        c_rows = jnp.pad(c_rows, ((0, r_pad), (0, 0)))
    m_all = _ada(c_rows, w_ada, b_ada)

    xp = x_prompt.reshape(bp * lp, d)
    xs = x_sample.reshape(bs * ls, d)
    ls_pad = 8 * ((ls + 7) // 8)
    outs = {k: [] for k in ("pk", "pv", "pki", "pdn", "pconv", "sk", "sv", "ski", "sdn", "sconv", "samlp")}
    zeros_conv = jnp.zeros((bp, CONV_K - 1, DN_CONV_DIM), F32)
    zeros_state = jnp.zeros((bp, DN_HEADS, DN_HEAD, DN_HEAD), F32)

    w_in_all = _permute_w_in(w_in)
    w_out_all = w_out.astype(BF16)
    for l in range(depth):
        g_l = g_norm[l].reshape(1, d)
        m_l = m_all[l]
        m3 = m_l.reshape(m_l.shape[0], 1, 3 * d)
        ms = jnp.repeat(m_l[bp:bp + bs], ls, axis=0)

        pp = _inproj(xp, g_l, m3, m3, w_in_all, l, rows_per_batch=lp, mod_row0=0)
        p3 = pp.reshape(bp, lp, NP)
        (ya,) = _mixa(p3, a_vnorm[l], a_ws[l], a_bs[l], emit_va=False)
        abt = jnp.transpose(p3[:, :, OFF["a"]:OFF["a"] + 2 * DN_HEADS], (0, 2, 1))
        yb, s_p = _delta(p3, abt, dn_conv_w[l], zeros_conv, zeros_state, dn_a_log[l], dn_dt_bias[l],
                         dn_onorm[l], t=_delta_t(lp), valid_len=_delta_t(lp))
        yc = _dsa_prompt(p3)
        xp = _outproj(ya.reshape(bp * lp, W_A), yb.reshape(bp * lp, W_B), yc.reshape(bp * lp, W_C),
                      w_out_all, l, xp, m3, rows_per_batch=lp, mod_row0=0)
        outs["pk"].append(p3[:, :, OFF["kc"]:OFF["kc"] + C_KV].reshape(bp, lp, C_KV_HEADS, C_HEAD))
        outs["pv"].append(p3[:, :, OFF["vc"]:OFF["vc"] + C_KV].reshape(bp, lp, C_KV_HEADS, C_HEAD))
        outs["pki"].append(p3[:, :, OFF_TAIL:OFF_TAIL + IDX_DIM])
        outs["pdn"].append(s_p)
        outs["pconv"].append(p3[:, lp - (CONV_K - 1):, OFF["qkv"]:OFF["qkv"] + DN_CONV_DIM])

        ps = _inproj(xs, g_l, ms[:, d:2 * d], ms[:, 0:d], w_in_all, l, rows_per_batch=ls, mod_row0=bp)
        p3s = ps.reshape(bs, ls, NP)
        ya_s, va_s = _mixa(p3s, a_vnorm[l], a_ws[l], a_bs[l], emit_va=True)
        p3s_pad = jnp.pad(p3s, ((0, 0), (0, ls_pad - ls), (0, 0)))
        abt_s = jnp.transpose(p3s_pad[:, :, OFF["a"]:OFF["a"] + 2 * DN_HEADS], (0, 2, 1))
        yb_s, s_s = _delta(p3s_pad, abt_s, dn_conv_w[l], state_conv[l], state_dn[l], dn_a_log[l],
                           dn_dt_bias[l], dn_onorm[l], t=ls_pad, valid_len=ls)
        yc_s = _dsa_sample(p3s, cache_k, cache_v, cache_kidx, page_table, l)
        xs = _outproj(ya_s.reshape(bs * ls, W_A), yb_s[:, :ls].reshape(bs * ls, W_B),
                      yc_s.reshape(bs * ls, W_C), w_out_all, l, xs, ms[:, 2 * d:3 * d],
                      rows_per_batch=ls, mod_row0=bp)
        outs["sk"].append(p3s[:, :, OFF["kc"]:OFF["kc"] + C_KV].reshape(bs, ls, C_KV_HEADS, C_HEAD))
        outs["sv"].append(p3s[:, :, OFF["vc"]:OFF["vc"] + C_KV].reshape(bs, ls, C_KV_HEADS, C_HEAD))
        outs["ski"].append(p3s[:, :, OFF_TAIL:OFF_TAIL + IDX_DIM])
        outs["sdn"].append(s_s)
        outs["sconv"].append(p3s[:, ls - (CONV_K - 1):, OFF["qkv"]:OFF["qkv"] + DN_CONV_DIM])
        outs["samlp"].append(va_s)

    y_prompt = _final_norm(xp, g_final).reshape(bp, lp, d)
    y_sample = _final_norm(xs, g_final).reshape(bs, ls, d)
    st = jnp.stack
    return (y_prompt, y_sample, st(outs["pk"]), st(outs["pv"]), st(outs["pki"]), st(outs["pdn"]),
            st(outs["pconv"]), st(outs["sk"]), st(outs["sv"]), st(outs["ski"]), st(outs["sdn"]),
            st(outs["sconv"]), st(outs["samlp"]))
```

```python
import functools

import jax
import jax.numpy as jnp
from jax import lax
from jax.experimental import pallas as pl
from jax.experimental.pallas import tpu as pltpu

F32 = jnp.float32
BF16 = jnp.bfloat16
I32 = jnp.int32
EPS = 1e-6
INT_MIN = -(2 ** 31)
NEG_INF = float("-inf")

D_MODEL = 4096
W_A = D_MODEL // 4
A_GROUP = 128
A_HEADS = W_A // A_GROUP
A_CHUNK = 128
W_B = D_MODEL // 2
DN_HEAD = 128
DN_HEADS = W_B // DN_HEAD
CONV_K = 4
DN_CONV_DIM = 3 * W_B
DN_CHUNK = 64
W_C = D_MODEL - W_A - W_B
C_HEAD = 128
C_HEADS = W_C // C_HEAD
C_KV_HEADS = 2
C_GROUPS = C_HEADS // C_KV_HEADS
C_KV = C_KV_HEADS * C_HEAD
IDX_HEADS = 16
IDX_DIM = 64
TOPK_MAX = 256
Q_BLOCK = 256

_SRC_SPLITS = (W_A, W_A, W_A, DN_CONV_DIM, W_B, DN_HEADS, DN_HEADS,
               W_C, C_KV, C_KV, W_C, IDX_HEADS * IDX_DIM, IDX_DIM, IDX_HEADS)
_SRC_NAMES = ("u", "v", "za", "qkv", "zb", "a", "b", "qc", "kc", "vc", "zc", "qi", "ki", "wi")
_SRC_OFF = {}
_o = 0
for _n, _w in zip(_SRC_NAMES, _SRC_SPLITS):
    _SRC_OFF[_n] = (_o, _w)
    _o += _w
D_IN = _o

_DST_ORDER = ("u", "v", "za", "qkv", "zb", "qc", "zc", "qi", "kc", "vc", "ki", "a", "b", "wi")
OFF = {}
_o = 0
for _n in _DST_ORDER:
    OFF[_n] = _o
    _o += _SRC_OFF[_n][1]
NP = ((_o + 127) // 128) * 128
OFF_TAIL = OFF["ki"]
TAIL_A = OFF["a"] - OFF_TAIL
TAIL_B = OFF["b"] - OFF_TAIL
TAIL_WI = OFF["wi"] - OFF_TAIL

MXU_COLUMNS = 256
INPROJ_TN = 6 * MXU_COLUMNS
DELTA_HEADS_PER_STEP = 8
DSA_KEY_TILE = 512

VMEM_CAPACITY_MIB = 64
VMEM_LIMIT_MIB = {
    "ada": 40, "w_in_layout": 32, "inproj": 56, "mixa": 32, "delta": 40, "dsa_prompt": 48,
    "dsa_s_score": 32, "dsa_s_select": 40, "dsa_s_attend": 40, "outproj": 48, "final_norm": 32,
    "kv_rows": 32,
}
assert max(VMEM_LIMIT_MIB.values()) < VMEM_CAPACITY_MIB


def _cp(name, n_axes):
    return pltpu.CompilerParams(dimension_semantics=("arbitrary",) * n_axes,
                                vmem_limit_bytes=VMEM_LIMIT_MIB[name] * 1024 * 1024)


def _silu(x):
    return (0.5 * x) * (1.0 + jnp.tanh(0.5 * x))


def _gelu(x):
    return 0.5 * x * (1.0 + jnp.tanh(0.7978845608028654 * (x + 0.044715 * (x * x * x))))


def _dot(a, b):
    return jnp.dot(a, b, preferred_element_type=F32)


def _dot_nt(a, b):
    return lax.dot_general(a, b, (((1,), (1,)), ((), ())), preferred_element_type=F32)


def _dot_tn(a, b):
    return lax.dot_general(a, b, (((0,), (0,)), ((), ())), preferred_element_type=F32)


def _b16(a):
    return a.astype(BF16)


def _ada_kernel(c_ref, w_ref, b_ref, o_ref):
    s = _silu(c_ref[...]).astype(BF16)
    o_ref[...] = _dot(s, w_ref[...].astype(BF16)) + b_ref[...]


def _ada(c_rows, w_ada, b_ada):
    depth, d, n = w_ada.shape
    r = c_rows.shape[0]
    tn = 512
    return pl.pallas_call(
        _ada_kernel,
        grid=(depth, n // tn),
        in_specs=[pl.BlockSpec((r, d), lambda l, j: (0, 0)),
                  pl.BlockSpec((None, d, tn), lambda l, j: (l, 0, j)),
                  pl.BlockSpec((None, 1, tn), lambda l, j: (l, 0, j))],
        out_specs=pl.BlockSpec((None, r, tn), lambda l, j: (l, 0, j)),
        out_shape=jax.ShapeDtypeStruct((depth, r, n), F32),
        compiler_params=_cp("ada", 2),
        name="ada",
    )(c_rows, w_ada, b_ada.reshape(depth, 1, n))


def _inproj_kernel(x_ref, g_ref, sc_ref, sh_ref, w_ref, o_ref, h_scr, *, rc):
    @pl.when(pl.program_id(1) == 0)
    def _():
        tm = x_ref.shape[0]
        per_row = sc_ref.shape[0] != 1

        def body(c, carry):
            r = pl.ds(pl.multiple_of(c * rc, rc), rc)
            x = x_ref[r, :]
            y = x * lax.rsqrt(jnp.mean(x * x, axis=-1, keepdims=True) + EPS)
            sc = sc_ref[r, :] if per_row else sc_ref[...]
            sh = sh_ref[r, :] if per_row else sh_ref[...]
            h_scr[r, :] = ((y * g_ref[...]) * (1.0 + sc) + sh).astype(BF16)
            return carry

        lax.fori_loop(0, tm // rc, body, 0)

    o_ref[...] = _dot_nt(h_scr[...], w_ref[...])


def _inproj(x2, g, sc, sh, w_bf16, layer, *, rows_per_batch, mod_row0):
    m, d = x2.shape
    n = w_bf16.shape[1]
    tn = INPROJ_TN
    if sc.ndim == 3:
        tm = 512
        assert rows_per_batch % tm == 0
        sc_spec = pl.BlockSpec((None, 1, d), lambda i, j: ((i * tm) // rows_per_batch + mod_row0, 0, 1))
        sh_spec = pl.BlockSpec((None, 1, d), lambda i, j: ((i * tm) // rows_per_batch + mod_row0, 0, 0))
        rc = 64
    else:
        tm = m
        sc_spec = pl.BlockSpec((tm, d), lambda i, j: (i, 0))
        sh_spec = pl.BlockSpec((tm, d), lambda i, j: (i, 0))
        rc = tm
    assert m % tm == 0
    return pl.pallas_call(
        functools.partial(_inproj_kernel, rc=rc),
        grid=(m // tm, pl.cdiv(n, tn)),
        in_specs=[pl.BlockSpec((tm, d), lambda i, j: (i, 0)),
                  pl.BlockSpec((1, d), lambda i, j: (0, 0)),
                  sc_spec, sh_spec,
                  pl.BlockSpec((None, tn, d), lambda i, j: (layer, j, 0))],
        out_specs=pl.BlockSpec((tm, tn), lambda i, j: (i, j)),
        out_shape=jax.ShapeDtypeStruct((m, n), F32),
        scratch_shapes=[pltpu.VMEM((tm, d), BF16)],
        compiler_params=_cp("inproj", 2),
        name="inproj",
    )(x2, g, sc, sh, w_bf16)


def _mixa_kernel(u_ref, v_ref, z_ref, vn_ref, ws_ref, bst_ref, y_ref, *rest, c, emit_va):
    u = _gelu(u_ref[...])
    v = _gelu(v_ref[...])
    mu = jnp.mean(v, axis=-1, keepdims=True)
    dv = v - mu
    va = dv * lax.rsqrt(jnp.mean(dv * dv, axis=-1, keepdims=True) + EPS) * vn_ref[...]
    if emit_va:
        rest[0][...] = va
    z = _silu(z_ref[...])
    row = lax.broadcasted_iota(I32, (c, c), 0)
    col = lax.broadcasted_iota(I32, (c, c), 1)
    tril = col <= row
    for h in range(A_HEADS):
        cols = slice(h * A_GROUP, (h + 1) * A_GROUP)
        wm = jnp.where(tril, ws_ref[h], 0.0)
        wm16 = wm.astype(BF16)
        for ci in range(u.shape[0] // c):
            rows = slice(ci * c, (ci + 1) * c)
            vh = va[rows, cols]
            if c >= 128:
                mixed = _dot(wm16, vh.astype(BF16))
            else:
                mixed = wm[:, 0:1] * vh[0:1, :]
                for s in range(1, c):
                    mixed = mixed + wm[:, s:s + 1] * vh[s:s + 1, :]
            mixed = mixed + bst_ref[:, h:h + 1]
            y_ref[rows, cols] = (u[rows, cols] * mixed * z[rows, cols]).astype(y_ref.dtype)


def _mixa(p3, a_vnorm, a_ws, a_bs, *, emit_va):
    b, l, _ = p3.shape
    c = min(A_CHUNK, l)
    r = 2 * c if l % (2 * c) == 0 else c
    n = l // r
    ws = a_ws[:, :c, :c]
    bst = a_bs[:, :c].T
    wblk = W_A
    outs = [jax.ShapeDtypeStruct((b, l, W_A), BF16)]
    out_specs = [pl.BlockSpec((None, r, W_A), lambda i, j: (i, j, 0))]
    if emit_va:
        outs.append(jax.ShapeDtypeStruct((b, l, W_A), F32))
        out_specs.append(pl.BlockSpec((None, r, W_A), lambda i, j: (i, j, 0)))
    res = pl.pallas_call(
        functools.partial(_mixa_kernel, c=c, emit_va=emit_va),
        grid=(b, n),
        in_specs=[pl.BlockSpec((None, r, wblk), lambda i, j: (i, j, OFF["u"] // wblk)),
                  pl.BlockSpec((None, r, wblk), lambda i, j: (i, j, OFF["v"] // wblk)),
                  pl.BlockSpec((None, r, wblk), lambda i, j: (i, j, OFF["za"] // wblk)),
                  pl.BlockSpec((1, W_A), lambda i, j: (0, 0)),
                  pl.BlockSpec((A_HEADS, c, c), lambda i, j: (0, 0, 0)),
                  pl.BlockSpec((c, A_HEADS), lambda i, j: (0, 0))],
        out_specs=out_specs,
        out_shape=outs,
        compiler_params=_cp("mixa", 2),
        name="mixa",
    )(p3, p3, p3, a_vnorm.reshape(1, W_A), ws, bst)
    return res


def _delta_kernel(alog_ref, dtb_ref,
                  q_ref, k_ref, v_ref, z_ref, ab_ref,
                  cwq_ref, cwk_ref, cwv_ref, cpq_ref, cpk_ref, cpv_ref,
                  s0_ref, on_ref,
                  y_ref, sout_ref,
                  xbuf, s_scr, *, t, c, hp, valid_len):
    n = pl.program_id(2)
    dh = DN_HEAD

    @pl.when(n == 0)
    def _():
        s_scr[...] = s0_ref[...]
        for j in range(hp):
            lanes = slice(j * dh, (j + 1) * dh)
            xbuf[3 * j + 0, 5:8, :] = cpq_ref[:, lanes]
            xbuf[3 * j + 1, 5:8, :] = cpk_ref[:, lanes]
            xbuf[3 * j + 2, 5:8, :] = cpv_ref[:, lanes]

    _delta_heads(alog_ref, dtb_ref, q_ref, k_ref, v_ref, z_ref, ab_ref, cwq_ref, cwk_ref, cwv_ref,
                 on_ref, y_ref, xbuf, s_scr, t=t, c=c, hp=hp, valid_len=valid_len)

    @pl.when(n == pl.num_programs(2) - 1)
    def _():
        sout_ref[...] = s_scr[...]


def _delta_heads(alog_ref, dtb_ref, q_ref, k_ref, v_ref, z_ref, ab_ref, cwq_ref, cwk_ref, cwv_ref,
                 on_ref, y_ref, xbuf, s_scr, *, t, c, hp, valid_len):
    hg = pl.program_id(1)
    dh = DN_HEAD
    heads = list(range(hp))

    row = lax.broadcasted_iota(I32, (t, t), 0)
    col = lax.broadcasted_iota(I32, (t, t), 1)
    shift = c.bit_length() - 1
    same = (row >> shift) == (col >> shift)
    eye = row == col
    incl = same & (col <= row)
    strict = same & (col < row)
    incl_t = same & (row <= col)
    blk8 = (row >> 3) == (col >> 3)
    off_masks = []
    bs = 8
    while bs < c:
        sh_b = bs.bit_length() - 1
        inner = (row >> sh_b) == (col >> sh_b)
        outer = (row >> (sh_b + 1)) == (col >> (sh_b + 1))
        off_masks.append(outer & jnp.logical_not(inner))
        bs *= 2
    eye_f = jnp.where(eye, 1.0, 0.0)
    if valid_len < t:
        lane_valid = lax.broadcasted_iota(I32, (1, t), 1) < valid_len
        sub_valid = lax.broadcasted_iota(I32, (t, 1), 0) < valid_len

    def to_col(r):
        return jnp.sum(jnp.where(eye, r, 0.0), axis=1, keepdims=True)

    def conv(idx, x, w_ref, lanes):
        xbuf[idx, 8:8 + t, :] = x
        y = xbuf[idx, pl.ds(5, t), :] * w_ref[0:1, lanes]
        for jj in range(1, CONV_K):
            y = y + xbuf[idx, pl.ds(5 + jj, t), :] * w_ref[jj:jj + 1, lanes]
        xbuf[idx, 5:8, :] = x[t - 3:t, :]
        return _silu(y)

    lanes_of = [slice(j * dh, (j + 1) * dh) for j in range(hp)]

    def per_head(f, *lists):
        return [f(*vals) for vals in zip(*lists)]

    def l2n(a):
        return a * lax.rsqrt(jnp.sum(a * a, axis=-1, keepdims=True) + EPS)

    qc = [l2n(conv(3 * j + 0, q_ref[:, lanes_of[j]], cwq_ref, lanes_of[j])) * (dh ** -0.5) for j in heads]
    kc = [l2n(conv(3 * j + 1, k_ref[:, lanes_of[j]], cwk_ref, lanes_of[j])) for j in heads]
    v = [conv(3 * j + 2, v_ref[:, lanes_of[j]], cwv_ref, lanes_of[j]) for j in heads]

    def gates(j):
        h = hg * hp + j
        a_row = ab_ref[pl.ds(h, 1), :]
        b_row = ab_ref[pl.ds(DN_HEADS + h, 1), :]
        xa = a_row + dtb_ref[h]
        softplus = jnp.maximum(xa, 0.0) + jnp.log(1.0 + jnp.exp(-jnp.abs(xa)))
        a_coef = jnp.exp(jnp.zeros((1, 1), F32) + alog_ref[h])
        g_row = -a_coef * softplus
        beta_row = 1.0 / (1.0 + jnp.exp(-b_row))
        if valid_len < t:
            g_row = jnp.where(lane_valid, g_row, 0.0)
            beta_row = jnp.where(lane_valid, beta_row, 0.0)
        return g_row, beta_row

    g_row, beta_row = zip(*[gates(j) for j in heads])
    if valid_len < t:
        kc = per_head(lambda a: jnp.where(sub_valid, a, 0.0), kc)
        v = per_head(lambda a: jnp.where(sub_valid, a, 0.0), v)

    g_col = per_head(to_col, g_row)
    beta_col = per_head(to_col, beta_row)
    gc_col = per_head(lambda r: jnp.sum(jnp.where(incl, r, 0.0), axis=1, keepdims=True), g_row)
    glast_col = per_head(lambda r: jnp.sum(jnp.where(same, r, 0.0), axis=1, keepdims=True), g_row)
    gc_row = per_head(lambda cl: jnp.sum(jnp.where(incl_t, cl, 0.0), axis=0, keepdims=True), g_col)
    kb = per_head(lambda a, b: a * b, kc, beta_col)
    kc16 = per_head(lambda a: a.astype(BF16), kc)

    def key_products(kb_, kc16_, qc_, gc, gr):
        decay = jnp.where(incl, jnp.exp(jnp.where(incl, gc - gr, 0.0)), 0.0)
        a_mat = jnp.where(strict, _dot_nt(kb_.astype(BF16), kc16_) * decay, 0.0)
        attn16 = (_dot_nt(qc_.astype(BF16), kc16_) * decay).astype(BF16)
        n0f = jnp.where(blk8, -a_mat, 0.0)
        return attn16, _b16(n0f), eye_f + n0f, [_b16(jnp.where(om, a_mat, 0.0)) for om in off_masks]

    attn, n0, x, a_offs = zip(*per_head(key_products, kb, kc16, qc, gc_col, gc_row))
    eg = per_head(jnp.exp, gc_col)
    rhs = per_head(lambda vv, bc, kk, e: _b16(jnp.concatenate([vv * bc, kk * e], axis=1)),
                   v, beta_col, kb, eg)
    n2 = per_head(lambda a: _b16(_dot(a, a)), n0)
    n4 = per_head(lambda a: _b16(_dot(a, a)), n2)
    x = per_head(lambda xx, nn: xx + _dot(_b16(xx), nn), x, n2)
    x = per_head(lambda xx, nn: xx + _dot(_b16(xx), nn), x, n4)
    for lvl in range(len(off_masks)):
        xs = per_head(_b16, x)
        xa_off = per_head(lambda s_, offs: _b16(_dot(s_, offs[lvl])), xs, a_offs)
        x = per_head(lambda xx, xo, s_: xx - _dot(xo, s_), x, xa_off, xs)
    sol16 = per_head(lambda xx, r: _b16(_dot(_b16(xx), r)), x, rhs)
    auw = per_head(_dot, attn, sol16)
    qw = per_head(lambda a, e, m_: (a * e - m_[:, dh:]).astype(BF16), qc, eg, auw)
    kg16 = per_head(lambda a, gl_, gc: (a * jnp.exp(gl_ - gc)).astype(BF16), kc, glast_col, gc_col)

    s = [s_scr[j] for j in heads]
    outs = [[] for _ in heads]
    for i in range(t // c):
        rows = slice(i * c, (i + 1) * c)
        s16 = per_head(lambda a: a.astype(BF16), s)
        kuw = per_head(lambda a, b: _dot_tn(a[rows], b[rows]), kg16, sol16)
        o_i = per_head(lambda a, b16, m_: _dot(a[rows], b16) + m_[rows, :dh], qw, s16, auw)
        for pos in range(len(heads)):
            outs[pos].append(o_i[pos])
        gl = per_head(lambda a: jnp.exp(a[i * c:i * c + 1, :]), glast_col)
        s = per_head(lambda g_, s_, m_, b16: g_ * s_ + m_[:, :dh] - _dot(m_[:, dh:].astype(BF16), b16),
                     gl, s, kuw, s16)
    for pos, j in enumerate(heads):
        s_scr[j] = s[pos]
        o = outs[pos][0] if len(outs[pos]) == 1 else jnp.concatenate(outs[pos], axis=0)
        on = o * lax.rsqrt(jnp.mean(o * o, axis=-1, keepdims=True) + EPS) * on_ref[...]
        y_ref[:, lanes_of[j]] = (on * _silu(z_ref[:, lanes_of[j]])).astype(y_ref.dtype)


def _delta(p3, abt, conv_w, conv_prev, s0, state_layer, a_log, dt_bias, onorm, *, t, valid_len):
    b, l, _ = p3.shape
    assert l % t == 0
    c = min(DN_CHUNK, t)
    hp = DELTA_HEADS_PER_STEP
    nh = DN_HEADS
    ng = nh // hp
    w = 128 * hp
    cb_q = OFF["qkv"] // w
    cb_z = OFF["zb"] // w
    assert OFF["qkv"] % w == 0 and OFF["zb"] % w == 0 and W_B % w == 0

    def pspec(cb0):
        return pl.BlockSpec((None, t, w), lambda i, h, n, a, d: (i, n, cb0 + h))

    def cwspec(sidx):
        return pl.BlockSpec((CONV_K, w), lambda i, h, n, a, d: (0, sidx * ng + h))

    def cpspec(sidx):
        return pl.BlockSpec((None, None, CONV_K - 1, w),
                            lambda i, h, n, a, d: (state_layer, i, 0, sidx * ng + h))

    grid_spec = pltpu.PrefetchScalarGridSpec(
        num_scalar_prefetch=2,
        grid=(b, ng, l // t),
        in_specs=[pspec(cb_q), pspec(cb_q + ng), pspec(cb_q + 2 * ng), pspec(cb_z),
                  pl.BlockSpec((None, 2 * nh, t), lambda i, h, n, a, d: (i, 0, n)),
                  cwspec(0), cwspec(1), cwspec(2), cpspec(0), cpspec(1), cpspec(2),
                  pl.BlockSpec((None, None, hp, DN_HEAD, DN_HEAD),
                               lambda i, h, n, a, d: (state_layer, i, h, 0, 0)),
                  pl.BlockSpec((1, DN_HEAD), lambda i, h, n, a, d: (0, 0))],
        out_specs=[pl.BlockSpec((None, t, w), lambda i, h, n, a, d: (i, n, h)),
                   pl.BlockSpec((None, hp, DN_HEAD, DN_HEAD), lambda i, h, n, a, d: (i, h, 0, 0))],
        scratch_shapes=[pltpu.VMEM((3 * hp, t + 8, 128), F32), pltpu.VMEM((hp, DN_HEAD, DN_HEAD), F32)],
    )
    return pl.pallas_call(
        functools.partial(_delta_kernel, t=t, c=c, hp=hp, valid_len=valid_len),
        grid_spec=grid_spec,
        out_shape=[jax.ShapeDtypeStruct((b, l, W_B), BF16),
                   jax.ShapeDtypeStruct((b, nh, DN_HEAD, DN_HEAD), F32)],
        compiler_params=_cp("delta", 3),
        name="delta",
    )(a_log, dt_bias, p3, p3, p3, p3, abt, conv_w, conv_w, conv_w,
      conv_prev, conv_prev, conv_prev, s0, onorm.reshape(1, DN_HEAD))


def _ordered_bits_to_float(u):
    key = u ^ jnp.int32(INT_MIN)
    bits = jnp.where(key < 0, key ^ jnp.int32(0x7FFFFFFF), key)
    return pltpu.bitcast(bits, F32)


def _topk_threshold(count_ge, shape, topk):
    def bit_body(i, u):
        bit = jnp.left_shift(jnp.int32(1), 31 - i)
        cand_u = u | bit
        cnt = count_ge(_ordered_bits_to_float(cand_u))
        return jnp.where(cnt >= float(topk), cand_u, u)

    u = lax.fori_loop(0, 32, bit_body, jnp.zeros(shape, I32))
    return _ordered_bits_to_float(u)


def _softmax_update(m_ref, l_ref, acc_ref, idx, s, v_tiles):
    m_old = m_ref[idx]
    m_new = jnp.maximum(m_old, jnp.max(s, axis=1, keepdims=True))
    m_safe = jnp.where(m_new == NEG_INF, 0.0, m_new)
    alpha = jnp.exp(m_old - m_safe)
    p = jnp.exp(s - m_safe)
    l_ref[idx] = alpha * l_ref[idx] + jnp.sum(p, axis=1, keepdims=True)
    acc = alpha * acc_ref[idx]
    p16 = p.astype(BF16)
    k0 = 0
    for vt in v_tiles:
        acc = acc + _dot(p16[:, k0:k0 + vt.shape[0]], vt)
        k0 += vt.shape[0]
    acc_ref[idx] = acc
    m_ref[idx] = m_new


def _dsa_prompt_kernel(qi_ref, tq_ref, tall_ref, qc_ref, k_ref, v_ref, z_ref, y_ref,
                       sc_scr, m_scr, l_scr, acc_scr, *, l, topk, kt):
    qb = pl.program_id(1)
    nq = Q_BLOCK
    n_tiles = (qb * nq + nq + kt - 1) // kt
    n_pairs = C_HEADS // 2
    w_rows = tq_ref[...].T[TAIL_WI:TAIL_WI + IDX_HEADS, :] * ((IDX_HEADS ** -0.5) * (IDX_DIM ** -0.5))
    rq = []
    for j in range(IDX_HEADS // 2):
        a = qi_ref[:, j * 128:(j + 1) * 128]
        rq.append(jnp.concatenate([a, pltpu.roll(a, IDX_DIM, 1)], axis=0).astype(BF16))
    kpos = lax.broadcasted_iota(I32, (kt, nq), 0)
    qpos = qb * nq + lax.broadcasted_iota(I32, (kt, nq), 1)
    lane = lax.broadcasted_iota(I32, (kt, 128), 1)

    def score_tile(ti, carry):
        k0 = pl.multiple_of(ti * kt, kt)
        ki = jnp.where(lane < IDX_DIM, tall_ref[pl.ds(k0, kt), :], 0.0).astype(BF16)
        score = jnp.zeros((kt, nq), F32)
        for j in range(IDX_HEADS // 2):
            lg = _dot_nt(ki, rq[j])
            score = score + w_rows[2 * j:2 * j + 1, :] * jnp.maximum(lg[:, :nq], 0.0)
            score = score + w_rows[2 * j + 1:2 * j + 2, :] * jnp.maximum(lg[:, nq:], 0.0)
        sc_scr[pl.ds(k0, kt), :] = jnp.where(kpos + k0 <= qpos, score, NEG_INF)
        return carry

    lax.fori_loop(0, n_tiles, score_tile, 0)

    ct = 256
    n_ct = (qb * nq + nq + ct - 1) // ct
    n_acc = 4

    def count(cand, strict):
        cand8 = jnp.broadcast_to(cand, (8, nq))

        def tile_body(ti, accs):
            k0 = pl.multiple_of(ti * ct, ct)
            tile = sc_scr[pl.ds(k0, ct), :]
            accs = list(accs)
            for r in range(ct // 8):
                blk = tile[8 * r:8 * r + 8, :]
                hit = (blk > cand8) if strict else (blk >= cand8)
                accs[r % n_acc] = accs[r % n_acc] + jnp.where(hit, 1.0, 0.0)
            return tuple(accs)

        accs = lax.fori_loop(0, n_ct, tile_body, tuple(jnp.zeros((8, nq), F32) for _ in range(n_acc)))
        acc = (accs[0] + accs[1]) + (accs[2] + accs[3])
        return jnp.sum(acc, axis=0, keepdims=True)

    thr = _topk_threshold(lambda cand: count(cand, False), (1, nq), topk)
    keep_all = (qb * nq + lax.broadcasted_iota(I32, (1, nq), 1)) < topk
    need = float(topk) - count(thr, True)
    tri16 = jnp.where(lax.broadcasted_iota(I32, (128, 128), 1) <= lax.broadcasted_iota(I32, (128, 128), 0),
                      1.0, 0.0).astype(BF16)

    def select_tile(ti, seen):
        k0 = pl.multiple_of(ti * kt, kt)
        sc = sc_scr[pl.ds(k0, kt), :]
        eq = sc == thr
        eq_f = jnp.where(eq, 1.0, 0.0)
        eq16 = eq_f.astype(BF16)
        prefs = [_dot(tri16, eq16[r:r + 128]) for r in range(0, kt, 128)]
        ranks = []
        for pref in prefs:
            ranks.append(seen + pref)
            seen = seen + pref[127:128, :]
        tie_ok = (jnp.concatenate(ranks, axis=0) - eq_f) < need
        chosen = jnp.where(sc > thr, 0.0, jnp.where(eq, jnp.where(tie_ok, 0.0, NEG_INF), NEG_INF))
        sc_scr[pl.ds(k0, kt), :] = jnp.where(keep_all, jnp.where(sc > NEG_INF, 0.0, NEG_INF), chosen)
        return seen

    lax.fori_loop(0, n_tiles, select_tile, jnp.zeros((1, nq), F32))

    m_scr[...] = jnp.full(m_scr.shape, NEG_INF, F32)
    l_scr[...] = jnp.zeros(l_scr.shape, F32)
    acc_scr[...] = jnp.zeros(acc_scr.shape, F32)
    scale = (C_HEAD ** -0.5) * 1.4426950408889634
    qp = []
    for pr in range(n_pairs):
        c0 = slice(2 * pr * C_HEAD, (2 * pr + 1) * C_HEAD)
        c1 = slice((2 * pr + 1) * C_HEAD, (2 * pr + 2) * C_HEAD)
        qp.append(jnp.concatenate([qc_ref[:, c0] * scale, qc_ref[:, c1] * scale], axis=0).astype(BF16))
    pairs_per_kv = n_pairs // C_KV_HEADS

    def attend_tile(ti, carry):
        k0 = pl.multiple_of(ti * kt, kt)
        bias = sc_scr[pl.ds(k0, kt), :]
        k16 = [k_ref[pl.ds(k0, kt), hk * C_HEAD:(hk + 1) * C_HEAD].astype(BF16) for hk in range(C_KV_HEADS)]
        vt16 = [v_ref[pl.ds(k0, kt), hk * C_HEAD:(hk + 1) * C_HEAD].T.astype(BF16) for hk in range(C_KV_HEADS)]
        bias2 = jnp.concatenate([bias, bias], axis=1)
        pr_all = range(n_pairs)
        s = [_dot_nt(k16[pr // pairs_per_kv], qp[pr]) + bias2 for pr in pr_all]
        m_old = [m_scr[pr] for pr in pr_all]
        m_new = [jnp.maximum(m_old[pr], jnp.max(s[pr], axis=0, keepdims=True)) for pr in pr_all]
        m_safe = [jnp.where(m_new[pr] == NEG_INF, 0.0, m_new[pr]) for pr in pr_all]
        alpha = [jnp.exp2(m_old[pr] - m_safe[pr]) for pr in pr_all]
        p = [jnp.exp2(s[pr] - m_safe[pr]) for pr in pr_all]
        for pr in pr_all:
            l_scr[pr] = alpha[pr] * l_scr[pr] + jnp.sum(p[pr], axis=0, keepdims=True)
            acc_scr[pr] = alpha[pr] * acc_scr[pr] + _dot(vt16[pr // pairs_per_kv], p[pr].astype(BF16))
            m_scr[pr] = m_new[pr]
        return carry

    lax.fori_loop(0, n_tiles, attend_tile, 0)
    for pr in range(n_pairs):
        o_t = acc_scr[pr] / l_scr[pr]
        for e in range(2):
            cols = slice((2 * pr + e) * C_HEAD, (2 * pr + e + 1) * C_HEAD)
            o = o_t[:, e * nq:(e + 1) * nq].T
            y_ref[:, cols] = (o * _silu(z_ref[:, cols])).astype(y_ref.dtype)


def _dsa_prompt(p3):
    b, l, _ = p3.shape
    assert l % Q_BLOCK == 0
    topk = min(TOPK_MAX, l // 4)
    nq = Q_BLOCK
    kt = DSA_KEY_TILE if l % DSA_KEY_TILE == 0 else 256
    assert l % kt == 0
    n_qb = l // nq
    return pl.pallas_call(
        functools.partial(_dsa_prompt_kernel, l=l, topk=topk, kt=kt),
        grid=(b, n_qb),
        in_specs=[pl.BlockSpec((None, nq, W_C), lambda i, j: (i, j, OFF["qi"] // W_C)),
                  pl.BlockSpec((None, nq, 128), lambda i, j: (i, j, OFF_TAIL // 128)),
                  pl.BlockSpec((None, l, 128), lambda i, j: (i, 0, OFF_TAIL // 128)),
                  pl.BlockSpec((None, nq, W_C), lambda i, j: (i, j, OFF["qc"] // W_C)),
                  pl.BlockSpec((None, l, C_KV), lambda i, j: (i, 0, OFF["kc"] // C_KV)),
                  pl.BlockSpec((None, l, C_KV), lambda i, j: (i, 0, OFF["vc"] // C_KV)),
                  pl.BlockSpec((None, nq, W_C), lambda i, j: (i, j, OFF["zc"] // W_C))],
        out_specs=pl.BlockSpec((None, nq, W_C), lambda i, j: (i, j, 0)),
        out_shape=jax.ShapeDtypeStruct((b, l, W_C), BF16),
        scratch_shapes=[pltpu.VMEM((l, nq), F32),
                        pltpu.VMEM((C_HEADS // 2, 1, 2 * nq), F32), pltpu.VMEM((C_HEADS // 2, 1, 2 * nq), F32),
                        pltpu.VMEM((C_HEADS // 2, C_HEAD, 2 * nq), F32)],
        compiler_params=_cp("dsa_prompt", 2),
        name="dsa_prompt",
    )(p3, p3, p3, p3, p3, p3, p3)


def _idx_scores(logits, wcol, t):
    r = jnp.maximum(logits, 0.0) * wcol
    return jnp.sum(r.reshape(t, IDX_HEADS, logits.shape[1]), axis=1)


def _dsa_s_score_kernel(pt_ref, q_ref, w_ref, kn_ref, *refs, pg, t):
    pages = refs[:pg]
    out_ref, new_ref = refs[pg:]
    q16 = q_ref[...].astype(BF16)
    wcol = w_ref[...] * ((IDX_HEADS ** -0.5) * (IDX_DIM ** -0.5))
    keys16 = jnp.concatenate([pages[i][...] for i in range(pg)], axis=1).astype(BF16)
    out_ref[...] = _idx_scores(_dot(q16, keys16), wcol, t)

    @pl.when(pl.program_id(1) == 0)
    def _():
        new_ref[...] = _idx_scores(_dot_nt(q16, kn_ref[...].astype(BF16)), wcol, t)


def _dsa_s_select_kernel(sc_ref, scn_ref, tq_ref, bias_ref, *, nk, topk):
    rows = sc_ref.shape[0]
    tq = tq_ref[...]
    new_ok = lax.broadcasted_iota(I32, (rows, 128), 1) <= tq
    sc_new = jnp.where(new_ok, scn_ref[...], NEG_INF)
    sc_past = sc_ref[...]

    def count_ge(cand):
        c1 = jnp.sum(jnp.where(sc_past >= cand, 1.0, 0.0), axis=1, keepdims=True)
        c2 = jnp.sum(jnp.where(sc_new >= cand, 1.0, 0.0), axis=1, keepdims=True)
        return c1 + c2

    thr = _topk_threshold(count_ge, (rows, 1), topk)
    keep_all = (nk + 1 + tq[:, 0:1]) <= topk
    n_gt = (jnp.sum(jnp.where(sc_past > thr, 1.0, 0.0), axis=1, keepdims=True)
            + jnp.sum(jnp.where(sc_new > thr, 1.0, 0.0), axis=1, keepdims=True))
    need = float(topk) - n_gt
    tri16 = jnp.where(lax.broadcasted_iota(I32, (128, 128), 0) <= lax.broadcasted_iota(I32, (128, 128), 1),
                      1.0, 0.0).astype(BF16)

    def bias_blocks(scs, seen):
        eqs = [sc == thr for sc in scs]
        eq_fs = [jnp.where(eq, 1.0, 0.0) for eq in eqs]
        prefs = [_dot(eq_f.astype(BF16), tri16) for eq_f in eq_fs]
        out = []
        for sc, eq, eq_f, pref in zip(scs, eqs, eq_fs, prefs):
            tie_ok = (seen + pref - eq_f) < need
            chosen = jnp.where(sc > thr, 0.0, jnp.where(eq, jnp.where(tie_ok, 0.0, NEG_INF), NEG_INF))
            out.append(jnp.where(keep_all, jnp.where(sc > NEG_INF, 0.0, NEG_INF), chosen))
            seen = seen + pref[:, 127:128]
        return out, seen

    n_blk = nk // 128
    grp = 8 if n_blk % 8 == 0 else 1

    def past_group(gi, seen):
        lanes = [pl.ds(pl.multiple_of((gi * grp + i) * 128, 128), 128) for i in range(grp)]
        out, seen = bias_blocks([sc_ref[:, ln] for ln in lanes], seen)
        for ln, bias in zip(lanes, out):
            bias_ref[:, ln] = bias
        return seen

    seen = lax.fori_loop(0, n_blk // grp, past_group, jnp.zeros((rows, 1), F32))
    (bias_new,), _ = bias_blocks([sc_new], seen)
    bias_ref[:, nk:] = bias_new


def _dsa_s_attend_kernel(pt_ref, q_ref, bias_ref, biasn_ref, kn_ref, vn_ref, z_ref, *refs, pg):
    kpages = refs[:pg]
    vpages = refs[pg:2 * pg]
    o_ref = refs[2 * pg]
    m_scr, l_scr, acc_scr = refs[2 * pg + 1:]
    g = pl.program_id(1)
    scale = C_HEAD ** -0.5
    page = kpages[0].shape[0] // C_KV_HEADS

    @pl.when(g == 0)
    def _():
        m_scr[...] = jnp.full(m_scr.shape, NEG_INF, F32)
        l_scr[...] = jnp.zeros(l_scr.shape, F32)
        acc_scr[...] = jnp.zeros(acc_scr.shape, F32)
        for hk in range(C_KV_HEADS):
            cols = slice(hk * C_HEAD, (hk + 1) * C_HEAD)
            q16 = (q_ref[hk] * scale).astype(BF16)
            s = _dot_nt(q16, kn_ref[:, cols].astype(BF16)) + biasn_ref[...]
            _softmax_update(m_scr, l_scr, acc_scr, hk, s, [vn_ref[:, cols].astype(BF16)])

    def two_pages(refs_, i, rows):
        return jnp.concatenate([refs_[i][rows, :], refs_[i + 1][rows, :]], axis=0).astype(BF16)

    for hk in range(C_KV_HEADS):
        q16 = (q_ref[hk] * scale).astype(BF16)
        rows = pl.ds(hk, page, stride=C_KV_HEADS)
        tiles = [_dot_nt(q16, two_pages(kpages, i, rows)) for i in range(0, pg, 2)]
        s = jnp.concatenate(tiles, axis=1) + bias_ref[...]
        _softmax_update(m_scr, l_scr, acc_scr, hk, s, [two_pages(vpages, i, rows) for i in range(0, pg, 2)])

    @pl.when(g == pl.num_programs(1) - 1)
    def _():
        for hk in range(C_KV_HEADS):
            o_ref[hk] = (acc_scr[hk] / l_scr[hk]) * _silu(z_ref[hk])


def _dsa_sample(p3s, cache_k, cache_v, cache_kidx, page_table, layer):
    b, t, _ = p3s.shape
    n_pages = page_table.shape[1]
    page = cache_k.shape[2]
    assert page == 128
    past = n_pages * page
    topk = min(TOPK_MAX, (past + t) // 4)
    pg = 32 if n_pages % 32 == 0 else (16 if n_pages % 16 == 0 else n_pages)
    assert pg % 2 == 0
    ng = n_pages // pg
    rows = t * C_GROUPS

    qi = p3s[:, :, OFF["qi"]:OFF["qi"] + IDX_HEADS * IDX_DIM].reshape(b, t * IDX_HEADS, IDX_DIM)
    wi = p3s[:, :, OFF_TAIL + TAIL_WI:OFF_TAIL + TAIL_WI + IDX_HEADS].reshape(b, t * IDX_HEADS, 1)
    ki_new = jnp.pad(p3s[:, :, OFF_TAIL:OFF_TAIL + IDX_DIM], ((0, 0), (0, 128 - t), (0, 0)))
    k_new = jnp.pad(p3s[:, :, OFF["kc"]:OFF["kc"] + C_KV], ((0, 0), (0, 128 - t), (0, 0)))
    v_new = jnp.pad(p3s[:, :, OFF["vc"]:OFF["vc"] + C_KV], ((0, 0), (0, 128 - t), (0, 0)))

    def heads_major(a):
        a = a.reshape(b, t, C_KV_HEADS, C_GROUPS, C_HEAD)
        return jnp.transpose(a, (0, 2, 1, 3, 4)).reshape(b, C_KV_HEADS, rows, C_HEAD)

    qh = heads_major(p3s[:, :, OFF["qc"]:OFF["qc"] + W_C])
    zh = heads_major(p3s[:, :, OFF["zc"]:OFF["zc"] + W_C])

    kidx_t = jnp.swapaxes(cache_kidx, 2, 3)
    score_spec = pltpu.PrefetchScalarGridSpec(
        num_scalar_prefetch=1,
        grid=(b, ng),
        in_specs=[pl.BlockSpec((None, t * IDX_HEADS, IDX_DIM), lambda i, g, pt: (i, 0, 0)),
                  pl.BlockSpec((None, t * IDX_HEADS, 1), lambda i, g, pt: (i, 0, 0)),
                  pl.BlockSpec((None, 128, IDX_DIM), lambda i, g, pt: (i, 0, 0))]
                 + [pl.BlockSpec((None, None, IDX_DIM, page),
                                 lambda i, g, pt, j=j: (layer, pt[i, g * pg + j], 0, 0)) for j in range(pg)],
        out_specs=[pl.BlockSpec((None, t, pg * page), lambda i, g, pt: (i, 0, g)),
                   pl.BlockSpec((None, t, 128), lambda i, g, pt: (i, 0, 0))],
    )
    scores, scores_new = pl.pallas_call(
        functools.partial(_dsa_s_score_kernel, pg=pg, t=t),
        grid_spec=score_spec,
        out_shape=[jax.ShapeDtypeStruct((b, t, past), F32), jax.ShapeDtypeStruct((b, t, 128), F32)],
        compiler_params=_cp("dsa_s_score", 2),
        name="dsa_s_score",
    )(page_table, qi, wi, ki_new, *([kidx_t] * pg))

    tq = jnp.broadcast_to(jnp.tile(jnp.arange(t, dtype=I32), b)[:, None], (b * t, 128))
    bias = pl.pallas_call(
        functools.partial(_dsa_s_select_kernel, nk=past, topk=topk),
        grid=(1,),
        in_specs=[pl.BlockSpec((b * t, past), lambda i: (0, 0)),
                  pl.BlockSpec((b * t, 128), lambda i: (0, 0)),
                  pl.BlockSpec((b * t, 128), lambda i: (0, 0))],
        out_specs=pl.BlockSpec((b * t, past + 128), lambda i: (0, 0)),
        out_shape=jax.ShapeDtypeStruct((b * t, past + 128), F32),
        compiler_params=_cp("dsa_s_select", 1),
        name="dsa_s_select",
    )(scores.reshape(b * t, past), scores_new.reshape(b * t, 128), tq).reshape(b, t, past + 128)

    bias_rows = jnp.repeat(bias, C_GROUPS, axis=1)
    bias_past = bias_rows[:, :, :past]
    bias_new = bias_rows[:, :, past:]

    ck = cache_k.reshape(cache_k.shape[0], cache_k.shape[1], page * C_KV_HEADS, C_HEAD)
    cv = cache_v.reshape(cache_v.shape[0], cache_v.shape[1], page * C_KV_HEADS, C_HEAD)
    pspec = [pl.BlockSpec((None, None, page * C_KV_HEADS, C_HEAD),
                          lambda i, g, pt, j=j: (layer, pt[i, g * pg + j], 0, 0)) for j in range(pg)]
    attend_spec = pltpu.PrefetchScalarGridSpec(
        num_scalar_prefetch=1,
        grid=(b, ng),
        in_specs=[pl.BlockSpec((None, C_KV_HEADS, rows, C_HEAD), lambda i, g, pt: (i, 0, 0, 0)),
                  pl.BlockSpec((None, rows, pg * page), lambda i, g, pt: (i, 0, g)),
                  pl.BlockSpec((None, rows, 128), lambda i, g, pt: (i, 0, 0)),
                  pl.BlockSpec((None, 128, C_KV), lambda i, g, pt: (i, 0, 0)),
                  pl.BlockSpec((None, 128, C_KV), lambda i, g, pt: (i, 0, 0)),
                  pl.BlockSpec((None, C_KV_HEADS, rows, C_HEAD), lambda i, g, pt: (i, 0, 0, 0))]
                 + pspec + pspec,
        out_specs=pl.BlockSpec((None, C_KV_HEADS, rows, C_HEAD), lambda i, g, pt: (i, 0, 0, 0)),
        scratch_shapes=[pltpu.VMEM((C_KV_HEADS, rows, 1), F32), pltpu.VMEM((C_KV_HEADS, rows, 1), F32),
                        pltpu.VMEM((C_KV_HEADS, rows, C_HEAD), F32)],
    )
    oh = pl.pallas_call(
        functools.partial(_dsa_s_attend_kernel, pg=pg),
        grid_spec=attend_spec,
        out_shape=jax.ShapeDtypeStruct((b, C_KV_HEADS, rows, C_HEAD), F32),
        compiler_params=_cp("dsa_s_attend", 2),
        name="dsa_s_attend",
    )(page_table, qh, bias_past, bias_new, k_new, v_new, zh, *([ck] * pg), *([cv] * pg))
    y = jnp.transpose(oh.reshape(b, C_KV_HEADS, t, C_GROUPS, C_HEAD), (0, 2, 1, 3, 4)).reshape(b, t, W_C)
    return y.astype(BF16)


def _outproj_kernel(ya_ref, yb_ref, yc_ref, w_ref, x_ref, gate_ref, o_ref, mix_scr):
    @pl.when(pl.program_id(1) == 0)
    def _():
        mix_scr[:, 0:W_A] = ya_ref[...]
        mix_scr[:, W_A:W_A + W_B] = yb_ref[...]
        mix_scr[:, W_A + W_B:] = yc_ref[...]

    o_ref[...] = x_ref[...] + gate_ref[...] * _dot(mix_scr[...], w_ref[...])


def _outproj(ya, yb, yc, w_bf16, layer, x2, gate, *, rows_per_batch, mod_row0):
    m, d = x2.shape
    tn = 1024
    if gate.ndim == 3:
        tm = 512
        gate_spec = pl.BlockSpec((None, 1, tn),
                                 lambda i, j: ((i * tm) // rows_per_batch + mod_row0, 0, 2 * (d // tn) + j))
    else:
        tm = m
        gate_spec = pl.BlockSpec((tm, tn), lambda i, j: (i, j))
    assert m % tm == 0
    return pl.pallas_call(
        _outproj_kernel,
        grid=(m // tm, d // tn),
        in_specs=[pl.BlockSpec((tm, W_A), lambda i, j: (i, 0)),
                  pl.BlockSpec((tm, W_B), lambda i, j: (i, 0)),
                  pl.BlockSpec((tm, W_C), lambda i, j: (i, 0)),
                  pl.BlockSpec((None, d, tn), lambda i, j: (layer, 0, j)),
                  pl.BlockSpec((tm, tn), lambda i, j: (i, j)),
                  gate_spec],
        out_specs=pl.BlockSpec((tm, tn), lambda i, j: (i, j)),
        out_shape=jax.ShapeDtypeStruct((m, d), F32),
        scratch_shapes=[pltpu.VMEM((tm, d), BF16)],
        compiler_params=_cp("outproj", 2),
        name="outproj",
    )(ya, yb, yc, w_bf16, x2, gate)


def _kv_rows_kernel(*refs, depth, tm):
    k_refs, v_refs = refs[:depth], refs[depth:2 * depth]
    pk_ref, pv_ref = refs[2 * depth:]
    for lyr in range(depth):
        for h in range(C_KV_HEADS):
            cols = slice(h * C_HEAD, (h + 1) * C_HEAD)
            rows = pl.ds(h, tm, stride=C_KV_HEADS)
            pk_ref.at[lyr][rows, :] = k_refs[lyr][:, cols]
            pv_ref.at[lyr][rows, :] = v_refs[lyr][:, cols]


def _kv_rows(p3_layers):
    depth = len(p3_layers)
    b, l, _ = p3_layers[0].shape
    tm = 512 if l % 512 == 0 else l
    kspec = pl.BlockSpec((None, tm, C_KV), lambda i, j: (i, j, OFF["kc"] // C_KV))
    vspec = pl.BlockSpec((None, tm, C_KV), lambda i, j: (i, j, OFF["vc"] // C_KV))
    ospec = pl.BlockSpec((depth, None, tm * C_KV_HEADS, C_HEAD), lambda i, j: (0, i, j, 0))
    oshape = jax.ShapeDtypeStruct((depth, b, l * C_KV_HEADS, C_HEAD), F32)
    pk, pv = pl.pallas_call(
        functools.partial(_kv_rows_kernel, depth=depth, tm=tm),
        grid=(b, l // tm),
        in_specs=[kspec] * depth + [vspec] * depth,
        out_specs=[ospec, ospec],
        out_shape=[oshape, oshape],
        compiler_params=_cp("kv_rows", 2),
        name="kv_rows",
    )(*p3_layers, *p3_layers)
    shape = (depth, b, l, C_KV_HEADS, C_HEAD)
    return pk.reshape(shape), pv.reshape(shape)


def _final_norm_kernel(x_ref, g_ref, o_ref):
    x = x_ref[...]
    o_ref[...] = x * lax.rsqrt(jnp.mean(x * x, axis=-1, keepdims=True) + EPS) * g_ref[...]


def _final_norm(x2, g):
    m, d = x2.shape
    tm = 256 if m % 256 == 0 else m
    return pl.pallas_call(
        _final_norm_kernel,
        grid=(m // tm,),
        in_specs=[pl.BlockSpec((tm, d), lambda i: (i, 0)), pl.BlockSpec((1, d), lambda i: (0, 0))],
        out_specs=pl.BlockSpec((tm, d), lambda i: (i, 0)),
        out_shape=jax.ShapeDtypeStruct((m, d), F32),
        compiler_params=_cp("final_norm", 1),
        name="final_norm",
    )(x2, g.reshape(1, d))


def _w_in_tile_table():
    n_t = NP // 128
    table, n_plain = [], None
    for dt in range(n_t - 1):
        c = dt * 128
        seg = [n for n in _DST_ORDER if OFF[n] <= c < OFF[n] + _SRC_OFF[n][1]][0]
        src = _SRC_OFF[seg][0] + (c - OFF[seg])
        if src % 128 == 0:
            assert n_plain is None
        else:
            assert src % 128 == 32
            if n_plain is None:
                n_plain = dt
        table.append(src // 128)
    t_kw, t_ab = _SRC_OFF["ki"][0] // 128, _SRC_OFF["a"][0] // 128
    assert _SRC_OFF["ki"][0] % 128 == 32 and _SRC_OFF["wi"][0] == t_kw * 128 + TAIL_WI
    assert _SRC_OFF["a"][0] % 128 == 0 and _SRC_OFF["b"][0] == _SRC_OFF["a"][0] + DN_HEADS
    assert TAIL_A == IDX_DIM and TAIL_B == TAIL_A + DN_HEADS and TAIL_WI == TAIL_B + DN_HEADS
    tile_a = table + [t_kw]
    rows_b = [t_ab * 4] * n_plain + [(s + 1) * 4 for s in table[n_plain:]] + [t_ab * 4]
    return tile_a, rows_b, n_plain


def _wprep_kernel(ta_ref, tb_ref, a_ref, b_ref, o_ref, *, n_plain, n_t):
    dt = pl.program_id(1)
    dt_o = o_ref.dtype

    @pl.when(dt < n_plain)
    def _():
        o_ref[...] = a_ref[...].astype(dt_o)

    @pl.when((dt >= n_plain) & (dt < n_t - 1))
    def _():
        o_ref[0:96, :] = a_ref[32:128, :].astype(dt_o)
        o_ref[96:128, :] = b_ref[...].astype(dt_o)

    @pl.when(dt == n_t - 1)
    def _():
        o_ref[0:TAIL_A, :] = a_ref[32:32 + IDX_DIM, :].astype(dt_o)
        o_ref[TAIL_A:TAIL_WI, :] = b_ref[...].astype(dt_o)
        o_ref[TAIL_WI:TAIL_WI + IDX_HEADS, :] = a_ref[TAIL_WI:TAIL_WI + IDX_HEADS, :].astype(dt_o)
        o_ref[TAIL_WI + IDX_HEADS:, :] = jnp.zeros((128 - TAIL_WI - IDX_HEADS, o_ref.shape[1]), dt_o)


def _permute_w_in(w_in):
    depth, d, _ = w_in.shape
    w_t = jnp.swapaxes(w_in, 1, 2)
    tile_a, rows_b, n_plain = _w_in_tile_table()
    n_t = NP // 128
    grid_spec = pltpu.PrefetchScalarGridSpec(
        num_scalar_prefetch=2,
        grid=(depth, n_t),
        in_specs=[pl.BlockSpec((None, 128, d), lambda l, t, ta, tb: (l, ta[t], 0)),
                  pl.BlockSpec((None, 32, d), lambda l, t, ta, tb: (l, tb[t], 0))],
        out_specs=pl.BlockSpec((None, 128, d), lambda l, t, ta, tb: (l, t, 0)),
    )
    return pl.pallas_call(
        functools.partial(_wprep_kernel, n_plain=n_plain, n_t=n_t),
        grid_spec=grid_spec,
        out_shape=jax.ShapeDtypeStruct((depth, NP, d), BF16),
        compiler_params=_cp("w_in_layout", 2),
        name="w_in_layout",
    )(jnp.asarray(tile_a, I32), jnp.asarray(rows_b, I32), w_t, w_t)


def _delta_t(l):
    for t in (128, 64):
        if l % t == 0:
            return t
    raise ValueError("sequence length must be a multiple of 64")


def kernel(x_prompt, x_sample, cache_k, cache_v, cache_kidx, state_dn, state_conv, page_table,
           c_prompt, c_sample, w_ada, b_ada, g_norm, w_in, a_vnorm, a_ws, a_bs, dn_conv_w,
           dn_a_log, dn_dt_bias, dn_onorm, w_out, g_final):
    bp, lp, d = x_prompt.shape
    bs, ls, _ = x_sample.shape
    depth = w_ada.shape[0]
    assert d == D_MODEL and w_in.shape[2] == D_IN
    assert CONV_K - 1 <= ls <= DN_CHUNK

    n_c = bp + bs
    c_rows = jnp.concatenate([c_prompt, c_sample], axis=0)
    r_pad = (-n_c) % 8
    if r_pad:
        c_rows = jnp.pad(c_rows, ((0, r_pad), (0, 0)))
    m_all = _ada(c_rows, w_ada, b_ada)

    xp = x_prompt.reshape(bp * lp, d)
    xs = x_sample.reshape(bs * ls, d)
    ls_pad = 8 * ((ls + 7) // 8)
    outs = {k: [] for k in ("p3", "pki", "pdn", "pconv", "sk", "sv", "ski", "sdn", "sconv", "samlp")}
    zeros_conv = jnp.zeros((1, bp, CONV_K - 1, DN_CONV_DIM), F32)
    zeros_state = jnp.zeros((1, bp, DN_HEADS, DN_HEAD, DN_HEAD), F32)

    w_in_all = _permute_w_in(w_in)
    w_out_all = w_out.astype(BF16)
    for l in range(depth):
        g_l = g_norm[l].reshape(1, d)
        m_l = m_all[l]
        m3 = m_l.reshape(m_l.shape[0], 1, 3 * d)
        ms = jnp.repeat(m_l[bp:bp + bs], ls, axis=0)

        pp = _inproj(xp, g_l, m3, m3, w_in_all, l, rows_per_batch=lp, mod_row0=0)
        p3 = pp.reshape(bp, lp, NP)
        (ya,) = _mixa(p3, a_vnorm[l], a_ws[l], a_bs[l], emit_va=False)
        abt = jnp.transpose(p3[:, :, OFF["a"]:OFF["a"] + 2 * DN_HEADS], (0, 2, 1))
        yb, s_p = _delta(p3, abt, dn_conv_w[l], zeros_conv, zeros_state, 0, dn_a_log[l], dn_dt_bias[l],
                         dn_onorm[l], t=_delta_t(lp), valid_len=_delta_t(lp))
        yc = _dsa_prompt(p3)
        xp = _outproj(ya.reshape(bp * lp, W_A), yb.reshape(bp * lp, W_B), yc.reshape(bp * lp, W_C),
                      w_out_all, l, xp, m3, rows_per_batch=lp, mod_row0=0)
        outs["p3"].append(p3)
        outs["pki"].append(p3[:, :, OFF_TAIL:OFF_TAIL + IDX_DIM])
        outs["pdn"].append(s_p)
        outs["pconv"].append(p3[:, lp - (CONV_K - 1):, OFF["qkv"]:OFF["qkv"] + DN_CONV_DIM])

        ps = _inproj(xs, g_l, ms[:, d:2 * d], ms[:, 0:d], w_in_all, l, rows_per_batch=ls, mod_row0=bp)
        p3s = ps.reshape(bs, ls, NP)
        ya_s, va_s = _mixa(p3s, a_vnorm[l], a_ws[l], a_bs[l], emit_va=True)
        p3s_pad = jnp.pad(p3s, ((0, 0), (0, ls_pad - ls), (0, 0)))
        abt_s = jnp.transpose(p3s_pad[:, :, OFF["a"]:OFF["a"] + 2 * DN_HEADS], (0, 2, 1))
        yb_s, s_s = _delta(p3s_pad, abt_s, dn_conv_w[l], state_conv, state_dn, l, dn_a_log[l],
                           dn_dt_bias[l], dn_onorm[l], t=ls_pad, valid_len=ls)
        yc_s = _dsa_sample(p3s, cache_k, cache_v, cache_kidx, page_table, l)
        xs = _outproj(ya_s.reshape(bs * ls, W_A), yb_s[:, :ls].reshape(bs * ls, W_B),
                      yc_s.reshape(bs * ls, W_C), w_out_all, l, xs, ms[:, 2 * d:3 * d],
                      rows_per_batch=ls, mod_row0=bp)
        outs["sk"].append(p3s[:, :, OFF["kc"]:OFF["kc"] + C_KV].reshape(bs, ls, C_KV_HEADS, C_HEAD))
        outs["sv"].append(p3s[:, :, OFF["vc"]:OFF["vc"] + C_KV].reshape(bs, ls, C_KV_HEADS, C_HEAD))
        outs["ski"].append(p3s[:, :, OFF_TAIL:OFF_TAIL + IDX_DIM])
        outs["sdn"].append(s_s)
        outs["sconv"].append(p3s[:, ls - (CONV_K - 1):, OFF["qkv"]:OFF["qkv"] + DN_CONV_DIM])
        outs["samlp"].append(va_s)

    y_prompt = _final_norm(xp, g_final).reshape(bp, lp, d)
    y_sample = _final_norm(xs, g_final).reshape(bs, ls, d)
    st = jnp.stack
    p_k, p_v = _kv_rows(outs["p3"])
    return (y_prompt, y_sample, p_k, p_v, st(outs["pki"]), st(outs["pdn"]),
            st(outs["pconv"]), st(outs["sk"]), st(outs["sv"]), st(outs["ski"]), st(outs["sdn"]),
            st(outs["sconv"]), st(outs["samlp"]))
```

```python
import functools

import jax
import jax.numpy as jnp
from jax import lax
from jax.experimental import pallas as pl
from jax.experimental.pallas import tpu as pltpu

F32 = jnp.float32
BF16 = jnp.bfloat16
I32 = jnp.int32
EPS = 1e-6
INT_MIN = -(2 ** 31)
NEG_INF = float("-inf")

D_MODEL = 4096
W_A = D_MODEL // 4
A_GROUP = 128
A_HEADS = W_A // A_GROUP
A_CHUNK = 128
W_B = D_MODEL // 2
DN_HEAD = 128
DN_HEADS = W_B // DN_HEAD
CONV_K = 4
DN_CONV_DIM = 3 * W_B
DN_CHUNK = 64
W_C = D_MODEL - W_A - W_B
C_HEAD = 128
C_HEADS = W_C // C_HEAD
C_KV_HEADS = 2
C_GROUPS = C_HEADS // C_KV_HEADS
C_KV = C_KV_HEADS * C_HEAD
IDX_HEADS = 16
IDX_DIM = 64
TOPK_MAX = 256
Q_BLOCK = 256

_SRC_SPLITS = (W_A, W_A, W_A, DN_CONV_DIM, W_B, DN_HEADS, DN_HEADS,
               W_C, C_KV, C_KV, W_C, IDX_HEADS * IDX_DIM, IDX_DIM, IDX_HEADS)
_SRC_NAMES = ("u", "v", "za", "qkv", "zb", "a", "b", "qc", "kc", "vc", "zc", "qi", "ki", "wi")
_SRC_OFF = {}
_o = 0
for _n, _w in zip(_SRC_NAMES, _SRC_SPLITS):
    _SRC_OFF[_n] = (_o, _w)
    _o += _w
D_IN = _o

_DST_ORDER = ("u", "v", "za", "qkv", "zb", "qc", "zc", "qi", "kc", "vc", "ki", "a", "b", "wi")
OFF = {}
_o = 0
for _n in _DST_ORDER:
    OFF[_n] = _o
    _o += _SRC_OFF[_n][1]
NP = ((_o + 127) // 128) * 128
OFF_TAIL = OFF["ki"]
TAIL_A = OFF["a"] - OFF_TAIL
TAIL_B = OFF["b"] - OFF_TAIL
TAIL_WI = OFF["wi"] - OFF_TAIL

MXU_COLUMNS = 256
INPROJ_TN = 6 * MXU_COLUMNS
DELTA_HEADS_PER_STEP = 8
DSA_KEY_TILE = 512

VMEM_CAPACITY_MIB = 64
VMEM_LIMIT_MIB = {
    "ada": 40, "w_in_layout": 32, "inproj": 56, "mixa": 32, "delta": 40, "dsa_prompt": 48,
    "dsa_s_score": 32, "dsa_s_select": 40, "dsa_s_attend": 40, "outproj": 48, "outproj_norm": 60,
    "kv_rows": 32,
}
assert max(VMEM_LIMIT_MIB.values()) < VMEM_CAPACITY_MIB


def _cp(name, n_axes):
    return pltpu.CompilerParams(dimension_semantics=("arbitrary",) * n_axes,
                                vmem_limit_bytes=VMEM_LIMIT_MIB[name] * 1024 * 1024)


def _silu(x):
    return (0.5 * x) * (1.0 + jnp.tanh(0.5 * x))


def _gelu(x):
    return 0.5 * x * (1.0 + jnp.tanh(0.7978845608028654 * (x + 0.044715 * (x * x * x))))


def _dot(a, b):
    return jnp.dot(a, b, preferred_element_type=F32)


def _dot_nt(a, b):
    return lax.dot_general(a, b, (((1,), (1,)), ((), ())), preferred_element_type=F32)


def _dot_tn(a, b):
    return lax.dot_general(a, b, (((0,), (0,)), ((), ())), preferred_element_type=F32)


def _b16(a):
    return a.astype(BF16)


def _ada_kernel(c_ref, w_ref, b_ref, o_ref):
    s = _silu(c_ref[...]).astype(BF16)
    o_ref[...] = _dot(s, w_ref[...].astype(BF16)) + b_ref[...]


def _ada(c_rows, w_ada, b_ada):
    depth, d, n = w_ada.shape
    r = c_rows.shape[0]
    tn = 512
    return pl.pallas_call(
        _ada_kernel,
        grid=(depth, n // tn),
        in_specs=[pl.BlockSpec((r, d), lambda l, j: (0, 0)),
                  pl.BlockSpec((None, d, tn), lambda l, j: (l, 0, j)),
                  pl.BlockSpec((None, 1, tn), lambda l, j: (l, 0, j))],
        out_specs=pl.BlockSpec((None, r, tn), lambda l, j: (l, 0, j)),
        out_shape=jax.ShapeDtypeStruct((depth, r, n), F32),
        compiler_params=_cp("ada", 2),
        name="ada",
    )(c_rows, w_ada, b_ada.reshape(depth, 1, n))


def _inproj_kernel(x_ref, g_ref, sc_ref, sh_ref, w_ref, o_ref, h_scr, *, rc):
    @pl.when(pl.program_id(1) == 0)
    def _():
        tm = x_ref.shape[0]
        per_row = sc_ref.shape[0] != 1

        def body(c, carry):
            r = pl.ds(pl.multiple_of(c * rc, rc), rc)
            x = x_ref[r, :]
            y = x * lax.rsqrt(jnp.mean(x * x, axis=-1, keepdims=True) + EPS)
            sc = sc_ref[r, :] if per_row else sc_ref[...]
            sh = sh_ref[r, :] if per_row else sh_ref[...]
            h_scr[r, :] = ((y * g_ref[...]) * (1.0 + sc) + sh).astype(BF16)
            return carry

        lax.fori_loop(0, tm // rc, body, 0)

    o_ref[...] = _dot_nt(h_scr[...], w_ref[...])


def _inproj(x2, g, sc, sh, w_bf16, layer, *, rows_per_batch, mod_row0):
    m, d = x2.shape
    n = w_bf16.shape[1]
    tn = INPROJ_TN
    if sc.ndim == 3:
        tm = 512
        assert rows_per_batch % tm == 0
        sc_spec = pl.BlockSpec((None, 1, d), lambda i, j: ((i * tm) // rows_per_batch + mod_row0, 0, 1))
        sh_spec = pl.BlockSpec((None, 1, d), lambda i, j: ((i * tm) // rows_per_batch + mod_row0, 0, 0))
        rc = 64
    else:
        tm = m
        sc_spec = pl.BlockSpec((tm, d), lambda i, j: (i, 0))
        sh_spec = pl.BlockSpec((tm, d), lambda i, j: (i, 0))
        rc = tm
    assert m % tm == 0
    return pl.pallas_call(
        functools.partial(_inproj_kernel, rc=rc),
        grid=(m // tm, pl.cdiv(n, tn)),
        in_specs=[pl.BlockSpec((tm, d), lambda i, j: (i, 0)),
                  pl.BlockSpec((1, d), lambda i, j: (0, 0)),
                  sc_spec, sh_spec,
                  pl.BlockSpec((None, tn, d), lambda i, j: (layer, j, 0))],
        out_specs=pl.BlockSpec((tm, tn), lambda i, j: (i, j)),
        out_shape=jax.ShapeDtypeStruct((m, n), F32),
        scratch_shapes=[pltpu.VMEM((tm, d), BF16)],
        compiler_params=_cp("inproj", 2),
        name="inproj",
    )(x2, g, sc, sh, w_bf16)


def _mixa_kernel(u_ref, v_ref, z_ref, vn_ref, ws_ref, bst_ref, y_ref, *rest, c, emit_va):
    u = _gelu(u_ref[...])
    v = _gelu(v_ref[...])
    mu = jnp.mean(v, axis=-1, keepdims=True)
    dv = v - mu
    va = dv * lax.rsqrt(jnp.mean(dv * dv, axis=-1, keepdims=True) + EPS) * vn_ref[...]
    if emit_va:
        rest[0][...] = va
    z = _silu(z_ref[...])
    row = lax.broadcasted_iota(I32, (c, c), 0)
    col = lax.broadcasted_iota(I32, (c, c), 1)
    tril = col <= row
    for h in range(A_HEADS):
        cols = slice(h * A_GROUP, (h + 1) * A_GROUP)
        wm = jnp.where(tril, ws_ref[h], 0.0)
        wm16 = wm.astype(BF16)
        for ci in range(u.shape[0] // c):
            rows = slice(ci * c, (ci + 1) * c)
            vh = va[rows, cols]
            if c >= 128:
                mixed = _dot(wm16, vh.astype(BF16))
            else:
                mixed = wm[:, 0:1] * vh[0:1, :]
                for s in range(1, c):
                    mixed = mixed + wm[:, s:s + 1] * vh[s:s + 1, :]
            mixed = mixed + bst_ref[:, h:h + 1]
            y_ref[rows, cols] = (u[rows, cols] * mixed * z[rows, cols]).astype(y_ref.dtype)


def _mixa(p3, a_vnorm, a_ws, a_bs, *, emit_va):
    b, l, _ = p3.shape
    c = min(A_CHUNK, l)
    r = 2 * c if l % (2 * c) == 0 else c
    n = l // r
    ws = a_ws[:, :c, :c]
    bst = a_bs[:, :c].T
    wblk = W_A
    outs = [jax.ShapeDtypeStruct((b, l, W_A), BF16)]
    out_specs = [pl.BlockSpec((None, r, W_A), lambda i, j: (i, j, 0))]
    if emit_va:
        outs.append(jax.ShapeDtypeStruct((b, l, W_A), F32))
        out_specs.append(pl.BlockSpec((None, r, W_A), lambda i, j: (i, j, 0)))
    res = pl.pallas_call(
        functools.partial(_mixa_kernel, c=c, emit_va=emit_va),
        grid=(b, n),
        in_specs=[pl.BlockSpec((None, r, wblk), lambda i, j: (i, j, OFF["u"] // wblk)),
                  pl.BlockSpec((None, r, wblk), lambda i, j: (i, j, OFF["v"] // wblk)),
                  pl.BlockSpec((None, r, wblk), lambda i, j: (i, j, OFF["za"] // wblk)),
                  pl.BlockSpec((1, W_A), lambda i, j: (0, 0)),
                  pl.BlockSpec((A_HEADS, c, c), lambda i, j: (0, 0, 0)),
                  pl.BlockSpec((c, A_HEADS), lambda i, j: (0, 0))],
        out_specs=out_specs,
        out_shape=outs,
        compiler_params=_cp("mixa", 2),
        name="mixa",
    )(p3, p3, p3, a_vnorm.reshape(1, W_A), ws, bst)
    return res


def _delta_kernel(alog_ref, dtb_ref,
                  q_ref, k_ref, v_ref, z_ref, ab_ref,
                  cwq_ref, cwk_ref, cwv_ref, cpq_ref, cpk_ref, cpv_ref,
                  s0_ref, on_ref,
                  y_ref, sout_ref,
                  xbuf, s_scr, *, t, c, hp, valid_len):
    n = pl.program_id(2)
    dh = DN_HEAD

    @pl.when(n == 0)
    def _():
        s_scr[...] = s0_ref[...]
        for j in range(hp):
            lanes = slice(j * dh, (j + 1) * dh)
            xbuf[3 * j + 0, 5:8, :] = cpq_ref[:, lanes]
            xbuf[3 * j + 1, 5:8, :] = cpk_ref[:, lanes]
            xbuf[3 * j + 2, 5:8, :] = cpv_ref[:, lanes]

    _delta_heads(alog_ref, dtb_ref, q_ref, k_ref, v_ref, z_ref, ab_ref, cwq_ref, cwk_ref, cwv_ref,
                 on_ref, y_ref, xbuf, s_scr, t=t, c=c, hp=hp, valid_len=valid_len)

    @pl.when(n == pl.num_programs(2) - 1)
    def _():
        sout_ref[...] = s_scr[...]


def _delta_heads(alog_ref, dtb_ref, q_ref, k_ref, v_ref, z_ref, ab_ref, cwq_ref, cwk_ref, cwv_ref,
                 on_ref, y_ref, xbuf, s_scr, *, t, c, hp, valid_len):
    hg = pl.program_id(1)
    dh = DN_HEAD
    heads = list(range(hp))

    row = lax.broadcasted_iota(I32, (t, t), 0)
    col = lax.broadcasted_iota(I32, (t, t), 1)
    shift = c.bit_length() - 1
    same = (row >> shift) == (col >> shift)
    eye = row == col
    incl = same & (col <= row)
    strict = same & (col < row)
    incl_t = same & (row <= col)
    blk8 = (row >> 3) == (col >> 3)
    off_masks = []
    bs = 8
    while bs < c:
        sh_b = bs.bit_length() - 1
        inner = (row >> sh_b) == (col >> sh_b)
        outer = (row >> (sh_b + 1)) == (col >> (sh_b + 1))
        off_masks.append(outer & jnp.logical_not(inner))
        bs *= 2
    eye_f = jnp.where(eye, 1.0, 0.0)
    if valid_len < t:
        lane_valid = lax.broadcasted_iota(I32, (1, t), 1) < valid_len
        sub_valid = lax.broadcasted_iota(I32, (t, 1), 0) < valid_len

    def to_col(r):
        return jnp.sum(jnp.where(eye, r, 0.0), axis=1, keepdims=True)

    def conv(idx, x, w_ref, lanes):
        xbuf[idx, 8:8 + t, :] = x
        y = xbuf[idx, pl.ds(5, t), :] * w_ref[0:1, lanes]
        for jj in range(1, CONV_K):
            y = y + xbuf[idx, pl.ds(5 + jj, t), :] * w_ref[jj:jj + 1, lanes]
        xbuf[idx, 5:8, :] = x[t - 3:t, :]
        return _silu(y)

    lanes_of = [slice(j * dh, (j + 1) * dh) for j in range(hp)]

    def per_head(f, *lists):
        return [f(*vals) for vals in zip(*lists)]

    def l2n(a):
        return a * lax.rsqrt(jnp.sum(a * a, axis=-1, keepdims=True) + EPS)

    qc = [l2n(conv(3 * j + 0, q_ref[:, lanes_of[j]], cwq_ref, lanes_of[j])) * (dh ** -0.5) for j in heads]
    kc = [l2n(conv(3 * j + 1, k_ref[:, lanes_of[j]], cwk_ref, lanes_of[j])) for j in heads]
    v = [conv(3 * j + 2, v_ref[:, lanes_of[j]], cwv_ref, lanes_of[j]) for j in heads]

    def gates(j):
        h = hg * hp + j
        a_row = ab_ref[pl.ds(h, 1), :]
        b_row = ab_ref[pl.ds(DN_HEADS + h, 1), :]
        xa = a_row + dtb_ref[h]
        softplus = jnp.maximum(xa, 0.0) + jnp.log(1.0 + jnp.exp(-jnp.abs(xa)))
        a_coef = jnp.exp(jnp.zeros((1, 1), F32) + alog_ref[h])
        g_row = -a_coef * softplus
        beta_row = 1.0 / (1.0 + jnp.exp(-b_row))
        if valid_len < t:
            g_row = jnp.where(lane_valid, g_row, 0.0)
            beta_row = jnp.where(lane_valid, beta_row, 0.0)
        return g_row, beta_row

    g_row, beta_row = zip(*[gates(j) for j in heads])
    if valid_len < t:
        kc = per_head(lambda a: jnp.where(sub_valid, a, 0.0), kc)
        v = per_head(lambda a: jnp.where(sub_valid, a, 0.0), v)

    g_col = per_head(to_col, g_row)
    beta_col = per_head(to_col, beta_row)
    gc_col = per_head(lambda r: jnp.sum(jnp.where(incl, r, 0.0), axis=1, keepdims=True), g_row)
    glast_col = per_head(lambda r: jnp.sum(jnp.where(same, r, 0.0), axis=1, keepdims=True), g_row)
    gc_row = per_head(lambda cl: jnp.sum(jnp.where(incl_t, cl, 0.0), axis=0, keepdims=True), g_col)
    kb = per_head(lambda a, b: a * b, kc, beta_col)
    kc16 = per_head(lambda a: a.astype(BF16), kc)

    def key_products(kb_, kc16_, qc_, gc, gr):
        decay = jnp.where(incl, jnp.exp(jnp.where(incl, gc - gr, 0.0)), 0.0)
        a_mat = jnp.where(strict, _dot_nt(kb_.astype(BF16), kc16_) * decay, 0.0)
        attn16 = (_dot_nt(qc_.astype(BF16), kc16_) * decay).astype(BF16)
        n0f = jnp.where(blk8, -a_mat, 0.0)
        return attn16, _b16(n0f), eye_f + n0f, [_b16(jnp.where(om, a_mat, 0.0)) for om in off_masks]

    attn, n0, x, a_offs = zip(*per_head(key_products, kb, kc16, qc, gc_col, gc_row))
    eg = per_head(jnp.exp, gc_col)
    rhs = per_head(lambda vv, bc, kk, e: _b16(jnp.concatenate([vv * bc, kk * e], axis=1)),
                   v, beta_col, kb, eg)
    n2 = per_head(lambda a: _b16(_dot(a, a)), n0)
    n4 = per_head(lambda a: _b16(_dot(a, a)), n2)
    x = per_head(lambda xx, nn: xx + _dot(_b16(xx), nn), x, n2)
    x = per_head(lambda xx, nn: xx + _dot(_b16(xx), nn), x, n4)
    for lvl in range(len(off_masks)):
        xs = per_head(_b16, x)
        xa_off = per_head(lambda s_, offs: _b16(_dot(s_, offs[lvl])), xs, a_offs)
        x = per_head(lambda xx, xo, s_: xx - _dot(xo, s_), x, xa_off, xs)
    sol16 = per_head(lambda xx, r: _b16(_dot(_b16(xx), r)), x, rhs)
    auw = per_head(_dot, attn, sol16)
    qw = per_head(lambda a, e, m_: (a * e - m_[:, dh:]).astype(BF16), qc, eg, auw)
    kg16 = per_head(lambda a, gl_, gc: (a * jnp.exp(gl_ - gc)).astype(BF16), kc, glast_col, gc_col)

    s = [s_scr[j] for j in heads]
    outs = [[] for _ in heads]
    for i in range(t // c):
        rows = slice(i * c, (i + 1) * c)
        s16 = per_head(lambda a: a.astype(BF16), s)
        kuw = per_head(lambda a, b: _dot_tn(a[rows], b[rows]), kg16, sol16)
        o_i = per_head(lambda a, b16, m_: _dot(a[rows], b16) + m_[rows, :dh], qw, s16, auw)
        for pos in range(len(heads)):
            outs[pos].append(o_i[pos])
        gl = per_head(lambda a: jnp.exp(a[i * c:i * c + 1, :]), glast_col)
        s = per_head(lambda g_, s_, m_, b16: g_ * s_ + m_[:, :dh] - _dot(m_[:, dh:].astype(BF16), b16),
                     gl, s, kuw, s16)
    for pos, j in enumerate(heads):
        s_scr[j] = s[pos]
        o = outs[pos][0] if len(outs[pos]) == 1 else jnp.concatenate(outs[pos], axis=0)
        on = o * lax.rsqrt(jnp.mean(o * o, axis=-1, keepdims=True) + EPS) * on_ref[...]
        y_ref[:, lanes_of[j]] = (on * _silu(z_ref[:, lanes_of[j]])).astype(y_ref.dtype)


def _delta(p3, abt, conv_w, conv_prev, s0, state_layer, a_log, dt_bias, onorm, *, t, valid_len):
    b, l, _ = p3.shape
    assert l % t == 0
    c = min(DN_CHUNK, t)
    hp = DELTA_HEADS_PER_STEP
    nh = DN_HEADS
    ng = nh // hp
    w = 128 * hp
    cb_q = OFF["qkv"] // w
    cb_z = OFF["zb"] // w
    assert OFF["qkv"] % w == 0 and OFF["zb"] % w == 0 and W_B % w == 0

    def pspec(cb0):
        return pl.BlockSpec((None, t, w), lambda i, h, n, a, d: (i, n, cb0 + h))

    def cwspec(sidx):
        return pl.BlockSpec((CONV_K, w), lambda i, h, n, a, d: (0, sidx * ng + h))

    def cpspec(sidx):
        return pl.BlockSpec((None, None, CONV_K - 1, w),
                            lambda i, h, n, a, d: (state_layer, i, 0, sidx * ng + h))

    grid_spec = pltpu.PrefetchScalarGridSpec(
        num_scalar_prefetch=2,
        grid=(b, ng, l // t),
        in_specs=[pspec(cb_q), pspec(cb_q + ng), pspec(cb_q + 2 * ng), pspec(cb_z),
                  pl.BlockSpec((None, 2 * nh, t), lambda i, h, n, a, d: (i, 0, n)),
                  cwspec(0), cwspec(1), cwspec(2), cpspec(0), cpspec(1), cpspec(2),
                  pl.BlockSpec((None, None, hp, DN_HEAD, DN_HEAD),
                               lambda i, h, n, a, d: (state_layer, i, h, 0, 0)),
                  pl.BlockSpec((1, DN_HEAD), lambda i, h, n, a, d: (0, 0))],
        out_specs=[pl.BlockSpec((None, t, w), lambda i, h, n, a, d: (i, n, h)),
                   pl.BlockSpec((None, hp, DN_HEAD, DN_HEAD), lambda i, h, n, a, d: (i, h, 0, 0))],
        scratch_shapes=[pltpu.VMEM((3 * hp, t + 8, 128), F32), pltpu.VMEM((hp, DN_HEAD, DN_HEAD), F32)],
    )
    return pl.pallas_call(
        functools.partial(_delta_kernel, t=t, c=c, hp=hp, valid_len=valid_len),
        grid_spec=grid_spec,
        out_shape=[jax.ShapeDtypeStruct((b, l, W_B), BF16),
                   jax.ShapeDtypeStruct((b, nh, DN_HEAD, DN_HEAD), F32)],
        compiler_params=_cp("delta", 3),
        name="delta",
    )(a_log, dt_bias, p3, p3, p3, p3, abt, conv_w, conv_w, conv_w,
      conv_prev, conv_prev, conv_prev, s0, onorm.reshape(1, DN_HEAD))


def _ordered_bits_to_float(u):
    key = u ^ jnp.int32(INT_MIN)
    bits = jnp.where(key < 0, key ^ jnp.int32(0x7FFFFFFF), key)
    return pltpu.bitcast(bits, F32)


def _topk_threshold(count_ge, shape, topk):
    def bit_body(i, u):
        bit = jnp.left_shift(jnp.int32(1), 31 - i)
        cand_u = u | bit
        cnt = count_ge(_ordered_bits_to_float(cand_u))
        return jnp.where(cnt >= float(topk), cand_u, u)

    u = lax.fori_loop(0, 32, bit_body, jnp.zeros(shape, I32))
    return _ordered_bits_to_float(u)


def _softmax_update(m_ref, l_ref, acc_ref, idx, s, v_tiles):
    m_old = m_ref[idx]
    m_new = jnp.maximum(m_old, jnp.max(s, axis=1, keepdims=True))
    m_safe = jnp.where(m_new == NEG_INF, 0.0, m_new)
    alpha = jnp.exp(m_old - m_safe)
    p = jnp.exp(s - m_safe)
    l_ref[idx] = alpha * l_ref[idx] + jnp.sum(p, axis=1, keepdims=True)
    acc = alpha * acc_ref[idx]
    p16 = p.astype(BF16)
    k0 = 0
    for vt in v_tiles:
        acc = acc + _dot(p16[:, k0:k0 + vt.shape[0]], vt)
        k0 += vt.shape[0]
    acc_ref[idx] = acc
    m_ref[idx] = m_new


def _dsa_prompt_kernel(qi_ref, tq_ref, tall_ref, qc_ref, k_ref, v_ref, z_ref, y_ref,
                       sc_scr, m_scr, l_scr, acc_scr, *, l, topk, kt):
    qb = pl.program_id(1)
    nq = Q_BLOCK
    n_tiles = (qb * nq + nq + kt - 1) // kt
    n_pairs = C_HEADS // 2
    w_rows = tq_ref[...].T[TAIL_WI:TAIL_WI + IDX_HEADS, :] * ((IDX_HEADS ** -0.5) * (IDX_DIM ** -0.5))
    rq = []
    for j in range(IDX_HEADS // 2):
        a = qi_ref[:, j * 128:(j + 1) * 128]
        rq.append(jnp.concatenate([a, pltpu.roll(a, IDX_DIM, 1)], axis=0).astype(BF16))
    kpos = lax.broadcasted_iota(I32, (kt, nq), 0)
    qpos = qb * nq + lax.broadcasted_iota(I32, (kt, nq), 1)
    lane = lax.broadcasted_iota(I32, (kt, 128), 1)

    def score_tile(ti, carry):
        k0 = pl.multiple_of(ti * kt, kt)
        ki = jnp.where(lane < IDX_DIM, tall_ref[pl.ds(k0, kt), :], 0.0).astype(BF16)
        score = jnp.zeros((kt, nq), F32)
        for j in range(IDX_HEADS // 2):
            lg = _dot_nt(ki, rq[j])
            score = score + w_rows[2 * j:2 * j + 1, :] * jnp.maximum(lg[:, :nq], 0.0)
            score = score + w_rows[2 * j + 1:2 * j + 2, :] * jnp.maximum(lg[:, nq:], 0.0)
        sc_scr[pl.ds(k0, kt), :] = jnp.where(kpos + k0 <= qpos, score, NEG_INF)
        return carry

    lax.fori_loop(0, n_tiles, score_tile, 0)

    ct = 256
    n_ct = (qb * nq + nq + ct - 1) // ct
    n_acc = 4

    def count(cand, strict):
        cand8 = jnp.broadcast_to(cand, (8, nq))

        def tile_body(ti, accs):
            k0 = pl.multiple_of(ti * ct, ct)
            tile = sc_scr[pl.ds(k0, ct), :]
            accs = list(accs)
            for r in range(ct // 8):
                blk = tile[8 * r:8 * r + 8, :]
                hit = (blk > cand8) if strict else (blk >= cand8)
                accs[r % n_acc] = accs[r % n_acc] + jnp.where(hit, 1.0, 0.0)
            return tuple(accs)

        accs = lax.fori_loop(0, n_ct, tile_body, tuple(jnp.zeros((8, nq), F32) for _ in range(n_acc)))
        acc = (accs[0] + accs[1]) + (accs[2] + accs[3])
        return jnp.sum(acc, axis=0, keepdims=True)

    thr = _topk_threshold(lambda cand: count(cand, False), (1, nq), topk)
    keep_all = (qb * nq + lax.broadcasted_iota(I32, (1, nq), 1)) < topk
    need = float(topk) - count(thr, True)
    tri16 = jnp.where(lax.broadcasted_iota(I32, (128, 128), 1) <= lax.broadcasted_iota(I32, (128, 128), 0),
                      1.0, 0.0).astype(BF16)

    def select_tile(ti, seen):
        k0 = pl.multiple_of(ti * kt, kt)
        sc = sc_scr[pl.ds(k0, kt), :]
        eq = sc == thr
        eq_f = jnp.where(eq, 1.0, 0.0)
        eq16 = eq_f.astype(BF16)
        prefs = [_dot(tri16, eq16[r:r + 128]) for r in range(0, kt, 128)]
        ranks = []
        for pref in prefs:
            ranks.append(seen + pref)
            seen = seen + pref[127:128, :]
        tie_ok = (jnp.concatenate(ranks, axis=0) - eq_f) < need
        chosen = jnp.where(sc > thr, 0.0, jnp.where(eq, jnp.where(tie_ok, 0.0, NEG_INF), NEG_INF))
        sc_scr[pl.ds(k0, kt), :] = jnp.where(keep_all, jnp.where(sc > NEG_INF, 0.0, NEG_INF), chosen)
        return seen

    lax.fori_loop(0, n_tiles, select_tile, jnp.zeros((1, nq), F32))

    m_scr[...] = jnp.full(m_scr.shape, NEG_INF, F32)
    l_scr[...] = jnp.zeros(l_scr.shape, F32)
    acc_scr[...] = jnp.zeros(acc_scr.shape, F32)
    scale = (C_HEAD ** -0.5) * 1.4426950408889634
    qp = []
    for pr in range(n_pairs):
        c0 = slice(2 * pr * C_HEAD, (2 * pr + 1) * C_HEAD)
        c1 = slice((2 * pr + 1) * C_HEAD, (2 * pr + 2) * C_HEAD)
        qp.append(jnp.concatenate([qc_ref[:, c0] * scale, qc_ref[:, c1] * scale], axis=0).astype(BF16))
    pairs_per_kv = n_pairs // C_KV_HEADS

    def attend_tile(ti, carry):
        k0 = pl.multiple_of(ti * kt, kt)
        bias = sc_scr[pl.ds(k0, kt), :]
        k16 = [k_ref[pl.ds(k0, kt), hk * C_HEAD:(hk + 1) * C_HEAD].astype(BF16) for hk in range(C_KV_HEADS)]
        vt16 = [v_ref[pl.ds(k0, kt), hk * C_HEAD:(hk + 1) * C_HEAD].T.astype(BF16) for hk in range(C_KV_HEADS)]
        bias2 = jnp.concatenate([bias, bias], axis=1)
        pr_all = range(n_pairs)
        s = [_dot_nt(k16[pr // pairs_per_kv], qp[pr]) + bias2 for pr in pr_all]
        m_old = [m_scr[pr] for pr in pr_all]
        m_new = [jnp.maximum(m_old[pr], jnp.max(s[pr], axis=0, keepdims=True)) for pr in pr_all]
        m_safe = [jnp.where(m_new[pr] == NEG_INF, 0.0, m_new[pr]) for pr in pr_all]
        alpha = [jnp.exp2(m_old[pr] - m_safe[pr]) for pr in pr_all]
        p = [jnp.exp2(s[pr] - m_safe[pr]) for pr in pr_all]
        for pr in pr_all:
            l_scr[pr] = alpha[pr] * l_scr[pr] + jnp.sum(p[pr], axis=0, keepdims=True)
            acc_scr[pr] = alpha[pr] * acc_scr[pr] + _dot(vt16[pr // pairs_per_kv], p[pr].astype(BF16))
            m_scr[pr] = m_new[pr]
        return carry

    lax.fori_loop(0, n_tiles, attend_tile, 0)
    for pr in range(n_pairs):
        o_t = acc_scr[pr] / l_scr[pr]
        for e in range(2):
            cols = slice((2 * pr + e) * C_HEAD, (2 * pr + e + 1) * C_HEAD)
            o = o_t[:, e * nq:(e + 1) * nq].T
            y_ref[:, cols] = (o * _silu(z_ref[:, cols])).astype(y_ref.dtype)


def _dsa_prompt(p3):
    b, l, _ = p3.shape
    assert l % Q_BLOCK == 0
    topk = min(TOPK_MAX, l // 4)
    nq = Q_BLOCK
    kt = DSA_KEY_TILE if l % DSA_KEY_TILE == 0 else 256
    assert l % kt == 0
    n_qb = l // nq
    return pl.pallas_call(
        functools.partial(_dsa_prompt_kernel, l=l, topk=topk, kt=kt),
        grid=(b, n_qb),
        in_specs=[pl.BlockSpec((None, nq, W_C), lambda i, j: (i, j, OFF["qi"] // W_C)),
                  pl.BlockSpec((None, nq, 128), lambda i, j: (i, j, OFF_TAIL // 128)),
                  pl.BlockSpec((None, l, 128), lambda i, j: (i, 0, OFF_TAIL // 128)),
                  pl.BlockSpec((None, nq, W_C), lambda i, j: (i, j, OFF["qc"] // W_C)),
                  pl.BlockSpec((None, l, C_KV), lambda i, j: (i, 0, OFF["kc"] // C_KV)),
                  pl.BlockSpec((None, l, C_KV), lambda i, j: (i, 0, OFF["vc"] // C_KV)),
                  pl.BlockSpec((None, nq, W_C), lambda i, j: (i, j, OFF["zc"] // W_C))],
        out_specs=pl.BlockSpec((None, nq, W_C), lambda i, j: (i, j, 0)),
        out_shape=jax.ShapeDtypeStruct((b, l, W_C), BF16),
        scratch_shapes=[pltpu.VMEM((l, nq), F32),
                        pltpu.VMEM((C_HEADS // 2, 1, 2 * nq), F32), pltpu.VMEM((C_HEADS // 2, 1, 2 * nq), F32),
                        pltpu.VMEM((C_HEADS // 2, C_HEAD, 2 * nq), F32)],
        compiler_params=_cp("dsa_prompt", 2),
        name="dsa_prompt",
    )(p3, p3, p3, p3, p3, p3, p3)


def _idx_scores(logits, wcol, t):
    r = jnp.maximum(logits, 0.0) * wcol
    return jnp.sum(r.reshape(t, IDX_HEADS, logits.shape[1]), axis=1)


def _dsa_s_score_kernel(pt_ref, q_ref, w_ref, kn_ref, *refs, pg, t):
    pages = refs[:pg]
    out_ref, new_ref = refs[pg:]
    q16 = q_ref[...].astype(BF16)
    wcol = w_ref[...] * ((IDX_HEADS ** -0.5) * (IDX_DIM ** -0.5))
    keys16 = jnp.concatenate([pages[i][...] for i in range(pg)], axis=1).astype(BF16)
    out_ref[...] = _idx_scores(_dot(q16, keys16), wcol, t)

    @pl.when(pl.program_id(1) == 0)
    def _():
        new_ref[...] = _idx_scores(_dot_nt(q16, kn_ref[...].astype(BF16)), wcol, t)


def _dsa_s_select_kernel(sc_ref, scn_ref, tq_ref, bias_ref, *, nk, topk):
    rows = sc_ref.shape[0]
    tq = tq_ref[...]
    new_ok = lax.broadcasted_iota(I32, (rows, 128), 1) <= tq
    sc_new = jnp.where(new_ok, scn_ref[...], NEG_INF)
    sc_past = sc_ref[...]

    def count_ge(cand):
        c1 = jnp.sum(jnp.where(sc_past >= cand, 1.0, 0.0), axis=1, keepdims=True)
        c2 = jnp.sum(jnp.where(sc_new >= cand, 1.0, 0.0), axis=1, keepdims=True)
        return c1 + c2

    thr = _topk_threshold(count_ge, (rows, 1), topk)
    keep_all = (nk + 1 + tq[:, 0:1]) <= topk
    n_gt = (jnp.sum(jnp.where(sc_past > thr, 1.0, 0.0), axis=1, keepdims=True)
            + jnp.sum(jnp.where(sc_new > thr, 1.0, 0.0), axis=1, keepdims=True))
    need = float(topk) - n_gt
    tri16 = jnp.where(lax.broadcasted_iota(I32, (128, 128), 0) <= lax.broadcasted_iota(I32, (128, 128), 1),
                      1.0, 0.0).astype(BF16)

    def bias_blocks(scs, seen):
        eqs = [sc == thr for sc in scs]
        eq_fs = [jnp.where(eq, 1.0, 0.0) for eq in eqs]
        prefs = [_dot(eq_f.astype(BF16), tri16) for eq_f in eq_fs]
        out = []
        for sc, eq, eq_f, pref in zip(scs, eqs, eq_fs, prefs):
            tie_ok = (seen + pref - eq_f) < need
            chosen = jnp.where(sc > thr, 0.0, jnp.where(eq, jnp.where(tie_ok, 0.0, NEG_INF), NEG_INF))
            out.append(jnp.where(keep_all, jnp.where(sc > NEG_INF, 0.0, NEG_INF), chosen))
            seen = seen + pref[:, 127:128]
        return out, seen

    n_blk = nk // 128
    grp = 8 if n_blk % 8 == 0 else 1

    def past_group(gi, seen):
        lanes = [pl.ds(pl.multiple_of((gi * grp + i) * 128, 128), 128) for i in range(grp)]
        out, seen = bias_blocks([sc_ref[:, ln] for ln in lanes], seen)
        for ln, bias in zip(lanes, out):
            bias_ref[:, ln] = bias
        return seen

    seen = lax.fori_loop(0, n_blk // grp, past_group, jnp.zeros((rows, 1), F32))
    (bias_new,), _ = bias_blocks([sc_new], seen)
    bias_ref[:, nk:] = bias_new


def _dsa_s_attend_kernel(pt_ref, q_ref, bias_ref, biasn_ref, kn_ref, vn_ref, z_ref, *refs, pg):
    kpages = refs[:pg]
    vpages = refs[pg:2 * pg]
    o_ref = refs[2 * pg]
    m_scr, l_scr, acc_scr = refs[2 * pg + 1:]
    g = pl.program_id(1)
    scale = C_HEAD ** -0.5
    page = kpages[0].shape[0] // C_KV_HEADS

    @pl.when(g == 0)
    def _():
        m_scr[...] = jnp.full(m_scr.shape, NEG_INF, F32)
        l_scr[...] = jnp.zeros(l_scr.shape, F32)
        acc_scr[...] = jnp.zeros(acc_scr.shape, F32)
        for hk in range(C_KV_HEADS):
            cols = slice(hk * C_HEAD, (hk + 1) * C_HEAD)
            q16 = (q_ref[hk] * scale).astype(BF16)
            s = _dot_nt(q16, kn_ref[:, cols].astype(BF16)) + biasn_ref[...]
            _softmax_update(m_scr, l_scr, acc_scr, hk, s, [vn_ref[:, cols].astype(BF16)])

    def two_pages(refs_, i, rows):
        return jnp.concatenate([refs_[i][rows, :], refs_[i + 1][rows, :]], axis=0).astype(BF16)

    for hk in range(C_KV_HEADS):
        q16 = (q_ref[hk] * scale).astype(BF16)
        rows = pl.ds(hk, page, stride=C_KV_HEADS)
        tiles = [_dot_nt(q16, two_pages(kpages, i, rows)) for i in range(0, pg, 2)]
        s = jnp.concatenate(tiles, axis=1) + bias_ref[...]
        _softmax_update(m_scr, l_scr, acc_scr, hk, s, [two_pages(vpages, i, rows) for i in range(0, pg, 2)])

    @pl.when(g == pl.num_programs(1) - 1)
    def _():
        for hk in range(C_KV_HEADS):
            o_ref[hk] = (acc_scr[hk] / l_scr[hk]) * _silu(z_ref[hk])


def _dsa_sample(p3s, cache_k, cache_v, cache_kidx, page_table, layer):
    b, t, _ = p3s.shape
    n_pages = page_table.shape[1]
    page = cache_k.shape[2]
    assert page == 128
    past = n_pages * page
    topk = min(TOPK_MAX, (past + t) // 4)
    pg = 32 if n_pages % 32 == 0 else (16 if n_pages % 16 == 0 else n_pages)
    assert pg % 2 == 0
    ng = n_pages // pg
    rows = t * C_GROUPS

    qi = p3s[:, :, OFF["qi"]:OFF["qi"] + IDX_HEADS * IDX_DIM].reshape(b, t * IDX_HEADS, IDX_DIM)
    wi = p3s[:, :, OFF_TAIL + TAIL_WI:OFF_TAIL + TAIL_WI + IDX_HEADS].reshape(b, t * IDX_HEADS, 1)
    ki_new = jnp.pad(p3s[:, :, OFF_TAIL:OFF_TAIL + IDX_DIM], ((0, 0), (0, 128 - t), (0, 0)))
    k_new = jnp.pad(p3s[:, :, OFF["kc"]:OFF["kc"] + C_KV], ((0, 0), (0, 128 - t), (0, 0)))
    v_new = jnp.pad(p3s[:, :, OFF["vc"]:OFF["vc"] + C_KV], ((0, 0), (0, 128 - t), (0, 0)))

    def heads_major(a):
        a = a.reshape(b, t, C_KV_HEADS, C_GROUPS, C_HEAD)
        return jnp.transpose(a, (0, 2, 1, 3, 4)).reshape(b, C_KV_HEADS, rows, C_HEAD)

    qh = heads_major(p3s[:, :, OFF["qc"]:OFF["qc"] + W_C])
    zh = heads_major(p3s[:, :, OFF["zc"]:OFF["zc"] + W_C])

    kidx_t = jnp.swapaxes(cache_kidx, 2, 3)
    score_spec = pltpu.PrefetchScalarGridSpec(
        num_scalar_prefetch=1,
        grid=(b, ng),
        in_specs=[pl.BlockSpec((None, t * IDX_HEADS, IDX_DIM), lambda i, g, pt: (i, 0, 0)),
                  pl.BlockSpec((None, t * IDX_HEADS, 1), lambda i, g, pt: (i, 0, 0)),
                  pl.BlockSpec((None, 128, IDX_DIM), lambda i, g, pt: (i, 0, 0))]
                 + [pl.BlockSpec((None, None, IDX_DIM, page),
                                 lambda i, g, pt, j=j: (layer, pt[i, g * pg + j], 0, 0)) for j in range(pg)],
        out_specs=[pl.BlockSpec((None, t, pg * page), lambda i, g, pt: (i, 0, g)),
                   pl.BlockSpec((None, t, 128), lambda i, g, pt: (i, 0, 0))],
    )
    scores, scores_new = pl.pallas_call(
        functools.partial(_dsa_s_score_kernel, pg=pg, t=t),
        grid_spec=score_spec,
        out_shape=[jax.ShapeDtypeStruct((b, t, past), F32), jax.ShapeDtypeStruct((b, t, 128), F32)],
        compiler_params=_cp("dsa_s_score", 2),
        name="dsa_s_score",
    )(page_table, qi, wi, ki_new, *([kidx_t] * pg))

    tq = jnp.broadcast_to(jnp.tile(jnp.arange(t, dtype=I32), b)[:, None], (b * t, 128))
    bias = pl.pallas_call(
        functools.partial(_dsa_s_select_kernel, nk=past, topk=topk),
        grid=(1,),
        in_specs=[pl.BlockSpec((b * t, past), lambda i: (0, 0)),
                  pl.BlockSpec((b * t, 128), lambda i: (0, 0)),
                  pl.BlockSpec((b * t, 128), lambda i: (0, 0))],
        out_specs=pl.BlockSpec((b * t, past + 128), lambda i: (0, 0)),
        out_shape=jax.ShapeDtypeStruct((b * t, past + 128), F32),
        compiler_params=_cp("dsa_s_select", 1),
        name="dsa_s_select",
    )(scores.reshape(b * t, past), scores_new.reshape(b * t, 128), tq).reshape(b, t, past + 128)

    bias_rows = jnp.repeat(bias, C_GROUPS, axis=1)
    bias_past = bias_rows[:, :, :past]
    bias_new = bias_rows[:, :, past:]

    ck = cache_k.reshape(cache_k.shape[0], cache_k.shape[1], page * C_KV_HEADS, C_HEAD)
    cv = cache_v.reshape(cache_v.shape[0], cache_v.shape[1], page * C_KV_HEADS, C_HEAD)
    pspec = [pl.BlockSpec((None, None, page * C_KV_HEADS, C_HEAD),
                          lambda i, g, pt, j=j: (layer, pt[i, g * pg + j], 0, 0)) for j in range(pg)]
    attend_spec = pltpu.PrefetchScalarGridSpec(
        num_scalar_prefetch=1,
        grid=(b, ng),
        in_specs=[pl.BlockSpec((None, C_KV_HEADS, rows, C_HEAD), lambda i, g, pt: (i, 0, 0, 0)),
                  pl.BlockSpec((None, rows, pg * page), lambda i, g, pt: (i, 0, g)),
                  pl.BlockSpec((None, rows, 128), lambda i, g, pt: (i, 0, 0)),
                  pl.BlockSpec((None, 128, C_KV), lambda i, g, pt: (i, 0, 0)),
                  pl.BlockSpec((None, 128, C_KV), lambda i, g, pt: (i, 0, 0)),
                  pl.BlockSpec((None, C_KV_HEADS, rows, C_HEAD), lambda i, g, pt: (i, 0, 0, 0))]
                 + pspec + pspec,
        out_specs=pl.BlockSpec((None, C_KV_HEADS, rows, C_HEAD), lambda i, g, pt: (i, 0, 0, 0)),
        scratch_shapes=[pltpu.VMEM((C_KV_HEADS, rows, 1), F32), pltpu.VMEM((C_KV_HEADS, rows, 1), F32),
                        pltpu.VMEM((C_KV_HEADS, rows, C_HEAD), F32)],
    )
    oh = pl.pallas_call(
        functools.partial(_dsa_s_attend_kernel, pg=pg),
        grid_spec=attend_spec,
        out_shape=jax.ShapeDtypeStruct((b, C_KV_HEADS, rows, C_HEAD), F32),
        compiler_params=_cp("dsa_s_attend", 2),
        name="dsa_s_attend",
    )(page_table, qh, bias_past, bias_new, k_new, v_new, zh, *([ck] * pg), *([cv] * pg))
    y = jnp.transpose(oh.reshape(b, C_KV_HEADS, t, C_GROUPS, C_HEAD), (0, 2, 1, 3, 4)).reshape(b, t, W_C)
    return y.astype(BF16)


def _outproj_kernel(ya_ref, yb_ref, yc_ref, w_ref, x_ref, gate_ref, o_ref, mix_scr):
    @pl.when(pl.program_id(1) == 0)
    def _():
        mix_scr[:, 0:W_A] = ya_ref[...]
        mix_scr[:, W_A:W_A + W_B] = yb_ref[...]
        mix_scr[:, W_A + W_B:] = yc_ref[...]

    o_ref[...] = x_ref[...] + gate_ref[...] * _dot(mix_scr[...], w_ref[...])


def _outproj_norm_kernel(ya_ref, yb_ref, yc_ref, w_ref, x_ref, gate_ref, g_ref, o_ref, mix_scr, ssq_scr):
    j = pl.program_id(1)
    tn = x_ref.shape[1]

    @pl.when(j == 0)
    def _():
        mix_scr[:, 0:W_A] = ya_ref[...]
        mix_scr[:, W_A:W_A + W_B] = yb_ref[...]
        mix_scr[:, W_A + W_B:] = yc_ref[...]
        ssq_scr[...] = jnp.zeros(ssq_scr.shape, F32)

    x_new = x_ref[...] + gate_ref[...] * _dot(mix_scr[...], w_ref[...])
    o_ref[:, pl.ds(pl.multiple_of(j * tn, tn), tn)] = x_new
    ssq_scr[...] += jnp.sum(x_new * x_new, axis=-1, keepdims=True)

    @pl.when(j == pl.num_programs(1) - 1)
    def _():
        tm, d = o_ref.shape
        rc = 64 if tm % 64 == 0 else tm

        def body(c, carry):
            r = pl.ds(pl.multiple_of(c * rc, rc), rc)
            o_ref[r, :] = o_ref[r, :] * lax.rsqrt(ssq_scr[r, :] / d + EPS) * g_ref[...]
            return carry

        lax.fori_loop(0, tm // rc, body, 0)


def _outproj(ya, yb, yc, w_bf16, layer, x2, gate, *, rows_per_batch, mod_row0, g_final=None):
    m, d = x2.shape
    tn = 1024
    if gate.ndim == 3:
        tm = 512
        gate_spec = pl.BlockSpec((None, 1, tn),
                                 lambda i, j: ((i * tm) // rows_per_batch + mod_row0, 0, 2 * (d // tn) + j))
    else:
        tm = m
        gate_spec = pl.BlockSpec((tm, tn), lambda i, j: (i, j))
    assert m % tm == 0
    in_specs = [pl.BlockSpec((tm, W_A), lambda i, j: (i, 0)),
                pl.BlockSpec((tm, W_B), lambda i, j: (i, 0)),
                pl.BlockSpec((tm, W_C), lambda i, j: (i, 0)),
                pl.BlockSpec((None, d, tn), lambda i, j: (layer, 0, j)),
                pl.BlockSpec((tm, tn), lambda i, j: (i, j)),
                gate_spec]
    operands = [ya, yb, yc, w_bf16, x2, gate]
    scratch = [pltpu.VMEM((tm, d), BF16)]
    if g_final is None:
        body, name = _outproj_kernel, "outproj"
        out_spec = pl.BlockSpec((tm, tn), lambda i, j: (i, j))
    else:
        body, name = _outproj_norm_kernel, "outproj_norm"
        in_specs.append(pl.BlockSpec((1, d), lambda i, j: (0, 0)))
        operands.append(g_final.reshape(1, d))
        scratch.append(pltpu.VMEM((tm, 1), F32))
        out_spec = pl.BlockSpec((tm, d), lambda i, j: (i, 0))
    return pl.pallas_call(
        body,
        grid=(m // tm, d // tn),
        in_specs=in_specs,
        out_specs=out_spec,
        out_shape=jax.ShapeDtypeStruct((m, d), F32),
        scratch_shapes=scratch,
        compiler_params=_cp(name, 2),
        name=name,
    )(*operands)


def _kv_rows_kernel(*refs, depth, tm):
    k_refs, v_refs = refs[:depth], refs[depth:2 * depth]
    pk_ref, pv_ref = refs[2 * depth:]
    for lyr in range(depth):
        for h in range(C_KV_HEADS):
            cols = slice(h * C_HEAD, (h + 1) * C_HEAD)
            rows = pl.ds(h, tm, stride=C_KV_HEADS)
            pk_ref.at[lyr][rows, :] = k_refs[lyr][:, cols]
            pv_ref.at[lyr][rows, :] = v_refs[lyr][:, cols]


def _kv_rows(p3_layers):
    depth = len(p3_layers)
    b, l, _ = p3_layers[0].shape
    tm = 512 if l % 512 == 0 else l
    kspec = pl.BlockSpec((None, tm, C_KV), lambda i, j: (i, j, OFF["kc"] // C_KV))
    vspec = pl.BlockSpec((None, tm, C_KV), lambda i, j: (i, j, OFF["vc"] // C_KV))
    ospec = pl.BlockSpec((depth, None, tm * C_KV_HEADS, C_HEAD), lambda i, j: (0, i, j, 0))
    oshape = jax.ShapeDtypeStruct((depth, b, l * C_KV_HEADS, C_HEAD), F32)
    pk, pv = pl.pallas_call(
        functools.partial(_kv_rows_kernel, depth=depth, tm=tm),
        grid=(b, l // tm),
        in_specs=[kspec] * depth + [vspec] * depth,
        out_specs=[ospec, ospec],
        out_shape=[oshape, oshape],
        compiler_params=_cp("kv_rows", 2),
        name="kv_rows",
    )(*p3_layers, *p3_layers)
    shape = (depth, b, l, C_KV_HEADS, C_HEAD)
    return pk.reshape(shape), pv.reshape(shape)


def _w_in_tile_table():
    n_t = NP // 128
    table, n_plain = [], None
    for dt in range(n_t - 1):
        c = dt * 128
        seg = [n for n in _DST_ORDER if OFF[n] <= c < OFF[n] + _SRC_OFF[n][1]][0]
        src = _SRC_OFF[seg][0] + (c - OFF[seg])
        if src % 128 == 0:
            assert n_plain is None
        else:
            assert src % 128 == 32
            if n_plain is None:
                n_plain = dt
        table.append(src // 128)
    t_kw, t_ab = _SRC_OFF["ki"][0] // 128, _SRC_OFF["a"][0] // 128
    assert _SRC_OFF["ki"][0] % 128 == 32 and _SRC_OFF["wi"][0] == t_kw * 128 + TAIL_WI
    assert _SRC_OFF["a"][0] % 128 == 0 and _SRC_OFF["b"][0] == _SRC_OFF["a"][0] + DN_HEADS
    assert TAIL_A == IDX_DIM and TAIL_B == TAIL_A + DN_HEADS and TAIL_WI == TAIL_B + DN_HEADS
    tile_a = table + [t_kw]
    rows_b = [t_ab * 4] * n_plain + [(s + 1) * 4 for s in table[n_plain:]] + [t_ab * 4]
    return tile_a, rows_b, n_plain


def _wprep_kernel(ta_ref, tb_ref, a_ref, b_ref, o_ref, *, n_plain, n_t):
    dt = pl.program_id(1)
    dt_o = o_ref.dtype

    @pl.when(dt < n_plain)
    def _():
        o_ref[...] = a_ref[...].astype(dt_o)

    @pl.when((dt >= n_plain) & (dt < n_t - 1))
    def _():
        o_ref[0:96, :] = a_ref[32:128, :].astype(dt_o)
        o_ref[96:128, :] = b_ref[...].astype(dt_o)

    @pl.when(dt == n_t - 1)
    def _():
        o_ref[0:TAIL_A, :] = a_ref[32:32 + IDX_DIM, :].astype(dt_o)
        o_ref[TAIL_A:TAIL_WI, :] = b_ref[...].astype(dt_o)
        o_ref[TAIL_WI:TAIL_WI + IDX_HEADS, :] = a_ref[TAIL_WI:TAIL_WI + IDX_HEADS, :].astype(dt_o)
        o_ref[TAIL_WI + IDX_HEADS:, :] = jnp.zeros((128 - TAIL_WI - IDX_HEADS, o_ref.shape[1]), dt_o)


def _permute_w_in(w_in):
    depth, d, _ = w_in.shape
    w_t = jnp.swapaxes(w_in, 1, 2)
    tile_a, rows_b, n_plain = _w_in_tile_table()
    n_t = NP // 128
    grid_spec = pltpu.PrefetchScalarGridSpec(
        num_scalar_prefetch=2,
        grid=(depth, n_t),
        in_specs=[pl.BlockSpec((None, 128, d), lambda l, t, ta, tb: (l, ta[t], 0)),
                  pl.BlockSpec((None, 32, d), lambda l, t, ta, tb: (l, tb[t], 0))],
        out_specs=pl.BlockSpec((None, 128, d), lambda l, t, ta, tb: (l, t, 0)),
    )
    return pl.pallas_call(
        functools.partial(_wprep_kernel, n_plain=n_plain, n_t=n_t),
        grid_spec=grid_spec,
        out_shape=jax.ShapeDtypeStruct((depth, NP, d), BF16),
        compiler_params=_cp("w_in_layout", 2),
        name="w_in_layout",
    )(jnp.asarray(tile_a, I32), jnp.asarray(rows_b, I32), w_t, w_t)


def _delta_t(l):
    for t in (128, 64):
        if l % t == 0:
            return t
    raise ValueError("sequence length must be a multiple of 64")


def kernel(x_prompt, x_sample, cache_k, cache_v, cache_kidx, state_dn, state_conv, page_table,
           c_prompt, c_sample, w_ada, b_ada, g_norm, w_in, a_vnorm, a_ws, a_bs, dn_conv_w,
           dn_a_log, dn_dt_bias, dn_onorm, w_out, g_final):
    bp, lp, d = x_prompt.shape
    bs, ls, _ = x_sample.shape
    depth = w_ada.shape[0]
    assert d == D_MODEL and w_in.shape[2] == D_IN
    assert CONV_K - 1 <= ls <= DN_CHUNK

    n_c = bp + bs
    c_rows = jnp.concatenate([c_prompt, c_sample], axis=0)
    r_pad = (-n_c) % 8
    if r_pad:
        c_rows = jnp.pad(c_rows, ((0, r_pad), (0, 0)))
    m_all = _ada(c_rows, w_ada, b_ada)

    xp = x_prompt.reshape(bp * lp, d)
    xs = x_sample.reshape(bs * ls, d)
    ls_pad = 8 * ((ls + 7) // 8)
    outs = {k: [] for k in ("p3", "pki", "pdn", "pconv", "sk", "sv", "ski", "sdn", "sconv", "samlp")}
    zeros_conv = jnp.zeros((1, bp, CONV_K - 1, DN_CONV_DIM), F32)
    zeros_state = jnp.zeros((1, bp, DN_HEADS, DN_HEAD, DN_HEAD), F32)

    w_in_all = _permute_w_in(w_in)
    w_out_all = w_out.astype(BF16)
    def g_last(layer):
        return g_final if layer == depth - 1 else None

    for l in range(depth):
        g_l = g_norm[l].reshape(1, d)
        m_l = m_all[l]
        m3 = m_l.reshape(m_l.shape[0], 1, 3 * d)
        ms = jnp.repeat(m_l[bp:bp + bs], ls, axis=0)

        pp = _inproj(xp, g_l, m3, m3, w_in_all, l, rows_per_batch=lp, mod_row0=0)
        p3 = pp.reshape(bp, lp, NP)
        (ya,) = _mixa(p3, a_vnorm[l], a_ws[l], a_bs[l], emit_va=False)
        abt = jnp.transpose(p3[:, :, OFF["a"]:OFF["a"] + 2 * DN_HEADS], (0, 2, 1))
        yb, s_p = _delta(p3, abt, dn_conv_w[l], zeros_conv, zeros_state, 0, dn_a_log[l], dn_dt_bias[l],
                         dn_onorm[l], t=_delta_t(lp), valid_len=_delta_t(lp))
        yc = _dsa_prompt(p3)
        xp = _outproj(ya.reshape(bp * lp, W_A), yb.reshape(bp * lp, W_B), yc.reshape(bp * lp, W_C),
                      w_out_all, l, xp, m3, rows_per_batch=lp, mod_row0=0, g_final=g_last(l))
        outs["p3"].append(p3)
        outs["pki"].append(p3[:, :, OFF_TAIL:OFF_TAIL + IDX_DIM])
        outs["pdn"].append(s_p)
        outs["pconv"].append(p3[:, lp - (CONV_K - 1):, OFF["qkv"]:OFF["qkv"] + DN_CONV_DIM])

        ps = _inproj(xs, g_l, ms[:, d:2 * d], ms[:, 0:d], w_in_all, l, rows_per_batch=ls, mod_row0=bp)
        p3s = ps.reshape(bs, ls, NP)
        ya_s, va_s = _mixa(p3s, a_vnorm[l], a_ws[l], a_bs[l], emit_va=True)
        p3s_pad = jnp.pad(p3s, ((0, 0), (0, ls_pad - ls), (0, 0)))
        abt_s = jnp.transpose(p3s_pad[:, :, OFF["a"]:OFF["a"] + 2 * DN_HEADS], (0, 2, 1))
        yb_s, s_s = _delta(p3s_pad, abt_s, dn_conv_w[l], state_conv, state_dn, l, dn_a_log[l],
                           dn_dt_bias[l], dn_onorm[l], t=ls_pad, valid_len=ls)
        yc_s = _dsa_sample(p3s, cache_k, cache_v, cache_kidx, page_table, l)
        xs = _outproj(ya_s.reshape(bs * ls, W_A), yb_s[:, :ls].reshape(bs * ls, W_B),
                      yc_s.reshape(bs * ls, W_C), w_out_all, l, xs, ms[:, 2 * d:3 * d],
                      rows_per_batch=ls, mod_row0=bp, g_final=g_last(l))
        outs["sk"].append(p3s[:, :, OFF["kc"]:OFF["kc"] + C_KV].reshape(bs, ls, C_KV_HEADS, C_HEAD))
        outs["sv"].append(p3s[:, :, OFF["vc"]:OFF["vc"] + C_KV].reshape(bs, ls, C_KV_HEADS, C_HEAD))
        outs["ski"].append(p3s[:, :, OFF_TAIL:OFF_TAIL + IDX_DIM])
        outs["sdn"].append(s_s)
        outs["sconv"].append(p3s[:, ls - (CONV_K - 1):, OFF["qkv"]:OFF["qkv"] + DN_CONV_DIM])
        outs["samlp"].append(va_s)

    y_prompt = xp.reshape(bp, lp, d)
    y_sample = xs.reshape(bs, ls, d)
    st = jnp.stack
    p_k, p_v = _kv_rows(outs["p3"])
    return (y_prompt, y_sample, p_k, p_v, st(outs["pki"]), st(outs["pdn"]),
            st(outs["pconv"]), st(outs["sk"]), st(outs["sv"]), st(outs["ski"]), st(outs["sdn"]),
            st(outs["sconv"]), st(outs["samlp"]))
```

```python
import functools

import jax
import jax.numpy as jnp
from jax import lax
from jax.experimental import pallas as pl
from jax.experimental.pallas import tpu as pltpu

F32 = jnp.float32
BF16 = jnp.bfloat16
I32 = jnp.int32
EPS = 1e-6
INT_MIN = -(2 ** 31)
NEG_INF = float("-inf")

D_MODEL = 4096
W_A = D_MODEL // 4
A_GROUP = 128
A_HEADS = W_A // A_GROUP
A_CHUNK = 128
W_B = D_MODEL // 2
DN_HEAD = 128
DN_HEADS = W_B // DN_HEAD
CONV_K = 4
DN_CONV_DIM = 3 * W_B
DN_CHUNK = 64
W_C = D_MODEL - W_A - W_B
C_HEAD = 128
C_HEADS = W_C // C_HEAD
C_KV_HEADS = 2
C_GROUPS = C_HEADS // C_KV_HEADS
C_KV = C_KV_HEADS * C_HEAD
IDX_HEADS = 16
IDX_DIM = 64
TOPK_MAX = 256
Q_BLOCK = 512

_SRC_SPLITS = (W_A, W_A, W_A, DN_CONV_DIM, W_B, DN_HEADS, DN_HEADS,
               W_C, C_KV, C_KV, W_C, IDX_HEADS * IDX_DIM, IDX_DIM, IDX_HEADS)
_SRC_NAMES = ("u", "v", "za", "qkv", "zb", "a", "b", "qc", "kc", "vc", "zc", "qi", "ki", "wi")
_SRC_OFF = {}
_o = 0
for _n, _w in zip(_SRC_NAMES, _SRC_SPLITS):
    _SRC_OFF[_n] = (_o, _w)
    _o += _w
D_IN = _o

_DST_ORDER = ("u", "v", "za", "qkv", "zb", "qc", "zc", "qi", "kc", "vc", "ki", "a", "b", "wi")
OFF = {}
_o = 0
for _n in _DST_ORDER:
    OFF[_n] = _o
    _o += _SRC_OFF[_n][1]
NP = ((_o + 127) // 128) * 128
OFF_TAIL = OFF["ki"]
TAIL_A = OFF["a"] - OFF_TAIL
TAIL_B = OFF["b"] - OFF_TAIL
TAIL_WI = OFF["wi"] - OFF_TAIL

MXU_COLUMNS = 256
INPROJ_TN = 6 * MXU_COLUMNS
DELTA_HEADS_PER_STEP = 8
DSA_KEY_TILE = 512

VMEM_CAPACITY_MIB = 64
VMEM_LIMIT_MIB = {
    "ada": 40, "w_in_layout": 32, "inproj": 56, "mixa": 40, "delta": 40, "dsa_prompt": 48,
    "dsa_s_score": 32, "dsa_s_select": 40, "dsa_s_attend": 40, "outproj": 48, "outproj_norm": 60,
    "kv_rows": 32,
}
assert max(VMEM_LIMIT_MIB.values()) < VMEM_CAPACITY_MIB


def _cp(name, n_axes):
    return pltpu.CompilerParams(dimension_semantics=("arbitrary",) * n_axes,
                                vmem_limit_bytes=VMEM_LIMIT_MIB[name] * 1024 * 1024)


def _silu(x):
    return (0.5 * x) * (1.0 + jnp.tanh(0.5 * x))


def _gelu(x):
    return 0.5 * x * (1.0 + jnp.tanh(0.7978845608028654 * (x + 0.044715 * (x * x * x))))


def _dot(a, b):
    return jnp.dot(a, b, preferred_element_type=F32)


def _dot_nt(a, b):
    return lax.dot_general(a, b, (((1,), (1,)), ((), ())), preferred_element_type=F32)


def _dot_tn(a, b):
    return lax.dot_general(a, b, (((0,), (0,)), ((), ())), preferred_element_type=F32)


def _b16(a):
    return a.astype(BF16)


def _ada_kernel(c_ref, w_ref, b_ref, o_ref):
    s = _silu(c_ref[...]).astype(BF16)
    o_ref[...] = _dot(s, w_ref[...].astype(BF16)) + b_ref[...]


def _ada(c_rows, w_ada, b_ada):
    depth, d, n = w_ada.shape
    r = c_rows.shape[0]
    tn = 512
    return pl.pallas_call(
        _ada_kernel,
        grid=(depth, n // tn),
        in_specs=[pl.BlockSpec((r, d), lambda l, j: (0, 0)),
                  pl.BlockSpec((None, d, tn), lambda l, j: (l, 0, j)),
                  pl.BlockSpec((None, 1, tn), lambda l, j: (l, 0, j))],
        out_specs=pl.BlockSpec((None, r, tn), lambda l, j: (l, 0, j)),
        out_shape=jax.ShapeDtypeStruct((depth, r, n), F32),
        compiler_params=_cp("ada", 2),
        name="ada",
    )(c_rows, w_ada, b_ada.reshape(depth, 1, n))


def _inproj_kernel(x_ref, g_ref, sc_ref, sh_ref, w_ref, o_ref, h_scr, *, rc):
    @pl.when(pl.program_id(1) == 0)
    def _():
        tm = x_ref.shape[0]

        def body(c, carry):
            r = pl.ds(pl.multiple_of(c * rc, rc), rc)
            x = x_ref[r, :]
            y = x * lax.rsqrt(jnp.mean(x * x, axis=-1, keepdims=True) + EPS)
            h_scr[r, :] = ((y * g_ref[...]) * (1.0 + sc_ref[...]) + sh_ref[...]).astype(BF16)
            return carry

        lax.fori_loop(0, tm // rc, body, 0)

    o_ref[...] = _dot_nt(h_scr[...], w_ref[...])


def _inproj(x2, g, mod, w_bf16, *, rows_per_batch):
    m, d = x2.shape
    n = w_bf16.shape[0]
    tn = INPROJ_TN
    tm = 512
    assert rows_per_batch % tm == 0 and m % tm == 0
    return pl.pallas_call(
        functools.partial(_inproj_kernel, rc=64),
        grid=(m // tm, pl.cdiv(n, tn)),
        in_specs=[pl.BlockSpec((tm, d), lambda i, j: (i, 0)),
                  pl.BlockSpec((1, d), lambda i, j: (0, 0)),
                  pl.BlockSpec((None, 1, d), lambda i, j: ((i * tm) // rows_per_batch, 0, 1)),
                  pl.BlockSpec((None, 1, d), lambda i, j: ((i * tm) // rows_per_batch, 0, 0)),
                  pl.BlockSpec((tn, d), lambda i, j: (j, 0))],
        out_specs=pl.BlockSpec((tm, tn), lambda i, j: (i, j)),
        out_shape=jax.ShapeDtypeStruct((m, n), F32),
        scratch_shapes=[pltpu.VMEM((tm, d), BF16)],
        compiler_params=_cp("inproj", 2),
        name="inproj",
    )(x2, g, mod, mod, w_bf16)


def _mixa_kernel(u_ref, v_ref, z_ref, vn_ref, ws_ref, bst_ref, y_ref, *rest, c, emit_va):
    u = _gelu(u_ref[...])
    v = _gelu(v_ref[...])
    mu = jnp.mean(v, axis=-1, keepdims=True)
    dv = v - mu
    va = dv * lax.rsqrt(jnp.mean(dv * dv, axis=-1, keepdims=True) + EPS) * vn_ref[...]
    if emit_va:
        rest[0][...] = va
    z = _silu(z_ref[...])
    row = lax.broadcasted_iota(I32, (c, c), 0)
    col = lax.broadcasted_iota(I32, (c, c), 1)
    tril = col <= row
    for h in range(A_HEADS):
        cols = slice(h * A_GROUP, (h + 1) * A_GROUP)
        wm = jnp.where(tril, ws_ref[h], 0.0)
        wm16 = wm.astype(BF16)
        for ci in range(u.shape[0] // c):
            rows = slice(ci * c, (ci + 1) * c)
            vh = va[rows, cols]
            if c >= 128:
                mixed = _dot(wm16, vh.astype(BF16))
            else:
                mixed = wm[:, 0:1] * vh[0:1, :]
                for s in range(1, c):
                    mixed = mixed + wm[:, s:s + 1] * vh[s:s + 1, :]
            mixed = mixed + bst_ref[:, h:h + 1]
            y_ref[rows, cols] = (u[rows, cols] * mixed * z[rows, cols]).astype(y_ref.dtype)


def _mixa(p3, a_vnorm, a_ws, a_bs, *, emit_va):
    b, l, _ = p3.shape
    c = min(A_CHUNK, l)
    r = next(k * c for k in (4, 2, 1) if l % (k * c) == 0)
    n = l // r
    ws = a_ws[:, :c, :c]
    bst = a_bs[:, :c].T
    wblk = W_A
    outs = [jax.ShapeDtypeStruct((b, l, W_A), BF16)]
    out_specs = [pl.BlockSpec((None, r, W_A), lambda i, j: (i, j, 0))]
    if emit_va:
        outs.append(jax.ShapeDtypeStruct((b, l, W_A), F32))
        out_specs.append(pl.BlockSpec((None, r, W_A), lambda i, j: (i, j, 0)))
    res = pl.pallas_call(
        functools.partial(_mixa_kernel, c=c, emit_va=emit_va),
        grid=(b, n),
        in_specs=[pl.BlockSpec((None, r, wblk), lambda i, j: (i, j, OFF["u"] // wblk)),
                  pl.BlockSpec((None, r, wblk), lambda i, j: (i, j, OFF["v"] // wblk)),
                  pl.BlockSpec((None, r, wblk), lambda i, j: (i, j, OFF["za"] // wblk)),
                  pl.BlockSpec((1, W_A), lambda i, j: (0, 0)),
                  pl.BlockSpec((A_HEADS, c, c), lambda i, j: (0, 0, 0)),
                  pl.BlockSpec((c, A_HEADS), lambda i, j: (0, 0))],
        out_specs=out_specs,
        out_shape=outs,
        compiler_params=_cp("mixa", 2),
        name="mixa",
    )(p3, p3, p3, a_vnorm.reshape(1, W_A), ws, bst)
    return res


def _delta_kernel(alog_ref, dtb_ref,
                  q_ref, k_ref, v_ref, z_ref, ab_ref,
                  cwq_ref, cwk_ref, cwv_ref, cpq_ref, cpk_ref, cpv_ref,
                  s0_ref, on_ref,
                  y_ref, sout_ref,
                  xbuf, s_scr, *, t, c, hp, valid_len):
    n = pl.program_id(2)
    dh = DN_HEAD

    @pl.when(n == 0)
    def _():
        s_scr[...] = s0_ref[...]
        for j in range(hp):
            lanes = slice(j * dh, (j + 1) * dh)
            xbuf[3 * j + 0, 5:8, :] = cpq_ref[:, lanes]
            xbuf[3 * j + 1, 5:8, :] = cpk_ref[:, lanes]
            xbuf[3 * j + 2, 5:8, :] = cpv_ref[:, lanes]

    _delta_heads(alog_ref, dtb_ref, q_ref, k_ref, v_ref, z_ref, ab_ref, cwq_ref, cwk_ref, cwv_ref,
                 on_ref, y_ref, xbuf, s_scr, t=t, c=c, hp=hp, valid_len=valid_len)

    @pl.when(n == pl.num_programs(2) - 1)
    def _():
        sout_ref[...] = s_scr[...]


def _delta_heads(alog_ref, dtb_ref, q_ref, k_ref, v_ref, z_ref, ab_ref, cwq_ref, cwk_ref, cwv_ref,
                 on_ref, y_ref, xbuf, s_scr, *, t, c, hp, valid_len):
    hg = pl.program_id(1)
    dh = DN_HEAD
    heads = list(range(hp))

    row = lax.broadcasted_iota(I32, (t, t), 0)
    col = lax.broadcasted_iota(I32, (t, t), 1)
    shift = c.bit_length() - 1
    same = (row >> shift) == (col >> shift)
    eye = row == col
    incl = same & (col <= row)
    strict = same & (col < row)
    incl_t = same & (row <= col)
    blk8 = (row >> 3) == (col >> 3)
    off_masks = []
    bs = 8
    while bs < c:
        sh_b = bs.bit_length() - 1
        inner = (row >> sh_b) == (col >> sh_b)
        outer = (row >> (sh_b + 1)) == (col >> (sh_b + 1))
        off_masks.append(outer & jnp.logical_not(inner))
        bs *= 2
    eye_f = jnp.where(eye, 1.0, 0.0)
    if valid_len < t:
        lane_valid = lax.broadcasted_iota(I32, (1, t), 1) < valid_len
        sub_valid = lax.broadcasted_iota(I32, (t, 1), 0) < valid_len

    def to_col(r):
        return jnp.sum(jnp.where(eye, r, 0.0), axis=1, keepdims=True)

    def conv(idx, x, w_ref, lanes):
        xbuf[idx, 8:8 + t, :] = x
        y = xbuf[idx, pl.ds(5, t), :] * w_ref[0:1, lanes]
        for jj in range(1, CONV_K):
            y = y + xbuf[idx, pl.ds(5 + jj, t), :] * w_ref[jj:jj + 1, lanes]
        xbuf[idx, 5:8, :] = x[t - 3:t, :]
        return _silu(y)

    lanes_of = [slice(j * dh, (j + 1) * dh) for j in range(hp)]

    def per_head(f, *lists):
        return [f(*vals) for vals in zip(*lists)]

    def l2n(a):
        return a * lax.rsqrt(jnp.sum(a * a, axis=-1, keepdims=True) + EPS)

    qc = [l2n(conv(3 * j + 0, q_ref[:, lanes_of[j]], cwq_ref, lanes_of[j])) * (dh ** -0.5) for j in heads]
    kc = [l2n(conv(3 * j + 1, k_ref[:, lanes_of[j]], cwk_ref, lanes_of[j])) for j in heads]
    v = [conv(3 * j + 2, v_ref[:, lanes_of[j]], cwv_ref, lanes_of[j]) for j in heads]

    def gates(j):
        h = hg * hp + j
        a_row = ab_ref[pl.ds(h, 1), :]
        b_row = ab_ref[pl.ds(DN_HEADS + h, 1), :]
        xa = a_row + dtb_ref[h]
        softplus = jnp.maximum(xa, 0.0) + jnp.log(1.0 + jnp.exp(-jnp.abs(xa)))
        a_coef = jnp.exp(jnp.zeros((1, 1), F32) + alog_ref[h])
        g_row = -a_coef * softplus
        beta_row = 1.0 / (1.0 + jnp.exp(-b_row))
        if valid_len < t:
            g_row = jnp.where(lane_valid, g_row, 0.0)
            beta_row = jnp.where(lane_valid, beta_row, 0.0)
        return g_row, beta_row

    g_row, beta_row = zip(*[gates(j) for j in heads])
    if valid_len < t:
        kc = per_head(lambda a: jnp.where(sub_valid, a, 0.0), kc)
        v = per_head(lambda a: jnp.where(sub_valid, a, 0.0), v)

    g_col = per_head(to_col, g_row)
    beta_col = per_head(to_col, beta_row)
    gc_col = per_head(lambda r: jnp.sum(jnp.where(incl, r, 0.0), axis=1, keepdims=True), g_row)
    glast_col = per_head(lambda r: jnp.sum(jnp.where(same, r, 0.0), axis=1, keepdims=True), g_row)
    gc_row = per_head(lambda cl: jnp.sum(jnp.where(incl_t, cl, 0.0), axis=0, keepdims=True), g_col)
    kb = per_head(lambda a, b: a * b, kc, beta_col)
    kc16 = per_head(lambda a: a.astype(BF16), kc)

    def key_products(kb_, kc16_, qc_, gc, gr):
        decay = jnp.where(incl, jnp.exp(jnp.where(incl, gc - gr, 0.0)), 0.0)
        a_mat = jnp.where(strict, _dot_nt(kb_.astype(BF16), kc16_) * decay, 0.0)
        attn16 = (_dot_nt(qc_.astype(BF16), kc16_) * decay).astype(BF16)
        n0f = jnp.where(blk8, -a_mat, 0.0)
        return attn16, _b16(n0f), eye_f + n0f, [_b16(jnp.where(om, a_mat, 0.0)) for om in off_masks]

    attn, n0, x, a_offs = zip(*per_head(key_products, kb, kc16, qc, gc_col, gc_row))
    eg = per_head(jnp.exp, gc_col)
    rhs = per_head(lambda vv, bc, kk, e: _b16(jnp.concatenate([vv * bc, kk * e], axis=1)),
                   v, beta_col, kb, eg)
    n2 = per_head(lambda a: _b16(_dot(a, a)), n0)
    n4 = per_head(lambda a: _b16(_dot(a, a)), n2)
    x = per_head(lambda xx, nn: xx + _dot(_b16(xx), nn), x, n2)
    x = per_head(lambda xx, nn: xx + _dot(_b16(xx), nn), x, n4)
    for lvl in range(len(off_masks)):
        xs = per_head(_b16, x)
        xa_off = per_head(lambda s_, offs: _b16(_dot(s_, offs[lvl])), xs, a_offs)
        x = per_head(lambda xx, xo, s_: xx - _dot(xo, s_), x, xa_off, xs)
    sol16 = per_head(lambda xx, r: _b16(_dot(_b16(xx), r)), x, rhs)
    auw = per_head(_dot, attn, sol16)
    qw = per_head(lambda a, e, m_: (a * e - m_[:, dh:]).astype(BF16), qc, eg, auw)
    kg16 = per_head(lambda a, gl_, gc: (a * jnp.exp(gl_ - gc)).astype(BF16), kc, glast_col, gc_col)

    s = [s_scr[j] for j in heads]
    outs = [[] for _ in heads]
    for i in range(t // c):
        rows = slice(i * c, (i + 1) * c)
        s16 = per_head(lambda a: a.astype(BF16), s)
        kuw = per_head(lambda a, b: _dot_tn(a[rows], b[rows]), kg16, sol16)
        o_i = per_head(lambda a, b16, m_: _dot(a[rows], b16) + m_[rows, :dh], qw, s16, auw)
        for pos in range(len(heads)):
            outs[pos].append(o_i[pos])
        gl = per_head(lambda a: jnp.exp(a[i * c:i * c + 1, :]), glast_col)
        s = per_head(lambda g_, s_, m_, b16: g_ * s_ + m_[:, :dh] - _dot(m_[:, dh:].astype(BF16), b16),
                     gl, s, kuw, s16)
    for pos, j in enumerate(heads):
        s_scr[j] = s[pos]
        o = outs[pos][0] if len(outs[pos]) == 1 else jnp.concatenate(outs[pos], axis=0)
        on = o * lax.rsqrt(jnp.mean(o * o, axis=-1, keepdims=True) + EPS) * on_ref[...]
        y_ref[:, lanes_of[j]] = (on * _silu(z_ref[:, lanes_of[j]])).astype(y_ref.dtype)


def _delta(p3, abt, conv_w, conv_prev, s0, state_layer, a_log, dt_bias, onorm, *, t, valid_len):
    b, l, _ = p3.shape
    assert l % t == 0
    c = min(DN_CHUNK, t)
    hp = DELTA_HEADS_PER_STEP
    nh = DN_HEADS
    ng = nh // hp
    w = 128 * hp
    cb_q = OFF["qkv"] // w
    cb_z = OFF["zb"] // w
    assert OFF["qkv"] % w == 0 and OFF["zb"] % w == 0 and W_B % w == 0

    def pspec(cb0):
        return pl.BlockSpec((None, t, w), lambda i, h, n, a, d: (i, n, cb0 + h))

    def cwspec(sidx):
        return pl.BlockSpec((CONV_K, w), lambda i, h, n, a, d: (0, sidx * ng + h))

    def cpspec(sidx):
        return pl.BlockSpec((None, None, CONV_K - 1, w),
                            lambda i, h, n, a, d: (state_layer, i, 0, sidx * ng + h))

    grid_spec = pltpu.PrefetchScalarGridSpec(
        num_scalar_prefetch=2,
        grid=(b, ng, l // t),
        in_specs=[pspec(cb_q), pspec(cb_q + ng), pspec(cb_q + 2 * ng), pspec(cb_z),
                  pl.BlockSpec((None, 2 * nh, t), lambda i, h, n, a, d: (i, 0, n)),
                  cwspec(0), cwspec(1), cwspec(2), cpspec(0), cpspec(1), cpspec(2),
                  pl.BlockSpec((None, None, hp, DN_HEAD, DN_HEAD),
                               lambda i, h, n, a, d: (state_layer, i, h, 0, 0)),
                  pl.BlockSpec((1, DN_HEAD), lambda i, h, n, a, d: (0, 0))],
        out_specs=[pl.BlockSpec((None, t, w), lambda i, h, n, a, d: (i, n, h)),
                   pl.BlockSpec((None, hp, DN_HEAD, DN_HEAD), lambda i, h, n, a, d: (i, h, 0, 0))],
        scratch_shapes=[pltpu.VMEM((3 * hp, t + 8, 128), F32), pltpu.VMEM((hp, DN_HEAD, DN_HEAD), F32)],
    )
    return pl.pallas_call(
        functools.partial(_delta_kernel, t=t, c=c, hp=hp, valid_len=valid_len),
        grid_spec=grid_spec,
        out_shape=[jax.ShapeDtypeStruct((b, l, W_B), BF16),
                   jax.ShapeDtypeStruct((b, nh, DN_HEAD, DN_HEAD), F32)],
        compiler_params=_cp("delta", 3),
        name="delta",
    )(a_log, dt_bias, p3, p3, p3, p3, abt, conv_w, conv_w, conv_w,
      conv_prev, conv_prev, conv_prev, s0, onorm.reshape(1, DN_HEAD))


def _ordered_bits_to_float(u):
    key = u ^ jnp.int32(INT_MIN)
    bits = jnp.where(key < 0, key ^ jnp.int32(0x7FFFFFFF), key)
    return pltpu.bitcast(bits, F32)


def _topk_threshold(count_ge, shape, topk):
    def bit_body(i, u):
        bit = jnp.left_shift(jnp.int32(1), 31 - i)
        cand_u = u | bit
        cnt = count_ge(_ordered_bits_to_float(cand_u))
        return jnp.where(cnt >= float(topk), cand_u, u)

    u = lax.fori_loop(0, 32, bit_body, jnp.zeros(shape, I32))
    return _ordered_bits_to_float(u)


def _softmax_update(m_ref, l_ref, acc_ref, idx, s, v_tiles):
    m_old = m_ref[idx]
    m_new = jnp.maximum(m_old, jnp.max(s, axis=1, keepdims=True))
    m_safe = jnp.where(m_new == NEG_INF, 0.0, m_new)
    alpha = jnp.exp(m_old - m_safe)
    p = jnp.exp(s - m_safe)
    l_ref[idx] = alpha * l_ref[idx] + jnp.sum(p, axis=1, keepdims=True)
    acc = alpha * acc_ref[idx]
    p16 = p.astype(BF16)
    k0 = 0
    for vt in v_tiles:
        acc = acc + _dot(p16[:, k0:k0 + vt.shape[0]], vt)
        k0 += vt.shape[0]
    acc_ref[idx] = acc
    m_ref[idx] = m_new


def _dsa_prompt_kernel(qi_ref, tq_ref, tall_ref, qc_ref, k_ref, v_ref, z_ref, y_ref,
                       sc_scr, m_scr, l_scr, acc_scr, *, l, topk, kt):
    qb = pl.program_id(1)
    nq = Q_BLOCK
    n_tiles = (qb * nq + nq + kt - 1) // kt
    n_pairs = C_HEADS // 2
    w_rows = tq_ref[...].T[TAIL_WI:TAIL_WI + IDX_HEADS, :] * ((IDX_HEADS ** -0.5) * (IDX_DIM ** -0.5))
    rq = []
    for j in range(IDX_HEADS // 2):
        a = qi_ref[:, j * 128:(j + 1) * 128]
        rq.append(jnp.concatenate([a, pltpu.roll(a, IDX_DIM, 1)], axis=0).astype(BF16))
    kpos = lax.broadcasted_iota(I32, (kt, nq), 0)
    qpos = qb * nq + lax.broadcasted_iota(I32, (kt, nq), 1)
    lane = lax.broadcasted_iota(I32, (kt, 128), 1)

    def score_tile(ti, carry):
        k0 = pl.multiple_of(ti * kt, kt)
        ki = jnp.where(lane < IDX_DIM, tall_ref[pl.ds(k0, kt), :], 0.0).astype(BF16)
        score = jnp.zeros((kt, nq), F32)
        for j in range(IDX_HEADS // 2):
            lg = _dot_nt(ki, rq[j])
            score = score + w_rows[2 * j:2 * j + 1, :] * jnp.maximum(lg[:, :nq], 0.0)
            score = score + w_rows[2 * j + 1:2 * j + 2, :] * jnp.maximum(lg[:, nq:], 0.0)
        sc_scr[pl.ds(k0, kt), :] = jnp.where(kpos + k0 <= qpos, score, NEG_INF)
        return carry

    lax.fori_loop(0, n_tiles, score_tile, 0)

    ct = 256
    n_ct = (qb * nq + nq + ct - 1) // ct
    n_acc = 4

    def count(cand, strict):
        cand8 = jnp.broadcast_to(cand, (8, nq))

        def tile_body(ti, accs):
            k0 = pl.multiple_of(ti * ct, ct)
            tile = sc_scr[pl.ds(k0, ct), :]
            accs = list(accs)
            for r in range(ct // 8):
                blk = tile[8 * r:8 * r + 8, :]
                hit = (blk > cand8) if strict else (blk >= cand8)
                accs[r % n_acc] = accs[r % n_acc] + jnp.where(hit, 1.0, 0.0)
            return tuple(accs)

        accs = lax.fori_loop(0, n_ct, tile_body, tuple(jnp.zeros((8, nq), F32) for _ in range(n_acc)))
        acc = (accs[0] + accs[1]) + (accs[2] + accs[3])
        return jnp.sum(acc, axis=0, keepdims=True)

    thr = _topk_threshold(lambda cand: count(cand, False), (1, nq), topk)
    keep_all = (qb * nq + lax.broadcasted_iota(I32, (1, nq), 1)) < topk
    need = float(topk) - count(thr, True)
    tri16 = jnp.where(lax.broadcasted_iota(I32, (128, 128), 1) <= lax.broadcasted_iota(I32, (128, 128), 0),
                      1.0, 0.0).astype(BF16)

    def select_tile(ti, seen):
        k0 = pl.multiple_of(ti * kt, kt)
        sc = sc_scr[pl.ds(k0, kt), :]
        eq = sc == thr
        eq_f = jnp.where(eq, 1.0, 0.0)
        eq16 = eq_f.astype(BF16)
        prefs = [_dot(tri16, eq16[r:r + 128]) for r in range(0, kt, 128)]
        ranks = []
        for pref in prefs:
            ranks.append(seen + pref)
            seen = seen + pref[127:128, :]
        tie_ok = (jnp.concatenate(ranks, axis=0) - eq_f) < need
        chosen = jnp.where(sc > thr, 0.0, jnp.where(eq, jnp.where(tie_ok, 0.0, NEG_INF), NEG_INF))
        sc_scr[pl.ds(k0, kt), :] = jnp.where(keep_all, jnp.where(sc > NEG_INF, 0.0, NEG_INF), chosen)
        return seen

    lax.fori_loop(0, n_tiles, select_tile, jnp.zeros((1, nq), F32))

    m_scr[...] = jnp.full(m_scr.shape, NEG_INF, F32)
    l_scr[...] = jnp.zeros(l_scr.shape, F32)
    acc_scr[...] = jnp.zeros(acc_scr.shape, F32)
    scale = (C_HEAD ** -0.5) * 1.4426950408889634
    qp = []
    for pr in range(n_pairs):
        c0 = slice(2 * pr * C_HEAD, (2 * pr + 1) * C_HEAD)
        c1 = slice((2 * pr + 1) * C_HEAD, (2 * pr + 2) * C_HEAD)
        qp.append(jnp.concatenate([qc_ref[:, c0] * scale, qc_ref[:, c1] * scale], axis=0).astype(BF16))
    pairs_per_kv = n_pairs // C_KV_HEADS

    def attend_tile(ti, carry):
        k0 = pl.multiple_of(ti * kt, kt)
        bias = sc_scr[pl.ds(k0, kt), :]
        k16 = [k_ref[pl.ds(k0, kt), hk * C_HEAD:(hk + 1) * C_HEAD].astype(BF16) for hk in range(C_KV_HEADS)]
        vt16 = [v_ref[pl.ds(k0, kt), hk * C_HEAD:(hk + 1) * C_HEAD].T.astype(BF16) for hk in range(C_KV_HEADS)]
        bias2 = jnp.concatenate([bias, bias], axis=1)
        pr_all = range(n_pairs)
        s = [_dot_nt(k16[pr // pairs_per_kv], qp[pr]) + bias2 for pr in pr_all]
        m_old = [m_scr[pr] for pr in pr_all]
        m_new = [jnp.maximum(m_old[pr], jnp.max(s[pr], axis=0, keepdims=True)) for pr in pr_all]
        m_safe = [jnp.where(m_new[pr] == NEG_INF, 0.0, m_new[pr]) for pr in pr_all]
        alpha = [jnp.exp2(m_old[pr] - m_safe[pr]) for pr in pr_all]
        p = [jnp.exp2(s[pr] - m_safe[pr]) for pr in pr_all]
        for pr in pr_all:
            l_scr[pr] = alpha[pr] * l_scr[pr] + jnp.sum(p[pr], axis=0, keepdims=True)
            acc_scr[pr] = alpha[pr] * acc_scr[pr] + _dot(vt16[pr // pairs_per_kv], p[pr].astype(BF16))
            m_scr[pr] = m_new[pr]
        return carry

    lax.fori_loop(0, n_tiles, attend_tile, 0)
    for pr in range(n_pairs):
        o_t = acc_scr[pr] / l_scr[pr]
        for e in range(2):
            cols = slice((2 * pr + e) * C_HEAD, (2 * pr + e + 1) * C_HEAD)
            o = o_t[:, e * nq:(e + 1) * nq].T
            y_ref[:, cols] = (o * _silu(z_ref[:, cols])).astype(y_ref.dtype)


def _dsa_prompt(p3):
    b, l, _ = p3.shape
    assert l % Q_BLOCK == 0
    topk = min(TOPK_MAX, l // 4)
    nq = Q_BLOCK
    kt = DSA_KEY_TILE if l % DSA_KEY_TILE == 0 else 256
    assert l % kt == 0
    n_qb = l // nq
    return pl.pallas_call(
        functools.partial(_dsa_prompt_kernel, l=l, topk=topk, kt=kt),
        grid=(b, n_qb),
        in_specs=[pl.BlockSpec((None, nq, W_C), lambda i, j: (i, j, OFF["qi"] // W_C)),
                  pl.BlockSpec((None, nq, 128), lambda i, j: (i, j, OFF_TAIL // 128)),
                  pl.BlockSpec((None, l, 128), lambda i, j: (i, 0, OFF_TAIL // 128)),
                  pl.BlockSpec((None, nq, W_C), lambda i, j: (i, j, OFF["qc"] // W_C)),
                  pl.BlockSpec((None, l, C_KV), lambda i, j: (i, 0, OFF["kc"] // C_KV)),
                  pl.BlockSpec((None, l, C_KV), lambda i, j: (i, 0, OFF["vc"] // C_KV)),
                  pl.BlockSpec((None, nq, W_C), lambda i, j: (i, j, OFF["zc"] // W_C))],
        out_specs=pl.BlockSpec((None, nq, W_C), lambda i, j: (i, j, 0)),
        out_shape=jax.ShapeDtypeStruct((b, l, W_C), BF16),
        scratch_shapes=[pltpu.VMEM((l, nq), F32),
                        pltpu.VMEM((C_HEADS // 2, 1, 2 * nq), F32), pltpu.VMEM((C_HEADS // 2, 1, 2 * nq), F32),
                        pltpu.VMEM((C_HEADS // 2, C_HEAD, 2 * nq), F32)],
        compiler_params=_cp("dsa_prompt", 2),
        name="dsa_prompt",
    )(p3, p3, p3, p3, p3, p3, p3)


def _idx_scores(logits, wcol, t):
    r = jnp.maximum(logits, 0.0) * wcol
    return jnp.sum(r.reshape(t, IDX_HEADS, logits.shape[1]), axis=1)


def _dsa_s_score_kernel(pt_ref, q_ref, w_ref, kn_ref, *refs, pg, t):
    pages = refs[:pg]
    out_ref, new_ref = refs[pg:]
    q16 = q_ref[...].astype(BF16)
    wcol = w_ref[...] * ((IDX_HEADS ** -0.5) * (IDX_DIM ** -0.5))
    keys16 = jnp.concatenate([pages[i][...] for i in range(pg)], axis=1).astype(BF16)
    out_ref[...] = _idx_scores(_dot(q16, keys16), wcol, t)

    @pl.when(pl.program_id(1) == 0)
    def _():
        new_ref[...] = _idx_scores(_dot_nt(q16, kn_ref[...].astype(BF16)), wcol, t)


def _dsa_s_select_kernel(sc_ref, scn_ref, tq_ref, bias_ref, *, nk, topk):
    rows = sc_ref.shape[0]
    tq = tq_ref[...]
    new_ok = lax.broadcasted_iota(I32, (rows, 128), 1) <= tq
    sc_new = jnp.where(new_ok, scn_ref[...], NEG_INF)
    sc_past = sc_ref[...]

    def count_ge(cand):
        c1 = jnp.sum(jnp.where(sc_past >= cand, 1.0, 0.0), axis=1, keepdims=True)
        c2 = jnp.sum(jnp.where(sc_new >= cand, 1.0, 0.0), axis=1, keepdims=True)
        return c1 + c2

    thr = _topk_threshold(count_ge, (rows, 1), topk)
    keep_all = (nk + 1 + tq[:, 0:1]) <= topk
    n_gt = (jnp.sum(jnp.where(sc_past > thr, 1.0, 0.0), axis=1, keepdims=True)
            + jnp.sum(jnp.where(sc_new > thr, 1.0, 0.0), axis=1, keepdims=True))
    need = float(topk) - n_gt
    tri16 = jnp.where(lax.broadcasted_iota(I32, (128, 128), 0) <= lax.broadcasted_iota(I32, (128, 128), 1),
                      1.0, 0.0).astype(BF16)

    def bias_blocks(scs, seen):
        eqs = [sc == thr for sc in scs]
        eq_fs = [jnp.where(eq, 1.0, 0.0) for eq in eqs]
        prefs = [_dot(eq_f.astype(BF16), tri16) for eq_f in eq_fs]
        out = []
        for sc, eq, eq_f, pref in zip(scs, eqs, eq_fs, prefs):
            tie_ok = (seen + pref - eq_f) < need
            chosen = jnp.where(sc > thr, 0.0, jnp.where(eq, jnp.where(tie_ok, 0.0, NEG_INF), NEG_INF))
            out.append(jnp.where(keep_all, jnp.where(sc > NEG_INF, 0.0, NEG_INF), chosen))
            seen = seen + pref[:, 127:128]
        return out, seen

    n_blk = nk // 128
    grp = 8 if n_blk % 8 == 0 else 1

    def past_group(gi, seen):
        lanes = [pl.ds(pl.multiple_of((gi * grp + i) * 128, 128), 128) for i in range(grp)]
        out, seen = bias_blocks([sc_ref[:, ln] for ln in lanes], seen)
        for ln, bias in zip(lanes, out):
            bias_ref[:, ln] = bias
        return seen

    seen = lax.fori_loop(0, n_blk // grp, past_group, jnp.zeros((rows, 1), F32))
    (bias_new,), _ = bias_blocks([sc_new], seen)
    bias_ref[:, nk:] = bias_new


def _dsa_s_attend_kernel(pt_ref, q_ref, bias_ref, biasn_ref, kn_ref, vn_ref, z_ref, *refs, pg):
    kpages = refs[:pg]
    vpages = refs[pg:2 * pg]
    o_ref = refs[2 * pg]
    m_scr, l_scr, acc_scr = refs[2 * pg + 1:]
    g = pl.program_id(1)
    scale = C_HEAD ** -0.5
    page = kpages[0].shape[0] // C_KV_HEADS

    @pl.when(g == 0)
    def _():
        m_scr[...] = jnp.full(m_scr.shape, NEG_INF, F32)
        l_scr[...] = jnp.zeros(l_scr.shape, F32)
        acc_scr[...] = jnp.zeros(acc_scr.shape, F32)
        for hk in range(C_KV_HEADS):
            cols = slice(hk * C_HEAD, (hk + 1) * C_HEAD)
            q16 = (q_ref[hk] * scale).astype(BF16)
            s = _dot_nt(q16, kn_ref[:, cols].astype(BF16)) + biasn_ref[...]
            _softmax_update(m_scr, l_scr, acc_scr, hk, s, [vn_ref[:, cols].astype(BF16)])

    def two_pages(refs_, i, rows):
        return jnp.concatenate([refs_[i][rows, :], refs_[i + 1][rows, :]], axis=0).astype(BF16)

    for hk in range(C_KV_HEADS):
        q16 = (q_ref[hk] * scale).astype(BF16)
        rows = pl.ds(hk, page, stride=C_KV_HEADS)
        tiles = [_dot_nt(q16, two_pages(kpages, i, rows)) for i in range(0, pg, 2)]
        s = jnp.concatenate(tiles, axis=1) + bias_ref[...]
        _softmax_update(m_scr, l_scr, acc_scr, hk, s, [two_pages(vpages, i, rows) for i in range(0, pg, 2)])

    @pl.when(g == pl.num_programs(1) - 1)
    def _():
        for hk in range(C_KV_HEADS):
            o_ref[hk] = (acc_scr[hk] / l_scr[hk]) * _silu(z_ref[hk])


def _dsa_sample(p3s, cache_k, cache_v, cache_kidx, page_table, layer):
    b, t, _ = p3s.shape
    n_pages = page_table.shape[1]
    page = cache_k.shape[2]
    assert page == 128
    past = n_pages * page
    topk = min(TOPK_MAX, (past + t) // 4)
    pg = 32 if n_pages % 32 == 0 else (16 if n_pages % 16 == 0 else n_pages)
    assert pg % 2 == 0
    ng = n_pages // pg
    rows = t * C_GROUPS

    qi = p3s[:, :, OFF["qi"]:OFF["qi"] + IDX_HEADS * IDX_DIM].reshape(b, t * IDX_HEADS, IDX_DIM)
    wi = p3s[:, :, OFF_TAIL + TAIL_WI:OFF_TAIL + TAIL_WI + IDX_HEADS].reshape(b, t * IDX_HEADS, 1)
    ki_new = jnp.pad(p3s[:, :, OFF_TAIL:OFF_TAIL + IDX_DIM], ((0, 0), (0, 128 - t), (0, 0)))
    k_new = jnp.pad(p3s[:, :, OFF["kc"]:OFF["kc"] + C_KV], ((0, 0), (0, 128 - t), (0, 0)))
    v_new = jnp.pad(p3s[:, :, OFF["vc"]:OFF["vc"] + C_KV], ((0, 0), (0, 128 - t), (0, 0)))

    def heads_major(a):
        a = a.reshape(b, t, C_KV_HEADS, C_GROUPS, C_HEAD)
        return jnp.transpose(a, (0, 2, 1, 3, 4)).reshape(b, C_KV_HEADS, rows, C_HEAD)

    qh = heads_major(p3s[:, :, OFF["qc"]:OFF["qc"] + W_C])
    zh = heads_major(p3s[:, :, OFF["zc"]:OFF["zc"] + W_C])

    kidx_t = jnp.swapaxes(cache_kidx, 2, 3)
    score_spec = pltpu.PrefetchScalarGridSpec(
        num_scalar_prefetch=1,
        grid=(b, ng),
        in_specs=[pl.BlockSpec((None, t * IDX_HEADS, IDX_DIM), lambda i, g, pt: (i, 0, 0)),
                  pl.BlockSpec((None, t * IDX_HEADS, 1), lambda i, g, pt: (i, 0, 0)),
                  pl.BlockSpec((None, 128, IDX_DIM), lambda i, g, pt: (i, 0, 0))]
                 + [pl.BlockSpec((None, None, IDX_DIM, page),
                                 lambda i, g, pt, j=j: (layer, pt[i, g * pg + j], 0, 0)) for j in range(pg)],
        out_specs=[pl.BlockSpec((None, t, pg * page), lambda i, g, pt: (i, 0, g)),
                   pl.BlockSpec((None, t, 128), lambda i, g, pt: (i, 0, 0))],
    )
    scores, scores_new = pl.pallas_call(
        functools.partial(_dsa_s_score_kernel, pg=pg, t=t),
        grid_spec=score_spec,
        out_shape=[jax.ShapeDtypeStruct((b, t, past), F32), jax.ShapeDtypeStruct((b, t, 128), F32)],
        compiler_params=_cp("dsa_s_score", 2),
        name="dsa_s_score",
    )(page_table, qi, wi, ki_new, *([kidx_t] * pg))

    tq = jnp.broadcast_to(jnp.tile(jnp.arange(t, dtype=I32), b)[:, None], (b * t, 128))
    bias = pl.pallas_call(
        functools.partial(_dsa_s_select_kernel, nk=past, topk=topk),
        grid=(1,),
        in_specs=[pl.BlockSpec((b * t, past), lambda i: (0, 0)),
                  pl.BlockSpec((b * t, 128), lambda i: (0, 0)),
                  pl.BlockSpec((b * t, 128), lambda i: (0, 0))],
        out_specs=pl.BlockSpec((b * t, past + 128), lambda i: (0, 0)),
        out_shape=jax.ShapeDtypeStruct((b * t, past + 128), F32),
        compiler_params=_cp("dsa_s_select", 1),
        name="dsa_s_select",
    )(scores.reshape(b * t, past), scores_new.reshape(b * t, 128), tq).reshape(b, t, past + 128)

    bias_rows = jnp.repeat(bias, C_GROUPS, axis=1)
    bias_past = bias_rows[:, :, :past]
    bias_new = bias_rows[:, :, past:]

    ck = cache_k.reshape(cache_k.shape[0], cache_k.shape[1], page * C_KV_HEADS, C_HEAD)
    cv = cache_v.reshape(cache_v.shape[0], cache_v.shape[1], page * C_KV_HEADS, C_HEAD)
    pspec = [pl.BlockSpec((None, None, page * C_KV_HEADS, C_HEAD),
                          lambda i, g, pt, j=j: (layer, pt[i, g * pg + j], 0, 0)) for j in range(pg)]
    attend_spec = pltpu.PrefetchScalarGridSpec(
        num_scalar_prefetch=1,
        grid=(b, ng),
        in_specs=[pl.BlockSpec((None, C_KV_HEADS, rows, C_HEAD), lambda i, g, pt: (i, 0, 0, 0)),
                  pl.BlockSpec((None, rows, pg * page), lambda i, g, pt: (i, 0, g)),
                  pl.BlockSpec((None, rows, 128), lambda i, g, pt: (i, 0, 0)),
                  pl.BlockSpec((None, 128, C_KV), lambda i, g, pt: (i, 0, 0)),
                  pl.BlockSpec((None, 128, C_KV), lambda i, g, pt: (i, 0, 0)),
                  pl.BlockSpec((None, C_KV_HEADS, rows, C_HEAD), lambda i, g, pt: (i, 0, 0, 0))]
                 + pspec + pspec,
        out_specs=pl.BlockSpec((None, C_KV_HEADS, rows, C_HEAD), lambda i, g, pt: (i, 0, 0, 0)),
        scratch_shapes=[pltpu.VMEM((C_KV_HEADS, rows, 1), F32), pltpu.VMEM((C_KV_HEADS, rows, 1), F32),
                        pltpu.VMEM((C_KV_HEADS, rows, C_HEAD), F32)],
    )
    oh = pl.pallas_call(
        functools.partial(_dsa_s_attend_kernel, pg=pg),
        grid_spec=attend_spec,
        out_shape=jax.ShapeDtypeStruct((b, C_KV_HEADS, rows, C_HEAD), F32),
        compiler_params=_cp("dsa_s_attend", 2),
        name="dsa_s_attend",
    )(page_table, qh, bias_past, bias_new, k_new, v_new, zh, *([ck] * pg), *([cv] * pg))
    y = jnp.transpose(oh.reshape(b, C_KV_HEADS, t, C_GROUPS, C_HEAD), (0, 2, 1, 3, 4)).reshape(b, t, W_C)
    return y.astype(BF16)


def _outproj_kernel(ya_ref, yb_ref, yc_ref, w_ref, x_ref, gate_ref, o_ref, mix_scr):
    @pl.when(pl.program_id(1) == 0)
    def _():
        mix_scr[:, 0:W_A] = ya_ref[...]
        mix_scr[:, W_A:W_A + W_B] = yb_ref[...]
        mix_scr[:, W_A + W_B:] = yc_ref[...]

    o_ref[...] = x_ref[...] + gate_ref[...] * _dot(mix_scr[...], w_ref[...])


def _outproj_norm_kernel(ya_ref, yb_ref, yc_ref, w_ref, x_ref, gate_ref, g_ref, o_ref, mix_scr, ssq_scr):
    j = pl.program_id(1)
    tn = x_ref.shape[1]

    @pl.when(j == 0)
    def _():
        mix_scr[:, 0:W_A] = ya_ref[...]
        mix_scr[:, W_A:W_A + W_B] = yb_ref[...]
        mix_scr[:, W_A + W_B:] = yc_ref[...]
        ssq_scr[...] = jnp.zeros(ssq_scr.shape, F32)

    x_new = x_ref[...] + gate_ref[...] * _dot(mix_scr[...], w_ref[...])
    o_ref[:, pl.ds(pl.multiple_of(j * tn, tn), tn)] = x_new
    ssq_scr[...] += jnp.sum(x_new * x_new, axis=-1, keepdims=True)

    @pl.when(j == pl.num_programs(1) - 1)
    def _():
        tm, d = o_ref.shape
        rc = 64 if tm % 64 == 0 else tm

        def body(c, carry):
            r = pl.ds(pl.multiple_of(c * rc, rc), rc)
            o_ref[r, :] = o_ref[r, :] * lax.rsqrt(ssq_scr[r, :] / d + EPS) * g_ref[...]
            return carry

        lax.fori_loop(0, tm // rc, body, 0)


def _outproj(ya, yb, yc, w_bf16, layer, x2, gate, *, rows_per_batch, mod_row0, g_final=None):
    m, d = x2.shape
    tn = 1024
    if gate.ndim == 3:
        tm = 512
        gate_spec = pl.BlockSpec((None, 1, tn),
                                 lambda i, j: ((i * tm) // rows_per_batch + mod_row0, 0, 2 * (d // tn) + j))
    else:
        tm = m
        gate_spec = pl.BlockSpec((tm, tn), lambda i, j: (i, j))
    assert m % tm == 0
    in_specs = [pl.BlockSpec((tm, W_A), lambda i, j: (i, 0)),
                pl.BlockSpec((tm, W_B), lambda i, j: (i, 0)),
                pl.BlockSpec((tm, W_C), lambda i, j: (i, 0)),
                pl.BlockSpec((None, d, tn), lambda i, j: (layer, 0, j)),
                pl.BlockSpec((tm, tn), lambda i, j: (i, j)),
                gate_spec]
    operands = [ya, yb, yc, w_bf16, x2, gate]
    scratch = [pltpu.VMEM((tm, d), BF16)]
    if g_final is None:
        body, name = _outproj_kernel, "outproj"
        out_spec = pl.BlockSpec((tm, tn), lambda i, j: (i, j))
    else:
        body, name = _outproj_norm_kernel, "outproj_norm"
        in_specs.append(pl.BlockSpec((1, d), lambda i, j: (0, 0)))
        operands.append(g_final.reshape(1, d))
        scratch.append(pltpu.VMEM((tm, 1), F32))
        out_spec = pl.BlockSpec((tm, d), lambda i, j: (i, 0))
    return pl.pallas_call(
        body,
        grid=(m // tm, d // tn),
        in_specs=in_specs,
        out_specs=out_spec,
        out_shape=jax.ShapeDtypeStruct((m, d), F32),
        scratch_shapes=scratch,
        compiler_params=_cp(name, 2),
        name=name,
    )(*operands)


def _kv_rows_kernel(*refs, depth, tm):
    k_refs, v_refs = refs[:depth], refs[depth:2 * depth]
    pk_ref, pv_ref = refs[2 * depth:]
    for lyr in range(depth):
        for h in range(C_KV_HEADS):
            cols = slice(h * C_HEAD, (h + 1) * C_HEAD)
            rows = pl.ds(h, tm, stride=C_KV_HEADS)
            pk_ref.at[lyr][rows, :] = k_refs[lyr][:, cols]
            pv_ref.at[lyr][rows, :] = v_refs[lyr][:, cols]


def _kv_rows(p3_layers):
    depth = len(p3_layers)
    b, l, _ = p3_layers[0].shape
    tm = 512 if l % 512 == 0 else l
    kspec = pl.BlockSpec((None, tm, C_KV), lambda i, j: (i, j, OFF["kc"] // C_KV))
    vspec = pl.BlockSpec((None, tm, C_KV), lambda i, j: (i, j, OFF["vc"] // C_KV))
    ospec = pl.BlockSpec((depth, None, tm * C_KV_HEADS, C_HEAD), lambda i, j: (0, i, j, 0))
    oshape = jax.ShapeDtypeStruct((depth, b, l * C_KV_HEADS, C_HEAD), F32)
    pk, pv = pl.pallas_call(
        functools.partial(_kv_rows_kernel, depth=depth, tm=tm),
        grid=(b, l // tm),
        in_specs=[kspec] * depth + [vspec] * depth,
        out_specs=[ospec, ospec],
        out_shape=[oshape, oshape],
        compiler_params=_cp("kv_rows", 2),
        name="kv_rows",
    )(*p3_layers, *p3_layers)
    shape = (depth, b, l, C_KV_HEADS, C_HEAD)
    return pk.reshape(shape), pv.reshape(shape)


def _w_in_tile_table():
    n_t = NP // 128
    table, n_plain = [], None
    for dt in range(n_t - 1):
        c = dt * 128
        seg = [n for n in _DST_ORDER if OFF[n] <= c < OFF[n] + _SRC_OFF[n][1]][0]
        src = _SRC_OFF[seg][0] + (c - OFF[seg])
        if src % 128 == 0:
            assert n_plain is None
        else:
            assert src % 128 == 32
            if n_plain is None:
                n_plain = dt
        table.append(src // 128)
    t_kw, t_ab = _SRC_OFF["ki"][0] // 128, _SRC_OFF["a"][0] // 128
    assert _SRC_OFF["ki"][0] % 128 == 32 and _SRC_OFF["wi"][0] == t_kw * 128 + TAIL_WI
    assert _SRC_OFF["a"][0] % 128 == 0 and _SRC_OFF["b"][0] == _SRC_OFF["a"][0] + DN_HEADS
    assert TAIL_A == IDX_DIM and TAIL_B == TAIL_A + DN_HEADS and TAIL_WI == TAIL_B + DN_HEADS
    tile_a = table + [t_kw]
    rows_b = [t_ab * 4] * n_plain + [(s + 1) * 4 for s in table[n_plain:]] + [t_ab * 4]
    return tile_a, rows_b, n_plain


def _wprep_kernel(ta_ref, tb_ref, a_ref, b_ref, xs_ref, g_ref, sc_ref, sh_ref, o_ref, ps_ref, h_scr,
                  *, n_plain, n_t):
    dt = pl.program_id(0)
    dt_o = o_ref.dtype

    @pl.when(dt == 0)
    def _():
        x = xs_ref[...]
        y = x * lax.rsqrt(jnp.mean(x * x, axis=-1, keepdims=True) + EPS)
        h_scr[...] = ((y * g_ref[...]) * (1.0 + sc_ref[...]) + sh_ref[...]).astype(BF16)

    @pl.when(dt < n_plain)
    def _():
        o_ref[...] = a_ref[...].astype(dt_o)

    @pl.when((dt >= n_plain) & (dt < n_t - 1))
    def _():
        o_ref[0:96, :] = a_ref[32:128, :].astype(dt_o)
        o_ref[96:128, :] = b_ref[...].astype(dt_o)

    @pl.when(dt == n_t - 1)
    def _():
        o_ref[0:TAIL_A, :] = a_ref[32:32 + IDX_DIM, :].astype(dt_o)
        o_ref[TAIL_A:TAIL_WI, :] = b_ref[...].astype(dt_o)
        o_ref[TAIL_WI:TAIL_WI + IDX_HEADS, :] = a_ref[TAIL_WI:TAIL_WI + IDX_HEADS, :].astype(dt_o)
        o_ref[TAIL_WI + IDX_HEADS:, :] = jnp.zeros((128 - TAIL_WI - IDX_HEADS, o_ref.shape[1]), dt_o)

    ps_ref[...] = _dot_nt(h_scr[...], o_ref[...])


def _permute_w_in(w_in, layer, xs, g, sc, sh):
    _, d, _ = w_in.shape
    ms = xs.shape[0]
    w_t = jnp.swapaxes(w_in, 1, 2)
    tile_a, rows_b, n_plain = _w_in_tile_table()
    n_t = NP // 128
    const = lambda t, ta, tb: (0, 0)
    grid_spec = pltpu.PrefetchScalarGridSpec(
        num_scalar_prefetch=2,
        grid=(n_t,),
        in_specs=[pl.BlockSpec((None, 128, d), lambda t, ta, tb: (layer, ta[t], 0)),
                  pl.BlockSpec((None, 32, d), lambda t, ta, tb: (layer, tb[t], 0)),
                  pl.BlockSpec((ms, d), const), pl.BlockSpec((1, d), const),
                  pl.BlockSpec((ms, d), const), pl.BlockSpec((ms, d), const)],
        out_specs=[pl.BlockSpec((128, d), lambda t, ta, tb: (t, 0)),
                   pl.BlockSpec((ms, 128), lambda t, ta, tb: (0, t))],
        scratch_shapes=[pltpu.VMEM((ms, d), BF16)],
    )
    return pl.pallas_call(
        functools.partial(_wprep_kernel, n_plain=n_plain, n_t=n_t),
        grid_spec=grid_spec,
        out_shape=[jax.ShapeDtypeStruct((NP, d), BF16), jax.ShapeDtypeStruct((ms, NP), F32)],
        compiler_params=_cp("w_in_layout", 1),
        name="w_in_layout",
    )(jnp.asarray(tile_a, I32), jnp.asarray(rows_b, I32), w_t, w_t, xs, g, sc, sh)


def _delta_t(l):
    for t in (128, 64):
        if l % t == 0:
            return t
    raise ValueError("sequence length must be a multiple of 64")


def kernel(x_prompt, x_sample, cache_k, cache_v, cache_kidx, state_dn, state_conv, page_table,
           c_prompt, c_sample, w_ada, b_ada, g_norm, w_in, a_vnorm, a_ws, a_bs, dn_conv_w,
           dn_a_log, dn_dt_bias, dn_onorm, w_out, g_final):
    bp, lp, d = x_prompt.shape
    bs, ls, _ = x_sample.shape
    depth = w_ada.shape[0]
    assert d == D_MODEL and w_in.shape[2] == D_IN
    assert CONV_K - 1 <= ls <= DN_CHUNK

    n_c = bp + bs
    c_rows = jnp.concatenate([c_prompt, c_sample], axis=0)
    r_pad = (-n_c) % 8
    if r_pad:
        c_rows = jnp.pad(c_rows, ((0, r_pad), (0, 0)))
    m_all = _ada(c_rows, w_ada, b_ada)

    xp = x_prompt.reshape(bp * lp, d)
    xs = x_sample.reshape(bs * ls, d)
    ls_pad = 8 * ((ls + 7) // 8)
    outs = {k: [] for k in ("p3", "pki", "pdn", "pconv", "sk", "sv", "ski", "sdn", "sconv", "samlp")}
    zeros_conv = jnp.zeros((1, bp, CONV_K - 1, DN_CONV_DIM), F32)
    zeros_state = jnp.zeros((1, bp, DN_HEADS, DN_HEAD, DN_HEAD), F32)

    w_out_all = w_out.astype(BF16)

    def g_last(layer):
        return g_final if layer == depth - 1 else None

    for l in range(depth):
        g_l = g_norm[l].reshape(1, d)
        m_l = m_all[l]
        m3 = m_l.reshape(m_l.shape[0], 1, 3 * d)
        ms = jnp.repeat(m_l[bp:bp + bs], ls, axis=0)

        w_in_l, ps = _permute_w_in(w_in, l, xs, g_l, ms[:, d:2 * d], ms[:, 0:d])

        pp = _inproj(xp, g_l, m3, w_in_l, rows_per_batch=lp)
        p3 = pp.reshape(bp, lp, NP)
        (ya,) = _mixa(p3, a_vnorm[l], a_ws[l], a_bs[l], emit_va=False)
        abt = jnp.transpose(p3[:, :, OFF["a"]:OFF["a"] + 2 * DN_HEADS], (0, 2, 1))
        yb, s_p = _delta(p3, abt, dn_conv_w[l], zeros_conv, zeros_state, 0, dn_a_log[l], dn_dt_bias[l],
                         dn_onorm[l], t=_delta_t(lp), valid_len=_delta_t(lp))
        yc = _dsa_prompt(p3)
        xp = _outproj(ya.reshape(bp * lp, W_A), yb.reshape(bp * lp, W_B), yc.reshape(bp * lp, W_C),
                      w_out_all, l, xp, m3, rows_per_batch=lp, mod_row0=0, g_final=g_last(l))
        outs["p3"].append(p3)
        outs["pki"].append(p3[:, :, OFF_TAIL:OFF_TAIL + IDX_DIM])
        outs["pdn"].append(s_p)
        outs["pconv"].append(p3[:, lp - (CONV_K - 1):, OFF["qkv"]:OFF["qkv"] + DN_CONV_DIM])

        p3s = ps.reshape(bs, ls, NP)
        ya_s, va_s = _mixa(p3s, a_vnorm[l], a_ws[l], a_bs[l], emit_va=True)
        p3s_pad = jnp.pad(p3s, ((0, 0), (0, ls_pad - ls), (0, 0)))
        abt_s = jnp.transpose(p3s_pad[:, :, OFF["a"]:OFF["a"] + 2 * DN_HEADS], (0, 2, 1))
        yb_s, s_s = _delta(p3s_pad, abt_s, dn_conv_w[l], state_conv, state_dn, l, dn_a_log[l],
                           dn_dt_bias[l], dn_onorm[l], t=ls_pad, valid_len=ls)
        yc_s = _dsa_sample(p3s, cache_k, cache_v, cache_kidx, page_table, l)
        xs = _outproj(ya_s.reshape(bs * ls, W_A), yb_s[:, :ls].reshape(bs * ls, W_B),
                      yc_s.reshape(bs * ls, W_C), w_out_all, l, xs, ms[:, 2 * d:3 * d],
                      rows_per_batch=ls, mod_row0=bp, g_final=g_last(l))
        outs["sk"].append(p3s[:, :, OFF["kc"]:OFF["kc"] + C_KV].reshape(bs, ls, C_KV_HEADS, C_HEAD))
        outs["sv"].append(p3s[:, :, OFF["vc"]:OFF["vc"] + C_KV].reshape(bs, ls, C_KV_HEADS, C_HEAD))
        outs["ski"].append(p3s[:, :, OFF_TAIL:OFF_TAIL + IDX_DIM])
        outs["sdn"].append(s_s)
        outs["sconv"].append(p3s[:, ls - (CONV_K - 1):, OFF["qkv"]:OFF["qkv"] + DN_CONV_DIM])
        outs["samlp"].append(va_s)

    y_prompt = xp.reshape(bp, lp, d)
    y_sample = xs.reshape(bs, ls, d)
    st = jnp.stack
    p_k, p_v = _kv_rows(outs["p3"])
    return (y_prompt, y_sample, p_k, p_v, st(outs["pki"]), st(outs["pdn"]),
            st(outs["pconv"]), st(outs["sk"]), st(outs["sv"]), st(outs["ski"]), st(outs["sdn"]),
            st(outs["sconv"]), st(outs["samlp"]))
```

```python
import functools

import jax
import jax.numpy as jnp
from jax import lax
from jax.experimental import pallas as pl
from jax.experimental.pallas import tpu as pltpu

F32 = jnp.float32
BF16 = jnp.bfloat16
I32 = jnp.int32
EPS = 1e-6
INT_MIN = -(2 ** 31)
NEG_INF = float("-inf")

D_MODEL = 4096
W_A = D_MODEL // 4
A_GROUP = 128
A_HEADS = W_A // A_GROUP
A_CHUNK = 128
W_B = D_MODEL // 2
DN_HEAD = 128
DN_HEADS = W_B // DN_HEAD
CONV_K = 4
DN_CONV_DIM = 3 * W_B
DN_CHUNK = 64
W_C = D_MODEL - W_A - W_B
C_HEAD = 128
C_HEADS = W_C // C_HEAD
C_KV_HEADS = 2
C_GROUPS = C_HEADS // C_KV_HEADS
C_KV = C_KV_HEADS * C_HEAD
IDX_HEADS = 16
IDX_DIM = 64
TOPK_MAX = 256
Q_BLOCK = 512

_SRC_SPLITS = (W_A, W_A, W_A, DN_CONV_DIM, W_B, DN_HEADS, DN_HEADS,
               W_C, C_KV, C_KV, W_C, IDX_HEADS * IDX_DIM, IDX_DIM, IDX_HEADS)
_SRC_NAMES = ("u", "v", "za", "qkv", "zb", "a", "b", "qc", "kc", "vc", "zc", "qi", "ki", "wi")
_SRC_OFF = {}
_o = 0
for _n, _w in zip(_SRC_NAMES, _SRC_SPLITS):
    _SRC_OFF[_n] = (_o, _w)
    _o += _w
D_IN = _o

_DST_ORDER = ("u", "v", "za", "qkv", "zb", "qc", "zc", "qi", "kc", "vc", "ki", "a", "b", "wi")
OFF = {}
_o = 0
for _n in _DST_ORDER:
    OFF[_n] = _o
    _o += _SRC_OFF[_n][1]
NP = ((_o + 127) // 128) * 128
OFF_TAIL = OFF["ki"]
TAIL_A = OFF["a"] - OFF_TAIL
TAIL_B = OFF["b"] - OFF_TAIL
TAIL_WI = OFF["wi"] - OFF_TAIL

MXU_COLUMNS = 256
INPROJ_TN = 6 * MXU_COLUMNS
DELTA_HEADS_PER_STEP = 8
DSA_KEY_TILE = 512

VMEM_CAPACITY_MIB = 64
VMEM_LIMIT_MIB = {
    "ada": 40, "w_in_layout": 32, "inproj": 56, "mixa": 32, "delta": 40, "dsa_prompt": 48,
    "dsa_s_score": 32, "dsa_s_select": 40, "dsa_s_attend": 40, "outproj": 48, "outproj_norm": 60,
    "kv_rows": 32,
}
assert max(VMEM_LIMIT_MIB.values()) < VMEM_CAPACITY_MIB


def _cp(name, n_axes):
    return pltpu.CompilerParams(dimension_semantics=("arbitrary",) * n_axes,
                                vmem_limit_bytes=VMEM_LIMIT_MIB[name] * 1024 * 1024)


def _silu(x):
    return (0.5 * x) * (1.0 + jnp.tanh(0.5 * x))


def _gelu(x):
    return 0.5 * x * (1.0 + jnp.tanh(0.7978845608028654 * (x + 0.044715 * (x * x * x))))


def _dot(a, b):
    return jnp.dot(a, b, preferred_element_type=F32)


def _dot_nt(a, b):
    return lax.dot_general(a, b, (((1,), (1,)), ((), ())), preferred_element_type=F32)


def _dot_tn(a, b):
    return lax.dot_general(a, b, (((0,), (0,)), ((), ())), preferred_element_type=F32)


def _b16(a):
    return a.astype(BF16)


def _ada_kernel(c_ref, w_ref, b_ref, o_ref):
    s = _silu(c_ref[...]).astype(BF16)
    o_ref[...] = _dot(s, w_ref[...].astype(BF16)) + b_ref[...]


def _ada(c_rows, w_ada, b_ada):
    depth, d, n = w_ada.shape
    r = c_rows.shape[0]
    tn = 512
    return pl.pallas_call(
        _ada_kernel,
        grid=(depth, n // tn),
        in_specs=[pl.BlockSpec((r, d), lambda l, j: (0, 0)),
                  pl.BlockSpec((None, d, tn), lambda l, j: (l, 0, j)),
                  pl.BlockSpec((None, 1, tn), lambda l, j: (l, 0, j))],
        out_specs=pl.BlockSpec((None, r, tn), lambda l, j: (l, 0, j)),
        out_shape=jax.ShapeDtypeStruct((depth, r, n), F32),
        compiler_params=_cp("ada", 2),
        name="ada",
    )(c_rows, w_ada, b_ada.reshape(depth, 1, n))


def _inproj_kernel(x_ref, g_ref, sc_ref, sh_ref, w_ref, o_ref, h_scr, *, rc):
    @pl.when(pl.program_id(1) == 0)
    def _():
        tm = x_ref.shape[0]

        def body(c, carry):
            r = pl.ds(pl.multiple_of(c * rc, rc), rc)
            x = x_ref[r, :]
            y = x * lax.rsqrt(jnp.mean(x * x, axis=-1, keepdims=True) + EPS)
            h_scr[r, :] = ((y * g_ref[...]) * (1.0 + sc_ref[...]) + sh_ref[...]).astype(BF16)
            return carry

        lax.fori_loop(0, tm // rc, body, 0)

    o_ref[...] = _dot_nt(h_scr[...], w_ref[...])


def _inproj(x2, g, mod, w_bf16, *, rows_per_batch):
    m, d = x2.shape
    n = w_bf16.shape[0]
    tn = INPROJ_TN
    tm = 512
    assert rows_per_batch % tm == 0 and m % tm == 0
    return pl.pallas_call(
        functools.partial(_inproj_kernel, rc=64),
        grid=(m // tm, pl.cdiv(n, tn)),
        in_specs=[pl.BlockSpec((tm, d), lambda i, j: (i, 0)),
                  pl.BlockSpec((1, d), lambda i, j: (0, 0)),
                  pl.BlockSpec((None, 1, d), lambda i, j: ((i * tm) // rows_per_batch, 0, 1)),
                  pl.BlockSpec((None, 1, d), lambda i, j: ((i * tm) // rows_per_batch, 0, 0)),
                  pl.BlockSpec((tn, d), lambda i, j: (j, 0))],
        out_specs=pl.BlockSpec((tm, tn), lambda i, j: (i, j)),
        out_shape=jax.ShapeDtypeStruct((m, n), F32),
        scratch_shapes=[pltpu.VMEM((tm, d), BF16)],
        compiler_params=_cp("inproj", 2),
        name="inproj",
    )(x2, g, mod, mod, w_bf16)


def _mixa_kernel(u_ref, v_ref, z_ref, vn_ref, ws_ref, bst_ref, y_ref, *rest, c, emit_va):
    u = _gelu(u_ref[...])
    v = _gelu(v_ref[...])
    mu = jnp.mean(v, axis=-1, keepdims=True)
    dv = v - mu
    va = dv * lax.rsqrt(jnp.mean(dv * dv, axis=-1, keepdims=True) + EPS) * vn_ref[...]
    if emit_va:
        rest[0][...] = va
    z = _silu(z_ref[...])
    row = lax.broadcasted_iota(I32, (c, c), 0)
    col = lax.broadcasted_iota(I32, (c, c), 1)
    tril = col <= row
    for h in range(A_HEADS):
        cols = slice(h * A_GROUP, (h + 1) * A_GROUP)
        wm = jnp.where(tril, ws_ref[h], 0.0)
        wm16 = wm.astype(BF16)
        for ci in range(u.shape[0] // c):
            rows = slice(ci * c, (ci + 1) * c)
            vh = va[rows, cols]
            if c >= 128:
                mixed = _dot(wm16, vh.astype(BF16))
            else:
                mixed = wm[:, 0:1] * vh[0:1, :]
                for s in range(1, c):
                    mixed = mixed + wm[:, s:s + 1] * vh[s:s + 1, :]
            mixed = mixed + bst_ref[:, h:h + 1]
            y_ref[rows, cols] = (u[rows, cols] * mixed * z[rows, cols]).astype(y_ref.dtype)


def _mixa(p3, a_vnorm, a_ws, a_bs, *, emit_va):
    b, l, _ = p3.shape
    c = min(A_CHUNK, l)
    r = 2 * c if l % (2 * c) == 0 else c
    n = l // r
    ws = a_ws[:, :c, :c]
    bst = a_bs[:, :c].T
    wblk = W_A
    outs = [jax.ShapeDtypeStruct((b, l, W_A), BF16)]
    out_specs = [pl.BlockSpec((None, r, W_A), lambda i, j: (i, j, 0))]
    if emit_va:
        outs.append(jax.ShapeDtypeStruct((b, l, W_A), F32))
        out_specs.append(pl.BlockSpec((None, r, W_A), lambda i, j: (i, j, 0)))
    res = pl.pallas_call(
        functools.partial(_mixa_kernel, c=c, emit_va=emit_va),
        grid=(b, n),
        in_specs=[pl.BlockSpec((None, r, wblk), lambda i, j: (i, j, OFF["u"] // wblk)),
                  pl.BlockSpec((None, r, wblk), lambda i, j: (i, j, OFF["v"] // wblk)),
                  pl.BlockSpec((None, r, wblk), lambda i, j: (i, j, OFF["za"] // wblk)),
                  pl.BlockSpec((1, W_A), lambda i, j: (0, 0)),
                  pl.BlockSpec((A_HEADS, c, c), lambda i, j: (0, 0, 0)),
                  pl.BlockSpec((c, A_HEADS), lambda i, j: (0, 0))],
        out_specs=out_specs,
        out_shape=outs,
        compiler_params=_cp("mixa", 2),
        name="mixa",
    )(p3, p3, p3, a_vnorm.reshape(1, W_A), ws, bst)
    return res


def _delta_kernel(alog_ref, dtb_ref,
                  q_ref, k_ref, v_ref, z_ref, ab_ref,
                  cwq_ref, cwk_ref, cwv_ref, cpq_ref, cpk_ref, cpv_ref,
                  s0_ref, on_ref,
                  y_ref, sout_ref,
                  xbuf, s_scr, *, t, c, hp, valid_len):
    n = pl.program_id(2)
    dh = DN_HEAD

    @pl.when(n == 0)
    def _():
        s_scr[...] = s0_ref[...]
        for j in range(hp):
            lanes = slice(j * dh, (j + 1) * dh)
            xbuf[3 * j + 0, 5:8, :] = cpq_ref[:, lanes]
            xbuf[3 * j + 1, 5:8, :] = cpk_ref[:, lanes]
            xbuf[3 * j + 2, 5:8, :] = cpv_ref[:, lanes]

    _delta_heads(alog_ref, dtb_ref, q_ref, k_ref, v_ref, z_ref, ab_ref, cwq_ref, cwk_ref, cwv_ref,
                 on_ref, y_ref, xbuf, s_scr, t=t, c=c, hp=hp, valid_len=valid_len)

    @pl.when(n == pl.num_programs(2) - 1)
    def _():
        sout_ref[...] = s_scr[...]


def _delta_heads(alog_ref, dtb_ref, q_ref, k_ref, v_ref, z_ref, ab_ref, cwq_ref, cwk_ref, cwv_ref,
                 on_ref, y_ref, xbuf, s_scr, *, t, c, hp, valid_len):
    hg = pl.program_id(1)
    dh = DN_HEAD
    heads = list(range(hp))

    row = lax.broadcasted_iota(I32, (t, t), 0)
    col = lax.broadcasted_iota(I32, (t, t), 1)
    shift = c.bit_length() - 1
    same = (row >> shift) == (col >> shift)
    eye = row == col
    incl = same & (col <= row)
    strict = same & (col < row)
    incl_t = same & (row <= col)
    blk8 = (row >> 3) == (col >> 3)
    off_masks = []
    bs = 8
    while bs < c:
        sh_b = bs.bit_length() - 1
        inner = (row >> sh_b) == (col >> sh_b)
        outer = (row >> (sh_b + 1)) == (col >> (sh_b + 1))
        off_masks.append(outer & jnp.logical_not(inner))
        bs *= 2
    eye_f = jnp.where(eye, 1.0, 0.0)
    if valid_len < t:
        lane_valid = lax.broadcasted_iota(I32, (1, t), 1) < valid_len
        sub_valid = lax.broadcasted_iota(I32, (t, 1), 0) < valid_len

    def to_col(r):
        return jnp.sum(jnp.where(eye, r, 0.0), axis=1, keepdims=True)

    def conv(idx, x, w_ref, lanes):
        xbuf[idx, 8:8 + t, :] = x
        y = xbuf[idx, pl.ds(5, t), :] * w_ref[0:1, lanes]
        for jj in range(1, CONV_K):
            y = y + xbuf[idx, pl.ds(5 + jj, t), :] * w_ref[jj:jj + 1, lanes]
        xbuf[idx, 5:8, :] = x[t - 3:t, :]
        return _silu(y)

    lanes_of = [slice(j * dh, (j + 1) * dh) for j in range(hp)]

    def per_head(f, *lists):
        return [f(*vals) for vals in zip(*lists)]

    def l2n(a):
        return a * lax.rsqrt(jnp.sum(a * a, axis=-1, keepdims=True) + EPS)

    qc = [l2n(conv(3 * j + 0, q_ref[:, lanes_of[j]], cwq_ref, lanes_of[j])) * (dh ** -0.5) for j in heads]
    kc = [l2n(conv(3 * j + 1, k_ref[:, lanes_of[j]], cwk_ref, lanes_of[j])) for j in heads]
    v = [conv(3 * j + 2, v_ref[:, lanes_of[j]], cwv_ref, lanes_of[j]) for j in heads]

    def gates(j):
        h = hg * hp + j
        a_row = ab_ref[pl.ds(h, 1), :]
        b_row = ab_ref[pl.ds(DN_HEADS + h, 1), :]
        xa = a_row + dtb_ref[h]
        softplus = jnp.maximum(xa, 0.0) + jnp.log(1.0 + jnp.exp(-jnp.abs(xa)))
        a_coef = jnp.exp(jnp.zeros((1, 1), F32) + alog_ref[h])
        g_row = -a_coef * softplus
        beta_row = 1.0 / (1.0 + jnp.exp(-b_row))
        if valid_len < t:
            g_row = jnp.where(lane_valid, g_row, 0.0)
            beta_row = jnp.where(lane_valid, beta_row, 0.0)
        return g_row, beta_row

    g_row, beta_row = zip(*[gates(j) for j in heads])
    if valid_len < t:
        kc = per_head(lambda a: jnp.where(sub_valid, a, 0.0), kc)
        v = per_head(lambda a: jnp.where(sub_valid, a, 0.0), v)

    g_col = per_head(to_col, g_row)
    beta_col = per_head(to_col, beta_row)
    gc_col = per_head(lambda r: jnp.sum(jnp.where(incl, r, 0.0), axis=1, keepdims=True), g_row)
    glast_col = per_head(lambda r: jnp.sum(jnp.where(same, r, 0.0), axis=1, keepdims=True), g_row)
    gc_row = per_head(lambda cl: jnp.sum(jnp.where(incl_t, cl, 0.0), axis=0, keepdims=True), g_col)
    kb = per_head(lambda a, b: a * b, kc, beta_col)
    kc16 = per_head(lambda a: a.astype(BF16), kc)

    def key_products(kb_, kc16_, qc_, gc, gr):
        decay = jnp.where(incl, jnp.exp(jnp.where(incl, gc - gr, 0.0)), 0.0)
        a_mat = jnp.where(strict, _dot_nt(kb_.astype(BF16), kc16_) * decay, 0.0)
        attn16 = (_dot_nt(qc_.astype(BF16), kc16_) * decay).astype(BF16)
        n0f = jnp.where(blk8, -a_mat, 0.0)
        return attn16, _b16(n0f), eye_f + n0f, [_b16(jnp.where(om, a_mat, 0.0)) for om in off_masks]

    attn, n0, x, a_offs = zip(*per_head(key_products, kb, kc16, qc, gc_col, gc_row))
    eg = per_head(jnp.exp, gc_col)
    rhs = per_head(lambda vv, bc, kk, e: _b16(jnp.concatenate([vv * bc, kk * e], axis=1)),
                   v, beta_col, kb, eg)
    n2 = per_head(lambda a: _b16(_dot(a, a)), n0)
    n4 = per_head(lambda a: _b16(_dot(a, a)), n2)
    x = per_head(lambda xx, nn: xx + _dot(_b16(xx), nn), x, n2)
    x = per_head(lambda xx, nn: xx + _dot(_b16(xx), nn), x, n4)
    for lvl in range(len(off_masks)):
        xs = per_head(_b16, x)
        xa_off = per_head(lambda s_, offs: _b16(_dot(s_, offs[lvl])), xs, a_offs)
        x = per_head(lambda xx, xo, s_: xx - _dot(xo, s_), x, xa_off, xs)
    sol16 = per_head(lambda xx, r: _b16(_dot(_b16(xx), r)), x, rhs)
    auw = per_head(_dot, attn, sol16)
    qw = per_head(lambda a, e, m_: (a * e - m_[:, dh:]).astype(BF16), qc, eg, auw)
    kg16 = per_head(lambda a, gl_, gc: (a * jnp.exp(gl_ - gc)).astype(BF16), kc, glast_col, gc_col)

    s = [s_scr[j] for j in heads]
    outs = [[] for _ in heads]
    for i in range(t // c):
        rows = slice(i * c, (i + 1) * c)
        s16 = per_head(lambda a: a.astype(BF16), s)
        kuw = per_head(lambda a, b: _dot_tn(a[rows], b[rows]), kg16, sol16)
        o_i = per_head(lambda a, b16, m_: _dot(a[rows], b16) + m_[rows, :dh], qw, s16, auw)
        for pos in range(len(heads)):
            outs[pos].append(o_i[pos])
        gl = per_head(lambda a: jnp.exp(a[i * c:i * c + 1, :]), glast_col)
        s = per_head(lambda g_, s_, m_, b16: g_ * s_ + m_[:, :dh] - _dot(m_[:, dh:].astype(BF16), b16),
                     gl, s, kuw, s16)
    for pos, j in enumerate(heads):
        s_scr[j] = s[pos]
        o = outs[pos][0] if len(outs[pos]) == 1 else jnp.concatenate(outs[pos], axis=0)
        on = o * lax.rsqrt(jnp.mean(o * o, axis=-1, keepdims=True) + EPS) * on_ref[...]
        y_ref[:, lanes_of[j]] = (on * _silu(z_ref[:, lanes_of[j]])).astype(y_ref.dtype)


def _delta(p3, abt, conv_w, conv_prev, s0, state_layer, a_log, dt_bias, onorm, *, t, valid_len):
    b, l, _ = p3.shape
    assert l % t == 0
    c = min(DN_CHUNK, t)
    hp = DELTA_HEADS_PER_STEP
    nh = DN_HEADS
    ng = nh // hp
    w = 128 * hp
    cb_q = OFF["qkv"] // w
    cb_z = OFF["zb"] // w
    assert OFF["qkv"] % w == 0 and OFF["zb"] % w == 0 and W_B % w == 0

    def pspec(cb0):
        return pl.BlockSpec((None, t, w), lambda i, h, n, a, d: (i, n, cb0 + h))

    def cwspec(sidx):
        return pl.BlockSpec((CONV_K, w), lambda i, h, n, a, d: (0, sidx * ng + h))

    def cpspec(sidx):
        return pl.BlockSpec((None, None, CONV_K - 1, w),
                            lambda i, h, n, a, d: (state_layer, i, 0, sidx * ng + h))

    grid_spec = pltpu.PrefetchScalarGridSpec(
        num_scalar_prefetch=2,
        grid=(b, ng, l // t),
        in_specs=[pspec(cb_q), pspec(cb_q + ng), pspec(cb_q + 2 * ng), pspec(cb_z),
                  pl.BlockSpec((None, 2 * nh, t), lambda i, h, n, a, d: (i, 0, n)),
                  cwspec(0), cwspec(1), cwspec(2), cpspec(0), cpspec(1), cpspec(2),
                  pl.BlockSpec((None, None, hp, DN_HEAD, DN_HEAD),
                               lambda i, h, n, a, d: (state_layer, i, h, 0, 0)),
                  pl.BlockSpec((1, DN_HEAD), lambda i, h, n, a, d: (0, 0))],
        out_specs=[pl.BlockSpec((None, t, w), lambda i, h, n, a, d: (i, n, h)),
                   pl.BlockSpec((None, hp, DN_HEAD, DN_HEAD), lambda i, h, n, a, d: (i, h, 0, 0))],
        scratch_shapes=[pltpu.VMEM((3 * hp, t + 8, 128), F32), pltpu.VMEM((hp, DN_HEAD, DN_HEAD), F32)],
    )
    return pl.pallas_call(
        functools.partial(_delta_kernel, t=t, c=c, hp=hp, valid_len=valid_len),
        grid_spec=grid_spec,
        out_shape=[jax.ShapeDtypeStruct((b, l, W_B), BF16),
                   jax.ShapeDtypeStruct((b, nh, DN_HEAD, DN_HEAD), F32)],
        compiler_params=_cp("delta", 3),
        name="delta",
    )(a_log, dt_bias, p3, p3, p3, p3, abt, conv_w, conv_w, conv_w,
      conv_prev, conv_prev, conv_prev, s0, onorm.reshape(1, DN_HEAD))


def _ordered_bits_to_float(u):
    key = u ^ jnp.int32(INT_MIN)
    bits = jnp.where(key < 0, key ^ jnp.int32(0x7FFFFFFF), key)
    return pltpu.bitcast(bits, F32)


def _topk_threshold(count_ge, shape, topk):
    def bit_body(i, u):
        bit = jnp.left_shift(jnp.int32(1), 31 - i)
        cand_u = u | bit
        cnt = count_ge(_ordered_bits_to_float(cand_u))
        return jnp.where(cnt >= float(topk), cand_u, u)

    u = lax.fori_loop(0, 32, bit_body, jnp.zeros(shape, I32))
    return _ordered_bits_to_float(u)


def _softmax_update(m_ref, l_ref, acc_ref, idx, s, v_tiles):
    m_old = m_ref[idx]
    m_new = jnp.maximum(m_old, jnp.max(s, axis=1, keepdims=True))
    m_safe = jnp.where(m_new == NEG_INF, 0.0, m_new)
    alpha = jnp.exp(m_old - m_safe)
    p = jnp.exp(s - m_safe)
    l_ref[idx] = alpha * l_ref[idx] + jnp.sum(p, axis=1, keepdims=True)
    acc = alpha * acc_ref[idx]
    p16 = p.astype(BF16)
    k0 = 0
    for vt in v_tiles:
        acc = acc + _dot(p16[:, k0:k0 + vt.shape[0]], vt)
        k0 += vt.shape[0]
    acc_ref[idx] = acc
    m_ref[idx] = m_new


def _dsa_prompt_kernel(qi_ref, tq_ref, tall_ref, qc_ref, k_ref, v_ref, z_ref, y_ref,
                       sc_scr, m_scr, l_scr, acc_scr, *, l, topk, kt):
    qb = pl.program_id(1)
    nq = Q_BLOCK
    n_tiles = (qb * nq + nq + kt - 1) // kt
    n_pairs = C_HEADS // 2
    w_rows = tq_ref[...].T[TAIL_WI:TAIL_WI + IDX_HEADS, :] * ((IDX_HEADS ** -0.5) * (IDX_DIM ** -0.5))
    rq = []
    for j in range(IDX_HEADS // 2):
        a = qi_ref[:, j * 128:(j + 1) * 128]
        rq.append(jnp.concatenate([a, pltpu.roll(a, IDX_DIM, 1)], axis=0).astype(BF16))
    kpos = lax.broadcasted_iota(I32, (kt, nq), 0)
    qpos = qb * nq + lax.broadcasted_iota(I32, (kt, nq), 1)
    lane = lax.broadcasted_iota(I32, (kt, 128), 1)

    def score_tile(ti, carry):
        k0 = pl.multiple_of(ti * kt, kt)
        ki = jnp.where(lane < IDX_DIM, tall_ref[pl.ds(k0, kt), :], 0.0).astype(BF16)
        score = jnp.zeros((kt, nq), F32)
        for j in range(IDX_HEADS // 2):
            lg = _dot_nt(ki, rq[j])
            score = score + w_rows[2 * j:2 * j + 1, :] * jnp.maximum(lg[:, :nq], 0.0)
            score = score + w_rows[2 * j + 1:2 * j + 2, :] * jnp.maximum(lg[:, nq:], 0.0)
        sc_scr[pl.ds(k0, kt), :] = jnp.where(kpos + k0 <= qpos, score, NEG_INF)
        return carry

    lax.fori_loop(0, n_tiles, score_tile, 0)

    ct = 256
    n_ct = (qb * nq + nq + ct - 1) // ct
    n_acc = 4

    def count(cand, strict):
        cand8 = jnp.broadcast_to(cand, (8, nq))

        def tile_body(ti, accs):
            k0 = pl.multiple_of(ti * ct, ct)
            tile = sc_scr[pl.ds(k0, ct), :]
            accs = list(accs)
            for r in range(ct // 8):
                blk = tile[8 * r:8 * r + 8, :]
                hit = (blk > cand8) if strict else (blk >= cand8)
                accs[r % n_acc] = accs[r % n_acc] + jnp.where(hit, 1.0, 0.0)
            return tuple(accs)

        accs = lax.fori_loop(0, n_ct, tile_body, tuple(jnp.zeros((8, nq), F32) for _ in range(n_acc)))
        acc = (accs[0] + accs[1]) + (accs[2] + accs[3])
        return jnp.sum(acc, axis=0, keepdims=True)

    thr = _topk_threshold(lambda cand: count(cand, False), (1, nq), topk)
    keep_all = (qb * nq + lax.broadcasted_iota(I32, (1, nq), 1)) < topk
    need = float(topk) - count(thr, True)
    tri16 = jnp.where(lax.broadcasted_iota(I32, (128, 128), 1) <= lax.broadcasted_iota(I32, (128, 128), 0),
                      1.0, 0.0).astype(BF16)

    def select_tile(ti, seen):
        k0 = pl.multiple_of(ti * kt, kt)
        sc = sc_scr[pl.ds(k0, kt), :]
        eq = sc == thr
        eq_f = jnp.where(eq, 1.0, 0.0)
        eq16 = eq_f.astype(BF16)
        prefs = [_dot(tri16, eq16[r:r + 128]) for r in range(0, kt, 128)]
        ranks = []
        for pref in prefs:
            ranks.append(seen + pref)
            seen = seen + pref[127:128, :]
        tie_ok = (jnp.concatenate(ranks, axis=0) - eq_f) < need
        chosen = jnp.where(sc > thr, 0.0, jnp.where(eq, jnp.where(tie_ok, 0.0, NEG_INF), NEG_INF))
        sc_scr[pl.ds(k0, kt), :] = jnp.where(keep_all, jnp.where(sc > NEG_INF, 0.0, NEG_INF), chosen)
        return seen

    lax.fori_loop(0, n_tiles, select_tile, jnp.zeros((1, nq), F32))

    m_scr[...] = jnp.full(m_scr.shape, NEG_INF, F32)
    l_scr[...] = jnp.zeros(l_scr.shape, F32)
    acc_scr[...] = jnp.zeros(acc_scr.shape, F32)
    scale = (C_HEAD ** -0.5) * 1.4426950408889634
    qp = []
    for pr in range(n_pairs):
        c0 = slice(2 * pr * C_HEAD, (2 * pr + 1) * C_HEAD)
        c1 = slice((2 * pr + 1) * C_HEAD, (2 * pr + 2) * C_HEAD)
        qp.append(jnp.concatenate([qc_ref[:, c0] * scale, qc_ref[:, c1] * scale], axis=0).astype(BF16))
    pairs_per_kv = n_pairs // C_KV_HEADS

    def attend_tile(ti, carry):
        k0 = pl.multiple_of(ti * kt, kt)
        bias = sc_scr[pl.ds(k0, kt), :]
        k16 = [k_ref[pl.ds(k0, kt), hk * C_HEAD:(hk + 1) * C_HEAD].astype(BF16) for hk in range(C_KV_HEADS)]
        vt16 = [v_ref[pl.ds(k0, kt), hk * C_HEAD:(hk + 1) * C_HEAD].T.astype(BF16) for hk in range(C_KV_HEADS)]
        bias2 = jnp.concatenate([bias, bias], axis=1)
        pr_all = range(n_pairs)
        s = [_dot_nt(k16[pr // pairs_per_kv], qp[pr]) + bias2 for pr in pr_all]
        m_old = [m_scr[pr] for pr in pr_all]
        m_new = [jnp.maximum(m_old[pr], jnp.max(s[pr], axis=0, keepdims=True)) for pr in pr_all]
        m_safe = [jnp.where(m_new[pr] == NEG_INF, 0.0, m_new[pr]) for pr in pr_all]
        alpha = [jnp.exp2(m_old[pr] - m_safe[pr]) for pr in pr_all]
        p = [jnp.exp2(s[pr] - m_safe[pr]) for pr in pr_all]
        for pr in pr_all:
            l_scr[pr] = alpha[pr] * l_scr[pr] + jnp.sum(p[pr], axis=0, keepdims=True)
            acc_scr[pr] = alpha[pr] * acc_scr[pr] + _dot(vt16[pr // pairs_per_kv], p[pr].astype(BF16))
            m_scr[pr] = m_new[pr]
        return carry

    lax.fori_loop(0, n_tiles, attend_tile, 0)
    for pr in range(n_pairs):
        o_t = acc_scr[pr] / l_scr[pr]
        for e in range(2):
            cols = slice((2 * pr + e) * C_HEAD, (2 * pr + e + 1) * C_HEAD)
            o = o_t[:, e * nq:(e + 1) * nq].T
            y_ref[:, cols] = (o * _silu(z_ref[:, cols])).astype(y_ref.dtype)


def _dsa_prompt(p3):
    b, l, _ = p3.shape
    assert l % Q_BLOCK == 0
    topk = min(TOPK_MAX, l // 4)
    nq = Q_BLOCK
    kt = DSA_KEY_TILE if l % DSA_KEY_TILE == 0 else 256
    assert l % kt == 0
    n_qb = l // nq
    return pl.pallas_call(
        functools.partial(_dsa_prompt_kernel, l=l, topk=topk, kt=kt),
        grid=(b, n_qb),
        in_specs=[pl.BlockSpec((None, nq, W_C), lambda i, j: (i, j, OFF["qi"] // W_C)),
                  pl.BlockSpec((None, nq, 128), lambda i, j: (i, j, OFF_TAIL // 128)),
                  pl.BlockSpec((None, l, 128), lambda i, j: (i, 0, OFF_TAIL // 128)),
                  pl.BlockSpec((None, nq, W_C), lambda i, j: (i, j, OFF["qc"] // W_C)),
                  pl.BlockSpec((None, l, C_KV), lambda i, j: (i, 0, OFF["kc"] // C_KV)),
                  pl.BlockSpec((None, l, C_KV), lambda i, j: (i, 0, OFF["vc"] // C_KV)),
                  pl.BlockSpec((None, nq, W_C), lambda i, j: (i, j, OFF["zc"] // W_C))],
        out_specs=pl.BlockSpec((None, nq, W_C), lambda i, j: (i, j, 0)),
        out_shape=jax.ShapeDtypeStruct((b, l, W_C), BF16),
        scratch_shapes=[pltpu.VMEM((l, nq), F32),
                        pltpu.VMEM((C_HEADS // 2, 1, 2 * nq), F32), pltpu.VMEM((C_HEADS // 2, 1, 2 * nq), F32),
                        pltpu.VMEM((C_HEADS // 2, C_HEAD, 2 * nq), F32)],
        compiler_params=_cp("dsa_prompt", 2),
        name="dsa_prompt",
    )(p3, p3, p3, p3, p3, p3, p3)


def _idx_scores(logits, wcol, t):
    r = jnp.maximum(logits, 0.0) * wcol
    return jnp.sum(r.reshape(t, IDX_HEADS, logits.shape[1]), axis=1)


def _dsa_s_score_kernel(pt_ref, q_ref, w_ref, kn_ref, *refs, pg, t):
    pages = refs[:pg]
    out_ref, new_ref = refs[pg:]
    q16 = q_ref[...].astype(BF16)
    wcol = w_ref[...] * ((IDX_HEADS ** -0.5) * (IDX_DIM ** -0.5))
    keys16 = jnp.concatenate([pages[i][...] for i in range(pg)], axis=1).astype(BF16)
    out_ref[...] = _idx_scores(_dot(q16, keys16), wcol, t)

    @pl.when(pl.program_id(1) == 0)
    def _():
        new_ref[...] = _idx_scores(_dot_nt(q16, kn_ref[...].astype(BF16)), wcol, t)


def _dsa_s_select_kernel(sc_ref, scn_ref, tq_ref, bias_ref, *, nk, topk):
    rows = sc_ref.shape[0]
    tq = tq_ref[...]
    new_ok = lax.broadcasted_iota(I32, (rows, 128), 1) <= tq
    sc_new = jnp.where(new_ok, scn_ref[...], NEG_INF)
    sc_past = sc_ref[...]

    def count_ge(cand):
        c1 = jnp.sum(jnp.where(sc_past >= cand, 1.0, 0.0), axis=1, keepdims=True)
        c2 = jnp.sum(jnp.where(sc_new >= cand, 1.0, 0.0), axis=1, keepdims=True)
        return c1 + c2

    thr = _topk_threshold(count_ge, (rows, 1), topk)
    keep_all = (nk + 1 + tq[:, 0:1]) <= topk
    n_gt = (jnp.sum(jnp.where(sc_past > thr, 1.0, 0.0), axis=1, keepdims=True)
            + jnp.sum(jnp.where(sc_new > thr, 1.0, 0.0), axis=1, keepdims=True))
    need = float(topk) - n_gt
    tri16 = jnp.where(lax.broadcasted_iota(I32, (128, 128), 0) <= lax.broadcasted_iota(I32, (128, 128), 1),
                      1.0, 0.0).astype(BF16)

    def bias_blocks(scs, seen):
        eqs = [sc == thr for sc in scs]
        eq_fs = [jnp.where(eq, 1.0, 0.0) for eq in eqs]
        prefs = [_dot(eq_f.astype(BF16), tri16) for eq_f in eq_fs]
        out = []
        for sc, eq, eq_f, pref in zip(scs, eqs, eq_fs, prefs):
            tie_ok = (seen + pref - eq_f) < need
            chosen = jnp.where(sc > thr, 0.0, jnp.where(eq, jnp.where(tie_ok, 0.0, NEG_INF), NEG_INF))
            out.append(jnp.where(keep_all, jnp.where(sc > NEG_INF, 0.0, NEG_INF), chosen))
            seen = seen + pref[:, 127:128]
        return out, seen

    n_blk = nk // 128
    grp = 8 if n_blk % 8 == 0 else 1

    def past_group(gi, seen):
        lanes = [pl.ds(pl.multiple_of((gi * grp + i) * 128, 128), 128) for i in range(grp)]
        out, seen = bias_blocks([sc_ref[:, ln] for ln in lanes], seen)
        for ln, bias in zip(lanes, out):
            bias_ref[:, ln] = bias
        return seen

    seen = lax.fori_loop(0, n_blk // grp, past_group, jnp.zeros((rows, 1), F32))
    (bias_new,), _ = bias_blocks([sc_new], seen)
    bias_ref[:, nk:] = bias_new


def _dsa_s_attend_kernel(pt_ref, q_ref, bias_ref, biasn_ref, kn_ref, vn_ref, z_ref, *refs, pg):
    kpages = refs[:pg]
    vpages = refs[pg:2 * pg]
    o_ref = refs[2 * pg]
    m_scr, l_scr, acc_scr = refs[2 * pg + 1:]
    g = pl.program_id(1)
    scale = C_HEAD ** -0.5
    page = kpages[0].shape[0] // C_KV_HEADS

    @pl.when(g == 0)
    def _():
        m_scr[...] = jnp.full(m_scr.shape, NEG_INF, F32)
        l_scr[...] = jnp.zeros(l_scr.shape, F32)
        acc_scr[...] = jnp.zeros(acc_scr.shape, F32)
        for hk in range(C_KV_HEADS):
            cols = slice(hk * C_HEAD, (hk + 1) * C_HEAD)
            q16 = (q_ref[hk] * scale).astype(BF16)
            s = _dot_nt(q16, kn_ref[:, cols].astype(BF16)) + biasn_ref[...]
            _softmax_update(m_scr, l_scr, acc_scr, hk, s, [vn_ref[:, cols].astype(BF16)])

    def two_pages(refs_, i, rows):
        return jnp.concatenate([refs_[i][rows, :], refs_[i + 1][rows, :]], axis=0).astype(BF16)

    for hk in range(C_KV_HEADS):
        q16 = (q_ref[hk] * scale).astype(BF16)
        rows = pl.ds(hk, page, stride=C_KV_HEADS)
        tiles = [_dot_nt(q16, two_pages(kpages, i, rows)) for i in range(0, pg, 2)]
        s = jnp.concatenate(tiles, axis=1) + bias_ref[...]
        _softmax_update(m_scr, l_scr, acc_scr, hk, s, [two_pages(vpages, i, rows) for i in range(0, pg, 2)])

    @pl.when(g == pl.num_programs(1) - 1)
    def _():
        for hk in range(C_KV_HEADS):
            o_ref[hk] = (acc_scr[hk] / l_scr[hk]) * _silu(z_ref[hk])


def _dsa_sample(p3s, cache_k, cache_v, cache_kidx, page_table, layer):
    b, t, _ = p3s.shape
    n_pages = page_table.shape[1]
    page = cache_k.shape[2]
    assert page == 128
    past = n_pages * page
    topk = min(TOPK_MAX, (past + t) // 4)
    pg = 32 if n_pages % 32 == 0 else (16 if n_pages % 16 == 0 else n_pages)
    assert pg % 2 == 0
    ng = n_pages // pg
    rows = t * C_GROUPS

    qi = p3s[:, :, OFF["qi"]:OFF["qi"] + IDX_HEADS * IDX_DIM].reshape(b, t * IDX_HEADS, IDX_DIM)
    wi = p3s[:, :, OFF_TAIL + TAIL_WI:OFF_TAIL + TAIL_WI + IDX_HEADS].reshape(b, t * IDX_HEADS, 1)
    ki_new = jnp.pad(p3s[:, :, OFF_TAIL:OFF_TAIL + IDX_DIM], ((0, 0), (0, 128 - t), (0, 0)))
    k_new = jnp.pad(p3s[:, :, OFF["kc"]:OFF["kc"] + C_KV], ((0, 0), (0, 128 - t), (0, 0)))
    v_new = jnp.pad(p3s[:, :, OFF["vc"]:OFF["vc"] + C_KV], ((0, 0), (0, 128 - t), (0, 0)))

    def heads_major(a):
        a = a.reshape(b, t, C_KV_HEADS, C_GROUPS, C_HEAD)
        return jnp.transpose(a, (0, 2, 1, 3, 4)).reshape(b, C_KV_HEADS, rows, C_HEAD)

    qh = heads_major(p3s[:, :, OFF["qc"]:OFF["qc"] + W_C])
    zh = heads_major(p3s[:, :, OFF["zc"]:OFF["zc"] + W_C])

    kidx_t = jnp.swapaxes(cache_kidx, 2, 3)
    score_spec = pltpu.PrefetchScalarGridSpec(
        num_scalar_prefetch=1,
        grid=(b, ng),
        in_specs=[pl.BlockSpec((None, t * IDX_HEADS, IDX_DIM), lambda i, g, pt: (i, 0, 0)),
                  pl.BlockSpec((None, t * IDX_HEADS, 1), lambda i, g, pt: (i, 0, 0)),
                  pl.BlockSpec((None, 128, IDX_DIM), lambda i, g, pt: (i, 0, 0))]
                 + [pl.BlockSpec((None, None, IDX_DIM, page),
                                 lambda i, g, pt, j=j: (layer, pt[i, g * pg + j], 0, 0)) for j in range(pg)],
        out_specs=[pl.BlockSpec((None, t, pg * page), lambda i, g, pt: (i, 0, g)),
                   pl.BlockSpec((None, t, 128), lambda i, g, pt: (i, 0, 0))],
    )
    scores, scores_new = pl.pallas_call(
        functools.partial(_dsa_s_score_kernel, pg=pg, t=t),
        grid_spec=score_spec,
        out_shape=[jax.ShapeDtypeStruct((b, t, past), F32), jax.ShapeDtypeStruct((b, t, 128), F32)],
        compiler_params=_cp("dsa_s_score", 2),
        name="dsa_s_score",
    )(page_table, qi, wi, ki_new, *([kidx_t] * pg))

    tq = jnp.broadcast_to(jnp.tile(jnp.arange(t, dtype=I32), b)[:, None], (b * t, 128))
    bias = pl.pallas_call(
        functools.partial(_dsa_s_select_kernel, nk=past, topk=topk),
        grid=(1,),
        in_specs=[pl.BlockSpec((b * t, past), lambda i: (0, 0)),
                  pl.BlockSpec((b * t, 128), lambda i: (0, 0)),
                  pl.BlockSpec((b * t, 128), lambda i: (0, 0))],
        out_specs=pl.BlockSpec((b * t, past + 128), lambda i: (0, 0)),
        out_shape=jax.ShapeDtypeStruct((b * t, past + 128), F32),
        compiler_params=_cp("dsa_s_select", 1),
        name="dsa_s_select",
    )(scores.reshape(b * t, past), scores_new.reshape(b * t, 128), tq).reshape(b, t, past + 128)

    bias_rows = jnp.repeat(bias, C_GROUPS, axis=1)
    bias_past = bias_rows[:, :, :past]
    bias_new = bias_rows[:, :, past:]

    ck = cache_k.reshape(cache_k.shape[0], cache_k.shape[1], page * C_KV_HEADS, C_HEAD)
    cv = cache_v.reshape(cache_v.shape[0], cache_v.shape[1], page * C_KV_HEADS, C_HEAD)
    pspec = [pl.BlockSpec((None, None, page * C_KV_HEADS, C_HEAD),
                          lambda i, g, pt, j=j: (layer, pt[i, g * pg + j], 0, 0)) for j in range(pg)]
    attend_spec = pltpu.PrefetchScalarGridSpec(
        num_scalar_prefetch=1,
        grid=(b, ng),
        in_specs=[pl.BlockSpec((None, C_KV_HEADS, rows, C_HEAD), lambda i, g, pt: (i, 0, 0, 0)),
                  pl.BlockSpec((None, rows, pg * page), lambda i, g, pt: (i, 0, g)),
                  pl.BlockSpec((None, rows, 128), lambda i, g, pt: (i, 0, 0)),
                  pl.BlockSpec((None, 128, C_KV), lambda i, g, pt: (i, 0, 0)),
                  pl.BlockSpec((None, 128, C_KV), lambda i, g, pt: (i, 0, 0)),
                  pl.BlockSpec((None, C_KV_HEADS, rows, C_HEAD), lambda i, g, pt: (i, 0, 0, 0))]
                 + pspec + pspec,
        out_specs=pl.BlockSpec((None, C_KV_HEADS, rows, C_HEAD), lambda i, g, pt: (i, 0, 0, 0)),
        scratch_shapes=[pltpu.VMEM((C_KV_HEADS, rows, 1), F32), pltpu.VMEM((C_KV_HEADS, rows, 1), F32),
                        pltpu.VMEM((C_KV_HEADS, rows, C_HEAD), F32)],
    )
    oh = pl.pallas_call(
        functools.partial(_dsa_s_attend_kernel, pg=pg),
        grid_spec=attend_spec,
        out_shape=jax.ShapeDtypeStruct((b, C_KV_HEADS, rows, C_HEAD), F32),
        compiler_params=_cp("dsa_s_attend", 2),
        name="dsa_s_attend",
    )(page_table, qh, bias_past, bias_new, k_new, v_new, zh, *([ck] * pg), *([cv] * pg))
    y = jnp.transpose(oh.reshape(b, C_KV_HEADS, t, C_GROUPS, C_HEAD), (0, 2, 1, 3, 4)).reshape(b, t, W_C)
    return y.astype(BF16)


def _outproj_kernel(ya_ref, yb_ref, yc_ref, w_ref, x_ref, gate_ref, o_ref, mix_scr):
    @pl.when(pl.program_id(1) == 0)
    def _():
        mix_scr[:, 0:W_A] = ya_ref[...]
        mix_scr[:, W_A:W_A + W_B] = yb_ref[...]
        mix_scr[:, W_A + W_B:] = yc_ref[...]

    o_ref[...] = x_ref[...] + gate_ref[...] * _dot(mix_scr[...], w_ref[...])


def _outproj_norm_kernel(ya_ref, yb_ref, yc_ref, w_ref, x_ref, gate_ref, g_ref, o_ref, mix_scr, ssq_scr):
    j = pl.program_id(1)
    tn = x_ref.shape[1]

    @pl.when(j == 0)
    def _():
        mix_scr[:, 0:W_A] = ya_ref[...]
        mix_scr[:, W_A:W_A + W_B] = yb_ref[...]
        mix_scr[:, W_A + W_B:] = yc_ref[...]
        ssq_scr[...] = jnp.zeros(ssq_scr.shape, F32)

    x_new = x_ref[...] + gate_ref[...] * _dot(mix_scr[...], w_ref[...])
    o_ref[:, pl.ds(pl.multiple_of(j * tn, tn), tn)] = x_new
    ssq_scr[...] += jnp.sum(x_new * x_new, axis=-1, keepdims=True)

    @pl.when(j == pl.num_programs(1) - 1)
    def _():
        tm, d = o_ref.shape
        rc = 64 if tm % 64 == 0 else tm

        def body(c, carry):
            r = pl.ds(pl.multiple_of(c * rc, rc), rc)
            o_ref[r, :] = o_ref[r, :] * lax.rsqrt(ssq_scr[r, :] / d + EPS) * g_ref[...]
            return carry

        lax.fori_loop(0, tm // rc, body, 0)


def _outproj(ya, yb, yc, w_bf16, layer, x2, gate, *, rows_per_batch, mod_row0, g_final=None):
    m, d = x2.shape
    tn = 1024
    if gate.ndim == 3:
        tm = 512
        gate_spec = pl.BlockSpec((None, 1, tn),
                                 lambda i, j: ((i * tm) // rows_per_batch + mod_row0, 0, 2 * (d // tn) + j))
    else:
        tm = m
        gate_spec = pl.BlockSpec((tm, tn), lambda i, j: (i, j))
    assert m % tm == 0
    in_specs = [pl.BlockSpec((tm, W_A), lambda i, j: (i, 0)),
                pl.BlockSpec((tm, W_B), lambda i, j: (i, 0)),
                pl.BlockSpec((tm, W_C), lambda i, j: (i, 0)),
                pl.BlockSpec((None, d, tn), lambda i, j: (layer, 0, j)),
                pl.BlockSpec((tm, tn), lambda i, j: (i, j)),
                gate_spec]
    operands = [ya, yb, yc, w_bf16, x2, gate]
    scratch = [pltpu.VMEM((tm, d), BF16)]
    if g_final is None:
        body, name = _outproj_kernel, "outproj"
        out_spec = pl.BlockSpec((tm, tn), lambda i, j: (i, j))
    else:
        body, name = _outproj_norm_kernel, "outproj_norm"
        in_specs.append(pl.BlockSpec((1, d), lambda i, j: (0, 0)))
        operands.append(g_final.reshape(1, d))
        scratch.append(pltpu.VMEM((tm, 1), F32))
        out_spec = pl.BlockSpec((tm, d), lambda i, j: (i, 0))
    return pl.pallas_call(
        body,
        grid=(m // tm, d // tn),
        in_specs=in_specs,
        out_specs=out_spec,
        out_shape=jax.ShapeDtypeStruct((m, d), F32),
        scratch_shapes=scratch,
        compiler_params=_cp(name, 2),
        name=name,
    )(*operands)


def _kv_rows_kernel(*refs, depth, tm):
    k_refs, v_refs = refs[:depth], refs[depth:2 * depth]
    pk_ref, pv_ref = refs[2 * depth:]
    for lyr in range(depth):
        for h in range(C_KV_HEADS):
            cols = slice(h * C_HEAD, (h + 1) * C_HEAD)
            rows = pl.ds(h, tm, stride=C_KV_HEADS)
            pk_ref.at[lyr][rows, :] = k_refs[lyr][:, cols]
            pv_ref.at[lyr][rows, :] = v_refs[lyr][:, cols]


def _kv_rows(p3_layers):
    depth = len(p3_layers)
    b, l, _ = p3_layers[0].shape
    tm = 512 if l % 512 == 0 else l
    kspec = pl.BlockSpec((None, tm, C_KV), lambda i, j: (i, j, OFF["kc"] // C_KV))
    vspec = pl.BlockSpec((None, tm, C_KV), lambda i, j: (i, j, OFF["vc"] // C_KV))
    ospec = pl.BlockSpec((depth, None, tm * C_KV_HEADS, C_HEAD), lambda i, j: (0, i, j, 0))
    oshape = jax.ShapeDtypeStruct((depth, b, l * C_KV_HEADS, C_HEAD), F32)
    pk, pv = pl.pallas_call(
        functools.partial(_kv_rows_kernel, depth=depth, tm=tm),
        grid=(b, l // tm),
        in_specs=[kspec] * depth + [vspec] * depth,
        out_specs=[ospec, ospec],
        out_shape=[oshape, oshape],
        compiler_params=_cp("kv_rows", 2),
        name="kv_rows",
    )(*p3_layers, *p3_layers)
    shape = (depth, b, l, C_KV_HEADS, C_HEAD)
    return pk.reshape(shape), pv.reshape(shape)


def _w_in_tile_table():
    n_t = NP // 128
    table, n_plain = [], None
    for dt in range(n_t - 1):
        c = dt * 128
        seg = [n for n in _DST_ORDER if OFF[n] <= c < OFF[n] + _SRC_OFF[n][1]][0]
        src = _SRC_OFF[seg][0] + (c - OFF[seg])
        if src % 128 == 0:
            assert n_plain is None
        else:
            assert src % 128 == 32
            if n_plain is None:
                n_plain = dt
        table.append(src // 128)
    t_kw, t_ab = _SRC_OFF["ki"][0] // 128, _SRC_OFF["a"][0] // 128
    assert _SRC_OFF["ki"][0] % 128 == 32 and _SRC_OFF["wi"][0] == t_kw * 128 + TAIL_WI
    assert _SRC_OFF["a"][0] % 128 == 0 and _SRC_OFF["b"][0] == _SRC_OFF["a"][0] + DN_HEADS
    assert TAIL_A == IDX_DIM and TAIL_B == TAIL_A + DN_HEADS and TAIL_WI == TAIL_B + DN_HEADS
    tile_a = table + [t_kw]
    rows_b = [t_ab * 4] * n_plain + [(s + 1) * 4 for s in table[n_plain:]] + [t_ab * 4]
    return tile_a, rows_b, n_plain


def _wprep_kernel(ta_ref, tb_ref, a_ref, b_ref, xs_ref, g_ref, sc_ref, sh_ref, o_ref, ps_ref, h_scr,
                  *, n_plain, n_t):
    dt = pl.program_id(0)
    dt_o = o_ref.dtype

    @pl.when(dt == 0)
    def _():
        x = xs_ref[...]
        y = x * lax.rsqrt(jnp.mean(x * x, axis=-1, keepdims=True) + EPS)
        h_scr[...] = ((y * g_ref[...]) * (1.0 + sc_ref[...]) + sh_ref[...]).astype(BF16)

    @pl.when(dt < n_plain)
    def _():
        o_ref[...] = a_ref[...].astype(dt_o)

    @pl.when((dt >= n_plain) & (dt < n_t - 1))
    def _():
        o_ref[0:96, :] = a_ref[32:128, :].astype(dt_o)
        o_ref[96:128, :] = b_ref[...].astype(dt_o)

    @pl.when(dt == n_t - 1)
    def _():
        o_ref[0:TAIL_A, :] = a_ref[32:32 + IDX_DIM, :].astype(dt_o)
        o_ref[TAIL_A:TAIL_WI, :] = b_ref[...].astype(dt_o)
        o_ref[TAIL_WI:TAIL_WI + IDX_HEADS, :] = a_ref[TAIL_WI:TAIL_WI + IDX_HEADS, :].astype(dt_o)
        o_ref[TAIL_WI + IDX_HEADS:, :] = jnp.zeros((128 - TAIL_WI - IDX_HEADS, o_ref.shape[1]), dt_o)

    ps_ref[...] = _dot_nt(h_scr[...], o_ref[...])


def _permute_w_in(w_in, layer, xs, g, sc, sh):
    _, d, _ = w_in.shape
    ms = xs.shape[0]
    w_t = jnp.swapaxes(w_in, 1, 2)
    tile_a, rows_b, n_plain = _w_in_tile_table()
    n_t = NP // 128
    const = lambda t, ta, tb: (0, 0)
    grid_spec = pltpu.PrefetchScalarGridSpec(
        num_scalar_prefetch=2,
        grid=(n_t,),
        in_specs=[pl.BlockSpec((None, 128, d), lambda t, ta, tb: (layer, ta[t], 0)),
                  pl.BlockSpec((None, 32, d), lambda t, ta, tb: (layer, tb[t], 0)),
                  pl.BlockSpec((ms, d), const), pl.BlockSpec((1, d), const),
                  pl.BlockSpec((ms, d), const), pl.BlockSpec((ms, d), const)],
        out_specs=[pl.BlockSpec((128, d), lambda t, ta, tb: (t, 0)),
                   pl.BlockSpec((ms, 128), lambda t, ta, tb: (0, t))],
        scratch_shapes=[pltpu.VMEM((ms, d), BF16)],
    )
    return pl.pallas_call(
        functools.partial(_wprep_kernel, n_plain=n_plain, n_t=n_t),
        grid_spec=grid_spec,
        out_shape=[jax.ShapeDtypeStruct((NP, d), BF16), jax.ShapeDtypeStruct((ms, NP), F32)],
        compiler_params=_cp("w_in_layout", 1),
        name="w_in_layout",
    )(jnp.asarray(tile_a, I32), jnp.asarray(rows_b, I32), w_t, w_t, xs, g, sc, sh)


def _delta_t(l):
    for t in (128, 64):
        if l % t == 0:
            return t
    raise ValueError("sequence length must be a multiple of 64")


def kernel(x_prompt, x_sample, cache_k, cache_v, cache_kidx, state_dn, state_conv, page_table,
           c_prompt, c_sample, w_ada, b_ada, g_norm, w_in, a_vnorm, a_ws, a_bs, dn_conv_w,
           dn_a_log, dn_dt_bias, dn_onorm, w_out, g_final):
    bp, lp, d = x_prompt.shape
    bs, ls, _ = x_sample.shape
    depth = w_ada.shape[0]
    assert d == D_MODEL and w_in.shape[2] == D_IN
    assert CONV_K - 1 <= ls <= DN_CHUNK

    n_c = bp + bs
    c_rows = jnp.concatenate([c_prompt, c_sample], axis=0)
    r_pad = (-n_c) % 8
    if r_pad:
        c_rows = jnp.pad(c_rows, ((0, r_pad), (0, 0)))
    m_all = _ada(c_rows, w_ada, b_ada)

    xp = x_prompt.reshape(bp * lp, d)
    xs = x_sample.reshape(bs * ls, d)
    ls_pad = 8 * ((ls + 7) // 8)
    outs = {k: [] for k in ("p3", "pki", "pdn", "pconv", "sk", "sv", "ski", "sdn", "sconv", "samlp")}
    zeros_conv = jnp.zeros((1, bp, CONV_K - 1, DN_CONV_DIM), F32)
    zeros_state = jnp.zeros((1, bp, DN_HEADS, DN_HEAD, DN_HEAD), F32)

    w_out_all = w_out.astype(BF16)

    def g_last(layer):
        return g_final if layer == depth - 1 else None

    for l in range(depth):
        g_l = g_norm[l].reshape(1, d)
        m_l = m_all[l]
        m3 = m_l.reshape(m_l.shape[0], 1, 3 * d)
        ms = jnp.repeat(m_l[bp:bp + bs], ls, axis=0)

        w_in_l, ps = _permute_w_in(w_in, l, xs, g_l, ms[:, d:2 * d], ms[:, 0:d])

        pp = _inproj(xp, g_l, m3, w_in_l, rows_per_batch=lp)
        p3 = pp.reshape(bp, lp, NP)
        (ya,) = _mixa(p3, a_vnorm[l], a_ws[l], a_bs[l], emit_va=False)
        abt = jnp.transpose(p3[:, :, OFF["a"]:OFF["a"] + 2 * DN_HEADS], (0, 2, 1))
        yb, s_p = _delta(p3, abt, dn_conv_w[l], zeros_conv, zeros_state, 0, dn_a_log[l], dn_dt_bias[l],
                         dn_onorm[l], t=_delta_t(lp), valid_len=_delta_t(lp))
        yc = _dsa_prompt(p3)
        xp = _outproj(ya.reshape(bp * lp, W_A), yb.reshape(bp * lp, W_B), yc.reshape(bp * lp, W_C),
                      w_out_all, l, xp, m3, rows_per_batch=lp, mod_row0=0, g_final=g_last(l))
        outs["p3"].append(p3)
        outs["pki"].append(p3[:, :, OFF_TAIL:OFF_TAIL + IDX_DIM])
        outs["pdn"].append(s_p)
        outs["pconv"].append(p3[:, lp - (CONV_K - 1):, OFF["qkv"]:OFF["qkv"] + DN_CONV_DIM])

        p3s = ps.reshape(bs, ls, NP)
        ya_s, va_s = _mixa(p3s, a_vnorm[l], a_ws[l], a_bs[l], emit_va=True)
        p3s_pad = jnp.pad(p3s, ((0, 0), (0, ls_pad - ls), (0, 0)))
        abt_s = jnp.transpose(p3s_pad[:, :, OFF["a"]:OFF["a"] + 2 * DN_HEADS], (0, 2, 1))
        yb_s, s_s = _delta(p3s_pad, abt_s, dn_conv_w[l], state_conv, state_dn, l, dn_a_log[l],
                           dn_dt_bias[l], dn_onorm[l], t=ls_pad, valid_len=ls)
        yc_s = _dsa_sample(p3s, cache_k, cache_v, cache_kidx, page_table, l)
        xs = _outproj(ya_s.reshape(bs * ls, W_A), yb_s[:, :ls].reshape(bs * ls, W_B),
                      yc_s.reshape(bs * ls, W_C), w_out_all, l, xs, ms[:, 2 * d:3 * d],
                      rows_per_batch=ls, mod_row0=bp, g_final=g_last(l))
        outs["sk"].append(p3s[:, :, OFF["kc"]:OFF["kc"] + C_KV].reshape(bs, ls, C_KV_HEADS, C_HEAD))
        outs["sv"].append(p3s[:, :, OFF["vc"]:OFF["vc"] + C_KV].reshape(bs, ls, C_KV_HEADS, C_HEAD))
        outs["ski"].append(p3s[:, :, OFF_TAIL:OFF_TAIL + IDX_DIM])
        outs["sdn"].append(s_s)
        outs["sconv"].append(p3s[:, ls - (CONV_K - 1):, OFF["qkv"]:OFF["qkv"] + DN_CONV_DIM])
        outs["samlp"].append(va_s)

    y_prompt = xp.reshape(bp, lp, d)
    y_sample = xs.reshape(bs, ls, d)
    st = jnp.stack
    p_k, p_v = _kv_rows(outs["p3"])
    return (y_prompt, y_sample, p_k, p_v, st(outs["pki"]), st(outs["pdn"]),
            st(outs["pconv"]), st(outs["sk"]), st(outs["sv"]), st(outs["ski"]), st(outs["sdn"]),
            st(outs["sconv"]), st(outs["samlp"]))
```

```python
import functools

import jax
import jax.numpy as jnp
from jax import lax
from jax.experimental import pallas as pl
from jax.experimental.pallas import tpu as pltpu

F32 = jnp.float32
BF16 = jnp.bfloat16
I32 = jnp.int32
EPS = 1e-6
INT_MIN = -(2 ** 31)
NEG_INF = float("-inf")

D_MODEL = 4096
W_A = D_MODEL // 4
A_GROUP = 128
A_HEADS = W_A // A_GROUP
A_CHUNK = 128
W_B = D_MODEL // 2
DN_HEAD = 128
DN_HEADS = W_B // DN_HEAD
CONV_K = 4
DN_CONV_DIM = 3 * W_B
DN_CHUNK = 64
W_C = D_MODEL - W_A - W_B
C_HEAD = 128
C_HEADS = W_C // C_HEAD
C_KV_HEADS = 2
C_GROUPS = C_HEADS // C_KV_HEADS
C_KV = C_KV_HEADS * C_HEAD
IDX_HEADS = 16
IDX_DIM = 64
TOPK_MAX = 256
Q_BLOCK = 512

_SRC_SPLITS = (W_A, W_A, W_A, DN_CONV_DIM, W_B, DN_HEADS, DN_HEADS,
               W_C, C_KV, C_KV, W_C, IDX_HEADS * IDX_DIM, IDX_DIM, IDX_HEADS)
_SRC_NAMES = ("u", "v", "za", "qkv", "zb", "a", "b", "qc", "kc", "vc", "zc", "qi", "ki", "wi")
_SRC_OFF = {}
_o = 0
for _n, _w in zip(_SRC_NAMES, _SRC_SPLITS):
    _SRC_OFF[_n] = (_o, _w)
    _o += _w
D_IN = _o

_DST_ORDER = ("u", "v", "za", "qkv", "zb", "qc", "zc", "qi", "kc", "vc", "ki", "a", "b", "wi")
OFF = {}
_o = 0
for _n in _DST_ORDER:
    OFF[_n] = _o
    _o += _SRC_OFF[_n][1]
NP = ((_o + 127) // 128) * 128
OFF_TAIL = OFF["ki"]
TAIL_A = OFF["a"] - OFF_TAIL
TAIL_B = OFF["b"] - OFF_TAIL
TAIL_WI = OFF["wi"] - OFF_TAIL

MXU_COLUMNS = 256
INPROJ_TN = 6 * MXU_COLUMNS
DELTA_HEADS_PER_STEP = 8
DSA_KEY_TILE = 512

VMEM_CAPACITY_MIB = 64
VMEM_LIMIT_MIB = {
    "ada": 40, "w_in_layout": 32, "inproj": 56, "mixa": 32, "delta": 40, "dsa_prompt": 48,
    "dsa_s_score": 32, "dsa_s_select": 40, "dsa_s_attend": 40, "outproj": 48, "outproj_norm": 60,
    "kv_rows": 32,
}
assert max(VMEM_LIMIT_MIB.values()) < VMEM_CAPACITY_MIB


def _cp(name, n_axes, fuse_inputs=0):
    fusion = [True] * fuse_inputs if fuse_inputs else None
    return pltpu.CompilerParams(dimension_semantics=("arbitrary",) * n_axes,
                                vmem_limit_bytes=VMEM_LIMIT_MIB[name] * 1024 * 1024,
                                allow_input_fusion=fusion)


def _silu(x):
    return (0.5 * x) * (1.0 + jnp.tanh(0.5 * x))


def _gelu(x):
    return 0.5 * x * (1.0 + jnp.tanh(0.7978845608028654 * (x + 0.044715 * (x * x * x))))


def _dot(a, b):
    return jnp.dot(a, b, preferred_element_type=F32)


def _dot_nt(a, b):
    return lax.dot_general(a, b, (((1,), (1,)), ((), ())), preferred_element_type=F32)


def _dot_tn(a, b):
    return lax.dot_general(a, b, (((0,), (0,)), ((), ())), preferred_element_type=F32)


def _b16(a):
    return a.astype(BF16)


def _ada_kernel(c_ref, w_ref, b_ref, o_ref):
    s = _silu(c_ref[...]).astype(BF16)
    o_ref[...] = _dot(s, w_ref[...].astype(BF16)) + b_ref[...]


def _ada(c_rows, w_ada, b_ada):
    depth, d, n = w_ada.shape
    r = c_rows.shape[0]
    tn = 512
    return pl.pallas_call(
        _ada_kernel,
        grid=(depth, n // tn),
        in_specs=[pl.BlockSpec((r, d), lambda l, j: (0, 0)),
                  pl.BlockSpec((None, d, tn), lambda l, j: (l, 0, j)),
                  pl.BlockSpec((None, 1, tn), lambda l, j: (l, 0, j))],
        out_specs=pl.BlockSpec((None, r, tn), lambda l, j: (l, 0, j)),
        out_shape=jax.ShapeDtypeStruct((depth, r, n), F32),
        compiler_params=_cp("ada", 2),
        name="ada",
    )(c_rows, w_ada, b_ada.reshape(depth, 1, n))


def _inproj_kernel(x_ref, g_ref, sc_ref, sh_ref, w_ref, o_ref, h_scr, *, rc):
    @pl.when(pl.program_id(1) == 0)
    def _():
        tm = x_ref.shape[0]

        def body(c, carry):
            r = pl.ds(pl.multiple_of(c * rc, rc), rc)
            x = x_ref[r, :]
            y = x * lax.rsqrt(jnp.mean(x * x, axis=-1, keepdims=True) + EPS)
            h_scr[r, :] = ((y * g_ref[...]) * (1.0 + sc_ref[...]) + sh_ref[...]).astype(BF16)
            return carry

        lax.fori_loop(0, tm // rc, body, 0)

    o_ref[...] = _dot_nt(h_scr[...], w_ref[...])


def _inproj(x2, g, mod, w_bf16, *, rows_per_batch):
    m, d = x2.shape
    n = w_bf16.shape[0]
    tn = INPROJ_TN
    tm = 512
    assert rows_per_batch % tm == 0 and m % tm == 0
    return pl.pallas_call(
        functools.partial(_inproj_kernel, rc=64),
        grid=(m // tm, pl.cdiv(n, tn)),
        in_specs=[pl.BlockSpec((tm, d), lambda i, j: (i, 0)),
                  pl.BlockSpec((1, d), lambda i, j: (0, 0)),
                  pl.BlockSpec((None, 1, d), lambda i, j: ((i * tm) // rows_per_batch, 0, 1)),
                  pl.BlockSpec((None, 1, d), lambda i, j: ((i * tm) // rows_per_batch, 0, 0)),
                  pl.BlockSpec((tn, d), lambda i, j: (j, 0))],
        out_specs=pl.BlockSpec((tm, tn), lambda i, j: (i, j)),
        out_shape=jax.ShapeDtypeStruct((m, n), F32),
        scratch_shapes=[pltpu.VMEM((tm, d), BF16)],
        compiler_params=_cp("inproj", 2),
        name="inproj",
    )(x2, g, mod, mod, w_bf16)


def _mixa_kernel(u_ref, v_ref, z_ref, vn_ref, ws_ref, bst_ref, y_ref, *rest, c, emit_va):
    u = _gelu(u_ref[...])
    v = _gelu(v_ref[...])
    mu = jnp.mean(v, axis=-1, keepdims=True)
    dv = v - mu
    va = dv * lax.rsqrt(jnp.mean(dv * dv, axis=-1, keepdims=True) + EPS) * vn_ref[...]
    if emit_va:
        rest[0][...] = va
    z = _silu(z_ref[...])
    row = lax.broadcasted_iota(I32, (c, c), 0)
    col = lax.broadcasted_iota(I32, (c, c), 1)
    tril = col <= row
    for h in range(A_HEADS):
        cols = slice(h * A_GROUP, (h + 1) * A_GROUP)
        wm = jnp.where(tril, ws_ref[h], 0.0)
        wm16 = wm.astype(BF16)
        for ci in range(u.shape[0] // c):
            rows = slice(ci * c, (ci + 1) * c)
            vh = va[rows, cols]
            if c >= 128:
                mixed = _dot(wm16, vh.astype(BF16))
            else:
                mixed = wm[:, 0:1] * vh[0:1, :]
                for s in range(1, c):
                    mixed = mixed + wm[:, s:s + 1] * vh[s:s + 1, :]
            mixed = mixed + bst_ref[:, h:h + 1]
            y_ref[rows, cols] = (u[rows, cols] * mixed * z[rows, cols]).astype(y_ref.dtype)


def _mixa(p3, a_vnorm, a_ws, a_bs, *, emit_va):
    b, l, _ = p3.shape
    c = min(A_CHUNK, l)
    r = 2 * c if l % (2 * c) == 0 else c
    n = l // r
    ws = a_ws[:, :c, :c]
    bst = a_bs[:, :c].T
    wblk = W_A
    outs = [jax.ShapeDtypeStruct((b, l, W_A), BF16)]
    out_specs = [pl.BlockSpec((None, r, W_A), lambda i, j: (i, j, 0))]
    if emit_va:
        outs.append(jax.ShapeDtypeStruct((b, l, W_A), F32))
        out_specs.append(pl.BlockSpec((None, r, W_A), lambda i, j: (i, j, 0)))
    res = pl.pallas_call(
        functools.partial(_mixa_kernel, c=c, emit_va=emit_va),
        grid=(b, n),
        in_specs=[pl.BlockSpec((None, r, wblk), lambda i, j: (i, j, OFF["u"] // wblk)),
                  pl.BlockSpec((None, r, wblk), lambda i, j: (i, j, OFF["v"] // wblk)),
                  pl.BlockSpec((None, r, wblk), lambda i, j: (i, j, OFF["za"] // wblk)),
                  pl.BlockSpec((1, W_A), lambda i, j: (0, 0)),
                  pl.BlockSpec((A_HEADS, c, c), lambda i, j: (0, 0, 0)),
                  pl.BlockSpec((c, A_HEADS), lambda i, j: (0, 0))],
        out_specs=out_specs,
        out_shape=outs,
        compiler_params=_cp("mixa", 2, fuse_inputs=6),
        name="mixa",
    )(p3, p3, p3, a_vnorm.reshape(1, W_A), ws, bst)
    return res


def _delta_kernel(alog_ref, dtb_ref,
                  q_ref, k_ref, v_ref, z_ref, ab_ref,
                  cwq_ref, cwk_ref, cwv_ref, cpq_ref, cpk_ref, cpv_ref,
                  s0_ref, on_ref,
                  y_ref, sout_ref,
                  xbuf, s_scr, *, t, c, hp, valid_len):
    n = pl.program_id(2)
    dh = DN_HEAD

    @pl.when(n == 0)
    def _():
        s_scr[...] = s0_ref[...]
        for j in range(hp):
            lanes = slice(j * dh, (j + 1) * dh)
            xbuf[3 * j + 0, 5:8, :] = cpq_ref[:, lanes]
            xbuf[3 * j + 1, 5:8, :] = cpk_ref[:, lanes]
            xbuf[3 * j + 2, 5:8, :] = cpv_ref[:, lanes]

    _delta_heads(alog_ref, dtb_ref, q_ref, k_ref, v_ref, z_ref, ab_ref, cwq_ref, cwk_ref, cwv_ref,
                 on_ref, y_ref, xbuf, s_scr, t=t, c=c, hp=hp, valid_len=valid_len)

    @pl.when(n == pl.num_programs(2) - 1)
    def _():
        sout_ref[...] = s_scr[...]


def _delta_heads(alog_ref, dtb_ref, q_ref, k_ref, v_ref, z_ref, ab_ref, cwq_ref, cwk_ref, cwv_ref,
                 on_ref, y_ref, xbuf, s_scr, *, t, c, hp, valid_len):
    hg = pl.program_id(1)
    dh = DN_HEAD
    heads = list(range(hp))

    row = lax.broadcasted_iota(I32, (t, t), 0)
    col = lax.broadcasted_iota(I32, (t, t), 1)
    shift = c.bit_length() - 1
    same = (row >> shift) == (col >> shift)
    eye = row == col
    incl = same & (col <= row)
    strict = same & (col < row)
    incl_t = same & (row <= col)
    blk8 = (row >> 3) == (col >> 3)
    off_masks = []
    bs = 8
    while bs < c:
        sh_b = bs.bit_length() - 1
        inner = (row >> sh_b) == (col >> sh_b)
        outer = (row >> (sh_b + 1)) == (col >> (sh_b + 1))
        off_masks.append(outer & jnp.logical_not(inner))
        bs *= 2
    eye_f = jnp.where(eye, 1.0, 0.0)
    if valid_len < t:
        lane_valid = lax.broadcasted_iota(I32, (1, t), 1) < valid_len
        sub_valid = lax.broadcasted_iota(I32, (t, 1), 0) < valid_len

    def to_col(r):
        return jnp.sum(jnp.where(eye, r, 0.0), axis=1, keepdims=True)

    def conv(idx, x, w_ref, lanes):
        xbuf[idx, 8:8 + t, :] = x
        y = xbuf[idx, pl.ds(5, t), :] * w_ref[0:1, lanes]
        for jj in range(1, CONV_K):
            y = y + xbuf[idx, pl.ds(5 + jj, t), :] * w_ref[jj:jj + 1, lanes]
        xbuf[idx, 5:8, :] = x[t - 3:t, :]
        return _silu(y)

    lanes_of = [slice(j * dh, (j + 1) * dh) for j in range(hp)]

    def per_head(f, *lists):
        return [f(*vals) for vals in zip(*lists)]

    def l2n(a):
        return a * lax.rsqrt(jnp.sum(a * a, axis=-1, keepdims=True) + EPS)

    qc = [l2n(conv(3 * j + 0, q_ref[:, lanes_of[j]], cwq_ref, lanes_of[j])) * (dh ** -0.5) for j in heads]
    kc = [l2n(conv(3 * j + 1, k_ref[:, lanes_of[j]], cwk_ref, lanes_of[j])) for j in heads]
    v = [conv(3 * j + 2, v_ref[:, lanes_of[j]], cwv_ref, lanes_of[j]) for j in heads]

    def gates(j):
        h = hg * hp + j
        a_row = ab_ref[pl.ds(h, 1), :]
        b_row = ab_ref[pl.ds(DN_HEADS + h, 1), :]
        xa = a_row + dtb_ref[h]
        softplus = jnp.maximum(xa, 0.0) + jnp.log(1.0 + jnp.exp(-jnp.abs(xa)))
        a_coef = jnp.exp(jnp.zeros((1, 1), F32) + alog_ref[h])
        g_row = -a_coef * softplus
        beta_row = 1.0 / (1.0 + jnp.exp(-b_row))
        if valid_len < t:
            g_row = jnp.where(lane_valid, g_row, 0.0)
            beta_row = jnp.where(lane_valid, beta_row, 0.0)
        return g_row, beta_row

    g_row, beta_row = zip(*[gates(j) for j in heads])
    if valid_len < t:
        kc = per_head(lambda a: jnp.where(sub_valid, a, 0.0), kc)
        v = per_head(lambda a: jnp.where(sub_valid, a, 0.0), v)

    g_col = per_head(to_col, g_row)
    beta_col = per_head(to_col, beta_row)
    gc_col = per_head(lambda r: jnp.sum(jnp.where(incl, r, 0.0), axis=1, keepdims=True), g_row)
    glast_col = per_head(lambda r: jnp.sum(jnp.where(same, r, 0.0), axis=1, keepdims=True), g_row)
    gc_row = per_head(lambda cl: jnp.sum(jnp.where(incl_t, cl, 0.0), axis=0, keepdims=True), g_col)
    kb = per_head(lambda a, b: a * b, kc, beta_col)
    kc16 = per_head(lambda a: a.astype(BF16), kc)

    def key_products(kb_, kc16_, qc_, gc, gr):
        decay = jnp.where(incl, jnp.exp(jnp.where(incl, gc - gr, 0.0)), 0.0)
        a_mat = jnp.where(strict, _dot_nt(kb_.astype(BF16), kc16_) * decay, 0.0)
        attn16 = (_dot_nt(qc_.astype(BF16), kc16_) * decay).astype(BF16)
        n0f = jnp.where(blk8, -a_mat, 0.0)
        return attn16, _b16(n0f), eye_f + n0f, [_b16(jnp.where(om, a_mat, 0.0)) for om in off_masks]

    attn, n0, x, a_offs = zip(*per_head(key_products, kb, kc16, qc, gc_col, gc_row))
    eg = per_head(jnp.exp, gc_col)
    rhs = per_head(lambda vv, bc, kk, e: _b16(jnp.concatenate([vv * bc, kk * e], axis=1)),
                   v, beta_col, kb, eg)
    n2 = per_head(lambda a: _b16(_dot(a, a)), n0)
    n4 = per_head(lambda a: _b16(_dot(a, a)), n2)
    x = per_head(lambda xx, nn: xx + _dot(_b16(xx), nn), x, n2)
    x = per_head(lambda xx, nn: xx + _dot(_b16(xx), nn), x, n4)
    for lvl in range(len(off_masks)):
        xs = per_head(_b16, x)
        xa_off = per_head(lambda s_, offs: _b16(_dot(s_, offs[lvl])), xs, a_offs)
        x = per_head(lambda xx, xo, s_: xx - _dot(xo, s_), x, xa_off, xs)
    sol16 = per_head(lambda xx, r: _b16(_dot(_b16(xx), r)), x, rhs)
    auw = per_head(_dot, attn, sol16)
    qw = per_head(lambda a, e, m_: (a * e - m_[:, dh:]).astype(BF16), qc, eg, auw)
    kg16 = per_head(lambda a, gl_, gc: (a * jnp.exp(gl_ - gc)).astype(BF16), kc, glast_col, gc_col)

    s = [s_scr[j] for j in heads]
    outs = [[] for _ in heads]
    for i in range(t // c):
        rows = slice(i * c, (i + 1) * c)
        s16 = per_head(lambda a: a.astype(BF16), s)
        kuw = per_head(lambda a, b: _dot_tn(a[rows], b[rows]), kg16, sol16)
        o_i = per_head(lambda a, b16, m_: _dot(a[rows], b16) + m_[rows, :dh], qw, s16, auw)
        for pos in range(len(heads)):
            outs[pos].append(o_i[pos])
        gl = per_head(lambda a: jnp.exp(a[i * c:i * c + 1, :]), glast_col)
        s = per_head(lambda g_, s_, m_, b16: g_ * s_ + m_[:, :dh] - _dot(m_[:, dh:].astype(BF16), b16),
                     gl, s, kuw, s16)
    for pos, j in enumerate(heads):
        s_scr[j] = s[pos]
        o = outs[pos][0] if len(outs[pos]) == 1 else jnp.concatenate(outs[pos], axis=0)
        on = o * lax.rsqrt(jnp.mean(o * o, axis=-1, keepdims=True) + EPS) * on_ref[...]
        y_ref[:, lanes_of[j]] = (on * _silu(z_ref[:, lanes_of[j]])).astype(y_ref.dtype)


def _delta(p3, abt, conv_w, conv_prev, s0, state_layer, a_log, dt_bias, onorm, *, t, valid_len):
    b, l, _ = p3.shape
    assert l % t == 0
    c = min(DN_CHUNK, t)
    hp = DELTA_HEADS_PER_STEP
    nh = DN_HEADS
    ng = nh // hp
    w = 128 * hp
    cb_q = OFF["qkv"] // w
    cb_z = OFF["zb"] // w
    assert OFF["qkv"] % w == 0 and OFF["zb"] % w == 0 and W_B % w == 0

    def pspec(cb0):
        return pl.BlockSpec((None, t, w), lambda i, h, n, a, d: (i, n, cb0 + h))

    def cwspec(sidx):
        return pl.BlockSpec((CONV_K, w), lambda i, h, n, a, d: (0, sidx * ng + h))

    def cpspec(sidx):
        return pl.BlockSpec((None, None, CONV_K - 1, w),
                            lambda i, h, n, a, d: (state_layer, i, 0, sidx * ng + h))

    grid_spec = pltpu.PrefetchScalarGridSpec(
        num_scalar_prefetch=2,
        grid=(b, ng, l // t),
        in_specs=[pspec(cb_q), pspec(cb_q + ng), pspec(cb_q + 2 * ng), pspec(cb_z),
                  pl.BlockSpec((None, 2 * nh, t), lambda i, h, n, a, d: (i, 0, n)),
                  cwspec(0), cwspec(1), cwspec(2), cpspec(0), cpspec(1), cpspec(2),
                  pl.BlockSpec((None, None, hp, DN_HEAD, DN_HEAD),
                               lambda i, h, n, a, d: (state_layer, i, h, 0, 0)),
                  pl.BlockSpec((1, DN_HEAD), lambda i, h, n, a, d: (0, 0))],
        out_specs=[pl.BlockSpec((None, t, w), lambda i, h, n, a, d: (i, n, h)),
                   pl.BlockSpec((None, hp, DN_HEAD, DN_HEAD), lambda i, h, n, a, d: (i, h, 0, 0))],
        scratch_shapes=[pltpu.VMEM((3 * hp, t + 8, 128), F32), pltpu.VMEM((hp, DN_HEAD, DN_HEAD), F32)],
    )
    return pl.pallas_call(
        functools.partial(_delta_kernel, t=t, c=c, hp=hp, valid_len=valid_len),
        grid_spec=grid_spec,
        out_shape=[jax.ShapeDtypeStruct((b, l, W_B), BF16),
                   jax.ShapeDtypeStruct((b, nh, DN_HEAD, DN_HEAD), F32)],
        compiler_params=_cp("delta", 3, fuse_inputs=15),
        name="delta",
    )(a_log, dt_bias, p3, p3, p3, p3, abt, conv_w, conv_w, conv_w,
      conv_prev, conv_prev, conv_prev, s0, onorm.reshape(1, DN_HEAD))


def _ordered_bits_to_float(u):
    key = u ^ jnp.int32(INT_MIN)
    bits = jnp.where(key < 0, key ^ jnp.int32(0x7FFFFFFF), key)
    return pltpu.bitcast(bits, F32)


def _topk_threshold(count_ge, shape, topk):
    def bit_body(i, u):
        bit = jnp.left_shift(jnp.int32(1), 31 - i)
        cand_u = u | bit
        cnt = count_ge(_ordered_bits_to_float(cand_u))
        return jnp.where(cnt >= float(topk), cand_u, u)

    u = lax.fori_loop(0, 32, bit_body, jnp.zeros(shape, I32))
    return _ordered_bits_to_float(u)


def _softmax_update(m_ref, l_ref, acc_ref, idx, s, v_tiles):
    m_old = m_ref[idx]
    m_new = jnp.maximum(m_old, jnp.max(s, axis=1, keepdims=True))
    m_safe = jnp.where(m_new == NEG_INF, 0.0, m_new)
    alpha = jnp.exp(m_old - m_safe)
    p = jnp.exp(s - m_safe)
    l_ref[idx] = alpha * l_ref[idx] + jnp.sum(p, axis=1, keepdims=True)
    acc = alpha * acc_ref[idx]
    p16 = p.astype(BF16)
    k0 = 0
    for vt in v_tiles:
        acc = acc + _dot(p16[:, k0:k0 + vt.shape[0]], vt)
        k0 += vt.shape[0]
    acc_ref[idx] = acc
    m_ref[idx] = m_new


def _dsa_prompt_kernel(qi_ref, tq_ref, tall_ref, qc_ref, k_ref, v_ref, z_ref, y_ref,
                       sc_scr, m_scr, l_scr, acc_scr, *, l, topk, kt):
    qb = pl.program_id(1)
    nq = Q_BLOCK
    n_tiles = (qb * nq + nq + kt - 1) // kt
    n_pairs = C_HEADS // 2
    w_rows = tq_ref[...].T[TAIL_WI:TAIL_WI + IDX_HEADS, :] * ((IDX_HEADS ** -0.5) * (IDX_DIM ** -0.5))
    rq = []
    for j in range(IDX_HEADS // 2):
        a = qi_ref[:, j * 128:(j + 1) * 128]
        rq.append(jnp.concatenate([a, pltpu.roll(a, IDX_DIM, 1)], axis=0).astype(BF16))
    kpos = lax.broadcasted_iota(I32, (kt, nq), 0)
    qpos = qb * nq + lax.broadcasted_iota(I32, (kt, nq), 1)
    lane = lax.broadcasted_iota(I32, (kt, 128), 1)

    def score_tile(ti, carry):
        k0 = pl.multiple_of(ti * kt, kt)
        ki = jnp.where(lane < IDX_DIM, tall_ref[pl.ds(k0, kt), :], 0.0).astype(BF16)
        score = jnp.zeros((kt, nq), F32)
        for j in range(IDX_HEADS // 2):
            lg = _dot_nt(ki, rq[j])
            score = score + w_rows[2 * j:2 * j + 1, :] * jnp.maximum(lg[:, :nq], 0.0)
            score = score + w_rows[2 * j + 1:2 * j + 2, :] * jnp.maximum(lg[:, nq:], 0.0)
        sc_scr[pl.ds(k0, kt), :] = jnp.where(kpos + k0 <= qpos, score, NEG_INF)
        return carry

    lax.fori_loop(0, n_tiles, score_tile, 0)

    ct = 256
    n_ct = (qb * nq + nq + ct - 1) // ct
    n_acc = 4

    def count(cand, strict):
        cand8 = jnp.broadcast_to(cand, (8, nq))

        def tile_body(ti, accs):
            k0 = pl.multiple_of(ti * ct, ct)
            tile = sc_scr[pl.ds(k0, ct), :]
            accs = list(accs)
            for r in range(ct // 8):
                blk = tile[8 * r:8 * r + 8, :]
                hit = (blk > cand8) if strict else (blk >= cand8)
                accs[r % n_acc] = accs[r % n_acc] + jnp.where(hit, 1.0, 0.0)
            return tuple(accs)

        accs = lax.fori_loop(0, n_ct, tile_body, tuple(jnp.zeros((8, nq), F32) for _ in range(n_acc)))
        acc = (accs[0] + accs[1]) + (accs[2] + accs[3])
        return jnp.sum(acc, axis=0, keepdims=True)

    thr = _topk_threshold(lambda cand: count(cand, False), (1, nq), topk)
    keep_all = (qb * nq + lax.broadcasted_iota(I32, (1, nq), 1)) < topk
    need = float(topk) - count(thr, True)
    tri16 = jnp.where(lax.broadcasted_iota(I32, (128, 128), 1) <= lax.broadcasted_iota(I32, (128, 128), 0),
                      1.0, 0.0).astype(BF16)

    def select_tile(ti, seen):
        k0 = pl.multiple_of(ti * kt, kt)
        sc = sc_scr[pl.ds(k0, kt), :]
        eq = sc == thr
        eq_f = jnp.where(eq, 1.0, 0.0)
        eq16 = eq_f.astype(BF16)
        prefs = [_dot(tri16, eq16[r:r + 128]) for r in range(0, kt, 128)]
        ranks = []
        for pref in prefs:
            ranks.append(seen + pref)
            seen = seen + pref[127:128, :]
        tie_ok = (jnp.concatenate(ranks, axis=0) - eq_f) < need
        chosen = jnp.where(sc > thr, 0.0, jnp.where(eq, jnp.where(tie_ok, 0.0, NEG_INF), NEG_INF))
        sc_scr[pl.ds(k0, kt), :] = jnp.where(keep_all, jnp.where(sc > NEG_INF, 0.0, NEG_INF), chosen)
        return seen

    lax.fori_loop(0, n_tiles, select_tile, jnp.zeros((1, nq), F32))

    m_scr[...] = jnp.full(m_scr.shape, NEG_INF, F32)
    l_scr[...] = jnp.zeros(l_scr.shape, F32)
    acc_scr[...] = jnp.zeros(acc_scr.shape, F32)
    scale = (C_HEAD ** -0.5) * 1.4426950408889634
    qp = []
    for pr in range(n_pairs):
        c0 = slice(2 * pr * C_HEAD, (2 * pr + 1) * C_HEAD)
        c1 = slice((2 * pr + 1) * C_HEAD, (2 * pr + 2) * C_HEAD)
        qp.append(jnp.concatenate([qc_ref[:, c0] * scale, qc_ref[:, c1] * scale], axis=0).astype(BF16))
    pairs_per_kv = n_pairs // C_KV_HEADS

    def attend_tile(ti, carry):
        k0 = pl.multiple_of(ti * kt, kt)
        bias = sc_scr[pl.ds(k0, kt), :]
        k16 = [k_ref[pl.ds(k0, kt), hk * C_HEAD:(hk + 1) * C_HEAD].astype(BF16) for hk in range(C_KV_HEADS)]
        vt16 = [v_ref[pl.ds(k0, kt), hk * C_HEAD:(hk + 1) * C_HEAD].T.astype(BF16) for hk in range(C_KV_HEADS)]
        bias2 = jnp.concatenate([bias, bias], axis=1)
        pr_all = range(n_pairs)
        s = [_dot_nt(k16[pr // pairs_per_kv], qp[pr]) + bias2 for pr in pr_all]
        m_old = [m_scr[pr] for pr in pr_all]
        m_new = [jnp.maximum(m_old[pr], jnp.max(s[pr], axis=0, keepdims=True)) for pr in pr_all]
        m_safe = [jnp.where(m_new[pr] == NEG_INF, 0.0, m_new[pr]) for pr in pr_all]
        alpha = [jnp.exp2(m_old[pr] - m_safe[pr]) for pr in pr_all]
        p = [jnp.exp2(s[pr] - m_safe[pr]) for pr in pr_all]
        for pr in pr_all:
            l_scr[pr] = alpha[pr] * l_scr[pr] + jnp.sum(p[pr], axis=0, keepdims=True)
            acc_scr[pr] = alpha[pr] * acc_scr[pr] + _dot(vt16[pr // pairs_per_kv], p[pr].astype(BF16))
            m_scr[pr] = m_new[pr]
        return carry

    lax.fori_loop(0, n_tiles, attend_tile, 0)
    for pr in range(n_pairs):
        o_t = acc_scr[pr] / l_scr[pr]
        for e in range(2):
            cols = slice((2 * pr + e) * C_HEAD, (2 * pr + e + 1) * C_HEAD)
            o = o_t[:, e * nq:(e + 1) * nq].T
            y_ref[:, cols] = (o * _silu(z_ref[:, cols])).astype(y_ref.dtype)


def _dsa_prompt(p3):
    b, l, _ = p3.shape
    assert l % Q_BLOCK == 0
    topk = min(TOPK_MAX, l // 4)
    nq = Q_BLOCK
    kt = DSA_KEY_TILE if l % DSA_KEY_TILE == 0 else 256
    assert l % kt == 0
    n_qb = l // nq
    return pl.pallas_call(
        functools.partial(_dsa_prompt_kernel, l=l, topk=topk, kt=kt),
        grid=(b, n_qb),
        in_specs=[pl.BlockSpec((None, nq, W_C), lambda i, j: (i, j, OFF["qi"] // W_C)),
                  pl.BlockSpec((None, nq, 128), lambda i, j: (i, j, OFF_TAIL // 128)),
                  pl.BlockSpec((None, l, 128), lambda i, j: (i, 0, OFF_TAIL // 128)),
                  pl.BlockSpec((None, nq, W_C), lambda i, j: (i, j, OFF["qc"] // W_C)),
                  pl.BlockSpec((None, l, C_KV), lambda i, j: (i, 0, OFF["kc"] // C_KV)),
                  pl.BlockSpec((None, l, C_KV), lambda i, j: (i, 0, OFF["vc"] // C_KV)),
                  pl.BlockSpec((None, nq, W_C), lambda i, j: (i, j, OFF["zc"] // W_C))],
        out_specs=pl.BlockSpec((None, nq, W_C), lambda i, j: (i, j, 0)),
        out_shape=jax.ShapeDtypeStruct((b, l, W_C), BF16),
        scratch_shapes=[pltpu.VMEM((l, nq), F32),
                        pltpu.VMEM((C_HEADS // 2, 1, 2 * nq), F32), pltpu.VMEM((C_HEADS // 2, 1, 2 * nq), F32),
                        pltpu.VMEM((C_HEADS // 2, C_HEAD, 2 * nq), F32)],
        compiler_params=_cp("dsa_prompt", 2),
        name="dsa_prompt",
    )(p3, p3, p3, p3, p3, p3, p3)


def _idx_scores(logits, wcol, t):
    r = jnp.maximum(logits, 0.0) * wcol
    return jnp.sum(r.reshape(t, IDX_HEADS, logits.shape[1]), axis=1)


def _dsa_s_score_kernel(pt_ref, q_ref, w_ref, kn_ref, *refs, pg, t):
    pages = refs[:pg]
    out_ref, new_ref = refs[pg:]
    q16 = q_ref[...].astype(BF16)
    wcol = w_ref[...] * ((IDX_HEADS ** -0.5) * (IDX_DIM ** -0.5))
    keys16 = jnp.concatenate([pages[i][...] for i in range(pg)], axis=1).astype(BF16)
    out_ref[...] = _idx_scores(_dot(q16, keys16), wcol, t)

    @pl.when(pl.program_id(1) == 0)
    def _():
        new_ref[...] = _idx_scores(_dot_nt(q16, kn_ref[...].astype(BF16)), wcol, t)


def _dsa_s_select_kernel(sc_ref, scn_ref, tq_ref, bias_ref, *, nk, topk):
    rows = sc_ref.shape[0]
    tq = tq_ref[...]
    new_ok = lax.broadcasted_iota(I32, (rows, 128), 1) <= tq
    sc_new = jnp.where(new_ok, scn_ref[...], NEG_INF)
    sc_past = sc_ref[...]

    def count_ge(cand):
        c1 = jnp.sum(jnp.where(sc_past >= cand, 1.0, 0.0), axis=1, keepdims=True)
        c2 = jnp.sum(jnp.where(sc_new >= cand, 1.0, 0.0), axis=1, keepdims=True)
        return c1 + c2

    thr = _topk_threshold(count_ge, (rows, 1), topk)
    keep_all = (nk + 1 + tq[:, 0:1]) <= topk
    n_gt = (jnp.sum(jnp.where(sc_past > thr, 1.0, 0.0), axis=1, keepdims=True)
            + jnp.sum(jnp.where(sc_new > thr, 1.0, 0.0), axis=1, keepdims=True))
    need = float(topk) - n_gt
    tri16 = jnp.where(lax.broadcasted_iota(I32, (128, 128), 0) <= lax.broadcasted_iota(I32, (128, 128), 1),
                      1.0, 0.0).astype(BF16)

    def bias_blocks(scs, seen):
        eqs = [sc == thr for sc in scs]
        eq_fs = [jnp.where(eq, 1.0, 0.0) for eq in eqs]
        prefs = [_dot(eq_f.astype(BF16), tri16) for eq_f in eq_fs]
        out = []
        for sc, eq, eq_f, pref in zip(scs, eqs, eq_fs, prefs):
            tie_ok = (seen + pref - eq_f) < need
            chosen = jnp.where(sc > thr, 0.0, jnp.where(eq, jnp.where(tie_ok, 0.0, NEG_INF), NEG_INF))
            out.append(jnp.where(keep_all, jnp.where(sc > NEG_INF, 0.0, NEG_INF), chosen))
            seen = seen + pref[:, 127:128]
        return out, seen

    n_blk = nk // 128
    grp = 8 if n_blk % 8 == 0 else 1

    def past_group(gi, seen):
        lanes = [pl.ds(pl.multiple_of((gi * grp + i) * 128, 128), 128) for i in range(grp)]
        out, seen = bias_blocks([sc_ref[:, ln] for ln in lanes], seen)
        for ln, bias in zip(lanes, out):
            bias_ref[:, ln] = bias
        return seen

    seen = lax.fori_loop(0, n_blk // grp, past_group, jnp.zeros((rows, 1), F32))
    (bias_new,), _ = bias_blocks([sc_new], seen)
    bias_ref[:, nk:] = bias_new


def _dsa_s_attend_kernel(pt_ref, q_ref, bias_ref, biasn_ref, kn_ref, vn_ref, z_ref, *refs, pg):
    kpages = refs[:pg]
    vpages = refs[pg:2 * pg]
    o_ref = refs[2 * pg]
    m_scr, l_scr, acc_scr = refs[2 * pg + 1:]
    g = pl.program_id(1)
    scale = C_HEAD ** -0.5
    page = kpages[0].shape[0] // C_KV_HEADS

    @pl.when(g == 0)
    def _():
        m_scr[...] = jnp.full(m_scr.shape, NEG_INF, F32)
        l_scr[...] = jnp.zeros(l_scr.shape, F32)
        acc_scr[...] = jnp.zeros(acc_scr.shape, F32)
        for hk in range(C_KV_HEADS):
            cols = slice(hk * C_HEAD, (hk + 1) * C_HEAD)
            q16 = (q_ref[hk] * scale).astype(BF16)
            s = _dot_nt(q16, kn_ref[:, cols].astype(BF16)) + biasn_ref[...]
            _softmax_update(m_scr, l_scr, acc_scr, hk, s, [vn_ref[:, cols].astype(BF16)])

    def two_pages(refs_, i, rows):
        return jnp.concatenate([refs_[i][rows, :], refs_[i + 1][rows, :]], axis=0).astype(BF16)

    for hk in range(C_KV_HEADS):
        q16 = (q_ref[hk] * scale).astype(BF16)
        rows = pl.ds(hk, page, stride=C_KV_HEADS)
        tiles = [_dot_nt(q16, two_pages(kpages, i, rows)) for i in range(0, pg, 2)]
        s = jnp.concatenate(tiles, axis=1) + bias_ref[...]
        _softmax_update(m_scr, l_scr, acc_scr, hk, s, [two_pages(vpages, i, rows) for i in range(0, pg, 2)])

    @pl.when(g == pl.num_programs(1) - 1)
    def _():
        for hk in range(C_KV_HEADS):
            o_ref[hk] = (acc_scr[hk] / l_scr[hk]) * _silu(z_ref[hk])


def _dsa_sample(p3s, cache_k, cache_v, cache_kidx, page_table, layer):
    b, t, _ = p3s.shape
    n_pages = page_table.shape[1]
    page = cache_k.shape[2]
    assert page == 128
    past = n_pages * page
    topk = min(TOPK_MAX, (past + t) // 4)
    pg = 32 if n_pages % 32 == 0 else (16 if n_pages % 16 == 0 else n_pages)
    assert pg % 2 == 0
    ng = n_pages // pg
    rows = t * C_GROUPS

    qi = p3s[:, :, OFF["qi"]:OFF["qi"] + IDX_HEADS * IDX_DIM].reshape(b, t * IDX_HEADS, IDX_DIM)
    wi = p3s[:, :, OFF_TAIL + TAIL_WI:OFF_TAIL + TAIL_WI + IDX_HEADS].reshape(b, t * IDX_HEADS, 1)
    ki_new = jnp.pad(p3s[:, :, OFF_TAIL:OFF_TAIL + IDX_DIM], ((0, 0), (0, 128 - t), (0, 0)))
    k_new = jnp.pad(p3s[:, :, OFF["kc"]:OFF["kc"] + C_KV], ((0, 0), (0, 128 - t), (0, 0)))
    v_new = jnp.pad(p3s[:, :, OFF["vc"]:OFF["vc"] + C_KV], ((0, 0), (0, 128 - t), (0, 0)))

    def heads_major(a):
        a = a.reshape(b, t, C_KV_HEADS, C_GROUPS, C_HEAD)
        return jnp.transpose(a, (0, 2, 1, 3, 4)).reshape(b, C_KV_HEADS, rows, C_HEAD)

    qh = heads_major(p3s[:, :, OFF["qc"]:OFF["qc"] + W_C])
    zh = heads_major(p3s[:, :, OFF["zc"]:OFF["zc"] + W_C])

    kidx_t = jnp.swapaxes(cache_kidx, 2, 3)
    score_spec = pltpu.PrefetchScalarGridSpec(
        num_scalar_prefetch=1,
        grid=(b, ng),
        in_specs=[pl.BlockSpec((None, t * IDX_HEADS, IDX_DIM), lambda i, g, pt: (i, 0, 0)),
                  pl.BlockSpec((None, t * IDX_HEADS, 1), lambda i, g, pt: (i, 0, 0)),
                  pl.BlockSpec((None, 128, IDX_DIM), lambda i, g, pt: (i, 0, 0))]
                 + [pl.BlockSpec((None, None, IDX_DIM, page),
                                 lambda i, g, pt, j=j: (layer, pt[i, g * pg + j], 0, 0)) for j in range(pg)],
        out_specs=[pl.BlockSpec((None, t, pg * page), lambda i, g, pt: (i, 0, g)),
                   pl.BlockSpec((None, t, 128), lambda i, g, pt: (i, 0, 0))],
    )
    scores, scores_new = pl.pallas_call(
        functools.partial(_dsa_s_score_kernel, pg=pg, t=t),
        grid_spec=score_spec,
        out_shape=[jax.ShapeDtypeStruct((b, t, past), F32), jax.ShapeDtypeStruct((b, t, 128), F32)],
        compiler_params=_cp("dsa_s_score", 2, fuse_inputs=4 + pg),
        name="dsa_s_score",
    )(page_table, qi, wi, ki_new, *([kidx_t] * pg))

    tq = jnp.broadcast_to(jnp.tile(jnp.arange(t, dtype=I32), b)[:, None], (b * t, 128))
    bias = pl.pallas_call(
        functools.partial(_dsa_s_select_kernel, nk=past, topk=topk),
        grid=(1,),
        in_specs=[pl.BlockSpec((b * t, past), lambda i: (0, 0)),
                  pl.BlockSpec((b * t, 128), lambda i: (0, 0)),
                  pl.BlockSpec((b * t, 128), lambda i: (0, 0))],
        out_specs=pl.BlockSpec((b * t, past + 128), lambda i: (0, 0)),
        out_shape=jax.ShapeDtypeStruct((b * t, past + 128), F32),
        compiler_params=_cp("dsa_s_select", 1, fuse_inputs=3),
        name="dsa_s_select",
    )(scores.reshape(b * t, past), scores_new.reshape(b * t, 128), tq).reshape(b, t, past + 128)

    bias_rows = jnp.repeat(bias, C_GROUPS, axis=1)
    bias_past = bias_rows[:, :, :past]
    bias_new = bias_rows[:, :, past:]

    ck = cache_k.reshape(cache_k.shape[0], cache_k.shape[1], page * C_KV_HEADS, C_HEAD)
    cv = cache_v.reshape(cache_v.shape[0], cache_v.shape[1], page * C_KV_HEADS, C_HEAD)
    pspec = [pl.BlockSpec((None, None, page * C_KV_HEADS, C_HEAD),
                          lambda i, g, pt, j=j: (layer, pt[i, g * pg + j], 0, 0)) for j in range(pg)]
    attend_spec = pltpu.PrefetchScalarGridSpec(
        num_scalar_prefetch=1,
        grid=(b, ng),
        in_specs=[pl.BlockSpec((None, C_KV_HEADS, rows, C_HEAD), lambda i, g, pt: (i, 0, 0, 0)),
                  pl.BlockSpec((None, rows, pg * page), lambda i, g, pt: (i, 0, g)),
                  pl.BlockSpec((None, rows, 128), lambda i, g, pt: (i, 0, 0)),
                  pl.BlockSpec((None, 128, C_KV), lambda i, g, pt: (i, 0, 0)),
                  pl.BlockSpec((None, 128, C_KV), lambda i, g, pt: (i, 0, 0)),
                  pl.BlockSpec((None, C_KV_HEADS, rows, C_HEAD), lambda i, g, pt: (i, 0, 0, 0))]
                 + pspec + pspec,
        out_specs=pl.BlockSpec((None, C_KV_HEADS, rows, C_HEAD), lambda i, g, pt: (i, 0, 0, 0)),
        scratch_shapes=[pltpu.VMEM((C_KV_HEADS, rows, 1), F32), pltpu.VMEM((C_KV_HEADS, rows, 1), F32),
                        pltpu.VMEM((C_KV_HEADS, rows, C_HEAD), F32)],
    )
    oh = pl.pallas_call(
        functools.partial(_dsa_s_attend_kernel, pg=pg),
        grid_spec=attend_spec,
        out_shape=jax.ShapeDtypeStruct((b, C_KV_HEADS, rows, C_HEAD), F32),
        compiler_params=_cp("dsa_s_attend", 2, fuse_inputs=7 + 2 * pg),
        name="dsa_s_attend",
    )(page_table, qh, bias_past, bias_new, k_new, v_new, zh, *([ck] * pg), *([cv] * pg))
    y = jnp.transpose(oh.reshape(b, C_KV_HEADS, t, C_GROUPS, C_HEAD), (0, 2, 1, 3, 4)).reshape(b, t, W_C)
    return y.astype(BF16)


def _outproj_kernel(ya_ref, yb_ref, yc_ref, w_ref, x_ref, gate_ref, o_ref, mix_scr):
    @pl.when(pl.program_id(1) == 0)
    def _():
        mix_scr[:, 0:W_A] = ya_ref[...]
        mix_scr[:, W_A:W_A + W_B] = yb_ref[...]
        mix_scr[:, W_A + W_B:] = yc_ref[...]

    o_ref[...] = x_ref[...] + gate_ref[...] * _dot(mix_scr[...], w_ref[...])


def _outproj_norm_kernel(ya_ref, yb_ref, yc_ref, w_ref, x_ref, gate_ref, g_ref, o_ref, mix_scr, ssq_scr):
    j = pl.program_id(1)
    tn = x_ref.shape[1]

    @pl.when(j == 0)
    def _():
        mix_scr[:, 0:W_A] = ya_ref[...]
        mix_scr[:, W_A:W_A + W_B] = yb_ref[...]
        mix_scr[:, W_A + W_B:] = yc_ref[...]
        ssq_scr[...] = jnp.zeros(ssq_scr.shape, F32)

    x_new = x_ref[...] + gate_ref[...] * _dot(mix_scr[...], w_ref[...])
    o_ref[:, pl.ds(pl.multiple_of(j * tn, tn), tn)] = x_new
    ssq_scr[...] += jnp.sum(x_new * x_new, axis=-1, keepdims=True)

    @pl.when(j == pl.num_programs(1) - 1)
    def _():
        tm, d = o_ref.shape
        rc = 64 if tm % 64 == 0 else tm

        def body(c, carry):
            r = pl.ds(pl.multiple_of(c * rc, rc), rc)
            o_ref[r, :] = o_ref[r, :] * lax.rsqrt(ssq_scr[r, :] / d + EPS) * g_ref[...]
            return carry

        lax.fori_loop(0, tm // rc, body, 0)


def _outproj(ya, yb, yc, w_bf16, layer, x2, gate, *, rows_per_batch, mod_row0, g_final=None):
    m, d = x2.shape
    tn = 1024
    if gate.ndim == 3:
        tm = 512
        gate_spec = pl.BlockSpec((None, 1, tn),
                                 lambda i, j: ((i * tm) // rows_per_batch + mod_row0, 0, 2 * (d // tn) + j))
    else:
        tm = m
        gate_spec = pl.BlockSpec((tm, tn), lambda i, j: (i, j))
    assert m % tm == 0
    in_specs = [pl.BlockSpec((tm, W_A), lambda i, j: (i, 0)),
                pl.BlockSpec((tm, W_B), lambda i, j: (i, 0)),
                pl.BlockSpec((tm, W_C), lambda i, j: (i, 0)),
                pl.BlockSpec((None, d, tn), lambda i, j: (layer, 0, j)),
                pl.BlockSpec((tm, tn), lambda i, j: (i, j)),
                gate_spec]
    operands = [ya, yb, yc, w_bf16, x2, gate]
    scratch = [pltpu.VMEM((tm, d), BF16)]
    if g_final is None:
        body, name = _outproj_kernel, "outproj"
        out_spec = pl.BlockSpec((tm, tn), lambda i, j: (i, j))
    else:
        body, name = _outproj_norm_kernel, "outproj_norm"
        in_specs.append(pl.BlockSpec((1, d), lambda i, j: (0, 0)))
        operands.append(g_final.reshape(1, d))
        scratch.append(pltpu.VMEM((tm, 1), F32))
        out_spec = pl.BlockSpec((tm, d), lambda i, j: (i, 0))
    return pl.pallas_call(
        body,
        grid=(m // tm, d // tn),
        in_specs=in_specs,
        out_specs=out_spec,
        out_shape=jax.ShapeDtypeStruct((m, d), F32),
        scratch_shapes=scratch,
        compiler_params=_cp(name, 2, fuse_inputs=len(operands)),
        name=name,
    )(*operands)


def _kv_rows_kernel(*refs, depth, tm):
    k_refs, v_refs = refs[:depth], refs[depth:2 * depth]
    pk_ref, pv_ref = refs[2 * depth:]
    for lyr in range(depth):
        for h in range(C_KV_HEADS):
            cols = slice(h * C_HEAD, (h + 1) * C_HEAD)
            rows = pl.ds(h, tm, stride=C_KV_HEADS)
            pk_ref.at[lyr][rows, :] = k_refs[lyr][:, cols]
            pv_ref.at[lyr][rows, :] = v_refs[lyr][:, cols]


def _kv_rows(p3_layers):
    depth = len(p3_layers)
    b, l, _ = p3_layers[0].shape
    tm = 512 if l % 512 == 0 else l
    kspec = pl.BlockSpec((None, tm, C_KV), lambda i, j: (i, j, OFF["kc"] // C_KV))
    vspec = pl.BlockSpec((None, tm, C_KV), lambda i, j: (i, j, OFF["vc"] // C_KV))
    ospec = pl.BlockSpec((depth, None, tm * C_KV_HEADS, C_HEAD), lambda i, j: (0, i, j, 0))
    oshape = jax.ShapeDtypeStruct((depth, b, l * C_KV_HEADS, C_HEAD), F32)
    pk, pv = pl.pallas_call(
        functools.partial(_kv_rows_kernel, depth=depth, tm=tm),
        grid=(b, l // tm),
        in_specs=[kspec] * depth + [vspec] * depth,
        out_specs=[ospec, ospec],
        out_shape=[oshape, oshape],
        compiler_params=_cp("kv_rows", 2),
        name="kv_rows",
    )(*p3_layers, *p3_layers)
    shape = (depth, b, l, C_KV_HEADS, C_HEAD)
    return pk.reshape(shape), pv.reshape(shape)


def _w_in_tile_table():
    n_t = NP // 128
    table, n_plain = [], None
    for dt in range(n_t - 1):
        c = dt * 128
        seg = [n for n in _DST_ORDER if OFF[n] <= c < OFF[n] + _SRC_OFF[n][1]][0]
        src = _SRC_OFF[seg][0] + (c - OFF[seg])
        if src % 128 == 0:
            assert n_plain is None
        else:
            assert src % 128 == 32
            if n_plain is None:
                n_plain = dt
        table.append(src // 128)
    t_kw, t_ab = _SRC_OFF["ki"][0] // 128, _SRC_OFF["a"][0] // 128
    assert _SRC_OFF["ki"][0] % 128 == 32 and _SRC_OFF["wi"][0] == t_kw * 128 + TAIL_WI
    assert _SRC_OFF["a"][0] % 128 == 0 and _SRC_OFF["b"][0] == _SRC_OFF["a"][0] + DN_HEADS
    assert TAIL_A == IDX_DIM and TAIL_B == TAIL_A + DN_HEADS and TAIL_WI == TAIL_B + DN_HEADS
    tile_a = table + [t_kw]
    rows_b = [t_ab * 4] * n_plain + [(s + 1) * 4 for s in table[n_plain:]] + [t_ab * 4]
    return tile_a, rows_b, n_plain


def _wprep_kernel(ta_ref, tb_ref, a_ref, b_ref, xs_ref, g_ref, sc_ref, sh_ref, o_ref, ps_ref, h_scr,
                  *, n_plain, n_t):
    dt = pl.program_id(0)
    dt_o = o_ref.dtype

    @pl.when(dt == 0)
    def _():
        x = xs_ref[...]
        y = x * lax.rsqrt(jnp.mean(x * x, axis=-1, keepdims=True) + EPS)
        h_scr[...] = ((y * g_ref[...]) * (1.0 + sc_ref[...]) + sh_ref[...]).astype(BF16)

    @pl.when(dt < n_plain)
    def _():
        o_ref[...] = a_ref[...].astype(dt_o)

    @pl.when((dt >= n_plain) & (dt < n_t - 1))
    def _():
        o_ref[0:96, :] = a_ref[32:128, :].astype(dt_o)
        o_ref[96:128, :] = b_ref[...].astype(dt_o)

    @pl.when(dt == n_t - 1)
    def _():
        o_ref[0:TAIL_A, :] = a_ref[32:32 + IDX_DIM, :].astype(dt_o)
        o_ref[TAIL_A:TAIL_WI, :] = b_ref[...].astype(dt_o)
        o_ref[TAIL_WI:TAIL_WI + IDX_HEADS, :] = a_ref[TAIL_WI:TAIL_WI + IDX_HEADS, :].astype(dt_o)
        o_ref[TAIL_WI + IDX_HEADS:, :] = jnp.zeros((128 - TAIL_WI - IDX_HEADS, o_ref.shape[1]), dt_o)

    ps_ref[...] = _dot_nt(h_scr[...], o_ref[...])


def _permute_w_in(w_in, layer, xs, g, sc, sh):
    _, d, _ = w_in.shape
    ms = xs.shape[0]
    w_t = jnp.swapaxes(w_in, 1, 2)
    tile_a, rows_b, n_plain = _w_in_tile_table()
    n_t = NP // 128
    const = lambda t, ta, tb: (0, 0)
    grid_spec = pltpu.PrefetchScalarGridSpec(
        num_scalar_prefetch=2,
        grid=(n_t,),
        in_specs=[pl.BlockSpec((None, 128, d), lambda t, ta, tb: (layer, ta[t], 0)),
                  pl.BlockSpec((None, 32, d), lambda t, ta, tb: (layer, tb[t], 0)),
                  pl.BlockSpec((ms, d), const), pl.BlockSpec((1, d), const),
                  pl.BlockSpec((ms, d), const), pl.BlockSpec((ms, d), const)],
        out_specs=[pl.BlockSpec((128, d), lambda t, ta, tb: (t, 0)),
                   pl.BlockSpec((ms, 128), lambda t, ta, tb: (0, t))],
        scratch_shapes=[pltpu.VMEM((ms, d), BF16)],
    )
    return pl.pallas_call(
        functools.partial(_wprep_kernel, n_plain=n_plain, n_t=n_t),
        grid_spec=grid_spec,
        out_shape=[jax.ShapeDtypeStruct((NP, d), BF16), jax.ShapeDtypeStruct((ms, NP), F32)],
        compiler_params=_cp("w_in_layout", 1),
        name="w_in_layout",
    )(jnp.asarray(tile_a, I32), jnp.asarray(rows_b, I32), w_t, w_t, xs, g, sc, sh)


def _delta_t(l):
    for t in (128, 64):
        if l % t == 0:
            return t
    raise ValueError("sequence length must be a multiple of 64")


def kernel(x_prompt, x_sample, cache_k, cache_v, cache_kidx, state_dn, state_conv, page_table,
           c_prompt, c_sample, w_ada, b_ada, g_norm, w_in, a_vnorm, a_ws, a_bs, dn_conv_w,
           dn_a_log, dn_dt_bias, dn_onorm, w_out, g_final):
    bp, lp, d = x_prompt.shape
    bs, ls, _ = x_sample.shape
    depth = w_ada.shape[0]
    assert d == D_MODEL and w_in.shape[2] == D_IN
    assert CONV_K - 1 <= ls <= DN_CHUNK

    n_c = bp + bs
    c_rows = jnp.concatenate([c_prompt, c_sample], axis=0)
    r_pad = (-n_c) % 8
    if r_pad:
        c_rows = jnp.pad(c_rows, ((0, r_pad), (0, 0)))
    m_all = _ada(c_rows, w_ada, b_ada)

    xp = x_prompt.reshape(bp * lp, d)
    xs = x_sample.reshape(bs * ls, d)
    ls_pad = 8 * ((ls + 7) // 8)
    outs = {k: [] for k in ("p3", "pki", "pdn", "pconv", "sk", "sv", "ski", "sdn", "sconv", "samlp")}
    zeros_conv = jnp.zeros((1, bp, CONV_K - 1, DN_CONV_DIM), F32)
    zeros_state = jnp.zeros((1, bp, DN_HEADS, DN_HEAD, DN_HEAD), F32)

    w_out_all = w_out.astype(BF16)

    def g_last(layer):
        return g_final if layer == depth - 1 else None

    for l in range(depth):
        g_l = g_norm[l].reshape(1, d)
        m_l = m_all[l]
        m3 = m_l.reshape(m_l.shape[0], 1, 3 * d)
        ms = jnp.repeat(m_l[bp:bp + bs], ls, axis=0)

        w_in_l, ps = _permute_w_in(w_in, l, xs, g_l, ms[:, d:2 * d], ms[:, 0:d])

        pp = _inproj(xp, g_l, m3, w_in_l, rows_per_batch=lp)
        p3 = pp.reshape(bp, lp, NP)
        (ya,) = _mixa(p3, a_vnorm[l], a_ws[l], a_bs[l], emit_va=False)
        abt = jnp.transpose(p3[:, :, OFF["a"]:OFF["a"] + 2 * DN_HEADS], (0, 2, 1))
        yb, s_p = _delta(p3, abt, dn_conv_w[l], zeros_conv, zeros_state, 0, dn_a_log[l], dn_dt_bias[l],
                         dn_onorm[l], t=_delta_t(lp), valid_len=_delta_t(lp))
        yc = _dsa_prompt(p3)
        xp = _outproj(ya.reshape(bp * lp, W_A), yb.reshape(bp * lp, W_B), yc.reshape(bp * lp, W_C),
                      w_out_all, l, xp, m3, rows_per_batch=lp, mod_row0=0, g_final=g_last(l))
        outs["p3"].append(p3)
        outs["pki"].append(p3[:, :, OFF_TAIL:OFF_TAIL + IDX_DIM])
        outs["pdn"].append(s_p)
        outs["pconv"].append(p3[:, lp - (CONV_K - 1):, OFF["qkv"]:OFF["qkv"] + DN_CONV_DIM])

        p3s = ps.reshape(bs, ls, NP)
        ya_s, va_s = _mixa(p3s, a_vnorm[l], a_ws[l], a_bs[l], emit_va=True)
        p3s_pad = jnp.pad(p3s, ((0, 0), (0, ls_pad - ls), (0, 0)))
        abt_s = jnp.transpose(p3s_pad[:, :, OFF["a"]:OFF["a"] + 2 * DN_HEADS], (0, 2, 1))
        yb_s, s_s = _delta(p3s_pad, abt_s, dn_conv_w[l], state_conv, state_dn, l, dn_a_log[l],
                           dn_dt_bias[l], dn_onorm[l], t=ls_pad, valid_len=ls)
        yc_s = _dsa_sample(p3s, cache_k, cache_v, cache_kidx, page_table, l)
        xs = _outproj(ya_s.reshape(bs * ls, W_A), yb_s[:, :ls].reshape(bs * ls, W_B),
                      yc_s.reshape(bs * ls, W_C), w_out_all, l, xs, ms[:, 2 * d:3 * d],
                      rows_per_batch=ls, mod_row0=bp, g_final=g_last(l))
        outs["sk"].append(p3s[:, :, OFF["kc"]:OFF["kc"] + C_KV].reshape(bs, ls, C_KV_HEADS, C_HEAD))
        outs["sv"].append(p3s[:, :, OFF["vc"]:OFF["vc"] + C_KV].reshape(bs, ls, C_KV_HEADS, C_HEAD))
        outs["ski"].append(p3s[:, :, OFF_TAIL:OFF_TAIL + IDX_DIM])
        outs["sdn"].append(s_s)
        outs["sconv"].append(p3s[:, ls - (CONV_K - 1):, OFF["qkv"]:OFF["qkv"] + DN_CONV_DIM])
        outs["samlp"].append(va_s)

    y_prompt = xp.reshape(bp, lp, d)
    y_sample = xs.reshape(bs, ls, d)
    st = jnp.stack
    p_k, p_v = _kv_rows(outs["p3"])
    return (y_prompt, y_sample, p_k, p_v, st(outs["pki"]), st(outs["pdn"]),
            st(outs["pconv"]), st(outs["sk"]), st(outs["sv"]), st(outs["ski"]), st(outs["sdn"]),
            st(outs["sconv"]), st(outs["samlp"]))
```

```python
import functools

import jax
import jax.numpy as jnp
from jax import lax
from jax.experimental import pallas as pl
from jax.experimental.pallas import tpu as pltpu

F32 = jnp.float32
BF16 = jnp.bfloat16
I32 = jnp.int32
EPS = 1e-6
INT_MIN = -(2 ** 31)
NEG_INF = float("-inf")

D_MODEL = 4096
W_A = D_MODEL // 4
A_GROUP = 128
A_HEADS = W_A // A_GROUP
A_CHUNK = 128
W_B = D_MODEL // 2
DN_HEAD = 128
DN_HEADS = W_B // DN_HEAD
CONV_K = 4
DN_CONV_DIM = 3 * W_B
DN_CHUNK = 64
W_C = D_MODEL - W_A - W_B
C_HEAD = 128
C_HEADS = W_C // C_HEAD
C_KV_HEADS = 2
C_GROUPS = C_HEADS // C_KV_HEADS
C_KV = C_KV_HEADS * C_HEAD
IDX_HEADS = 16
IDX_DIM = 64
TOPK_MAX = 256
Q_BLOCK = 512

_SRC_SPLITS = (W_A, W_A, W_A, DN_CONV_DIM, W_B, DN_HEADS, DN_HEADS,
               W_C, C_KV, C_KV, W_C, IDX_HEADS * IDX_DIM, IDX_DIM, IDX_HEADS)
_SRC_NAMES = ("u", "v", "za", "qkv", "zb", "a", "b", "qc", "kc", "vc", "zc", "qi", "ki", "wi")
_SRC_OFF = {}
_o = 0
for _n, _w in zip(_SRC_NAMES, _SRC_SPLITS):
    _SRC_OFF[_n] = (_o, _w)
    _o += _w
D_IN = _o

_DST_ORDER = ("u", "v", "za", "qkv", "zb", "qc", "zc", "qi", "kc", "vc", "ki", "a", "b", "wi")
OFF = {}
_o = 0
for _n in _DST_ORDER:
    OFF[_n] = _o
    _o += _SRC_OFF[_n][1]
NP = ((_o + 127) // 128) * 128
OFF_TAIL = OFF["ki"]
TAIL_A = OFF["a"] - OFF_TAIL
TAIL_B = OFF["b"] - OFF_TAIL
TAIL_WI = OFF["wi"] - OFF_TAIL

MXU_COLUMNS = 256
INPROJ_TN = 6 * MXU_COLUMNS
DELTA_HEADS_PER_STEP = 8
DSA_KEY_TILE = 512

VMEM_CAPACITY_MIB = 64
VMEM_LIMIT_MIB = {
    "ada": 40, "w_in_layout": 32, "inproj": 56, "mixa": 32, "delta": 40, "dsa_prompt": 48,
    "dsa_s_score": 32, "dsa_s_select": 40, "dsa_s_attend": 40, "outproj": 48, "outproj_norm": 60,
    "kv_rows": 32,
}
assert max(VMEM_LIMIT_MIB.values()) < VMEM_CAPACITY_MIB


def _cp(name, n_axes, fuse_inputs=0):
    fusion = [True] * fuse_inputs if fuse_inputs else None
    return pltpu.CompilerParams(dimension_semantics=("arbitrary",) * n_axes,
                                vmem_limit_bytes=VMEM_LIMIT_MIB[name] * 1024 * 1024,
                                allow_input_fusion=fusion)


def _silu(x):
    return (0.5 * x) * (1.0 + jnp.tanh(0.5 * x))


def _gelu(x):
    return 0.5 * x * (1.0 + jnp.tanh(0.7978845608028654 * (x + 0.044715 * (x * x * x))))


def _dot(a, b):
    return jnp.dot(a, b, preferred_element_type=F32)


def _dot_nt(a, b):
    return lax.dot_general(a, b, (((1,), (1,)), ((), ())), preferred_element_type=F32)


def _dot_tn(a, b):
    return lax.dot_general(a, b, (((0,), (0,)), ((), ())), preferred_element_type=F32)


def _b16(a):
    return a.astype(BF16)


def _ada_kernel(c_ref, w_ref, b_ref, o_ref):
    s = _silu(c_ref[...]).astype(BF16)
    o_ref[...] = _dot(s, w_ref[...].astype(BF16)) + b_ref[...]


def _ada(c_rows, w_ada, b_ada):
    depth, d, n = w_ada.shape
    r = c_rows.shape[0]
    tn = 512
    return pl.pallas_call(
        _ada_kernel,
        grid=(depth, n // tn),
        in_specs=[pl.BlockSpec((r, d), lambda l, j: (0, 0)),
                  pl.BlockSpec((None, d, tn), lambda l, j: (l, 0, j)),
                  pl.BlockSpec((None, 1, tn), lambda l, j: (l, 0, j))],
        out_specs=pl.BlockSpec((None, r, tn), lambda l, j: (l, 0, j)),
        out_shape=jax.ShapeDtypeStruct((depth, r, n), F32),
        compiler_params=_cp("ada", 2),
        name="ada",
    )(c_rows, w_ada, b_ada.reshape(depth, 1, n))


def _inproj_kernel(x_ref, g_ref, sc_ref, sh_ref, w_ref, o_ref, h_scr, *, rc):
    @pl.when(pl.program_id(1) == 0)
    def _():
        tm = x_ref.shape[0]

        def body(c, carry):
            r = pl.ds(pl.multiple_of(c * rc, rc), rc)
            x = x_ref[r, :]
            y = x * lax.rsqrt(jnp.mean(x * x, axis=-1, keepdims=True) + EPS)
            h_scr[r, :] = ((y * g_ref[...]) * (1.0 + sc_ref[...]) + sh_ref[...]).astype(BF16)
            return carry

        lax.fori_loop(0, tm // rc, body, 0)

    o_ref[...] = _dot_nt(h_scr[...], w_ref[...])


def _inproj(x2, g, mod, w_bf16, *, rows_per_batch):
    m, d = x2.shape
    n = w_bf16.shape[0]
    tn = INPROJ_TN
    tm = 512
    assert rows_per_batch % tm == 0 and m % tm == 0
    return pl.pallas_call(
        functools.partial(_inproj_kernel, rc=64),
        grid=(m // tm, pl.cdiv(n, tn)),
        in_specs=[pl.BlockSpec((tm, d), lambda i, j: (i, 0)),
                  pl.BlockSpec((1, d), lambda i, j: (0, 0)),
                  pl.BlockSpec((None, 1, d), lambda i, j: ((i * tm) // rows_per_batch, 0, 1)),
                  pl.BlockSpec((None, 1, d), lambda i, j: ((i * tm) // rows_per_batch, 0, 0)),
                  pl.BlockSpec((tn, d), lambda i, j: (j, 0))],
        out_specs=pl.BlockSpec((tm, tn), lambda i, j: (i, j)),
        out_shape=jax.ShapeDtypeStruct((m, n), F32),
        scratch_shapes=[pltpu.VMEM((tm, d), BF16)],
        compiler_params=_cp("inproj", 2),
        name="inproj",
    )(x2, g, mod, mod, w_bf16)


def _mixa_kernel(u_ref, v_ref, z_ref, vn_ref, ws_ref, bst_ref, y_ref, *rest, c, emit_va):
    u = _gelu(u_ref[...])
    v = _gelu(v_ref[...])
    mu = jnp.mean(v, axis=-1, keepdims=True)
    dv = v - mu
    va = dv * lax.rsqrt(jnp.mean(dv * dv, axis=-1, keepdims=True) + EPS) * vn_ref[...]
    if emit_va:
        rest[0][...] = va
    z = _silu(z_ref[...])
    row = lax.broadcasted_iota(I32, (c, c), 0)
    col = lax.broadcasted_iota(I32, (c, c), 1)
    tril = col <= row
    for h in range(A_HEADS):
        cols = slice(h * A_GROUP, (h + 1) * A_GROUP)
        wm = jnp.where(tril, ws_ref[h], 0.0)
        wm16 = wm.astype(BF16)
        for ci in range(u.shape[0] // c):
            rows = slice(ci * c, (ci + 1) * c)
            vh = va[rows, cols]
            if c >= 128:
                mixed = _dot(wm16, vh.astype(BF16))
            else:
                mixed = wm[:, 0:1] * vh[0:1, :]
                for s in range(1, c):
                    mixed = mixed + wm[:, s:s + 1] * vh[s:s + 1, :]
            mixed = mixed + bst_ref[:, h:h + 1]
            y_ref[rows, cols] = (u[rows, cols] * mixed * z[rows, cols]).astype(y_ref.dtype)


def _mixa(p3, a_vnorm, a_ws, a_bs, *, emit_va):
    b, l, _ = p3.shape
    c = min(A_CHUNK, l)
    r = 2 * c if l % (2 * c) == 0 else c
    n = l // r
    ws = a_ws[:, :c, :c]
    bst = a_bs[:, :c].T
    wblk = W_A
    outs = [jax.ShapeDtypeStruct((b, l, W_A), BF16)]
    out_specs = [pl.BlockSpec((None, r, W_A), lambda i, j: (i, j, 0))]
    if emit_va:
        outs.append(jax.ShapeDtypeStruct((b, l, W_A), F32))
        out_specs.append(pl.BlockSpec((None, r, W_A), lambda i, j: (i, j, 0)))
    res = pl.pallas_call(
        functools.partial(_mixa_kernel, c=c, emit_va=emit_va),
        grid=(b, n),
        in_specs=[pl.BlockSpec((None, r, wblk), lambda i, j: (i, j, OFF["u"] // wblk)),
                  pl.BlockSpec((None, r, wblk), lambda i, j: (i, j, OFF["v"] // wblk)),
                  pl.BlockSpec((None, r, wblk), lambda i, j: (i, j, OFF["za"] // wblk)),
                  pl.BlockSpec((1, W_A), lambda i, j: (0, 0)),
                  pl.BlockSpec((A_HEADS, c, c), lambda i, j: (0, 0, 0)),
                  pl.BlockSpec((c, A_HEADS), lambda i, j: (0, 0))],
        out_specs=out_specs,
        out_shape=outs,
        compiler_params=_cp("mixa", 2),
        name="mixa",
    )(p3, p3, p3, a_vnorm.reshape(1, W_A), ws, bst)
    return res


def _delta_kernel(alog_ref, dtb_ref,
                  q_ref, k_ref, v_ref, z_ref, ab_ref,
                  cwq_ref, cwk_ref, cwv_ref, cpq_ref, cpk_ref, cpv_ref,
                  s0_ref, on_ref,
                  y_ref, sout_ref,
                  xbuf, s_scr, *, t, c, hp, valid_len):
    n = pl.program_id(2)
    dh = DN_HEAD

    @pl.when(n == 0)
    def _():
        s_scr[...] = s0_ref[...]
        for j in range(hp):
            lanes = slice(j * dh, (j + 1) * dh)
            xbuf[3 * j + 0, 5:8, :] = cpq_ref[:, lanes]
            xbuf[3 * j + 1, 5:8, :] = cpk_ref[:, lanes]
            xbuf[3 * j + 2, 5:8, :] = cpv_ref[:, lanes]

    _delta_heads(alog_ref, dtb_ref, q_ref, k_ref, v_ref, z_ref, ab_ref, cwq_ref, cwk_ref, cwv_ref,
                 on_ref, y_ref, xbuf, s_scr, t=t, c=c, hp=hp, valid_len=valid_len)

    @pl.when(n == pl.num_programs(2) - 1)
    def _():
        sout_ref[...] = s_scr[...]


def _delta_heads(alog_ref, dtb_ref, q_ref, k_ref, v_ref, z_ref, ab_ref, cwq_ref, cwk_ref, cwv_ref,
                 on_ref, y_ref, xbuf, s_scr, *, t, c, hp, valid_len):
    hg = pl.program_id(1)
    dh = DN_HEAD
    heads = list(range(hp))

    row = lax.broadcasted_iota(I32, (t, t), 0)
    col = lax.broadcasted_iota(I32, (t, t), 1)
    shift = c.bit_length() - 1
    same = (row >> shift) == (col >> shift)
    eye = row == col
    incl = same & (col <= row)
    strict = same & (col < row)
    incl_t = same & (row <= col)
    blk8 = (row >> 3) == (col >> 3)
    off_masks = []
    bs = 8
    while bs < c:
        sh_b = bs.bit_length() - 1
        inner = (row >> sh_b) == (col >> sh_b)
        outer = (row >> (sh_b + 1)) == (col >> (sh_b + 1))
        off_masks.append(outer & jnp.logical_not(inner))
        bs *= 2
    eye_f = jnp.where(eye, 1.0, 0.0)
    if valid_len < t:
        lane_valid = lax.broadcasted_iota(I32, (1, t), 1) < valid_len
        sub_valid = lax.broadcasted_iota(I32, (t, 1), 0) < valid_len

    def to_col(r):
        return jnp.sum(jnp.where(eye, r, 0.0), axis=1, keepdims=True)

    def conv(idx, x, w_ref, lanes):
        xbuf[idx, 8:8 + t, :] = x
        y = xbuf[idx, pl.ds(5, t), :] * w_ref[0:1, lanes]
        for jj in range(1, CONV_K):
            y = y + xbuf[idx, pl.ds(5 + jj, t), :] * w_ref[jj:jj + 1, lanes]
        xbuf[idx, 5:8, :] = x[t - 3:t, :]
        return _silu(y)

    lanes_of = [slice(j * dh, (j + 1) * dh) for j in range(hp)]

    def per_head(f, *lists):
        return [f(*vals) for vals in zip(*lists)]

    def l2n(a):
        return a * lax.rsqrt(jnp.sum(a * a, axis=-1, keepdims=True) + EPS)

    qc = [l2n(conv(3 * j + 0, q_ref[:, lanes_of[j]], cwq_ref, lanes_of[j])) * (dh ** -0.5) for j in heads]
    kc = [l2n(conv(3 * j + 1, k_ref[:, lanes_of[j]], cwk_ref, lanes_of[j])) for j in heads]
    v = [conv(3 * j + 2, v_ref[:, lanes_of[j]], cwv_ref, lanes_of[j]) for j in heads]

    def gates(j):
        h = hg * hp + j
        a_row = ab_ref[pl.ds(h, 1), :]
        b_row = ab_ref[pl.ds(DN_HEADS + h, 1), :]
        xa = a_row + dtb_ref[h]
        softplus = jnp.maximum(xa, 0.0) + jnp.log(1.0 + jnp.exp(-jnp.abs(xa)))
        a_coef = jnp.exp(jnp.zeros((1, 1), F32) + alog_ref[h])
        g_row = -a_coef * softplus
        beta_row = 1.0 / (1.0 + jnp.exp(-b_row))
        if valid_len < t:
            g_row = jnp.where(lane_valid, g_row, 0.0)
            beta_row = jnp.where(lane_valid, beta_row, 0.0)
        return g_row, beta_row

    g_row, beta_row = zip(*[gates(j) for j in heads])
    if valid_len < t:
        kc = per_head(lambda a: jnp.where(sub_valid, a, 0.0), kc)
        v = per_head(lambda a: jnp.where(sub_valid, a, 0.0), v)

    g_col = per_head(to_col, g_row)
    beta_col = per_head(to_col, beta_row)
    gc_col = per_head(lambda r: jnp.sum(jnp.where(incl, r, 0.0), axis=1, keepdims=True), g_row)
    glast_col = per_head(lambda r: jnp.sum(jnp.where(same, r, 0.0), axis=1, keepdims=True), g_row)
    gc_row = per_head(lambda cl: jnp.sum(jnp.where(incl_t, cl, 0.0), axis=0, keepdims=True), g_col)
    kb = per_head(lambda a, b: a * b, kc, beta_col)
    kc16 = per_head(lambda a: a.astype(BF16), kc)

    def key_products(kb_, kc16_, qc_, gc, gr):
        decay = jnp.where(incl, jnp.exp(jnp.where(incl, gc - gr, 0.0)), 0.0)
        a_mat = jnp.where(strict, _dot_nt(kb_.astype(BF16), kc16_) * decay, 0.0)
        attn16 = (_dot_nt(qc_.astype(BF16), kc16_) * decay).astype(BF16)
        n0f = jnp.where(blk8, -a_mat, 0.0)
        return attn16, _b16(n0f), eye_f + n0f, [_b16(jnp.where(om, a_mat, 0.0)) for om in off_masks]

    attn, n0, x, a_offs = zip(*per_head(key_products, kb, kc16, qc, gc_col, gc_row))
    eg = per_head(jnp.exp, gc_col)
    rhs = per_head(lambda vv, bc, kk, e: _b16(jnp.concatenate([vv * bc, kk * e], axis=1)),
                   v, beta_col, kb, eg)
    n2 = per_head(lambda a: _b16(_dot(a, a)), n0)
    n4 = per_head(lambda a: _b16(_dot(a, a)), n2)
    x = per_head(lambda xx, nn: xx + _dot(_b16(xx), nn), x, n2)
    x = per_head(lambda xx, nn: xx + _dot(_b16(xx), nn), x, n4)
    for lvl in range(len(off_masks)):
        xs = per_head(_b16, x)
        xa_off = per_head(lambda s_, offs: _b16(_dot(s_, offs[lvl])), xs, a_offs)
        x = per_head(lambda xx, xo, s_: xx - _dot(xo, s_), x, xa_off, xs)
    sol16 = per_head(lambda xx, r: _b16(_dot(_b16(xx), r)), x, rhs)
    auw = per_head(_dot, attn, sol16)
    qw = per_head(lambda a, e, m_: (a * e - m_[:, dh:]).astype(BF16), qc, eg, auw)
    kg16 = per_head(lambda a, gl_, gc: (a * jnp.exp(gl_ - gc)).astype(BF16), kc, glast_col, gc_col)

    s = [s_scr[j] for j in heads]
    outs = [[] for _ in heads]
    for i in range(t // c):
        rows = slice(i * c, (i + 1) * c)
        s16 = per_head(lambda a: a.astype(BF16), s)
        kuw = per_head(lambda a, b: _dot_tn(a[rows], b[rows]), kg16, sol16)
        o_i = per_head(lambda a, b16, m_: _dot(a[rows], b16) + m_[rows, :dh], qw, s16, auw)
        for pos in range(len(heads)):
            outs[pos].append(o_i[pos])
        gl = per_head(lambda a: jnp.exp(a[i * c:i * c + 1, :]), glast_col)
        s = per_head(lambda g_, s_, m_, b16: g_ * s_ + m_[:, :dh] - _dot(m_[:, dh:].astype(BF16), b16),
                     gl, s, kuw, s16)
    for pos, j in enumerate(heads):
        s_scr[j] = s[pos]
        o = outs[pos][0] if len(outs[pos]) == 1 else jnp.concatenate(outs[pos], axis=0)
        on = o * lax.rsqrt(jnp.mean(o * o, axis=-1, keepdims=True) + EPS) * on_ref[...]
        y_ref[:, lanes_of[j]] = (on * _silu(z_ref[:, lanes_of[j]])).astype(y_ref.dtype)


def _delta(p3, abt, conv_w, conv_prev, s0, state_layer, a_log, dt_bias, onorm, *, t, valid_len):
    b, l, _ = p3.shape
    assert l % t == 0
    c = min(DN_CHUNK, t)
    hp = DELTA_HEADS_PER_STEP
    nh = DN_HEADS
    ng = nh // hp
    w = 128 * hp
    cb_q = OFF["qkv"] // w
    cb_z = OFF["zb"] // w
    assert OFF["qkv"] % w == 0 and OFF["zb"] % w == 0 and W_B % w == 0

    def pspec(cb0):
        return pl.BlockSpec((None, t, w), lambda i, h, n, a, d: (i, n, cb0 + h))

    def cwspec(sidx):
        return pl.BlockSpec((CONV_K, w), lambda i, h, n, a, d: (0, sidx * ng + h))

    def cpspec(sidx):
        return pl.BlockSpec((None, None, CONV_K - 1, w),
                            lambda i, h, n, a, d: (state_layer, i, 0, sidx * ng + h))

    grid_spec = pltpu.PrefetchScalarGridSpec(
        num_scalar_prefetch=2,
        grid=(b, ng, l // t),
        in_specs=[pspec(cb_q), pspec(cb_q + ng), pspec(cb_q + 2 * ng), pspec(cb_z),
                  pl.BlockSpec((None, 2 * nh, t), lambda i, h, n, a, d: (i, 0, n)),
                  cwspec(0), cwspec(1), cwspec(2), cpspec(0), cpspec(1), cpspec(2),
                  pl.BlockSpec((None, None, hp, DN_HEAD, DN_HEAD),
                               lambda i, h, n, a, d: (state_layer, i, h, 0, 0)),
                  pl.BlockSpec((1, DN_HEAD), lambda i, h, n, a, d: (0, 0))],
        out_specs=[pl.BlockSpec((None, t, w), lambda i, h, n, a, d: (i, n, h)),
                   pl.BlockSpec((None, hp, DN_HEAD, DN_HEAD), lambda i, h, n, a, d: (i, h, 0, 0))],
        scratch_shapes=[pltpu.VMEM((3 * hp, t + 8, 128), F32), pltpu.VMEM((hp, DN_HEAD, DN_HEAD), F32)],
    )
    return pl.pallas_call(
        functools.partial(_delta_kernel, t=t, c=c, hp=hp, valid_len=valid_len),
        grid_spec=grid_spec,
        out_shape=[jax.ShapeDtypeStruct((b, l, W_B), BF16),
                   jax.ShapeDtypeStruct((b, nh, DN_HEAD, DN_HEAD), F32)],
        compiler_params=_cp("delta", 3, fuse_inputs=15),
        name="delta",
    )(a_log, dt_bias, p3, p3, p3, p3, abt, conv_w, conv_w, conv_w,
      conv_prev, conv_prev, conv_prev, s0, onorm.reshape(1, DN_HEAD))


def _ordered_bits_to_float(u):
    key = u ^ jnp.int32(INT_MIN)
    bits = jnp.where(key < 0, key ^ jnp.int32(0x7FFFFFFF), key)
    return pltpu.bitcast(bits, F32)


def _topk_threshold(count_ge, shape, topk):
    def bit_body(i, u):
        bit = jnp.left_shift(jnp.int32(1), 31 - i)
        cand_u = u | bit
        cnt = count_ge(_ordered_bits_to_float(cand_u))
        return jnp.where(cnt >= float(topk), cand_u, u)

    u = lax.fori_loop(0, 32, bit_body, jnp.zeros(shape, I32))
    return _ordered_bits_to_float(u)


def _softmax_update(m_ref, l_ref, acc_ref, idx, s, v_tiles):
    m_old = m_ref[idx]
    m_new = jnp.maximum(m_old, jnp.max(s, axis=1, keepdims=True))
    m_safe = jnp.where(m_new == NEG_INF, 0.0, m_new)
    alpha = jnp.exp(m_old - m_safe)
    p = jnp.exp(s - m_safe)
    l_ref[idx] = alpha * l_ref[idx] + jnp.sum(p, axis=1, keepdims=True)
    acc = alpha * acc_ref[idx]
    p16 = p.astype(BF16)
    k0 = 0
    for vt in v_tiles:
        acc = acc + _dot(p16[:, k0:k0 + vt.shape[0]], vt)
        k0 += vt.shape[0]
    acc_ref[idx] = acc
    m_ref[idx] = m_new


def _dsa_prompt_kernel(qi_ref, tq_ref, tall_ref, qc_ref, k_ref, v_ref, z_ref, y_ref,
                       sc_scr, m_scr, l_scr, acc_scr, *, l, topk, kt):
    qb = pl.program_id(1)
    nq = Q_BLOCK
    n_tiles = (qb * nq + nq + kt - 1) // kt
    n_pairs = C_HEADS // 2
    w_rows = tq_ref[...].T[TAIL_WI:TAIL_WI + IDX_HEADS, :] * ((IDX_HEADS ** -0.5) * (IDX_DIM ** -0.5))
    rq = []
    for j in range(IDX_HEADS // 2):
        a = qi_ref[:, j * 128:(j + 1) * 128]
        rq.append(jnp.concatenate([a, pltpu.roll(a, IDX_DIM, 1)], axis=0).astype(BF16))
    kpos = lax.broadcasted_iota(I32, (kt, nq), 0)
    qpos = qb * nq + lax.broadcasted_iota(I32, (kt, nq), 1)
    lane = lax.broadcasted_iota(I32, (kt, 128), 1)

    def score_tile(ti, carry):
        k0 = pl.multiple_of(ti * kt, kt)
        ki = jnp.where(lane < IDX_DIM, tall_ref[pl.ds(k0, kt), :], 0.0).astype(BF16)
        score = jnp.zeros((kt, nq), F32)
        for j in range(IDX_HEADS // 2):
            lg = _dot_nt(ki, rq[j])
            score = score + w_rows[2 * j:2 * j + 1, :] * jnp.maximum(lg[:, :nq], 0.0)
            score = score + w_rows[2 * j + 1:2 * j + 2, :] * jnp.maximum(lg[:, nq:], 0.0)
        sc_scr[pl.ds(k0, kt), :] = jnp.where(kpos + k0 <= qpos, score, NEG_INF)
        return carry

    lax.fori_loop(0, n_tiles, score_tile, 0)

    ct = 256
    n_ct = (qb * nq + nq + ct - 1) // ct
    n_acc = 4

    def count(cand, strict):
        cand8 = jnp.broadcast_to(cand, (8, nq))

        def tile_body(ti, accs):
            k0 = pl.multiple_of(ti * ct, ct)
            tile = sc_scr[pl.ds(k0, ct), :]
            accs = list(accs)
            for r in range(ct // 8):
                blk = tile[8 * r:8 * r + 8, :]
                hit = (blk > cand8) if strict else (blk >= cand8)
                accs[r % n_acc] = accs[r % n_acc] + jnp.where(hit, 1.0, 0.0)
            return tuple(accs)

        accs = lax.fori_loop(0, n_ct, tile_body, tuple(jnp.zeros((8, nq), F32) for _ in range(n_acc)))
        acc = (accs[0] + accs[1]) + (accs[2] + accs[3])
        return jnp.sum(acc, axis=0, keepdims=True)

    thr = _topk_threshold(lambda cand: count(cand, False), (1, nq), topk)
    keep_all = (qb * nq + lax.broadcasted_iota(I32, (1, nq), 1)) < topk
    need = float(topk) - count(thr, True)
    tri16 = jnp.where(lax.broadcasted_iota(I32, (128, 128), 1) <= lax.broadcasted_iota(I32, (128, 128), 0),
                      1.0, 0.0).astype(BF16)

    def select_tile(ti, seen):
        k0 = pl.multiple_of(ti * kt, kt)
        sc = sc_scr[pl.ds(k0, kt), :]
        eq = sc == thr
        eq_f = jnp.where(eq, 1.0, 0.0)
        eq16 = eq_f.astype(BF16)
        prefs = [_dot(tri16, eq16[r:r + 128]) for r in range(0, kt, 128)]
        ranks = []
        for pref in prefs:
            ranks.append(seen + pref)
            seen = seen + pref[127:128, :]
        tie_ok = (jnp.concatenate(ranks, axis=0) - eq_f) < need
        chosen = jnp.where(sc > thr, 0.0, jnp.where(eq, jnp.where(tie_ok, 0.0, NEG_INF), NEG_INF))
        sc_scr[pl.ds(k0, kt), :] = jnp.where(keep_all, jnp.where(sc > NEG_INF, 0.0, NEG_INF), chosen)
        return seen

    lax.fori_loop(0, n_tiles, select_tile, jnp.zeros((1, nq), F32))

    m_scr[...] = jnp.full(m_scr.shape, NEG_INF, F32)
    l_scr[...] = jnp.zeros(l_scr.shape, F32)
    acc_scr[...] = jnp.zeros(acc_scr.shape, F32)
    scale = (C_HEAD ** -0.5) * 1.4426950408889634
    qp = []
    for pr in range(n_pairs):
        c0 = slice(2 * pr * C_HEAD, (2 * pr + 1) * C_HEAD)
        c1 = slice((2 * pr + 1) * C_HEAD, (2 * pr + 2) * C_HEAD)
        qp.append(jnp.concatenate([qc_ref[:, c0] * scale, qc_ref[:, c1] * scale], axis=0).astype(BF16))
    pairs_per_kv = n_pairs // C_KV_HEADS

    def attend_tile(ti, carry):
        k0 = pl.multiple_of(ti * kt, kt)
        bias = sc_scr[pl.ds(k0, kt), :]
        k16 = [k_ref[pl.ds(k0, kt), hk * C_HEAD:(hk + 1) * C_HEAD].astype(BF16) for hk in range(C_KV_HEADS)]
        vt16 = [v_ref[pl.ds(k0, kt), hk * C_HEAD:(hk + 1) * C_HEAD].T.astype(BF16) for hk in range(C_KV_HEADS)]
        bias2 = jnp.concatenate([bias, bias], axis=1)
        pr_all = range(n_pairs)
        s = [_dot_nt(k16[pr // pairs_per_kv], qp[pr]) + bias2 for pr in pr_all]
        m_old = [m_scr[pr] for pr in pr_all]
        m_new = [jnp.maximum(m_old[pr], jnp.max(s[pr], axis=0, keepdims=True)) for pr in pr_all]
        m_safe = [jnp.where(m_new[pr] == NEG_INF, 0.0, m_new[pr]) for pr in pr_all]
        alpha = [jnp.exp2(m_old[pr] - m_safe[pr]) for pr in pr_all]
        p = [jnp.exp2(s[pr] - m_safe[pr]) for pr in pr_all]
        for pr in pr_all:
            l_scr[pr] = alpha[pr] * l_scr[pr] + jnp.sum(p[pr], axis=0, keepdims=True)
            acc_scr[pr] = alpha[pr] * acc_scr[pr] + _dot(vt16[pr // pairs_per_kv], p[pr].astype(BF16))
            m_scr[pr] = m_new[pr]
        return carry

    lax.fori_loop(0, n_tiles, attend_tile, 0)
    for pr in range(n_pairs):
        o_t = acc_scr[pr] / l_scr[pr]
        for e in range(2):
            cols = slice((2 * pr + e) * C_HEAD, (2 * pr + e + 1) * C_HEAD)
            o = o_t[:, e * nq:(e + 1) * nq].T
            y_ref[:, cols] = (o * _silu(z_ref[:, cols])).astype(y_ref.dtype)


def _dsa_prompt(p3):
    b, l, _ = p3.shape
    assert l % Q_BLOCK == 0
    topk = min(TOPK_MAX, l // 4)
    nq = Q_BLOCK
    kt = DSA_KEY_TILE if l % DSA_KEY_TILE == 0 else 256
    assert l % kt == 0
    n_qb = l // nq
    return pl.pallas_call(
        functools.partial(_dsa_prompt_kernel, l=l, topk=topk, kt=kt),
        grid=(b, n_qb),
        in_specs=[pl.BlockSpec((None, nq, W_C), lambda i, j: (i, j, OFF["qi"] // W_C)),
                  pl.BlockSpec((None, nq, 128), lambda i, j: (i, j, OFF_TAIL // 128)),
                  pl.BlockSpec((None, l, 128), lambda i, j: (i, 0, OFF_TAIL // 128)),
                  pl.BlockSpec((None, nq, W_C), lambda i, j: (i, j, OFF["qc"] // W_C)),
                  pl.BlockSpec((None, l, C_KV), lambda i, j: (i, 0, OFF["kc"] // C_KV)),
                  pl.BlockSpec((None, l, C_KV), lambda i, j: (i, 0, OFF["vc"] // C_KV)),
                  pl.BlockSpec((None, nq, W_C), lambda i, j: (i, j, OFF["zc"] // W_C))],
        out_specs=pl.BlockSpec((None, nq, W_C), lambda i, j: (i, j, 0)),
        out_shape=jax.ShapeDtypeStruct((b, l, W_C), BF16),
        scratch_shapes=[pltpu.VMEM((l, nq), F32),
                        pltpu.VMEM((C_HEADS // 2, 1, 2 * nq), F32), pltpu.VMEM((C_HEADS // 2, 1, 2 * nq), F32),
                        pltpu.VMEM((C_HEADS // 2, C_HEAD, 2 * nq), F32)],
        compiler_params=_cp("dsa_prompt", 2),
        name="dsa_prompt",
    )(p3, p3, p3, p3, p3, p3, p3)


def _idx_scores(logits, wcol, t):
    r = jnp.maximum(logits, 0.0) * wcol
    return jnp.sum(r.reshape(t, IDX_HEADS, logits.shape[1]), axis=1)


def _dsa_s_score_kernel(pt_ref, q_ref, w_ref, kn_ref, *refs, pg, t):
    pages = refs[:pg]
    out_ref, new_ref = refs[pg:]
    q16 = q_ref[...].astype(BF16)
    wcol = w_ref[...] * ((IDX_HEADS ** -0.5) * (IDX_DIM ** -0.5))
    keys16 = jnp.concatenate([pages[i][...] for i in range(pg)], axis=1).astype(BF16)
    out_ref[...] = _idx_scores(_dot(q16, keys16), wcol, t)

    @pl.when(pl.program_id(1) == 0)
    def _():
        new_ref[...] = _idx_scores(_dot_nt(q16, kn_ref[...].astype(BF16)), wcol, t)


def _dsa_s_select_kernel(sc_ref, scn_ref, tq_ref, bias_ref, *, nk, topk):
    rows = sc_ref.shape[0]
    tq = tq_ref[...]
    new_ok = lax.broadcasted_iota(I32, (rows, 128), 1) <= tq
    sc_new = jnp.where(new_ok, scn_ref[...], NEG_INF)
    sc_past = sc_ref[...]

    def count_ge(cand):
        c1 = jnp.sum(jnp.where(sc_past >= cand, 1.0, 0.0), axis=1, keepdims=True)
        c2 = jnp.sum(jnp.where(sc_new >= cand, 1.0, 0.0), axis=1, keepdims=True)
        return c1 + c2

    thr = _topk_threshold(count_ge, (rows, 1), topk)
    keep_all = (nk + 1 + tq[:, 0:1]) <= topk
    n_gt = (jnp.sum(jnp.where(sc_past > thr, 1.0, 0.0), axis=1, keepdims=True)
            + jnp.sum(jnp.where(sc_new > thr, 1.0, 0.0), axis=1, keepdims=True))
    need = float(topk) - n_gt
    tri16 = jnp.where(lax.broadcasted_iota(I32, (128, 128), 0) <= lax.broadcasted_iota(I32, (128, 128), 1),
                      1.0, 0.0).astype(BF16)

    def bias_blocks(scs, seen):
        eqs = [sc == thr for sc in scs]
        eq_fs = [jnp.where(eq, 1.0, 0.0) for eq in eqs]
        prefs = [_dot(eq_f.astype(BF16), tri16) for eq_f in eq_fs]
        out = []
        for sc, eq, eq_f, pref in zip(scs, eqs, eq_fs, prefs):
            tie_ok = (seen + pref - eq_f) < need
            chosen = jnp.where(sc > thr, 0.0, jnp.where(eq, jnp.where(tie_ok, 0.0, NEG_INF), NEG_INF))
            out.append(jnp.where(keep_all, jnp.where(sc > NEG_INF, 0.0, NEG_INF), chosen))
            seen = seen + pref[:, 127:128]
        return out, seen

    n_blk = nk // 128
    grp = 8 if n_blk % 8 == 0 else 1

    def past_group(gi, seen):
        lanes = [pl.ds(pl.multiple_of((gi * grp + i) * 128, 128), 128) for i in range(grp)]
        out, seen = bias_blocks([sc_ref[:, ln] for ln in lanes], seen)
        for ln, bias in zip(lanes, out):
            bias_ref[:, ln] = bias
        return seen

    seen = lax.fori_loop(0, n_blk // grp, past_group, jnp.zeros((rows, 1), F32))
    (bias_new,), _ = bias_blocks([sc_new], seen)
    bias_ref[:, nk:] = bias_new


def _dsa_s_attend_kernel(pt_ref, q_ref, bias_ref, biasn_ref, kn_ref, vn_ref, z_ref, *refs, pg):
    kpages = refs[:pg]
    vpages = refs[pg:2 * pg]
    o_ref = refs[2 * pg]
    m_scr, l_scr, acc_scr = refs[2 * pg + 1:]
    g = pl.program_id(1)
    scale = C_HEAD ** -0.5
    page = kpages[0].shape[0] // C_KV_HEADS

    @pl.when(g == 0)
    def _():
        m_scr[...] = jnp.full(m_scr.shape, NEG_INF, F32)
        l_scr[...] = jnp.zeros(l_scr.shape, F32)
        acc_scr[...] = jnp.zeros(acc_scr.shape, F32)
        for hk in range(C_KV_HEADS):
            cols = slice(hk * C_HEAD, (hk + 1) * C_HEAD)
            q16 = (q_ref[hk] * scale).astype(BF16)
            s = _dot_nt(q16, kn_ref[:, cols].astype(BF16)) + biasn_ref[...]
            _softmax_update(m_scr, l_scr, acc_scr, hk, s, [vn_ref[:, cols].astype(BF16)])

    def two_pages(refs_, i, rows):
        return jnp.concatenate([refs_[i][rows, :], refs_[i + 1][rows, :]], axis=0).astype(BF16)

    for hk in range(C_KV_HEADS):
        q16 = (q_ref[hk] * scale).astype(BF16)
        rows = pl.ds(hk, page, stride=C_KV_HEADS)
        tiles = [_dot_nt(q16, two_pages(kpages, i, rows)) for i in range(0, pg, 2)]
        s = jnp.concatenate(tiles, axis=1) + bias_ref[...]
        _softmax_update(m_scr, l_scr, acc_scr, hk, s, [two_pages(vpages, i, rows) for i in range(0, pg, 2)])

    @pl.when(g == pl.num_programs(1) - 1)
    def _():
        for hk in range(C_KV_HEADS):
            o_ref[hk] = (acc_scr[hk] / l_scr[hk]) * _silu(z_ref[hk])


def _dsa_sample(p3s, cache_k, cache_v, cache_kidx, page_table, layer):
    b, t, _ = p3s.shape
    n_pages = page_table.shape[1]
    page = cache_k.shape[2]
    assert page == 128
    past = n_pages * page
    topk = min(TOPK_MAX, (past + t) // 4)
    pg = 32 if n_pages % 32 == 0 else (16 if n_pages % 16 == 0 else n_pages)
    assert pg % 2 == 0
    ng = n_pages // pg
    rows = t * C_GROUPS

    qi = p3s[:, :, OFF["qi"]:OFF["qi"] + IDX_HEADS * IDX_DIM].reshape(b, t * IDX_HEADS, IDX_DIM)
    wi = p3s[:, :, OFF_TAIL + TAIL_WI:OFF_TAIL + TAIL_WI + IDX_HEADS].reshape(b, t * IDX_HEADS, 1)
    ki_new = jnp.pad(p3s[:, :, OFF_TAIL:OFF_TAIL + IDX_DIM], ((0, 0), (0, 128 - t), (0, 0)))
    k_new = jnp.pad(p3s[:, :, OFF["kc"]:OFF["kc"] + C_KV], ((0, 0), (0, 128 - t), (0, 0)))
    v_new = jnp.pad(p3s[:, :, OFF["vc"]:OFF["vc"] + C_KV], ((0, 0), (0, 128 - t), (0, 0)))

    def heads_major(a):
        a = a.reshape(b, t, C_KV_HEADS, C_GROUPS, C_HEAD)
        return jnp.transpose(a, (0, 2, 1, 3, 4)).reshape(b, C_KV_HEADS, rows, C_HEAD)

    qh = heads_major(p3s[:, :, OFF["qc"]:OFF["qc"] + W_C])
    zh = heads_major(p3s[:, :, OFF["zc"]:OFF["zc"] + W_C])

    kidx_t = jnp.swapaxes(cache_kidx, 2, 3)
    score_spec = pltpu.PrefetchScalarGridSpec(
        num_scalar_prefetch=1,
        grid=(b, ng),
        in_specs=[pl.BlockSpec((None, t * IDX_HEADS, IDX_DIM), lambda i, g, pt: (i, 0, 0)),
                  pl.BlockSpec((None, t * IDX_HEADS, 1), lambda i, g, pt: (i, 0, 0)),
                  pl.BlockSpec((None, 128, IDX_DIM), lambda i, g, pt: (i, 0, 0))]
                 + [pl.BlockSpec((None, None, IDX_DIM, page),
                                 lambda i, g, pt, j=j: (layer, pt[i, g * pg + j], 0, 0)) for j in range(pg)],
        out_specs=[pl.BlockSpec((None, t, pg * page), lambda i, g, pt: (i, 0, g)),
                   pl.BlockSpec((None, t, 128), lambda i, g, pt: (i, 0, 0))],
    )
    scores, scores_new = pl.pallas_call(
        functools.partial(_dsa_s_score_kernel, pg=pg, t=t),
        grid_spec=score_spec,
        out_shape=[jax.ShapeDtypeStruct((b, t, past), F32), jax.ShapeDtypeStruct((b, t, 128), F32)],
        compiler_params=_cp("dsa_s_score", 2, fuse_inputs=4 + pg),
        name="dsa_s_score",
    )(page_table, qi, wi, ki_new, *([kidx_t] * pg))

    tq = jnp.broadcast_to(jnp.tile(jnp.arange(t, dtype=I32), b)[:, None], (b * t, 128))
    bias = pl.pallas_call(
        functools.partial(_dsa_s_select_kernel, nk=past, topk=topk),
        grid=(1,),
        in_specs=[pl.BlockSpec((b * t, past), lambda i: (0, 0)),
                  pl.BlockSpec((b * t, 128), lambda i: (0, 0)),
                  pl.BlockSpec((b * t, 128), lambda i: (0, 0))],
        out_specs=pl.BlockSpec((b * t, past + 128), lambda i: (0, 0)),
        out_shape=jax.ShapeDtypeStruct((b * t, past + 128), F32),
        compiler_params=_cp("dsa_s_select", 1, fuse_inputs=3),
        name="dsa_s_select",
    )(scores.reshape(b * t, past), scores_new.reshape(b * t, 128), tq).reshape(b, t, past + 128)

    bias_rows = jnp.repeat(bias, C_GROUPS, axis=1)
    bias_past = bias_rows[:, :, :past]
    bias_new = bias_rows[:, :, past:]

    ck = cache_k.reshape(cache_k.shape[0], cache_k.shape[1], page * C_KV_HEADS, C_HEAD)
    cv = cache_v.reshape(cache_v.shape[0], cache_v.shape[1], page * C_KV_HEADS, C_HEAD)
    pspec = [pl.BlockSpec((None, None, page * C_KV_HEADS, C_HEAD),
                          lambda i, g, pt, j=j: (layer, pt[i, g * pg + j], 0, 0)) for j in range(pg)]
    attend_spec = pltpu.PrefetchScalarGridSpec(
        num_scalar_prefetch=1,
        grid=(b, ng),
        in_specs=[pl.BlockSpec((None, C_KV_HEADS, rows, C_HEAD), lambda i, g, pt: (i, 0, 0, 0)),
                  pl.BlockSpec((None, rows, pg * page), lambda i, g, pt: (i, 0, g)),
                  pl.BlockSpec((None, rows, 128), lambda i, g, pt: (i, 0, 0)),
                  pl.BlockSpec((None, 128, C_KV), lambda i, g, pt: (i, 0, 0)),
                  pl.BlockSpec((None, 128, C_KV), lambda i, g, pt: (i, 0, 0)),
                  pl.BlockSpec((None, C_KV_HEADS, rows, C_HEAD), lambda i, g, pt: (i, 0, 0, 0))]
                 + pspec + pspec,
        out_specs=pl.BlockSpec((None, C_KV_HEADS, rows, C_HEAD), lambda i, g, pt: (i, 0, 0, 0)),
        scratch_shapes=[pltpu.VMEM((C_KV_HEADS, rows, 1), F32), pltpu.VMEM((C_KV_HEADS, rows, 1), F32),
                        pltpu.VMEM((C_KV_HEADS, rows, C_HEAD), F32)],
    )
    oh = pl.pallas_call(
        functools.partial(_dsa_s_attend_kernel, pg=pg),
        grid_spec=attend_spec,
        out_shape=jax.ShapeDtypeStruct((b, C_KV_HEADS, rows, C_HEAD), F32),
        compiler_params=_cp("dsa_s_attend", 2, fuse_inputs=7 + 2 * pg),
        name="dsa_s_attend",
    )(page_table, qh, bias_past, bias_new, k_new, v_new, zh, *([ck] * pg), *([cv] * pg))
    y = jnp.transpose(oh.reshape(b, C_KV_HEADS, t, C_GROUPS, C_HEAD), (0, 2, 1, 3, 4)).reshape(b, t, W_C)
    return y.astype(BF16)


def _outproj_kernel(ya_ref, yb_ref, yc_ref, w_ref, x_ref, gate_ref, o_ref, mix_scr):
    @pl.when(pl.program_id(1) == 0)
    def _():
        mix_scr[:, 0:W_A] = ya_ref[...]
        mix_scr[:, W_A:W_A + W_B] = yb_ref[...]
        mix_scr[:, W_A + W_B:] = yc_ref[...]

    o_ref[...] = x_ref[...] + gate_ref[...] * _dot(mix_scr[...], w_ref[...])


def _outproj_norm_kernel(ya_ref, yb_ref, yc_ref, w_ref, x_ref, gate_ref, g_ref, o_ref, mix_scr, ssq_scr):
    j = pl.program_id(1)
    tn = x_ref.shape[1]

    @pl.when(j == 0)
    def _():
        mix_scr[:, 0:W_A] = ya_ref[...]
        mix_scr[:, W_A:W_A + W_B] = yb_ref[...]
        mix_scr[:, W_A + W_B:] = yc_ref[...]
        ssq_scr[...] = jnp.zeros(ssq_scr.shape, F32)

    x_new = x_ref[...] + gate_ref[...] * _dot(mix_scr[...], w_ref[...])
    o_ref[:, pl.ds(pl.multiple_of(j * tn, tn), tn)] = x_new
    ssq_scr[...] += jnp.sum(x_new * x_new, axis=-1, keepdims=True)

    @pl.when(j == pl.num_programs(1) - 1)
    def _():
        tm, d = o_ref.shape
        rc = 64 if tm % 64 == 0 else tm

        def body(c, carry):
            r = pl.ds(pl.multiple_of(c * rc, rc), rc)
            o_ref[r, :] = o_ref[r, :] * lax.rsqrt(ssq_scr[r, :] / d + EPS) * g_ref[...]
            return carry

        lax.fori_loop(0, tm // rc, body, 0)


def _outproj(ya, yb, yc, w_bf16, layer, x2, gate, *, rows_per_batch, mod_row0, g_final=None):
    m, d = x2.shape
    tn = 1024
    if gate.ndim == 3:
        tm = 512
        gate_spec = pl.BlockSpec((None, 1, tn),
                                 lambda i, j: ((i * tm) // rows_per_batch + mod_row0, 0, 2 * (d // tn) + j))
    else:
        tm = m
        gate_spec = pl.BlockSpec((tm, tn), lambda i, j: (i, j))
    assert m % tm == 0
    in_specs = [pl.BlockSpec((tm, W_A), lambda i, j: (i, 0)),
                pl.BlockSpec((tm, W_B), lambda i, j: (i, 0)),
                pl.BlockSpec((tm, W_C), lambda i, j: (i, 0)),
                pl.BlockSpec((None, d, tn), lambda i, j: (layer, 0, j)),
                pl.BlockSpec((tm, tn), lambda i, j: (i, j)),
                gate_spec]
    operands = [ya, yb, yc, w_bf16, x2, gate]
    scratch = [pltpu.VMEM((tm, d), BF16)]
    if g_final is None:
        body, name = _outproj_kernel, "outproj"
        out_spec = pl.BlockSpec((tm, tn), lambda i, j: (i, j))
    else:
        body, name = _outproj_norm_kernel, "outproj_norm"
        in_specs.append(pl.BlockSpec((1, d), lambda i, j: (0, 0)))
        operands.append(g_final.reshape(1, d))
        scratch.append(pltpu.VMEM((tm, 1), F32))
        out_spec = pl.BlockSpec((tm, d), lambda i, j: (i, 0))
    return pl.pallas_call(
        body,
        grid=(m // tm, d // tn),
        in_specs=in_specs,
        out_specs=out_spec,
        out_shape=jax.ShapeDtypeStruct((m, d), F32),
        scratch_shapes=scratch,
        compiler_params=_cp(name, 2, fuse_inputs=len(operands)),
        name=name,
    )(*operands)


def _kv_rows_kernel(*refs, depth, tm):
    k_refs, v_refs = refs[:depth], refs[depth:2 * depth]
    pk_ref, pv_ref = refs[2 * depth:]
    for lyr in range(depth):
        for h in range(C_KV_HEADS):
            cols = slice(h * C_HEAD, (h + 1) * C_HEAD)
            rows = pl.ds(h, tm, stride=C_KV_HEADS)
            pk_ref.at[lyr][rows, :] = k_refs[lyr][:, cols]
            pv_ref.at[lyr][rows, :] = v_refs[lyr][:, cols]


def _kv_rows(p3_layers):
    depth = len(p3_layers)
    b, l, _ = p3_layers[0].shape
    tm = 512 if l % 512 == 0 else l
    kspec = pl.BlockSpec((None, tm, C_KV), lambda i, j: (i, j, OFF["kc"] // C_KV))
    vspec = pl.BlockSpec((None, tm, C_KV), lambda i, j: (i, j, OFF["vc"] // C_KV))
    ospec = pl.BlockSpec((depth, None, tm * C_KV_HEADS, C_HEAD), lambda i, j: (0, i, j, 0))
    oshape = jax.ShapeDtypeStruct((depth, b, l * C_KV_HEADS, C_HEAD), F32)
    pk, pv = pl.pallas_call(
        functools.partial(_kv_rows_kernel, depth=depth, tm=tm),
        grid=(b, l // tm),
        in_specs=[kspec] * depth + [vspec] * depth,
        out_specs=[ospec, ospec],
        out_shape=[oshape, oshape],
        compiler_params=_cp("kv_rows", 2),
        name="kv_rows",
    )(*p3_layers, *p3_layers)
    shape = (depth, b, l, C_KV_HEADS, C_HEAD)
    return pk.reshape(shape), pv.reshape(shape)


def _w_in_tile_table():
    n_t = NP // 128
    table, n_plain = [], None
    for dt in range(n_t - 1):
        c = dt * 128
        seg = [n for n in _DST_ORDER if OFF[n] <= c < OFF[n] + _SRC_OFF[n][1]][0]
        src = _SRC_OFF[seg][0] + (c - OFF[seg])
        if src % 128 == 0:
            assert n_plain is None
        else:
            assert src % 128 == 32
            if n_plain is None:
                n_plain = dt
        table.append(src // 128)
    t_kw, t_ab = _SRC_OFF["ki"][0] // 128, _SRC_OFF["a"][0] // 128
    assert _SRC_OFF["ki"][0] % 128 == 32 and _SRC_OFF["wi"][0] == t_kw * 128 + TAIL_WI
    assert _SRC_OFF["a"][0] % 128 == 0 and _SRC_OFF["b"][0] == _SRC_OFF["a"][0] + DN_HEADS
    assert TAIL_A == IDX_DIM and TAIL_B == TAIL_A + DN_HEADS and TAIL_WI == TAIL_B + DN_HEADS
    tile_a = table + [t_kw]
    rows_b = [t_ab * 4] * n_plain + [(s + 1) * 4 for s in table[n_plain:]] + [t_ab * 4]
    return tile_a, rows_b, n_plain


def _wprep_kernel(ta_ref, tb_ref, a_ref, b_ref, xs_ref, g_ref, sc_ref, sh_ref, o_ref, ps_ref, h_scr,
                  *, n_plain, n_t):
    dt = pl.program_id(0)
    dt_o = o_ref.dtype

    @pl.when(dt == 0)
    def _():
        x = xs_ref[...]
        y = x * lax.rsqrt(jnp.mean(x * x, axis=-1, keepdims=True) + EPS)
        h_scr[...] = ((y * g_ref[...]) * (1.0 + sc_ref[...]) + sh_ref[...]).astype(BF16)

    @pl.when(dt < n_plain)
    def _():
        o_ref[...] = a_ref[...].astype(dt_o)

    @pl.when((dt >= n_plain) & (dt < n_t - 1))
    def _():
        o_ref[0:96, :] = a_ref[32:128, :].astype(dt_o)
        o_ref[96:128, :] = b_ref[...].astype(dt_o)

    @pl.when(dt == n_t - 1)
    def _():
        o_ref[0:TAIL_A, :] = a_ref[32:32 + IDX_DIM, :].astype(dt_o)
        o_ref[TAIL_A:TAIL_WI, :] = b_ref[...].astype(dt_o)
        o_ref[TAIL_WI:TAIL_WI + IDX_HEADS, :] = a_ref[TAIL_WI:TAIL_WI + IDX_HEADS, :].astype(dt_o)
        o_ref[TAIL_WI + IDX_HEADS:, :] = jnp.zeros((128 - TAIL_WI - IDX_HEADS, o_ref.shape[1]), dt_o)

    ps_ref[...] = _dot_nt(h_scr[...], o_ref[...])


def _permute_w_in(w_in, layer, xs, g, sc, sh):
    _, d, _ = w_in.shape
    ms = xs.shape[0]
    w_t = jnp.swapaxes(w_in, 1, 2)
    tile_a, rows_b, n_plain = _w_in_tile_table()
    n_t = NP // 128
    const = lambda t, ta, tb: (0, 0)
    grid_spec = pltpu.PrefetchScalarGridSpec(
        num_scalar_prefetch=2,
        grid=(n_t,),
        in_specs=[pl.BlockSpec((None, 128, d), lambda t, ta, tb: (layer, ta[t], 0)),
                  pl.BlockSpec((None, 32, d), lambda t, ta, tb: (layer, tb[t], 0)),
                  pl.BlockSpec((ms, d), const), pl.BlockSpec((1, d), const),
                  pl.BlockSpec((ms, d), const), pl.BlockSpec((ms, d), const)],
        out_specs=[pl.BlockSpec((128, d), lambda t, ta, tb: (t, 0)),
                   pl.BlockSpec((ms, 128), lambda t, ta, tb: (0, t))],
        scratch_shapes=[pltpu.VMEM((ms, d), BF16)],
    )
    return pl.pallas_call(
        functools.partial(_wprep_kernel, n_plain=n_plain, n_t=n_t),
        grid_spec=grid_spec,
        out_shape=[jax.ShapeDtypeStruct((NP, d), BF16), jax.ShapeDtypeStruct((ms, NP), F32)],
        compiler_params=_cp("w_in_layout", 1, fuse_inputs=8),
        name="w_in_layout",
    )(jnp.asarray(tile_a, I32), jnp.asarray(rows_b, I32), w_t, w_t, xs, g, sc, sh)


def _delta_t(l):
    for t in (128, 64):
        if l % t == 0:
            return t
    raise ValueError("sequence length must be a multiple of 64")


def kernel(x_prompt, x_sample, cache_k, cache_v, cache_kidx, state_dn, state_conv, page_table,
           c_prompt, c_sample, w_ada, b_ada, g_norm, w_in, a_vnorm, a_ws, a_bs, dn_conv_w,
           dn_a_log, dn_dt_bias, dn_onorm, w_out, g_final):
    bp, lp, d = x_prompt.shape
    bs, ls, _ = x_sample.shape
    depth = w_ada.shape[0]
    assert d == D_MODEL and w_in.shape[2] == D_IN
    assert CONV_K - 1 <= ls <= DN_CHUNK

    n_c = bp + bs
    c_rows = jnp.concatenate([c_prompt, c_sample], axis=0)
    r_pad = (-n_c) % 8
    if r_pad:
        c_rows = jnp.pad(c_rows, ((0, r_pad), (0, 0)))
    m_all = _ada(c_rows, w_ada, b_ada)

    xp = x_prompt.reshape(bp * lp, d)
    xs = x_sample.reshape(bs * ls, d)
    ls_pad = 8 * ((ls + 7) // 8)
    outs = {k: [] for k in ("p3", "pki", "pdn", "pconv", "sk", "sv", "ski", "sdn", "sconv", "samlp")}
    zeros_conv = jnp.zeros((1, bp, CONV_K - 1, DN_CONV_DIM), F32)
    zeros_state = jnp.zeros((1, bp, DN_HEADS, DN_HEAD, DN_HEAD), F32)

    w_out_all = w_out.astype(BF16)

    def g_last(layer):
        return g_final if layer == depth - 1 else None

    for l in range(depth):
        g_l = g_norm[l].reshape(1, d)
        m_l = m_all[l]
        m3 = m_l.reshape(m_l.shape[0], 1, 3 * d)
        ms = jnp.repeat(m_l[bp:bp + bs], ls, axis=0)

        w_in_l, ps = _permute_w_in(w_in, l, xs, g_l, ms[:, d:2 * d], ms[:, 0:d])

        pp = _inproj(xp, g_l, m3, w_in_l, rows_per_batch=lp)
        p3 = pp.reshape(bp, lp, NP)
        (ya,) = _mixa(p3, a_vnorm[l], a_ws[l], a_bs[l], emit_va=False)
        abt = jnp.transpose(p3[:, :, OFF["a"]:OFF["a"] + 2 * DN_HEADS], (0, 2, 1))
        yb, s_p = _delta(p3, abt, dn_conv_w[l], zeros_conv, zeros_state, 0, dn_a_log[l], dn_dt_bias[l],
                         dn_onorm[l], t=_delta_t(lp), valid_len=_delta_t(lp))
        yc = _dsa_prompt(p3)
        xp = _outproj(ya.reshape(bp * lp, W_A), yb.reshape(bp * lp, W_B), yc.reshape(bp * lp, W_C),
                      w_out_all, l, xp, m3, rows_per_batch=lp, mod_row0=0, g_final=g_last(l))
        outs["p3"].append(p3)
        outs["pki"].append(p3[:, :, OFF_TAIL:OFF_TAIL + IDX_DIM])
        outs["pdn"].append(s_p)
        outs["pconv"].append(p3[:, lp - (CONV_K - 1):, OFF["qkv"]:OFF["qkv"] + DN_CONV_DIM])

        p3s = ps.reshape(bs, ls, NP)
        ya_s, va_s = _mixa(p3s, a_vnorm[l], a_ws[l], a_bs[l], emit_va=True)
        p3s_pad = jnp.pad(p3s, ((0, 0), (0, ls_pad - ls), (0, 0)))
        abt_s = jnp.transpose(p3s_pad[:, :, OFF["a"]:OFF["a"] + 2 * DN_HEADS], (0, 2, 1))
        yb_s, s_s = _delta(p3s_pad, abt_s, dn_conv_w[l], state_conv, state_dn, l, dn_a_log[l],
                           dn_dt_bias[l], dn_onorm[l], t=ls_pad, valid_len=ls)
        yc_s = _dsa_sample(p3s, cache_k, cache_v, cache_kidx, page_table, l)
        xs = _outproj(ya_s.reshape(bs * ls, W_A), yb_s[:, :ls].reshape(bs * ls, W_B),
                      yc_s.reshape(bs * ls, W_C), w_out_all, l, xs, ms[:, 2 * d:3 * d],
                      rows_per_batch=ls, mod_row0=bp, g_final=g_last(l))
        outs["sk"].append(p3s[:, :, OFF["kc"]:OFF["kc"] + C_KV].reshape(bs, ls, C_KV_HEADS, C_HEAD))
        outs["sv"].append(p3s[:, :, OFF["vc"]:OFF["vc"] + C_KV].reshape(bs, ls, C_KV_HEADS, C_HEAD))
        outs["ski"].append(p3s[:, :, OFF_TAIL:OFF_TAIL + IDX_DIM])
        outs["sdn"].append(s_s)
        outs["sconv"].append(p3s[:, ls - (CONV_K - 1):, OFF["qkv"]:OFF["qkv"] + DN_CONV_DIM])
        outs["samlp"].append(va_s)

    y_prompt = xp.reshape(bp, lp, d)
    y_sample = xs.reshape(bs, ls, d)
    st = jnp.stack
    p_k, p_v = _kv_rows(outs["p3"])
    return (y_prompt, y_sample, p_k, p_v, st(outs["pki"]), st(outs["pdn"]),
            st(outs["pconv"]), st(outs["sk"]), st(outs["sv"]), st(outs["ski"]), st(outs["sdn"]),
            st(outs["sconv"]), st(outs["samlp"]))
```

```python
import functools

import jax
import jax.numpy as jnp
from jax import lax
from jax.experimental import pallas as pl
from jax.experimental.pallas import tpu as pltpu

F32 = jnp.float32
BF16 = jnp.bfloat16
I32 = jnp.int32
EPS = 1e-6
INT_MIN = -(2 ** 31)
NEG_INF = float("-inf")

D_MODEL = 4096
W_A = D_MODEL // 4
A_GROUP = 128
A_HEADS = W_A // A_GROUP
A_CHUNK = 128
W_B = D_MODEL // 2
DN_HEAD = 128
DN_HEADS = W_B // DN_HEAD
CONV_K = 4
DN_CONV_DIM = 3 * W_B
DN_CHUNK = 64
W_C = D_MODEL - W_A - W_B
C_HEAD = 128
C_HEADS = W_C // C_HEAD
C_KV_HEADS = 2
C_GROUPS = C_HEADS // C_KV_HEADS
C_KV = C_KV_HEADS * C_HEAD
IDX_HEADS = 16
IDX_DIM = 64
TOPK_MAX = 256
Q_BLOCK = 512

_SRC_SPLITS = (W_A, W_A, W_A, DN_CONV_DIM, W_B, DN_HEADS, DN_HEADS,
               W_C, C_KV, C_KV, W_C, IDX_HEADS * IDX_DIM, IDX_DIM, IDX_HEADS)
_SRC_NAMES = ("u", "v", "za", "qkv", "zb", "a", "b", "qc", "kc", "vc", "zc", "qi", "ki", "wi")
_SRC_OFF = {}
_o = 0
for _n, _w in zip(_SRC_NAMES, _SRC_SPLITS):
    _SRC_OFF[_n] = (_o, _w)
    _o += _w
D_IN = _o

_DST_ORDER = ("u", "v", "za", "qkv", "zb", "qc", "zc", "qi", "kc", "vc", "ki", "a", "b", "wi")
OFF = {}
_o = 0
for _n in _DST_ORDER:
    OFF[_n] = _o
    _o += _SRC_OFF[_n][1]
NP = ((_o + 127) // 128) * 128
OFF_TAIL = OFF["ki"]
TAIL_A = OFF["a"] - OFF_TAIL
TAIL_B = OFF["b"] - OFF_TAIL
TAIL_WI = OFF["wi"] - OFF_TAIL

MXU_COLUMNS = 256
INPROJ_TN = 6 * MXU_COLUMNS
DELTA_HEADS_PER_STEP = 8
DSA_KEY_TILE = 512

VMEM_CAPACITY_MIB = 64
VMEM_LIMIT_MIB = {
    "ada": 40, "w_in_layout": 32, "inproj": 56, "mixa": 32, "delta": 40, "dsa_prompt": 48,
    "dsa_s_score": 32, "dsa_s_select": 40, "dsa_s_attend": 40, "outproj": 48, "outproj_norm": 60,
    "kv_rows": 32,
}
assert max(VMEM_LIMIT_MIB.values()) < VMEM_CAPACITY_MIB


def _cp(name, n_axes, fuse_inputs=0):
    fusion = [True] * fuse_inputs if fuse_inputs else None
    return pltpu.CompilerParams(dimension_semantics=("arbitrary",) * n_axes,
                                vmem_limit_bytes=VMEM_LIMIT_MIB[name] * 1024 * 1024,
                                allow_input_fusion=fusion)


def _silu(x):
    return (0.5 * x) * (1.0 + jnp.tanh(0.5 * x))


def _gelu(x):
    return 0.5 * x * (1.0 + jnp.tanh(0.7978845608028654 * (x + 0.044715 * (x * x * x))))


def _dot(a, b):
    return jnp.dot(a, b, preferred_element_type=F32)


def _dot_nt(a, b):
    return lax.dot_general(a, b, (((1,), (1,)), ((), ())), preferred_element_type=F32)


def _dot_tn(a, b):
    return lax.dot_general(a, b, (((0,), (0,)), ((), ())), preferred_element_type=F32)


def _b16(a):
    return a.astype(BF16)


def _ada_kernel(c_ref, w_ref, b_ref, o_ref):
    s = _silu(c_ref[...]).astype(BF16)
    o_ref[...] = _dot(s, w_ref[...].astype(BF16)) + b_ref[...]


def _ada(c_rows, w_ada, b_ada):
    depth, d, n = w_ada.shape
    r = c_rows.shape[0]
    tn = 512
    return pl.pallas_call(
        _ada_kernel,
        grid=(depth, n // tn),
        in_specs=[pl.BlockSpec((r, d), lambda l, j: (0, 0)),
                  pl.BlockSpec((None, d, tn), lambda l, j: (l, 0, j)),
                  pl.BlockSpec((None, 1, tn), lambda l, j: (l, 0, j))],
        out_specs=pl.BlockSpec((None, r, tn), lambda l, j: (l, 0, j)),
        out_shape=jax.ShapeDtypeStruct((depth, r, n), F32),
        compiler_params=_cp("ada", 2, fuse_inputs=3),
        name="ada",
    )(c_rows, w_ada, b_ada.reshape(depth, 1, n))


def _inproj_kernel(x_ref, g_ref, sc_ref, sh_ref, w_ref, o_ref, h_scr, *, rc):
    @pl.when(pl.program_id(1) == 0)
    def _():
        tm = x_ref.shape[0]

        def body(c, carry):
            r = pl.ds(pl.multiple_of(c * rc, rc), rc)
            x = x_ref[r, :]
            y = x * lax.rsqrt(jnp.mean(x * x, axis=-1, keepdims=True) + EPS)
            h_scr[r, :] = ((y * g_ref[...]) * (1.0 + sc_ref[...]) + sh_ref[...]).astype(BF16)
            return carry

        lax.fori_loop(0, tm // rc, body, 0)

    o_ref[...] = _dot_nt(h_scr[...], w_ref[...])


def _inproj(x2, g, mod, w_bf16, *, rows_per_batch):
    m, d = x2.shape
    n = w_bf16.shape[0]
    tn = INPROJ_TN
    tm = 512
    assert rows_per_batch % tm == 0 and m % tm == 0
    return pl.pallas_call(
        functools.partial(_inproj_kernel, rc=64),
        grid=(m // tm, pl.cdiv(n, tn)),
        in_specs=[pl.BlockSpec((tm, d), lambda i, j: (i, 0)),
                  pl.BlockSpec((1, d), lambda i, j: (0, 0)),
                  pl.BlockSpec((None, 1, d), lambda i, j: ((i * tm) // rows_per_batch, 0, 1)),
                  pl.BlockSpec((None, 1, d), lambda i, j: ((i * tm) // rows_per_batch, 0, 0)),
                  pl.BlockSpec((tn, d), lambda i, j: (j, 0))],
        out_specs=pl.BlockSpec((tm, tn), lambda i, j: (i, j)),
        out_shape=jax.ShapeDtypeStruct((m, n), F32),
        scratch_shapes=[pltpu.VMEM((tm, d), BF16)],
        compiler_params=_cp("inproj", 2),
        name="inproj",
    )(x2, g, mod, mod, w_bf16)


def _mixa_kernel(u_ref, v_ref, z_ref, vn_ref, ws_ref, bst_ref, y_ref, *rest, c, emit_va):
    u = _gelu(u_ref[...])
    v = _gelu(v_ref[...])
    mu = jnp.mean(v, axis=-1, keepdims=True)
    dv = v - mu
    va = dv * lax.rsqrt(jnp.mean(dv * dv, axis=-1, keepdims=True) + EPS) * vn_ref[...]
    if emit_va:
        rest[0][...] = va
    z = _silu(z_ref[...])
    row = lax.broadcasted_iota(I32, (c, c), 0)
    col = lax.broadcasted_iota(I32, (c, c), 1)
    tril = col <= row
    for h in range(A_HEADS):
        cols = slice(h * A_GROUP, (h + 1) * A_GROUP)
        wm = jnp.where(tril, ws_ref[h], 0.0)
        wm16 = wm.astype(BF16)
        for ci in range(u.shape[0] // c):
            rows = slice(ci * c, (ci + 1) * c)
            vh = va[rows, cols]
            if c >= 128:
                mixed = _dot(wm16, vh.astype(BF16))
            else:
                mixed = wm[:, 0:1] * vh[0:1, :]
                for s in range(1, c):
                    mixed = mixed + wm[:, s:s + 1] * vh[s:s + 1, :]
            mixed = mixed + bst_ref[:, h:h + 1]
            y_ref[rows, cols] = (u[rows, cols] * mixed * z[rows, cols]).astype(y_ref.dtype)


def _mixa(p3, a_vnorm, a_ws, a_bs, *, emit_va):
    b, l, _ = p3.shape
    c = min(A_CHUNK, l)
    r = 2 * c if l % (2 * c) == 0 else c
    n = l // r
    ws = a_ws[:, :c, :c]
    bst = a_bs[:, :c].T
    wblk = W_A
    outs = [jax.ShapeDtypeStruct((b, l, W_A), BF16)]
    out_specs = [pl.BlockSpec((None, r, W_A), lambda i, j: (i, j, 0))]
    if emit_va:
        outs.append(jax.ShapeDtypeStruct((b, l, W_A), F32))
        out_specs.append(pl.BlockSpec((None, r, W_A), lambda i, j: (i, j, 0)))
    res = pl.pallas_call(
        functools.partial(_mixa_kernel, c=c, emit_va=emit_va),
        grid=(b, n),
        in_specs=[pl.BlockSpec((None, r, wblk), lambda i, j: (i, j, OFF["u"] // wblk)),
                  pl.BlockSpec((None, r, wblk), lambda i, j: (i, j, OFF["v"] // wblk)),
                  pl.BlockSpec((None, r, wblk), lambda i, j: (i, j, OFF["za"] // wblk)),
                  pl.BlockSpec((1, W_A), lambda i, j: (0, 0)),
                  pl.BlockSpec((A_HEADS, c, c), lambda i, j: (0, 0, 0)),
                  pl.BlockSpec((c, A_HEADS), lambda i, j: (0, 0))],
        out_specs=out_specs,
        out_shape=outs,
        compiler_params=_cp("mixa", 2, fuse_inputs=6),
        name="mixa",
    )(p3, p3, p3, a_vnorm.reshape(1, W_A), ws, bst)
    return res


def _delta_kernel(alog_ref, dtb_ref,
                  q_ref, k_ref, v_ref, z_ref, ab_ref,
                  cwq_ref, cwk_ref, cwv_ref, cpq_ref, cpk_ref, cpv_ref,
                  s0_ref, on_ref,
                  y_ref, sout_ref,
                  xbuf, s_scr, *, t, c, hp, valid_len):
    n = pl.program_id(2)
    dh = DN_HEAD

    @pl.when(n == 0)
    def _():
        s_scr[...] = s0_ref[...]
        for j in range(hp):
            lanes = slice(j * dh, (j + 1) * dh)
            xbuf[3 * j + 0, 5:8, :] = cpq_ref[:, lanes]
            xbuf[3 * j + 1, 5:8, :] = cpk_ref[:, lanes]
            xbuf[3 * j + 2, 5:8, :] = cpv_ref[:, lanes]

    _delta_heads(alog_ref, dtb_ref, q_ref, k_ref, v_ref, z_ref, ab_ref, cwq_ref, cwk_ref, cwv_ref,
                 on_ref, y_ref, xbuf, s_scr, t=t, c=c, hp=hp, valid_len=valid_len)

    @pl.when(n == pl.num_programs(2) - 1)
    def _():
        sout_ref[...] = s_scr[...]


def _delta_heads(alog_ref, dtb_ref, q_ref, k_ref, v_ref, z_ref, ab_ref, cwq_ref, cwk_ref, cwv_ref,
                 on_ref, y_ref, xbuf, s_scr, *, t, c, hp, valid_len):
    hg = pl.program_id(1)
    dh = DN_HEAD
    heads = list(range(hp))

    row = lax.broadcasted_iota(I32, (t, t), 0)
    col = lax.broadcasted_iota(I32, (t, t), 1)
    shift = c.bit_length() - 1
    same = (row >> shift) == (col >> shift)
    eye = row == col
    incl = same & (col <= row)
    strict = same & (col < row)
    incl_t = same & (row <= col)
    blk8 = (row >> 3) == (col >> 3)
    off_masks = []
    bs = 8
    while bs < c:
        sh_b = bs.bit_length() - 1
        inner = (row >> sh_b) == (col >> sh_b)
        outer = (row >> (sh_b + 1)) == (col >> (sh_b + 1))
        off_masks.append(outer & jnp.logical_not(inner))
        bs *= 2
    eye_f = jnp.where(eye, 1.0, 0.0)
    if valid_len < t:
        lane_valid = lax.broadcasted_iota(I32, (1, t), 1) < valid_len
        sub_valid = lax.broadcasted_iota(I32, (t, 1), 0) < valid_len

    def to_col(r):
        return jnp.sum(jnp.where(eye, r, 0.0), axis=1, keepdims=True)

    def conv(idx, x, w_ref, lanes):
        xbuf[idx, 8:8 + t, :] = x
        y = xbuf[idx, pl.ds(5, t), :] * w_ref[0:1, lanes]
        for jj in range(1, CONV_K):
            y = y + xbuf[idx, pl.ds(5 + jj, t), :] * w_ref[jj:jj + 1, lanes]
        xbuf[idx, 5:8, :] = x[t - 3:t, :]
        return _silu(y)

    lanes_of = [slice(j * dh, (j + 1) * dh) for j in range(hp)]

    def per_head(f, *lists):
        return [f(*vals) for vals in zip(*lists)]

    def l2n(a):
        return a * lax.rsqrt(jnp.sum(a * a, axis=-1, keepdims=True) + EPS)

    qc = [l2n(conv(3 * j + 0, q_ref[:, lanes_of[j]], cwq_ref, lanes_of[j])) * (dh ** -0.5) for j in heads]
    kc = [l2n(conv(3 * j + 1, k_ref[:, lanes_of[j]], cwk_ref, lanes_of[j])) for j in heads]
    v = [conv(3 * j + 2, v_ref[:, lanes_of[j]], cwv_ref, lanes_of[j]) for j in heads]

    def gates(j):
        h = hg * hp + j
        a_row = ab_ref[pl.ds(h, 1), :]
        b_row = ab_ref[pl.ds(DN_HEADS + h, 1), :]
        xa = a_row + dtb_ref[h]
        softplus = jnp.maximum(xa, 0.0) + jnp.log(1.0 + jnp.exp(-jnp.abs(xa)))
        a_coef = jnp.exp(jnp.zeros((1, 1), F32) + alog_ref[h])
        g_row = -a_coef * softplus
        beta_row = 1.0 / (1.0 + jnp.exp(-b_row))
        if valid_len < t:
            g_row = jnp.where(lane_valid, g_row, 0.0)
            beta_row = jnp.where(lane_valid, beta_row, 0.0)
        return g_row, beta_row

    g_row, beta_row = zip(*[gates(j) for j in heads])
    if valid_len < t:
        kc = per_head(lambda a: jnp.where(sub_valid, a, 0.0), kc)
        v = per_head(lambda a: jnp.where(sub_valid, a, 0.0), v)

    g_col = per_head(to_col, g_row)
    beta_col = per_head(to_col, beta_row)
    gc_col = per_head(lambda r: jnp.sum(jnp.where(incl, r, 0.0), axis=1, keepdims=True), g_row)
    glast_col = per_head(lambda r: jnp.sum(jnp.where(same, r, 0.0), axis=1, keepdims=True), g_row)
    gc_row = per_head(lambda cl: jnp.sum(jnp.where(incl_t, cl, 0.0), axis=0, keepdims=True), g_col)
    kb = per_head(lambda a, b: a * b, kc, beta_col)
    kc16 = per_head(lambda a: a.astype(BF16), kc)

    def key_products(kb_, kc16_, qc_, gc, gr):
        decay = jnp.where(incl, jnp.exp(jnp.where(incl, gc - gr, 0.0)), 0.0)
        a_mat = jnp.where(strict, _dot_nt(kb_.astype(BF16), kc16_) * decay, 0.0)
        attn16 = (_dot_nt(qc_.astype(BF16), kc16_) * decay).astype(BF16)
        n0f = jnp.where(blk8, -a_mat, 0.0)
        return attn16, _b16(n0f), eye_f + n0f, [_b16(jnp.where(om, a_mat, 0.0)) for om in off_masks]

    attn, n0, x, a_offs = zip(*per_head(key_products, kb, kc16, qc, gc_col, gc_row))
    eg = per_head(jnp.exp, gc_col)
    rhs = per_head(lambda vv, bc, kk, e: _b16(jnp.concatenate([vv * bc, kk * e], axis=1)),
                   v, beta_col, kb, eg)
    n2 = per_head(lambda a: _b16(_dot(a, a)), n0)
    n4 = per_head(lambda a: _b16(_dot(a, a)), n2)
    x = per_head(lambda xx, nn: xx + _dot(_b16(xx), nn), x, n2)
    x = per_head(lambda xx, nn: xx + _dot(_b16(xx), nn), x, n4)
    for lvl in range(len(off_masks)):
        xs = per_head(_b16, x)
        xa_off = per_head(lambda s_, offs: _b16(_dot(s_, offs[lvl])), xs, a_offs)
        x = per_head(lambda xx, xo, s_: xx - _dot(xo, s_), x, xa_off, xs)
    sol16 = per_head(lambda xx, r: _b16(_dot(_b16(xx), r)), x, rhs)
    auw = per_head(_dot, attn, sol16)
    qw = per_head(lambda a, e, m_: (a * e - m_[:, dh:]).astype(BF16), qc, eg, auw)
    kg16 = per_head(lambda a, gl_, gc: (a * jnp.exp(gl_ - gc)).astype(BF16), kc, glast_col, gc_col)

    s = [s_scr[j] for j in heads]
    outs = [[] for _ in heads]
    for i in range(t // c):
        rows = slice(i * c, (i + 1) * c)
        s16 = per_head(lambda a: a.astype(BF16), s)
        kuw = per_head(lambda a, b: _dot_tn(a[rows], b[rows]), kg16, sol16)
        o_i = per_head(lambda a, b16, m_: _dot(a[rows], b16) + m_[rows, :dh], qw, s16, auw)
        for pos in range(len(heads)):
            outs[pos].append(o_i[pos])
        gl = per_head(lambda a: jnp.exp(a[i * c:i * c + 1, :]), glast_col)
        s = per_head(lambda g_, s_, m_, b16: g_ * s_ + m_[:, :dh] - _dot(m_[:, dh:].astype(BF16), b16),
                     gl, s, kuw, s16)
    for pos, j in enumerate(heads):
        s_scr[j] = s[pos]
        o = outs[pos][0] if len(outs[pos]) == 1 else jnp.concatenate(outs[pos], axis=0)
        on = o * lax.rsqrt(jnp.mean(o * o, axis=-1, keepdims=True) + EPS) * on_ref[...]
        y_ref[:, lanes_of[j]] = (on * _silu(z_ref[:, lanes_of[j]])).astype(y_ref.dtype)


def _delta(p3, abt, conv_w, conv_prev, s0, state_layer, a_log, dt_bias, onorm, *, t, valid_len):
    b, l, _ = p3.shape
    assert l % t == 0
    c = min(DN_CHUNK, t)
    hp = DELTA_HEADS_PER_STEP
    nh = DN_HEADS
    ng = nh // hp
    w = 128 * hp
    cb_q = OFF["qkv"] // w
    cb_z = OFF["zb"] // w
    assert OFF["qkv"] % w == 0 and OFF["zb"] % w == 0 and W_B % w == 0

    def pspec(cb0):
        return pl.BlockSpec((None, t, w), lambda i, h, n, a, d: (i, n, cb0 + h))

    def cwspec(sidx):
        return pl.BlockSpec((CONV_K, w), lambda i, h, n, a, d: (0, sidx * ng + h))

    def cpspec(sidx):
        return pl.BlockSpec((None, None, CONV_K - 1, w),
                            lambda i, h, n, a, d: (state_layer, i, 0, sidx * ng + h))

    grid_spec = pltpu.PrefetchScalarGridSpec(
        num_scalar_prefetch=2,
        grid=(b, ng, l // t),
        in_specs=[pspec(cb_q), pspec(cb_q + ng), pspec(cb_q + 2 * ng), pspec(cb_z),
                  pl.BlockSpec((None, 2 * nh, t), lambda i, h, n, a, d: (i, 0, n)),
                  cwspec(0), cwspec(1), cwspec(2), cpspec(0), cpspec(1), cpspec(2),
                  pl.BlockSpec((None, None, hp, DN_HEAD, DN_HEAD),
                               lambda i, h, n, a, d: (state_layer, i, h, 0, 0)),
                  pl.BlockSpec((1, DN_HEAD), lambda i, h, n, a, d: (0, 0))],
        out_specs=[pl.BlockSpec((None, t, w), lambda i, h, n, a, d: (i, n, h)),
                   pl.BlockSpec((None, hp, DN_HEAD, DN_HEAD), lambda i, h, n, a, d: (i, h, 0, 0))],
        scratch_shapes=[pltpu.VMEM((3 * hp, t + 8, 128), F32), pltpu.VMEM((hp, DN_HEAD, DN_HEAD), F32)],
    )
    return pl.pallas_call(
        functools.partial(_delta_kernel, t=t, c=c, hp=hp, valid_len=valid_len),
        grid_spec=grid_spec,
        out_shape=[jax.ShapeDtypeStruct((b, l, W_B), BF16),
                   jax.ShapeDtypeStruct((b, nh, DN_HEAD, DN_HEAD), F32)],
        compiler_params=_cp("delta", 3, fuse_inputs=15),
        name="delta",
    )(a_log, dt_bias, p3, p3, p3, p3, abt, conv_w, conv_w, conv_w,
      conv_prev, conv_prev, conv_prev, s0, onorm.reshape(1, DN_HEAD))


def _ordered_bits_to_float(u):
    key = u ^ jnp.int32(INT_MIN)
    bits = jnp.where(key < 0, key ^ jnp.int32(0x7FFFFFFF), key)
    return pltpu.bitcast(bits, F32)


def _topk_threshold(count_ge, shape, topk):
    def bit_body(i, u):
        bit = jnp.left_shift(jnp.int32(1), 31 - i)
        cand_u = u | bit
        cnt = count_ge(_ordered_bits_to_float(cand_u))
        return jnp.where(cnt >= float(topk), cand_u, u)

    u = lax.fori_loop(0, 32, bit_body, jnp.zeros(shape, I32))
    return _ordered_bits_to_float(u)


def _softmax_update(m_ref, l_ref, acc_ref, idx, s, v_tiles):
    m_old = m_ref[idx]
    m_new = jnp.maximum(m_old, jnp.max(s, axis=1, keepdims=True))
    m_safe = jnp.where(m_new == NEG_INF, 0.0, m_new)
    alpha = jnp.exp(m_old - m_safe)
    p = jnp.exp(s - m_safe)
    l_ref[idx] = alpha * l_ref[idx] + jnp.sum(p, axis=1, keepdims=True)
    acc = alpha * acc_ref[idx]
    p16 = p.astype(BF16)
    k0 = 0
    for vt in v_tiles:
        acc = acc + _dot(p16[:, k0:k0 + vt.shape[0]], vt)
        k0 += vt.shape[0]
    acc_ref[idx] = acc
    m_ref[idx] = m_new


def _dsa_prompt_kernel(qi_ref, tq_ref, tall_ref, qc_ref, k_ref, v_ref, z_ref, y_ref,
                       sc_scr, m_scr, l_scr, acc_scr, *, l, topk, kt):
    qb = pl.program_id(1)
    nq = Q_BLOCK
    n_tiles = (qb * nq + nq + kt - 1) // kt
    n_pairs = C_HEADS // 2
    w_rows = tq_ref[...].T[TAIL_WI:TAIL_WI + IDX_HEADS, :] * ((IDX_HEADS ** -0.5) * (IDX_DIM ** -0.5))
    rq = []
    for j in range(IDX_HEADS // 2):
        a = qi_ref[:, j * 128:(j + 1) * 128]
        rq.append(jnp.concatenate([a, pltpu.roll(a, IDX_DIM, 1)], axis=0).astype(BF16))
    kpos = lax.broadcasted_iota(I32, (kt, nq), 0)
    qpos = qb * nq + lax.broadcasted_iota(I32, (kt, nq), 1)
    lane = lax.broadcasted_iota(I32, (kt, 128), 1)

    def score_tile(ti, carry):
        k0 = pl.multiple_of(ti * kt, kt)
        ki = jnp.where(lane < IDX_DIM, tall_ref[pl.ds(k0, kt), :], 0.0).astype(BF16)
        score = jnp.zeros((kt, nq), F32)
        for j in range(IDX_HEADS // 2):
            lg = _dot_nt(ki, rq[j])
            score = score + w_rows[2 * j:2 * j + 1, :] * jnp.maximum(lg[:, :nq], 0.0)
            score = score + w_rows[2 * j + 1:2 * j + 2, :] * jnp.maximum(lg[:, nq:], 0.0)
        sc_scr[pl.ds(k0, kt), :] = jnp.where(kpos + k0 <= qpos, score, NEG_INF)
        return carry

    lax.fori_loop(0, n_tiles, score_tile, 0)

    ct = 256
    n_ct = (qb * nq + nq + ct - 1) // ct
    n_acc = 4

    def count(cand, strict):
        cand8 = jnp.broadcast_to(cand, (8, nq))

        def tile_body(ti, accs):
            k0 = pl.multiple_of(ti * ct, ct)
            tile = sc_scr[pl.ds(k0, ct), :]
            accs = list(accs)
            for r in range(ct // 8):
                blk = tile[8 * r:8 * r + 8, :]
                hit = (blk > cand8) if strict else (blk >= cand8)
                accs[r % n_acc] = accs[r % n_acc] + jnp.where(hit, 1.0, 0.0)
            return tuple(accs)

        accs = lax.fori_loop(0, n_ct, tile_body, tuple(jnp.zeros((8, nq), F32) for _ in range(n_acc)))
        acc = (accs[0] + accs[1]) + (accs[2] + accs[3])
        return jnp.sum(acc, axis=0, keepdims=True)

    thr = _topk_threshold(lambda cand: count(cand, False), (1, nq), topk)
    keep_all = (qb * nq + lax.broadcasted_iota(I32, (1, nq), 1)) < topk
    need = float(topk) - count(thr, True)
    tri16 = jnp.where(lax.broadcasted_iota(I32, (128, 128), 1) <= lax.broadcasted_iota(I32, (128, 128), 0),
                      1.0, 0.0).astype(BF16)

    def select_tile(ti, seen):
        k0 = pl.multiple_of(ti * kt, kt)
        sc = sc_scr[pl.ds(k0, kt), :]
        eq = sc == thr
        eq_f = jnp.where(eq, 1.0, 0.0)
        eq16 = eq_f.astype(BF16)
        prefs = [_dot(tri16, eq16[r:r + 128]) for r in range(0, kt, 128)]
        ranks = []
        for pref in prefs:
            ranks.append(seen + pref)
            seen = seen + pref[127:128, :]
        tie_ok = (jnp.concatenate(ranks, axis=0) - eq_f) < need
        chosen = jnp.where(sc > thr, 0.0, jnp.where(eq, jnp.where(tie_ok, 0.0, NEG_INF), NEG_INF))
        sc_scr[pl.ds(k0, kt), :] = jnp.where(keep_all, jnp.where(sc > NEG_INF, 0.0, NEG_INF), chosen)
        return seen

    lax.fori_loop(0, n_tiles, select_tile, jnp.zeros((1, nq), F32))

    m_scr[...] = jnp.full(m_scr.shape, NEG_INF, F32)
    l_scr[...] = jnp.zeros(l_scr.shape, F32)
    acc_scr[...] = jnp.zeros(acc_scr.shape, F32)
    scale = (C_HEAD ** -0.5) * 1.4426950408889634
    qp = []
    for pr in range(n_pairs):
        c0 = slice(2 * pr * C_HEAD, (2 * pr + 1) * C_HEAD)
        c1 = slice((2 * pr + 1) * C_HEAD, (2 * pr + 2) * C_HEAD)
        qp.append(jnp.concatenate([qc_ref[:, c0] * scale, qc_ref[:, c1] * scale], axis=0).astype(BF16))
    pairs_per_kv = n_pairs // C_KV_HEADS

    def attend_tile(ti, carry):
        k0 = pl.multiple_of(ti * kt, kt)
        bias = sc_scr[pl.ds(k0, kt), :]
        k16 = [k_ref[pl.ds(k0, kt), hk * C_HEAD:(hk + 1) * C_HEAD].astype(BF16) for hk in range(C_KV_HEADS)]
        vt16 = [v_ref[pl.ds(k0, kt), hk * C_HEAD:(hk + 1) * C_HEAD].T.astype(BF16) for hk in range(C_KV_HEADS)]
        bias2 = jnp.concatenate([bias, bias], axis=1)
        pr_all = range(n_pairs)
        s = [_dot_nt(k16[pr // pairs_per_kv], qp[pr]) + bias2 for pr in pr_all]
        m_old = [m_scr[pr] for pr in pr_all]
        m_new = [jnp.maximum(m_old[pr], jnp.max(s[pr], axis=0, keepdims=True)) for pr in pr_all]
        m_safe = [jnp.where(m_new[pr] == NEG_INF, 0.0, m_new[pr]) for pr in pr_all]
        alpha = [jnp.exp2(m_old[pr] - m_safe[pr]) for pr in pr_all]
        p = [jnp.exp2(s[pr] - m_safe[pr]) for pr in pr_all]
        for pr in pr_all:
            l_scr[pr] = alpha[pr] * l_scr[pr] + jnp.sum(p[pr], axis=0, keepdims=True)
            acc_scr[pr] = alpha[pr] * acc_scr[pr] + _dot(vt16[pr // pairs_per_kv], p[pr].astype(BF16))
            m_scr[pr] = m_new[pr]
        return carry

    lax.fori_loop(0, n_tiles, attend_tile, 0)
    for pr in range(n_pairs):
        o_t = acc_scr[pr] / l_scr[pr]
        for e in range(2):
            cols = slice((2 * pr + e) * C_HEAD, (2 * pr + e + 1) * C_HEAD)
            o = o_t[:, e * nq:(e + 1) * nq].T
            y_ref[:, cols] = (o * _silu(z_ref[:, cols])).astype(y_ref.dtype)


def _dsa_prompt(p3):
    b, l, _ = p3.shape
    assert l % Q_BLOCK == 0
    topk = min(TOPK_MAX, l // 4)
    nq = Q_BLOCK
    kt = DSA_KEY_TILE if l % DSA_KEY_TILE == 0 else 256
    assert l % kt == 0
    n_qb = l // nq
    return pl.pallas_call(
        functools.partial(_dsa_prompt_kernel, l=l, topk=topk, kt=kt),
        grid=(b, n_qb),
        in_specs=[pl.BlockSpec((None, nq, W_C), lambda i, j: (i, j, OFF["qi"] // W_C)),
                  pl.BlockSpec((None, nq, 128), lambda i, j: (i, j, OFF_TAIL // 128)),
                  pl.BlockSpec((None, l, 128), lambda i, j: (i, 0, OFF_TAIL // 128)),
                  pl.BlockSpec((None, nq, W_C), lambda i, j: (i, j, OFF["qc"] // W_C)),
                  pl.BlockSpec((None, l, C_KV), lambda i, j: (i, 0, OFF["kc"] // C_KV)),
                  pl.BlockSpec((None, l, C_KV), lambda i, j: (i, 0, OFF["vc"] // C_KV)),
                  pl.BlockSpec((None, nq, W_C), lambda i, j: (i, j, OFF["zc"] // W_C))],
        out_specs=pl.BlockSpec((None, nq, W_C), lambda i, j: (i, j, 0)),
        out_shape=jax.ShapeDtypeStruct((b, l, W_C), BF16),
        scratch_shapes=[pltpu.VMEM((l, nq), F32),
                        pltpu.VMEM((C_HEADS // 2, 1, 2 * nq), F32), pltpu.VMEM((C_HEADS // 2, 1, 2 * nq), F32),
                        pltpu.VMEM((C_HEADS // 2, C_HEAD, 2 * nq), F32)],
        compiler_params=_cp("dsa_prompt", 2),
        name="dsa_prompt",
    )(p3, p3, p3, p3, p3, p3, p3)


def _idx_scores(logits, wcol, t):
    r = jnp.maximum(logits, 0.0) * wcol
    return jnp.sum(r.reshape(t, IDX_HEADS, logits.shape[1]), axis=1)


def _dsa_s_score_kernel(pt_ref, q_ref, w_ref, kn_ref, *refs, pg, t):
    pages = refs[:pg]
    out_ref, new_ref = refs[pg:]
    q16 = q_ref[...].astype(BF16)
    wcol = w_ref[...] * ((IDX_HEADS ** -0.5) * (IDX_DIM ** -0.5))
    keys16 = jnp.concatenate([pages[i][...] for i in range(pg)], axis=1).astype(BF16)
    out_ref[...] = _idx_scores(_dot(q16, keys16), wcol, t)

    @pl.when(pl.program_id(1) == 0)
    def _():
        new_ref[...] = _idx_scores(_dot_nt(q16, kn_ref[...].astype(BF16)), wcol, t)


def _dsa_s_select_kernel(sc_ref, scn_ref, tq_ref, bias_ref, *, nk, topk):
    rows = sc_ref.shape[0]
    tq = tq_ref[...]
    new_ok = lax.broadcasted_iota(I32, (rows, 128), 1) <= tq
    sc_new = jnp.where(new_ok, scn_ref[...], NEG_INF)
    sc_past = sc_ref[...]

    def count_ge(cand):
        c1 = jnp.sum(jnp.where(sc_past >= cand, 1.0, 0.0), axis=1, keepdims=True)
        c2 = jnp.sum(jnp.where(sc_new >= cand, 1.0, 0.0), axis=1, keepdims=True)
        return c1 + c2

    thr = _topk_threshold(count_ge, (rows, 1), topk)
    keep_all = (nk + 1 + tq[:, 0:1]) <= topk
    n_gt = (jnp.sum(jnp.where(sc_past > thr, 1.0, 0.0), axis=1, keepdims=True)
            + jnp.sum(jnp.where(sc_new > thr, 1.0, 0.0), axis=1, keepdims=True))
    need = float(topk) - n_gt
    tri16 = jnp.where(lax.broadcasted_iota(I32, (128, 128), 0) <= lax.broadcasted_iota(I32, (128, 128), 1),
                      1.0, 0.0).astype(BF16)

    def bias_blocks(scs, seen):
        eqs = [sc == thr for sc in scs]
        eq_fs = [jnp.where(eq, 1.0, 0.0) for eq in eqs]
        prefs = [_dot(eq_f.astype(BF16), tri16) for eq_f in eq_fs]
        out = []
        for sc, eq, eq_f, pref in zip(scs, eqs, eq_fs, prefs):
            tie_ok = (seen + pref - eq_f) < need
            chosen = jnp.where(sc > thr, 0.0, jnp.where(eq, jnp.where(tie_ok, 0.0, NEG_INF), NEG_INF))
            out.append(jnp.where(keep_all, jnp.where(sc > NEG_INF, 0.0, NEG_INF), chosen))
            seen = seen + pref[:, 127:128]
        return out, seen

    n_blk = nk // 128
    grp = 8 if n_blk % 8 == 0 else 1

    def past_group(gi, seen):
        lanes = [pl.ds(pl.multiple_of((gi * grp + i) * 128, 128), 128) for i in range(grp)]
        out, seen = bias_blocks([sc_ref[:, ln] for ln in lanes], seen)
        for ln, bias in zip(lanes, out):
            bias_ref[:, ln] = bias
        return seen

    seen = lax.fori_loop(0, n_blk // grp, past_group, jnp.zeros((rows, 1), F32))
    (bias_new,), _ = bias_blocks([sc_new], seen)
    bias_ref[:, nk:] = bias_new


def _dsa_s_attend_kernel(pt_ref, q_ref, bias_ref, biasn_ref, kn_ref, vn_ref, z_ref, *refs, pg):
    kpages = refs[:pg]
    vpages = refs[pg:2 * pg]
    o_ref = refs[2 * pg]
    m_scr, l_scr, acc_scr = refs[2 * pg + 1:]
    g = pl.program_id(1)
    scale = C_HEAD ** -0.5
    page = kpages[0].shape[0] // C_KV_HEADS

    @pl.when(g == 0)
    def _():
        m_scr[...] = jnp.full(m_scr.shape, NEG_INF, F32)
        l_scr[...] = jnp.zeros(l_scr.shape, F32)
        acc_scr[...] = jnp.zeros(acc_scr.shape, F32)
        for hk in range(C_KV_HEADS):
            cols = slice(hk * C_HEAD, (hk + 1) * C_HEAD)
            q16 = (q_ref[hk] * scale).astype(BF16)
            s = _dot_nt(q16, kn_ref[:, cols].astype(BF16)) + biasn_ref[...]
            _softmax_update(m_scr, l_scr, acc_scr, hk, s, [vn_ref[:, cols].astype(BF16)])

    def two_pages(refs_, i, rows):
        return jnp.concatenate([refs_[i][rows, :], refs_[i + 1][rows, :]], axis=0).astype(BF16)

    for hk in range(C_KV_HEADS):
        q16 = (q_ref[hk] * scale).astype(BF16)
        rows = pl.ds(hk, page, stride=C_KV_HEADS)
        tiles = [_dot_nt(q16, two_pages(kpages, i, rows)) for i in range(0, pg, 2)]
        s = jnp.concatenate(tiles, axis=1) + bias_ref[...]
        _softmax_update(m_scr, l_scr, acc_scr, hk, s, [two_pages(vpages, i, rows) for i in range(0, pg, 2)])

    @pl.when(g == pl.num_programs(1) - 1)
    def _():
        for hk in range(C_KV_HEADS):
            o_ref[hk] = (acc_scr[hk] / l_scr[hk]) * _silu(z_ref[hk])


def _dsa_sample(p3s, cache_k, cache_v, cache_kidx, page_table, layer):
    b, t, _ = p3s.shape
    n_pages = page_table.shape[1]
    page = cache_k.shape[2]
    assert page == 128
    past = n_pages * page
    topk = min(TOPK_MAX, (past + t) // 4)
    pg = 32 if n_pages % 32 == 0 else (16 if n_pages % 16 == 0 else n_pages)
    assert pg % 2 == 0
    ng = n_pages // pg
    rows = t * C_GROUPS

    qi = p3s[:, :, OFF["qi"]:OFF["qi"] + IDX_HEADS * IDX_DIM].reshape(b, t * IDX_HEADS, IDX_DIM)
    wi = p3s[:, :, OFF_TAIL + TAIL_WI:OFF_TAIL + TAIL_WI + IDX_HEADS].reshape(b, t * IDX_HEADS, 1)
    ki_new = jnp.pad(p3s[:, :, OFF_TAIL:OFF_TAIL + IDX_DIM], ((0, 0), (0, 128 - t), (0, 0)))
    k_new = jnp.pad(p3s[:, :, OFF["kc"]:OFF["kc"] + C_KV], ((0, 0), (0, 128 - t), (0, 0)))
    v_new = jnp.pad(p3s[:, :, OFF["vc"]:OFF["vc"] + C_KV], ((0, 0), (0, 128 - t), (0, 0)))

    def heads_major(a):
        a = a.reshape(b, t, C_KV_HEADS, C_GROUPS, C_HEAD)
        return jnp.transpose(a, (0, 2, 1, 3, 4)).reshape(b, C_KV_HEADS, rows, C_HEAD)

    qh = heads_major(p3s[:, :, OFF["qc"]:OFF["qc"] + W_C])
    zh = heads_major(p3s[:, :, OFF["zc"]:OFF["zc"] + W_C])

    kidx_t = jnp.swapaxes(cache_kidx, 2, 3)
    score_spec = pltpu.PrefetchScalarGridSpec(
        num_scalar_prefetch=1,
        grid=(b, ng),
        in_specs=[pl.BlockSpec((None, t * IDX_HEADS, IDX_DIM), lambda i, g, pt: (i, 0, 0)),
                  pl.BlockSpec((None, t * IDX_HEADS, 1), lambda i, g, pt: (i, 0, 0)),
                  pl.BlockSpec((None, 128, IDX_DIM), lambda i, g, pt: (i, 0, 0))]
                 + [pl.BlockSpec((None, None, IDX_DIM, page),
                                 lambda i, g, pt, j=j: (layer, pt[i, g * pg + j], 0, 0)) for j in range(pg)],
        out_specs=[pl.BlockSpec((None, t, pg * page), lambda i, g, pt: (i, 0, g)),
                   pl.BlockSpec((None, t, 128), lambda i, g, pt: (i, 0, 0))],
    )
    scores, scores_new = pl.pallas_call(
        functools.partial(_dsa_s_score_kernel, pg=pg, t=t),
        grid_spec=score_spec,
        out_shape=[jax.ShapeDtypeStruct((b, t, past), F32), jax.ShapeDtypeStruct((b, t, 128), F32)],
        compiler_params=_cp("dsa_s_score", 2, fuse_inputs=4 + pg),
        name="dsa_s_score",
    )(page_table, qi, wi, ki_new, *([kidx_t] * pg))

    tq = jnp.broadcast_to(jnp.tile(jnp.arange(t, dtype=I32), b)[:, None], (b * t, 128))
    bias = pl.pallas_call(
        functools.partial(_dsa_s_select_kernel, nk=past, topk=topk),
        grid=(1,),
        in_specs=[pl.BlockSpec((b * t, past), lambda i: (0, 0)),
                  pl.BlockSpec((b * t, 128), lambda i: (0, 0)),
                  pl.BlockSpec((b * t, 128), lambda i: (0, 0))],
        out_specs=pl.BlockSpec((b * t, past + 128), lambda i: (0, 0)),
        out_shape=jax.ShapeDtypeStruct((b * t, past + 128), F32),
        compiler_params=_cp("dsa_s_select", 1, fuse_inputs=3),
        name="dsa_s_select",
    )(scores.reshape(b * t, past), scores_new.reshape(b * t, 128), tq).reshape(b, t, past + 128)

    bias_rows = jnp.repeat(bias, C_GROUPS, axis=1)
    bias_past = bias_rows[:, :, :past]
    bias_new = bias_rows[:, :, past:]

    ck = cache_k.reshape(cache_k.shape[0], cache_k.shape[1], page * C_KV_HEADS, C_HEAD)
    cv = cache_v.reshape(cache_v.shape[0], cache_v.shape[1], page * C_KV_HEADS, C_HEAD)
    pspec = [pl.BlockSpec((None, None, page * C_KV_HEADS, C_HEAD),
                          lambda i, g, pt, j=j: (layer, pt[i, g * pg + j], 0, 0)) for j in range(pg)]
    attend_spec = pltpu.PrefetchScalarGridSpec(
        num_scalar_prefetch=1,
        grid=(b, ng),
        in_specs=[pl.BlockSpec((None, C_KV_HEADS, rows, C_HEAD), lambda i, g, pt: (i, 0, 0, 0)),
                  pl.BlockSpec((None, rows, pg * page), lambda i, g, pt: (i, 0, g)),
                  pl.BlockSpec((None, rows, 128), lambda i, g, pt: (i, 0, 0)),
                  pl.BlockSpec((None, 128, C_KV), lambda i, g, pt: (i, 0, 0)),
                  pl.BlockSpec((None, 128, C_KV), lambda i, g, pt: (i, 0, 0)),
                  pl.BlockSpec((None, C_KV_HEADS, rows, C_HEAD), lambda i, g, pt: (i, 0, 0, 0))]
                 + pspec + pspec,
        out_specs=pl.BlockSpec((None, C_KV_HEADS, rows, C_HEAD), lambda i, g, pt: (i, 0, 0, 0)),
        scratch_shapes=[pltpu.VMEM((C_KV_HEADS, rows, 1), F32), pltpu.VMEM((C_KV_HEADS, rows, 1), F32),
                        pltpu.VMEM((C_KV_HEADS, rows, C_HEAD), F32)],
    )
    oh = pl.pallas_call(
        functools.partial(_dsa_s_attend_kernel, pg=pg),
        grid_spec=attend_spec,
        out_shape=jax.ShapeDtypeStruct((b, C_KV_HEADS, rows, C_HEAD), F32),
        compiler_params=_cp("dsa_s_attend", 2, fuse_inputs=7 + 2 * pg),
        name="dsa_s_attend",
    )(page_table, qh, bias_past, bias_new, k_new, v_new, zh, *([ck] * pg), *([cv] * pg))
    y = jnp.transpose(oh.reshape(b, C_KV_HEADS, t, C_GROUPS, C_HEAD), (0, 2, 1, 3, 4)).reshape(b, t, W_C)
    return y.astype(BF16)


def _outproj_kernel(ya_ref, yb_ref, yc_ref, w_ref, x_ref, gate_ref, o_ref, mix_scr):
    @pl.when(pl.program_id(1) == 0)
    def _():
        mix_scr[:, 0:W_A] = ya_ref[...]
        mix_scr[:, W_A:W_A + W_B] = yb_ref[...]
        mix_scr[:, W_A + W_B:] = yc_ref[...]

    o_ref[...] = x_ref[...] + gate_ref[...] * _dot(mix_scr[...], w_ref[...])


def _outproj_norm_kernel(ya_ref, yb_ref, yc_ref, w_ref, x_ref, gate_ref, g_ref, o_ref, mix_scr, ssq_scr):
    j = pl.program_id(1)
    tn = x_ref.shape[1]

    @pl.when(j == 0)
    def _():
        mix_scr[:, 0:W_A] = ya_ref[...]
        mix_scr[:, W_A:W_A + W_B] = yb_ref[...]
        mix_scr[:, W_A + W_B:] = yc_ref[...]
        ssq_scr[...] = jnp.zeros(ssq_scr.shape, F32)

    x_new = x_ref[...] + gate_ref[...] * _dot(mix_scr[...], w_ref[...])
    o_ref[:, pl.ds(pl.multiple_of(j * tn, tn), tn)] = x_new
    ssq_scr[...] += jnp.sum(x_new * x_new, axis=-1, keepdims=True)

    @pl.when(j == pl.num_programs(1) - 1)
    def _():
        tm, d = o_ref.shape
        rc = 64 if tm % 64 == 0 else tm

        def body(c, carry):
            r = pl.ds(pl.multiple_of(c * rc, rc), rc)
            o_ref[r, :] = o_ref[r, :] * lax.rsqrt(ssq_scr[r, :] / d + EPS) * g_ref[...]
            return carry

        lax.fori_loop(0, tm // rc, body, 0)


def _outproj(ya, yb, yc, w_bf16, layer, x2, gate, *, rows_per_batch, mod_row0, g_final=None):
    m, d = x2.shape
    tn = 1024
    if gate.ndim == 3:
        tm = 512
        gate_spec = pl.BlockSpec((None, 1, tn),
                                 lambda i, j: ((i * tm) // rows_per_batch + mod_row0, 0, 2 * (d // tn) + j))
    else:
        tm = m
        gate_spec = pl.BlockSpec((tm, tn), lambda i, j: (i, j))
    assert m % tm == 0
    in_specs = [pl.BlockSpec((tm, W_A), lambda i, j: (i, 0)),
                pl.BlockSpec((tm, W_B), lambda i, j: (i, 0)),
                pl.BlockSpec((tm, W_C), lambda i, j: (i, 0)),
                pl.BlockSpec((None, d, tn), lambda i, j: (layer, 0, j)),
                pl.BlockSpec((tm, tn), lambda i, j: (i, j)),
                gate_spec]
    operands = [ya, yb, yc, w_bf16, x2, gate]
    scratch = [pltpu.VMEM((tm, d), BF16)]
    if g_final is None:
        body, name = _outproj_kernel, "outproj"
        out_spec = pl.BlockSpec((tm, tn), lambda i, j: (i, j))
    else:
        body, name = _outproj_norm_kernel, "outproj_norm"
        in_specs.append(pl.BlockSpec((1, d), lambda i, j: (0, 0)))
        operands.append(g_final.reshape(1, d))
        scratch.append(pltpu.VMEM((tm, 1), F32))
        out_spec = pl.BlockSpec((tm, d), lambda i, j: (i, 0))
    return pl.pallas_call(
        body,
        grid=(m // tm, d // tn),
        in_specs=in_specs,
        out_specs=out_spec,
        out_shape=jax.ShapeDtypeStruct((m, d), F32),
        scratch_shapes=scratch,
        compiler_params=_cp(name, 2, fuse_inputs=len(operands)),
        name=name,
    )(*operands)


def _kv_rows_kernel(*refs, depth, tm):
    k_refs, v_refs = refs[:depth], refs[depth:2 * depth]
    pk_ref, pv_ref = refs[2 * depth:]
    for lyr in range(depth):
        for h in range(C_KV_HEADS):
            cols = slice(h * C_HEAD, (h + 1) * C_HEAD)
            rows = pl.ds(h, tm, stride=C_KV_HEADS)
            pk_ref.at[lyr][rows, :] = k_refs[lyr][:, cols]
            pv_ref.at[lyr][rows, :] = v_refs[lyr][:, cols]


def _kv_rows(p3_layers):
    depth = len(p3_layers)
    b, l, _ = p3_layers[0].shape
    tm = 512 if l % 512 == 0 else l
    kspec = pl.BlockSpec((None, tm, C_KV), lambda i, j: (i, j, OFF["kc"] // C_KV))
    vspec = pl.BlockSpec((None, tm, C_KV), lambda i, j: (i, j, OFF["vc"] // C_KV))
    ospec = pl.BlockSpec((depth, None, tm * C_KV_HEADS, C_HEAD), lambda i, j: (0, i, j, 0))
    oshape = jax.ShapeDtypeStruct((depth, b, l * C_KV_HEADS, C_HEAD), F32)
    pk, pv = pl.pallas_call(
        functools.partial(_kv_rows_kernel, depth=depth, tm=tm),
        grid=(b, l // tm),
        in_specs=[kspec] * depth + [vspec] * depth,
        out_specs=[ospec, ospec],
        out_shape=[oshape, oshape],
        compiler_params=_cp("kv_rows", 2),
        name="kv_rows",
    )(*p3_layers, *p3_layers)
    shape = (depth, b, l, C_KV_HEADS, C_HEAD)
    return pk.reshape(shape), pv.reshape(shape)


def _w_in_tile_table():
    n_t = NP // 128
    table, n_plain = [], None
    for dt in range(n_t - 1):
        c = dt * 128
        seg = [n for n in _DST_ORDER if OFF[n] <= c < OFF[n] + _SRC_OFF[n][1]][0]
        src = _SRC_OFF[seg][0] + (c - OFF[seg])
        if src % 128 == 0:
            assert n_plain is None
        else:
            assert src % 128 == 32
            if n_plain is None:
                n_plain = dt
        table.append(src // 128)
    t_kw, t_ab = _SRC_OFF["ki"][0] // 128, _SRC_OFF["a"][0] // 128
    assert _SRC_OFF["ki"][0] % 128 == 32 and _SRC_OFF["wi"][0] == t_kw * 128 + TAIL_WI
    assert _SRC_OFF["a"][0] % 128 == 0 and _SRC_OFF["b"][0] == _SRC_OFF["a"][0] + DN_HEADS
    assert TAIL_A == IDX_DIM and TAIL_B == TAIL_A + DN_HEADS and TAIL_WI == TAIL_B + DN_HEADS
    tile_a = table + [t_kw]
    rows_b = [t_ab * 4] * n_plain + [(s + 1) * 4 for s in table[n_plain:]] + [t_ab * 4]
    return tile_a, rows_b, n_plain


def _wprep_kernel(ta_ref, tb_ref, a_ref, b_ref, xs_ref, g_ref, sc_ref, sh_ref, o_ref, ps_ref, h_scr,
                  *, n_plain, n_t):
    dt = pl.program_id(0)
    dt_o = o_ref.dtype

    @pl.when(dt == 0)
    def _():
        x = xs_ref[...]
        y = x * lax.rsqrt(jnp.mean(x * x, axis=-1, keepdims=True) + EPS)
        h_scr[...] = ((y * g_ref[...]) * (1.0 + sc_ref[...]) + sh_ref[...]).astype(BF16)

    @pl.when(dt < n_plain)
    def _():
        o_ref[...] = a_ref[...].astype(dt_o)

    @pl.when((dt >= n_plain) & (dt < n_t - 1))
    def _():
        o_ref[0:96, :] = a_ref[32:128, :].astype(dt_o)
        o_ref[96:128, :] = b_ref[...].astype(dt_o)

    @pl.when(dt == n_t - 1)
    def _():
        o_ref[0:TAIL_A, :] = a_ref[32:32 + IDX_DIM, :].astype(dt_o)
        o_ref[TAIL_A:TAIL_WI, :] = b_ref[...].astype(dt_o)
        o_ref[TAIL_WI:TAIL_WI + IDX_HEADS, :] = a_ref[TAIL_WI:TAIL_WI + IDX_HEADS, :].astype(dt_o)
        o_ref[TAIL_WI + IDX_HEADS:, :] = jnp.zeros((128 - TAIL_WI - IDX_HEADS, o_ref.shape[1]), dt_o)

    ps_ref[...] = _dot_nt(h_scr[...], o_ref[...])


def _permute_w_in(w_in, layer, xs, g, sc, sh):
    _, d, _ = w_in.shape
    ms = xs.shape[0]
    w_t = jnp.swapaxes(w_in, 1, 2)
    tile_a, rows_b, n_plain = _w_in_tile_table()
    n_t = NP // 128
    const = lambda t, ta, tb: (0, 0)
    grid_spec = pltpu.PrefetchScalarGridSpec(
        num_scalar_prefetch=2,
        grid=(n_t,),
        in_specs=[pl.BlockSpec((None, 128, d), lambda t, ta, tb: (layer, ta[t], 0)),
                  pl.BlockSpec((None, 32, d), lambda t, ta, tb: (layer, tb[t], 0)),
                  pl.BlockSpec((ms, d), const), pl.BlockSpec((1, d), const),
                  pl.BlockSpec((ms, d), const), pl.BlockSpec((ms, d), const)],
        out_specs=[pl.BlockSpec((128, d), lambda t, ta, tb: (t, 0)),
                   pl.BlockSpec((ms, 128), lambda t, ta, tb: (0, t))],
        scratch_shapes=[pltpu.VMEM((ms, d), BF16)],
    )
    return pl.pallas_call(
        functools.partial(_wprep_kernel, n_plain=n_plain, n_t=n_t),
        grid_spec=grid_spec,
        out_shape=[jax.ShapeDtypeStruct((NP, d), BF16), jax.ShapeDtypeStruct((ms, NP), F32)],
        compiler_params=_cp("w_in_layout", 1),
        name="w_in_layout",
    )(jnp.asarray(tile_a, I32), jnp.asarray(rows_b, I32), w_t, w_t, xs, g, sc, sh)


def _delta_t(l):
    for t in (128, 64):
        if l % t == 0:
            return t
    raise ValueError("sequence length must be a multiple of 64")


def kernel(x_prompt, x_sample, cache_k, cache_v, cache_kidx, state_dn, state_conv, page_table,
           c_prompt, c_sample, w_ada, b_ada, g_norm, w_in, a_vnorm, a_ws, a_bs, dn_conv_w,
           dn_a_log, dn_dt_bias, dn_onorm, w_out, g_final):
    bp, lp, d = x_prompt.shape
    bs, ls, _ = x_sample.shape
    depth = w_ada.shape[0]
    assert d == D_MODEL and w_in.shape[2] == D_IN
    assert CONV_K - 1 <= ls <= DN_CHUNK

    n_c = bp + bs
    c_rows = jnp.concatenate([c_prompt, c_sample], axis=0)
    r_pad = (-n_c) % 8
    if r_pad:
        c_rows = jnp.pad(c_rows, ((0, r_pad), (0, 0)))
    m_all = _ada(c_rows, w_ada, b_ada)

    xp = x_prompt.reshape(bp * lp, d)
    xs = x_sample.reshape(bs * ls, d)
    ls_pad = 8 * ((ls + 7) // 8)
    outs = {k: [] for k in ("p3", "pki", "pdn", "pconv", "sk", "sv", "ski", "sdn", "sconv", "samlp")}
    zeros_conv = jnp.zeros((1, bp, CONV_K - 1, DN_CONV_DIM), F32)
    zeros_state = jnp.zeros((1, bp, DN_HEADS, DN_HEAD, DN_HEAD), F32)

    w_out_all = w_out.astype(BF16)

    def g_last(layer):
        return g_final if layer == depth - 1 else None

    for l in range(depth):
        g_l = g_norm[l].reshape(1, d)
        m_l = m_all[l]
        m3 = m_l.reshape(m_l.shape[0], 1, 3 * d)
        ms = jnp.repeat(m_l[bp:bp + bs], ls, axis=0)

        w_in_l, ps = _permute_w_in(w_in, l, xs, g_l, ms[:, d:2 * d], ms[:, 0:d])

        pp = _inproj(xp, g_l, m3, w_in_l, rows_per_batch=lp)
        p3 = pp.reshape(bp, lp, NP)
        (ya,) = _mixa(p3, a_vnorm[l], a_ws[l], a_bs[l], emit_va=False)
        abt = jnp.transpose(p3[:, :, OFF["a"]:OFF["a"] + 2 * DN_HEADS], (0, 2, 1))
        yb, s_p = _delta(p3, abt, dn_conv_w[l], zeros_conv, zeros_state, 0, dn_a_log[l], dn_dt_bias[l],
                         dn_onorm[l], t=_delta_t(lp), valid_len=_delta_t(lp))
        yc = _dsa_prompt(p3)
        xp = _outproj(ya.reshape(bp * lp, W_A), yb.reshape(bp * lp, W_B), yc.reshape(bp * lp, W_C),
                      w_out_all, l, xp, m3, rows_per_batch=lp, mod_row0=0, g_final=g_last(l))
        outs["p3"].append(p3)
        outs["pki"].append(p3[:, :, OFF_TAIL:OFF_TAIL + IDX_DIM])
        outs["pdn"].append(s_p)
        outs["pconv"].append(p3[:, lp - (CONV_K - 1):, OFF["qkv"]:OFF["qkv"] + DN_CONV_DIM])

        p3s = ps.reshape(bs, ls, NP)
        ya_s, va_s = _mixa(p3s, a_vnorm[l], a_ws[l], a_bs[l], emit_va=True)
        p3s_pad = jnp.pad(p3s, ((0, 0), (0, ls_pad - ls), (0, 0)))
        abt_s = jnp.transpose(p3s_pad[:, :, OFF["a"]:OFF["a"] + 2 * DN_HEADS], (0, 2, 1))
        yb_s, s_s = _delta(p3s_pad, abt_s, dn_conv_w[l], state_conv, state_dn, l, dn_a_log[l],
                           dn_dt_bias[l], dn_onorm[l], t=ls_pad, valid_len=ls)
        yc_s = _dsa_sample(p3s, cache_k, cache_v, cache_kidx, page_table, l)
        xs = _outproj(ya_s.reshape(bs * ls, W_A), yb_s[:, :ls].reshape(bs * ls, W_B),
                      yc_s.reshape(bs * ls, W_C), w_out_all, l, xs, ms[:, 2 * d:3 * d],
                      rows_per_batch=ls, mod_row0=bp, g_final=g_last(l))
        outs["sk"].append(p3s[:, :, OFF["kc"]:OFF["kc"] + C_KV].reshape(bs, ls, C_KV_HEADS, C_HEAD))
        outs["sv"].append(p3s[:, :, OFF["vc"]:OFF["vc"] + C_KV].reshape(bs, ls, C_KV_HEADS, C_HEAD))
        outs["ski"].append(p3s[:, :, OFF_TAIL:OFF_TAIL + IDX_DIM])
        outs["sdn"].append(s_s)
        outs["sconv"].append(p3s[:, ls - (CONV_K - 1):, OFF["qkv"]:OFF["qkv"] + DN_CONV_DIM])
        outs["samlp"].append(va_s)

    y_prompt = xp.reshape(bp, lp, d)
    y_sample = xs.reshape(bs, ls, d)
    st = jnp.stack
    p_k, p_v = _kv_rows(outs["p3"])
    return (y_prompt, y_sample, p_k, p_v, st(outs["pki"]), st(outs["pdn"]),
            st(outs["pconv"]), st(outs["sk"]), st(outs["sv"]), st(outs["ski"]), st(outs["sdn"]),
            st(outs["sconv"]), st(outs["samlp"]))
```
